```python
import math
import jax, jax.numpy as jnp
from jax import lax
import numpy as np

D_MODEL = 1024
BATCH = 32
SEQ = 2048
DEPTH = 1

CTX_LEN = 256
GRID_W = 64

N_ATTN_HEADS = 16
QK_NOPE_DIM = 64
QK_ROPE_DIM = 32
V_HEAD_DIM = 64
Q_LORA_RANK = 384
KV_LORA_RANK = 256
ROPE_THETA = 10000.0
Q_BLOCK = 128
ATTN_SCALE = (QK_NOPE_DIM + QK_ROPE_DIM) ** -0.5

N_SSD_HEADS = 16
SSD_HEAD_DIM = 64
SSD_GROUPS = 2
HEADS_PER_GROUP = N_SSD_HEADS // SSD_GROUPS
SSD_STATE = 128
SSD_CONV = 5
SSD_CHUNK = 128
D_INNER = N_SSD_HEADS * SSD_HEAD_DIM
GN = SSD_GROUPS * SSD_STATE
XBC_WIDTH = D_INNER + 2 * GN

ATTN_WIDTH = N_ATTN_HEADS * V_HEAD_DIM
MIX_WIDTH = ATTN_WIDTH + D_INNER

D_FF = 2816
FFN_CONV = 3

N_MOD = 6
EPS = 1e-6

IN_SPLITS = (Q_LORA_RANK, KV_LORA_RANK, QK_ROPE_DIM, D_INNER, XBC_WIDTH, 2 * N_SSD_HEADS)
IN_WIDTH = sum(IN_SPLITS)
IN_OFFSETS = tuple(int(o) for o in np.cumsum(IN_SPLITS)[:-1])

kernel_name = "hybrid_mla_ssd_diffusion_layer"


def rms_norm(x, w):
    xf = x.astype(jnp.float32)
    y = xf * lax.rsqrt(jnp.mean(xf * xf, axis=-1, keepdims=True) + EPS)
    return (y * w.astype(jnp.float32)).astype(x.dtype)


def modulate(h, shift, scale):
    return h * (1 + scale) + shift


def in_split(h):
    return jnp.split(h, IN_OFFSETS, axis=-1)


def depthwise_conv(x, w, b):
    k = w.shape[0]
    out = lax.conv_general_dilated(
        x, w[:, None, :].astype(x.dtype), window_strides=(1,), padding=((k // 2, k // 2),),
        dimension_numbers=("NWC", "WIO", "NWC"), feature_group_count=x.shape[-1])
    return out + b


def axial_rope_tables(seq_len):
    n_rows = seq_len // GRID_W
    row = jnp.repeat(jnp.arange(n_rows), GRID_W).astype(jnp.float32)
    col = jnp.tile(jnp.arange(GRID_W), n_rows).astype(jnp.float32)
    axis_dim = QK_ROPE_DIM // 2
    inv_freq = ROPE_THETA ** (-jnp.arange(0, axis_dim, 2, dtype=jnp.float32) / axis_dim)
    ang_r = row[:, None] * inv_freq
    ang_c = col[:, None] * inv_freq
    ang = jnp.concatenate([ang_r, ang_r, ang_c, ang_c], axis=-1)
    return jnp.cos(ang), jnp.sin(ang)


def rotate_half_axial(x):
    def rh(t):
        a, b = jnp.split(t, 2, axis=-1)
        return jnp.concatenate([-b, a], axis=-1)
    xr, xc = jnp.split(x, 2, axis=-1)
    return jnp.concatenate([rh(xr), rh(xc)], axis=-1)


def apply_rope(x, cos, sin):
    xf = x.astype(jnp.float32)
    return (xf * cos + rotate_half_axial(xf) * sin).astype(x.dtype)


def mla_queries(cq, q_norm_w, w_q_up):
    b, l, _ = cq.shape
    q = (rms_norm(cq, q_norm_w) @ w_q_up).reshape(b, l, N_ATTN_HEADS, QK_NOPE_DIM + QK_ROPE_DIM)
    return q[..., :QK_NOPE_DIM], q[..., QK_NOPE_DIM:]


def mla_keys_values(ckv, kv_norm_w, w_kv_up):
    b, l, _ = ckv.shape
    kv = (rms_norm(ckv, kv_norm_w) @ w_kv_up).reshape(b, l, N_ATTN_HEADS, QK_NOPE_DIM + V_HEAD_DIM)
    return kv[..., :QK_NOPE_DIM], kv[..., QK_NOPE_DIM:]


def attend(q_nope, q_rope, k_nope, k_rope, v):
    s = (jnp.einsum("bqhd,bkhd->bhqk", q_nope, k_nope)
         + jnp.einsum("bqhr,bkr->bhqk", q_rope, k_rope)) * ATTN_SCALE
    p = jax.nn.softmax(s.astype(jnp.float32), axis=-1).astype(v.dtype)
    return jnp.einsum("bhqk,bkhd->bqhd", p, v)


def latent_attention(q_nope, q_rope, k_nope, k_rope, v):
    b, s, h, _ = q_nope.shape
    nb = s // Q_BLOCK

    def to_blocks(t):
        return jnp.moveaxis(t.reshape(b, nb, Q_BLOCK, *t.shape[2:]), 1, 0)

    out = lax.map(lambda qb: attend(qb[0], qb[1], k_nope, k_rope, v),
                  (to_blocks(q_nope), to_blocks(q_rope)))
    return jnp.moveaxis(out, 0, 1).reshape(b, s, h * V_HEAD_DIM)


def ssd_prepare(xbc, dt_raw, conv_w, conv_b, dt_bias):
    b, l, _ = xbc.shape
    xbc = jax.nn.silu(depthwise_conv(xbc, conv_w, conv_b))
    xs, bm, cm = jnp.split(xbc, [D_INNER, D_INNER + GN], axis=-1)
    xs = xs.reshape(b, l, SSD_GROUPS, HEADS_PER_GROUP, SSD_HEAD_DIM)
    bm = bm.reshape(b, l, SSD_GROUPS, SSD_STATE)
    cm = cm.reshape(b, l, SSD_GROUPS, SSD_STATE)
    dt = jax.nn.softplus(dt_raw.astype(jnp.float32).reshape(b, l, 2, SSD_GROUPS, HEADS_PER_GROUP)
                         + dt_bias.astype(jnp.float32).reshape(2, SSD_GROUPS, HEADS_PER_GROUP))
    return xs, bm, cm, dt


def segment_decay(cum):
    n = cum.shape[-1]
    diff = cum[..., :, None] - cum[..., None, :]
    mask = jnp.tril(jnp.ones((n, n), dtype=bool))
    return jnp.exp(jnp.where(mask, diff, -jnp.inf))


def ssd_states(xh, dt, A, bm, cm, init_state):
    b, l, g, e, p = xh.shape
    nc = l // SSD_CHUNK
    xd = (xh * dt[..., None]).reshape(b, nc, SSD_CHUNK, g, e, p)
    a_cum = jnp.cumsum(jnp.moveaxis((dt * A).reshape(b, nc, SSD_CHUNK, g, e), 2, -1), axis=-1)
    bc = bm.reshape(b, nc, SSD_CHUNK, g, -1)
    cc = cm.reshape(b, nc, SSD_CHUNK, g, -1)
    decay_to_end = jnp.exp(a_cum[..., -1:] - a_cum)
    chunk_states = jnp.einsum("bclgn,bcgel,bclgep->bcgepn", bc, decay_to_end, xd)
    states = jnp.concatenate([init_state[:, None], chunk_states], axis=1)
    chunk_cum = jnp.cumsum(jnp.pad(a_cum[..., -1], ((0, 0), (1, 0), (0, 0), (0, 0))), axis=1)
    decay_chunks = segment_decay(jnp.moveaxis(chunk_cum, 1, -1))
    states = jnp.einsum("bgezc,bcgepn->bzgepn", decay_chunks, states)
    return (xd, a_cum, bc, cc), states[:, :-1], states[:, -1]


def ssd_output(pieces, entering_states):
    xd, a_cum, bc, cc = pieces
    within = segment_decay(a_cum)
    cb = jnp.einsum("bclgn,bcsgn->bcgls", cc, bc)
    y_diag = jnp.einsum("bcgls,bcgels,bcsgep->bclgep", cb, within, xd)
    y_off = jnp.einsum("bclgn,bcgepn,bcgel->bclgep", cc, entering_states, jnp.exp(a_cum))
    b, nc, q, g, e, p = y_diag.shape
    return (y_diag + y_off).reshape(b, nc * q, g, e, p)


def ssd_finish(y, xs, z, d_skip, norm_w):
    b, l = y.shape[:2]
    y = y + d_skip.reshape(SSD_GROUPS, HEADS_PER_GROUP, 1) * xs
    y = y.reshape(b, l, D_INNER).astype(z.dtype)
    return rms_norm(y * jax.nn.silu(z), norm_w)


def conv_glu(h, w_up, conv_w, conv_b, w_down):
    gate, val = jnp.split(h @ w_up, 2, axis=-1)
    gate = depthwise_conv(gate, conv_w, conv_b)
    return (jax.nn.gelu(gate, approximate=False) * val) @ w_down


def _fwd_setup_inputs(seed: int = 0) -> dict:
    key = jax.random.key(seed)
    ks = jax.random.split(key, 32)
    f32 = jnp.float32
    L = DEPTH

    def dense(k, shape, fan_in):
        return jax.random.normal(k, shape, f32) * fan_in ** -0.5

    def gain(k, shape):
        return 1.0 + 0.1 * jax.random.normal(k, shape, f32)

    def bias(k, shape):
        return 0.02 * jax.random.normal(k, shape, f32)

    dt0 = jnp.exp(jax.random.uniform(ks[15], (L, 2, N_SSD_HEADS), f32, math.log(1e-3), math.log(1e-1)))
    return {
        "x": jax.random.normal(ks[0], (BATCH, SEQ, D_MODEL), f32),
        "c": jax.random.normal(ks[1], (BATCH, D_MODEL), f32),
        "ctx": jax.random.normal(ks[2], (BATCH, CTX_LEN, D_MODEL), f32),
        "c_ctx": jax.random.normal(ks[3], (D_MODEL,), f32),
        "w_mod": dense(ks[4], (L, D_MODEL, N_MOD * D_MODEL), D_MODEL),
        "b_mod": bias(ks[5], (L, N_MOD * D_MODEL)),
        "mix_pre_norm": gain(ks[6], (L, D_MODEL)),
        "mix_post_norm": gain(ks[7], (L, D_MODEL)),
        "w_in": dense(ks[8], (L, D_MODEL, IN_WIDTH), D_MODEL),
        "q_norm": gain(ks[9], (L, Q_LORA_RANK)),
        "w_q_up": dense(ks[10], (L, Q_LORA_RANK, N_ATTN_HEADS * (QK_NOPE_DIM + QK_ROPE_DIM)), Q_LORA_RANK),
        "kv_norm": gain(ks[11], (L, KV_LORA_RANK)),
        "w_kv_up": dense(ks[12], (L, KV_LORA_RANK, N_ATTN_HEADS * (QK_NOPE_DIM + V_HEAD_DIM)), KV_LORA_RANK),
        "ssd_conv_w": dense(ks[13], (L, SSD_CONV, XBC_WIDTH), SSD_CONV),
        "ssd_conv_b": bias(ks[14], (L, XBC_WIDTH)),
        "ssd_a_log": jnp.log(jax.random.uniform(ks[16], (L, 2, N_SSD_HEADS), f32, 1.0, 16.0)),
        "ssd_dt_bias": dt0 + jnp.log(-jnp.expm1(-dt0)),
        "ssd_d": gain(ks[17], (L, N_SSD_HEADS)),
        "ssd_norm": gain(ks[18], (L, D_INNER)),
        "w_out": dense(ks[19], (L, MIX_WIDTH, D_MODEL), MIX_WIDTH),
        "ffn_pre_norm": gain(ks[20], (L, D_MODEL)),
        "ffn_post_norm": gain(ks[21], (L, D_MODEL)),
        "w_up": dense(ks[22], (L, D_MODEL, 2 * D_FF), D_MODEL),
        "ffn_conv_w": dense(ks[23], (L, FFN_CONV, D_FF), FFN_CONV),
        "ffn_conv_b": bias(ks[24], (L, D_FF)),
        "w_down": dense(ks[25], (L, D_FF, D_MODEL), D_FF),
    }


def _fwd_reference(x, c, ctx, c_ctx, w_mod, b_mod, mix_pre_norm, mix_post_norm, w_in, q_norm, w_q_up,
              kv_norm, w_kv_up, ssd_conv_w, ssd_conv_b, ssd_a_log, ssd_dt_bias, ssd_d, ssd_norm, w_out,
              ffn_pre_norm, ffn_post_norm, w_up, ffn_conv_w, ffn_conv_b, w_down):
    bsz, seq, _ = x.shape
    cos, sin = axial_rope_tables(seq)
    cos_q, sin_q = cos[:, None, :], sin[:, None, :]

    def flip(t):
        return jnp.flip(t, axis=1)

    for l in range(DEPTH):
        last = l == DEPTH - 1
        mod_x = jnp.split((jax.nn.silu(c) @ w_mod[l] + b_mod[l])[:, None, :], N_MOD, axis=-1)
        mod_c = jnp.split((jax.nn.silu(c_ctx) @ w_mod[l] + b_mod[l])[None, None, :], N_MOD, axis=-1)

        cq_x, ckv_x, kr_x, z_x, xbc_x, dt_x = in_split(
            modulate(rms_norm(x, mix_pre_norm[l]), mod_x[0], mod_x[1]) @ w_in[l])
        cq_c, ckv_c, kr_c, z_c, xbc_c, dt_c = in_split(
            modulate(rms_norm(ctx, mix_pre_norm[l]), mod_c[0], mod_c[1]) @ w_in[l])

        q_nope_x, q_rope_x = mla_queries(cq_x, q_norm[l], w_q_up[l])
        q_rope_x = apply_rope(q_rope_x, cos_q, sin_q)
        k_nope_x, v_x = mla_keys_values(ckv_x, kv_norm[l], w_kv_up[l])
        k_rope_x = apply_rope(kr_x, cos, sin)
        k_nope_c, v_c = mla_keys_values(ckv_c, kv_norm[l], w_kv_up[l])
        attn_x = latent_attention(q_nope_x, q_rope_x,
                                  jnp.concatenate([k_nope_c, k_nope_x], axis=1),
                                  jnp.concatenate([kr_c, k_rope_x], axis=1),
                                  jnp.concatenate([v_c, v_x], axis=1))

        A = -jnp.exp(ssd_a_log[l].astype(jnp.float32)).reshape(2, SSD_GROUPS, HEADS_PER_GROUP)
        xs_x, b_x, c_x, dtv_x = ssd_prepare(xbc_x, dt_x, ssd_conv_w[l], ssd_conv_b[l], ssd_dt_bias[l])
        xs_c, b_c, c_c, dtv_c = ssd_prepare(xbc_c, dt_c, ssd_conv_w[l], ssd_conv_b[l], ssd_dt_bias[l])
        zero_state = jnp.zeros((bsz, SSD_GROUPS, HEADS_PER_GROUP, SSD_HEAD_DIM, SSD_STATE), xs_x.dtype)
        st_cf = ssd_states(xs_c, dtv_c[:, :, 0], A[0], b_c, c_c, zero_state)
        st_cb = ssd_states(flip(xs_c), flip(dtv_c[:, :, 1]), A[1], flip(b_c), flip(c_c), zero_state)
        st_xf = ssd_states(xs_x, dtv_x[:, :, 0], A[0], b_x, c_x, st_cf[2])
        st_xb = ssd_states(flip(xs_x), flip(dtv_x[:, :, 1]), A[1], flip(b_x), flip(c_x), st_cb[2])
        y_x = ssd_output(st_xf[0], st_xf[1]) + flip(ssd_output(st_xb[0], st_xb[1]))
        ssd_x = ssd_finish(y_x, xs_x, z_x, ssd_d[l], ssd_norm[l])

        mix_x = jnp.concatenate([attn_x, ssd_x], axis=-1) @ w_out[l]
        x = x + mod_x[2] * rms_norm(mix_x, mix_post_norm[l])

        if not last:
            q_nope_c, q_rope_c = mla_queries(cq_c, q_norm[l], w_q_up[l])
            attn_c = attend(q_nope_c, q_rope_c, k_nope_c, kr_c, v_c).reshape(bsz, -1, ATTN_WIDTH)
            y_c = ssd_output(st_cf[0], st_cf[1]) + flip(ssd_output(st_cb[0], st_cb[1]))
            ssd_c = ssd_finish(y_c, xs_c, z_c, ssd_d[l], ssd_norm[l])
            mix_c = jnp.concatenate([attn_c, ssd_c], axis=-1) @ w_out[l]
            ctx = ctx + mod_c[2] * rms_norm(mix_c, mix_post_norm[l])
            ffn_c = conv_glu(modulate(rms_norm(ctx, ffn_pre_norm[l]), mod_c[3], mod_c[4]),
                             w_up[l], ffn_conv_w[l], ffn_conv_b[l], w_down[l])
            ctx = ctx + mod_c[5] * rms_norm(ffn_c, ffn_post_norm[l])

        ffn_x = conv_glu(modulate(rms_norm(x, ffn_pre_norm[l]), mod_x[3], mod_x[4]),
                         w_up[l], ffn_conv_w[l], ffn_conv_b[l], w_down[l])
        x = x + mod_x[5] * rms_norm(ffn_x, ffn_post_norm[l])
    return x


import jax as _jax
import jax.numpy as _jnp

TWIN_FORMAT = 'train_step'
FWD_PARAMS = ['x', 'c', 'ctx', 'c_ctx', 'w_mod', 'b_mod', 'mix_pre_norm', 'mix_post_norm', 'w_in', 'q_norm', 'w_q_up', 'kv_norm', 'w_kv_up', 'ssd_conv_w', 'ssd_conv_b', 'ssd_a_log', 'ssd_dt_bias', 'ssd_d', 'ssd_norm', 'w_out', 'ffn_pre_norm', 'ffn_post_norm', 'w_up', 'ffn_conv_w', 'ffn_conv_b', 'w_down']
TWIN_WEIGHTS = ['c_ctx', 'w_mod', 'b_mod', 'mix_pre_norm', 'mix_post_norm', 'w_in', 'q_norm', 'w_q_up', 'kv_norm', 'w_kv_up', 'ssd_conv_w', 'ssd_conv_b', 'ssd_a_log', 'ssd_dt_bias', 'ssd_d', 'ssd_norm', 'w_out', 'ffn_pre_norm', 'ffn_post_norm', 'w_up', 'ffn_conv_w', 'ffn_conv_b', 'w_down']
TWIN_DIFF_INPUT = 'x'
TWIN_INPUTS = ['x', 'c', 'ctx', 'c_ctx', 'w_mod', 'b_mod', 'mix_pre_norm', 'mix_post_norm', 'w_in', 'q_norm', 'w_q_up', 'kv_norm', 'w_kv_up', 'ssd_conv_w', 'ssd_conv_b', 'ssd_a_log', 'ssd_dt_bias', 'ssd_d', 'ssd_norm', 'w_out', 'ffn_pre_norm', 'ffn_post_norm', 'w_up', 'ffn_conv_w', 'ffn_conv_b', 'w_down', 'loss_target', 'm_c_ctx', 'm_w_mod', 'm_b_mod', 'm_mix_pre_norm', 'm_mix_post_norm', 'm_w_in', 'm_q_norm', 'm_w_q_up', 'm_kv_norm', 'm_w_kv_up', 'm_ssd_conv_w', 'm_ssd_conv_b', 'm_ssd_a_log', 'm_ssd_dt_bias', 'm_ssd_d', 'm_ssd_norm', 'm_w_out', 'm_ffn_pre_norm', 'm_ffn_post_norm', 'm_w_up', 'm_ffn_conv_w', 'm_ffn_conv_b', 'm_w_down', 'v_c_ctx', 'v_w_mod', 'v_b_mod', 'v_mix_pre_norm', 'v_mix_post_norm', 'v_w_in', 'v_q_norm', 'v_w_q_up', 'v_kv_norm', 'v_w_kv_up', 'v_ssd_conv_w', 'v_ssd_conv_b', 'v_ssd_a_log', 'v_ssd_dt_bias', 'v_ssd_d', 'v_ssd_norm', 'v_w_out', 'v_ffn_pre_norm', 'v_ffn_post_norm', 'v_w_up', 'v_ffn_conv_w', 'v_ffn_conv_b', 'v_w_down']
TWIN_OUTPUTS = ['loss', 'grad_x', 'grad_c_ctx', 'grad_w_mod', 'grad_b_mod', 'grad_mix_pre_norm', 'grad_mix_post_norm', 'grad_w_in', 'grad_q_norm', 'grad_w_q_up', 'grad_kv_norm', 'grad_w_kv_up', 'grad_ssd_conv_w', 'grad_ssd_conv_b', 'grad_ssd_a_log', 'grad_ssd_dt_bias', 'grad_ssd_d', 'grad_ssd_norm', 'grad_w_out', 'grad_ffn_pre_norm', 'grad_ffn_post_norm', 'grad_w_up', 'grad_ffn_conv_w', 'grad_ffn_conv_b', 'grad_w_down', 'delta_c_ctx', 'delta_w_mod', 'delta_b_mod', 'delta_mix_pre_norm', 'delta_mix_post_norm', 'delta_w_in', 'delta_q_norm', 'delta_w_q_up', 'delta_kv_norm', 'delta_w_kv_up', 'delta_ssd_conv_w', 'delta_ssd_conv_b', 'delta_ssd_a_log', 'delta_ssd_dt_bias', 'delta_ssd_d', 'delta_ssd_norm', 'delta_w_out', 'delta_ffn_pre_norm', 'delta_ffn_post_norm', 'delta_w_up', 'delta_ffn_conv_w', 'delta_ffn_conv_b', 'delta_w_down', 'new_m_c_ctx', 'new_m_w_mod', 'new_m_b_mod', 'new_m_mix_pre_norm', 'new_m_mix_post_norm', 'new_m_w_in', 'new_m_q_norm', 'new_m_w_q_up', 'new_m_kv_norm', 'new_m_w_kv_up', 'new_m_ssd_conv_w', 'new_m_ssd_conv_b', 'new_m_ssd_a_log', 'new_m_ssd_dt_bias', 'new_m_ssd_d', 'new_m_ssd_norm', 'new_m_w_out', 'new_m_ffn_pre_norm', 'new_m_ffn_post_norm', 'new_m_w_up', 'new_m_ffn_conv_w', 'new_m_ffn_conv_b', 'new_m_w_down', 'new_v_c_ctx', 'new_v_w_mod', 'new_v_b_mod', 'new_v_mix_pre_norm', 'new_v_mix_post_norm', 'new_v_w_in', 'new_v_q_norm', 'new_v_w_q_up', 'new_v_kv_norm', 'new_v_w_kv_up', 'new_v_ssd_conv_w', 'new_v_ssd_conv_b', 'new_v_ssd_a_log', 'new_v_ssd_dt_bias', 'new_v_ssd_d', 'new_v_ssd_norm', 'new_v_w_out', 'new_v_ffn_pre_norm', 'new_v_ffn_post_norm', 'new_v_w_up', 'new_v_ffn_conv_w', 'new_v_ffn_conv_b', 'new_v_w_down']
TWIN_LEAF_KINDS = {'loss': 'loss', 'grad_x': 'grad_x', 'grad_c_ctx': 'grad_w', 'grad_w_mod': 'grad_w', 'grad_b_mod': 'grad_w', 'grad_mix_pre_norm': 'grad_w', 'grad_mix_post_norm': 'grad_w', 'grad_w_in': 'grad_w', 'grad_q_norm': 'grad_w', 'grad_w_q_up': 'grad_w', 'grad_kv_norm': 'grad_w', 'grad_w_kv_up': 'grad_w', 'grad_ssd_conv_w': 'grad_w', 'grad_ssd_conv_b': 'grad_w', 'grad_ssd_a_log': 'grad_w', 'grad_ssd_dt_bias': 'grad_w', 'grad_ssd_d': 'grad_w', 'grad_ssd_norm': 'grad_w', 'grad_w_out': 'grad_w', 'grad_ffn_pre_norm': 'grad_w', 'grad_ffn_post_norm': 'grad_w', 'grad_w_up': 'grad_w', 'grad_ffn_conv_w': 'grad_w', 'grad_ffn_conv_b': 'grad_w', 'grad_w_down': 'grad_w', 'delta_c_ctx': 'delta_w', 'delta_w_mod': 'delta_w', 'delta_b_mod': 'delta_w', 'delta_mix_pre_norm': 'delta_w', 'delta_mix_post_norm': 'delta_w', 'delta_w_in': 'delta_w', 'delta_q_norm': 'delta_w', 'delta_w_q_up': 'delta_w', 'delta_kv_norm': 'delta_w', 'delta_w_kv_up': 'delta_w', 'delta_ssd_conv_w': 'delta_w', 'delta_ssd_conv_b': 'delta_w', 'delta_ssd_a_log': 'delta_w', 'delta_ssd_dt_bias': 'delta_w', 'delta_ssd_d': 'delta_w', 'delta_ssd_norm': 'delta_w', 'delta_w_out': 'delta_w', 'delta_ffn_pre_norm': 'delta_w', 'delta_ffn_post_norm': 'delta_w', 'delta_w_up': 'delta_w', 'delta_ffn_conv_w': 'delta_w', 'delta_ffn_conv_b': 'delta_w', 'delta_w_down': 'delta_w', 'new_m_c_ctx': 'new_m', 'new_m_w_mod': 'new_m', 'new_m_b_mod': 'new_m', 'new_m_mix_pre_norm': 'new_m', 'new_m_mix_post_norm': 'new_m', 'new_m_w_in': 'new_m', 'new_m_q_norm': 'new_m', 'new_m_w_q_up': 'new_m', 'new_m_kv_norm': 'new_m', 'new_m_w_kv_up': 'new_m', 'new_m_ssd_conv_w': 'new_m', 'new_m_ssd_conv_b': 'new_m', 'new_m_ssd_a_log': 'new_m', 'new_m_ssd_dt_bias': 'new_m', 'new_m_ssd_d': 'new_m', 'new_m_ssd_norm': 'new_m', 'new_m_w_out': 'new_m', 'new_m_ffn_pre_norm': 'new_m', 'new_m_ffn_post_norm': 'new_m', 'new_m_w_up': 'new_m', 'new_m_ffn_conv_w': 'new_m', 'new_m_ffn_conv_b': 'new_m', 'new_m_w_down': 'new_m', 'new_v_c_ctx': 'new_v', 'new_v_w_mod': 'new_v', 'new_v_b_mod': 'new_v', 'new_v_mix_pre_norm': 'new_v', 'new_v_mix_post_norm': 'new_v', 'new_v_w_in': 'new_v', 'new_v_q_norm': 'new_v', 'new_v_w_q_up': 'new_v', 'new_v_kv_norm': 'new_v', 'new_v_w_kv_up': 'new_v', 'new_v_ssd_conv_w': 'new_v', 'new_v_ssd_conv_b': 'new_v', 'new_v_ssd_a_log': 'new_v', 'new_v_ssd_dt_bias': 'new_v', 'new_v_ssd_d': 'new_v', 'new_v_ssd_norm': 'new_v', 'new_v_w_out': 'new_v', 'new_v_ffn_pre_norm': 'new_v', 'new_v_ffn_post_norm': 'new_v', 'new_v_w_up': 'new_v', 'new_v_ffn_conv_w': 'new_v', 'new_v_ffn_conv_b': 'new_v', 'new_v_w_down': 'new_v'}


def _forward(args):
    return _fwd_reference(*[args[k] for k in FWD_PARAMS])


def _output_shape():
    out = _jax.eval_shape(lambda: _forward(_fwd_setup_inputs(0)))
    return out.shape, out.dtype

N_MICROBATCH = 1
ADAM_LR = 0.001
ADAM_B1 = 0.9
ADAM_B2 = 0.999
ADAM_EPS = 1e-08
ADAM_WD = 0.01
ADAM_STEP = 10
PER_EXAMPLE_BATCH_AXIS = {'x': 0, 'c': 0, 'ctx': 0, 'loss_target': 0}
SHARED_INPUTS = []
_WEIGHT_DTYPES = {'c_ctx': _jnp.float32, 'w_mod': _jnp.float32, 'b_mod': _jnp.float32, 'mix_pre_norm': _jnp.float32, 'mix_post_norm': _jnp.float32, 'w_in': _jnp.float32, 'q_norm': _jnp.float32, 'w_q_up': _jnp.float32, 'kv_norm': _jnp.float32, 'w_kv_up': _jnp.float32, 'ssd_conv_w': _jnp.float32, 'ssd_conv_b': _jnp.float32, 'ssd_a_log': _jnp.float32, 'ssd_dt_bias': _jnp.float32, 'ssd_d': _jnp.float32, 'ssd_norm': _jnp.float32, 'w_out': _jnp.float32, 'ffn_pre_norm': _jnp.float32, 'ffn_post_norm': _jnp.float32, 'w_up': _jnp.float32, 'ffn_conv_w': _jnp.float32, 'ffn_conv_b': _jnp.float32, 'w_down': _jnp.float32}
MOMENT_SCALE = {'c_ctx': 3.356604e-01, 'w_mod': 8.363755e+00, 'b_mod': 1.511280e+01, 'mix_pre_norm': 5.245446e-01, 'mix_post_norm': 3.065415e+01, 'w_in': 2.872243e+00, 'q_norm': 3.659152e-01, 'w_q_up': 1.583766e-01, 'kv_norm': 7.274366e+00, 'w_kv_up': 2.607830e+00, 'ssd_conv_w': 2.713574e+00, 'ssd_conv_b': 3.585095e+00, 'ssd_a_log': 1.051257e+01, 'ssd_dt_bias': 7.172658e-01, 'ssd_d': 3.326186e+00, 'ssd_norm': 3.306988e+00, 'w_out': 6.469214e+00, 'ffn_pre_norm': 1.285512e+00, 'ffn_post_norm': 3.116619e+01, 'w_up': 1.477991e+00, 'ffn_conv_w': 1.502333e+00, 'ffn_conv_b': 1.652477e+00, 'w_down': 3.227645e+00}


def _to_microbatches(a, axis):
    t = _jnp.moveaxis(a, axis, 0)
    t = t.reshape((N_MICROBATCH, t.shape[0] // N_MICROBATCH) + t.shape[1:])
    return _jnp.moveaxis(t, 1, axis + 1)


def setup_inputs(seed: int = 0) -> dict:
    inp = _fwd_setup_inputs(seed)
    key = _jax.random.fold_in(_jax.random.key(seed), 7919)
    shape, _ = _output_shape()
    out = dict(inp)
    out["loss_target"] = _jax.random.normal(_jax.random.fold_in(key, 0), shape, _jnp.float32)
    for i, name in enumerate(TWIN_WEIGHTS):
        w = inp[name].astype(_jnp.float32)
        if MOMENT_SCALE is None:
            s = _jnp.sqrt(_jnp.mean(_jnp.square(w)) + 1e-30)
        else:
            s = MOMENT_SCALE[name]
        km, kv = _jax.random.split(_jax.random.fold_in(key, i + 1))
        out[name] = w
        out["m_" + name] = s * _jax.random.normal(km, w.shape, _jnp.float32)
        out["v_" + name] = (s * s) * _jax.random.uniform(kv, w.shape, _jnp.float32, 0.5, 1.5)
    if N_MICROBATCH > 1:
        for name, axis in PER_EXAMPLE_BATCH_AXIS.items():
            out[name] = _to_microbatches(out[name], axis)
    return {'x': out['x'], 'c': out['c'], 'ctx': out['ctx'], 'c_ctx': out['c_ctx'], 'w_mod': out['w_mod'], 'b_mod': out['b_mod'], 'mix_pre_norm': out['mix_pre_norm'], 'mix_post_norm': out['mix_post_norm'], 'w_in': out['w_in'], 'q_norm': out['q_norm'], 'w_q_up': out['w_q_up'], 'kv_norm': out['kv_norm'], 'w_kv_up': out['w_kv_up'], 'ssd_conv_w': out['ssd_conv_w'], 'ssd_conv_b': out['ssd_conv_b'], 'ssd_a_log': out['ssd_a_log'], 'ssd_dt_bias': out['ssd_dt_bias'], 'ssd_d': out['ssd_d'], 'ssd_norm': out['ssd_norm'], 'w_out': out['w_out'], 'ffn_pre_norm': out['ffn_pre_norm'], 'ffn_post_norm': out['ffn_post_norm'], 'w_up': out['w_up'], 'ffn_conv_w': out['ffn_conv_w'], 'ffn_conv_b': out['ffn_conv_b'], 'w_down': out['w_down'], 'loss_target': out['loss_target'], 'm_c_ctx': out['m_c_ctx'], 'm_w_mod': out['m_w_mod'], 'm_b_mod': out['m_b_mod'], 'm_mix_pre_norm': out['m_mix_pre_norm'], 'm_mix_post_norm': out['m_mix_post_norm'], 'm_w_in': out['m_w_in'], 'm_q_norm': out['m_q_norm'], 'm_w_q_up': out['m_w_q_up'], 'm_kv_norm': out['m_kv_norm'], 'm_w_kv_up': out['m_w_kv_up'], 'm_ssd_conv_w': out['m_ssd_conv_w'], 'm_ssd_conv_b': out['m_ssd_conv_b'], 'm_ssd_a_log': out['m_ssd_a_log'], 'm_ssd_dt_bias': out['m_ssd_dt_bias'], 'm_ssd_d': out['m_ssd_d'], 'm_ssd_norm': out['m_ssd_norm'], 'm_w_out': out['m_w_out'], 'm_ffn_pre_norm': out['m_ffn_pre_norm'], 'm_ffn_post_norm': out['m_ffn_post_norm'], 'm_w_up': out['m_w_up'], 'm_ffn_conv_w': out['m_ffn_conv_w'], 'm_ffn_conv_b': out['m_ffn_conv_b'], 'm_w_down': out['m_w_down'], 'v_c_ctx': out['v_c_ctx'], 'v_w_mod': out['v_w_mod'], 'v_b_mod': out['v_b_mod'], 'v_mix_pre_norm': out['v_mix_pre_norm'], 'v_mix_post_norm': out['v_mix_post_norm'], 'v_w_in': out['v_w_in'], 'v_q_norm': out['v_q_norm'], 'v_w_q_up': out['v_w_q_up'], 'v_kv_norm': out['v_kv_norm'], 'v_w_kv_up': out['v_w_kv_up'], 'v_ssd_conv_w': out['v_ssd_conv_w'], 'v_ssd_conv_b': out['v_ssd_conv_b'], 'v_ssd_a_log': out['v_ssd_a_log'], 'v_ssd_dt_bias': out['v_ssd_dt_bias'], 'v_ssd_d': out['v_ssd_d'], 'v_ssd_norm': out['v_ssd_norm'], 'v_w_out': out['v_w_out'], 'v_ffn_pre_norm': out['v_ffn_pre_norm'], 'v_ffn_post_norm': out['v_ffn_post_norm'], 'v_w_up': out['v_w_up'], 'v_ffn_conv_w': out['v_ffn_conv_w'], 'v_ffn_conv_b': out['v_ffn_conv_b'], 'v_w_down': out['v_w_down']}


def _loss(weights, diff, rest, loss_target):
    with _jax.named_scope("forward"):
        args = {**rest, TWIN_DIFF_INPUT: diff, **{k: w.astype(_WEIGHT_DTYPES[k]) for k, w in weights.items()}}
        y = _forward(args)
    with _jax.named_scope("loss_head"):
        err = _jnp.square(y.astype(_jnp.float32) - loss_target)
        return 0.5 * _jnp.sum(_jnp.mean(err, axis=-1)) if err.ndim else 0.5 * err


def _adamw(w, g, m, v):
    m = ADAM_B1 * m + (1.0 - ADAM_B1) * g
    v = ADAM_B2 * v + (1.0 - ADAM_B2) * _jnp.square(g)
    m_hat = m / (1.0 - ADAM_B1 ** ADAM_STEP)
    v_hat = v / (1.0 - ADAM_B2 ** ADAM_STEP)
    delta = -ADAM_LR * (m_hat / (_jnp.sqrt(v_hat) + ADAM_EPS) + ADAM_WD * w)
    return delta, m, v


def reference(x, c, ctx, c_ctx, w_mod, b_mod, mix_pre_norm, mix_post_norm, w_in, q_norm, w_q_up, kv_norm, w_kv_up, ssd_conv_w, ssd_conv_b, ssd_a_log, ssd_dt_bias, ssd_d, ssd_norm, w_out, ffn_pre_norm, ffn_post_norm, w_up, ffn_conv_w, ffn_conv_b, w_down, loss_target, m_c_ctx, m_w_mod, m_b_mod, m_mix_pre_norm, m_mix_post_norm, m_w_in, m_q_norm, m_w_q_up, m_kv_norm, m_w_kv_up, m_ssd_conv_w, m_ssd_conv_b, m_ssd_a_log, m_ssd_dt_bias, m_ssd_d, m_ssd_norm, m_w_out, m_ffn_pre_norm, m_ffn_post_norm, m_w_up, m_ffn_conv_w, m_ffn_conv_b, m_w_down, v_c_ctx, v_w_mod, v_b_mod, v_mix_pre_norm, v_mix_post_norm, v_w_in, v_q_norm, v_w_q_up, v_kv_norm, v_w_kv_up, v_ssd_conv_w, v_ssd_conv_b, v_ssd_a_log, v_ssd_dt_bias, v_ssd_d, v_ssd_norm, v_w_out, v_ffn_pre_norm, v_ffn_post_norm, v_w_up, v_ffn_conv_w, v_ffn_conv_b, v_w_down):
    given = dict(x=x, c=c, ctx=ctx, c_ctx=c_ctx, w_mod=w_mod, b_mod=b_mod, mix_pre_norm=mix_pre_norm, mix_post_norm=mix_post_norm, w_in=w_in, q_norm=q_norm, w_q_up=w_q_up, kv_norm=kv_norm, w_kv_up=w_kv_up, ssd_conv_w=ssd_conv_w, ssd_conv_b=ssd_conv_b, ssd_a_log=ssd_a_log, ssd_dt_bias=ssd_dt_bias, ssd_d=ssd_d, ssd_norm=ssd_norm, w_out=w_out, ffn_pre_norm=ffn_pre_norm, ffn_post_norm=ffn_post_norm, w_up=w_up, ffn_conv_w=ffn_conv_w, ffn_conv_b=ffn_conv_b, w_down=w_down, loss_target=loss_target, m_c_ctx=m_c_ctx, m_w_mod=m_w_mod, m_b_mod=m_b_mod, m_mix_pre_norm=m_mix_pre_norm, m_mix_post_norm=m_mix_post_norm, m_w_in=m_w_in, m_q_norm=m_q_norm, m_w_q_up=m_w_q_up, m_kv_norm=m_kv_norm, m_w_kv_up=m_w_kv_up, m_ssd_conv_w=m_ssd_conv_w, m_ssd_conv_b=m_ssd_conv_b, m_ssd_a_log=m_ssd_a_log, m_ssd_dt_bias=m_ssd_dt_bias, m_ssd_d=m_ssd_d, m_ssd_norm=m_ssd_norm, m_w_out=m_w_out, m_ffn_pre_norm=m_ffn_pre_norm, m_ffn_post_norm=m_ffn_post_norm, m_w_up=m_w_up, m_ffn_conv_w=m_ffn_conv_w, m_ffn_conv_b=m_ffn_conv_b, m_w_down=m_w_down, v_c_ctx=v_c_ctx, v_w_mod=v_w_mod, v_b_mod=v_b_mod, v_mix_pre_norm=v_mix_pre_norm, v_mix_post_norm=v_mix_post_norm, v_w_in=v_w_in, v_q_norm=v_q_norm, v_w_q_up=v_w_q_up, v_kv_norm=v_kv_norm, v_w_kv_up=v_w_kv_up, v_ssd_conv_w=v_ssd_conv_w, v_ssd_conv_b=v_ssd_conv_b, v_ssd_a_log=v_ssd_a_log, v_ssd_dt_bias=v_ssd_dt_bias, v_ssd_d=v_ssd_d, v_ssd_norm=v_ssd_norm, v_w_out=v_w_out, v_ffn_pre_norm=v_ffn_pre_norm, v_ffn_post_norm=v_ffn_post_norm, v_w_up=v_w_up, v_ffn_conv_w=v_ffn_conv_w, v_ffn_conv_b=v_ffn_conv_b, v_w_down=v_w_down)
    weights = {n: given[n] for n in TWIN_WEIGHTS}
    shared = {n: given[n] for n in SHARED_INPUTS}
    per_example = {n: given[n] for n in ['x', 'c', 'ctx']}
    grad_fn = _jax.value_and_grad(_loss, argnums=(0, 1))

    def one_microbatch(ex, loss_target):
        ex = dict(ex)
        diff = ex.pop(TWIN_DIFF_INPUT)
        return grad_fn(weights, diff, {**shared, **ex}, loss_target)

    if N_MICROBATCH == 1:
        loss, (grad_w, grad_x) = one_microbatch(per_example, given["loss_target"])
    else:
        def body(carry, xs):
            loss_sum, grad_sum = carry
            l_k, (gw_k, gx_k) = one_microbatch(xs[0], xs[1])
            with _jax.named_scope("update"):
                return (loss_sum + l_k, _jax.tree.map(_jnp.add, grad_sum, gw_k)), gx_k

        init = (_jnp.zeros((), _jnp.float32), _jax.tree.map(_jnp.zeros_like, weights))
        (loss, grad_w), grad_x = _jax.lax.scan(body, init, (per_example, given["loss_target"]))
    with _jax.named_scope("update"):
        delta_w, new_m, new_v = {}, {}, {}
        for n in TWIN_WEIGHTS:
            delta_w[n], new_m[n], new_v[n] = _adamw(weights[n], grad_w[n], given["m_" + n], given["v_" + n])
    return (loss, grad_x, *[grad_w[n] for n in TWIN_WEIGHTS], *[delta_w[n] for n in TWIN_WEIGHTS],
            *[new_m[n] for n in TWIN_WEIGHTS], *[new_v[n] for n in TWIN_WEIGHTS])
```

```python
import functools
import math

import numpy as np
import jax
import jax.numpy as jnp
from jax import lax
from jax.experimental import pallas as pl
from jax.experimental.pallas import tpu as pltpu

F32 = jnp.float32
MXU_DTYPE = jnp.bfloat16
WIRE_DTYPE = jnp.bfloat16
VMEM_LIMIT_BYTES = 56 * 1024 * 1024
HIGHEST = lax.Precision.HIGHEST

D_MODEL = 1024
N_MOD = 6
EPS = 1e-6
GRID_W = 64
N_ATTN_HEADS = 16
QK_NOPE_DIM = 64
QK_ROPE_DIM = 32
QK_DIM = QK_NOPE_DIM + QK_ROPE_DIM
V_HEAD_DIM = 64
Q_LORA_RANK = 384
KV_LORA_RANK = 256
ROPE_THETA = 10000.0
ATTN_SCALE = QK_DIM ** -0.5
ATTN_WIDTH = N_ATTN_HEADS * V_HEAD_DIM
N_SSD_HEADS = 16
SSD_HEAD_DIM = 64
SSD_GROUPS = 2
HEADS_PER_GROUP = N_SSD_HEADS // SSD_GROUPS
SSD_STATE = 128
SSD_CONV = 5
SSD_CHUNK = 128
D_INNER = N_SSD_HEADS * SSD_HEAD_DIM
GN = SSD_GROUPS * SSD_STATE
XBC_WIDTH = D_INNER + 2 * GN
D_FF = 2816
FFN_CONV = 3
KRDT_WIDTH = 128
IN_WIDTH = Q_LORA_RANK + KV_LORA_RANK + QK_ROPE_DIM + D_INNER + XBC_WIDTH + 2 * N_SSD_HEADS

ADAM_LR = 0.001
ADAM_B1 = 0.9
ADAM_B2 = 0.999
ADAM_EPS = 1e-08
ADAM_WD = 0.01
ADAM_STEP = 10

N_CHIPS = 4
N_DEV = 8
MESH = pl.DeviceIdType.MESH
LANES = 128

BIG = (("w_mod", D_MODEL, N_MOD * D_MODEL, 1), ("w_in", D_MODEL, IN_WIDTH, 1),
       ("w_q_up", Q_LORA_RANK, N_ATTN_HEADS * QK_DIM, 1),
       ("w_kv_up", KV_LORA_RANK, N_ATTN_HEADS * (QK_NOPE_DIM + V_HEAD_DIM), 1),
       ("w_out", ATTN_WIDTH + D_INNER, D_MODEL, 0), ("w_up", D_MODEL, 2 * D_FF, 1),
       ("w_down", D_FF, D_MODEL, 0))
PACK_COLS = 1024


def _cparams(sem):
    return pltpu.CompilerParams(dimension_semantics=sem, vmem_limit_bytes=VMEM_LIMIT_BYTES)


def _pick(n, cands):
    for c in cands:
        if n % c == 0:
            return c
    return n


def _sigmoid(x):
    return 0.5 * (jnp.tanh(0.5 * x) + 1.0)


def _silu(x):
    return x * _sigmoid(x)


@jax.custom_vjp
def _softplus(x):
    u = jnp.exp(-jnp.abs(x))
    w = 1.0 + u
    log1p = jnp.where(w == 1.0, u, jnp.log(w) * (u / jnp.where(w == 1.0, 1.0, w - 1.0)))
    return jnp.maximum(x, 0.0) + log1p


def _softplus_fwd(x):
    return _softplus(x), x


def _softplus_bwd(x, g):
    return (g * _sigmoid(x),)


_softplus.defvjp(_softplus_fwd, _softplus_bwd)


@jax.custom_vjp
def _gelu(x):
    return 0.5 * x * (1.0 + lax.erf(x * (2.0 ** -0.5)))


def _gelu_fwd(x):
    return _gelu(x), x


def _gelu_bwd(x, g):
    cdf = 0.5 * (1.0 + lax.erf(x * (2.0 ** -0.5)))
    pdf = jnp.exp(-0.5 * x * x) * (1.0 / math.sqrt(2.0 * math.pi))
    return (g * (cdf + x * pdf),)


_gelu.defvjp(_gelu_fwd, _gelu_bwd)


def _rms(x, w):
    return x * lax.rsqrt(jnp.mean(x * x, axis=-1, keepdims=True) + EPS) * w


def _shift_rows_raw(x, off, seg):
    n = x.shape[0]
    if off == 0:
        return x
    r = pltpu.roll(x, (-off) % n, 0)
    idx = lax.broadcasted_iota(jnp.int32, x.shape, 0)
    src = idx + off
    ok = (src >= 0) & (src < n)
    if seg:
        ok = ok & ((idx < seg) == (src < seg))
    return jnp.where(ok, r, 0.0)


@functools.partial(jax.custom_vjp, nondiff_argnums=(1, 2))
def _shift_rows(x, off, seg):
    return _shift_rows_raw(x, off, seg)


def _shift_rows_fwd(x, off, seg):
    return _shift_rows_raw(x, off, seg), None


def _shift_rows_bwd(off, seg, _, g):
    return (_shift_rows_raw(g, -off, seg),)


_shift_rows.defvjp(_shift_rows_fwd, _shift_rows_bwd)


def _row_of(w, k):
    sel = lax.broadcasted_iota(jnp.int32, (w.shape[0], 1), 0) == k
    return jnp.sum(jnp.where(sel, w, 0.0), axis=0, keepdims=True)


def _col_of(w, k):
    sel = lax.broadcasted_iota(jnp.int32, (1, w.shape[1]), 1) == k
    return jnp.sum(jnp.where(sel, w, 0.0), axis=1, keepdims=True)


def _dwconv(x, w, seg):
    k = w.shape[0]
    acc = None
    for t in range(k):
        term = _shift_rows(x, t - k // 2, seg) * _row_of(w, t)
        acc = term if acc is None else acc + term
    return acc


def _dot(a, b, dims):
    return lax.dot_general(a.astype(MXU_DTYPE), b.astype(MXU_DTYPE), (dims, ((), ())),
                           preferred_element_type=F32)


def _dot_exact(a, b):
    return lax.dot_general(a, b, (((1,), (0,)), ((), ())), precision=HIGHEST,
                           preferred_element_type=F32)


def _mm(a, b, *, ta=False, tb=False, out_dtype=F32, name):
    if ta:
        kdim, m = a.shape
    else:
        m, kdim = a.shape
    if tb:
        n, k2 = b.shape
    else:
        k2, n = b.shape
    assert kdim == k2, (a.shape, b.shape, ta, tb)
    tm = _pick(m, (512, 384, 256, 128))
    tn = _pick(n, (512, 384, 256, 128))
    tk = kdim if kdim <= 1024 else _pick(kdim, (1024, 512, 256, 128))
    nk = kdim // tk
    a_spec = pl.BlockSpec((tk, tm), lambda i, j, k: (k, i)) if ta else pl.BlockSpec((tm, tk), lambda i, j, k: (i, k))
    b_spec = pl.BlockSpec((tn, tk), lambda i, j, k: (j, k)) if tb else pl.BlockSpec((tk, tn), lambda i, j, k: (k, j))
    dims = ((0,) if ta else (1,), (1,) if tb else (0,))

    def body(a_ref, b_ref, o_ref, acc_ref):
        k = pl.program_id(2)

        @pl.when(k == 0)
        def _():
            acc_ref[...] = jnp.zeros_like(acc_ref)

        acc_ref[...] += _dot(a_ref[...], b_ref[...], dims)

        @pl.when(k == nk - 1)
        def _():
            o_ref[...] = acc_ref[...].astype(o_ref.dtype)

    return pl.pallas_call(
        body, name=name, grid=(m // tm, n // tn, nk),
        in_specs=[a_spec, b_spec], out_specs=pl.BlockSpec((tm, tn), lambda i, j, k: (i, j)),
        out_shape=jax.ShapeDtypeStruct((m, n), out_dtype),
        scratch_shapes=[pltpu.VMEM((tm, tn), F32)],
        compiler_params=_cparams(("parallel", "parallel", "arbitrary")),
    )(a, b)


def _row_specs(toks, poss, vecs, bvecs, tl, nctx, nb):
    specs, args = [], []
    for arr, off, cw, ci in toks:
        cw = arr.shape[2] if cw is None else cw
        specs.append(pl.BlockSpec((1, tl, cw), lambda b, l, off=off, ci=ci: (b, l + off, ci)))
        args.append(arr)
    for arr in poss:
        specs.append(pl.BlockSpec((1, tl, arr.shape[2]), lambda b, l: (0, l, 0)))
        args.append(arr)
    for arr in vecs:
        specs.append(pl.BlockSpec(arr.shape, lambda b, l: (0, 0)))
        args.append(arr)
    for arr in bvecs:
        if nctx:
            specs.append(pl.BlockSpec((1, 1, arr.shape[2]), lambda b, l: (jnp.where(l < nctx, nb, b), 0, 0)))
        else:
            specs.append(pl.BlockSpec((1, 1, arr.shape[2]), lambda b, l: (b, 0, 0)))
        args.append(arr)
    return specs, args


def _row_fwd(fn, *, toks, poss=(), vecs=(), bvecs=(), outs, nb, nl, tl, nctx=0, name):
    nt, npos, nv, nbv = len(toks), len(poss), len(vecs), len(bvecs)
    specs, args = _row_specs(toks, poss, vecs, bvecs, tl, nctx, nb)

    def body(*refs):
        ins, os = refs[:len(specs)], refs[len(specs):]
        tv = [r[0].astype(F32) for r in ins[:nt]]
        pv = [r[0] for r in ins[nt:nt + npos]]
        vv = [r[...] for r in ins[nt + npos:nt + npos + nv]]
        bv = [r[0] for r in ins[nt + npos + nv:]]
        res = fn(*tv, *pv, *vv, *bv)
        for o, r in zip(os, res):
            o[0] = r.astype(o.dtype)

    return pl.pallas_call(
        body, name=name, grid=(nb, nl // tl), in_specs=specs,
        out_specs=[pl.BlockSpec((1, tl, c), lambda b, l: (b, l, 0)) for c, _ in outs],
        out_shape=[jax.ShapeDtypeStruct((nb, nl, c), dt) for c, dt in outs],
        compiler_params=_cparams(("parallel", "parallel")),
    )(*args)


def _row_bwd(fn, *, toks, poss=(), vecs=(), bvecs=(), cots, tok_grads, emit=(), nb, nl, tl, nctx=0, name):
    nt, npos, nv, nbv = len(toks), len(poss), len(vecs), len(bvecs)
    specs, args = _row_specs(toks, poss, vecs, bvecs, tl, nctx, nb)
    n_in = len(specs)
    cot_slots = []
    for arr, off in cots:
        if arr is None:
            cot_slots.append(None)
            continue
        cot_slots.append((len(specs), off))
        specs.append(pl.BlockSpec((1, tl, arr.shape[2]), lambda b, l, off=off: (b, jnp.maximum(l + off, 0), 0)))
        args.append(arr)
    n_all_in = len(specs)

    out_specs, out_shapes = [], []
    tok_out = []
    for (arr, off, cw, ci), dt in zip(toks, tok_grads):
        if dt is None:
            tok_out.append(None)
            continue
        cw = arr.shape[2] if cw is None else cw
        tok_out.append(len(out_specs))
        out_specs.append(pl.BlockSpec((1, tl, cw), lambda b, l: (b, l, 0)))
        out_shapes.append(jax.ShapeDtypeStruct((nb, nl, cw), dt))
    vec_out = []
    for arr in vecs:
        vec_out.append(len(out_specs))
        out_specs.append(pl.BlockSpec(arr.shape, lambda b, l: (0, 0)))
        out_shapes.append(jax.ShapeDtypeStruct(arr.shape, F32))
    bv_out = []
    for arr in bvecs:
        c = arr.shape[2]
        lat = len(out_specs)
        out_specs.append(pl.BlockSpec((1, 1, c), lambda b, l: (b, 0, 0)))
        out_shapes.append(jax.ShapeDtypeStruct((nb, 1, c), F32))
        ctx = None
        if nctx:
            ctx = len(out_specs)
            out_specs.append(pl.BlockSpec((1, 1, c), lambda b, l: (0, 0, 0)))
            out_shapes.append(jax.ShapeDtypeStruct((1, 1, c), F32))
        bv_out.append((lat, ctx))
    emit_out = []
    emit_cols = {}
    for idx, c, dt in emit:
        emit_out.append((idx, len(out_specs)))
        out_specs.append(pl.BlockSpec((1, tl, c), lambda b, l: (b, l, 0)))
        out_shapes.append(jax.ShapeDtypeStruct((nb, nl, c), dt))

    def body(*refs):
        ins, os = refs[:n_all_in], refs[n_all_in:]
        b, l = pl.program_id(0), pl.program_id(1)
        tv = [r[0].astype(F32) for r in ins[:nt]]
        pv = [r[0] for r in ins[nt:nt + npos]]
        vv = [r[...] for r in ins[nt + npos:nt + npos + nv]]
        bv = [r[0] for r in ins[nt + npos + nv:n_in]]

        def f(*d):
            return tuple(fn(*d[:nt], *pv, *d[nt:]))

        res, vjp = jax.vjp(f, *tv, *vv, *bv)
        cts = []
        for r, slot in zip(res, cot_slots):
            if slot is None:
                cts.append(jnp.zeros_like(r))
            else:
                i, off = slot
                ct = ins[i][0].astype(F32)
                if off < 0:
                    ct = jnp.where(l + off >= 0, ct, 0.0)
                cts.append(ct)
        grads = vjp(tuple(cts))

        for g, slot in zip(grads[:nt], tok_out):
            if slot is not None:
                os[slot][0] = g.astype(os[slot].dtype)

        @pl.when((b == 0) & (l == 0))
        def _():
            for slot in vec_out:
                os[slot][...] = jnp.zeros_like(os[slot])
            for _, ctx in bv_out:
                if ctx is not None:
                    os[ctx][...] = jnp.zeros_like(os[ctx])

        @pl.when(l == 0)
        def _():
            for lat, _ in bv_out:
                os[lat][...] = jnp.zeros_like(os[lat])

        for g, slot in zip(grads[nt:nt + nv], vec_out):
            os[slot][...] += g
        for g, (lat, ctx) in zip(grads[nt + nv:], bv_out):
            if ctx is None:
                os[lat][0] += g
            else:
                is_ctx = l < nctx
                os[lat][0] += jnp.where(is_ctx, 0.0, g)
                os[ctx][0] += jnp.where(is_ctx, g, 0.0)
        for idx, slot in emit_out:
            os[slot][0] = res[idx].astype(os[slot].dtype)

    out = pl.pallas_call(
        body, name=name, grid=(nb, nl // tl), in_specs=specs, out_specs=out_specs, out_shape=out_shapes,
        compiler_params=_cparams(("arbitrary", "arbitrary")),
    )(*args)
    tg = [None if s is None else out[s] for s in tok_out]
    vg = [out[s] for s in vec_out]
    bg = [(out[lat], None if ctx is None else out[ctx]) for lat, ctx in bv_out]
    em = [out[s] for _, s in emit_out]
    return tg, vg, bg, em


def _seq_specs(toks, vecs, nl, cb):
    specs, args = [], []
    for arr, off in toks:
        specs.append(pl.BlockSpec((1, nl, cb), lambda j, b, off=off: (b, 0, j + off)))
        args.append(arr)
    for arr, off in vecs:
        specs.append(pl.BlockSpec((arr.shape[0], cb), lambda j, b, off=off: (0, j + off)))
        args.append(arr)
    return specs, args


def _seq_fwd(fn, *, toks, vecs, outs, nb, nl, nc, cb, name):
    nt = len(toks)
    specs, args = _seq_specs(toks, vecs, nl, cb)

    def body(*refs):
        ins, os = refs[:len(specs)], refs[len(specs):]
        tv = [r[0].astype(F32) for r in ins[:nt]]
        vv = [r[...] for r in ins[nt:]]
        for o, r in zip(os, fn(*tv, *vv)):
            o[0] = r.astype(o.dtype)

    return pl.pallas_call(
        body, name=name, grid=(nc // cb, nb), in_specs=specs,
        out_specs=[pl.BlockSpec((1, nl, cb), lambda j, b: (b, 0, j)) for _ in outs],
        out_shape=[jax.ShapeDtypeStruct((nb, nl, nc), dt) for dt in outs],
        compiler_params=_cparams(("parallel", "parallel")),
    )(*args)


def _seq_bwd(fn, *, toks, vecs, cots, tok_grads, nb, nl, nc, cb, name):
    nt, nv = len(toks), len(vecs)
    specs, args = _seq_specs(toks, vecs, nl, cb)
    n_in = len(specs)
    for arr in cots:
        specs.append(pl.BlockSpec((1, nl, cb), lambda j, b: (b, 0, j)))
        args.append(arr)
    out_specs, out_shapes = [], []
    for dt in tok_grads:
        out_specs.append(pl.BlockSpec((1, nl, cb), lambda j, b: (b, 0, j)))
        out_shapes.append(jax.ShapeDtypeStruct((nb, nl, nc), dt))
    for arr, _ in vecs:
        out_specs.append(pl.BlockSpec((arr.shape[0], cb), lambda j, b: (0, j)))
        out_shapes.append(jax.ShapeDtypeStruct((arr.shape[0], nc), F32))

    def body(*refs):
        ins, os = refs[:len(specs)], refs[len(specs):]
        b = pl.program_id(1)
        tv = [r[0].astype(F32) for r in ins[:nt]]
        vv = [r[...] for r in ins[nt:n_in]]
        _, vjp = jax.vjp(lambda *d: tuple(fn(*d)), *tv, *vv)
        grads = vjp(tuple(r[0].astype(F32) for r in ins[n_in:]))
        for g, o in zip(grads[:nt], os[:nt]):
            o[0] = g.astype(o.dtype)

        @pl.when(b == 0)
        def _():
            for o in os[nt:]:
                o[...] = jnp.zeros_like(o)

        for g, o in zip(grads[nt:], os[nt:]):
            o[...] += g

    out = pl.pallas_call(
        body, name=name, grid=(nc // cb, nb), in_specs=specs, out_specs=out_specs, out_shape=out_shapes,
        compiler_params=_cparams(("parallel", "arbitrary")),
    )(*args)
    return out[:nt], out[nt:]


def _attn_fn(q, k, v):
    s = _dot(q, k, ((1,), (1,))) * ATTN_SCALE
    m = lax.stop_gradient(jnp.max(s, axis=-1, keepdims=True))
    p = jnp.exp(s - m)
    p = p / jnp.sum(p, axis=-1, keepdims=True)
    return _dot(p, v, ((1,), (0,)))


def _attn_fwd(qh, kh, vh, *, tq, name):
    nbh, s, dq = qh.shape
    lk, dv = kh.shape[1], vh.shape[2]

    def body(q_ref, k_ref, v_ref, o_ref):
        o_ref[0] = _attn_fn(q_ref[0].astype(F32), k_ref[0].astype(F32), v_ref[0].astype(F32)).astype(o_ref.dtype)

    return pl.pallas_call(
        body, name=name, grid=(nbh, s // tq),
        in_specs=[pl.BlockSpec((1, tq, dq), lambda i, j: (i, j, 0)), pl.BlockSpec((1, lk, dq), lambda i, j: (i, 0, 0)),
                  pl.BlockSpec((1, lk, dv), lambda i, j: (i, 0, 0))],
        out_specs=pl.BlockSpec((1, tq, dv), lambda i, j: (i, j, 0)),
        out_shape=jax.ShapeDtypeStruct((nbh, s, dv), MXU_DTYPE),
        compiler_params=_cparams(("parallel", "parallel")),
    )(qh, kh, vh)


def _attn_bwd(qh, kh, vh, doh, *, tq, name):
    nbh, s, dq = qh.shape
    lk, dv = kh.shape[1], vh.shape[2]

    def body(q_ref, k_ref, v_ref, do_ref, dq_ref, dk_ref, dv_ref):
        j = pl.program_id(1)
        _, vjp = jax.vjp(_attn_fn, q_ref[0].astype(F32), k_ref[0].astype(F32), v_ref[0].astype(F32))
        gq, gk, gv = vjp(do_ref[0].astype(F32))
        dq_ref[0] = gq

        @pl.when(j == 0)
        def _():
            dk_ref[...] = jnp.zeros_like(dk_ref)
            dv_ref[...] = jnp.zeros_like(dv_ref)

        dk_ref[0] += gk
        dv_ref[0] += gv

    return pl.pallas_call(
        body, name=name, grid=(nbh, s // tq),
        in_specs=[pl.BlockSpec((1, tq, dq), lambda i, j: (i, j, 0)), pl.BlockSpec((1, lk, dq), lambda i, j: (i, 0, 0)),
                  pl.BlockSpec((1, lk, dv), lambda i, j: (i, 0, 0)), pl.BlockSpec((1, tq, dv), lambda i, j: (i, j, 0))],
        out_specs=[pl.BlockSpec((1, tq, dq), lambda i, j: (i, j, 0)), pl.BlockSpec((1, lk, dq), lambda i, j: (i, 0, 0)),
                   pl.BlockSpec((1, lk, dv), lambda i, j: (i, 0, 0))],
        out_shape=[jax.ShapeDtypeStruct((nbh, s, dq), F32), jax.ShapeDtypeStruct((nbh, lk, dq), F32),
                   jax.ShapeDtypeStruct((nbh, lk, dv), F32)],
        compiler_params=_cparams(("parallel", "arbitrary")),
    )(qh, kh, vh, doh)


N_PAIRS = HEADS_PER_GROUP // 2
PAIR_W = 2 * SSD_HEAD_DIM


def _ssd_chunk(states, xs, dtc, dtr, bm, cm, ac, ar, *, reverse):
    q = dtc.shape[0]
    row = lax.broadcasted_iota(jnp.int32, (q, q), 0)
    col = lax.broadcasted_iota(jnp.int32, (q, q), 1)
    if reverse:
        tri_c, tri_r, mask = col < row, row < col, col >= row
    else:
        tri_c, tri_r, mask = col <= row, row <= col, col <= row
    a_col, a_row = dtc * ac, dtr * ar
    cum_c = _dot_exact(tri_c.astype(F32), a_col)
    cum_r = _dot_exact(a_row, tri_r.astype(F32))
    tot = jnp.sum(a_col, axis=0, keepdims=True)
    cb = _dot(cm, bm, ((1,), (1,)))
    first = lax.broadcasted_iota(jnp.int32, (1, PAIR_W), 1) < SSD_HEAD_DIM
    first_rows = lax.broadcasted_iota(jnp.int32, (PAIR_W, 1), 0) < SSD_HEAD_DIM
    ys, new_states = [], []
    for pr in range(N_PAIRS):
        per_head = []
        for h in range(2):
            e = 2 * pr + h
            cc, cr, dc, te = _col_of(cum_c, e), _row_of(cum_r, e), _col_of(dtc, e), _col_of(tot, e)
            if reverse:
                within = jnp.exp(jnp.where(mask, cr - cc, -jnp.inf))
                into, to_end = jnp.exp(te - cc), jnp.exp(cc)
            else:
                within = jnp.exp(jnp.where(mask, cc - cr, -jnp.inf))
                into, to_end = jnp.exp(cc), jnp.exp(te - cc)
            per_head.append((cb * within, dc, into, to_end, jnp.exp(te)))
        (m0, dc0, in0, end0, t0), (m1, dc1, in1, end1, t1) = per_head
        xd = xs[pr] * jnp.where(first, dc0, dc1)
        y_diag = (_dot(m0, jnp.where(first, xd, 0.0), ((1,), (0,)))
                  + _dot(m1, jnp.where(first, 0.0, xd), ((1,), (0,))))
        y_off = _dot(cm, states[pr], ((1,), (1,))) * jnp.where(first, in0, in1)
        ys.append(y_diag + y_off)
        grow = _dot(xd * jnp.where(first, end0, end1), bm, ((0,), (0,)))
        new_states.append(states[pr] * jnp.where(first_rows, t0, t1) + grow)
    return tuple(ys) + tuple(new_states)


def _chunk_of_step(t, ncc, nch, reverse):
    if not reverse:
        return t
    return jnp.where(t < ncc, ncc - 1 - t, nch - 1 - (t - ncc))


def _scan_in_specs(nch, ncc, reverse, back):
    q, n, e = SSD_CHUNK, SSD_STATE, HEADS_PER_GROUP
    xcols = D_INNER // SSD_GROUPS

    def ch(t):
        return _chunk_of_step((nch - 1 - t) if back else t, ncc, nch, reverse)

    return ch, [
        pl.BlockSpec((1, q, xcols), lambda b, g, t: (b, ch(t), g)),
        pl.BlockSpec((1, q, n), lambda b, g, t: (b, ch(t), D_INNER // n + g)),
        pl.BlockSpec((1, q, n), lambda b, g, t: (b, ch(t), D_INNER // n + SSD_GROUPS + g)),
        pl.BlockSpec((1, 1, q, e), lambda b, g, t: (b, g, ch(t), 0)),
        pl.BlockSpec((1, 1, e, q), lambda b, g, t: (b, g, 0, ch(t))),
        pl.BlockSpec((1, 1, e), lambda b, g, t: (g, 0, 0)),
        pl.BlockSpec((1, e, 1), lambda b, g, t: (g, 0, 0)),
    ]


def _scan_fwd(xbc, dtc, dtr, ac, ar, *, ncc, reverse, name):
    nb, lt, _ = xbc.shape
    q, n = SSD_CHUNK, SSD_STATE
    nch = lt // q
    xcols = D_INNER // SSD_GROUPS
    ch, in_specs = _scan_in_specs(nch, ncc, reverse, False)

    def body(x_ref, b_ref, c_ref, dtc_ref, dtr_ref, ac_ref, ar_ref, y_ref, ent_ref, st_ref):
        t = pl.program_id(2)

        @pl.when(t == 0)
        def _():
            st_ref[...] = jnp.zeros_like(st_ref)

        states = [st_ref[i] for i in range(N_PAIRS)]
        for i in range(N_PAIRS):
            ent_ref[0, 0, 0, i] = states[i]
        xs = [x_ref[0, :, pl.ds(PAIR_W * i, PAIR_W)] for i in range(N_PAIRS)]
        res = _ssd_chunk(states, xs, dtc_ref[0, 0], dtr_ref[0, 0], b_ref[0], c_ref[0], ac_ref[0], ar_ref[0],
                         reverse=reverse)
        for i in range(N_PAIRS):
            y_ref[0, :, pl.ds(PAIR_W * i, PAIR_W)] = res[i]
            st_ref[i] = res[N_PAIRS + i]

    return pl.pallas_call(
        body, name=name, grid=(nb, SSD_GROUPS, nch), in_specs=in_specs,
        out_specs=[pl.BlockSpec((1, q, xcols), lambda b, g, t: (b, ch(t), g)),
                   pl.BlockSpec((1, 1, 1, N_PAIRS, PAIR_W, n), lambda b, g, t: (b, g, t, 0, 0, 0))],
        out_shape=[jax.ShapeDtypeStruct((nb, lt, D_INNER), F32),
                   jax.ShapeDtypeStruct((nb, SSD_GROUPS, nch, N_PAIRS, PAIR_W, n), F32)],
        scratch_shapes=[pltpu.VMEM((N_PAIRS, PAIR_W, n), F32)],
        compiler_params=_cparams(("parallel", "parallel", "arbitrary")),
    )(xbc, xbc, xbc, dtc, dtr, ac, ar)


def _scan_bwd(xbc, dtc, dtr, ac, ar, entering, dy, *, ncc, reverse, name):
    nb, lt, _ = xbc.shape
    q, n, e = SSD_CHUNK, SSD_STATE, HEADS_PER_GROUP
    nch = lt // q
    xcols = D_INNER // SSD_GROUPS
    ch, in_specs = _scan_in_specs(nch, ncc, reverse, True)
    in_specs = in_specs + [
        pl.BlockSpec((1, 1, 1, N_PAIRS, PAIR_W, n), lambda b, g, t: (b, g, nch - 1 - t, 0, 0, 0)),
        pl.BlockSpec((1, q, xcols), lambda b, g, t: (b, jnp.maximum(ch(t) - ncc, 0), g)),
    ]

    def body(x_ref, b_ref, c_ref, dtc_ref, dtr_ref, ac_ref, ar_ref, ent_ref, dy_ref,
             dx_ref, db_ref, dc_ref, ddtc_ref, ddtr_ref, dac_ref, dar_ref, ds_ref):
        t = pl.program_id(2)

        @pl.when(t == 0)
        def _():
            ds_ref[...] = jnp.zeros_like(ds_ref)
            dac_ref[...] = jnp.zeros_like(dac_ref)
            dar_ref[...] = jnp.zeros_like(dar_ref)

        states = [ent_ref[0, 0, 0, i] for i in range(N_PAIRS)]
        xs = [x_ref[0, :, pl.ds(PAIR_W * i, PAIR_W)] for i in range(N_PAIRS)]

        def f(states, xs, dtc, dtr, bm, cm, ac, ar):
            return _ssd_chunk(states, xs, dtc, dtr, bm, cm, ac, ar, reverse=reverse)

        _, vjp = jax.vjp(f, states, xs, dtc_ref[0, 0], dtr_ref[0, 0], b_ref[0], c_ref[0], ac_ref[0], ar_ref[0])
        latent = ch(t) >= ncc
        dys = [jnp.where(latent, dy_ref[0, :, pl.ds(PAIR_W * i, PAIR_W)], 0.0) for i in range(N_PAIRS)]
        gs, gx, gdtc, gdtr, gb, gc, gac, gar = vjp(tuple(dys) + tuple(ds_ref[i] for i in range(N_PAIRS)))
        for i in range(N_PAIRS):
            ds_ref[i] = gs[i]
            dx_ref[0, :, pl.ds(PAIR_W * i, PAIR_W)] = gx[i]
        db_ref[0] = gb
        dc_ref[0] = gc
        ddtc_ref[0, 0] = gdtc
        ddtr_ref[0, 0] = gdtr
        dac_ref[0, 0] += gac
        dar_ref[0, 0] += gar

    return pl.pallas_call(
        body, name=name, grid=(nb, SSD_GROUPS, nch), in_specs=in_specs,
        out_specs=[pl.BlockSpec((1, q, xcols), lambda b, g, t: (b, ch(t), g)),
                   pl.BlockSpec((1, q, n), lambda b, g, t: (b, ch(t), g)),
                   pl.BlockSpec((1, q, n), lambda b, g, t: (b, ch(t), g)),
                   pl.BlockSpec((1, 1, q, e), lambda b, g, t: (b, g, ch(t), 0)),
                   pl.BlockSpec((1, 1, e, q), lambda b, g, t: (b, g, 0, ch(t))),
                   pl.BlockSpec((1, 1, 1, e), lambda b, g, t: (b, g, 0, 0)),
                   pl.BlockSpec((1, 1, e, 1), lambda b, g, t: (b, g, 0, 0))],
        out_shape=[jax.ShapeDtypeStruct((nb, lt, D_INNER), F32), jax.ShapeDtypeStruct((nb, lt, GN), F32),
                   jax.ShapeDtypeStruct((nb, lt, GN), F32), jax.ShapeDtypeStruct((nb, SSD_GROUPS, lt, e), F32),
                   jax.ShapeDtypeStruct((nb, SSD_GROUPS, e, lt), F32), jax.ShapeDtypeStruct((nb, SSD_GROUPS, 1, e), F32),
                   jax.ShapeDtypeStruct((nb, SSD_GROUPS, e, 1), F32)],
        scratch_shapes=[pltpu.VMEM((N_PAIRS, PAIR_W, n), F32)],
        compiler_params=_cparams(("parallel", "parallel", "arbitrary")),
    )(xbc, xbc, xbc, dtc, dtr, ac, ar, entering, dy)


def _adamw(w, g, m, v, *, name):
    r, c = w.shape
    tr = _pick(r, (256, 176, 128, 96, 64, 8))
    c1 = 1.0 / (1.0 - ADAM_B1 ** ADAM_STEP)
    c2 = 1.0 / (1.0 - ADAM_B2 ** ADAM_STEP)

    def body(w_ref, g_ref, m_ref, v_ref, d_ref, nm_ref, nv_ref):
        gv = g_ref[...]
        nm = ADAM_B1 * m_ref[...] + (1.0 - ADAM_B1) * gv
        nv = ADAM_B2 * v_ref[...] + (1.0 - ADAM_B2) * (gv * gv)
        d_ref[...] = -ADAM_LR * ((nm * c1) / (jnp.sqrt(nv * c2) + ADAM_EPS) + ADAM_WD * w_ref[...])
        nm_ref[...] = nm
        nv_ref[...] = nv

    spec = pl.BlockSpec((tr, c), lambda i: (i, 0))
    return pl.pallas_call(
        body, name=name, grid=(r // tr,), in_specs=[spec] * 4, out_specs=[spec] * 3,
        out_shape=[jax.ShapeDtypeStruct((r, c), F32)] * 3, compiler_params=_cparams(("parallel",)),
    )(w, g, m, v)


def _sum_slots(x, *, out_dtype, name):
    n, r, c = x.shape
    tr = r if r <= 1024 else _pick(r, (656, 512, 256, 128, 64, 32, 16))

    def body(x_ref, o_ref):
        acc = x_ref[0].astype(F32)
        for k in range(1, n):
            acc = acc + x_ref[k].astype(F32)
        o_ref[...] = acc.astype(o_ref.dtype)

    return pl.pallas_call(
        body, name=name, grid=(r // tr,), in_specs=[pl.BlockSpec((n, tr, c), lambda i: (0, i, 0))],
        out_specs=pl.BlockSpec((tr, c), lambda i: (i, 0)), out_shape=jax.ShapeDtypeStruct((r, c), out_dtype),
        compiler_params=_cparams(("parallel",)),
    )(x)


ANY = pl.BlockSpec(memory_space=pl.ANY)


def _place():
    return lax.axis_index("x"), lax.axis_index("y"), lax.axis_index("c")


def _allgather_small(v, *, name):
    r, c = v.shape

    def body(v_ref, out_ref, send_sems, recv_sems, local_sem):
        x, y, cc = _place()
        me = 4 * x + 2 * y + cc
        mine = pltpu.make_async_copy(v_ref, out_ref.at[me], local_sem)
        mine.start()
        copies = []
        for k in range(1, N_DEV):
            fx, fy, fc = (k >> 2) & 1, (k >> 1) & 1, k & 1
            peer = (1 - x if fx else x, 1 - y if fy else y, 1 - cc if fc else cc)
            copies.append(pltpu.make_async_remote_copy(
                src_ref=v_ref, dst_ref=out_ref.at[me], send_sem=send_sems.at[k - 1], recv_sem=recv_sems.at[k - 1],
                device_id=peer, device_id_type=MESH))
        for cp in copies:
            cp.start()
        for cp in copies:
            cp.wait()
        mine.wait()

    return pl.pallas_call(
        body, name=name, in_specs=[ANY], out_specs=ANY, out_shape=jax.ShapeDtypeStruct((N_DEV, r, c), v.dtype),
        scratch_shapes=[pltpu.SemaphoreType.DMA((N_DEV - 1,)), pltpu.SemaphoreType.DMA((N_DEV - 1,)),
                        pltpu.SemaphoreType.DMA],
    )(v)


def _other_chips(x, y):
    return [(1 - x, y), (x, 1 - y), (1 - x, 1 - y)]


def _gather_shards(mine, *, name):
    r, c = mine.shape
    half = r // 2

    def body(v_ref, out_ref, send_sems, recv_sems, local_sem):
        x, y, cc = _place()
        sibling = (x, y, 1 - cc)
        chips = _other_chips(x, y)

        def rows(px, py, pc):
            return out_ref.at[2 * px + py, pl.ds(pc * half, half), :]

        def copy(k, block, to, src=None):
            return pltpu.make_async_remote_copy(
                src_ref=rows(*block) if src is None else src, dst_ref=rows(*block),
                send_sem=send_sems.at[k], recv_sem=recv_sems.at[k], device_id=to, device_id_type=MESH)

        local = pltpu.make_async_copy(v_ref, out_ref.at[2 * x + y], local_sem)
        local.start()
        my_half = v_ref.at[pl.ds(cc * half, half), :]
        first = [copy(j, (x, y, cc), (*chip, cc), src=my_half) for j, chip in enumerate(chips)]
        for cp in first:
            cp.start()
        passed = [copy(3 + j, (*chip, cc), sibling) for j, chip in enumerate(chips)]
        for j, chip in enumerate(chips):
            copy(j, (*chip, cc), (x, y, cc)).wait_recv()
            passed[j].start()
        for j, chip in enumerate(chips):
            copy(3 + j, (*chip, 1 - cc), (x, y, cc)).wait_recv()
        for cp in first + passed:
            cp.wait_send()
        local.wait()

    return pl.pallas_call(
        body, name=name, in_specs=[ANY], out_specs=ANY, out_shape=jax.ShapeDtypeStruct((N_CHIPS, r, c), mine.dtype),
        scratch_shapes=[pltpu.SemaphoreType.DMA((6,)), pltpu.SemaphoreType.DMA((6,)), pltpu.SemaphoreType.DMA],
    )(mine)


def _swap_halves(g, *, name):
    n, _, r, c = g.shape

    def body(g_ref, own_ref, got_ref, send_sems, recv_sems, local_sems):
        x, y, cc = _place()
        sibling = (x, y, 1 - cc)
        locs, rems = [], []
        for j in range(n):
            locs.append(pltpu.make_async_copy(g_ref.at[j, cc], own_ref.at[j], local_sems.at[j]))
            rems.append(pltpu.make_async_remote_copy(
                src_ref=g_ref.at[j, 1 - cc], dst_ref=got_ref.at[j], send_sem=send_sems.at[j],
                recv_sem=recv_sems.at[j], device_id=sibling, device_id_type=MESH))
        for cp in rems + locs:
            cp.start()
        for cp in rems + locs:
            cp.wait()

    return pl.pallas_call(
        body, name=name, in_specs=[ANY], out_specs=[ANY, ANY],
        out_shape=[jax.ShapeDtypeStruct((n, r, c), g.dtype)] * 2,
        scratch_shapes=[pltpu.SemaphoreType.DMA((n,)), pltpu.SemaphoreType.DMA((n,)), pltpu.SemaphoreType.DMA((n,))],
    )(g)


def _scatter_to_chips(s, *, name):
    n, r, c = s.shape

    def body(s_ref, out_ref, send_sems, recv_sems, local_sem):
        x, y, cc = _place()
        me = 2 * x + y
        local = pltpu.make_async_copy(s_ref.at[me], out_ref.at[me], local_sem)
        local.start()
        copies = []
        for j, (px, py) in enumerate(_other_chips(x, y)):
            copies.append(pltpu.make_async_remote_copy(
                src_ref=s_ref.at[2 * px + py], dst_ref=out_ref.at[me], send_sem=send_sems.at[j],
                recv_sem=recv_sems.at[j], device_id=(px, py, cc), device_id_type=MESH))
        for cp in copies:
            cp.start()
        for cp in copies:
            cp.wait()
        local.wait()

    return pl.pallas_call(
        body, name=name, in_specs=[ANY], out_specs=ANY, out_shape=jax.ShapeDtypeStruct((n, r, c), s.dtype),
        scratch_shapes=[pltpu.SemaphoreType.DMA((3,)), pltpu.SemaphoreType.DMA((3,)), pltpu.SemaphoreType.DMA],
    )(s)


def _join_halves(f, *, name):
    r, c = f.shape

    def body(f_ref, out_ref, send_sem, recv_sem, local_sem):
        x, y, cc = _place()
        local = pltpu.make_async_copy(f_ref, out_ref.at[cc], local_sem)
        local.start()
        cp = pltpu.make_async_remote_copy(src_ref=f_ref, dst_ref=out_ref.at[cc], send_sem=send_sem, recv_sem=recv_sem,
                                          device_id=(x, y, 1 - cc), device_id_type=MESH)
        cp.start()
        cp.wait()
        local.wait()

    return pl.pallas_call(
        body, name=name, in_specs=[ANY], out_specs=ANY, out_shape=jax.ShapeDtypeStruct((2, r, c), f.dtype),
        scratch_shapes=[pltpu.SemaphoreType.DMA, pltpu.SemaphoreType.DMA, pltpu.SemaphoreType.DMA],
    )(f)


def _pack_rows(parts, width=PACK_COLS):
    return jnp.concatenate([p.reshape(-1, width) for p in parts], axis=0)


def _pack_small(parts, rows):
    flat = jnp.concatenate([p.reshape(-1).astype(F32) for p in parts])
    return jnp.pad(flat, (0, rows * LANES - flat.shape[0])).reshape(rows, LANES)


def _unpack_small(packed, shapes):
    flat = packed.reshape(-1)
    out, o = [], 0
    for shp in shapes:
        n = int(np.prod(shp))
        out.append(flat[o:o + n].reshape(shp))
        o += n
    return out


def _perm_in_cols(w):
    a, b = Q_LORA_RANK + KV_LORA_RANK, Q_LORA_RANK + KV_LORA_RANK + QK_ROPE_DIM
    c = IN_WIDTH - 2 * N_SSD_HEADS
    return jnp.concatenate([w[:, :a], w[:, b:c], w[:, a:b], w[:, c:]], axis=1)


def _unperm_in_cols(w):
    a = Q_LORA_RANK + KV_LORA_RANK
    zx = D_INNER + XBC_WIDTH
    return jnp.concatenate([w[:, :a], w[:, a + zx:a + zx + QK_ROPE_DIM], w[:, a:a + zx], w[:, a + zx + QK_ROPE_DIM:]],
                           axis=1)


def _rope_tables(seq_len):
    n_rows = seq_len // GRID_W
    row = jnp.repeat(jnp.arange(n_rows), GRID_W).astype(F32)
    col = jnp.tile(jnp.arange(GRID_W), n_rows).astype(F32)
    axis_dim = QK_ROPE_DIM // 2
    inv_freq = ROPE_THETA ** (-jnp.arange(0, axis_dim, 2, dtype=F32) / axis_dim)
    ang_r = row[:, None] * inv_freq
    ang_c = col[:, None] * inv_freq
    ang = jnp.concatenate([ang_r, ang_r, ang_c, ang_c], axis=-1)
    return jnp.cos(ang), jnp.sin(ang)


def _rot_matrix(width, start):
    r = np.zeros((width, width), np.float32)
    quarter = QK_ROPE_DIM // 4
    for base in (0, QK_ROPE_DIM // 2):
        for i in range(quarter):
            r[start + base + quarter + i, start + base + i] = -1.0
            r[start + base + i, start + base + quarter + i] = 1.0
    return jnp.asarray(r)


def _rope_fn(x, cos, sin, rot):
    return (x * cos + _dot_exact(x, rot) * sin,)


def _krdt_fn(x, cos, sin, rot, bias):
    lane = lax.broadcasted_iota(jnp.int32, (1, KRDT_WIDTH), 1)
    is_dt = (lane >= QK_ROPE_DIM) & (lane < QK_ROPE_DIM + 2 * N_SSD_HEADS)
    roped = x * cos + _dot_exact(x, rot) * sin
    return (jnp.where(is_dt, _softplus(x + bias), roped),)


def _pre_fn(u, w, shift, scale):
    return (_rms(u, w) * (1.0 + scale) + shift,)


def _norm_fn(x, w):
    return (_rms(x, w),)


def _finish_fn(yf, yb, xs, z, d_skip, w):
    y = yf + yb + d_skip * xs
    return (_rms(y * _silu(z), w),)


def _mid_fn(x, mix, w_post, w_pre, gate, shift, scale):
    x1 = x + gate * _rms(mix, w_post)
    return (x1, _rms(x1, w_pre) * (1.0 + scale) + shift)


def _loss_fn(x1, ffn, tgt, w_post, gate):
    y = x1 + gate * _rms(ffn, w_post)
    err = y - tgt
    return (0.5 * jnp.mean(err * err, axis=-1, keepdims=True),)


def _bias_fn(x, b):
    return (x + b,)


def _silu_fn(x):
    return (_silu(x),)


def kernel(x, c, ctx, c_ctx, w_mod, b_mod, mix_pre_norm, mix_post_norm, w_in, q_norm, w_q_up, kv_norm, w_kv_up, ssd_conv_w, ssd_conv_b, ssd_a_log, ssd_dt_bias, ssd_d, ssd_norm, w_out, ffn_pre_norm, ffn_post_norm, w_up, ffn_conv_w, ffn_conv_b, w_down, loss_target, m_c_ctx, m_w_mod, m_b_mod, m_mix_pre_norm, m_mix_post_norm, m_w_in, m_q_norm, m_w_q_up, m_kv_norm, m_w_kv_up, m_ssd_conv_w, m_ssd_conv_b, m_ssd_a_log, m_ssd_dt_bias, m_ssd_d, m_ssd_norm, m_w_out, m_ffn_pre_norm, m_ffn_post_norm, m_w_up, m_ffn_conv_w, m_ffn_conv_b, m_w_down, v_c_ctx, v_w_mod, v_b_mod, v_mix_pre_norm, v_mix_post_norm, v_w_in, v_q_norm, v_w_q_up, v_kv_norm, v_w_kv_up, v_ssd_conv_w, v_ssd_conv_b, v_ssd_a_log, v_ssd_dt_bias, v_ssd_d, v_ssd_norm, v_w_out, v_ffn_pre_norm, v_ffn_post_norm, v_w_up, v_ffn_conv_w, v_ffn_conv_b, v_w_down):
    args = dict(locals())
    names = ["c_ctx", "w_mod", "b_mod", "mix_pre_norm", "mix_post_norm", "w_in", "q_norm", "w_q_up", "kv_norm",
             "w_kv_up", "ssd_conv_w", "ssd_conv_b", "ssd_a_log", "ssd_dt_bias", "ssd_d", "ssd_norm", "w_out",
             "ffn_pre_norm", "ffn_post_norm", "w_up", "ffn_conv_w", "ffn_conv_b", "w_down"]
    nb, s, d = x.shape
    nctx_rows = ctx.shape[1]
    lt = nctx_rows + s
    tl = 256 if (nctx_rows % 256 == 0 and s % 256 == 0) else 128
    nctx = nctx_rows // tl
    ncc = nctx_rows // SSD_CHUNK
    h, e, g2 = N_ATTN_HEADS, HEADS_PER_GROUP, SSD_GROUPS
    chip = 2 * lax.axis_index("x") + lax.axis_index("y")

    big_local = {n: args[n][0] for n, _, _, _ in BIG}
    packed = _pack_rows([big_local[n].astype(WIRE_DTYPE) for n, _, _, _ in BIG])
    shard_rows = packed.shape[0]
    gathered = _gather_shards(packed, name="gather_weights")
    full, o = {}, 0
    for n, rows, cols, axis in BIG:
        lr, lc = big_local[n].shape
        nr = lr * lc // PACK_COLS
        seg = gathered[:, o:o + nr].reshape(N_CHIPS, lr, lc)
        o += nr
        full[n] = seg.reshape(rows, cols) if axis == 0 else jnp.transpose(seg, (1, 0, 2)).reshape(rows, cols)
    conv_rows = 32
    conv_all = _allgather_small(_pack_small([ssd_conv_w[0], ffn_conv_w[0]], conv_rows), name="gather_conv")
    conv_all = conv_all[::2]
    n_sc, n_fc = ssd_conv_w.shape[2], ffn_conv_w.shape[2]
    ssd_conv_full = jnp.concatenate(
        [conv_all[j].reshape(-1)[:SSD_CONV * n_sc].reshape(SSD_CONV, n_sc) for j in range(N_CHIPS)], axis=1)
    ffn_conv_full = jnp.concatenate(
        [conv_all[j].reshape(-1)[SSD_CONV * n_sc:SSD_CONV * n_sc + FFN_CONV * n_fc].reshape(FFN_CONV, n_fc)
         for j in range(N_CHIPS)], axis=1)

    w_in_p = _perm_in_cols(full["w_in"])
    o_cq, o_ckv, o_z = 0, Q_LORA_RANK, Q_LORA_RANK + KV_LORA_RANK
    o_xbc, o_kr = o_z + D_INNER, o_z + D_INNER + XBC_WIDTH
    w_krdt = jnp.pad(w_in_p[:, o_kr:], ((0, 0), (0, KRDT_WIDTH - QK_ROPE_DIM - 2 * N_SSD_HEADS)))
    w_in_p = jnp.concatenate([w_in_p[:, :o_kr], w_krdt], axis=1)
    in_p_width = w_in_p.shape[1]

    mod_rows = 16
    c_all = jnp.concatenate([c, c_ctx[None, :], jnp.zeros((mod_rows - nb - 1, d), F32)], axis=0)[None]
    (s_all,) = _row_fwd(_silu_fn, toks=[(c_all, 0, None, 0)], outs=[(d, F32)], nb=1, nl=mod_rows, tl=mod_rows,
                        name="mod_silu")
    mod_lin = _mm(s_all[0], full["w_mod"], name="mod_mm")
    (mod,) = _row_fwd(_bias_fn, toks=[(mod_lin[None], 0, None, 0)], vecs=[b_mod], outs=[(N_MOD * d, F32)], nb=1,
                      nl=mod_rows, tl=mod_rows, name="mod_bias")
    mods = [mod[0][:, k * d:(k + 1) * d][:, None, :] for k in range(N_MOD)]
    mods_lat = [m[:nb] for m in mods]

    u = jnp.concatenate([ctx, x], axis=1)
    (h1,) = _row_fwd(_pre_fn, toks=[(u, 0, None, 0)], vecs=[mix_pre_norm], bvecs=[mods[0], mods[1]],
                     outs=[(d, MXU_DTYPE)], nb=nb, nl=lt, tl=tl, nctx=nctx, name="pre1")
    h1f = h1.reshape(nb * lt, d)
    p_cq = _mm(h1f, w_in_p[:, o_cq:o_ckv], name="in_cq").reshape(nb, lt, -1)
    p_ckv = _mm(h1f, w_in_p[:, o_ckv:o_z], name="in_ckv").reshape(nb, lt, -1)
    p_z = _mm(h1f, w_in_p[:, o_z:o_xbc], name="in_z").reshape(nb, lt, -1)
    p_xbc = _mm(h1f, w_in_p[:, o_xbc:o_kr], name="in_xbc").reshape(nb, lt, -1)
    p_krdt = _mm(h1f, w_in_p[:, o_kr:], name="in_krdt").reshape(nb, lt, -1)

    (cqn,) = _row_fwd(_norm_fn, toks=[(p_cq, nctx, None, 0)], vecs=[q_norm], outs=[(Q_LORA_RANK, MXU_DTYPE)],
                      nb=nb, nl=s, tl=tl, name="q_norm")
    q_flat = _mm(cqn.reshape(nb * s, -1), full["w_q_up"], name="q_up")
    q_heads = jnp.transpose(q_flat.reshape(nb, s, h, QK_DIM), (0, 2, 1, 3)).reshape(nb * h, s, QK_DIM)
    cos, sin = _rope_tables(s)
    ones, zeros = jnp.ones((s, QK_NOPE_DIM), F32), jnp.zeros((s, QK_NOPE_DIM), F32)
    cos_q = jnp.concatenate([ones, cos], axis=1)[None]
    sin_q = jnp.concatenate([zeros, sin], axis=1)[None]
    rot_q = _rot_matrix(QK_DIM, QK_NOPE_DIM)
    tq = 256
    (qh,) = _row_fwd(_rope_fn, toks=[(q_heads, 0, None, 0)], poss=[cos_q, sin_q], vecs=[rot_q],
                     outs=[(QK_DIM, MXU_DTYPE)], nb=nb * h, nl=s, tl=tq, name="rope_q")

    (ckvn,) = _row_fwd(_norm_fn, toks=[(p_ckv, 0, None, 0)], vecs=[kv_norm], outs=[(KV_LORA_RANK, MXU_DTYPE)],
                       nb=nb, nl=lt, tl=tl, name="kv_norm")
    kv_flat = _mm(ckvn.reshape(nb * lt, -1), full["w_kv_up"], out_dtype=MXU_DTYPE, name="kv_up")
    kv_heads = jnp.transpose(kv_flat.reshape(nb, lt, h, QK_NOPE_DIM + V_HEAD_DIM), (0, 2, 1, 3))
    k_nope, vh = kv_heads[..., :QK_NOPE_DIM], kv_heads[..., QK_NOPE_DIM:].reshape(nb * h, lt, V_HEAD_DIM)

    pad_w = KRDT_WIDTH - QK_ROPE_DIM
    cos_k = jnp.concatenate([jnp.ones((nctx_rows, KRDT_WIDTH), F32),
                             jnp.concatenate([cos, jnp.ones((s, pad_w), F32)], axis=1)], axis=0)[None]
    sin_k = jnp.concatenate([jnp.zeros((nctx_rows, KRDT_WIDTH), F32),
                             jnp.concatenate([sin, jnp.zeros((s, pad_w), F32)], axis=1)], axis=0)[None]
    rot_k = _rot_matrix(KRDT_WIDTH, 0)
    dt_bias_row = jnp.pad(ssd_dt_bias.reshape(1, -1), ((0, 0), (QK_ROPE_DIM, pad_w - 2 * N_SSD_HEADS)))
    (krdt,) = _row_fwd(_krdt_fn, toks=[(p_krdt, 0, None, 0)], poss=[cos_k, sin_k], vecs=[rot_k, dt_bias_row],
                       outs=[(KRDT_WIDTH, F32)], nb=nb, nl=lt, tl=tl, name="krdt")
    k_rope = krdt[..., :QK_ROPE_DIM].astype(MXU_DTYPE)
    kh = jnp.concatenate([k_nope, jnp.broadcast_to(k_rope[:, None], (nb, h, lt, QK_ROPE_DIM))], axis=-1)
    kh = kh.reshape(nb * h, lt, QK_DIM)
    oh = _attn_fwd(qh, kh, vh, tq=tq, name="attn_fwd")
    attn = jnp.transpose(oh.reshape(nb, h, s, V_HEAD_DIM), (0, 2, 1, 3)).reshape(nb, s, ATTN_WIDTH)

    seg = nctx_rows

    def conv_ssd_fn(xv, w, b):
        return (_silu(_dwconv(xv, w, seg) + b),)

    cb_ssd = 256
    (xbc,) = _seq_fwd(conv_ssd_fn, toks=[(p_xbc, 0)], vecs=[(ssd_conv_full, 0), (ssd_conv_b, 0)], outs=[F32], nb=nb,
                      nl=lt, nc=XBC_WIDTH, cb=cb_ssd, name="conv_ssd")
    dt = krdt[..., QK_ROPE_DIM:QK_ROPE_DIM + 2 * N_SSD_HEADS].reshape(nb, lt, 2, g2, e)
    dtc = jnp.transpose(dt, (2, 0, 3, 1, 4))
    dtr = jnp.transpose(dt, (2, 0, 3, 4, 1))
    a_neg = -jnp.exp(ssd_a_log[0]).reshape(2, g2, e)
    ys, ents = [], []
    for dr in range(2):
        y_d, ent_d = _scan_fwd(xbc, dtc[dr], dtr[dr], a_neg[dr][:, None, :], a_neg[dr][:, :, None], ncc=ncc,
                               reverse=bool(dr), name=f"scan_fwd{dr}")
        ys.append(y_d)
        ents.append(ent_d)
    d_row = jnp.repeat(ssd_d[0], SSD_HEAD_DIM)[None, :]
    finish_toks = [(ys[0], nctx, None, 0), (ys[1], nctx, None, 0), (xbc, nctx, D_INNER, 0), (p_z, nctx, None, 0)]
    (ssd,) = _row_fwd(_finish_fn, toks=finish_toks, vecs=[d_row, ssd_norm], outs=[(D_INNER, MXU_DTYPE)], nb=nb,
                      nl=s, tl=tl, name="ssd_finish")

    cat = jnp.concatenate([attn, ssd], axis=-1).reshape(nb * s, ATTN_WIDTH + D_INNER)
    mix = _mm(cat, full["w_out"], name="out_proj").reshape(nb, s, d)

    mid_bvecs = [mods_lat[2], mods_lat[3], mods_lat[4]]
    x1, h2 = _row_fwd(_mid_fn, toks=[(x, 0, None, 0), (mix, 0, None, 0)], vecs=[mix_post_norm, ffn_pre_norm],
                      bvecs=mid_bvecs, outs=[(d, F32), (d, MXU_DTYPE)], nb=nb, nl=s, tl=tl, name="mid")
    up = _mm(h2.reshape(nb * s, d), full["w_up"], name="ffn_up").reshape(nb, s, 2 * D_FF)

    def glu_fn(gate, val, w, b):
        return (_gelu(_dwconv(gate, w, 0) + b) * val,)

    cb_ffn = 256
    glu_toks = [(up, 0), (up, D_FF // cb_ffn)]
    glu_vecs = [(ffn_conv_full, 0), (ffn_conv_b, 0)]
    (act,) = _seq_fwd(glu_fn, toks=glu_toks, vecs=glu_vecs, outs=[MXU_DTYPE], nb=nb, nl=s, nc=D_FF, cb=cb_ffn,
                      name="conv_glu")
    ffn = _mm(act.reshape(nb * s, D_FF), full["w_down"], name="ffn_down").reshape(nb, s, d)

    loss_toks = [(x1, 0, None, 0), (ffn, 0, None, 0), (loss_target, 0, None, 0)]
    ones_rows = jnp.ones((nb, s, 1), F32)
    (dx1_a, dffn, _), (g_ffn_post,), ((g_gate5, _),), (loss_rows,) = _row_bwd(
        _loss_fn, toks=loss_toks, vecs=[ffn_post_norm], bvecs=[mods_lat[5]], cots=[(ones_rows, 0)],
        tok_grads=[F32, MXU_DTYPE, None], emit=[(0, 1, F32)], nb=nb, nl=s, tl=tl, name="loss_bwd")
    loss_part = jnp.sum(loss_rows)

    dffn_f = dffn.reshape(nb * s, d)
    g_w_down = _mm(act.reshape(nb * s, D_FF), dffn_f, ta=True, name="wg_down")
    dact = _mm(dffn_f, full["w_down"], tb=True, out_dtype=MXU_DTYPE, name="dg_down").reshape(nb, s, D_FF)
    (dgate, dval), (g_ffn_conv_w, g_ffn_conv_b) = _seq_bwd(
        glu_fn, toks=glu_toks, vecs=glu_vecs, cots=[dact], tok_grads=[MXU_DTYPE, MXU_DTYPE], nb=nb, nl=s, nc=D_FF,
        cb=cb_ffn, name="conv_glu_bwd")
    dup = jnp.concatenate([dgate, dval], axis=-1).reshape(nb * s, 2 * D_FF)
    g_w_up = _mm(h2.reshape(nb * s, d), dup, ta=True, name="wg_up")
    dh2 = _mm(dup, full["w_up"], tb=True, name="dg_up").reshape(nb, s, d)

    (dx_res, dmix), (g_mix_post, g_ffn_pre), ((g_gate2, _), (g_shift3, _), (g_scale4, _)), _ = _row_bwd(
        _mid_fn, toks=[(x, 0, None, 0), (mix, 0, None, 0)], vecs=[mix_post_norm, ffn_pre_norm], bvecs=mid_bvecs,
        cots=[(dx1_a, 0), (dh2, 0)], tok_grads=[F32, MXU_DTYPE], nb=nb, nl=s, tl=tl, name="mid_bwd")

    dmix_f = dmix.reshape(nb * s, d)
    g_w_out = _mm(cat, dmix_f, ta=True, name="wg_out")
    dcat = _mm(dmix_f, full["w_out"], tb=True, name="dg_out").reshape(nb, s, ATTN_WIDTH + D_INNER)
    dattn, dssd = dcat[..., :ATTN_WIDTH], dcat[..., ATTN_WIDTH:]

    (dyf, _, dxs_skip, dz), (g_d_row, g_ssd_norm), _, _ = _row_bwd(
        _finish_fn, toks=finish_toks, vecs=[d_row, ssd_norm], cots=[(dssd, 0)], tok_grads=[F32, None, F32, MXU_DTYPE],
        nb=nb, nl=s, tl=tl, name="ssd_finish_bwd")
    g_xs = jnp.pad(dxs_skip, ((0, 0), (nctx_rows, 0), (0, 0)))
    g_bm = g_cm = None
    g_dt_dirs, g_a = [], []
    for dr in range(2):
        gx, gb, gc, gdtc, gdtr, gac, gar = _scan_bwd(
            xbc, dtc[dr], dtr[dr], a_neg[dr][:, None, :], a_neg[dr][:, :, None], ents[dr], dyf, ncc=ncc,
            reverse=bool(dr), name=f"scan_bwd{dr}")
        g_xs = g_xs + gx
        g_bm = gb if g_bm is None else g_bm + gb
        g_cm = gc if g_cm is None else g_cm + gc
        g_dt_dirs.append(jnp.transpose(gdtc, (0, 2, 1, 3)) + jnp.transpose(gdtr, (0, 3, 1, 2)))
        g_a.append(jnp.sum(gac[:, :, 0, :] + gar[:, :, :, 0], axis=0))
    g_a_log = (jnp.stack(g_a) * a_neg).reshape(1, 2, N_SSD_HEADS)
    g_dt = jnp.stack(g_dt_dirs, axis=2).reshape(nb, lt, 2 * N_SSD_HEADS)
    g_xbc_act = jnp.concatenate([g_xs, g_bm, g_cm], axis=-1)
    (dp_xbc,), (g_ssd_conv_w, g_ssd_conv_b) = _seq_bwd(
        conv_ssd_fn, toks=[(p_xbc, 0)], vecs=[(ssd_conv_full, 0), (ssd_conv_b, 0)], cots=[g_xbc_act],
        tok_grads=[MXU_DTYPE], nb=nb, nl=lt, nc=XBC_WIDTH, cb=cb_ssd, name="conv_ssd_bwd")

    doh = jnp.transpose(dattn.reshape(nb, s, h, V_HEAD_DIM), (0, 2, 1, 3)).reshape(nb * h, s, V_HEAD_DIM)
    dqh, dkh, dvh = _attn_bwd(qh, kh, vh, doh, tq=tq, name="attn_bwd")
    (dq_heads,), _, _, _ = _row_bwd(_rope_fn, toks=[(q_heads, 0, None, 0)], poss=[cos_q, sin_q], vecs=[rot_q],
                                    cots=[(dqh, 0)], tok_grads=[MXU_DTYPE], nb=nb * h, nl=s, tl=tq, name="rope_q_bwd")
    dq_flat = jnp.transpose(dq_heads.reshape(nb, h, s, QK_DIM), (0, 2, 1, 3)).reshape(nb * s, h * QK_DIM)
    g_w_q_up = _mm(cqn.reshape(nb * s, -1), dq_flat, ta=True, name="wg_q_up")
    dcqn = _mm(dq_flat, full["w_q_up"], tb=True, name="dg_q_up").reshape(nb, s, Q_LORA_RANK)
    (dp_cq,), (g_q_norm,), _, _ = _row_bwd(_norm_fn, toks=[(p_cq, nctx, None, 0)], vecs=[q_norm], cots=[(dcqn, 0)],
                                           tok_grads=[MXU_DTYPE], nb=nb, nl=s, tl=tl, name="q_norm_bwd")

    dkh = dkh.reshape(nb, h, lt, QK_DIM)
    dkv_heads = jnp.concatenate([dkh[..., :QK_NOPE_DIM], dvh.reshape(nb, h, lt, V_HEAD_DIM)], axis=-1)
    dkv_flat = jnp.transpose(dkv_heads, (0, 2, 1, 3)).reshape(nb * lt, -1).astype(MXU_DTYPE)
    g_w_kv_up = _mm(ckvn.reshape(nb * lt, -1), dkv_flat, ta=True, name="wg_kv_up")
    dckvn = _mm(dkv_flat, full["w_kv_up"], tb=True, name="dg_kv_up").reshape(nb, lt, KV_LORA_RANK)
    (dp_ckv,), (g_kv_norm,), _, _ = _row_bwd(_norm_fn, toks=[(p_ckv, 0, None, 0)], vecs=[kv_norm], cots=[(dckvn, 0)],
                                             tok_grads=[MXU_DTYPE], nb=nb, nl=lt, tl=tl, name="kv_norm_bwd")

    dk_rope = jnp.sum(dkh[..., QK_NOPE_DIM:], axis=1)
    g_krdt = jnp.concatenate([dk_rope, g_dt, jnp.zeros((nb, lt, pad_w - 2 * N_SSD_HEADS), F32)], axis=-1)
    (dp_krdt,), (_, g_dt_bias_row), _, _ = _row_bwd(
        _krdt_fn, toks=[(p_krdt, 0, None, 0)], poss=[cos_k, sin_k], vecs=[rot_k, dt_bias_row], cots=[(g_krdt, 0)],
        tok_grads=[MXU_DTYPE], nb=nb, nl=lt, tl=tl, name="krdt_bwd")

    zero_ctx = lambda width: jnp.zeros((nb, nctx_rows, width), MXU_DTYPE)
    dp = jnp.concatenate([jnp.concatenate([zero_ctx(Q_LORA_RANK), dp_cq], axis=1), dp_ckv,
                          jnp.concatenate([zero_ctx(D_INNER), dz], axis=1), dp_xbc, dp_krdt], axis=-1)
    dp_f = dp.reshape(nb * lt, in_p_width)
    g_w_in_p = _mm(h1f, dp_f, ta=True, name="wg_in")
    dh1 = _mm(dp_f, w_in_p, tb=True, name="dg_in").reshape(nb, lt, d)
    (du,), (g_mix_pre,), ((g_shift0, g_shift0c), (g_scale1, g_scale1c)), _ = _row_bwd(
        _pre_fn, toks=[(u, 0, None, 0)], vecs=[mix_pre_norm], bvecs=[mods[0], mods[1]], cots=[(dh1, 0)],
        tok_grads=[F32], nb=nb, nl=lt, tl=tl, nctx=nctx, name="pre1_bwd")
    grad_x = dx_res + du[:, nctx_rows:]

    zero_row = jnp.zeros((1, 1, d), F32)
    lat = [g_shift0, g_scale1, g_gate2, g_shift3, g_scale4, g_gate5]
    ctxg = [g_shift0c, g_scale1c, zero_row, zero_row, zero_row, zero_row]
    dmod = jnp.concatenate([jnp.concatenate([a, b], axis=0)[:, 0, :] for a, b in zip(lat, ctxg)], axis=-1)
    dmod = jnp.pad(dmod, ((0, mod_rows - nb - 1), (0, 0)))
    _, (g_b_mod,), _, _ = _row_bwd(_bias_fn, toks=[(mod_lin[None], 0, None, 0)], vecs=[b_mod], cots=[(dmod[None], 0)],
                                   tok_grads=[None], nb=1, nl=mod_rows, tl=mod_rows, name="mod_bias_bwd")
    g_w_mod = _mm(s_all[0], dmod, ta=True, name="wg_mod")
    ds_all = _mm(dmod, full["w_mod"], tb=True, name="dg_mod")
    (dc_all,), _, _, _ = _row_bwd(_silu_fn, toks=[(c_all, 0, None, 0)], cots=[(ds_all[None], 0)], tok_grads=[F32],
                                  nb=1, nl=mod_rows, tl=mod_rows, name="mod_silu_bwd")
    g_c_ctx = dc_all[0, nb]

    g_w_in = _unperm_in_cols(g_w_in_p[:, :IN_WIDTH])
    big_grads = {"w_mod": g_w_mod, "w_in": g_w_in, "w_q_up": g_w_q_up, "w_kv_up": g_w_kv_up, "w_out": g_w_out,
                 "w_up": g_w_up, "w_down": g_w_down}
    parts = []
    for n, rows, cols, axis in BIG:
        lr, lc = big_local[n].shape
        gfull = big_grads[n]
        shards = gfull.reshape(N_CHIPS, lr, lc) if axis == 0 else jnp.transpose(gfull.reshape(lr, N_CHIPS, lc), (1, 0, 2))
        parts.append(shards.reshape(N_CHIPS, lr * lc // PACK_COLS, PACK_COLS))
    gpack = jnp.concatenate(parts, axis=1).astype(WIRE_DTYPE)
    half = shard_rows // 2
    own, got = _swap_halves(gpack.reshape(N_CHIPS, 2, half, PACK_COLS), name="grad_swap")
    chip_sum = _sum_slots(jnp.stack([own, got]).reshape(2, N_CHIPS * half, PACK_COLS), out_dtype=WIRE_DTYPE,
                          name="grad_add_pair").reshape(N_CHIPS, half, PACK_COLS)
    from_chips = _scatter_to_chips(chip_sum, name="grad_scatter")
    my_half = _sum_slots(from_chips, out_dtype=F32, name="grad_add_chips")
    g_shard = _join_halves(my_half, name="grad_join").reshape(shard_rows, PACK_COLS)

    g_d = jnp.sum(g_d_row.reshape(N_SSD_HEADS, SSD_HEAD_DIM), axis=1)[None]
    g_dt_bias = g_dt_bias_row[:, QK_ROPE_DIM:QK_ROPE_DIM + 2 * N_SSD_HEADS].reshape(1, 2, N_SSD_HEADS)
    small_names = ["c_ctx", "b_mod", "mix_pre_norm", "mix_post_norm", "q_norm", "kv_norm", "ssd_conv_w", "ssd_conv_b",
                   "ssd_a_log", "ssd_dt_bias", "ssd_d", "ssd_norm", "ffn_pre_norm", "ffn_post_norm", "ffn_conv_w",
                   "ffn_conv_b"]
    small_grads = [g_c_ctx, g_b_mod, g_mix_pre, g_mix_post, g_q_norm, g_kv_norm, g_ssd_conv_w, g_ssd_conv_b,
                   g_a_log, g_dt_bias, g_d, g_ssd_norm, g_ffn_pre, g_ffn_post, g_ffn_conv_w, g_ffn_conv_b]
    small_shapes = [tuple(np.shape(a)) for a in small_grads] + [()]
    n_small = sum(int(np.prod(shp)) for shp in small_shapes)
    small_rows = -(-n_small // (8 * LANES)) * 8
    small_all = _allgather_small(_pack_small(small_grads + [loss_part], small_rows), name="gather_small")
    small_sum = _sum_slots(small_all, out_dtype=F32, name="small_add")
    small_red = _unpack_small(small_sum, small_shapes)
    loss = small_red[-1]
    grads = dict(zip(small_names, small_red[:-1]))
    grads["ssd_conv_w"] = lax.dynamic_slice_in_dim(grads["ssd_conv_w"], chip * n_sc, n_sc, axis=1)[None]
    grads["ffn_conv_w"] = lax.dynamic_slice_in_dim(grads["ffn_conv_w"], chip * n_fc, n_fc, axis=1)[None]
    for n in small_names:
        grads[n] = grads[n].reshape(args[n].shape)

    delta, new_m, new_v = {}, {}, {}
    o = 0
    for n, _, _, _ in BIG:
        lr, lc = big_local[n].shape
        nr = lr * lc // PACK_COLS
        grads[n] = g_shard[o:o + nr].reshape(1, lr, lc)
        o += nr
        dl, nm, nv = _adamw(big_local[n], grads[n][0], args["m_" + n][0], args["v_" + n][0], name="adamw_" + n)
        delta[n], new_m[n], new_v[n] = dl[None], nm[None], nv[None]
    sm_shapes = [args[n].shape for n in small_names]
    n_sm = sum(int(np.prod(shp)) for shp in sm_shapes)
    sm_rows = -(-n_sm // (8 * LANES)) * 8
    packs = [_pack_small([src[n] for n in small_names], sm_rows)
             for src in (args, grads, {n: args["m_" + n] for n in small_names}, {n: args["v_" + n] for n in small_names})]
    for out_dict, packed_out in zip((delta, new_m, new_v), _adamw(*packs, name="adamw_small")):
        out_dict.update(zip(small_names, _unpack_small(packed_out, sm_shapes)))

    return (loss, grad_x, *[grads[n] for n in names], *[delta[n] for n in names], *[new_m[n] for n in names],
            *[new_v[n] for n in names])
```

```python
import functools
import math

import numpy as np
import jax
import jax.numpy as jnp
from jax import lax
from jax.experimental import pallas as pl
from jax.experimental.pallas import tpu as pltpu

F32 = jnp.float32
MXU_DTYPE = jnp.bfloat16
WIRE_DTYPE = jnp.bfloat16
VMEM_LIMIT_BYTES = 56 * 1024 * 1024
HIGHEST = lax.Precision.HIGHEST

D_MODEL = 1024
N_MOD = 6
EPS = 1e-6
GRID_W = 64
N_ATTN_HEADS = 16
QK_NOPE_DIM = 64
QK_ROPE_DIM = 32
QK_DIM = QK_NOPE_DIM + QK_ROPE_DIM
V_HEAD_DIM = 64
Q_LORA_RANK = 384
KV_LORA_RANK = 256
ROPE_THETA = 10000.0
ATTN_SCALE = QK_DIM ** -0.5
ATTN_WIDTH = N_ATTN_HEADS * V_HEAD_DIM
N_SSD_HEADS = 16
SSD_HEAD_DIM = 64
SSD_GROUPS = 2
HEADS_PER_GROUP = N_SSD_HEADS // SSD_GROUPS
SSD_STATE = 128
SSD_CONV = 5
SSD_CHUNK = 128
D_INNER = N_SSD_HEADS * SSD_HEAD_DIM
GN = SSD_GROUPS * SSD_STATE
XBC_WIDTH = D_INNER + 2 * GN
D_FF = 2816
FFN_CONV = 3
KRDT_WIDTH = 128
IN_WIDTH = Q_LORA_RANK + KV_LORA_RANK + QK_ROPE_DIM + D_INNER + XBC_WIDTH + 2 * N_SSD_HEADS

ADAM_LR = 0.001
ADAM_B1 = 0.9
ADAM_B2 = 0.999
ADAM_EPS = 1e-08
ADAM_WD = 0.01
ADAM_STEP = 10

N_CHIPS = 4
N_DEV = 8
MESH = pl.DeviceIdType.MESH
LANES = 128

BIG = (("w_mod", D_MODEL, N_MOD * D_MODEL, 1), ("w_in", D_MODEL, IN_WIDTH, 1),
       ("w_q_up", Q_LORA_RANK, N_ATTN_HEADS * QK_DIM, 1),
       ("w_kv_up", KV_LORA_RANK, N_ATTN_HEADS * (QK_NOPE_DIM + V_HEAD_DIM), 1),
       ("w_out", ATTN_WIDTH + D_INNER, D_MODEL, 0), ("w_up", D_MODEL, 2 * D_FF, 1),
       ("w_down", D_FF, D_MODEL, 0))
PACK_COLS = 1024


def _cparams(sem):
    return pltpu.CompilerParams(dimension_semantics=sem, vmem_limit_bytes=VMEM_LIMIT_BYTES)


def _pick(n, cands):
    for c in cands:
        if n % c == 0:
            return c
    return n


def _sigmoid(x):
    return 0.5 * (jnp.tanh(0.5 * x) + 1.0)


def _silu(x):
    return x * _sigmoid(x)


@jax.custom_vjp
def _softplus(x):
    u = jnp.exp(-jnp.abs(x))
    w = 1.0 + u
    log1p = jnp.where(w == 1.0, u, jnp.log(w) * (u / jnp.where(w == 1.0, 1.0, w - 1.0)))
    return jnp.maximum(x, 0.0) + log1p


def _softplus_fwd(x):
    return _softplus(x), x


def _softplus_bwd(x, g):
    return (g * _sigmoid(x),)


_softplus.defvjp(_softplus_fwd, _softplus_bwd)


@jax.custom_vjp
def _gelu(x):
    return 0.5 * x * (1.0 + lax.erf(x * (2.0 ** -0.5)))


def _gelu_fwd(x):
    return _gelu(x), x


def _gelu_bwd(x, g):
    cdf = 0.5 * (1.0 + lax.erf(x * (2.0 ** -0.5)))
    pdf = jnp.exp(-0.5 * x * x) * (1.0 / math.sqrt(2.0 * math.pi))
    return (g * (cdf + x * pdf),)


_gelu.defvjp(_gelu_fwd, _gelu_bwd)


def _rms(x, w):
    return x * lax.rsqrt(jnp.mean(x * x, axis=-1, keepdims=True) + EPS) * w


def _shift_rows_raw(x, off, seg):
    n = x.shape[0]
    if off == 0:
        return x
    r = pltpu.roll(x, (-off) % n, 0)
    idx = lax.broadcasted_iota(jnp.int32, x.shape, 0)
    src = idx + off
    ok = (src >= 0) & (src < n)
    if seg:
        ok = ok & ((idx < seg) == (src < seg))
    return jnp.where(ok, r, 0.0)


@functools.partial(jax.custom_vjp, nondiff_argnums=(1, 2))
def _shift_rows(x, off, seg):
    return _shift_rows_raw(x, off, seg)


def _shift_rows_fwd(x, off, seg):
    return _shift_rows_raw(x, off, seg), None


def _shift_rows_bwd(off, seg, _, g):
    return (_shift_rows_raw(g, -off, seg),)


_shift_rows.defvjp(_shift_rows_fwd, _shift_rows_bwd)


@functools.partial(jax.custom_vjp, nondiff_argnums=(1,))
def _roll_lanes(x, shift):
    return pltpu.roll(x, shift % x.shape[1], 1)


def _roll_lanes_fwd(x, shift):
    return _roll_lanes(x, shift), None


def _roll_lanes_bwd(shift, _, g):
    return (pltpu.roll(g, (-shift) % g.shape[1], 1),)


_roll_lanes.defvjp(_roll_lanes_fwd, _roll_lanes_bwd)


def _row_of(w, k):
    sel = lax.broadcasted_iota(jnp.int32, (w.shape[0], 1), 0) == k
    return jnp.sum(jnp.where(sel, w, 0.0), axis=0, keepdims=True)


def _col_of(w, k):
    sel = lax.broadcasted_iota(jnp.int32, (1, w.shape[1]), 1) == k
    return jnp.sum(jnp.where(sel, w, 0.0), axis=1, keepdims=True)


def _dwconv(x, w, seg):
    k = w.shape[0]
    acc = None
    for t in range(k):
        term = _shift_rows(x, t - k // 2, seg) * _row_of(w, t)
        acc = term if acc is None else acc + term
    return acc


def _dot(a, b, dims):
    return lax.dot_general(a.astype(MXU_DTYPE), b.astype(MXU_DTYPE), (dims, ((), ())),
                           preferred_element_type=F32)


def _dot_exact(a, b):
    return lax.dot_general(a, b, (((1,), (0,)), ((), ())), precision=HIGHEST,
                           preferred_element_type=F32)


def _mm(a, b, *, ta=False, tb=False, out_dtype=F32, name):
    if ta:
        kdim, m = a.shape
    else:
        m, kdim = a.shape
    if tb:
        n, k2 = b.shape
    else:
        k2, n = b.shape
    assert kdim == k2, (a.shape, b.shape, ta, tb)
    tm = _pick(m, (1024, 512, 384, 256, 128))
    tn = _pick(n, (512, 384, 256, 128))
    tk = kdim if kdim <= 2048 else _pick(kdim, (2048, 1664, 1536, 1408, 1024, 512, 256, 128))
    nk = kdim // tk
    a_spec = pl.BlockSpec((tk, tm), lambda i, j, k: (k, i)) if ta else pl.BlockSpec((tm, tk), lambda i, j, k: (i, k))
    b_spec = pl.BlockSpec((tn, tk), lambda i, j, k: (j, k)) if tb else pl.BlockSpec((tk, tn), lambda i, j, k: (k, j))
    dims = ((0,) if ta else (1,), (1,) if tb else (0,))

    def body(a_ref, b_ref, o_ref, *scratch):
        if nk == 1:
            o_ref[...] = _dot(a_ref[...], b_ref[...], dims).astype(o_ref.dtype)
            return
        acc_ref, = scratch
        k = pl.program_id(2)

        @pl.when(k == 0)
        def _():
            acc_ref[...] = jnp.zeros_like(acc_ref)

        acc_ref[...] += _dot(a_ref[...], b_ref[...], dims)

        @pl.when(k == nk - 1)
        def _():
            o_ref[...] = acc_ref[...].astype(o_ref.dtype)

    return pl.pallas_call(
        body, name=name, grid=(m // tm, n // tn, nk),
        in_specs=[a_spec, b_spec], out_specs=pl.BlockSpec((tm, tn), lambda i, j, k: (i, j)),
        out_shape=jax.ShapeDtypeStruct((m, n), out_dtype),
        scratch_shapes=[pltpu.VMEM((tm, tn), F32)] if nk > 1 else [],
        compiler_params=_cparams(("parallel", "parallel", "arbitrary")),
    )(a, b)


def _row_specs(toks, poss, vecs, bvecs, tl, nctx, nb):
    specs, args = [], []
    for arr, off, cw, ci in toks:
        cw = arr.shape[2] if cw is None else cw
        specs.append(pl.BlockSpec((1, tl, cw), lambda b, l, off=off, ci=ci: (b, l + off, ci)))
        args.append(arr)
    for arr in poss:
        specs.append(pl.BlockSpec((1, tl, arr.shape[2]), lambda b, l: (0, l, 0)))
        args.append(arr)
    for arr in vecs:
        specs.append(pl.BlockSpec(arr.shape, lambda b, l: (0, 0)))
        args.append(arr)
    for arr in bvecs:
        if nctx:
            specs.append(pl.BlockSpec((1, 1, arr.shape[2]), lambda b, l: (jnp.where(l < nctx, nb, b), 0, 0)))
        else:
            specs.append(pl.BlockSpec((1, 1, arr.shape[2]), lambda b, l: (b, 0, 0)))
        args.append(arr)
    return specs, args


def _row_fwd(fn, *, toks, poss=(), vecs=(), bvecs=(), outs, nb, nl, tl, nctx=0, name):
    nt, npos, nv, nbv = len(toks), len(poss), len(vecs), len(bvecs)
    specs, args = _row_specs(toks, poss, vecs, bvecs, tl, nctx, nb)

    def body(*refs):
        ins, os = refs[:len(specs)], refs[len(specs):]
        tv = [r[0].astype(F32) for r in ins[:nt]]
        pv = [r[0] for r in ins[nt:nt + npos]]
        vv = [r[...] for r in ins[nt + npos:nt + npos + nv]]
        bv = [r[0] for r in ins[nt + npos + nv:]]
        res = fn(*tv, *pv, *vv, *bv)
        for o, r in zip(os, res):
            o[0] = r.astype(o.dtype)

    return pl.pallas_call(
        body, name=name, grid=(nb, nl // tl), in_specs=specs,
        out_specs=[pl.BlockSpec((1, tl, c), lambda b, l: (b, l, 0)) for c, _ in outs],
        out_shape=[jax.ShapeDtypeStruct((nb, nl, c), dt) for c, dt in outs],
        compiler_params=_cparams(("parallel", "parallel")),
    )(*args)


def _row_bwd(fn, *, toks, poss=(), vecs=(), bvecs=(), cots, tok_grads, emit=(), nb, nl, tl, nctx=0, name):
    nt, npos, nv, nbv = len(toks), len(poss), len(vecs), len(bvecs)
    specs, args = _row_specs(toks, poss, vecs, bvecs, tl, nctx, nb)
    n_in = len(specs)
    cot_slots = []
    for arr, off in cots:
        if arr is None:
            cot_slots.append(None)
            continue
        cot_slots.append((len(specs), off))
        specs.append(pl.BlockSpec((1, tl, arr.shape[2]), lambda b, l, off=off: (b, jnp.maximum(l + off, 0), 0)))
        args.append(arr)
    n_all_in = len(specs)

    out_specs, out_shapes = [], []
    tok_out = []
    for (arr, off, cw, ci), dt in zip(toks, tok_grads):
        if dt is None:
            tok_out.append(None)
            continue
        cw = arr.shape[2] if cw is None else cw
        tok_out.append(len(out_specs))
        out_specs.append(pl.BlockSpec((1, tl, cw), lambda b, l: (b, l, 0)))
        out_shapes.append(jax.ShapeDtypeStruct((nb, nl, cw), dt))
    vec_out = []
    for arr in vecs:
        vec_out.append(len(out_specs))
        out_specs.append(pl.BlockSpec(arr.shape, lambda b, l: (0, 0)))
        out_shapes.append(jax.ShapeDtypeStruct(arr.shape, F32))
    bv_out = []
    for arr in bvecs:
        c = arr.shape[2]
        lat = len(out_specs)
        out_specs.append(pl.BlockSpec((1, 1, c), lambda b, l: (b, 0, 0)))
        out_shapes.append(jax.ShapeDtypeStruct((nb, 1, c), F32))
        ctx = None
        if nctx:
            ctx = len(out_specs)
            out_specs.append(pl.BlockSpec((1, 1, c), lambda b, l: (0, 0, 0)))
            out_shapes.append(jax.ShapeDtypeStruct((1, 1, c), F32))
        bv_out.append((lat, ctx))
    emit_out = []
    emit_cols = {}
    for idx, c, dt in emit:
        emit_out.append((idx, len(out_specs)))
        out_specs.append(pl.BlockSpec((1, tl, c), lambda b, l: (b, l, 0)))
        out_shapes.append(jax.ShapeDtypeStruct((nb, nl, c), dt))

    def body(*refs):
        ins, os = refs[:n_all_in], refs[n_all_in:]
        b, l = pl.program_id(0), pl.program_id(1)
        tv = [r[0].astype(F32) for r in ins[:nt]]
        pv = [r[0] for r in ins[nt:nt + npos]]
        vv = [r[...] for r in ins[nt + npos:nt + npos + nv]]
        bv = [r[0] for r in ins[nt + npos + nv:n_in]]

        def f(*d):
            return tuple(fn(*d[:nt], *pv, *d[nt:]))

        res, vjp = jax.vjp(f, *tv, *vv, *bv)
        cts = []
        for r, slot in zip(res, cot_slots):
            if slot is None:
                cts.append(jnp.zeros_like(r))
            else:
                i, off = slot
                ct = ins[i][0].astype(F32)
                if off < 0:
                    ct = jnp.where(l + off >= 0, ct, 0.0)
                cts.append(ct)
        grads = vjp(tuple(cts))

        for g, slot in zip(grads[:nt], tok_out):
            if slot is not None:
                os[slot][0] = g.astype(os[slot].dtype)

        @pl.when((b == 0) & (l == 0))
        def _():
            for slot in vec_out:
                os[slot][...] = jnp.zeros_like(os[slot])
            for _, ctx in bv_out:
                if ctx is not None:
                    os[ctx][...] = jnp.zeros_like(os[ctx])

        @pl.when(l == 0)
        def _():
            for lat, _ in bv_out:
                os[lat][...] = jnp.zeros_like(os[lat])

        for g, slot in zip(grads[nt:nt + nv], vec_out):
            os[slot][...] += g
        for g, (lat, ctx) in zip(grads[nt + nv:], bv_out):
            if ctx is None:
                os[lat][0] += g
            else:
                is_ctx = l < nctx
                os[lat][0] += jnp.where(is_ctx, 0.0, g)
                os[ctx][0] += jnp.where(is_ctx, g, 0.0)
        for idx, slot in emit_out:
            os[slot][0] = res[idx].astype(os[slot].dtype)

    out = pl.pallas_call(
        body, name=name, grid=(nb, nl // tl), in_specs=specs, out_specs=out_specs, out_shape=out_shapes,
        compiler_params=_cparams(("arbitrary", "arbitrary")),
    )(*args)
    tg = [None if s is None else out[s] for s in tok_out]
    vg = [out[s] for s in vec_out]
    bg = [(out[lat], None if ctx is None else out[ctx]) for lat, ctx in bv_out]
    em = [out[s] for _, s in emit_out]
    return tg, vg, bg, em


def _seq_specs(toks, vecs, nl, cb):
    specs, args = [], []
    for arr, off in toks:
        specs.append(pl.BlockSpec((1, nl, cb), lambda j, b, off=off: (b, 0, j + off)))
        args.append(arr)
    for arr, off in vecs:
        specs.append(pl.BlockSpec((arr.shape[0], cb), lambda j, b, off=off: (0, j + off)))
        args.append(arr)
    return specs, args


def _seq_fwd(fn, *, toks, vecs, outs, nb, nl, nc, cb, name):
    nt = len(toks)
    specs, args = _seq_specs(toks, vecs, nl, cb)

    def body(*refs):
        ins, os = refs[:len(specs)], refs[len(specs):]
        tv = [r[0].astype(F32) for r in ins[:nt]]
        vv = [r[...] for r in ins[nt:]]
        for o, r in zip(os, fn(*tv, *vv)):
            o[0] = r.astype(o.dtype)

    return pl.pallas_call(
        body, name=name, grid=(nc // cb, nb), in_specs=specs,
        out_specs=[pl.BlockSpec((1, nl, cb), lambda j, b: (b, 0, j)) for _ in outs],
        out_shape=[jax.ShapeDtypeStruct((nb, nl, nc), dt) for dt in outs],
        compiler_params=_cparams(("parallel", "parallel")),
    )(*args)


def _seq_bwd(fn, *, toks, vecs, cots, tok_grads, nb, nl, nc, cb, name):
    nt, nv = len(toks), len(vecs)
    specs, args = _seq_specs(toks, vecs, nl, cb)
    n_in = len(specs)
    for arr in cots:
        specs.append(pl.BlockSpec((1, nl, cb), lambda j, b: (b, 0, j)))
        args.append(arr)
    out_specs, out_shapes = [], []
    for dt in tok_grads:
        out_specs.append(pl.BlockSpec((1, nl, cb), lambda j, b: (b, 0, j)))
        out_shapes.append(jax.ShapeDtypeStruct((nb, nl, nc), dt))
    for arr, _ in vecs:
        out_specs.append(pl.BlockSpec((arr.shape[0], cb), lambda j, b: (0, j)))
        out_shapes.append(jax.ShapeDtypeStruct((arr.shape[0], nc), F32))

    def body(*refs):
        ins, os = refs[:len(specs)], refs[len(specs):]
        b = pl.program_id(1)
        tv = [r[0].astype(F32) for r in ins[:nt]]
        vv = [r[...] for r in ins[nt:n_in]]
        _, vjp = jax.vjp(lambda *d: tuple(fn(*d)), *tv, *vv)
        grads = vjp(tuple(r[0].astype(F32) for r in ins[n_in:]))
        for g, o in zip(grads[:nt], os[:nt]):
            o[0] = g.astype(o.dtype)

        @pl.when(b == 0)
        def _():
            for o in os[nt:]:
                o[...] = jnp.zeros_like(o)

        for g, o in zip(grads[nt:], os[nt:]):
            o[...] += g

    out = pl.pallas_call(
        body, name=name, grid=(nc // cb, nb), in_specs=specs, out_specs=out_specs, out_shape=out_shapes,
        compiler_params=_cparams(("parallel", "arbitrary")),
    )(*args)
    return out[:nt], out[nt:]


EXP2_SCALE = ATTN_SCALE * math.log2(math.e)


def _attn_fwd(qh, kh, vh, *, tq, name):
    nbh, s, dq = qh.shape
    lk, dv = kh.shape[1], vh.shape[2]

    def body(q_ref, k_ref, v_ref, o_ref, lse_ref):
        sc = _dot(q_ref[0], k_ref[0], ((1,), (1,)))
        m = jnp.max(sc, axis=-1, keepdims=True)
        p = jnp.exp2((sc - m) * EXP2_SCALE)
        denom = jnp.sum(p, axis=-1, keepdims=True)
        o_ref[0] = _dot(p, v_ref[0], ((1,), (0,))) / denom
        lse_ref[0] = m * EXP2_SCALE + jnp.log2(denom)

    return pl.pallas_call(
        body, name=name, grid=(nbh, s // tq),
        in_specs=[pl.BlockSpec((1, tq, dq), lambda i, j: (i, j, 0)), pl.BlockSpec((1, lk, dq), lambda i, j: (i, 0, 0)),
                  pl.BlockSpec((1, lk, dv), lambda i, j: (i, 0, 0))],
        out_specs=[pl.BlockSpec((1, tq, dv), lambda i, j: (i, j, 0)), pl.BlockSpec((1, tq, 1), lambda i, j: (i, j, 0))],
        out_shape=[jax.ShapeDtypeStruct((nbh, s, dv), F32), jax.ShapeDtypeStruct((nbh, s, 1), F32)],
        compiler_params=_cparams(("parallel", "parallel")),
    )(qh, kh, vh)


def _attn_bwd(qh, kh, vh, oh, lse, doh, *, tq, name):
    nbh, s, dq = qh.shape
    lk, dv = kh.shape[1], vh.shape[2]
    nj = s // tq

    def body(q_ref, k_ref, v_ref, o_ref, lse_ref, do_ref, dq_ref, dk_ref, dv_ref):
        j = pl.program_id(1)
        q, k, v, do = q_ref[0], k_ref[0], v_ref[0], do_ref[0]
        sc = _dot(q, k, ((1,), (1,)))
        p = jnp.exp2(sc * EXP2_SCALE - lse_ref[0])
        dp = _dot(do, v, ((1,), (1,)))
        delta = jnp.sum(do * o_ref[0], axis=-1, keepdims=True)
        ds = (p * (dp - delta)).astype(MXU_DTYPE)
        dq_ref[0] = _dot(ds, k, ((1,), (0,))) * ATTN_SCALE

        @pl.when(j == 0)
        def _():
            dk_ref[...] = jnp.zeros_like(dk_ref)
            dv_ref[...] = jnp.zeros_like(dv_ref)

        dk_ref[0] += _dot(ds, q, ((0,), (0,)))
        dv_ref[0] += _dot(p, do, ((0,), (0,)))

        @pl.when(j == nj - 1)
        def _():
            dk_ref[0] = dk_ref[0] * ATTN_SCALE

    qspec = lambda w: pl.BlockSpec((1, tq, w), lambda i, j: (i, j, 0))
    kspec = lambda w: pl.BlockSpec((1, lk, w), lambda i, j: (i, 0, 0))
    return pl.pallas_call(
        body, name=name, grid=(nbh, nj),
        in_specs=[qspec(dq), kspec(dq), kspec(dv), qspec(dv), qspec(1), qspec(dv)],
        out_specs=[qspec(dq), kspec(dq), kspec(dv)],
        out_shape=[jax.ShapeDtypeStruct((nbh, s, dq), F32), jax.ShapeDtypeStruct((nbh, lk, dq), F32),
                   jax.ShapeDtypeStruct((nbh, lk, dv), F32)],
        compiler_params=_cparams(("parallel", "arbitrary")),
    )(qh, kh, vh, oh, lse, doh)


N_PAIRS = HEADS_PER_GROUP // 2
PAIR_W = 2 * SSD_HEAD_DIM


def _ssd_chunk(states, xs, dtc, dtr, bm, cm, ac, ar, *, reverse):
    q = dtc.shape[0]
    row = lax.broadcasted_iota(jnp.int32, (q, q), 0)
    col = lax.broadcasted_iota(jnp.int32, (q, q), 1)
    if reverse:
        tri_c, tri_r, mask = col < row, row < col, col >= row
    else:
        tri_c, tri_r, mask = col <= row, row <= col, col <= row
    a_col, a_row = dtc * ac, dtr * ar
    cum_c = _dot_exact(tri_c.astype(F32), a_col)
    cum_r = _dot_exact(a_row, tri_r.astype(F32))
    tot = jnp.sum(a_col, axis=0, keepdims=True)
    cb = _dot(cm, bm, ((1,), (1,)))
    first = lax.broadcasted_iota(jnp.int32, (1, PAIR_W), 1) < SSD_HEAD_DIM
    first_rows = lax.broadcasted_iota(jnp.int32, (PAIR_W, 1), 0) < SSD_HEAD_DIM
    ys, new_states = [], []
    for pr in range(N_PAIRS):
        per_head = []
        for h in range(2):
            e = 2 * pr + h
            cc, cr, dc, te = _col_of(cum_c, e), _row_of(cum_r, e), _col_of(dtc, e), _col_of(tot, e)
            if reverse:
                within = jnp.exp(jnp.where(mask, cr - cc, -jnp.inf))
                into, to_end = jnp.exp(te - cc), jnp.exp(cc)
            else:
                within = jnp.exp(jnp.where(mask, cc - cr, -jnp.inf))
                into, to_end = jnp.exp(cc), jnp.exp(te - cc)
            per_head.append((cb * within, dc, into, to_end, jnp.exp(te)))
        (m0, dc0, in0, end0, t0), (m1, dc1, in1, end1, t1) = per_head
        xd = xs[pr] * jnp.where(first, dc0, dc1)
        y_diag = (_dot(m0, jnp.where(first, xd, 0.0), ((1,), (0,)))
                  + _dot(m1, jnp.where(first, 0.0, xd), ((1,), (0,))))
        y_off = _dot(cm, states[pr], ((1,), (1,))) * jnp.where(first, in0, in1)
        ys.append(y_diag + y_off)
        grow = _dot(xd * jnp.where(first, end0, end1), bm, ((0,), (0,)))
        new_states.append(states[pr] * jnp.where(first_rows, t0, t1) + grow)
    return tuple(ys) + tuple(new_states)


def _chunk_of_step(t, ncc, nch, reverse):
    if not reverse:
        return t
    return jnp.where(t < ncc, ncc - 1 - t, nch - 1 - (t - ncc))


def _scan_in_specs(nch, ncc, reverse, back):
    q, n, e = SSD_CHUNK, SSD_STATE, HEADS_PER_GROUP
    xcols = D_INNER // SSD_GROUPS

    def ch(t):
        return _chunk_of_step((nch - 1 - t) if back else t, ncc, nch, reverse)

    return ch, [
        pl.BlockSpec((1, q, xcols), lambda b, g, t: (b, ch(t), g)),
        pl.BlockSpec((1, q, n), lambda b, g, t: (b, ch(t), D_INNER // n + g)),
        pl.BlockSpec((1, q, n), lambda b, g, t: (b, ch(t), D_INNER // n + SSD_GROUPS + g)),
        pl.BlockSpec((1, 1, q, e), lambda b, g, t: (b, g, ch(t), 0)),
        pl.BlockSpec((1, 1, e, q), lambda b, g, t: (b, g, 0, ch(t))),
        pl.BlockSpec((1, 1, e), lambda b, g, t: (g, 0, 0)),
        pl.BlockSpec((1, e, 1), lambda b, g, t: (g, 0, 0)),
    ]


def _scan_fwd(xbc, dtc, dtr, ac, ar, *, ncc, reverse, name):
    nb, lt, _ = xbc.shape
    q, n = SSD_CHUNK, SSD_STATE
    nch = lt // q
    xcols = D_INNER // SSD_GROUPS
    ch, in_specs = _scan_in_specs(nch, ncc, reverse, False)

    def body(x_ref, b_ref, c_ref, dtc_ref, dtr_ref, ac_ref, ar_ref, y_ref, ent_ref, st_ref):
        t = pl.program_id(2)

        @pl.when(t == 0)
        def _():
            st_ref[...] = jnp.zeros_like(st_ref)

        states = [st_ref[i] for i in range(N_PAIRS)]
        for i in range(N_PAIRS):
            ent_ref[0, 0, 0, i] = states[i]
        xs = [x_ref[0, :, pl.ds(PAIR_W * i, PAIR_W)] for i in range(N_PAIRS)]
        res = _ssd_chunk(states, xs, dtc_ref[0, 0], dtr_ref[0, 0], b_ref[0], c_ref[0], ac_ref[0], ar_ref[0],
                         reverse=reverse)
        for i in range(N_PAIRS):
            y_ref[0, :, pl.ds(PAIR_W * i, PAIR_W)] = res[i]
            st_ref[i] = res[N_PAIRS + i]

    return pl.pallas_call(
        body, name=name, grid=(nb, SSD_GROUPS, nch), in_specs=in_specs,
        out_specs=[pl.BlockSpec((1, q, xcols), lambda b, g, t: (b, ch(t), g)),
                   pl.BlockSpec((1, 1, 1, N_PAIRS, PAIR_W, n), lambda b, g, t: (b, g, t, 0, 0, 0))],
        out_shape=[jax.ShapeDtypeStruct((nb, lt, D_INNER), F32),
                   jax.ShapeDtypeStruct((nb, SSD_GROUPS, nch, N_PAIRS, PAIR_W, n), F32)],
        scratch_shapes=[pltpu.VMEM((N_PAIRS, PAIR_W, n), F32)],
        compiler_params=_cparams(("parallel", "parallel", "arbitrary")),
    )(xbc, xbc, xbc, dtc, dtr, ac, ar)


def _scan_bwd(xbc, dtc, dtr, ac, ar, entering, dy, *, ncc, reverse, name):
    nb, lt, _ = xbc.shape
    q, n, e = SSD_CHUNK, SSD_STATE, HEADS_PER_GROUP
    nch = lt // q
    xcols = D_INNER // SSD_GROUPS
    ch, in_specs = _scan_in_specs(nch, ncc, reverse, True)
    in_specs = in_specs + [
        pl.BlockSpec((1, 1, 1, N_PAIRS, PAIR_W, n), lambda b, g, t: (b, g, nch - 1 - t, 0, 0, 0)),
        pl.BlockSpec((1, q, xcols), lambda b, g, t: (b, jnp.maximum(ch(t) - ncc, 0), g)),
    ]

    def body(x_ref, b_ref, c_ref, dtc_ref, dtr_ref, ac_ref, ar_ref, ent_ref, dy_ref,
             dx_ref, db_ref, dc_ref, ddtc_ref, ddtr_ref, dac_ref, dar_ref, ds_ref):
        t = pl.program_id(2)

        @pl.when(t == 0)
        def _():
            ds_ref[...] = jnp.zeros_like(ds_ref)
            dac_ref[...] = jnp.zeros_like(dac_ref)
            dar_ref[...] = jnp.zeros_like(dar_ref)

        states = [ent_ref[0, 0, 0, i] for i in range(N_PAIRS)]
        xs = [x_ref[0, :, pl.ds(PAIR_W * i, PAIR_W)] for i in range(N_PAIRS)]

        def f(states, xs, dtc, dtr, bm, cm, ac, ar):
            return _ssd_chunk(states, xs, dtc, dtr, bm, cm, ac, ar, reverse=reverse)

        _, vjp = jax.vjp(f, states, xs, dtc_ref[0, 0], dtr_ref[0, 0], b_ref[0], c_ref[0], ac_ref[0], ar_ref[0])
        latent = ch(t) >= ncc
        dys = [jnp.where(latent, dy_ref[0, :, pl.ds(PAIR_W * i, PAIR_W)], 0.0) for i in range(N_PAIRS)]
        gs, gx, gdtc, gdtr, gb, gc, gac, gar = vjp(tuple(dys) + tuple(ds_ref[i] for i in range(N_PAIRS)))
        for i in range(N_PAIRS):
            ds_ref[i] = gs[i]
            dx_ref[0, :, pl.ds(PAIR_W * i, PAIR_W)] = gx[i]
        db_ref[0] = gb
        dc_ref[0] = gc
        ddtc_ref[0, 0] = gdtc
        ddtr_ref[0, 0] = gdtr
        dac_ref[0, 0] += gac
        dar_ref[0, 0] += gar

    return pl.pallas_call(
        body, name=name, grid=(nb, SSD_GROUPS, nch), in_specs=in_specs,
        out_specs=[pl.BlockSpec((1, q, xcols), lambda b, g, t: (b, ch(t), g)),
                   pl.BlockSpec((1, q, n), lambda b, g, t: (b, ch(t), g)),
                   pl.BlockSpec((1, q, n), lambda b, g, t: (b, ch(t), g)),
                   pl.BlockSpec((1, 1, q, e), lambda b, g, t: (b, g, ch(t), 0)),
                   pl.BlockSpec((1, 1, e, q), lambda b, g, t: (b, g, 0, ch(t))),
                   pl.BlockSpec((1, 1, 1, e), lambda b, g, t: (b, g, 0, 0)),
                   pl.BlockSpec((1, 1, e, 1), lambda b, g, t: (b, g, 0, 0))],
        out_shape=[jax.ShapeDtypeStruct((nb, lt, D_INNER), F32), jax.ShapeDtypeStruct((nb, lt, GN), F32),
                   jax.ShapeDtypeStruct((nb, lt, GN), F32), jax.ShapeDtypeStruct((nb, SSD_GROUPS, lt, e), F32),
                   jax.ShapeDtypeStruct((nb, SSD_GROUPS, e, lt), F32), jax.ShapeDtypeStruct((nb, SSD_GROUPS, 1, e), F32),
                   jax.ShapeDtypeStruct((nb, SSD_GROUPS, e, 1), F32)],
        scratch_shapes=[pltpu.VMEM((N_PAIRS, PAIR_W, n), F32)],
        compiler_params=_cparams(("parallel", "parallel", "arbitrary")),
    )(xbc, xbc, xbc, dtc, dtr, ac, ar, entering, dy)


def _adamw(w, g, m, v, *, name):
    r, c = w.shape
    tr = _pick(r, (256, 176, 128, 96, 64, 8))
    c1 = 1.0 / (1.0 - ADAM_B1 ** ADAM_STEP)
    c2 = 1.0 / (1.0 - ADAM_B2 ** ADAM_STEP)

    def body(w_ref, g_ref, m_ref, v_ref, d_ref, nm_ref, nv_ref):
        gv = g_ref[...]
        nm = ADAM_B1 * m_ref[...] + (1.0 - ADAM_B1) * gv
        nv = ADAM_B2 * v_ref[...] + (1.0 - ADAM_B2) * (gv * gv)
        d_ref[...] = -ADAM_LR * ((nm * c1) / (jnp.sqrt(nv * c2) + ADAM_EPS) + ADAM_WD * w_ref[...])
        nm_ref[...] = nm
        nv_ref[...] = nv

    spec = pl.BlockSpec((tr, c), lambda i: (i, 0))
    return pl.pallas_call(
        body, name=name, grid=(r // tr,), in_specs=[spec] * 4, out_specs=[spec] * 3,
        out_shape=[jax.ShapeDtypeStruct((r, c), F32)] * 3, compiler_params=_cparams(("parallel",)),
    )(w, g, m, v)


def _sum_rows_tile(r):
    return r if r <= 1024 else _pick(r, (656, 512, 256, 128, 64, 32, 16))


def _sum_slots(x, *, out_dtype, name):
    n, r, c = x.shape
    tr = _sum_rows_tile(r)

    def body(x_ref, o_ref):
        acc = x_ref[0].astype(F32)
        for k in range(1, n):
            acc = acc + x_ref[k].astype(F32)
        o_ref[...] = acc.astype(o_ref.dtype)

    return pl.pallas_call(
        body, name=name, grid=(r // tr,), in_specs=[pl.BlockSpec((n, tr, c), lambda i: (0, i, 0))],
        out_specs=pl.BlockSpec((tr, c), lambda i: (i, 0)), out_shape=jax.ShapeDtypeStruct((r, c), out_dtype),
        compiler_params=_cparams(("parallel",)),
    )(x)


def _sum_list(xs, *, out_dtype, name):
    r, c = xs[0].shape
    tr = _sum_rows_tile(r)

    def body(*refs):
        acc = refs[0][...].astype(F32)
        for ref in refs[1:-1]:
            acc = acc + ref[...].astype(F32)
        refs[-1][...] = acc.astype(refs[-1].dtype)

    spec = pl.BlockSpec((tr, c), lambda i: (i, 0))
    return pl.pallas_call(
        body, name=name, grid=(r // tr,), in_specs=[spec] * len(xs), out_specs=spec,
        out_shape=jax.ShapeDtypeStruct((r, c), out_dtype), compiler_params=_cparams(("parallel",)),
    )(*xs)


ANY = pl.BlockSpec(memory_space=pl.ANY)


def _place():
    return lax.axis_index("x"), lax.axis_index("y"), lax.axis_index("c")


def _allgather_small(v, *, name):
    r, c = v.shape

    def body(v_ref, out_ref, send_sems, recv_sems, local_sem):
        x, y, cc = _place()
        me = 4 * x + 2 * y + cc
        mine = pltpu.make_async_copy(v_ref, out_ref.at[me], local_sem)
        mine.start()
        copies = []
        for k in range(1, N_DEV):
            fx, fy, fc = (k >> 2) & 1, (k >> 1) & 1, k & 1
            peer = (1 - x if fx else x, 1 - y if fy else y, 1 - cc if fc else cc)
            copies.append(pltpu.make_async_remote_copy(
                src_ref=v_ref, dst_ref=out_ref.at[me], send_sem=send_sems.at[k - 1], recv_sem=recv_sems.at[k - 1],
                device_id=peer, device_id_type=MESH))
        for cp in copies:
            cp.start()
        for cp in copies:
            cp.wait()
        mine.wait()

    return pl.pallas_call(
        body, name=name, in_specs=[ANY], out_specs=ANY, out_shape=jax.ShapeDtypeStruct((N_DEV, r, c), v.dtype),
        scratch_shapes=[pltpu.SemaphoreType.DMA((N_DEV - 1,)), pltpu.SemaphoreType.DMA((N_DEV - 1,)),
                        pltpu.SemaphoreType.DMA],
    )(v)


def _other_chips(x, y):
    return [(1 - x, y), (x, 1 - y), (1 - x, 1 - y)]


def _gather_shards(mine, *, name):
    r, c = mine.shape
    half = r // 2

    def body(v_ref, out_ref, send_sems, recv_sems):
        x, y, cc = _place()
        sibling = (x, y, 1 - cc)
        chips = _other_chips(x, y)

        def rows(px, py, pc):
            return out_ref.at[2 * px + py, pl.ds(pc * half, half), :]

        def copy(k, block, to, src=None):
            return pltpu.make_async_remote_copy(
                src_ref=rows(*block) if src is None else src, dst_ref=rows(*block),
                send_sem=send_sems.at[k], recv_sem=recv_sems.at[k], device_id=to, device_id_type=MESH)

        my_half = v_ref.at[pl.ds(cc * half, half), :]
        first = [copy(j, (x, y, cc), (*chip, cc), src=my_half) for j, chip in enumerate(chips)]
        for cp in first:
            cp.start()
        passed = [copy(3 + j, (*chip, cc), sibling) for j, chip in enumerate(chips)]
        for j, chip in enumerate(chips):
            copy(j, (*chip, cc), (x, y, cc)).wait_recv()
            passed[j].start()
        for j, chip in enumerate(chips):
            copy(3 + j, (*chip, 1 - cc), (x, y, cc)).wait_recv()
        for cp in first + passed:
            cp.wait_send()

    return pl.pallas_call(
        body, name=name, in_specs=[ANY], out_specs=ANY, out_shape=jax.ShapeDtypeStruct((N_CHIPS, r, c), mine.dtype),
        scratch_shapes=[pltpu.SemaphoreType.DMA((6,)), pltpu.SemaphoreType.DMA((6,))],
    )(mine)


def _swap_halves(g, *, name):
    n, _, r, c = g.shape

    def body(g_ref, got_ref, send_sems, recv_sems):
        x, y, cc = _place()
        sibling = (x, y, 1 - cc)
        rems = []
        for j in range(n):
            rems.append(pltpu.make_async_remote_copy(
                src_ref=g_ref.at[j, 1 - cc], dst_ref=got_ref.at[j], send_sem=send_sems.at[j],
                recv_sem=recv_sems.at[j], device_id=sibling, device_id_type=MESH))
        for cp in rems:
            cp.start()
        for cp in rems:
            cp.wait()

    return pl.pallas_call(
        body, name=name, in_specs=[ANY], out_specs=ANY, out_shape=jax.ShapeDtypeStruct((n, r, c), g.dtype),
        scratch_shapes=[pltpu.SemaphoreType.DMA((n,)), pltpu.SemaphoreType.DMA((n,))],
    )(g)


def _scatter_to_chips(s, *, name):
    n, r, c = s.shape

    def body(s_ref, out_ref, send_sems, recv_sems):
        x, y, cc = _place()
        me = 2 * x + y
        copies = []
        for j, (px, py) in enumerate(_other_chips(x, y)):
            copies.append(pltpu.make_async_remote_copy(
                src_ref=s_ref.at[2 * px + py], dst_ref=out_ref.at[me], send_sem=send_sems.at[j],
                recv_sem=recv_sems.at[j], device_id=(px, py, cc), device_id_type=MESH))
        for cp in copies:
            cp.start()
        for cp in copies:
            cp.wait()

    return pl.pallas_call(
        body, name=name, in_specs=[ANY], out_specs=ANY, out_shape=jax.ShapeDtypeStruct((n, r, c), s.dtype),
        scratch_shapes=[pltpu.SemaphoreType.DMA((3,)), pltpu.SemaphoreType.DMA((3,))],
    )(s)


def _join_halves(f, *, name):
    r, c = f.shape

    def body(f_ref, out_ref, send_sem, recv_sem):
        x, y, cc = _place()
        cp = pltpu.make_async_remote_copy(src_ref=f_ref, dst_ref=out_ref.at[cc], send_sem=send_sem, recv_sem=recv_sem,
                                          device_id=(x, y, 1 - cc), device_id_type=MESH)
        cp.start()
        cp.wait()

    return pl.pallas_call(
        body, name=name, in_specs=[ANY], out_specs=ANY, out_shape=jax.ShapeDtypeStruct((2, r, c), f.dtype),
        scratch_shapes=[pltpu.SemaphoreType.DMA, pltpu.SemaphoreType.DMA],
    )(f)


def _pack_rows(parts, width=PACK_COLS):
    return jnp.concatenate([p.reshape(-1, width) for p in parts], axis=0)


def _pack_small(parts, rows):
    flat = jnp.concatenate([p.reshape(-1).astype(F32) for p in parts])
    return jnp.pad(flat, (0, rows * LANES - flat.shape[0])).reshape(rows, LANES)


def _unpack_small(packed, shapes):
    flat = packed.reshape(-1)
    out, o = [], 0
    for shp in shapes:
        n = int(np.prod(shp))
        out.append(flat[o:o + n].reshape(shp))
        o += n
    return out


def _perm_in_cols(w):
    a, b = Q_LORA_RANK + KV_LORA_RANK, Q_LORA_RANK + KV_LORA_RANK + QK_ROPE_DIM
    c = IN_WIDTH - 2 * N_SSD_HEADS
    return jnp.concatenate([w[:, :a], w[:, b:c], w[:, a:b], w[:, c:]], axis=1)


def _unperm_in_cols(w):
    a = Q_LORA_RANK + KV_LORA_RANK
    zx = D_INNER + XBC_WIDTH
    return jnp.concatenate([w[:, :a], w[:, a + zx:a + zx + QK_ROPE_DIM], w[:, a:a + zx], w[:, a + zx + QK_ROPE_DIM:]],
                           axis=1)


def _rope_tables(seq_len):
    n_rows = seq_len // GRID_W
    row = jnp.repeat(jnp.arange(n_rows), GRID_W).astype(F32)
    col = jnp.tile(jnp.arange(GRID_W), n_rows).astype(F32)
    axis_dim = QK_ROPE_DIM // 2
    inv_freq = ROPE_THETA ** (-jnp.arange(0, axis_dim, 2, dtype=F32) / axis_dim)
    ang_r = row[:, None] * inv_freq
    ang_c = col[:, None] * inv_freq
    ang = jnp.concatenate([ang_r, ang_r, ang_c, ang_c], axis=-1)
    return jnp.cos(ang), jnp.sin(ang)


def _rot_matrix(width, start):
    r = np.zeros((width, width), np.float32)
    quarter = QK_ROPE_DIM // 4
    for base in (0, QK_ROPE_DIM // 2):
        for i in range(quarter):
            r[start + base + quarter + i, start + base + i] = -1.0
            r[start + base + i, start + base + quarter + i] = 1.0
    return jnp.asarray(r)


ROPE_STEP = QK_ROPE_DIM // 4


def _rope_flat_fn(x, cos, sin_up, sin_down):
    return (x * cos + _roll_lanes(x, -ROPE_STEP) * sin_up + _roll_lanes(x, ROPE_STEP) * sin_down,)


def _krdt_fn(x, cos, sin, rot, bias):
    lane = lax.broadcasted_iota(jnp.int32, (1, KRDT_WIDTH), 1)
    is_dt = (lane >= QK_ROPE_DIM) & (lane < QK_ROPE_DIM + 2 * N_SSD_HEADS)
    roped = x * cos + _dot_exact(x, rot) * sin
    return (jnp.where(is_dt, _softplus(x + bias), roped),)


def _pre_fn(u, w, shift, scale):
    return (_rms(u, w) * (1.0 + scale) + shift,)


def _norm_fn(x, w):
    return (_rms(x, w),)


def _finish_fn(yf, yb, xs, z, d_skip, w):
    y = yf + yb + d_skip * xs
    return (_rms(y * _silu(z), w),)


def _mid_fn(x, mix, w_post, w_pre, gate, shift, scale):
    x1 = x + gate * _rms(mix, w_post)
    return (x1, _rms(x1, w_pre) * (1.0 + scale) + shift)


def _loss_fn(x1, ffn, tgt, w_post, gate):
    y = x1 + gate * _rms(ffn, w_post)
    err = y - tgt
    return (0.5 * jnp.mean(err * err, axis=-1, keepdims=True),)


def _bias_fn(x, b):
    return (x + b,)


def _silu_fn(x):
    return (_silu(x),)


def kernel(x, c, ctx, c_ctx, w_mod, b_mod, mix_pre_norm, mix_post_norm, w_in, q_norm, w_q_up, kv_norm, w_kv_up, ssd_conv_w, ssd_conv_b, ssd_a_log, ssd_dt_bias, ssd_d, ssd_norm, w_out, ffn_pre_norm, ffn_post_norm, w_up, ffn_conv_w, ffn_conv_b, w_down, loss_target, m_c_ctx, m_w_mod, m_b_mod, m_mix_pre_norm, m_mix_post_norm, m_w_in, m_q_norm, m_w_q_up, m_kv_norm, m_w_kv_up, m_ssd_conv_w, m_ssd_conv_b, m_ssd_a_log, m_ssd_dt_bias, m_ssd_d, m_ssd_norm, m_w_out, m_ffn_pre_norm, m_ffn_post_norm, m_w_up, m_ffn_conv_w, m_ffn_conv_b, m_w_down, v_c_ctx, v_w_mod, v_b_mod, v_mix_pre_norm, v_mix_post_norm, v_w_in, v_q_norm, v_w_q_up, v_kv_norm, v_w_kv_up, v_ssd_conv_w, v_ssd_conv_b, v_ssd_a_log, v_ssd_dt_bias, v_ssd_d, v_ssd_norm, v_w_out, v_ffn_pre_norm, v_ffn_post_norm, v_w_up, v_ffn_conv_w, v_ffn_conv_b, v_w_down):
    args = dict(locals())
    names = ["c_ctx", "w_mod", "b_mod", "mix_pre_norm", "mix_post_norm", "w_in", "q_norm", "w_q_up", "kv_norm",
             "w_kv_up", "ssd_conv_w", "ssd_conv_b", "ssd_a_log", "ssd_dt_bias", "ssd_d", "ssd_norm", "w_out",
             "ffn_pre_norm", "ffn_post_norm", "w_up", "ffn_conv_w", "ffn_conv_b", "w_down"]
    nb, s, d = x.shape
    nctx_rows = ctx.shape[1]
    lt = nctx_rows + s
    tl = 256 if (nctx_rows % 256 == 0 and s % 256 == 0) else 128
    nctx = nctx_rows // tl
    ncc = nctx_rows // SSD_CHUNK
    h, e, g2 = N_ATTN_HEADS, HEADS_PER_GROUP, SSD_GROUPS
    chip = 2 * lax.axis_index("x") + lax.axis_index("y")

    big_local = {n: args[n][0] for n, _, _, _ in BIG}
    packed = _pack_rows([big_local[n].astype(WIRE_DTYPE) for n, _, _, _ in BIG])
    shard_rows = packed.shape[0]
    gathered = _gather_shards(packed, name="gather_weights")
    gathered = lax.dynamic_update_slice(gathered, packed[None], (chip, 0, 0))
    full, o = {}, 0
    for n, rows, cols, axis in BIG:
        lr, lc = big_local[n].shape
        nr = lr * lc // PACK_COLS
        seg = gathered[:, o:o + nr].reshape(N_CHIPS, lr, lc)
        o += nr
        full[n] = seg.reshape(rows, cols) if axis == 0 else jnp.transpose(seg, (1, 0, 2)).reshape(rows, cols)
    conv_rows = 32
    conv_all = _allgather_small(_pack_small([ssd_conv_w[0], ffn_conv_w[0]], conv_rows), name="gather_conv")
    conv_all = conv_all[::2]
    n_sc, n_fc = ssd_conv_w.shape[2], ffn_conv_w.shape[2]
    ssd_conv_full = jnp.concatenate(
        [conv_all[j].reshape(-1)[:SSD_CONV * n_sc].reshape(SSD_CONV, n_sc) for j in range(N_CHIPS)], axis=1)
    ffn_conv_full = jnp.concatenate(
        [conv_all[j].reshape(-1)[SSD_CONV * n_sc:SSD_CONV * n_sc + FFN_CONV * n_fc].reshape(FFN_CONV, n_fc)
         for j in range(N_CHIPS)], axis=1)

    w_in_p = _perm_in_cols(full["w_in"])
    o_cq, o_ckv, o_z = 0, Q_LORA_RANK, Q_LORA_RANK + KV_LORA_RANK
    o_xbc, o_kr = o_z + D_INNER, o_z + D_INNER + XBC_WIDTH
    w_krdt = jnp.pad(w_in_p[:, o_kr:], ((0, 0), (0, KRDT_WIDTH - QK_ROPE_DIM - 2 * N_SSD_HEADS)))
    w_in_p = jnp.concatenate([w_in_p[:, :o_kr], w_krdt], axis=1)
    in_p_width = w_in_p.shape[1]

    mod_rows = 16
    c_all = jnp.concatenate([c, c_ctx[None, :], jnp.zeros((mod_rows - nb - 1, d), F32)], axis=0)[None]
    (s_all,) = _row_fwd(_silu_fn, toks=[(c_all, 0, None, 0)], outs=[(d, F32)], nb=1, nl=mod_rows, tl=mod_rows,
                        name="mod_silu")
    mod_lin = _mm(s_all[0], full["w_mod"], name="mod_mm")
    (mod,) = _row_fwd(_bias_fn, toks=[(mod_lin[None], 0, None, 0)], vecs=[b_mod], outs=[(N_MOD * d, F32)], nb=1,
                      nl=mod_rows, tl=mod_rows, name="mod_bias")
    mods = [mod[0][:, k * d:(k + 1) * d][:, None, :] for k in range(N_MOD)]
    mods_lat = [m[:nb] for m in mods]

    u = jnp.concatenate([ctx, x], axis=1)
    (h1,) = _row_fwd(_pre_fn, toks=[(u, 0, None, 0)], vecs=[mix_pre_norm], bvecs=[mods[0], mods[1]],
                     outs=[(d, MXU_DTYPE)], nb=nb, nl=lt, tl=tl, nctx=nctx, name="pre1")
    h1f = h1.reshape(nb * lt, d)
    p_cq = _mm(h1f, w_in_p[:, o_cq:o_ckv], name="in_cq").reshape(nb, lt, -1)
    p_ckv = _mm(h1f, w_in_p[:, o_ckv:o_z], name="in_ckv").reshape(nb, lt, -1)
    p_z = _mm(h1f, w_in_p[:, o_z:o_xbc], name="in_z").reshape(nb, lt, -1)
    p_xbc = _mm(h1f, w_in_p[:, o_xbc:o_kr], name="in_xbc").reshape(nb, lt, -1)
    p_krdt = _mm(h1f, w_in_p[:, o_kr:], name="in_krdt").reshape(nb, lt, -1)

    (cqn,) = _row_fwd(_norm_fn, toks=[(p_cq, nctx, None, 0)], vecs=[q_norm], outs=[(Q_LORA_RANK, MXU_DTYPE)],
                      nb=nb, nl=s, tl=tl, name="q_norm")
    q_flat = _mm(cqn.reshape(nb * s, -1), full["w_q_up"], name="q_up").reshape(nb, s, h * QK_DIM)
    cos, sin = _rope_tables(s)
    ones, zeros = jnp.ones((s, QK_NOPE_DIM), F32), jnp.zeros((s, QK_NOPE_DIM), F32)
    up_lanes = ((jnp.arange(QK_ROPE_DIM) // ROPE_STEP) % 2 == 0)[None, :]
    q_tables = [jnp.tile(jnp.concatenate([pad, t], axis=1), (1, h))[None]
                for pad, t in ((ones, cos), (zeros, jnp.where(up_lanes, -sin, 0.0)), (zeros, jnp.where(up_lanes, 0.0, sin)))]
    tq = 256
    (q_roped,) = _row_fwd(_rope_flat_fn, toks=[(q_flat, 0, None, 0)], poss=q_tables, outs=[(h * QK_DIM, MXU_DTYPE)],
                          nb=nb, nl=s, tl=tl, name="rope_q")
    qh = jnp.transpose(q_roped.reshape(nb, s, h, QK_DIM), (0, 2, 1, 3)).reshape(nb * h, s, QK_DIM)

    (ckvn,) = _row_fwd(_norm_fn, toks=[(p_ckv, 0, None, 0)], vecs=[kv_norm], outs=[(KV_LORA_RANK, MXU_DTYPE)],
                       nb=nb, nl=lt, tl=tl, name="kv_norm")
    kv_flat = _mm(ckvn.reshape(nb * lt, -1), full["w_kv_up"], out_dtype=MXU_DTYPE, name="kv_up")
    kv_heads = jnp.transpose(kv_flat.reshape(nb, lt, h, QK_NOPE_DIM + V_HEAD_DIM), (0, 2, 1, 3))
    k_nope, vh = kv_heads[..., :QK_NOPE_DIM], kv_heads[..., QK_NOPE_DIM:].reshape(nb * h, lt, V_HEAD_DIM)

    pad_w = KRDT_WIDTH - QK_ROPE_DIM
    cos_k = jnp.concatenate([jnp.ones((nctx_rows, KRDT_WIDTH), F32),
                             jnp.concatenate([cos, jnp.ones((s, pad_w), F32)], axis=1)], axis=0)[None]
    sin_k = jnp.concatenate([jnp.zeros((nctx_rows, KRDT_WIDTH), F32),
                             jnp.concatenate([sin, jnp.zeros((s, pad_w), F32)], axis=1)], axis=0)[None]
    rot_k = _rot_matrix(KRDT_WIDTH, 0)
    dt_bias_row = jnp.pad(ssd_dt_bias.reshape(1, -1), ((0, 0), (QK_ROPE_DIM, pad_w - 2 * N_SSD_HEADS)))
    (krdt,) = _row_fwd(_krdt_fn, toks=[(p_krdt, 0, None, 0)], poss=[cos_k, sin_k], vecs=[rot_k, dt_bias_row],
                       outs=[(KRDT_WIDTH, F32)], nb=nb, nl=lt, tl=tl, name="krdt")
    k_rope = krdt[..., :QK_ROPE_DIM].astype(MXU_DTYPE)
    kh = jnp.concatenate([k_nope, jnp.broadcast_to(k_rope[:, None], (nb, h, lt, QK_ROPE_DIM))], axis=-1)
    kh = kh.reshape(nb * h, lt, QK_DIM)
    oh, lse = _attn_fwd(qh, kh, vh, tq=tq, name="attn_fwd")
    attn = jnp.transpose(oh.reshape(nb, h, s, V_HEAD_DIM), (0, 2, 1, 3)).reshape(nb, s, ATTN_WIDTH).astype(MXU_DTYPE)

    seg = nctx_rows

    def conv_ssd_fn(xv, w, b):
        return (_silu(_dwconv(xv, w, seg) + b),)

    cb_ssd = 256
    (xbc,) = _seq_fwd(conv_ssd_fn, toks=[(p_xbc, 0)], vecs=[(ssd_conv_full, 0), (ssd_conv_b, 0)], outs=[F32], nb=nb,
                      nl=lt, nc=XBC_WIDTH, cb=cb_ssd, name="conv_ssd")
    dt = krdt[..., QK_ROPE_DIM:QK_ROPE_DIM + 2 * N_SSD_HEADS].reshape(nb, lt, 2, g2, e)
    dtc = jnp.transpose(dt, (2, 0, 3, 1, 4))
    dtr = jnp.transpose(dt, (2, 0, 3, 4, 1))
    a_neg = -jnp.exp(ssd_a_log[0]).reshape(2, g2, e)
    ys, ents = [], []
    for dr in range(2):
        y_d, ent_d = _scan_fwd(xbc, dtc[dr], dtr[dr], a_neg[dr][:, None, :], a_neg[dr][:, :, None], ncc=ncc,
                               reverse=bool(dr), name=f"scan_fwd{dr}")
        ys.append(y_d)
        ents.append(ent_d)
    d_row = jnp.repeat(ssd_d[0], SSD_HEAD_DIM)[None, :]
    finish_toks = [(ys[0], nctx, None, 0), (ys[1], nctx, None, 0), (xbc, nctx, D_INNER, 0), (p_z, nctx, None, 0)]
    (ssd,) = _row_fwd(_finish_fn, toks=finish_toks, vecs=[d_row, ssd_norm], outs=[(D_INNER, MXU_DTYPE)], nb=nb,
                      nl=s, tl=tl, name="ssd_finish")

    cat = jnp.concatenate([attn, ssd], axis=-1).reshape(nb * s, ATTN_WIDTH + D_INNER)
    mix = _mm(cat, full["w_out"], name="out_proj").reshape(nb, s, d)

    mid_bvecs = [mods_lat[2], mods_lat[3], mods_lat[4]]
    x1, h2 = _row_fwd(_mid_fn, toks=[(x, 0, None, 0), (mix, 0, None, 0)], vecs=[mix_post_norm, ffn_pre_norm],
                      bvecs=mid_bvecs, outs=[(d, F32), (d, MXU_DTYPE)], nb=nb, nl=s, tl=tl, name="mid")
    up = _mm(h2.reshape(nb * s, d), full["w_up"], name="ffn_up").reshape(nb, s, 2 * D_FF)

    def glu_fn(gate, val, w, b):
        return (_gelu(_dwconv(gate, w, 0) + b) * val,)

    cb_ffn = 256
    glu_toks = [(up, 0), (up, D_FF // cb_ffn)]
    glu_vecs = [(ffn_conv_full, 0), (ffn_conv_b, 0)]
    (act,) = _seq_fwd(glu_fn, toks=glu_toks, vecs=glu_vecs, outs=[MXU_DTYPE], nb=nb, nl=s, nc=D_FF, cb=cb_ffn,
                      name="conv_glu")
    ffn = _mm(act.reshape(nb * s, D_FF), full["w_down"], name="ffn_down").reshape(nb, s, d)

    loss_toks = [(x1, 0, None, 0), (ffn, 0, None, 0), (loss_target, 0, None, 0)]
    ones_rows = jnp.ones((nb, s, 1), F32)
    (dx1_a, dffn, _), (g_ffn_post,), ((g_gate5, _),), (loss_rows,) = _row_bwd(
        _loss_fn, toks=loss_toks, vecs=[ffn_post_norm], bvecs=[mods_lat[5]], cots=[(ones_rows, 0)],
        tok_grads=[F32, MXU_DTYPE, None], emit=[(0, 1, F32)], nb=nb, nl=s, tl=tl, name="loss_bwd")
    loss_part = jnp.sum(loss_rows)

    dffn_f = dffn.reshape(nb * s, d)
    g_w_down = _mm(act.reshape(nb * s, D_FF), dffn_f, ta=True, name="wg_down")
    dact = _mm(dffn_f, full["w_down"], tb=True, out_dtype=MXU_DTYPE, name="dg_down").reshape(nb, s, D_FF)
    (dgate, dval), (g_ffn_conv_w, g_ffn_conv_b) = _seq_bwd(
        glu_fn, toks=glu_toks, vecs=glu_vecs, cots=[dact], tok_grads=[MXU_DTYPE, MXU_DTYPE], nb=nb, nl=s, nc=D_FF,
        cb=cb_ffn, name="conv_glu_bwd")
    dup = jnp.concatenate([dgate, dval], axis=-1).reshape(nb * s, 2 * D_FF)
    g_w_up = _mm(h2.reshape(nb * s, d), dup, ta=True, name="wg_up")
    dh2 = _mm(dup, full["w_up"], tb=True, name="dg_up").reshape(nb, s, d)

    (dx_res, dmix), (g_mix_post, g_ffn_pre), ((g_gate2, _), (g_shift3, _), (g_scale4, _)), _ = _row_bwd(
        _mid_fn, toks=[(x, 0, None, 0), (mix, 0, None, 0)], vecs=[mix_post_norm, ffn_pre_norm], bvecs=mid_bvecs,
        cots=[(dx1_a, 0), (dh2, 0)], tok_grads=[F32, MXU_DTYPE], nb=nb, nl=s, tl=tl, name="mid_bwd")

    dmix_f = dmix.reshape(nb * s, d)
    g_w_out = _mm(cat, dmix_f, ta=True, name="wg_out")
    dcat = _mm(dmix_f, full["w_out"], tb=True, name="dg_out").reshape(nb, s, ATTN_WIDTH + D_INNER)
    dattn, dssd = dcat[..., :ATTN_WIDTH], dcat[..., ATTN_WIDTH:]

    (dyf, _, dxs_skip, dz), (g_d_row, g_ssd_norm), _, _ = _row_bwd(
        _finish_fn, toks=finish_toks, vecs=[d_row, ssd_norm], cots=[(dssd, 0)], tok_grads=[F32, None, F32, MXU_DTYPE],
        nb=nb, nl=s, tl=tl, name="ssd_finish_bwd")
    g_xs = jnp.pad(dxs_skip, ((0, 0), (nctx_rows, 0), (0, 0)))
    g_bm = g_cm = None
    g_dt_dirs, g_a = [], []
    for dr in range(2):
        gx, gb, gc, gdtc, gdtr, gac, gar = _scan_bwd(
            xbc, dtc[dr], dtr[dr], a_neg[dr][:, None, :], a_neg[dr][:, :, None], ents[dr], dyf, ncc=ncc,
            reverse=bool(dr), name=f"scan_bwd{dr}")
        g_xs = g_xs + gx
        g_bm = gb if g_bm is None else g_bm + gb
        g_cm = gc if g_cm is None else g_cm + gc
        g_dt_dirs.append(jnp.transpose(gdtc, (0, 2, 1, 3)) + jnp.transpose(gdtr, (0, 3, 1, 2)))
        g_a.append(jnp.sum(gac[:, :, 0, :] + gar[:, :, :, 0], axis=0))
    g_a_log = (jnp.stack(g_a) * a_neg).reshape(1, 2, N_SSD_HEADS)
    g_dt = jnp.stack(g_dt_dirs, axis=2).reshape(nb, lt, 2 * N_SSD_HEADS)
    g_xbc_act = jnp.concatenate([g_xs, g_bm, g_cm], axis=-1)
    (dp_xbc,), (g_ssd_conv_w, g_ssd_conv_b) = _seq_bwd(
        conv_ssd_fn, toks=[(p_xbc, 0)], vecs=[(ssd_conv_full, 0), (ssd_conv_b, 0)], cots=[g_xbc_act],
        tok_grads=[MXU_DTYPE], nb=nb, nl=lt, nc=XBC_WIDTH, cb=cb_ssd, name="conv_ssd_bwd")

    doh = jnp.transpose(dattn.reshape(nb, s, h, V_HEAD_DIM), (0, 2, 1, 3)).reshape(nb * h, s, V_HEAD_DIM)
    dqh, dkh, dvh = _attn_bwd(qh, kh, vh, oh, lse, doh, tq=tq, name="attn_bwd")
    dq_roped = jnp.transpose(dqh.reshape(nb, h, s, QK_DIM), (0, 2, 1, 3)).reshape(nb, s, h * QK_DIM)
    (dq_flat,), _, _, _ = _row_bwd(_rope_flat_fn, toks=[(q_flat, 0, None, 0)], poss=q_tables, cots=[(dq_roped, 0)],
                                   tok_grads=[MXU_DTYPE], nb=nb, nl=s, tl=tl, name="rope_q_bwd")
    dq_flat = dq_flat.reshape(nb * s, h * QK_DIM)
    g_w_q_up = _mm(cqn.reshape(nb * s, -1), dq_flat, ta=True, name="wg_q_up")
    dcqn = _mm(dq_flat, full["w_q_up"], tb=True, name="dg_q_up").reshape(nb, s, Q_LORA_RANK)
    (dp_cq,), (g_q_norm,), _, _ = _row_bwd(_norm_fn, toks=[(p_cq, nctx, None, 0)], vecs=[q_norm], cots=[(dcqn, 0)],
                                           tok_grads=[MXU_DTYPE], nb=nb, nl=s, tl=tl, name="q_norm_bwd")

    dkh = dkh.reshape(nb, h, lt, QK_DIM)
    dkv_heads = jnp.concatenate([dkh[..., :QK_NOPE_DIM], dvh.reshape(nb, h, lt, V_HEAD_DIM)], axis=-1)
    dkv_flat = jnp.transpose(dkv_heads, (0, 2, 1, 3)).reshape(nb * lt, -1).astype(MXU_DTYPE)
    g_w_kv_up = _mm(ckvn.reshape(nb * lt, -1), dkv_flat, ta=True, name="wg_kv_up")
    dckvn = _mm(dkv_flat, full["w_kv_up"], tb=True, name="dg_kv_up").reshape(nb, lt, KV_LORA_RANK)
    (dp_ckv,), (g_kv_norm,), _, _ = _row_bwd(_norm_fn, toks=[(p_ckv, 0, None, 0)], vecs=[kv_norm], cots=[(dckvn, 0)],
                                             tok_grads=[MXU_DTYPE], nb=nb, nl=lt, tl=tl, name="kv_norm_bwd")

    dk_rope = jnp.sum(dkh[..., QK_NOPE_DIM:], axis=1)
    g_krdt = jnp.concatenate([dk_rope, g_dt, jnp.zeros((nb, lt, pad_w - 2 * N_SSD_HEADS), F32)], axis=-1)
    (dp_krdt,), (_, g_dt_bias_row), _, _ = _row_bwd(
        _krdt_fn, toks=[(p_krdt, 0, None, 0)], poss=[cos_k, sin_k], vecs=[rot_k, dt_bias_row], cots=[(g_krdt, 0)],
        tok_grads=[MXU_DTYPE], nb=nb, nl=lt, tl=tl, name="krdt_bwd")

    zero_ctx = lambda width: jnp.zeros((nb, nctx_rows, width), MXU_DTYPE)
    dp = jnp.concatenate([jnp.concatenate([zero_ctx(Q_LORA_RANK), dp_cq], axis=1), dp_ckv,
                          jnp.concatenate([zero_ctx(D_INNER), dz], axis=1), dp_xbc, dp_krdt], axis=-1)
    dp_f = dp.reshape(nb * lt, in_p_width)
    g_w_in_p = _mm(h1f, dp_f, ta=True, name="wg_in")
    dh1 = _mm(dp_f, w_in_p, tb=True, name="dg_in").reshape(nb, lt, d)
    (du,), (g_mix_pre,), ((g_shift0, g_shift0c), (g_scale1, g_scale1c)), _ = _row_bwd(
        _pre_fn, toks=[(u, 0, None, 0)], vecs=[mix_pre_norm], bvecs=[mods[0], mods[1]], cots=[(dh1, 0)],
        tok_grads=[F32], nb=nb, nl=lt, tl=tl, nctx=nctx, name="pre1_bwd")
    grad_x = dx_res + du[:, nctx_rows:]

    zero_row = jnp.zeros((1, 1, d), F32)
    lat = [g_shift0, g_scale1, g_gate2, g_shift3, g_scale4, g_gate5]
    ctxg = [g_shift0c, g_scale1c, zero_row, zero_row, zero_row, zero_row]
    dmod = jnp.concatenate([jnp.concatenate([a, b], axis=0)[:, 0, :] for a, b in zip(lat, ctxg)], axis=-1)
    dmod = jnp.pad(dmod, ((0, mod_rows - nb - 1), (0, 0)))
    _, (g_b_mod,), _, _ = _row_bwd(_bias_fn, toks=[(mod_lin[None], 0, None, 0)], vecs=[b_mod], cots=[(dmod[None], 0)],
                                   tok_grads=[None], nb=1, nl=mod_rows, tl=mod_rows, name="mod_bias_bwd")
    g_w_mod = _mm(s_all[0], dmod, ta=True, name="wg_mod")
    ds_all = _mm(dmod, full["w_mod"], tb=True, name="dg_mod")
    (dc_all,), _, _, _ = _row_bwd(_silu_fn, toks=[(c_all, 0, None, 0)], cots=[(ds_all[None], 0)], tok_grads=[F32],
                                  nb=1, nl=mod_rows, tl=mod_rows, name="mod_silu_bwd")
    g_c_ctx = dc_all[0, nb]

    g_w_in = _unperm_in_cols(g_w_in_p[:, :IN_WIDTH])
    big_grads = {"w_mod": g_w_mod, "w_in": g_w_in, "w_q_up": g_w_q_up, "w_kv_up": g_w_kv_up, "w_out": g_w_out,
                 "w_up": g_w_up, "w_down": g_w_down}
    parts = []
    for n, rows, cols, axis in BIG:
        lr, lc = big_local[n].shape
        gfull = big_grads[n]
        shards = gfull.reshape(N_CHIPS, lr, lc) if axis == 0 else jnp.transpose(gfull.reshape(lr, N_CHIPS, lc), (1, 0, 2))
        parts.append(shards.reshape(N_CHIPS, lr * lc // PACK_COLS, PACK_COLS))
    gpack = jnp.concatenate(parts, axis=1).astype(WIRE_DTYPE)
    half = shard_rows // 2
    core = lax.axis_index("c")
    gpack = gpack.reshape(N_CHIPS, 2, half, PACK_COLS)
    got = _swap_halves(gpack, name="grad_swap")
    own = lax.dynamic_index_in_dim(gpack, core, axis=1, keepdims=False)
    flat = (N_CHIPS * half, PACK_COLS)
    chip_sum = _sum_list([own.reshape(flat), got.reshape(flat)], out_dtype=WIRE_DTYPE,
                         name="grad_add_pair").reshape(N_CHIPS, half, PACK_COLS)
    from_chips = _scatter_to_chips(chip_sum, name="grad_scatter")
    mine_sum = lax.dynamic_index_in_dim(chip_sum, chip, axis=0, keepdims=True)
    from_chips = lax.dynamic_update_slice(from_chips, mine_sum, (chip, 0, 0))
    my_half = _sum_slots(from_chips, out_dtype=F32, name="grad_add_chips")
    joined = _join_halves(my_half, name="grad_join")
    g_shard = lax.dynamic_update_slice(joined, my_half[None], (core, 0, 0)).reshape(shard_rows, PACK_COLS)

    g_d = jnp.sum(g_d_row.reshape(N_SSD_HEADS, SSD_HEAD_DIM), axis=1)[None]
    g_dt_bias = g_dt_bias_row[:, QK_ROPE_DIM:QK_ROPE_DIM + 2 * N_SSD_HEADS].reshape(1, 2, N_SSD_HEADS)
    small_names = ["c_ctx", "b_mod", "mix_pre_norm", "mix_post_norm", "q_norm", "kv_norm", "ssd_conv_w", "ssd_conv_b",
                   "ssd_a_log", "ssd_dt_bias", "ssd_d", "ssd_norm", "ffn_pre_norm", "ffn_post_norm", "ffn_conv_w",
                   "ffn_conv_b"]
    small_grads = [g_c_ctx, g_b_mod, g_mix_pre, g_mix_post, g_q_norm, g_kv_norm, g_ssd_conv_w, g_ssd_conv_b,
                   g_a_log, g_dt_bias, g_d, g_ssd_norm, g_ffn_pre, g_ffn_post, g_ffn_conv_w, g_ffn_conv_b]
    small_shapes = [tuple(np.shape(a)) for a in small_grads] + [()]
    n_small = sum(int(np.prod(shp)) for shp in small_shapes)
    small_rows = -(-n_small // (8 * LANES)) * 8
    small_all = _allgather_small(_pack_small(small_grads + [loss_part], small_rows), name="gather_small")
    small_sum = _sum_slots(small_all, out_dtype=F32, name="small_add")
    small_red = _unpack_small(small_sum, small_shapes)
    loss = small_red[-1]
    grads = dict(zip(small_names, small_red[:-1]))
    grads["ssd_conv_w"] = lax.dynamic_slice_in_dim(grads["ssd_conv_w"], chip * n_sc, n_sc, axis=1)[None]
    grads["ffn_conv_w"] = lax.dynamic_slice_in_dim(grads["ffn_conv_w"], chip * n_fc, n_fc, axis=1)[None]
    for n in small_names:
        grads[n] = grads[n].reshape(args[n].shape)

    delta, new_m, new_v = {}, {}, {}
    o = 0
    for n, _, _, _ in BIG:
        lr, lc = big_local[n].shape
        nr = lr * lc // PACK_COLS
        grads[n] = g_shard[o:o + nr].reshape(1, lr, lc)
        o += nr
        dl, nm, nv = _adamw(big_local[n], grads[n][0], args["m_" + n][0], args["v_" + n][0], name="adamw_" + n)
        delta[n], new_m[n], new_v[n] = dl[None], nm[None], nv[None]
    sm_shapes = [args[n].shape for n in small_names]
    n_sm = sum(int(np.prod(shp)) for shp in sm_shapes)
    sm_rows = -(-n_sm // (8 * LANES)) * 8
    packs = [_pack_small([src[n] for n in small_names], sm_rows)
             for src in (args, grads, {n: args["m_" + n] for n in small_names}, {n: args["v_" + n] for n in small_names})]
    for out_dict, packed_out in zip((delta, new_m, new_v), _adamw(*packs, name="adamw_small")):
        out_dict.update(zip(small_names, _unpack_small(packed_out, sm_shapes)))

    return (loss, grad_x, *[grads[n] for n in names], *[delta[n] for n in names], *[new_m[n] for n in names],
            *[new_v[n] for n in names])
```

```python
import functools
import math

import numpy as np
import jax
import jax.numpy as jnp
from jax import lax
from jax.experimental import pallas as pl
from jax.experimental.pallas import tpu as pltpu

F32 = jnp.float32
MXU_DTYPE = jnp.bfloat16
WIRE_DTYPE = jnp.bfloat16
VMEM_LIMIT_BYTES = 56 * 1024 * 1024
HIGHEST = lax.Precision.HIGHEST

D_MODEL = 1024
N_MOD = 6
EPS = 1e-6
GRID_W = 64
N_ATTN_HEADS = 16
QK_NOPE_DIM = 64
QK_ROPE_DIM = 32
QK_DIM = QK_NOPE_DIM + QK_ROPE_DIM
V_HEAD_DIM = 64
Q_LORA_RANK = 384
KV_LORA_RANK = 256
ROPE_THETA = 10000.0
ATTN_SCALE = QK_DIM ** -0.5
ATTN_WIDTH = N_ATTN_HEADS * V_HEAD_DIM
N_SSD_HEADS = 16
SSD_HEAD_DIM = 64
SSD_GROUPS = 2
HEADS_PER_GROUP = N_SSD_HEADS // SSD_GROUPS
SSD_STATE = 128
SSD_CONV = 5
SSD_CHUNK = 128
D_INNER = N_SSD_HEADS * SSD_HEAD_DIM
GN = SSD_GROUPS * SSD_STATE
XBC_WIDTH = D_INNER + 2 * GN
D_FF = 2816
FFN_CONV = 3
KRDT_WIDTH = 128
IN_WIDTH = Q_LORA_RANK + KV_LORA_RANK + QK_ROPE_DIM + D_INNER + XBC_WIDTH + 2 * N_SSD_HEADS

ADAM_LR = 0.001
ADAM_B1 = 0.9
ADAM_B2 = 0.999
ADAM_EPS = 1e-08
ADAM_WD = 0.01
ADAM_STEP = 10

N_CHIPS = 4
N_DEV = 8
MESH = pl.DeviceIdType.MESH
LANES = 128

BIG = (("w_mod", D_MODEL, N_MOD * D_MODEL, 1), ("w_in", D_MODEL, IN_WIDTH, 1),
       ("w_q_up", Q_LORA_RANK, N_ATTN_HEADS * QK_DIM, 1),
       ("w_kv_up", KV_LORA_RANK, N_ATTN_HEADS * (QK_NOPE_DIM + V_HEAD_DIM), 1),
       ("w_out", ATTN_WIDTH + D_INNER, D_MODEL, 0), ("w_up", D_MODEL, 2 * D_FF, 1),
       ("w_down", D_FF, D_MODEL, 0))
PACK_COLS = 1024


def _cparams(sem):
    return pltpu.CompilerParams(dimension_semantics=sem, vmem_limit_bytes=VMEM_LIMIT_BYTES)


def _pick(n, cands):
    for c in cands:
        if n % c == 0:
            return c
    return n


def _sigmoid(x):
    return 0.5 * (jnp.tanh(0.5 * x) + 1.0)


def _silu(x):
    return x * _sigmoid(x)


@jax.custom_vjp
def _softplus(x):
    u = jnp.exp(-jnp.abs(x))
    w = 1.0 + u
    log1p = jnp.where(w == 1.0, u, jnp.log(w) * (u / jnp.where(w == 1.0, 1.0, w - 1.0)))
    return jnp.maximum(x, 0.0) + log1p


def _softplus_fwd(x):
    return _softplus(x), x


def _softplus_bwd(x, g):
    return (g * _sigmoid(x),)


_softplus.defvjp(_softplus_fwd, _softplus_bwd)


@jax.custom_vjp
def _gelu(x):
    return 0.5 * x * (1.0 + lax.erf(x * (2.0 ** -0.5)))


def _gelu_fwd(x):
    return _gelu(x), x


def _gelu_bwd(x, g):
    cdf = 0.5 * (1.0 + lax.erf(x * (2.0 ** -0.5)))
    pdf = jnp.exp(-0.5 * x * x) * (1.0 / math.sqrt(2.0 * math.pi))
    return (g * (cdf + x * pdf),)


_gelu.defvjp(_gelu_fwd, _gelu_bwd)


def _rms(x, w):
    return x * lax.rsqrt(jnp.mean(x * x, axis=-1, keepdims=True) + EPS) * w


def _shift_rows_raw(x, off, seg):
    n = x.shape[0]
    if off == 0:
        return x
    r = pltpu.roll(x, (-off) % n, 0)
    idx = lax.broadcasted_iota(jnp.int32, x.shape, 0)
    src = idx + off
    ok = (src >= 0) & (src < n)
    if seg:
        ok = ok & ((idx < seg) == (src < seg))
    return jnp.where(ok, r, 0.0)


@functools.partial(jax.custom_vjp, nondiff_argnums=(1, 2))
def _shift_rows(x, off, seg):
    return _shift_rows_raw(x, off, seg)


def _shift_rows_fwd(x, off, seg):
    return _shift_rows_raw(x, off, seg), None


def _shift_rows_bwd(off, seg, _, g):
    return (_shift_rows_raw(g, -off, seg),)


_shift_rows.defvjp(_shift_rows_fwd, _shift_rows_bwd)


@functools.partial(jax.custom_vjp, nondiff_argnums=(1,))
def _roll_lanes(x, shift):
    return pltpu.roll(x, shift % x.shape[1], 1)


def _roll_lanes_fwd(x, shift):
    return _roll_lanes(x, shift), None


def _roll_lanes_bwd(shift, _, g):
    return (pltpu.roll(g, (-shift) % g.shape[1], 1),)


_roll_lanes.defvjp(_roll_lanes_fwd, _roll_lanes_bwd)


def _row_of(w, k):
    sel = lax.broadcasted_iota(jnp.int32, (w.shape[0], 1), 0) == k
    return jnp.sum(jnp.where(sel, w, 0.0), axis=0, keepdims=True)


def _col_of(w, k):
    sel = lax.broadcasted_iota(jnp.int32, (1, w.shape[1]), 1) == k
    return jnp.sum(jnp.where(sel, w, 0.0), axis=1, keepdims=True)


def _dwconv(x, w, seg):
    k = w.shape[0]
    acc = None
    for t in range(k):
        term = _shift_rows(x, t - k // 2, seg) * _row_of(w, t)
        acc = term if acc is None else acc + term
    return acc


def _dot(a, b, dims):
    return lax.dot_general(a.astype(MXU_DTYPE), b.astype(MXU_DTYPE), (dims, ((), ())),
                           preferred_element_type=F32)


def _dot_exact(a, b):
    return lax.dot_general(a, b, (((1,), (0,)), ((), ())), precision=HIGHEST,
                           preferred_element_type=F32)


def _mm(a, b, *, ta=False, tb=False, out_dtype=F32, name):
    if ta:
        kdim, m = a.shape
    else:
        m, kdim = a.shape
    if tb:
        n, k2 = b.shape
    else:
        k2, n = b.shape
    assert kdim == k2, (a.shape, b.shape, ta, tb)
    tm = _pick(m, (1024, 512, 384, 256, 128))
    tn = _pick(n, (512, 384, 256, 128))
    tk = kdim if kdim <= 2048 else _pick(kdim, (2048, 1664, 1536, 1408, 1024, 512, 256, 128))
    nk = kdim // tk
    a_spec = pl.BlockSpec((tk, tm), lambda i, j, k: (k, i)) if ta else pl.BlockSpec((tm, tk), lambda i, j, k: (i, k))
    b_spec = pl.BlockSpec((tn, tk), lambda i, j, k: (j, k)) if tb else pl.BlockSpec((tk, tn), lambda i, j, k: (k, j))
    dims = ((0,) if ta else (1,), (1,) if tb else (0,))

    def body(a_ref, b_ref, o_ref, *scratch):
        if nk == 1:
            o_ref[...] = _dot(a_ref[...], b_ref[...], dims).astype(o_ref.dtype)
            return
        acc_ref, = scratch
        k = pl.program_id(2)

        @pl.when(k == 0)
        def _():
            acc_ref[...] = jnp.zeros_like(acc_ref)

        acc_ref[...] += _dot(a_ref[...], b_ref[...], dims)

        @pl.when(k == nk - 1)
        def _():
            o_ref[...] = acc_ref[...].astype(o_ref.dtype)

    return pl.pallas_call(
        body, name=name, grid=(m // tm, n // tn, nk),
        in_specs=[a_spec, b_spec], out_specs=pl.BlockSpec((tm, tn), lambda i, j, k: (i, j)),
        out_shape=jax.ShapeDtypeStruct((m, n), out_dtype),
        scratch_shapes=[pltpu.VMEM((tm, tn), F32)] if nk > 1 else [],
        compiler_params=_cparams(("parallel", "parallel", "arbitrary")),
    )(a, b)


def _mm_sum(pairs, *, tb=False, out_dtype=F32, name):
    m = pairs[0][0].shape[0]
    n = pairs[0][1].shape[0] if tb else pairs[0][1].shape[1]
    tm = _pick(m, (1024, 512, 384, 256, 128))
    tn = _pick(n, (512, 384, 256, 128))
    specs, args = [], []
    for a, b in pairs:
        kdim = a.shape[1]
        specs.append(pl.BlockSpec((tm, kdim), lambda i, j: (i, 0)))
        specs.append(pl.BlockSpec((tn, kdim), lambda i, j: (j, 0)) if tb else pl.BlockSpec((kdim, tn), lambda i, j: (0, j)))
        args += [a, b]
    dims = ((1,), (1,) if tb else (0,))

    def body(*refs):
        acc = None
        for t in range(len(pairs)):
            term = _dot(refs[2 * t][...], refs[2 * t + 1][...], dims)
            acc = term if acc is None else acc + term
        refs[-1][...] = acc.astype(refs[-1].dtype)

    return pl.pallas_call(
        body, name=name, grid=(m // tm, n // tn), in_specs=specs,
        out_specs=pl.BlockSpec((tm, tn), lambda i, j: (i, j)), out_shape=jax.ShapeDtypeStruct((m, n), out_dtype),
        compiler_params=_cparams(("parallel", "parallel")),
    )(*args)


def _row_specs(toks, poss, vecs, bvecs, tl, nctx, nb):
    specs, args = [], []
    for arr, off, cw, ci in toks:
        cw = arr.shape[2] if cw is None else cw
        specs.append(pl.BlockSpec((1, tl, cw), lambda b, l, off=off, ci=ci: (b, l + off, ci)))
        args.append(arr)
    for arr in poss:
        specs.append(pl.BlockSpec((1, tl, arr.shape[2]), lambda b, l: (0, l, 0)))
        args.append(arr)
    for arr in vecs:
        specs.append(pl.BlockSpec(arr.shape, lambda b, l: (0, 0)))
        args.append(arr)
    for arr in bvecs:
        if nctx:
            specs.append(pl.BlockSpec((1, 1, arr.shape[2]), lambda b, l: (jnp.where(l < nctx, nb, b), 0, 0)))
        else:
            specs.append(pl.BlockSpec((1, 1, arr.shape[2]), lambda b, l: (b, 0, 0)))
        args.append(arr)
    return specs, args


def _row_fwd(fn, *, toks, poss=(), vecs=(), bvecs=(), outs, nb, nl, tl, nctx=0, name):
    nt, npos, nv, nbv = len(toks), len(poss), len(vecs), len(bvecs)
    specs, args = _row_specs(toks, poss, vecs, bvecs, tl, nctx, nb)

    def body(*refs):
        ins, os = refs[:len(specs)], refs[len(specs):]
        tv = [r[0].astype(F32) for r in ins[:nt]]
        pv = [r[0] for r in ins[nt:nt + npos]]
        vv = [r[...] for r in ins[nt + npos:nt + npos + nv]]
        bv = [r[0] for r in ins[nt + npos + nv:]]
        res = fn(*tv, *pv, *vv, *bv)
        for o, r in zip(os, res):
            o[0] = r.astype(o.dtype)

    return pl.pallas_call(
        body, name=name, grid=(nb, nl // tl), in_specs=specs,
        out_specs=[pl.BlockSpec((1, tl, c), lambda b, l: (b, l, 0)) for c, _ in outs],
        out_shape=[jax.ShapeDtypeStruct((nb, nl, c), dt) for c, dt in outs],
        compiler_params=_cparams(("parallel", "parallel")),
    )(*args)


def _row_bwd(fn, *, toks, poss=(), vecs=(), bvecs=(), cots, tok_grads, emit=(), nb, nl, tl, nctx=0, name,
             drop_blocks=0):
    nt, npos, nv, nbv = len(toks), len(poss), len(vecs), len(bvecs)
    specs, args = _row_specs(toks, poss, vecs, bvecs, tl, nctx, nb)
    n_in = len(specs)
    cot_slots = []
    for arr, off in cots:
        if arr is None:
            cot_slots.append(None)
            continue
        cot_slots.append((len(specs), off))
        specs.append(pl.BlockSpec((1, tl, arr.shape[2]), lambda b, l, off=off: (b, jnp.maximum(l + off, 0), 0)))
        args.append(arr)
    n_all_in = len(specs)

    out_specs, out_shapes = [], []
    tok_out = []
    for (arr, off, cw, ci), dt in zip(toks, tok_grads):
        if dt is None:
            tok_out.append(None)
            continue
        cw = arr.shape[2] if cw is None else cw
        tok_out.append(len(out_specs))
        out_specs.append(pl.BlockSpec((1, tl, cw), lambda b, l: (b, jnp.maximum(l - drop_blocks, 0), 0)))
        out_shapes.append(jax.ShapeDtypeStruct((nb, nl - drop_blocks * tl, cw), dt))
    vec_out = []
    for arr in vecs:
        vec_out.append(len(out_specs))
        out_specs.append(pl.BlockSpec(arr.shape, lambda b, l: (0, 0)))
        out_shapes.append(jax.ShapeDtypeStruct(arr.shape, F32))
    bv_out = []
    for arr in bvecs:
        c = arr.shape[2]
        lat = len(out_specs)
        out_specs.append(pl.BlockSpec((1, 1, c), lambda b, l: (b, 0, 0)))
        out_shapes.append(jax.ShapeDtypeStruct((nb, 1, c), F32))
        ctx = None
        if nctx:
            ctx = len(out_specs)
            out_specs.append(pl.BlockSpec((1, 1, c), lambda b, l: (0, 0, 0)))
            out_shapes.append(jax.ShapeDtypeStruct((1, 1, c), F32))
        bv_out.append((lat, ctx))
    emit_out = []
    emit_cols = {}
    for idx, c, dt in emit:
        emit_out.append((idx, len(out_specs)))
        out_specs.append(pl.BlockSpec((1, tl, c), lambda b, l: (b, l, 0)))
        out_shapes.append(jax.ShapeDtypeStruct((nb, nl, c), dt))

    def body(*refs):
        ins, os = refs[:n_all_in], refs[n_all_in:]
        b, l = pl.program_id(0), pl.program_id(1)
        tv = [r[0].astype(F32) for r in ins[:nt]]
        pv = [r[0] for r in ins[nt:nt + npos]]
        vv = [r[...] for r in ins[nt + npos:nt + npos + nv]]
        bv = [r[0] for r in ins[nt + npos + nv:n_in]]

        def f(*d):
            return tuple(fn(*d[:nt], *pv, *d[nt:]))

        res, vjp = jax.vjp(f, *tv, *vv, *bv)
        cts = []
        for r, slot in zip(res, cot_slots):
            if slot is None:
                cts.append(jnp.zeros_like(r))
            else:
                i, off = slot
                ct = ins[i][0].astype(F32)
                if off < 0:
                    ct = jnp.where(l + off >= 0, ct, 0.0)
                cts.append(ct)
        grads = vjp(tuple(cts))

        for g, slot in zip(grads[:nt], tok_out):
            if slot is not None:
                os[slot][0] = g.astype(os[slot].dtype)

        @pl.when((b == 0) & (l == 0))
        def _():
            for slot in vec_out:
                os[slot][...] = jnp.zeros_like(os[slot])
            for _, ctx in bv_out:
                if ctx is not None:
                    os[ctx][...] = jnp.zeros_like(os[ctx])

        @pl.when(l == 0)
        def _():
            for lat, _ in bv_out:
                os[lat][...] = jnp.zeros_like(os[lat])

        for g, slot in zip(grads[nt:nt + nv], vec_out):
            os[slot][...] += g
        for g, (lat, ctx) in zip(grads[nt + nv:], bv_out):
            if ctx is None:
                os[lat][0] += g
            else:
                is_ctx = l < nctx
                os[lat][0] += jnp.where(is_ctx, 0.0, g)
                os[ctx][0] += jnp.where(is_ctx, g, 0.0)
        for idx, slot in emit_out:
            os[slot][0] = res[idx].astype(os[slot].dtype)

    out = pl.pallas_call(
        body, name=name, grid=(nb, nl // tl), in_specs=specs, out_specs=out_specs, out_shape=out_shapes,
        compiler_params=_cparams(("arbitrary", "arbitrary")),
    )(*args)
    tg = [None if s is None else out[s] for s in tok_out]
    vg = [out[s] for s in vec_out]
    bg = [(out[lat], None if ctx is None else out[ctx]) for lat, ctx in bv_out]
    em = [out[s] for _, s in emit_out]
    return tg, vg, bg, em


def _seq_specs(toks, vecs, nl, cb):
    specs, args = [], []
    for arr, off, mult in toks:
        specs.append(pl.BlockSpec((1, nl, cb * mult), lambda j, b, off=off: (b, 0, j + off)))
        args.append(arr)
    for arr, off in vecs:
        specs.append(pl.BlockSpec((arr.shape[0], cb), lambda j, b, off=off: (0, j + off)))
        args.append(arr)
    return specs, args


def _seq_fwd(fn, *, toks, vecs, outs, nb, nl, nc, cb, name):
    nt = len(toks)
    specs, args = _seq_specs(toks, vecs, nl, cb)

    def body(*refs):
        ins, os = refs[:len(specs)], refs[len(specs):]
        tv = [r[0].astype(F32) for r in ins[:nt]]
        vv = [r[...] for r in ins[nt:]]
        for o, r in zip(os, fn(*tv, *vv)):
            o[0] = r.astype(o.dtype)

    return pl.pallas_call(
        body, name=name, grid=(nc // cb, nb), in_specs=specs,
        out_specs=[pl.BlockSpec((1, nl, cb), lambda j, b: (b, 0, j)) for _ in outs],
        out_shape=[jax.ShapeDtypeStruct((nb, nl, nc), dt) for dt in outs],
        compiler_params=_cparams(("parallel", "parallel")),
    )(*args)


def _seq_bwd(fn, *, toks, vecs, cots, tok_grads, nb, nl, nc, cb, name):
    nt, nv = len(toks), len(vecs)
    specs, args = _seq_specs(toks, vecs, nl, cb)
    n_in = len(specs)
    for arr in cots:
        specs.append(pl.BlockSpec((1, nl, cb), lambda j, b: (b, 0, j)))
        args.append(arr)
    out_specs, out_shapes = [], []
    for (_, _, mult), dt in zip(toks, tok_grads):
        out_specs.append(pl.BlockSpec((1, nl, cb * mult), lambda j, b: (b, 0, j)))
        out_shapes.append(jax.ShapeDtypeStruct((nb, nl, nc * mult), dt))
    for arr, _ in vecs:
        out_specs.append(pl.BlockSpec((arr.shape[0], cb), lambda j, b: (0, j)))
        out_shapes.append(jax.ShapeDtypeStruct((arr.shape[0], nc), F32))

    def body(*refs):
        ins, os = refs[:len(specs)], refs[len(specs):]
        b = pl.program_id(1)
        tv = [r[0].astype(F32) for r in ins[:nt]]
        vv = [r[...] for r in ins[nt:n_in]]
        _, vjp = jax.vjp(lambda *d: tuple(fn(*d)), *tv, *vv)
        grads = vjp(tuple(r[0].astype(F32) for r in ins[n_in:]))
        for g, o in zip(grads[:nt], os[:nt]):
            o[0] = g.astype(o.dtype)

        @pl.when(b == 0)
        def _():
            for o in os[nt:]:
                o[...] = jnp.zeros_like(o)

        for g, o in zip(grads[nt:], os[nt:]):
            o[...] += g

    out = pl.pallas_call(
        body, name=name, grid=(nc // cb, nb), in_specs=specs, out_specs=out_specs, out_shape=out_shapes,
        compiler_params=_cparams(("parallel", "arbitrary")),
    )(*args)
    return out[:nt], out[nt:]


EXP2_SCALE = ATTN_SCALE * math.log2(math.e)


HEAD_TILE = 128
N_HEAD_PAIRS = N_ATTN_HEADS // 2


def _head_lanes():
    lane = lax.broadcasted_iota(jnp.int32, (1, HEAD_TILE), 1)
    return lane < QK_NOPE_DIM, (lane >= QK_NOPE_DIM) & (lane < QK_DIM)


def _attn_specs(tq, lk):
    q = pl.BlockSpec((1, tq, 2 * HEAD_TILE), lambda b, pr, j: (b, j, pr))
    kv = pl.BlockSpec((1, lk, 2 * HEAD_TILE), lambda b, pr, j: (b, 0, pr))
    kr = pl.BlockSpec((1, lk, HEAD_TILE), lambda b, pr, j: (b, 0, 0))
    o = pl.BlockSpec((1, tq, HEAD_TILE), lambda b, pr, j: (b, j, pr))
    lse = pl.BlockSpec((1, 2, tq, 1), lambda b, pr, j: (b, pr, j, 0))
    return q, kv, kr, o, lse


def _attn_fwd(q, kv, kr, *, tq, name):
    nb, s, _ = q.shape
    lk = kv.shape[1]
    qs, kvs, krs, os_, lses = _attn_specs(tq, lk)

    def body(q_ref, kv_ref, kr_ref, o_ref, lse_ref):
        low, _ = _head_lanes()
        outs = []
        for e in range(2):
            tile = pl.ds(HEAD_TILE * e, HEAD_TILE)
            kv_e = kv_ref[0, :, tile]
            keys = jnp.where(low, kv_e, kr_ref[0])
            sc = _dot(q_ref[0, :, tile], keys, ((1,), (1,)))
            m = jnp.max(sc, axis=-1, keepdims=True)
            p = jnp.exp2((sc - m) * EXP2_SCALE)
            denom = jnp.sum(p, axis=-1, keepdims=True)
            outs.append(_dot(p, kv_e, ((1,), (0,))) / denom)
            lse_ref[0, e] = m * EXP2_SCALE + jnp.log2(denom)
        o_ref[0] = jnp.where(low, pltpu.roll(outs[0], V_HEAD_DIM, 1), outs[1])

    return pl.pallas_call(
        body, name=name, grid=(nb, N_HEAD_PAIRS, s // tq), in_specs=[qs, kvs, krs], out_specs=[os_, lses],
        out_shape=[jax.ShapeDtypeStruct((nb, s, ATTN_WIDTH), F32), jax.ShapeDtypeStruct((nb, N_ATTN_HEADS, s, 1), F32)],
        compiler_params=_cparams(("parallel", "parallel", "parallel")),
    )(q, kv, kr)


def _attn_bwd(q, kv, kr, o, lse, do, *, tq, name):
    nb, s, _ = q.shape
    lk = kv.shape[1]
    nj = s // tq
    qs, kvs, krs, os_, lses = _attn_specs(tq, lk)

    def body(q_ref, kv_ref, kr_ref, o_ref, lse_ref, do_ref, dq_ref, dkv_ref, dkr_ref):
        pr, j = pl.program_id(1), pl.program_id(2)
        low, rope = _head_lanes()
        do_pair = do_ref[0]
        prod = do_pair * o_ref[0]

        @pl.when(j == 0)
        def _():
            dkv_ref[...] = jnp.zeros_like(dkv_ref)

        @pl.when((pr == 0) & (j == 0))
        def _():
            dkr_ref[...] = jnp.zeros_like(dkr_ref)

        dkr = None
        for e in range(2):
            tile = pl.ds(HEAD_TILE * e, HEAD_TILE)
            delta = jnp.sum(jnp.where(low if e == 0 else ~low, prod, 0.0), axis=-1, keepdims=True)
            do_e = jnp.where(low, 0.0, do_pair if e == 1 else pltpu.roll(do_pair, V_HEAD_DIM, 1))
            kv_e, q_e = kv_ref[0, :, tile], q_ref[0, :, tile]
            keys = jnp.where(low, kv_e, kr_ref[0])
            sc = _dot(q_e, keys, ((1,), (1,)))
            p = jnp.exp2(sc * EXP2_SCALE - lse_ref[0, e])
            dp = _dot(do_e, kv_e, ((1,), (1,)))
            ds = (p * (dp - delta)).astype(MXU_DTYPE)
            dq_ref[0, :, tile] = _dot(ds, keys, ((1,), (0,))) * ATTN_SCALE
            dkeys = _dot(ds, q_e, ((0,), (0,)))
            dv = _dot(p, do_e, ((0,), (0,)))
            dkv_ref[0, :, tile] += jnp.where(low, dkeys, dv)
            part = jnp.where(rope, dkeys, 0.0)
            dkr = part if dkr is None else dkr + part
        dkr_ref[0] += dkr

        @pl.when(j == nj - 1)
        def _():
            for e in range(2):
                tile = pl.ds(HEAD_TILE * e, HEAD_TILE)
                dkv_ref[0, :, tile] = dkv_ref[0, :, tile] * jnp.where(low, ATTN_SCALE, 1.0)

        @pl.when((pr == N_HEAD_PAIRS - 1) & (j == nj - 1))
        def _():
            dkr_ref[0] = dkr_ref[0] * ATTN_SCALE

    return pl.pallas_call(
        body, name=name, grid=(nb, N_HEAD_PAIRS, nj), in_specs=[qs, kvs, krs, os_, lses, os_],
        out_specs=[qs, kvs, krs],
        out_shape=[jax.ShapeDtypeStruct(q.shape, F32), jax.ShapeDtypeStruct(kv.shape, F32),
                   jax.ShapeDtypeStruct(kr.shape, F32)],
        compiler_params=_cparams(("parallel", "arbitrary", "arbitrary")),
    )(q, kv, kr, o, lse, do)


N_PAIRS = HEADS_PER_GROUP // 2
PAIR_W = 2 * SSD_HEAD_DIM


def _ssd_chunk(states, xs, dtc, dtr, bm, cm, ac, ar, *, reverse):
    q = dtc.shape[0]
    row = lax.broadcasted_iota(jnp.int32, (q, q), 0)
    col = lax.broadcasted_iota(jnp.int32, (q, q), 1)
    if reverse:
        tri_c, tri_r, mask = col < row, row < col, col >= row
    else:
        tri_c, tri_r, mask = col <= row, row <= col, col <= row
    a_col, a_row = dtc * ac, dtr * ar
    cum_c = _dot_exact(tri_c.astype(F32), a_col)
    cum_r = _dot_exact(a_row, tri_r.astype(F32))
    tot = jnp.sum(a_col, axis=0, keepdims=True)
    cb = _dot(cm, bm, ((1,), (1,)))
    first = lax.broadcasted_iota(jnp.int32, (1, PAIR_W), 1) < SSD_HEAD_DIM
    first_rows = lax.broadcasted_iota(jnp.int32, (PAIR_W, 1), 0) < SSD_HEAD_DIM
    ys, new_states = [], []
    for pr in range(N_PAIRS):
        per_head = []
        for h in range(2):
            e = 2 * pr + h
            cc, cr, dc, te = _col_of(cum_c, e), _row_of(cum_r, e), _col_of(dtc, e), _col_of(tot, e)
            if reverse:
                within = jnp.exp(jnp.where(mask, cr - cc, -jnp.inf))
                into, to_end = jnp.exp(te - cc), jnp.exp(cc)
            else:
                within = jnp.exp(jnp.where(mask, cc - cr, -jnp.inf))
                into, to_end = jnp.exp(cc), jnp.exp(te - cc)
            per_head.append((cb * within, dc, into, to_end, jnp.exp(te)))
        (m0, dc0, in0, end0, t0), (m1, dc1, in1, end1, t1) = per_head
        xd = xs[pr] * jnp.where(first, dc0, dc1)
        y_diag = (_dot(m0, jnp.where(first, xd, 0.0), ((1,), (0,)))
                  + _dot(m1, jnp.where(first, 0.0, xd), ((1,), (0,))))
        y_off = _dot(cm, states[pr], ((1,), (1,))) * jnp.where(first, in0, in1)
        ys.append(y_diag + y_off)
        grow = _dot(xd * jnp.where(first, end0, end1), bm, ((0,), (0,)))
        new_states.append(states[pr] * jnp.where(first_rows, t0, t1) + grow)
    return tuple(ys) + tuple(new_states)


def _chunk_of_step(t, ncc, nch, reverse):
    if not reverse:
        return t
    return jnp.where(t < ncc, ncc - 1 - t, nch - 1 - (t - ncc))


X_COLS = D_INNER // SSD_GROUPS
GROUP_COLS = X_COLS + 2 * SSD_STATE


def _scan_in_specs(nch, ncc, reverse, back):
    q, e = SSD_CHUNK, HEADS_PER_GROUP

    def ch(t):
        return _chunk_of_step((nch - 1 - t) if back else t, ncc, nch, reverse)

    return ch, [
        pl.BlockSpec((1, q, GROUP_COLS), lambda b, g, t: (b, ch(t), g)),
        pl.BlockSpec((1, 1, q, e), lambda b, g, t: (b, g, ch(t), 0)),
        pl.BlockSpec((1, 1, e, q), lambda b, g, t: (b, g, 0, ch(t))),
        pl.BlockSpec((1, 1, e), lambda b, g, t: (g, 0, 0)),
        pl.BlockSpec((1, e, 1), lambda b, g, t: (g, 0, 0)),
        pl.BlockSpec((1, 1, X_COLS), lambda b, g, t: (g, 0, 0)),
    ]


def _scan_chunk_fn(reverse, skip):
    def f(states, xs, dtc, dtr, bm, cm, ac, ar, d):
        res = _ssd_chunk(states, xs, dtc, dtr, bm, cm, ac, ar, reverse=reverse)
        if not skip:
            return res
        ys = tuple(res[i] + d[:, PAIR_W * i:PAIR_W * (i + 1)] * xs[i] for i in range(N_PAIRS))
        return ys + tuple(res[N_PAIRS:])

    return f


def _scan_operands(x_ref, dtc_ref, dtr_ref, ac_ref, ar_ref, d_ref):
    xs = [x_ref[0, :, pl.ds(PAIR_W * i, PAIR_W)] for i in range(N_PAIRS)]
    bm = x_ref[0, :, pl.ds(X_COLS, SSD_STATE)]
    cm = x_ref[0, :, pl.ds(X_COLS + SSD_STATE, SSD_STATE)]
    return xs, dtc_ref[0, 0], dtr_ref[0, 0], bm, cm, ac_ref[0], ar_ref[0], d_ref[0]


def _scan_fwd(xbc, dtc, dtr, ac, ar, d, *, ncc, reverse, skip, name):
    nb, lt, _ = xbc.shape
    q, n = SSD_CHUNK, SSD_STATE
    nch = lt // q
    ch, in_specs = _scan_in_specs(nch, ncc, reverse, False)
    f = _scan_chunk_fn(reverse, skip)

    def body(x_ref, dtc_ref, dtr_ref, ac_ref, ar_ref, d_ref, y_ref, ent_ref, st_ref):
        t = pl.program_id(2)

        @pl.when(t == 0)
        def _():
            st_ref[...] = jnp.zeros_like(st_ref)

        states = [st_ref[i] for i in range(N_PAIRS)]
        for i in range(N_PAIRS):
            ent_ref[0, 0, 0, i] = states[i]
        res = f(states, *_scan_operands(x_ref, dtc_ref, dtr_ref, ac_ref, ar_ref, d_ref))
        for i in range(N_PAIRS):
            y_ref[0, :, pl.ds(PAIR_W * i, PAIR_W)] = res[i]
            st_ref[i] = res[N_PAIRS + i]

    return pl.pallas_call(
        body, name=name, grid=(nb, SSD_GROUPS, nch), in_specs=in_specs,
        out_specs=[pl.BlockSpec((1, q, X_COLS), lambda b, g, t: (b, ch(t), g)),
                   pl.BlockSpec((1, 1, 1, N_PAIRS, PAIR_W, n), lambda b, g, t: (b, g, t, 0, 0, 0))],
        out_shape=[jax.ShapeDtypeStruct((nb, lt, D_INNER), F32),
                   jax.ShapeDtypeStruct((nb, SSD_GROUPS, nch, N_PAIRS, PAIR_W, n), F32)],
        scratch_shapes=[pltpu.VMEM((N_PAIRS, PAIR_W, n), F32)],
        compiler_params=_cparams(("parallel", "parallel", "arbitrary")),
    )(xbc, dtc, dtr, ac, ar, d)


def _scan_bwd(xbc, dtc, dtr, ac, ar, d, entering, dy, prev, *, ncc, reverse, skip, name):
    nb, lt, _ = xbc.shape
    q, n, e = SSD_CHUNK, SSD_STATE, HEADS_PER_GROUP
    nch = lt // q
    ch, in_specs = _scan_in_specs(nch, ncc, reverse, True)
    group_spec = pl.BlockSpec((1, q, GROUP_COLS), lambda b, g, t: (b, ch(t), g))
    in_specs = in_specs + [
        pl.BlockSpec((1, 1, 1, N_PAIRS, PAIR_W, n), lambda b, g, t: (b, g, nch - 1 - t, 0, 0, 0)),
        pl.BlockSpec((1, q, X_COLS), lambda b, g, t: (b, ch(t), g)),
    ] + ([] if prev is None else [group_spec])
    f = _scan_chunk_fn(reverse, skip)

    def body(x_ref, dtc_ref, dtr_ref, ac_ref, ar_ref, d_ref, ent_ref, dy_ref, *rest):
        prev_ref = None if prev is None else rest[0]
        dx_ref, ddtc_ref, ddtr_ref, dac_ref, dar_ref, dd_ref, ds_ref = rest[0 if prev is None else 1:]
        t = pl.program_id(2)

        @pl.when(t == 0)
        def _():
            ds_ref[...] = jnp.zeros_like(ds_ref)
            dac_ref[...] = jnp.zeros_like(dac_ref)
            dar_ref[...] = jnp.zeros_like(dar_ref)
            dd_ref[...] = jnp.zeros_like(dd_ref)

        states = [ent_ref[0, 0, 0, i] for i in range(N_PAIRS)]
        _, vjp = jax.vjp(f, states, *_scan_operands(x_ref, dtc_ref, dtr_ref, ac_ref, ar_ref, d_ref))
        dys = [dy_ref[0, :, pl.ds(PAIR_W * i, PAIR_W)] for i in range(N_PAIRS)]
        gs, gx, gdtc, gdtr, gb, gc, gac, gar, gd = vjp(tuple(dys) + tuple(ds_ref[i] for i in range(N_PAIRS)))
        parts = list(gx) + [gb, gc]
        o = 0
        for i, part in enumerate(parts):
            cols = pl.ds(o, part.shape[1])
            dx_ref[0, :, cols] = part if prev_ref is None else part + prev_ref[0, :, cols]
            o += part.shape[1]
        for i in range(N_PAIRS):
            ds_ref[i] = gs[i]
        ddtc_ref[0, 0] = gdtc
        ddtr_ref[0, 0] = gdtr
        dac_ref[0, 0] += gac
        dar_ref[0, 0] += gar
        dd_ref[0, 0] += gd

    args = (xbc, dtc, dtr, ac, ar, d, entering, dy) + (() if prev is None else (prev,))
    return pl.pallas_call(
        body, name=name, grid=(nb, SSD_GROUPS, nch), in_specs=in_specs,
        out_specs=[group_spec,
                   pl.BlockSpec((1, 1, q, e), lambda b, g, t: (b, g, ch(t), 0)),
                   pl.BlockSpec((1, 1, e, q), lambda b, g, t: (b, g, 0, ch(t))),
                   pl.BlockSpec((1, 1, 1, e), lambda b, g, t: (b, g, 0, 0)),
                   pl.BlockSpec((1, 1, e, 1), lambda b, g, t: (b, g, 0, 0)),
                   pl.BlockSpec((1, 1, 1, X_COLS), lambda b, g, t: (b, g, 0, 0))],
        out_shape=[jax.ShapeDtypeStruct((nb, lt, SSD_GROUPS * GROUP_COLS), F32),
                   jax.ShapeDtypeStruct((nb, SSD_GROUPS, lt, e), F32), jax.ShapeDtypeStruct((nb, SSD_GROUPS, e, lt), F32),
                   jax.ShapeDtypeStruct((nb, SSD_GROUPS, 1, e), F32), jax.ShapeDtypeStruct((nb, SSD_GROUPS, e, 1), F32),
                   jax.ShapeDtypeStruct((nb, SSD_GROUPS, 1, X_COLS), F32)],
        scratch_shapes=[pltpu.VMEM((N_PAIRS, PAIR_W, n), F32)],
        compiler_params=_cparams(("parallel", "parallel", "arbitrary")),
    )(*args)


def _adamw(w, g, m, v, *, name):
    r, c = w.shape
    tr = _pick(r, (256, 176, 128, 96, 64, 8))
    c1 = 1.0 / (1.0 - ADAM_B1 ** ADAM_STEP)
    c2 = 1.0 / (1.0 - ADAM_B2 ** ADAM_STEP)

    def body(w_ref, g_ref, m_ref, v_ref, d_ref, nm_ref, nv_ref):
        gv = g_ref[...]
        nm = ADAM_B1 * m_ref[...] + (1.0 - ADAM_B1) * gv
        nv = ADAM_B2 * v_ref[...] + (1.0 - ADAM_B2) * (gv * gv)
        d_ref[...] = -ADAM_LR * ((nm * c1) / (jnp.sqrt(nv * c2) + ADAM_EPS) + ADAM_WD * w_ref[...])
        nm_ref[...] = nm
        nv_ref[...] = nv

    spec = pl.BlockSpec((tr, c), lambda i: (i, 0))
    return pl.pallas_call(
        body, name=name, grid=(r // tr,), in_specs=[spec] * 4, out_specs=[spec] * 3,
        out_shape=[jax.ShapeDtypeStruct((r, c), F32)] * 3, compiler_params=_cparams(("parallel",)),
    )(w, g, m, v)


def _sum_rows_tile(r):
    return r if r <= 1024 else _pick(r, (656, 512, 256, 128, 64, 32, 16))


def _sum_slots(x, *, out_dtype, name):
    n, r, c = x.shape
    tr = _sum_rows_tile(r)

    def body(x_ref, o_ref):
        acc = x_ref[0].astype(F32)
        for k in range(1, n):
            acc = acc + x_ref[k].astype(F32)
        o_ref[...] = acc.astype(o_ref.dtype)

    return pl.pallas_call(
        body, name=name, grid=(r // tr,), in_specs=[pl.BlockSpec((n, tr, c), lambda i: (0, i, 0))],
        out_specs=pl.BlockSpec((tr, c), lambda i: (i, 0)), out_shape=jax.ShapeDtypeStruct((r, c), out_dtype),
        compiler_params=_cparams(("parallel",)),
    )(x)


def _sum_list(xs, *, out_dtype, name):
    r, c = xs[0].shape
    tr = _sum_rows_tile(r)

    def body(*refs):
        acc = refs[0][...].astype(F32)
        for ref in refs[1:-1]:
            acc = acc + ref[...].astype(F32)
        refs[-1][...] = acc.astype(refs[-1].dtype)

    spec = pl.BlockSpec((tr, c), lambda i: (i, 0))
    return pl.pallas_call(
        body, name=name, grid=(r // tr,), in_specs=[spec] * len(xs), out_specs=spec,
        out_shape=jax.ShapeDtypeStruct((r, c), out_dtype), compiler_params=_cparams(("parallel",)),
    )(*xs)


ANY = pl.BlockSpec(memory_space=pl.ANY)


def _place():
    return lax.axis_index("x"), lax.axis_index("y"), lax.axis_index("c")


def _allgather_small(v, *, name):
    r, c = v.shape

    def body(v_ref, out_ref, send_sems, recv_sems, local_sem):
        x, y, cc = _place()
        me = 4 * x + 2 * y + cc
        mine = pltpu.make_async_copy(v_ref, out_ref.at[me], local_sem)
        mine.start()
        copies = []
        for k in range(1, N_DEV):
            fx, fy, fc = (k >> 2) & 1, (k >> 1) & 1, k & 1
            peer = (1 - x if fx else x, 1 - y if fy else y, 1 - cc if fc else cc)
            copies.append(pltpu.make_async_remote_copy(
                src_ref=v_ref, dst_ref=out_ref.at[me], send_sem=send_sems.at[k - 1], recv_sem=recv_sems.at[k - 1],
                device_id=peer, device_id_type=MESH))
        for cp in copies:
            cp.start()
        for cp in copies:
            cp.wait()
        mine.wait()

    return pl.pallas_call(
        body, name=name, in_specs=[ANY], out_specs=ANY, out_shape=jax.ShapeDtypeStruct((N_DEV, r, c), v.dtype),
        scratch_shapes=[pltpu.SemaphoreType.DMA((N_DEV - 1,)), pltpu.SemaphoreType.DMA((N_DEV - 1,)),
                        pltpu.SemaphoreType.DMA],
    )(v)


def _other_chips(x, y):
    return [(1 - x, y), (x, 1 - y), (1 - x, 1 - y)]


def _gather_shards(mine, *, name):
    r, c = mine.shape
    half = r // 2

    def body(v_ref, out_ref, send_sems, recv_sems):
        x, y, cc = _place()
        sibling = (x, y, 1 - cc)
        chips = _other_chips(x, y)

        def rows(px, py, pc):
            return out_ref.at[2 * px + py, pl.ds(pc * half, half), :]

        def copy(k, block, to, src=None):
            return pltpu.make_async_remote_copy(
                src_ref=rows(*block) if src is None else src, dst_ref=rows(*block),
                send_sem=send_sems.at[k], recv_sem=recv_sems.at[k], device_id=to, device_id_type=MESH)

        my_half = v_ref.at[pl.ds(cc * half, half), :]
        first = [copy(j, (x, y, cc), (*chip, cc), src=my_half) for j, chip in enumerate(chips)]
        for cp in first:
            cp.start()
        passed = [copy(3 + j, (*chip, cc), sibling) for j, chip in enumerate(chips)]
        for j, chip in enumerate(chips):
            copy(j, (*chip, cc), (x, y, cc)).wait_recv()
            passed[j].start()
        for j, chip in enumerate(chips):
            copy(3 + j, (*chip, 1 - cc), (x, y, cc)).wait_recv()
        for cp in first + passed:
            cp.wait_send()

    return pl.pallas_call(
        body, name=name, in_specs=[ANY], out_specs=ANY, out_shape=jax.ShapeDtypeStruct((N_CHIPS, r, c), mine.dtype),
        scratch_shapes=[pltpu.SemaphoreType.DMA((6,)), pltpu.SemaphoreType.DMA((6,))],
    )(mine)


def _swap_halves(g, *, name):
    n, _, r, c = g.shape

    def body(g_ref, got_ref, send_sems, recv_sems):
        x, y, cc = _place()
        sibling = (x, y, 1 - cc)
        rems = []
        for j in range(n):
            rems.append(pltpu.make_async_remote_copy(
                src_ref=g_ref.at[j, 1 - cc], dst_ref=got_ref.at[j], send_sem=send_sems.at[j],
                recv_sem=recv_sems.at[j], device_id=sibling, device_id_type=MESH))
        for cp in rems:
            cp.start()
        for cp in rems:
            cp.wait()

    return pl.pallas_call(
        body, name=name, in_specs=[ANY], out_specs=ANY, out_shape=jax.ShapeDtypeStruct((n, r, c), g.dtype),
        scratch_shapes=[pltpu.SemaphoreType.DMA((n,)), pltpu.SemaphoreType.DMA((n,))],
    )(g)


def _scatter_to_chips(s, *, name):
    n, r, c = s.shape

    def body(s_ref, out_ref, send_sems, recv_sems):
        x, y, cc = _place()
        me = 2 * x + y
        copies = []
        for j, (px, py) in enumerate(_other_chips(x, y)):
            copies.append(pltpu.make_async_remote_copy(
                src_ref=s_ref.at[2 * px + py], dst_ref=out_ref.at[me], send_sem=send_sems.at[j],
                recv_sem=recv_sems.at[j], device_id=(px, py, cc), device_id_type=MESH))
        for cp in copies:
            cp.start()
        for cp in copies:
            cp.wait()

    return pl.pallas_call(
        body, name=name, in_specs=[ANY], out_specs=ANY, out_shape=jax.ShapeDtypeStruct((n, r, c), s.dtype),
        scratch_shapes=[pltpu.SemaphoreType.DMA((3,)), pltpu.SemaphoreType.DMA((3,))],
    )(s)


def _join_halves(f, *, name):
    r, c = f.shape

    def body(f_ref, out_ref, send_sem, recv_sem):
        x, y, cc = _place()
        cp = pltpu.make_async_remote_copy(src_ref=f_ref, dst_ref=out_ref.at[cc], send_sem=send_sem, recv_sem=recv_sem,
                                          device_id=(x, y, 1 - cc), device_id_type=MESH)
        cp.start()
        cp.wait()

    return pl.pallas_call(
        body, name=name, in_specs=[ANY], out_specs=ANY, out_shape=jax.ShapeDtypeStruct((2, r, c), f.dtype),
        scratch_shapes=[pltpu.SemaphoreType.DMA, pltpu.SemaphoreType.DMA],
    )(f)


def _pack_rows(parts, width=PACK_COLS):
    return jnp.concatenate([p.reshape(-1, width) for p in parts], axis=0)


def _pack_small(parts, rows):
    flat = jnp.concatenate([p.reshape(-1).astype(F32) for p in parts])
    return jnp.pad(flat, (0, rows * LANES - flat.shape[0])).reshape(rows, LANES)


def _unpack_small(packed, shapes):
    flat = packed.reshape(-1)
    out, o = [], 0
    for shp in shapes:
        n = int(np.prod(shp))
        out.append(flat[o:o + n].reshape(shp))
        o += n
    return out


def _perm_in_cols(w):
    a, b = Q_LORA_RANK + KV_LORA_RANK, Q_LORA_RANK + KV_LORA_RANK + QK_ROPE_DIM
    c = IN_WIDTH - 2 * N_SSD_HEADS
    return jnp.concatenate([w[:, :a], w[:, b:c], w[:, a:b], w[:, c:]], axis=1)


def _unperm_in_cols(w):
    a = Q_LORA_RANK + KV_LORA_RANK
    zx = D_INNER + XBC_WIDTH
    return jnp.concatenate([w[:, :a], w[:, a + zx:a + zx + QK_ROPE_DIM], w[:, a:a + zx], w[:, a + zx + QK_ROPE_DIM:]],
                           axis=1)


def _group_xbc(a):
    n = SSD_STATE
    parts = []
    for g in range(SSD_GROUPS):
        parts += [a[..., g * X_COLS:(g + 1) * X_COLS], a[..., D_INNER + g * n:D_INNER + (g + 1) * n],
                  a[..., D_INNER + GN + g * n:D_INNER + GN + (g + 1) * n]]
    return jnp.concatenate(parts, axis=-1)


def _ungroup_xbc(a):
    n = SSD_STATE
    xs = [a[..., g * GROUP_COLS:g * GROUP_COLS + X_COLS] for g in range(SSD_GROUPS)]
    bs = [a[..., g * GROUP_COLS + X_COLS:g * GROUP_COLS + X_COLS + n] for g in range(SSD_GROUPS)]
    cs = [a[..., g * GROUP_COLS + X_COLS + n:(g + 1) * GROUP_COLS] for g in range(SSD_GROUPS)]
    return jnp.concatenate(xs + bs + cs, axis=-1)


UP_BLOCK = 256


def _interleave_up(w):
    k = w.shape[0]
    return jnp.transpose(w.reshape(k, 2, D_FF // UP_BLOCK, UP_BLOCK), (0, 2, 1, 3)).reshape(k, 2 * D_FF)


def _deinterleave_up(w):
    k = w.shape[0]
    return jnp.transpose(w.reshape(k, D_FF // UP_BLOCK, 2, UP_BLOCK), (0, 2, 1, 3)).reshape(k, 2 * D_FF)


def _pad_q_heads(w):
    k = w.shape[0]
    return jnp.pad(w.reshape(k, N_ATTN_HEADS, QK_DIM), ((0, 0), (0, 0), (0, HEAD_TILE - QK_DIM))).reshape(k, -1)


def _unpad_q_heads(w):
    k = w.shape[0]
    return w.reshape(k, N_ATTN_HEADS, HEAD_TILE)[..., :QK_DIM].reshape(k, N_ATTN_HEADS * QK_DIM)


def _rope_tables(seq_len):
    n_rows = seq_len // GRID_W
    row = jnp.repeat(jnp.arange(n_rows), GRID_W).astype(F32)
    col = jnp.tile(jnp.arange(GRID_W), n_rows).astype(F32)
    axis_dim = QK_ROPE_DIM // 2
    inv_freq = ROPE_THETA ** (-jnp.arange(0, axis_dim, 2, dtype=F32) / axis_dim)
    ang_r = row[:, None] * inv_freq
    ang_c = col[:, None] * inv_freq
    ang = jnp.concatenate([ang_r, ang_r, ang_c, ang_c], axis=-1)
    return jnp.cos(ang), jnp.sin(ang)


def _rot_matrix(width, start):
    r = np.zeros((width, width), np.float32)
    quarter = QK_ROPE_DIM // 4
    for base in (0, QK_ROPE_DIM // 2):
        for i in range(quarter):
            r[start + base + quarter + i, start + base + i] = -1.0
            r[start + base + i, start + base + quarter + i] = 1.0
    return jnp.asarray(r)


ROPE_STEP = QK_ROPE_DIM // 4


def _rope_flat_fn(x, cos, sin_up, sin_down):
    return (x * cos + _roll_lanes(x, -ROPE_STEP) * sin_up + _roll_lanes(x, ROPE_STEP) * sin_down,)


def _krdt_fn(x, cos, sin, rot, bias):
    lane = lax.broadcasted_iota(jnp.int32, (1, KRDT_WIDTH), 1)
    is_dt = (lane >= QK_ROPE_DIM) & (lane < QK_ROPE_DIM + 2 * N_SSD_HEADS)
    roped = x * cos + _dot_exact(x, rot) * sin
    return (jnp.where(is_dt, _softplus(x + bias), roped),)


def _pre_fn(u, w, shift, scale):
    return (_rms(u, w) * (1.0 + scale) + shift,)


def _norm_fn(x, w):
    return (_rms(x, w),)


def _finish_fn(yf, yb, z, w):
    return (_rms((yf + yb) * _silu(z), w),)


def _mid_fn(x, mix, w_post, w_pre, gate, shift, scale):
    x1 = x + gate * _rms(mix, w_post)
    return (x1, _rms(x1, w_pre) * (1.0 + scale) + shift)


def _loss_fn(x1, ffn, tgt, w_post, gate):
    y = x1 + gate * _rms(ffn, w_post)
    err = y - tgt
    return (0.5 * jnp.mean(err * err, axis=-1, keepdims=True),)


def _bias_fn(x, b):
    return (x + b,)


def _silu_fn(x):
    return (_silu(x),)


def kernel(x, c, ctx, c_ctx, w_mod, b_mod, mix_pre_norm, mix_post_norm, w_in, q_norm, w_q_up, kv_norm, w_kv_up, ssd_conv_w, ssd_conv_b, ssd_a_log, ssd_dt_bias, ssd_d, ssd_norm, w_out, ffn_pre_norm, ffn_post_norm, w_up, ffn_conv_w, ffn_conv_b, w_down, loss_target, m_c_ctx, m_w_mod, m_b_mod, m_mix_pre_norm, m_mix_post_norm, m_w_in, m_q_norm, m_w_q_up, m_kv_norm, m_w_kv_up, m_ssd_conv_w, m_ssd_conv_b, m_ssd_a_log, m_ssd_dt_bias, m_ssd_d, m_ssd_norm, m_w_out, m_ffn_pre_norm, m_ffn_post_norm, m_w_up, m_ffn_conv_w, m_ffn_conv_b, m_w_down, v_c_ctx, v_w_mod, v_b_mod, v_mix_pre_norm, v_mix_post_norm, v_w_in, v_q_norm, v_w_q_up, v_kv_norm, v_w_kv_up, v_ssd_conv_w, v_ssd_conv_b, v_ssd_a_log, v_ssd_dt_bias, v_ssd_d, v_ssd_norm, v_w_out, v_ffn_pre_norm, v_ffn_post_norm, v_w_up, v_ffn_conv_w, v_ffn_conv_b, v_w_down):
    args = dict(locals())
    names = ["c_ctx", "w_mod", "b_mod", "mix_pre_norm", "mix_post_norm", "w_in", "q_norm", "w_q_up", "kv_norm",
             "w_kv_up", "ssd_conv_w", "ssd_conv_b", "ssd_a_log", "ssd_dt_bias", "ssd_d", "ssd_norm", "w_out",
             "ffn_pre_norm", "ffn_post_norm", "w_up", "ffn_conv_w", "ffn_conv_b", "w_down"]
    nb, s, d = x.shape
    nctx_rows = ctx.shape[1]
    lt = nctx_rows + s
    tl = 256 if (nctx_rows % 256 == 0 and s % 256 == 0) else 128
    nctx = nctx_rows // tl
    ncc = nctx_rows // SSD_CHUNK
    h, e, g2 = N_ATTN_HEADS, HEADS_PER_GROUP, SSD_GROUPS
    chip = 2 * lax.axis_index("x") + lax.axis_index("y")

    big_local = {n: args[n][0] for n, _, _, _ in BIG}
    packed = _pack_rows([big_local[n].astype(WIRE_DTYPE) for n, _, _, _ in BIG])
    shard_rows = packed.shape[0]
    gathered = _gather_shards(packed, name="gather_weights")
    gathered = lax.dynamic_update_slice(gathered, packed[None], (chip, 0, 0))
    full, o = {}, 0
    for n, rows, cols, axis in BIG:
        lr, lc = big_local[n].shape
        nr = lr * lc // PACK_COLS
        seg = gathered[:, o:o + nr].reshape(N_CHIPS, lr, lc)
        o += nr
        full[n] = seg.reshape(rows, cols) if axis == 0 else jnp.transpose(seg, (1, 0, 2)).reshape(rows, cols)
    conv_rows = 32
    conv_all = _allgather_small(_pack_small([ssd_conv_w[0], ffn_conv_w[0]], conv_rows), name="gather_conv")
    conv_all = conv_all[::2]
    n_sc, n_fc = ssd_conv_w.shape[2], ffn_conv_w.shape[2]
    ssd_conv_full = jnp.concatenate(
        [conv_all[j].reshape(-1)[:SSD_CONV * n_sc].reshape(SSD_CONV, n_sc) for j in range(N_CHIPS)], axis=1)
    ffn_conv_full = jnp.concatenate(
        [conv_all[j].reshape(-1)[SSD_CONV * n_sc:SSD_CONV * n_sc + FFN_CONV * n_fc].reshape(FFN_CONV, n_fc)
         for j in range(N_CHIPS)], axis=1)

    w_in_p = _perm_in_cols(full["w_in"])
    o_cq, o_ckv, o_z = 0, Q_LORA_RANK, Q_LORA_RANK + KV_LORA_RANK
    o_xbc, o_kr = o_z + D_INNER, o_z + D_INNER + XBC_WIDTH
    w_krdt = jnp.pad(w_in_p[:, o_kr:], ((0, 0), (0, KRDT_WIDTH - QK_ROPE_DIM - 2 * N_SSD_HEADS)))
    w_segs = [w_in_p[:, o_cq:o_ckv], w_in_p[:, o_ckv:o_z], w_in_p[:, o_z:o_xbc], _group_xbc(w_in_p[:, o_xbc:o_kr]),
              w_krdt]
    ssd_conv_g, ssd_conv_b_g = _group_xbc(ssd_conv_full), _group_xbc(ssd_conv_b)
    w_q_pad = _pad_q_heads(full["w_q_up"])
    w_up_il = _interleave_up(full["w_up"])
    w_out_a, w_out_s = full["w_out"][:ATTN_WIDTH], full["w_out"][ATTN_WIDTH:]

    mod_rows = 16
    c_all = jnp.concatenate([c, c_ctx[None, :], jnp.zeros((mod_rows - nb - 1, d), F32)], axis=0)[None]
    (s_all,) = _row_fwd(_silu_fn, toks=[(c_all, 0, None, 0)], outs=[(d, F32)], nb=1, nl=mod_rows, tl=mod_rows,
                        name="mod_silu")
    mod_lin = _mm(s_all[0], full["w_mod"], name="mod_mm")
    (mod,) = _row_fwd(_bias_fn, toks=[(mod_lin[None], 0, None, 0)], vecs=[b_mod], outs=[(N_MOD * d, F32)], nb=1,
                      nl=mod_rows, tl=mod_rows, name="mod_bias")
    mods = [mod[0][:, k * d:(k + 1) * d][:, None, :] for k in range(N_MOD)]
    mods_lat = [m[:nb] for m in mods]

    u = jnp.concatenate([ctx, x], axis=1)
    (h1,) = _row_fwd(_pre_fn, toks=[(u, 0, None, 0)], vecs=[mix_pre_norm], bvecs=[mods[0], mods[1]],
                     outs=[(d, MXU_DTYPE)], nb=nb, nl=lt, tl=tl, nctx=nctx, name="pre1")
    h1f = h1.reshape(nb * lt, d)
    p_cq, p_ckv, p_z, p_xbc, p_krdt = [
        _mm(h1f, w, name="in_" + nm).reshape(nb, lt, -1)
        for nm, w in zip(("cq", "ckv", "z", "xbc", "krdt"), w_segs)]

    (cqn,) = _row_fwd(_norm_fn, toks=[(p_cq, nctx, None, 0)], vecs=[q_norm], outs=[(Q_LORA_RANK, MXU_DTYPE)],
                      nb=nb, nl=s, tl=tl, name="q_norm")
    q_flat = _mm(cqn.reshape(nb * s, -1), w_q_pad, name="q_up").reshape(nb, s, h * HEAD_TILE)
    cos, sin = _rope_tables(s)
    ones, zeros = jnp.ones((s, QK_NOPE_DIM), F32), jnp.zeros((s, QK_NOPE_DIM), F32)
    tail = HEAD_TILE - QK_DIM
    up_lanes = ((jnp.arange(QK_ROPE_DIM) // ROPE_STEP) % 2 == 0)[None, :]
    q_tables = [jnp.tile(jnp.concatenate([pad, t, pad[:, :tail]], axis=1), (1, h))[None]
                for pad, t in ((ones, cos), (zeros, jnp.where(up_lanes, -sin, 0.0)), (zeros, jnp.where(up_lanes, 0.0, sin)))]
    tq = 256
    (q_roped,) = _row_fwd(_rope_flat_fn, toks=[(q_flat, 0, None, 0)], poss=q_tables, outs=[(h * HEAD_TILE, MXU_DTYPE)],
                          nb=nb, nl=s, tl=tl, name="rope_q")

    (ckvn,) = _row_fwd(_norm_fn, toks=[(p_ckv, 0, None, 0)], vecs=[kv_norm], outs=[(KV_LORA_RANK, MXU_DTYPE)],
                       nb=nb, nl=lt, tl=tl, name="kv_norm")
    kv_flat = _mm(ckvn.reshape(nb * lt, -1), full["w_kv_up"], out_dtype=MXU_DTYPE, name="kv_up").reshape(nb, lt, -1)

    pad_w = KRDT_WIDTH - QK_ROPE_DIM
    cos_k = jnp.concatenate([jnp.ones((nctx_rows, KRDT_WIDTH), F32),
                             jnp.concatenate([cos, jnp.ones((s, pad_w), F32)], axis=1)], axis=0)[None]
    sin_k = jnp.concatenate([jnp.zeros((nctx_rows, KRDT_WIDTH), F32),
                             jnp.concatenate([sin, jnp.zeros((s, pad_w), F32)], axis=1)], axis=0)[None]
    rot_k = _rot_matrix(KRDT_WIDTH, 0)
    dt_bias_row = jnp.pad(ssd_dt_bias.reshape(1, -1), ((0, 0), (QK_ROPE_DIM, pad_w - 2 * N_SSD_HEADS)))
    (krdt,) = _row_fwd(_krdt_fn, toks=[(p_krdt, 0, None, 0)], poss=[cos_k, sin_k], vecs=[rot_k, dt_bias_row],
                       outs=[(KRDT_WIDTH, F32)], nb=nb, nl=lt, tl=tl, name="krdt")
    kr = jnp.pad(krdt[..., :QK_ROPE_DIM].astype(MXU_DTYPE), ((0, 0), (0, 0), (QK_NOPE_DIM, HEAD_TILE - QK_DIM)))
    attn, lse = _attn_fwd(q_roped, kv_flat, kr, tq=tq, name="attn_fwd")

    seg = nctx_rows

    def conv_ssd_fn(xv, w, b):
        return (_silu(_dwconv(xv, w, seg) + b),)

    cb_ssd = 256
    conv_vecs = [(ssd_conv_g, 0), (ssd_conv_b_g, 0)]
    (xbc,) = _seq_fwd(conv_ssd_fn, toks=[(p_xbc, 0, 1)], vecs=conv_vecs, outs=[F32], nb=nb, nl=lt, nc=XBC_WIDTH,
                      cb=cb_ssd, name="conv_ssd")
    dt = krdt[..., QK_ROPE_DIM:QK_ROPE_DIM + 2 * N_SSD_HEADS].reshape(nb, lt, 2, g2, e)
    dtc = jnp.transpose(dt, (2, 0, 3, 1, 4))
    dtr = jnp.transpose(dt, (2, 0, 3, 4, 1))
    a_neg = -jnp.exp(ssd_a_log[0]).reshape(2, g2, e)
    d_chan = jnp.repeat(ssd_d[0], SSD_HEAD_DIM).reshape(g2, 1, X_COLS)
    scan_args = [(xbc, dtc[dr], dtr[dr], a_neg[dr][:, None, :], a_neg[dr][:, :, None], d_chan) for dr in range(2)]
    ys, ents = [], []
    for dr in range(2):
        y_d, ent_d = _scan_fwd(*scan_args[dr], ncc=ncc, reverse=bool(dr), skip=dr == 0, name=f"scan_fwd{dr}")
        ys.append(y_d)
        ents.append(ent_d)
    (ssd,) = _row_fwd(_finish_fn, toks=[(ys[0], nctx, None, 0), (ys[1], nctx, None, 0), (p_z, nctx, None, 0)],
                      vecs=[ssd_norm], outs=[(D_INNER, MXU_DTYPE)], nb=nb, nl=s, tl=tl, name="ssd_finish")

    attn_f, ssd_f = attn.reshape(nb * s, ATTN_WIDTH), ssd.reshape(nb * s, D_INNER)
    mix = _mm_sum([(attn_f, w_out_a), (ssd_f, w_out_s)], name="out_proj").reshape(nb, s, d)

    mid_bvecs = [mods_lat[2], mods_lat[3], mods_lat[4]]
    x1, h2 = _row_fwd(_mid_fn, toks=[(x, 0, None, 0), (mix, 0, None, 0)], vecs=[mix_post_norm, ffn_pre_norm],
                      bvecs=mid_bvecs, outs=[(d, F32), (d, MXU_DTYPE)], nb=nb, nl=s, tl=tl, name="mid")
    up = _mm(h2.reshape(nb * s, d), w_up_il, name="ffn_up").reshape(nb, s, 2 * D_FF)

    def glu_fn(gv, w, b):
        return (_gelu(_dwconv(gv[:, :UP_BLOCK], w, 0) + b) * gv[:, UP_BLOCK:],)

    cb_ffn = UP_BLOCK
    glu_toks = [(up, 0, 2)]
    glu_vecs = [(ffn_conv_full, 0), (ffn_conv_b, 0)]
    (act,) = _seq_fwd(glu_fn, toks=glu_toks, vecs=glu_vecs, outs=[MXU_DTYPE], nb=nb, nl=s, nc=D_FF, cb=cb_ffn,
                      name="conv_glu")
    ffn = _mm(act.reshape(nb * s, D_FF), full["w_down"], name="ffn_down").reshape(nb, s, d)

    loss_toks = [(x1, 0, None, 0), (ffn, 0, None, 0), (loss_target, 0, None, 0)]
    ones_rows = jnp.ones((nb, s, 1), F32)
    (dx1_a, dffn, _), (g_ffn_post,), ((g_gate5, _),), (loss_rows,) = _row_bwd(
        _loss_fn, toks=loss_toks, vecs=[ffn_post_norm], bvecs=[mods_lat[5]], cots=[(ones_rows, 0)],
        tok_grads=[F32, MXU_DTYPE, None], emit=[(0, 1, F32)], nb=nb, nl=s, tl=tl, name="loss_bwd")
    loss_part = jnp.sum(loss_rows)

    dffn_f = dffn.reshape(nb * s, d)
    g_w_down = _mm(act.reshape(nb * s, D_FF), dffn_f, ta=True, name="wg_down")
    dact = _mm(dffn_f, full["w_down"], tb=True, out_dtype=MXU_DTYPE, name="dg_down").reshape(nb, s, D_FF)
    (dup,), (g_ffn_conv_w, g_ffn_conv_b) = _seq_bwd(
        glu_fn, toks=glu_toks, vecs=glu_vecs, cots=[dact], tok_grads=[MXU_DTYPE], nb=nb, nl=s, nc=D_FF,
        cb=cb_ffn, name="conv_glu_bwd")
    dup = dup.reshape(nb * s, 2 * D_FF)
    g_w_up = _deinterleave_up(_mm(h2.reshape(nb * s, d), dup, ta=True, name="wg_up"))
    dh2 = _mm(dup, w_up_il, tb=True, name="dg_up").reshape(nb, s, d)

    (dx_res, dmix), (g_mix_post, g_ffn_pre), ((g_gate2, _), (g_shift3, _), (g_scale4, _)), _ = _row_bwd(
        _mid_fn, toks=[(x, 0, None, 0), (mix, 0, None, 0)], vecs=[mix_post_norm, ffn_pre_norm], bvecs=mid_bvecs,
        cots=[(dx1_a, 0), (dh2, 0)], tok_grads=[F32, MXU_DTYPE], nb=nb, nl=s, tl=tl, name="mid_bwd")

    dmix_f = dmix.reshape(nb * s, d)
    g_w_out = jnp.concatenate([_mm(attn_f, dmix_f, ta=True, name="wg_out_attn"),
                               _mm(ssd_f, dmix_f, ta=True, name="wg_out_ssd")], axis=0)
    dattn = _mm(dmix_f, w_out_a, tb=True, name="dg_out_attn").reshape(nb, s, ATTN_WIDTH)
    dssd = _mm(dmix_f, w_out_s, tb=True, name="dg_out_ssd").reshape(nb, s, D_INNER)

    (dy, _, dz), (g_ssd_norm,), _, _ = _row_bwd(
        _finish_fn, toks=[(ys[0], 0, None, 0), (ys[1], 0, None, 0), (p_z, 0, None, 0)], vecs=[ssd_norm],
        cots=[(dssd, -nctx)], tok_grads=[F32, None, MXU_DTYPE], nb=nb, nl=lt, tl=tl, name="ssd_finish_bwd")
    g_xbc_act = None
    g_dt_dirs, g_a, g_d_chan = [], [], None
    for dr in range(2):
        g_xbc_act, gdtc, gdtr, gac, gar, gd = _scan_bwd(
            *scan_args[dr], ents[dr], dy, g_xbc_act, ncc=ncc, reverse=bool(dr), skip=dr == 0, name=f"scan_bwd{dr}")
        g_dt_dirs.append(jnp.transpose(gdtc, (0, 2, 1, 3)) + jnp.transpose(gdtr, (0, 3, 1, 2)))
        g_a.append(jnp.sum(gac[:, :, 0, :] + gar[:, :, :, 0], axis=0))
        if dr == 0:
            g_d_chan = jnp.sum(gd, axis=0)
    g_a_log = (jnp.stack(g_a) * a_neg).reshape(1, 2, N_SSD_HEADS)
    g_dt = jnp.stack(g_dt_dirs, axis=2).reshape(nb, lt, 2 * N_SSD_HEADS)
    (dp_xbc,), (g_ssd_conv_w, g_ssd_conv_b) = _seq_bwd(
        conv_ssd_fn, toks=[(p_xbc, 0, 1)], vecs=conv_vecs, cots=[g_xbc_act], tok_grads=[MXU_DTYPE], nb=nb, nl=lt,
        nc=XBC_WIDTH, cb=cb_ssd, name="conv_ssd_bwd")
    g_ssd_conv_w, g_ssd_conv_b = _ungroup_xbc(g_ssd_conv_w), _ungroup_xbc(g_ssd_conv_b)

    dq_roped, dkv, dkr = _attn_bwd(q_roped, kv_flat, kr, attn, lse, dattn, tq=tq, name="attn_bwd")
    (dq_flat,), _, _, _ = _row_bwd(_rope_flat_fn, toks=[(q_flat, 0, None, 0)], poss=q_tables, cots=[(dq_roped, 0)],
                                   tok_grads=[MXU_DTYPE], nb=nb, nl=s, tl=tl, name="rope_q_bwd")
    dq_flat = dq_flat.reshape(nb * s, h * HEAD_TILE)
    g_w_q_up = _unpad_q_heads(_mm(cqn.reshape(nb * s, -1), dq_flat, ta=True, name="wg_q_up"))
    dcqn = _mm(dq_flat, w_q_pad, tb=True, name="dg_q_up").reshape(nb, s, Q_LORA_RANK)
    (dp_cq,), (g_q_norm,), _, _ = _row_bwd(_norm_fn, toks=[(p_cq, 0, None, 0)], vecs=[q_norm], cots=[(dcqn, -nctx)],
                                           tok_grads=[MXU_DTYPE], nb=nb, nl=lt, tl=tl, name="q_norm_bwd")

    dkv_flat = dkv.reshape(nb * lt, -1)
    g_w_kv_up = _mm(ckvn.reshape(nb * lt, -1), dkv_flat, ta=True, name="wg_kv_up")
    dckvn = _mm(dkv_flat, full["w_kv_up"], tb=True, name="dg_kv_up").reshape(nb, lt, KV_LORA_RANK)
    (dp_ckv,), (g_kv_norm,), _, _ = _row_bwd(_norm_fn, toks=[(p_ckv, 0, None, 0)], vecs=[kv_norm], cots=[(dckvn, 0)],
                                             tok_grads=[MXU_DTYPE], nb=nb, nl=lt, tl=tl, name="kv_norm_bwd")

    g_krdt = jnp.concatenate([dkr[..., QK_NOPE_DIM:QK_DIM], g_dt, jnp.zeros((nb, lt, pad_w - 2 * N_SSD_HEADS), F32)],
                             axis=-1)
    (dp_krdt,), (_, g_dt_bias_row), _, _ = _row_bwd(
        _krdt_fn, toks=[(p_krdt, 0, None, 0)], poss=[cos_k, sin_k], vecs=[rot_k, dt_bias_row], cots=[(g_krdt, 0)],
        tok_grads=[MXU_DTYPE], nb=nb, nl=lt, tl=tl, name="krdt_bwd")

    dp_segs = [t.reshape(nb * lt, -1) for t in (dp_cq, dp_ckv, dz, dp_xbc, dp_krdt)]
    g_segs = [_mm(h1f, t, ta=True, name="wg_in_" + nm) for nm, t in zip(("cq", "ckv", "z", "xbc", "krdt"), dp_segs)]
    g_segs[3] = _ungroup_xbc(g_segs[3])
    g_w_in_p = jnp.concatenate(g_segs, axis=1)
    dh1 = _mm_sum(list(zip(dp_segs, w_segs)), tb=True, name="dg_in").reshape(nb, lt, d)

    def pre_res_fn(uv, w, shift, scale):
        return _pre_fn(uv, w, shift, scale) + (uv,)

    (grad_x,), (g_mix_pre,), ((g_shift0, g_shift0c), (g_scale1, g_scale1c)), _ = _row_bwd(
        pre_res_fn, toks=[(u, 0, None, 0)], vecs=[mix_pre_norm], bvecs=[mods[0], mods[1]],
        cots=[(dh1, 0), (dx_res, -nctx)], tok_grads=[F32], nb=nb, nl=lt, tl=tl, nctx=nctx, drop_blocks=nctx,
        name="pre1_bwd")

    zero_row = jnp.zeros((1, 1, d), F32)
    lat = [g_shift0, g_scale1, g_gate2, g_shift3, g_scale4, g_gate5]
    ctxg = [g_shift0c, g_scale1c, zero_row, zero_row, zero_row, zero_row]
    dmod = jnp.concatenate([jnp.concatenate([a, b], axis=0)[:, 0, :] for a, b in zip(lat, ctxg)], axis=-1)
    dmod = jnp.pad(dmod, ((0, mod_rows - nb - 1), (0, 0)))
    _, (g_b_mod,), _, _ = _row_bwd(_bias_fn, toks=[(mod_lin[None], 0, None, 0)], vecs=[b_mod], cots=[(dmod[None], 0)],
                                   tok_grads=[None], nb=1, nl=mod_rows, tl=mod_rows, name="mod_bias_bwd")
    g_w_mod = _mm(s_all[0], dmod, ta=True, name="wg_mod")
    ds_all = _mm(dmod, full["w_mod"], tb=True, name="dg_mod")
    (dc_all,), _, _, _ = _row_bwd(_silu_fn, toks=[(c_all, 0, None, 0)], cots=[(ds_all[None], 0)], tok_grads=[F32],
                                  nb=1, nl=mod_rows, tl=mod_rows, name="mod_silu_bwd")
    g_c_ctx = dc_all[0, nb]

    g_w_in = _unperm_in_cols(g_w_in_p[:, :IN_WIDTH])
    big_grads = {"w_mod": g_w_mod, "w_in": g_w_in, "w_q_up": g_w_q_up, "w_kv_up": g_w_kv_up, "w_out": g_w_out,
                 "w_up": g_w_up, "w_down": g_w_down}
    parts = []
    for n, rows, cols, axis in BIG:
        lr, lc = big_local[n].shape
        gfull = big_grads[n]
        shards = gfull.reshape(N_CHIPS, lr, lc) if axis == 0 else jnp.transpose(gfull.reshape(lr, N_CHIPS, lc), (1, 0, 2))
        parts.append(shards.reshape(N_CHIPS, lr * lc // PACK_COLS, PACK_COLS))
    gpack = jnp.concatenate(parts, axis=1).astype(WIRE_DTYPE)
    half = shard_rows // 2
    core = lax.axis_index("c")
    gpack = gpack.reshape(N_CHIPS, 2, half, PACK_COLS)
    got = _swap_halves(gpack, name="grad_swap")
    own = lax.dynamic_index_in_dim(gpack, core, axis=1, keepdims=False)
    flat = (N_CHIPS * half, PACK_COLS)
    chip_sum = _sum_list([own.reshape(flat), got.reshape(flat)], out_dtype=WIRE_DTYPE,
                         name="grad_add_pair").reshape(N_CHIPS, half, PACK_COLS)
    from_chips = _scatter_to_chips(chip_sum, name="grad_scatter")
    mine_sum = lax.dynamic_index_in_dim(chip_sum, chip, axis=0, keepdims=True)
    from_chips = lax.dynamic_update_slice(from_chips, mine_sum, (chip, 0, 0))
    my_half = _sum_slots(from_chips, out_dtype=F32, name="grad_add_chips")
    joined = _join_halves(my_half, name="grad_join")
    g_shard = lax.dynamic_update_slice(joined, my_half[None], (core, 0, 0)).reshape(shard_rows, PACK_COLS)

    g_d = jnp.sum(g_d_chan.reshape(N_SSD_HEADS, SSD_HEAD_DIM), axis=1)[None]
    g_dt_bias = g_dt_bias_row[:, QK_ROPE_DIM:QK_ROPE_DIM + 2 * N_SSD_HEADS].reshape(1, 2, N_SSD_HEADS)
    small_names = ["c_ctx", "b_mod", "mix_pre_norm", "mix_post_norm", "q_norm", "kv_norm", "ssd_conv_w", "ssd_conv_b",
                   "ssd_a_log", "ssd_dt_bias", "ssd_d", "ssd_norm", "ffn_pre_norm", "ffn_post_norm", "ffn_conv_w",
                   "ffn_conv_b"]
    small_grads = [g_c_ctx, g_b_mod, g_mix_pre, g_mix_post, g_q_norm, g_kv_norm, g_ssd_conv_w, g_ssd_conv_b,
                   g_a_log, g_dt_bias, g_d, g_ssd_norm, g_ffn_pre, g_ffn_post, g_ffn_conv_w, g_ffn_conv_b]
    small_shapes = [tuple(np.shape(a)) for a in small_grads] + [()]
    n_small = sum(int(np.prod(shp)) for shp in small_shapes)
    small_rows = -(-n_small // (8 * LANES)) * 8
    small_all = _allgather_small(_pack_small(small_grads + [loss_part], small_rows), name="gather_small")
    small_sum = _sum_slots(small_all, out_dtype=F32, name="small_add")
    small_red = _unpack_small(small_sum, small_shapes)
    loss = small_red[-1]
    grads = dict(zip(small_names, small_red[:-1]))
    grads["ssd_conv_w"] = lax.dynamic_slice_in_dim(grads["ssd_conv_w"], chip * n_sc, n_sc, axis=1)[None]
    grads["ffn_conv_w"] = lax.dynamic_slice_in_dim(grads["ffn_conv_w"], chip * n_fc, n_fc, axis=1)[None]
    for n in small_names:
        grads[n] = grads[n].reshape(args[n].shape)

    delta, new_m, new_v = {}, {}, {}
    o = 0
    for n, _, _, _ in BIG:
        lr, lc = big_local[n].shape
        nr = lr * lc // PACK_COLS
        grads[n] = g_shard[o:o + nr].reshape(1, lr, lc)
        o += nr
        dl, nm, nv = _adamw(big_local[n], grads[n][0], args["m_" + n][0], args["v_" + n][0], name="adamw_" + n)
        delta[n], new_m[n], new_v[n] = dl[None], nm[None], nv[None]
    sm_shapes = [args[n].shape for n in small_names]
    n_sm = sum(int(np.prod(shp)) for shp in sm_shapes)
    sm_rows = -(-n_sm // (8 * LANES)) * 8
    packs = [_pack_small([src[n] for n in small_names], sm_rows)
             for src in (args, grads, {n: args["m_" + n] for n in small_names}, {n: args["v_" + n] for n in small_names})]
    for out_dict, packed_out in zip((delta, new_m, new_v), _adamw(*packs, name="adamw_small")):
        out_dict.update(zip(small_names, _unpack_small(packed_out, sm_shapes)))

    return (loss, grad_x, *[grads[n] for n in names], *[delta[n] for n in names], *[new_m[n] for n in names],
            *[new_v[n] for n in names])
```

```python
import functools
import math

import numpy as np
import jax
import jax.numpy as jnp
from jax import lax
from jax.experimental import pallas as pl
from jax.experimental.pallas import tpu as pltpu

F32 = jnp.float32
MXU_DTYPE = jnp.bfloat16
WIRE_DTYPE = jnp.bfloat16
VMEM_LIMIT_BYTES = 56 * 1024 * 1024
HIGHEST = lax.Precision.HIGHEST

D_MODEL = 1024
N_MOD = 6
EPS = 1e-6
GRID_W = 64
N_ATTN_HEADS = 16
QK_NOPE_DIM = 64
QK_ROPE_DIM = 32
QK_DIM = QK_NOPE_DIM + QK_ROPE_DIM
V_HEAD_DIM = 64
Q_LORA_RANK = 384
KV_LORA_RANK = 256
ROPE_THETA = 10000.0
ATTN_SCALE = QK_DIM ** -0.5
ATTN_WIDTH = N_ATTN_HEADS * V_HEAD_DIM
N_SSD_HEADS = 16
SSD_HEAD_DIM = 64
SSD_GROUPS = 2
HEADS_PER_GROUP = N_SSD_HEADS // SSD_GROUPS
SSD_STATE = 128
SSD_CONV = 5
SSD_CHUNK = 128
D_INNER = N_SSD_HEADS * SSD_HEAD_DIM
GN = SSD_GROUPS * SSD_STATE
XBC_WIDTH = D_INNER + 2 * GN
D_FF = 2816
FFN_CONV = 3
KRDT_WIDTH = 128
IN_WIDTH = Q_LORA_RANK + KV_LORA_RANK + QK_ROPE_DIM + D_INNER + XBC_WIDTH + 2 * N_SSD_HEADS

ADAM_LR = 0.001
ADAM_B1 = 0.9
ADAM_B2 = 0.999
ADAM_EPS = 1e-08
ADAM_WD = 0.01
ADAM_STEP = 10

N_CHIPS = 4
N_DEV = 8
MESH = pl.DeviceIdType.MESH
LANES = 128

BIG = (("w_in", D_MODEL, IN_WIDTH, 1),
       ("w_q_up", Q_LORA_RANK, N_ATTN_HEADS * QK_DIM, 1),
       ("w_kv_up", KV_LORA_RANK, N_ATTN_HEADS * (QK_NOPE_DIM + V_HEAD_DIM), 1),
       ("w_out", ATTN_WIDTH + D_INNER, D_MODEL, 0), ("w_up", D_MODEL, 2 * D_FF, 1),
       ("w_down", D_FF, D_MODEL, 0))
PACK_COLS = 1024


def _cparams(sem):
    return pltpu.CompilerParams(dimension_semantics=sem, vmem_limit_bytes=VMEM_LIMIT_BYTES)


def _pick(n, cands):
    for c in cands:
        if n % c == 0:
            return c
    return n


def _sigmoid(x):
    return 0.5 * (jnp.tanh(0.5 * x) + 1.0)


def _silu(x):
    return x * _sigmoid(x)


@jax.custom_vjp
def _softplus(x):
    u = jnp.exp(-jnp.abs(x))
    w = 1.0 + u
    log1p = jnp.where(w == 1.0, u, jnp.log(w) * (u / jnp.where(w == 1.0, 1.0, w - 1.0)))
    return jnp.maximum(x, 0.0) + log1p


def _softplus_fwd(x):
    return _softplus(x), x


def _softplus_bwd(x, g):
    return (g * _sigmoid(x),)


_softplus.defvjp(_softplus_fwd, _softplus_bwd)


@jax.custom_vjp
def _gelu(x):
    return 0.5 * x * (1.0 + lax.erf(x * (2.0 ** -0.5)))


def _gelu_fwd(x):
    return _gelu(x), x


def _gelu_bwd(x, g):
    cdf = 0.5 * (1.0 + lax.erf(x * (2.0 ** -0.5)))
    pdf = jnp.exp(-0.5 * x * x) * (1.0 / math.sqrt(2.0 * math.pi))
    return (g * (cdf + x * pdf),)


_gelu.defvjp(_gelu_fwd, _gelu_bwd)


def _rms(x, w):
    return x * lax.rsqrt(jnp.mean(x * x, axis=-1, keepdims=True) + EPS) * w


def _shift_rows_raw(x, off, seg):
    n = x.shape[0]
    if off == 0:
        return x
    r = pltpu.roll(x, (-off) % n, 0)
    idx = lax.broadcasted_iota(jnp.int32, x.shape, 0)
    src = idx + off
    ok = (src >= 0) & (src < n)
    if seg:
        ok = ok & ((idx < seg) == (src < seg))
    return jnp.where(ok, r, 0.0)


@functools.partial(jax.custom_vjp, nondiff_argnums=(1, 2))
def _shift_rows(x, off, seg):
    return _shift_rows_raw(x, off, seg)


def _shift_rows_fwd(x, off, seg):
    return _shift_rows_raw(x, off, seg), None


def _shift_rows_bwd(off, seg, _, g):
    return (_shift_rows_raw(g, -off, seg),)


_shift_rows.defvjp(_shift_rows_fwd, _shift_rows_bwd)


@functools.partial(jax.custom_vjp, nondiff_argnums=(1,))
def _roll_lanes(x, shift):
    return pltpu.roll(x, shift % x.shape[1], 1)


def _roll_lanes_fwd(x, shift):
    return _roll_lanes(x, shift), None


def _roll_lanes_bwd(shift, _, g):
    return (pltpu.roll(g, (-shift) % g.shape[1], 1),)


_roll_lanes.defvjp(_roll_lanes_fwd, _roll_lanes_bwd)


def _row_of(w, k):
    sel = lax.broadcasted_iota(jnp.int32, (w.shape[0], 1), 0) == k
    return jnp.sum(jnp.where(sel, w, 0.0), axis=0, keepdims=True)


def _col_of(w, k):
    sel = lax.broadcasted_iota(jnp.int32, (1, w.shape[1]), 1) == k
    return jnp.sum(jnp.where(sel, w, 0.0), axis=1, keepdims=True)


def _dwconv(x, w, seg):
    k = w.shape[0]
    acc = None
    for t in range(k):
        term = _shift_rows(x, t - k // 2, seg) * _row_of(w, t)
        acc = term if acc is None else acc + term
    return acc


def _dot(a, b, dims):
    return lax.dot_general(a.astype(MXU_DTYPE), b.astype(MXU_DTYPE), (dims, ((), ())),
                           preferred_element_type=F32)


def _dot_exact(a, b):
    return lax.dot_general(a, b, (((1,), (0,)), ((), ())), precision=HIGHEST,
                           preferred_element_type=F32)


def _mm(a, b, *, ta=False, tb=False, out_dtype=F32, name):
    if ta:
        kdim, m = a.shape
    else:
        m, kdim = a.shape
    if tb:
        n, k2 = b.shape
    else:
        k2, n = b.shape
    assert kdim == k2, (a.shape, b.shape, ta, tb)
    tm = _pick(m, (1024, 1408, 512, 384, 256, 128))
    tn = _pick(n, (512, 1408, 384, 256, 128))
    tk = kdim if kdim <= 2048 else _pick(kdim, (2048, 1664, 1536, 1408, 1024, 512, 256, 128))
    nk = kdim // tk
    a_spec = pl.BlockSpec((tk, tm), lambda i, j, k: (k, i)) if ta else pl.BlockSpec((tm, tk), lambda i, j, k: (i, k))
    b_spec = pl.BlockSpec((tn, tk), lambda i, j, k: (j, k)) if tb else pl.BlockSpec((tk, tn), lambda i, j, k: (k, j))
    dims = ((0,) if ta else (1,), (1,) if tb else (0,))

    def body(a_ref, b_ref, o_ref, *scratch):
        if nk == 1:
            o_ref[...] = _dot(a_ref[...], b_ref[...], dims).astype(o_ref.dtype)
            return
        acc_ref, = scratch
        k = pl.program_id(2)

        @pl.when(k == 0)
        def _():
            acc_ref[...] = jnp.zeros_like(acc_ref)

        acc_ref[...] += _dot(a_ref[...], b_ref[...], dims)

        @pl.when(k == nk - 1)
        def _():
            o_ref[...] = acc_ref[...].astype(o_ref.dtype)

    return pl.pallas_call(
        body, name=name, grid=(m // tm, n // tn, nk),
        in_specs=[a_spec, b_spec], out_specs=pl.BlockSpec((tm, tn), lambda i, j, k: (i, j)),
        out_shape=jax.ShapeDtypeStruct((m, n), out_dtype),
        scratch_shapes=[pltpu.VMEM((tm, tn), F32)] if nk > 1 else [],
        compiler_params=_cparams(("parallel", "parallel", "arbitrary")),
    )(a, b)


def _mm_sum(pairs, *, tb=False, out_dtype=F32, name):
    m = pairs[0][0].shape[0]
    n = pairs[0][1].shape[0] if tb else pairs[0][1].shape[1]
    tm = _pick(m, (1024, 1408, 512, 384, 256, 128))
    tn = _pick(n, (512, 1408, 384, 256, 128))
    specs, args = [], []
    for a, b in pairs:
        kdim = a.shape[1]
        specs.append(pl.BlockSpec((tm, kdim), lambda i, j: (i, 0)))
        specs.append(pl.BlockSpec((tn, kdim), lambda i, j: (j, 0)) if tb else pl.BlockSpec((kdim, tn), lambda i, j: (0, j)))
        args += [a, b]
    dims = ((1,), (1,) if tb else (0,))

    def body(*refs):
        acc = None
        for t in range(len(pairs)):
            term = _dot(refs[2 * t][...], refs[2 * t + 1][...], dims)
            acc = term if acc is None else acc + term
        refs[-1][...] = acc.astype(refs[-1].dtype)

    return pl.pallas_call(
        body, name=name, grid=(m // tm, n // tn), in_specs=specs,
        out_specs=pl.BlockSpec((tm, tn), lambda i, j: (i, j)), out_shape=jax.ShapeDtypeStruct((m, n), out_dtype),
        compiler_params=_cparams(("parallel", "parallel")),
    )(*args)


def _row_specs(toks, poss, vecs, bvecs, tl, nctx, nb):
    specs, args = [], []
    for arr, off, cw, ci in toks:
        cw = arr.shape[2] if cw is None else cw
        specs.append(pl.BlockSpec((1, tl, cw), lambda b, l, off=off, ci=ci: (b, l + off, ci)))
        args.append(arr)
    for arr in poss:
        specs.append(pl.BlockSpec((1, tl, arr.shape[2]), lambda b, l: (0, l, 0)))
        args.append(arr)
    for arr in vecs:
        specs.append(pl.BlockSpec(arr.shape, lambda b, l: (0, 0)))
        args.append(arr)
    for arr in bvecs:
        if nctx:
            specs.append(pl.BlockSpec((1, 1, arr.shape[2]), lambda b, l: (jnp.where(l < nctx, nb, b), 0, 0)))
        else:
            specs.append(pl.BlockSpec((1, 1, arr.shape[2]), lambda b, l: (b, 0, 0)))
        args.append(arr)
    return specs, args


def _row_fwd(fn, *, toks, poss=(), vecs=(), bvecs=(), outs, nb, nl, tl, nctx=0, name):
    nt, npos, nv, nbv = len(toks), len(poss), len(vecs), len(bvecs)
    specs, args = _row_specs(toks, poss, vecs, bvecs, tl, nctx, nb)

    def body(*refs):
        ins, os = refs[:len(specs)], refs[len(specs):]
        tv = [r[0].astype(F32) for r in ins[:nt]]
        pv = [r[0] for r in ins[nt:nt + npos]]
        vv = [r[...] for r in ins[nt + npos:nt + npos + nv]]
        bv = [r[0] for r in ins[nt + npos + nv:]]
        res = fn(*tv, *pv, *vv, *bv)
        for o, r in zip(os, res):
            o[0] = r.astype(o.dtype)

    return pl.pallas_call(
        body, name=name, grid=(nb, nl // tl), in_specs=specs,
        out_specs=[pl.BlockSpec((1, tl, c), lambda b, l: (b, l, 0)) for c, _ in outs],
        out_shape=[jax.ShapeDtypeStruct((nb, nl, c), dt) for c, dt in outs],
        compiler_params=_cparams(("parallel", "parallel")),
    )(*args)


def _row_bwd(fn, *, toks, poss=(), vecs=(), bvecs=(), cots, tok_grads, emit=(), nb, nl, tl, nctx=0, name,
             drop_blocks=0):
    nt, npos, nv, nbv = len(toks), len(poss), len(vecs), len(bvecs)
    specs, args = _row_specs(toks, poss, vecs, bvecs, tl, nctx, nb)
    n_in = len(specs)
    cot_slots = []
    for arr, off in cots:
        if arr is None:
            cot_slots.append(None)
            continue
        cot_slots.append((len(specs), off))
        specs.append(pl.BlockSpec((1, tl, arr.shape[2]), lambda b, l, off=off: (b, jnp.maximum(l + off, 0), 0)))
        args.append(arr)
    n_all_in = len(specs)

    out_specs, out_shapes = [], []
    tok_out = []
    for (arr, off, cw, ci), dt in zip(toks, tok_grads):
        if dt is None:
            tok_out.append(None)
            continue
        cw = arr.shape[2] if cw is None else cw
        tok_out.append(len(out_specs))
        out_specs.append(pl.BlockSpec((1, tl, cw), lambda b, l: (b, jnp.maximum(l - drop_blocks, 0), 0)))
        out_shapes.append(jax.ShapeDtypeStruct((nb, nl - drop_blocks * tl, cw), dt))
    vec_out = []
    for arr in vecs:
        vec_out.append(len(out_specs))
        out_specs.append(pl.BlockSpec(arr.shape, lambda b, l: (0, 0)))
        out_shapes.append(jax.ShapeDtypeStruct(arr.shape, F32))
    bv_out = []
    for arr in bvecs:
        c = arr.shape[2]
        lat = len(out_specs)
        out_specs.append(pl.BlockSpec((1, 1, c), lambda b, l: (b, 0, 0)))
        out_shapes.append(jax.ShapeDtypeStruct((nb, 1, c), F32))
        ctx = None
        if nctx:
            ctx = len(out_specs)
            out_specs.append(pl.BlockSpec((1, 1, c), lambda b, l: (0, 0, 0)))
            out_shapes.append(jax.ShapeDtypeStruct((1, 1, c), F32))
        bv_out.append((lat, ctx))
    emit_out = []
    emit_cols = {}
    for idx, c, dt in emit:
        emit_out.append((idx, len(out_specs)))
        out_specs.append(pl.BlockSpec((1, tl, c), lambda b, l: (b, l, 0)))
        out_shapes.append(jax.ShapeDtypeStruct((nb, nl, c), dt))

    def body(*refs):
        ins, os = refs[:n_all_in], refs[n_all_in:]
        b, l = pl.program_id(0), pl.program_id(1)
        tv = [r[0].astype(F32) for r in ins[:nt]]
        pv = [r[0] for r in ins[nt:nt + npos]]
        vv = [r[...] for r in ins[nt + npos:nt + npos + nv]]
        bv = [r[0] for r in ins[nt + npos + nv:n_in]]

        def f(*d):
            return tuple(fn(*d[:nt], *pv, *d[nt:]))

        res, vjp = jax.vjp(f, *tv, *vv, *bv)
        cts = []
        for r, slot in zip(res, cot_slots):
            if slot is None:
                cts.append(jnp.zeros_like(r))
            else:
                i, off = slot
                ct = ins[i][0].astype(F32)
                if off < 0:
                    ct = jnp.where(l + off >= 0, ct, 0.0)
                cts.append(ct)
        grads = vjp(tuple(cts))

        for g, slot in zip(grads[:nt], tok_out):
            if slot is not None:
                os[slot][0] = g.astype(os[slot].dtype)

        @pl.when((b == 0) & (l == 0))
        def _():
            for slot in vec_out:
                os[slot][...] = jnp.zeros_like(os[slot])
            for _, ctx in bv_out:
                if ctx is not None:
                    os[ctx][...] = jnp.zeros_like(os[ctx])

        @pl.when(l == 0)
        def _():
            for lat, _ in bv_out:
                os[lat][...] = jnp.zeros_like(os[lat])

        for g, slot in zip(grads[nt:nt + nv], vec_out):
            os[slot][...] += g
        for g, (lat, ctx) in zip(grads[nt + nv:], bv_out):
            if ctx is None:
                os[lat][0] += g
            else:
                is_ctx = l < nctx
                os[lat][0] += jnp.where(is_ctx, 0.0, g)
                os[ctx][0] += jnp.where(is_ctx, g, 0.0)
        for idx, slot in emit_out:
            os[slot][0] = res[idx].astype(os[slot].dtype)

    out = pl.pallas_call(
        body, name=name, grid=(nb, nl // tl), in_specs=specs, out_specs=out_specs, out_shape=out_shapes,
        compiler_params=_cparams(("arbitrary", "arbitrary")),
    )(*args)
    tg = [None if s is None else out[s] for s in tok_out]
    vg = [out[s] for s in vec_out]
    bg = [(out[lat], None if ctx is None else out[ctx]) for lat, ctx in bv_out]
    em = [out[s] for _, s in emit_out]
    return tg, vg, bg, em


def _seq_specs(toks, vecs, nl, cb):
    specs, args = [], []
    for arr, off, mult in toks:
        specs.append(pl.BlockSpec((1, nl, cb * mult), lambda j, b, off=off: (b, 0, j + off)))
        args.append(arr)
    for arr, off in vecs:
        specs.append(pl.BlockSpec((arr.shape[0], cb), lambda j, b, off=off: (0, j + off)))
        args.append(arr)
    return specs, args


def _seq_fwd(fn, *, toks, vecs, outs, nb, nl, nc, cb, name):
    nt = len(toks)
    specs, args = _seq_specs(toks, vecs, nl, cb)

    def body(*refs):
        ins, os = refs[:len(specs)], refs[len(specs):]
        tv = [r[0].astype(F32) for r in ins[:nt]]
        vv = [r[...] for r in ins[nt:]]
        for o, r in zip(os, fn(*tv, *vv)):
            o[0] = r.astype(o.dtype)

    return pl.pallas_call(
        body, name=name, grid=(nc // cb, nb), in_specs=specs,
        out_specs=[pl.BlockSpec((1, nl, cb), lambda j, b: (b, 0, j)) for _ in outs],
        out_shape=[jax.ShapeDtypeStruct((nb, nl, nc), dt) for dt in outs],
        compiler_params=_cparams(("parallel", "parallel")),
    )(*args)


def _seq_bwd(fn, *, toks, vecs, cots, tok_grads, nb, nl, nc, cb, name):
    nt, nv = len(toks), len(vecs)
    specs, args = _seq_specs(toks, vecs, nl, cb)
    n_in = len(specs)
    cot_counts = [len(group) for group in cots]
    for group in cots:
        for arr in group:
            specs.append(pl.BlockSpec((1, nl, cb), lambda j, b: (b, 0, j)))
            args.append(arr)
    out_specs, out_shapes = [], []
    for (_, _, mult), dt in zip(toks, tok_grads):
        out_specs.append(pl.BlockSpec((1, nl, cb * mult), lambda j, b: (b, 0, j)))
        out_shapes.append(jax.ShapeDtypeStruct((nb, nl, nc * mult), dt))
    for arr, _ in vecs:
        out_specs.append(pl.BlockSpec((arr.shape[0], cb), lambda j, b: (0, j)))
        out_shapes.append(jax.ShapeDtypeStruct((arr.shape[0], nc), F32))

    def body(*refs):
        ins, os = refs[:len(specs)], refs[len(specs):]
        b = pl.program_id(1)
        tv = [r[0].astype(F32) for r in ins[:nt]]
        vv = [r[...] for r in ins[nt:n_in]]
        _, vjp = jax.vjp(lambda *d: tuple(fn(*d)), *tv, *vv)
        cts, o = [], n_in
        for cnt in cot_counts:
            ct = ins[o][0].astype(F32)
            for r in ins[o + 1:o + cnt]:
                ct = ct + r[0].astype(F32)
            cts.append(ct)
            o += cnt
        grads = vjp(tuple(cts))
        for g, o in zip(grads[:nt], os[:nt]):
            o[0] = g.astype(o.dtype)

        @pl.when(b == 0)
        def _():
            for o in os[nt:]:
                o[...] = jnp.zeros_like(o)

        for g, o in zip(grads[nt:], os[nt:]):
            o[...] += g

    out = pl.pallas_call(
        body, name=name, grid=(nc // cb, nb), in_specs=specs, out_specs=out_specs, out_shape=out_shapes,
        compiler_params=_cparams(("parallel", "arbitrary")),
    )(*args)
    return out[:nt], out[nt:]


EXP2_SCALE = ATTN_SCALE * math.log2(math.e)


HEAD_TILE = 128
N_HEAD_PAIRS = N_ATTN_HEADS // 2


def _head_lanes():
    lane = lax.broadcasted_iota(jnp.int32, (1, HEAD_TILE), 1)
    return lane < QK_NOPE_DIM, (lane >= QK_NOPE_DIM) & (lane < QK_DIM)


def _attn_specs(tq, lk):
    q = pl.BlockSpec((1, tq, 2 * HEAD_TILE), lambda b, pr, j: (b, j, pr))
    kv = pl.BlockSpec((1, lk, 2 * HEAD_TILE), lambda b, pr, j: (b, 0, pr))
    kr = pl.BlockSpec((1, lk, HEAD_TILE), lambda b, pr, j: (b, 0, 0))
    o = pl.BlockSpec((1, tq, HEAD_TILE), lambda b, pr, j: (b, j, pr))
    lse = pl.BlockSpec((1, 2, tq, 1), lambda b, pr, j: (b, pr, j, 0))
    return q, kv, kr, o, lse


def _attn_fwd(q, kv, kr, *, tq, name):
    nb, s, _ = q.shape
    lk = kv.shape[1]
    qs, kvs, krs, os_, lses = _attn_specs(tq, lk)

    def body(q_ref, kv_ref, kr_ref, o_ref, lse_ref):
        low, _ = _head_lanes()
        outs = []
        for e in range(2):
            tile = pl.ds(HEAD_TILE * e, HEAD_TILE)
            kv_e = kv_ref[0, :, tile]
            keys = jnp.where(low, kv_e, kr_ref[0])
            sc = _dot(q_ref[0, :, tile], keys, ((1,), (1,)))
            m = jnp.max(sc, axis=-1, keepdims=True)
            p = jnp.exp2((sc - m) * EXP2_SCALE)
            denom = jnp.sum(p, axis=-1, keepdims=True)
            outs.append(_dot(p, kv_e, ((1,), (0,))) / denom)
            lse_ref[0, e] = m * EXP2_SCALE + jnp.log2(denom)
        o_ref[0] = jnp.where(low, pltpu.roll(outs[0], V_HEAD_DIM, 1), outs[1])

    return pl.pallas_call(
        body, name=name, grid=(nb, N_HEAD_PAIRS, s // tq), in_specs=[qs, kvs, krs], out_specs=[os_, lses],
        out_shape=[jax.ShapeDtypeStruct((nb, s, ATTN_WIDTH), F32), jax.ShapeDtypeStruct((nb, N_ATTN_HEADS, s, 1), F32)],
        compiler_params=_cparams(("parallel", "parallel", "parallel")),
    )(q, kv, kr)


def _attn_bwd(q, kv, kr, o, lse, do, *, tq, name):
    nb, s, _ = q.shape
    lk = kv.shape[1]
    nj = s // tq
    qs, kvs, krs, os_, lses = _attn_specs(tq, lk)

    def body(q_ref, kv_ref, kr_ref, o_ref, lse_ref, do_ref, dq_ref, dkv_ref, dkr_ref):
        pr, j = pl.program_id(1), pl.program_id(2)
        low, rope = _head_lanes()
        do_pair = do_ref[0]
        prod = do_pair * o_ref[0]

        @pl.when(j == 0)
        def _():
            dkv_ref[...] = jnp.zeros_like(dkv_ref)

        @pl.when((pr == 0) & (j == 0))
        def _():
            dkr_ref[...] = jnp.zeros_like(dkr_ref)

        dkr = None
        for e in range(2):
            tile = pl.ds(HEAD_TILE * e, HEAD_TILE)
            delta = jnp.sum(jnp.where(low if e == 0 else ~low, prod, 0.0), axis=-1, keepdims=True)
            do_e = jnp.where(low, 0.0, do_pair if e == 1 else pltpu.roll(do_pair, V_HEAD_DIM, 1))
            kv_e, q_e = kv_ref[0, :, tile], q_ref[0, :, tile]
            keys = jnp.where(low, kv_e, kr_ref[0])
            sc = _dot(q_e, keys, ((1,), (1,)))
            p = jnp.exp2(sc * EXP2_SCALE - lse_ref[0, e])
            dp = _dot(do_e, kv_e, ((1,), (1,)))
            ds = (p * (dp - delta)).astype(MXU_DTYPE)
            dq_ref[0, :, tile] = _dot(ds, keys, ((1,), (0,))) * ATTN_SCALE
            dkeys = _dot(ds, q_e, ((0,), (0,)))
            dv = _dot(p, do_e, ((0,), (0,)))
            dkv_ref[0, :, tile] += jnp.where(low, dkeys, dv)
            part = jnp.where(rope, dkeys, 0.0)
            dkr = part if dkr is None else dkr + part
        dkr_ref[0] += dkr

        @pl.when(j == nj - 1)
        def _():
            for e in range(2):
                tile = pl.ds(HEAD_TILE * e, HEAD_TILE)
                dkv_ref[0, :, tile] = dkv_ref[0, :, tile] * jnp.where(low, ATTN_SCALE, 1.0)

        @pl.when((pr == N_HEAD_PAIRS - 1) & (j == nj - 1))
        def _():
            dkr_ref[0] = dkr_ref[0] * ATTN_SCALE

    return pl.pallas_call(
        body, name=name, grid=(nb, N_HEAD_PAIRS, nj), in_specs=[qs, kvs, krs, os_, lses, os_],
        out_specs=[qs, kvs, krs],
        out_shape=[jax.ShapeDtypeStruct(q.shape, F32), jax.ShapeDtypeStruct(kv.shape, F32),
                   jax.ShapeDtypeStruct(kr.shape, F32)],
        compiler_params=_cparams(("parallel", "arbitrary", "arbitrary")),
    )(q, kv, kr, o, lse, do)


N_PAIRS = HEADS_PER_GROUP // 2
PAIR_W = 2 * SSD_HEAD_DIM


def _ssd_chunk(states, xs, dtc, dtr, bm, cm, ac, ar, *, reverse):
    q = dtc.shape[0]
    row = lax.broadcasted_iota(jnp.int32, (q, q), 0)
    col = lax.broadcasted_iota(jnp.int32, (q, q), 1)
    if reverse:
        tri_c, tri_r, mask = col < row, row < col, col >= row
    else:
        tri_c, tri_r, mask = col <= row, row <= col, col <= row
    a_col, a_row = dtc * ac, dtr * ar
    cum_c = _dot_exact(tri_c.astype(F32), a_col)
    cum_r = _dot_exact(a_row, tri_r.astype(F32))
    tot = jnp.sum(a_col, axis=0, keepdims=True)
    cb = _dot(cm, bm, ((1,), (1,)))
    first = lax.broadcasted_iota(jnp.int32, (1, PAIR_W), 1) < SSD_HEAD_DIM
    first_rows = lax.broadcasted_iota(jnp.int32, (PAIR_W, 1), 0) < SSD_HEAD_DIM
    ys, new_states = [], []
    for pr in range(N_PAIRS):
        per_head = []
        for h in range(2):
            e = 2 * pr + h
            cc, cr, dc, te = _col_of(cum_c, e), _row_of(cum_r, e), _col_of(dtc, e), _col_of(tot, e)
            if reverse:
                within = jnp.exp(jnp.where(mask, cr - cc, -jnp.inf))
                into, to_end = jnp.exp(te - cc), jnp.exp(cc)
            else:
                within = jnp.exp(jnp.where(mask, cc - cr, -jnp.inf))
                into, to_end = jnp.exp(cc), jnp.exp(te - cc)
            per_head.append((cb * within, dc, into, to_end, jnp.exp(te)))
        (m0, dc0, in0, end0, t0), (m1, dc1, in1, end1, t1) = per_head
        xd = xs[pr] * jnp.where(first, dc0, dc1)
        y_diag = (_dot(m0, jnp.where(first, xd, 0.0), ((1,), (0,)))
                  + _dot(m1, jnp.where(first, 0.0, xd), ((1,), (0,))))
        y_off = _dot(cm, states[pr], ((1,), (1,))) * jnp.where(first, in0, in1)
        ys.append(y_diag + y_off)
        grow = _dot(xd * jnp.where(first, end0, end1), bm, ((0,), (0,)))
        new_states.append(states[pr] * jnp.where(first_rows, t0, t1) + grow)
    return tuple(ys) + tuple(new_states)


def _chunk_of_step(t, ncc, nch, reverse):
    if not reverse:
        return t
    return jnp.where(t < ncc, ncc - 1 - t, nch - 1 - (t - ncc))


X_COLS = D_INNER // SSD_GROUPS
GROUP_COLS = X_COLS + 2 * SSD_STATE


def _scan_in_specs(nch, ncc, reverse, back):
    q, e = SSD_CHUNK, HEADS_PER_GROUP

    def ch(t):
        return _chunk_of_step((nch - 1 - t) if back else t, ncc, nch, reverse)

    return ch, [
        pl.BlockSpec((1, q, GROUP_COLS), lambda b, g, t: (b, ch(t), g)),
        pl.BlockSpec((1, 1, q, e), lambda b, g, t: (b, g, ch(t), 0)),
        pl.BlockSpec((1, 1, e, q), lambda b, g, t: (b, g, 0, ch(t))),
        pl.BlockSpec((1, 1, e), lambda b, g, t: (g, 0, 0)),
        pl.BlockSpec((1, e, 1), lambda b, g, t: (g, 0, 0)),
        pl.BlockSpec((1, 1, X_COLS), lambda b, g, t: (g, 0, 0)),
    ]


def _scan_chunk_fn(reverse, skip):
    def f(states, xs, dtc, dtr, bm, cm, ac, ar, d):
        res = _ssd_chunk(states, xs, dtc, dtr, bm, cm, ac, ar, reverse=reverse)
        if not skip:
            return res
        ys = tuple(res[i] + d[:, PAIR_W * i:PAIR_W * (i + 1)] * xs[i] for i in range(N_PAIRS))
        return ys + tuple(res[N_PAIRS:])

    return f


def _scan_operands(x_ref, dtc_ref, dtr_ref, ac_ref, ar_ref, d_ref):
    xs = [x_ref[0, :, pl.ds(PAIR_W * i, PAIR_W)] for i in range(N_PAIRS)]
    bm = x_ref[0, :, pl.ds(X_COLS, SSD_STATE)]
    cm = x_ref[0, :, pl.ds(X_COLS + SSD_STATE, SSD_STATE)]
    return xs, dtc_ref[0, 0], dtr_ref[0, 0], bm, cm, ac_ref[0], ar_ref[0], d_ref[0]


N_IN = 6


def _scan_fwd(dirs, *, ncc, name):
    nb, lt, _ = dirs[0][0].shape
    q, n = SSD_CHUNK, SSD_STATE
    nch = lt // q
    in_specs, out_specs, out_shapes, fs = [], [], [], []
    for dr in range(2):
        ch, specs = _scan_in_specs(nch, ncc, bool(dr), False)
        in_specs += specs
        fs.append(_scan_chunk_fn(bool(dr), dr == 0))
        out_specs += [pl.BlockSpec((1, q, X_COLS), lambda b, g, t, ch=ch: (b, ch(t), g)),
                      pl.BlockSpec((1, 1, 1, N_PAIRS, PAIR_W, n), lambda b, g, t: (b, g, t, 0, 0, 0))]
        out_shapes += [jax.ShapeDtypeStruct((nb, lt, D_INNER), F32),
                       jax.ShapeDtypeStruct((nb, SSD_GROUPS, nch, N_PAIRS, PAIR_W, n), F32)]

    def body(*refs):
        ins, outs, sts = refs[:2 * N_IN], refs[2 * N_IN:2 * N_IN + 4], refs[2 * N_IN + 4:]
        t = pl.program_id(2)

        @pl.when(t == 0)
        def _():
            for st_ref in sts:
                st_ref[...] = jnp.zeros_like(st_ref)

        for dr in range(2):
            (y_ref, ent_ref), st_ref = outs[2 * dr:2 * dr + 2], sts[dr]
            states = [st_ref[i] for i in range(N_PAIRS)]
            for i in range(N_PAIRS):
                ent_ref[0, 0, 0, i] = states[i]
            res = fs[dr](states, *_scan_operands(*ins[N_IN * dr:N_IN * (dr + 1)]))
            for i in range(N_PAIRS):
                y_ref[0, :, pl.ds(PAIR_W * i, PAIR_W)] = res[i]
                st_ref[i] = res[N_PAIRS + i]

    out = pl.pallas_call(
        body, name=name, grid=(nb, SSD_GROUPS, nch), in_specs=in_specs, out_specs=out_specs, out_shape=out_shapes,
        scratch_shapes=[pltpu.VMEM((N_PAIRS, PAIR_W, n), F32)] * 2,
        compiler_params=_cparams(("parallel", "parallel", "arbitrary")),
    )(*dirs[0], *dirs[1])
    return out[:2], out[2:]


N_SCAN_GRADS = 6


def _scan_bwd(dirs, entering, dy, *, ncc, name):
    nb, lt, _ = dirs[0][0].shape
    q, n, e = SSD_CHUNK, SSD_STATE, HEADS_PER_GROUP
    nch = lt // q
    in_specs, out_specs, out_shapes, fs, args = [], [], [], [], []
    for dr in range(2):
        ch, specs = _scan_in_specs(nch, ncc, bool(dr), True)
        in_specs += specs + [
            pl.BlockSpec((1, 1, 1, N_PAIRS, PAIR_W, n), lambda b, g, t: (b, g, nch - 1 - t, 0, 0, 0)),
            pl.BlockSpec((1, q, X_COLS), lambda b, g, t, ch=ch: (b, ch(t), g))]
        args += list(dirs[dr]) + [entering[dr], dy]
        fs.append(_scan_chunk_fn(bool(dr), dr == 0))
        out_specs += [pl.BlockSpec((1, q, GROUP_COLS), lambda b, g, t, ch=ch: (b, ch(t), g)),
                      pl.BlockSpec((1, 1, q, e), lambda b, g, t, ch=ch: (b, g, ch(t), 0)),
                      pl.BlockSpec((1, 1, e, q), lambda b, g, t, ch=ch: (b, g, 0, ch(t))),
                      pl.BlockSpec((1, 1, 1, e), lambda b, g, t: (b, g, 0, 0)),
                      pl.BlockSpec((1, 1, e, 1), lambda b, g, t: (b, g, 0, 0)),
                      pl.BlockSpec((1, 1, 1, X_COLS), lambda b, g, t: (b, g, 0, 0))]
        out_shapes += [jax.ShapeDtypeStruct((nb, lt, SSD_GROUPS * GROUP_COLS), F32),
                       jax.ShapeDtypeStruct((nb, SSD_GROUPS, lt, e), F32), jax.ShapeDtypeStruct((nb, SSD_GROUPS, e, lt), F32),
                       jax.ShapeDtypeStruct((nb, SSD_GROUPS, 1, e), F32), jax.ShapeDtypeStruct((nb, SSD_GROUPS, e, 1), F32),
                       jax.ShapeDtypeStruct((nb, SSD_GROUPS, 1, X_COLS), F32)]
    n_in = N_IN + 2

    def body(*refs):
        ins = refs[:2 * n_in]
        outs = refs[2 * n_in:2 * n_in + 2 * N_SCAN_GRADS]
        dss = refs[2 * n_in + 2 * N_SCAN_GRADS:]
        t = pl.program_id(2)

        for dr in range(2):
            mine = ins[n_in * dr:n_in * (dr + 1)]
            ent_ref, dy_ref = mine[N_IN], mine[N_IN + 1]
            dx_ref, ddtc_ref, ddtr_ref, dac_ref, dar_ref, dd_ref = outs[N_SCAN_GRADS * dr:N_SCAN_GRADS * (dr + 1)]
            ds_ref = dss[dr]

            @pl.when(t == 0)
            def _():
                for ref in (ds_ref, dac_ref, dar_ref, dd_ref):
                    ref[...] = jnp.zeros_like(ref)

            states = [ent_ref[0, 0, 0, i] for i in range(N_PAIRS)]
            _, vjp = jax.vjp(fs[dr], states, *_scan_operands(*mine[:N_IN]))
            dys = [dy_ref[0, :, pl.ds(PAIR_W * i, PAIR_W)] for i in range(N_PAIRS)]
            gs, gx, gdtc, gdtr, gb, gc, gac, gar, gd = vjp(tuple(dys) + tuple(ds_ref[i] for i in range(N_PAIRS)))
            o = 0
            for part in list(gx) + [gb, gc]:
                dx_ref[0, :, pl.ds(o, part.shape[1])] = part
                o += part.shape[1]
            for i in range(N_PAIRS):
                ds_ref[i] = gs[i]
            ddtc_ref[0, 0] = gdtc
            ddtr_ref[0, 0] = gdtr
            dac_ref[0, 0] += gac
            dar_ref[0, 0] += gar
            dd_ref[0, 0] += gd

    out = pl.pallas_call(
        body, name=name, grid=(nb, SSD_GROUPS, nch), in_specs=in_specs, out_specs=out_specs, out_shape=out_shapes,
        scratch_shapes=[pltpu.VMEM((N_PAIRS, PAIR_W, n), F32)] * 2,
        compiler_params=_cparams(("parallel", "parallel", "arbitrary")),
    )(*args)
    return out[:N_SCAN_GRADS], out[N_SCAN_GRADS:]


def _adamw(w, g, m, v, *, name):
    r, c = w.shape
    tr = _pick(r, (256, 176, 128, 96, 64, 8))
    c1 = 1.0 / (1.0 - ADAM_B1 ** ADAM_STEP)
    c2 = 1.0 / (1.0 - ADAM_B2 ** ADAM_STEP)

    def body(w_ref, g_ref, m_ref, v_ref, d_ref, nm_ref, nv_ref):
        gv = g_ref[...]
        nm = ADAM_B1 * m_ref[...] + (1.0 - ADAM_B1) * gv
        nv = ADAM_B2 * v_ref[...] + (1.0 - ADAM_B2) * (gv * gv)
        d_ref[...] = -ADAM_LR * ((nm * c1) / (jnp.sqrt(nv * c2) + ADAM_EPS) + ADAM_WD * w_ref[...])
        nm_ref[...] = nm
        nv_ref[...] = nv

    spec = pl.BlockSpec((tr, c), lambda i: (i, 0))
    return pl.pallas_call(
        body, name=name, grid=(r // tr,), in_specs=[spec] * 4, out_specs=[spec] * 3,
        out_shape=[jax.ShapeDtypeStruct((r, c), F32)] * 3, compiler_params=_cparams(("parallel",)),
    )(w, g, m, v)


def _sum_rows_tile(r):
    return r if r <= 1024 else _pick(r, (656, 512, 256, 128, 64, 32, 16))


def _sum_slots(x, *, out_dtype, name):
    n, r, c = x.shape
    tr = _sum_rows_tile(r)

    def body(x_ref, o_ref):
        acc = x_ref[0].astype(F32)
        for k in range(1, n):
            acc = acc + x_ref[k].astype(F32)
        o_ref[...] = acc.astype(o_ref.dtype)

    return pl.pallas_call(
        body, name=name, grid=(r // tr,), in_specs=[pl.BlockSpec((n, tr, c), lambda i: (0, i, 0))],
        out_specs=pl.BlockSpec((tr, c), lambda i: (i, 0)), out_shape=jax.ShapeDtypeStruct((r, c), out_dtype),
        compiler_params=_cparams(("parallel",)),
    )(x)


def _sum_list(xs, *, out_dtype, name):
    r, c = xs[0].shape
    tr = _sum_rows_tile(r)

    def body(*refs):
        acc = refs[0][...].astype(F32)
        for ref in refs[1:-1]:
            acc = acc + ref[...].astype(F32)
        refs[-1][...] = acc.astype(refs[-1].dtype)

    spec = pl.BlockSpec((tr, c), lambda i: (i, 0))
    return pl.pallas_call(
        body, name=name, grid=(r // tr,), in_specs=[spec] * len(xs), out_specs=spec,
        out_shape=jax.ShapeDtypeStruct((r, c), out_dtype), compiler_params=_cparams(("parallel",)),
    )(*xs)


ANY = pl.BlockSpec(memory_space=pl.ANY)


def _place():
    return lax.axis_index("x"), lax.axis_index("y"), lax.axis_index("c")


def _allgather_small(v, *, name):
    r, c = v.shape

    def body(v_ref, out_ref, send_sems, recv_sems, local_sem):
        x, y, cc = _place()
        me = 4 * x + 2 * y + cc
        mine = pltpu.make_async_copy(v_ref, out_ref.at[me], local_sem)
        mine.start()
        copies = []
        for k in range(1, N_DEV):
            fx, fy, fc = (k >> 2) & 1, (k >> 1) & 1, k & 1
            peer = (1 - x if fx else x, 1 - y if fy else y, 1 - cc if fc else cc)
            copies.append(pltpu.make_async_remote_copy(
                src_ref=v_ref, dst_ref=out_ref.at[me], send_sem=send_sems.at[k - 1], recv_sem=recv_sems.at[k - 1],
                device_id=peer, device_id_type=MESH))
        for cp in copies:
            cp.start()
        for cp in copies:
            cp.wait()
        mine.wait()

    return pl.pallas_call(
        body, name=name, in_specs=[ANY], out_specs=ANY, out_shape=jax.ShapeDtypeStruct((N_DEV, r, c), v.dtype),
        scratch_shapes=[pltpu.SemaphoreType.DMA((N_DEV - 1,)), pltpu.SemaphoreType.DMA((N_DEV - 1,)),
                        pltpu.SemaphoreType.DMA],
    )(v)


def _other_chips(x, y):
    return [(1 - x, y), (x, 1 - y), (1 - x, 1 - y)]


def _gather_shards(mine, *, name):
    r, c = mine.shape
    half = r // 2

    def body(v_ref, out_ref, send_sems, recv_sems):
        x, y, cc = _place()
        sibling = (x, y, 1 - cc)
        chips = _other_chips(x, y)

        def rows(px, py, pc):
            return out_ref.at[2 * px + py, pl.ds(pc * half, half), :]

        def copy(k, block, to, src=None):
            return pltpu.make_async_remote_copy(
                src_ref=rows(*block) if src is None else src, dst_ref=rows(*block),
                send_sem=send_sems.at[k], recv_sem=recv_sems.at[k], device_id=to, device_id_type=MESH)

        my_half = v_ref.at[pl.ds(cc * half, half), :]
        first = [copy(j, (x, y, cc), (*chip, cc), src=my_half) for j, chip in enumerate(chips)]
        for cp in first:
            cp.start()
        passed = [copy(3 + j, (*chip, cc), sibling) for j, chip in enumerate(chips)]
        for j, chip in enumerate(chips):
            copy(j, (*chip, cc), (x, y, cc)).wait_recv()
            passed[j].start()
        for j, chip in enumerate(chips):
            copy(3 + j, (*chip, 1 - cc), (x, y, cc)).wait_recv()
        for cp in first + passed:
            cp.wait_send()

    return pl.pallas_call(
        body, name=name, in_specs=[ANY], out_specs=ANY, out_shape=jax.ShapeDtypeStruct((N_CHIPS, r, c), mine.dtype),
        scratch_shapes=[pltpu.SemaphoreType.DMA((6,)), pltpu.SemaphoreType.DMA((6,))],
    )(mine)


def _swap_halves(g, *, name):
    n, _, r, c = g.shape

    def body(g_ref, got_ref, send_sems, recv_sems):
        x, y, cc = _place()
        sibling = (x, y, 1 - cc)
        rems = []
        for j in range(n):
            rems.append(pltpu.make_async_remote_copy(
                src_ref=g_ref.at[j, 1 - cc], dst_ref=got_ref.at[j], send_sem=send_sems.at[j],
                recv_sem=recv_sems.at[j], device_id=sibling, device_id_type=MESH))
        for cp in rems:
            cp.start()
        for cp in rems:
            cp.wait()

    return pl.pallas_call(
        body, name=name, in_specs=[ANY], out_specs=ANY, out_shape=jax.ShapeDtypeStruct((n, r, c), g.dtype),
        scratch_shapes=[pltpu.SemaphoreType.DMA((n,)), pltpu.SemaphoreType.DMA((n,))],
    )(g)


def _scatter_to_chips(s, *, name):
    n, r, c = s.shape

    def body(s_ref, out_ref, send_sems, recv_sems):
        x, y, cc = _place()
        me = 2 * x + y
        copies = []
        for j, (px, py) in enumerate(_other_chips(x, y)):
            copies.append(pltpu.make_async_remote_copy(
                src_ref=s_ref.at[2 * px + py], dst_ref=out_ref.at[me], send_sem=send_sems.at[j],
                recv_sem=recv_sems.at[j], device_id=(px, py, cc), device_id_type=MESH))
        for cp in copies:
            cp.start()
        for cp in copies:
            cp.wait()

    return pl.pallas_call(
        body, name=name, in_specs=[ANY], out_specs=ANY, out_shape=jax.ShapeDtypeStruct((n, r, c), s.dtype),
        scratch_shapes=[pltpu.SemaphoreType.DMA((3,)), pltpu.SemaphoreType.DMA((3,))],
    )(s)


def _join_halves(f, *, name):
    r, c = f.shape

    def body(f_ref, out_ref, send_sem, recv_sem):
        x, y, cc = _place()
        cp = pltpu.make_async_remote_copy(src_ref=f_ref, dst_ref=out_ref.at[cc], send_sem=send_sem, recv_sem=recv_sem,
                                          device_id=(x, y, 1 - cc), device_id_type=MESH)
        cp.start()
        cp.wait()

    return pl.pallas_call(
        body, name=name, in_specs=[ANY], out_specs=ANY, out_shape=jax.ShapeDtypeStruct((2, r, c), f.dtype),
        scratch_shapes=[pltpu.SemaphoreType.DMA, pltpu.SemaphoreType.DMA],
    )(f)


def _pack_rows(parts, width=PACK_COLS):
    return jnp.concatenate([p.reshape(-1, width) for p in parts], axis=0)


def _pack_small(parts, rows):
    flat = jnp.concatenate([p.reshape(-1).astype(F32) for p in parts])
    return jnp.pad(flat, (0, rows * LANES - flat.shape[0])).reshape(rows, LANES)


def _unpack_small(packed, shapes):
    flat = packed.reshape(-1)
    out, o = [], 0
    for shp in shapes:
        n = int(np.prod(shp))
        out.append(flat[o:o + n].reshape(shp))
        o += n
    return out


def _perm_in_cols(w):
    a, b = Q_LORA_RANK + KV_LORA_RANK, Q_LORA_RANK + KV_LORA_RANK + QK_ROPE_DIM
    c = IN_WIDTH - 2 * N_SSD_HEADS
    return jnp.concatenate([w[:, :a], w[:, b:c], w[:, a:b], w[:, c:]], axis=1)


def _unperm_in_cols(w):
    a = Q_LORA_RANK + KV_LORA_RANK
    zx = D_INNER + XBC_WIDTH
    return jnp.concatenate([w[:, :a], w[:, a + zx:a + zx + QK_ROPE_DIM], w[:, a:a + zx], w[:, a + zx + QK_ROPE_DIM:]],
                           axis=1)


def _group_xbc(a):
    n = SSD_STATE
    parts = []
    for g in range(SSD_GROUPS):
        parts += [a[..., g * X_COLS:(g + 1) * X_COLS], a[..., D_INNER + g * n:D_INNER + (g + 1) * n],
                  a[..., D_INNER + GN + g * n:D_INNER + GN + (g + 1) * n]]
    return jnp.concatenate(parts, axis=-1)


def _ungroup_xbc(a):
    n = SSD_STATE
    xs = [a[..., g * GROUP_COLS:g * GROUP_COLS + X_COLS] for g in range(SSD_GROUPS)]
    bs = [a[..., g * GROUP_COLS + X_COLS:g * GROUP_COLS + X_COLS + n] for g in range(SSD_GROUPS)]
    cs = [a[..., g * GROUP_COLS + X_COLS + n:(g + 1) * GROUP_COLS] for g in range(SSD_GROUPS)]
    return jnp.concatenate(xs + bs + cs, axis=-1)


UP_BLOCK = 256


def _interleave_up(w):
    parts = []
    for j in range(D_FF // UP_BLOCK):
        parts += [w[:, j * UP_BLOCK:(j + 1) * UP_BLOCK], w[:, D_FF + j * UP_BLOCK:D_FF + (j + 1) * UP_BLOCK]]
    return jnp.concatenate(parts, axis=1)


def _deinterleave_up(w):
    blocks = [w[:, j * UP_BLOCK:(j + 1) * UP_BLOCK] for j in range(2 * D_FF // UP_BLOCK)]
    return jnp.concatenate(blocks[0::2] + blocks[1::2], axis=1)


def _pad_q_heads(w):
    k = w.shape[0]
    return jnp.pad(w.reshape(k, N_ATTN_HEADS, QK_DIM), ((0, 0), (0, 0), (0, HEAD_TILE - QK_DIM))).reshape(k, -1)


def _unpad_q_heads(w):
    k = w.shape[0]
    return w.reshape(k, N_ATTN_HEADS, HEAD_TILE)[..., :QK_DIM].reshape(k, N_ATTN_HEADS * QK_DIM)


def _rope_tables(seq_len):
    n_rows = seq_len // GRID_W
    row = jnp.repeat(jnp.arange(n_rows), GRID_W).astype(F32)
    col = jnp.tile(jnp.arange(GRID_W), n_rows).astype(F32)
    axis_dim = QK_ROPE_DIM // 2
    inv_freq = ROPE_THETA ** (-jnp.arange(0, axis_dim, 2, dtype=F32) / axis_dim)
    ang_r = row[:, None] * inv_freq
    ang_c = col[:, None] * inv_freq
    ang = jnp.concatenate([ang_r, ang_r, ang_c, ang_c], axis=-1)
    return jnp.cos(ang), jnp.sin(ang)


def _rot_matrix(width, start):
    r = np.zeros((width, width), np.float32)
    quarter = QK_ROPE_DIM // 4
    for base in (0, QK_ROPE_DIM // 2):
        for i in range(quarter):
            r[start + base + quarter + i, start + base + i] = -1.0
            r[start + base + i, start + base + quarter + i] = 1.0
    return jnp.asarray(r)


ROPE_STEP = QK_ROPE_DIM // 4


def _rope_flat_fn(x, cos, sin_up, sin_down):
    reps = x.shape[1] // cos.shape[1]

    def heads(t):
        return jnp.concatenate([t] * reps, axis=1)

    return (x * heads(cos) + _roll_lanes(x, -ROPE_STEP) * heads(sin_up) + _roll_lanes(x, ROPE_STEP) * heads(sin_down),)


def _krdt_fn(x, cos, sin, rot, bias):
    lane = lax.broadcasted_iota(jnp.int32, (1, KRDT_WIDTH), 1)
    is_dt = (lane >= QK_ROPE_DIM) & (lane < QK_ROPE_DIM + 2 * N_SSD_HEADS)
    roped = x * cos + _dot_exact(x, rot) * sin
    return (jnp.where(is_dt, _softplus(x + bias), roped),)


def _pre_fn(u, w, shift, scale):
    return (_rms(u, w) * (1.0 + scale) + shift,)


def _norm_fn(x, w):
    return (_rms(x, w),)


def _finish_fn(yf, yb, z, w):
    return (_rms((yf + yb) * _silu(z), w),)


def _mid_fn(x, mix, w_post, w_pre, gate, shift, scale):
    x1 = x + gate * _rms(mix, w_post)
    return (x1, _rms(x1, w_pre) * (1.0 + scale) + shift)


def _loss_fn(x1, ffn, tgt, w_post, gate):
    y = x1 + gate * _rms(ffn, w_post)
    err = y - tgt
    return (0.5 * jnp.mean(err * err, axis=-1, keepdims=True),)


def _bias_fn(x, b):
    return (x + b,)


def _silu_fn(x):
    return (_silu(x),)


def kernel(x, c, ctx, c_ctx, w_mod, b_mod, mix_pre_norm, mix_post_norm, w_in, q_norm, w_q_up, kv_norm, w_kv_up, ssd_conv_w, ssd_conv_b, ssd_a_log, ssd_dt_bias, ssd_d, ssd_norm, w_out, ffn_pre_norm, ffn_post_norm, w_up, ffn_conv_w, ffn_conv_b, w_down, loss_target, m_c_ctx, m_w_mod, m_b_mod, m_mix_pre_norm, m_mix_post_norm, m_w_in, m_q_norm, m_w_q_up, m_kv_norm, m_w_kv_up, m_ssd_conv_w, m_ssd_conv_b, m_ssd_a_log, m_ssd_dt_bias, m_ssd_d, m_ssd_norm, m_w_out, m_ffn_pre_norm, m_ffn_post_norm, m_w_up, m_ffn_conv_w, m_ffn_conv_b, m_w_down, v_c_ctx, v_w_mod, v_b_mod, v_mix_pre_norm, v_mix_post_norm, v_w_in, v_q_norm, v_w_q_up, v_kv_norm, v_w_kv_up, v_ssd_conv_w, v_ssd_conv_b, v_ssd_a_log, v_ssd_dt_bias, v_ssd_d, v_ssd_norm, v_w_out, v_ffn_pre_norm, v_ffn_post_norm, v_w_up, v_ffn_conv_w, v_ffn_conv_b, v_w_down):
    args = dict(locals())
    names = ["c_ctx", "w_mod", "b_mod", "mix_pre_norm", "mix_post_norm", "w_in", "q_norm", "w_q_up", "kv_norm",
             "w_kv_up", "ssd_conv_w", "ssd_conv_b", "ssd_a_log", "ssd_dt_bias", "ssd_d", "ssd_norm", "w_out",
             "ffn_pre_norm", "ffn_post_norm", "w_up", "ffn_conv_w", "ffn_conv_b", "w_down"]
    nb, s, d = x.shape
    nctx_rows = ctx.shape[1]
    lt = nctx_rows + s
    tl = 256 if (nctx_rows % 256 == 0 and s % 256 == 0) else 128
    nctx = nctx_rows // tl
    ncc = nctx_rows // SSD_CHUNK
    h, e, g2 = N_ATTN_HEADS, HEADS_PER_GROUP, SSD_GROUPS
    chip = 2 * lax.axis_index("x") + lax.axis_index("y")

    big_local = {n: args[n][0] for n, _, _, _ in BIG}
    packed = _pack_rows([big_local[n].astype(WIRE_DTYPE) for n, _, _, _ in BIG])
    shard_rows = packed.shape[0]
    gathered = _gather_shards(packed, name="gather_weights")
    gathered = lax.dynamic_update_slice(gathered, packed[None], (chip, 0, 0))
    full, o = {}, 0
    for n, rows, cols, axis in BIG:
        lr, lc = big_local[n].shape
        nr = lr * lc // PACK_COLS
        seg = gathered[:, o:o + nr].reshape(N_CHIPS, lr, lc)
        o += nr
        full[n] = seg.reshape(rows, cols) if axis == 0 else jnp.transpose(seg, (1, 0, 2)).reshape(rows, cols)
    n_sc, n_fc = ssd_conv_w.shape[2], ffn_conv_w.shape[2]
    n_conv = SSD_CONV * n_sc + FFN_CONV * n_fc
    first_rows = -(-(n_conv + nb * d) // (8 * LANES)) * 8
    first_all = _allgather_small(_pack_small([ssd_conv_w[0], ffn_conv_w[0], c], first_rows), name="gather_conv_c")
    first_all = first_all.reshape(N_DEV, -1)
    conv_all = first_all[::2]
    ssd_conv_full = jnp.concatenate(
        [conv_all[j][:SSD_CONV * n_sc].reshape(SSD_CONV, n_sc) for j in range(N_CHIPS)], axis=1)
    ffn_conv_full = jnp.concatenate(
        [conv_all[j][SSD_CONV * n_sc:n_conv].reshape(FFN_CONV, n_fc) for j in range(N_CHIPS)], axis=1)
    c_every = first_all[:, n_conv:n_conv + nb * d].reshape(N_DEV * nb, d)

    w_in_p = _perm_in_cols(full["w_in"])
    o_cq, o_ckv, o_z = 0, Q_LORA_RANK, Q_LORA_RANK + KV_LORA_RANK
    o_xbc, o_kr = o_z + D_INNER, o_z + D_INNER + XBC_WIDTH
    w_krdt = jnp.pad(w_in_p[:, o_kr:], ((0, 0), (0, KRDT_WIDTH - QK_ROPE_DIM - 2 * N_SSD_HEADS)))
    w_segs = [w_in_p[:, o_cq:o_ckv], w_in_p[:, o_ckv:o_z], w_in_p[:, o_z:o_xbc], _group_xbc(w_in_p[:, o_xbc:o_kr]),
              w_krdt]
    ssd_conv_g, ssd_conv_b_g = _group_xbc(ssd_conv_full), _group_xbc(ssd_conv_b)
    w_q_pad = _pad_q_heads(full["w_q_up"])
    w_up_il = _interleave_up(full["w_up"])
    w_out_a, w_out_s = full["w_out"][:ATTN_WIDTH], full["w_out"][ATTN_WIDTH:]

    mod_rows = 16
    n_ex = N_DEV * nb
    all_rows = -(-(n_ex + 1) // 16) * 16
    me = 2 * chip + lax.axis_index("c")
    c_all = jnp.concatenate([c_every, c_ctx[None, :], jnp.zeros((all_rows - n_ex - 1, d), F32)], axis=0)[None]
    (s_all,) = _row_fwd(_silu_fn, toks=[(c_all, 0, None, 0)], outs=[(d, F32)], nb=1, nl=all_rows, tl=all_rows,
                        name="mod_silu")
    w_mod_local = w_mod[0]
    mod_cols = w_mod_local.shape[1]
    mod_part = _mm(s_all[0], w_mod_local, name="mod_mm")
    mod_parts = _allgather_small(mod_part, name="gather_mod")[::2]
    mod_every = jnp.concatenate([mod_parts[j] for j in range(N_CHIPS)], axis=1)
    mod_lin = jnp.concatenate([lax.dynamic_slice_in_dim(mod_every, me * nb, nb, axis=0), mod_every[n_ex:n_ex + 1],
                               jnp.zeros((mod_rows - nb - 1, N_MOD * d), F32)], axis=0)
    (mod,) = _row_fwd(_bias_fn, toks=[(mod_lin[None], 0, None, 0)], vecs=[b_mod], outs=[(N_MOD * d, F32)], nb=1,
                      nl=mod_rows, tl=mod_rows, name="mod_bias")
    mods = [mod[0][:, k * d:(k + 1) * d][:, None, :] for k in range(N_MOD)]
    mods_lat = [m[:nb] for m in mods]

    u = jnp.concatenate([ctx, x], axis=1)
    (h1,) = _row_fwd(_pre_fn, toks=[(u, 0, None, 0)], vecs=[mix_pre_norm], bvecs=[mods[0], mods[1]],
                     outs=[(d, MXU_DTYPE)], nb=nb, nl=lt, tl=tl, nctx=nctx, name="pre1")
    h1f = h1.reshape(nb * lt, d)
    p_cq, p_ckv, p_z, p_xbc, p_krdt = [
        _mm(h1f, w, name="in_" + nm).reshape(nb, lt, -1)
        for nm, w in zip(("cq", "ckv", "z", "xbc", "krdt"), w_segs)]

    (cqn,) = _row_fwd(_norm_fn, toks=[(p_cq, nctx, None, 0)], vecs=[q_norm], outs=[(Q_LORA_RANK, MXU_DTYPE)],
                      nb=nb, nl=s, tl=tl, name="q_norm")
    q_flat = _mm(cqn.reshape(nb * s, -1), w_q_pad, name="q_up").reshape(nb, s, h * HEAD_TILE)
    cos, sin = _rope_tables(s)
    ones, zeros = jnp.ones((s, QK_NOPE_DIM), F32), jnp.zeros((s, QK_NOPE_DIM), F32)
    tail = HEAD_TILE - QK_DIM
    up_lanes = ((jnp.arange(QK_ROPE_DIM) // ROPE_STEP) % 2 == 0)[None, :]
    q_tables = [jnp.concatenate([pad, t, pad[:, :tail]], axis=1)[None]
                for pad, t in ((ones, cos), (zeros, jnp.where(up_lanes, -sin, 0.0)), (zeros, jnp.where(up_lanes, 0.0, sin)))]
    tq = 256
    (q_roped,) = _row_fwd(_rope_flat_fn, toks=[(q_flat, 0, None, 0)], poss=q_tables, outs=[(h * HEAD_TILE, MXU_DTYPE)],
                          nb=nb, nl=s, tl=tl, name="rope_q")

    (ckvn,) = _row_fwd(_norm_fn, toks=[(p_ckv, 0, None, 0)], vecs=[kv_norm], outs=[(KV_LORA_RANK, MXU_DTYPE)],
                       nb=nb, nl=lt, tl=tl, name="kv_norm")
    kv_flat = _mm(ckvn.reshape(nb * lt, -1), full["w_kv_up"], out_dtype=MXU_DTYPE, name="kv_up").reshape(nb, lt, -1)

    pad_w = KRDT_WIDTH - QK_ROPE_DIM
    cos_k = jnp.concatenate([jnp.ones((nctx_rows, KRDT_WIDTH), F32),
                             jnp.concatenate([cos, jnp.ones((s, pad_w), F32)], axis=1)], axis=0)[None]
    sin_k = jnp.concatenate([jnp.zeros((nctx_rows, KRDT_WIDTH), F32),
                             jnp.concatenate([sin, jnp.zeros((s, pad_w), F32)], axis=1)], axis=0)[None]
    rot_k = _rot_matrix(KRDT_WIDTH, 0)
    dt_bias_row = jnp.pad(ssd_dt_bias.reshape(1, -1), ((0, 0), (QK_ROPE_DIM, pad_w - 2 * N_SSD_HEADS)))
    (krdt,) = _row_fwd(_krdt_fn, toks=[(p_krdt, 0, None, 0)], poss=[cos_k, sin_k], vecs=[rot_k, dt_bias_row],
                       outs=[(KRDT_WIDTH, F32)], nb=nb, nl=lt, tl=tl, name="krdt")
    kr = jnp.pad(krdt[..., :QK_ROPE_DIM].astype(MXU_DTYPE), ((0, 0), (0, 0), (QK_NOPE_DIM, HEAD_TILE - QK_DIM)))
    attn, lse = _attn_fwd(q_roped, kv_flat, kr, tq=tq, name="attn_fwd")

    seg = nctx_rows

    def conv_ssd_fn(xv, w, b):
        return (_silu(_dwconv(xv, w, seg) + b),)

    cb_ssd = 256
    conv_vecs = [(ssd_conv_g, 0), (ssd_conv_b_g, 0)]
    (xbc,) = _seq_fwd(conv_ssd_fn, toks=[(p_xbc, 0, 1)], vecs=conv_vecs, outs=[F32], nb=nb, nl=lt, nc=XBC_WIDTH,
                      cb=cb_ssd, name="conv_ssd")
    dt = krdt[..., QK_ROPE_DIM:QK_ROPE_DIM + 2 * N_SSD_HEADS].reshape(nb, lt, 2, g2, e)
    dtc = jnp.transpose(dt, (2, 0, 3, 1, 4))
    dtr = jnp.transpose(dt, (2, 0, 3, 4, 1))
    a_neg = -jnp.exp(ssd_a_log[0]).reshape(2, g2, e)
    d_chan = jnp.repeat(ssd_d[0], SSD_HEAD_DIM).reshape(g2, 1, X_COLS)
    scan_args = [(xbc, dtc[dr], dtr[dr], a_neg[dr][:, None, :], a_neg[dr][:, :, None], d_chan) for dr in range(2)]
    (y0, ent0), (y1, ent1) = _scan_fwd(scan_args, ncc=ncc, name="scan_fwd")
    ys, ents = [y0, y1], [ent0, ent1]
    (ssd,) = _row_fwd(_finish_fn, toks=[(ys[0], nctx, None, 0), (ys[1], nctx, None, 0), (p_z, nctx, None, 0)],
                      vecs=[ssd_norm], outs=[(D_INNER, MXU_DTYPE)], nb=nb, nl=s, tl=tl, name="ssd_finish")

    attn_f, ssd_f = attn.reshape(nb * s, ATTN_WIDTH), ssd.reshape(nb * s, D_INNER)
    mix = _mm_sum([(attn_f, w_out_a), (ssd_f, w_out_s)], name="out_proj").reshape(nb, s, d)

    mid_bvecs = [mods_lat[2], mods_lat[3], mods_lat[4]]
    x1, h2 = _row_fwd(_mid_fn, toks=[(x, 0, None, 0), (mix, 0, None, 0)], vecs=[mix_post_norm, ffn_pre_norm],
                      bvecs=mid_bvecs, outs=[(d, F32), (d, MXU_DTYPE)], nb=nb, nl=s, tl=tl, name="mid")
    up = _mm(h2.reshape(nb * s, d), w_up_il, name="ffn_up").reshape(nb, s, 2 * D_FF)

    def glu_fn(gv, w, b):
        return (_gelu(_dwconv(gv[:, :UP_BLOCK], w, 0) + b) * gv[:, UP_BLOCK:],)

    cb_ffn = UP_BLOCK
    glu_toks = [(up, 0, 2)]
    glu_vecs = [(ffn_conv_full, 0), (ffn_conv_b, 0)]
    (act,) = _seq_fwd(glu_fn, toks=glu_toks, vecs=glu_vecs, outs=[MXU_DTYPE], nb=nb, nl=s, nc=D_FF, cb=cb_ffn,
                      name="conv_glu")
    ffn = _mm(act.reshape(nb * s, D_FF), full["w_down"], name="ffn_down").reshape(nb, s, d)

    loss_toks = [(x1, 0, None, 0), (ffn, 0, None, 0), (loss_target, 0, None, 0)]
    ones_rows = jnp.ones((nb, s, 1), F32)
    (dx1_a, dffn, _), (g_ffn_post,), ((g_gate5, _),), (loss_rows,) = _row_bwd(
        _loss_fn, toks=loss_toks, vecs=[ffn_post_norm], bvecs=[mods_lat[5]], cots=[(ones_rows, 0)],
        tok_grads=[F32, MXU_DTYPE, None], emit=[(0, 1, F32)], nb=nb, nl=s, tl=tl, name="loss_bwd")
    loss_part = jnp.sum(loss_rows)

    dffn_f = dffn.reshape(nb * s, d)
    g_w_down = _mm(act.reshape(nb * s, D_FF), dffn_f, ta=True, name="wg_down")
    dact = _mm(dffn_f, full["w_down"], tb=True, out_dtype=MXU_DTYPE, name="dg_down").reshape(nb, s, D_FF)
    (dup,), (g_ffn_conv_w, g_ffn_conv_b) = _seq_bwd(
        glu_fn, toks=glu_toks, vecs=glu_vecs, cots=[[dact]], tok_grads=[MXU_DTYPE], nb=nb, nl=s, nc=D_FF,
        cb=cb_ffn, name="conv_glu_bwd")
    dup = dup.reshape(nb * s, 2 * D_FF)
    g_w_up = _deinterleave_up(_mm(h2.reshape(nb * s, d), dup, ta=True, name="wg_up"))
    dh2 = _mm(dup, w_up_il, tb=True, name="dg_up").reshape(nb, s, d)

    (dx_res, dmix), (g_mix_post, g_ffn_pre), ((g_gate2, _), (g_shift3, _), (g_scale4, _)), _ = _row_bwd(
        _mid_fn, toks=[(x, 0, None, 0), (mix, 0, None, 0)], vecs=[mix_post_norm, ffn_pre_norm], bvecs=mid_bvecs,
        cots=[(dx1_a, 0), (dh2, 0)], tok_grads=[F32, MXU_DTYPE], nb=nb, nl=s, tl=tl, name="mid_bwd")

    dmix_f = dmix.reshape(nb * s, d)
    g_w_out = jnp.concatenate([_mm(attn_f, dmix_f, ta=True, name="wg_out_attn"),
                               _mm(ssd_f, dmix_f, ta=True, name="wg_out_ssd")], axis=0)
    dattn = _mm(dmix_f, w_out_a, tb=True, name="dg_out_attn").reshape(nb, s, ATTN_WIDTH)
    dssd = _mm(dmix_f, w_out_s, tb=True, name="dg_out_ssd").reshape(nb, s, D_INNER)

    (dy, _, dz), (g_ssd_norm,), _, _ = _row_bwd(
        _finish_fn, toks=[(ys[0], 0, None, 0), (ys[1], 0, None, 0), (p_z, 0, None, 0)], vecs=[ssd_norm],
        cots=[(dssd, -nctx)], tok_grads=[F32, None, MXU_DTYPE], nb=nb, nl=lt, tl=tl, name="ssd_finish_bwd")
    scan_grads = _scan_bwd(scan_args, ents, dy, ncc=ncc, name="scan_bwd")
    g_dt_dirs, g_a = [], []
    for _, gdtc, gdtr, gac, gar, _ in scan_grads:
        g_dt_dirs.append(jnp.transpose(gdtc, (0, 2, 1, 3)) + jnp.transpose(gdtr, (0, 3, 1, 2)))
        g_a.append(jnp.sum(gac[:, :, 0, :] + gar[:, :, :, 0], axis=0))
    g_d_chan = jnp.sum(scan_grads[0][5], axis=0)
    g_a_log = (jnp.stack(g_a) * a_neg).reshape(1, 2, N_SSD_HEADS)
    g_dt = jnp.stack(g_dt_dirs, axis=2).reshape(nb, lt, 2 * N_SSD_HEADS)
    (dp_xbc,), (g_ssd_conv_w, g_ssd_conv_b) = _seq_bwd(
        conv_ssd_fn, toks=[(p_xbc, 0, 1)], vecs=conv_vecs, cots=[[scan_grads[0][0], scan_grads[1][0]]],
        tok_grads=[MXU_DTYPE], nb=nb, nl=lt, nc=XBC_WIDTH, cb=cb_ssd, name="conv_ssd_bwd")
    g_ssd_conv_w, g_ssd_conv_b = _ungroup_xbc(g_ssd_conv_w), _ungroup_xbc(g_ssd_conv_b)

    dq_roped, dkv, dkr = _attn_bwd(q_roped, kv_flat, kr, attn, lse, dattn, tq=tq, name="attn_bwd")
    (dq_flat,), _, _, _ = _row_bwd(_rope_flat_fn, toks=[(q_flat, 0, None, 0)], poss=q_tables, cots=[(dq_roped, 0)],
                                   tok_grads=[MXU_DTYPE], nb=nb, nl=s, tl=tl, name="rope_q_bwd")
    dq_flat = dq_flat.reshape(nb * s, h * HEAD_TILE)
    g_w_q_up = _unpad_q_heads(_mm(cqn.reshape(nb * s, -1), dq_flat, ta=True, name="wg_q_up"))
    dcqn = _mm(dq_flat, w_q_pad, tb=True, name="dg_q_up").reshape(nb, s, Q_LORA_RANK)
    (dp_cq,), (g_q_norm,), _, _ = _row_bwd(_norm_fn, toks=[(p_cq, 0, None, 0)], vecs=[q_norm], cots=[(dcqn, -nctx)],
                                           tok_grads=[MXU_DTYPE], nb=nb, nl=lt, tl=tl, name="q_norm_bwd")

    dkv_flat = dkv.reshape(nb * lt, -1)
    g_w_kv_up = _mm(ckvn.reshape(nb * lt, -1), dkv_flat, ta=True, name="wg_kv_up")
    dckvn = _mm(dkv_flat, full["w_kv_up"], tb=True, name="dg_kv_up").reshape(nb, lt, KV_LORA_RANK)
    (dp_ckv,), (g_kv_norm,), _, _ = _row_bwd(_norm_fn, toks=[(p_ckv, 0, None, 0)], vecs=[kv_norm], cots=[(dckvn, 0)],
                                             tok_grads=[MXU_DTYPE], nb=nb, nl=lt, tl=tl, name="kv_norm_bwd")

    g_krdt = jnp.concatenate([dkr[..., QK_NOPE_DIM:QK_DIM], g_dt, jnp.zeros((nb, lt, pad_w - 2 * N_SSD_HEADS), F32)],
                             axis=-1)
    (dp_krdt,), (_, g_dt_bias_row), _, _ = _row_bwd(
        _krdt_fn, toks=[(p_krdt, 0, None, 0)], poss=[cos_k, sin_k], vecs=[rot_k, dt_bias_row], cots=[(g_krdt, 0)],
        tok_grads=[MXU_DTYPE], nb=nb, nl=lt, tl=tl, name="krdt_bwd")

    dp_segs = [t.reshape(nb * lt, -1) for t in (dp_cq, dp_ckv, dz, dp_xbc, dp_krdt)]
    g_segs = [_mm(h1f, t, ta=True, name="wg_in_" + nm) for nm, t in zip(("cq", "ckv", "z", "xbc", "krdt"), dp_segs)]
    g_segs[3] = _ungroup_xbc(g_segs[3])
    g_w_in_p = jnp.concatenate(g_segs, axis=1)
    dh1 = _mm_sum(list(zip(dp_segs, w_segs)), tb=True, name="dg_in").reshape(nb, lt, d)

    def pre_res_fn(uv, w, shift, scale):
        return _pre_fn(uv, w, shift, scale) + (uv,)

    (grad_x,), (g_mix_pre,), ((g_shift0, g_shift0c), (g_scale1, g_scale1c)), _ = _row_bwd(
        pre_res_fn, toks=[(u, 0, None, 0)], vecs=[mix_pre_norm], bvecs=[mods[0], mods[1]],
        cots=[(dh1, 0), (dx_res, -nctx)], tok_grads=[F32], nb=nb, nl=lt, tl=tl, nctx=nctx, drop_blocks=nctx,
        name="pre1_bwd")

    zero_row = jnp.zeros((1, 1, d), F32)
    lat = [g_shift0, g_scale1, g_gate2, g_shift3, g_scale4, g_gate5]
    ctxg = [g_shift0c, g_scale1c, zero_row, zero_row, zero_row, zero_row]
    dmod = jnp.concatenate([jnp.concatenate([a, b], axis=0)[:, 0, :] for a, b in zip(lat, ctxg)], axis=-1)
    dmod = jnp.pad(dmod, ((0, mod_rows - nb - 1), (0, 0)))
    _, (g_b_mod,), _, _ = _row_bwd(_bias_fn, toks=[(mod_lin[None], 0, None, 0)], vecs=[b_mod], cots=[(dmod[None], 0)],
                                   tok_grads=[None], nb=1, nl=mod_rows, tl=mod_rows, name="mod_bias_bwd")
    dmod_all = _allgather_small(dmod, name="gather_dmod")
    dmod_ctx = _sum_slots(dmod_all, out_dtype=F32, name="dmod_ctx_add")[nb:nb + 1]
    dmod_every = jnp.concatenate([dmod_all[:, :nb].reshape(n_ex, N_MOD * d), dmod_ctx,
                                  jnp.zeros((all_rows - n_ex - 1, N_MOD * d), F32)], axis=0)
    dmod_mine = lax.dynamic_slice_in_dim(dmod_every, chip * mod_cols, mod_cols, axis=1)
    g_w_mod = _mm(s_all[0], dmod_mine, ta=True, name="wg_mod")[None]
    ds_all = _mm(dmod_mine, w_mod_local, tb=True, name="dg_mod")
    (dc_all,), _, _, _ = _row_bwd(_silu_fn, toks=[(c_all, 0, None, 0)], cots=[(ds_all[None], 0)], tok_grads=[F32],
                                  nb=1, nl=all_rows, tl=all_rows, name="mod_silu_bwd")
    g_c_ctx = 0.5 * dc_all[0, n_ex]

    g_w_in = _unperm_in_cols(g_w_in_p[:, :IN_WIDTH])
    big_grads = {"w_in": g_w_in, "w_q_up": g_w_q_up, "w_kv_up": g_w_kv_up, "w_out": g_w_out,
                 "w_up": g_w_up, "w_down": g_w_down}
    parts = []
    for n, rows, cols, axis in BIG:
        lr, lc = big_local[n].shape
        gfull = big_grads[n]
        shards = gfull.reshape(N_CHIPS, lr, lc) if axis == 0 else jnp.transpose(gfull.reshape(lr, N_CHIPS, lc), (1, 0, 2))
        parts.append(shards.reshape(N_CHIPS, lr * lc // PACK_COLS, PACK_COLS))
    gpack = jnp.concatenate(parts, axis=1).astype(WIRE_DTYPE)
    half = shard_rows // 2
    core = lax.axis_index("c")
    gpack = gpack.reshape(N_CHIPS, 2, half, PACK_COLS)
    got = _swap_halves(gpack, name="grad_swap")
    own = lax.dynamic_index_in_dim(gpack, core, axis=1, keepdims=False)
    flat = (N_CHIPS * half, PACK_COLS)
    chip_sum = _sum_list([own.reshape(flat), got.reshape(flat)], out_dtype=WIRE_DTYPE,
                         name="grad_add_pair").reshape(N_CHIPS, half, PACK_COLS)
    from_chips = _scatter_to_chips(chip_sum, name="grad_scatter")
    mine_sum = lax.dynamic_index_in_dim(chip_sum, chip, axis=0, keepdims=True)
    from_chips = lax.dynamic_update_slice(from_chips, mine_sum, (chip, 0, 0))
    my_half = _sum_slots(from_chips, out_dtype=F32, name="grad_add_chips")
    joined = _join_halves(my_half, name="grad_join")
    g_shard = lax.dynamic_update_slice(joined, my_half[None], (core, 0, 0)).reshape(shard_rows, PACK_COLS)

    g_d = jnp.sum(g_d_chan.reshape(N_SSD_HEADS, SSD_HEAD_DIM), axis=1)[None]
    g_dt_bias = g_dt_bias_row[:, QK_ROPE_DIM:QK_ROPE_DIM + 2 * N_SSD_HEADS].reshape(1, 2, N_SSD_HEADS)
    small_names = ["c_ctx", "b_mod", "mix_pre_norm", "mix_post_norm", "q_norm", "kv_norm", "ssd_conv_w", "ssd_conv_b",
                   "ssd_a_log", "ssd_dt_bias", "ssd_d", "ssd_norm", "ffn_pre_norm", "ffn_post_norm", "ffn_conv_w",
                   "ffn_conv_b"]
    small_grads = [g_c_ctx, g_b_mod, g_mix_pre, g_mix_post, g_q_norm, g_kv_norm, g_ssd_conv_w, g_ssd_conv_b,
                   g_a_log, g_dt_bias, g_d, g_ssd_norm, g_ffn_pre, g_ffn_post, g_ffn_conv_w, g_ffn_conv_b]
    small_shapes = [tuple(np.shape(a)) for a in small_grads] + [()]
    n_small = sum(int(np.prod(shp)) for shp in small_shapes)
    small_rows = -(-n_small // (8 * LANES)) * 8
    small_all = _allgather_small(_pack_small(small_grads + [loss_part], small_rows), name="gather_small")
    small_sum = _sum_slots(small_all, out_dtype=F32, name="small_add")
    small_red = _unpack_small(small_sum, small_shapes)
    loss = small_red[-1]
    grads = dict(zip(small_names, small_red[:-1]))
    grads["ssd_conv_w"] = lax.dynamic_slice_in_dim(grads["ssd_conv_w"], chip * n_sc, n_sc, axis=1)[None]
    grads["ffn_conv_w"] = lax.dynamic_slice_in_dim(grads["ffn_conv_w"], chip * n_fc, n_fc, axis=1)[None]
    for n in small_names:
        grads[n] = grads[n].reshape(args[n].shape)

    delta, new_m, new_v = {}, {}, {}
    grads["w_mod"] = g_w_mod
    o = 0
    for n, _, _, _ in BIG:
        lr, lc = big_local[n].shape
        nr = lr * lc // PACK_COLS
        grads[n] = g_shard[o:o + nr].reshape(1, lr, lc)
        o += nr
    for n in ["w_mod"] + [n for n, _, _, _ in BIG]:
        dl, nm, nv = _adamw(args[n][0], grads[n][0], args["m_" + n][0], args["v_" + n][0], name="adamw_" + n)
        delta[n], new_m[n], new_v[n] = dl[None], nm[None], nv[None]
    sm_shapes = [args[n].shape for n in small_names]
    n_sm = sum(int(np.prod(shp)) for shp in sm_shapes)
    sm_rows = -(-n_sm // (8 * LANES)) * 8
    packs = [_pack_small([src[n] for n in small_names], sm_rows)
             for src in (args, grads, {n: args["m_" + n] for n in small_names}, {n: args["v_" + n] for n in small_names})]
    for out_dict, packed_out in zip((delta, new_m, new_v), _adamw(*packs, name="adamw_small")):
        out_dict.update(zip(small_names, _unpack_small(packed_out, sm_shapes)))

    return (loss, grad_x, *[grads[n] for n in names], *[delta[n] for n in names], *[new_m[n] for n in names],
            *[new_v[n] for n in names])
```

```python
import functools
import math

import numpy as np
import jax
import jax.numpy as jnp
from jax import lax
from jax.experimental import pallas as pl
from jax.experimental.pallas import tpu as pltpu

F32 = jnp.float32
MXU_DTYPE = jnp.bfloat16
WIRE_DTYPE = jnp.bfloat16
VMEM_LIMIT_BYTES = 56 * 1024 * 1024
HIGHEST = lax.Precision.HIGHEST

D_MODEL = 1024
N_MOD = 6
EPS = 1e-6
GRID_W = 64
N_ATTN_HEADS = 16
QK_NOPE_DIM = 64
QK_ROPE_DIM = 32
QK_DIM = QK_NOPE_DIM + QK_ROPE_DIM
V_HEAD_DIM = 64
Q_LORA_RANK = 384
KV_LORA_RANK = 256
ROPE_THETA = 10000.0
ATTN_SCALE = QK_DIM ** -0.5
ATTN_WIDTH = N_ATTN_HEADS * V_HEAD_DIM
N_SSD_HEADS = 16
SSD_HEAD_DIM = 64
SSD_GROUPS = 2
HEADS_PER_GROUP = N_SSD_HEADS // SSD_GROUPS
SSD_STATE = 128
SSD_CONV = 5
SSD_CHUNK = 128
D_INNER = N_SSD_HEADS * SSD_HEAD_DIM
GN = SSD_GROUPS * SSD_STATE
XBC_WIDTH = D_INNER + 2 * GN
D_FF = 2816
FFN_CONV = 3
KRDT_WIDTH = 128
IN_WIDTH = Q_LORA_RANK + KV_LORA_RANK + QK_ROPE_DIM + D_INNER + XBC_WIDTH + 2 * N_SSD_HEADS

ADAM_LR = 0.001
ADAM_B1 = 0.9
ADAM_B2 = 0.999
ADAM_EPS = 1e-08
ADAM_WD = 0.01
ADAM_STEP = 10

N_CHIPS = 4
N_DEV = 8
MESH = pl.DeviceIdType.MESH
LANES = 128

BIG = (("w_in", D_MODEL, IN_WIDTH, 1),
       ("w_q_up", Q_LORA_RANK, N_ATTN_HEADS * QK_DIM, 1),
       ("w_kv_up", KV_LORA_RANK, N_ATTN_HEADS * (QK_NOPE_DIM + V_HEAD_DIM), 1),
       ("w_out", ATTN_WIDTH + D_INNER, D_MODEL, 0), ("w_up", D_MODEL, 2 * D_FF, 1),
       ("w_down", D_FF, D_MODEL, 0))
PACK_COLS = 1024
GATHER_NOW, GATHER_LATE = ("w_in", "w_q_up", "w_kv_up"), ("w_out", "w_up", "w_down")
REDUCE_EARLY, REDUCE_LAST = ("w_up", "w_down"), ("w_in", "w_q_up", "w_kv_up", "w_out")


def _cparams(sem):
    return pltpu.CompilerParams(dimension_semantics=sem, vmem_limit_bytes=VMEM_LIMIT_BYTES)


def _pick(n, cands):
    for c in cands:
        if n % c == 0:
            return c
    return n


def _sigmoid(x):
    return 0.5 * (jnp.tanh(0.5 * x) + 1.0)


def _silu(x):
    return x * _sigmoid(x)


@jax.custom_vjp
def _softplus(x):
    u = jnp.exp(-jnp.abs(x))
    w = 1.0 + u
    log1p = jnp.where(w == 1.0, u, jnp.log(w) * (u / jnp.where(w == 1.0, 1.0, w - 1.0)))
    return jnp.maximum(x, 0.0) + log1p


def _softplus_fwd(x):
    return _softplus(x), x


def _softplus_bwd(x, g):
    return (g * _sigmoid(x),)


_softplus.defvjp(_softplus_fwd, _softplus_bwd)


@jax.custom_vjp
def _gelu(x):
    return 0.5 * x * (1.0 + lax.erf(x * (2.0 ** -0.5)))


def _gelu_fwd(x):
    return _gelu(x), x


def _gelu_bwd(x, g):
    cdf = 0.5 * (1.0 + lax.erf(x * (2.0 ** -0.5)))
    pdf = jnp.exp(-0.5 * x * x) * (1.0 / math.sqrt(2.0 * math.pi))
    return (g * (cdf + x * pdf),)


_gelu.defvjp(_gelu_fwd, _gelu_bwd)


def _rms(x, w):
    return x * lax.rsqrt(jnp.mean(x * x, axis=-1, keepdims=True) + EPS) * w


def _shift_rows_raw(x, off, seg):
    n = x.shape[0]
    if off == 0:
        return x
    r = pltpu.roll(x, (-off) % n, 0)
    idx = lax.broadcasted_iota(jnp.int32, x.shape, 0)
    src = idx + off
    ok = (src >= 0) & (src < n)
    if seg:
        ok = ok & ((idx < seg) == (src < seg))
    return jnp.where(ok, r, 0.0)


@functools.partial(jax.custom_vjp, nondiff_argnums=(1, 2))
def _shift_rows(x, off, seg):
    return _shift_rows_raw(x, off, seg)


def _shift_rows_fwd(x, off, seg):
    return _shift_rows_raw(x, off, seg), None


def _shift_rows_bwd(off, seg, _, g):
    return (_shift_rows_raw(g, -off, seg),)


_shift_rows.defvjp(_shift_rows_fwd, _shift_rows_bwd)


@functools.partial(jax.custom_vjp, nondiff_argnums=(1,))
def _roll_lanes(x, shift):
    return pltpu.roll(x, shift % x.shape[1], 1)


def _roll_lanes_fwd(x, shift):
    return _roll_lanes(x, shift), None


def _roll_lanes_bwd(shift, _, g):
    return (pltpu.roll(g, (-shift) % g.shape[1], 1),)


_roll_lanes.defvjp(_roll_lanes_fwd, _roll_lanes_bwd)


def _row_of(w, k):
    sel = lax.broadcasted_iota(jnp.int32, (w.shape[0], 1), 0) == k
    return jnp.sum(jnp.where(sel, w, 0.0), axis=0, keepdims=True)


def _col_of(w, k):
    sel = lax.broadcasted_iota(jnp.int32, (1, w.shape[1]), 1) == k
    return jnp.sum(jnp.where(sel, w, 0.0), axis=1, keepdims=True)


def _dwconv(x, w, seg):
    k = w.shape[0]
    acc = None
    for t in range(k):
        term = _shift_rows(x, t - k // 2, seg) * _row_of(w, t)
        acc = term if acc is None else acc + term
    return acc


def _dot(a, b, dims):
    return lax.dot_general(a.astype(MXU_DTYPE), b.astype(MXU_DTYPE), (dims, ((), ())),
                           preferred_element_type=F32)


def _dot_exact(a, b):
    return lax.dot_general(a, b, (((1,), (0,)), ((), ())), precision=HIGHEST,
                           preferred_element_type=F32)


def _mm(a, b, *, ta=False, tb=False, out_dtype=F32, name):
    if ta:
        kdim, m = a.shape
    else:
        m, kdim = a.shape
    if tb:
        n, k2 = b.shape
    else:
        k2, n = b.shape
    assert kdim == k2, (a.shape, b.shape, ta, tb)
    tm = _pick(m, (1024, 1408, 512, 384, 256, 128))
    tn = _pick(n, (512, 1408, 384, 256, 128))
    tk = kdim if kdim <= 2048 else _pick(kdim, (2048, 1664, 1536, 1408, 1024, 512, 256, 128))
    nk = kdim // tk
    a_spec = pl.BlockSpec((tk, tm), lambda i, j, k: (k, i)) if ta else pl.BlockSpec((tm, tk), lambda i, j, k: (i, k))
    b_spec = pl.BlockSpec((tn, tk), lambda i, j, k: (j, k)) if tb else pl.BlockSpec((tk, tn), lambda i, j, k: (k, j))
    dims = ((0,) if ta else (1,), (1,) if tb else (0,))

    def body(a_ref, b_ref, o_ref, *scratch):
        if nk == 1:
            o_ref[...] = _dot(a_ref[...], b_ref[...], dims).astype(o_ref.dtype)
            return
        acc_ref, = scratch
        k = pl.program_id(2)

        @pl.when(k == 0)
        def _():
            acc_ref[...] = jnp.zeros_like(acc_ref)

        acc_ref[...] += _dot(a_ref[...], b_ref[...], dims)

        @pl.when(k == nk - 1)
        def _():
            o_ref[...] = acc_ref[...].astype(o_ref.dtype)

    return pl.pallas_call(
        body, name=name, grid=(m // tm, n // tn, nk),
        in_specs=[a_spec, b_spec], out_specs=pl.BlockSpec((tm, tn), lambda i, j, k: (i, j)),
        out_shape=jax.ShapeDtypeStruct((m, n), out_dtype),
        scratch_shapes=[pltpu.VMEM((tm, tn), F32)] if nk > 1 else [],
        compiler_params=_cparams(("parallel", "parallel", "arbitrary")),
    )(a, b)


def _mm_sum(pairs, *, tb=False, out_dtype=F32, name):
    m = pairs[0][0].shape[0]
    n = pairs[0][1].shape[0] if tb else pairs[0][1].shape[1]
    tm = _pick(m, (1024, 1408, 512, 384, 256, 128))
    tn = _pick(n, (512, 1408, 384, 256, 128))
    specs, args = [], []
    for a, b in pairs:
        kdim = a.shape[1]
        specs.append(pl.BlockSpec((tm, kdim), lambda i, j: (i, 0)))
        specs.append(pl.BlockSpec((tn, kdim), lambda i, j: (j, 0)) if tb else pl.BlockSpec((kdim, tn), lambda i, j: (0, j)))
        args += [a, b]
    dims = ((1,), (1,) if tb else (0,))

    def body(*refs):
        acc = None
        for t in range(len(pairs)):
            term = _dot(refs[2 * t][...], refs[2 * t + 1][...], dims)
            acc = term if acc is None else acc + term
        refs[-1][...] = acc.astype(refs[-1].dtype)

    return pl.pallas_call(
        body, name=name, grid=(m // tm, n // tn), in_specs=specs,
        out_specs=pl.BlockSpec((tm, tn), lambda i, j: (i, j)), out_shape=jax.ShapeDtypeStruct((m, n), out_dtype),
        compiler_params=_cparams(("parallel", "parallel")),
    )(*args)


def _row_specs(toks, poss, vecs, bvecs, tl, nctx, nb):
    specs, args = [], []
    for arr, off, cw, ci in toks:
        cw = arr.shape[2] if cw is None else cw
        specs.append(pl.BlockSpec((1, tl, cw), lambda b, l, off=off, ci=ci: (b, l + off, ci)))
        args.append(arr)
    for arr in poss:
        specs.append(pl.BlockSpec((1, tl, arr.shape[2]), lambda b, l: (0, l, 0)))
        args.append(arr)
    for arr in vecs:
        specs.append(pl.BlockSpec(arr.shape, lambda b, l: (0, 0)))
        args.append(arr)
    for arr in bvecs:
        if nctx:
            specs.append(pl.BlockSpec((1, 1, arr.shape[2]), lambda b, l: (jnp.where(l < nctx, nb, b), 0, 0)))
        else:
            specs.append(pl.BlockSpec((1, 1, arr.shape[2]), lambda b, l: (b, 0, 0)))
        args.append(arr)
    return specs, args


def _row_fwd(fn, *, toks, poss=(), vecs=(), bvecs=(), outs, nb, nl, tl, nctx=0, name):
    nt, npos, nv, nbv = len(toks), len(poss), len(vecs), len(bvecs)
    specs, args = _row_specs(toks, poss, vecs, bvecs, tl, nctx, nb)

    def body(*refs):
        ins, os = refs[:len(specs)], refs[len(specs):]
        tv = [r[0].astype(F32) for r in ins[:nt]]
        pv = [r[0] for r in ins[nt:nt + npos]]
        vv = [r[...] for r in ins[nt + npos:nt + npos + nv]]
        bv = [r[0] for r in ins[nt + npos + nv:]]
        res = fn(*tv, *pv, *vv, *bv)
        for o, r in zip(os, res):
            o[0] = r.astype(o.dtype)

    return pl.pallas_call(
        body, name=name, grid=(nb, nl // tl), in_specs=specs,
        out_specs=[pl.BlockSpec((1, tl, c), lambda b, l: (b, l, 0)) for c, _ in outs],
        out_shape=[jax.ShapeDtypeStruct((nb, nl, c), dt) for c, dt in outs],
        compiler_params=_cparams(("parallel", "parallel")),
    )(*args)


def _row_bwd(fn, *, toks, poss=(), vecs=(), bvecs=(), cots, tok_grads, emit=(), nb, nl, tl, nctx=0, name,
             drop_blocks=0):
    nt, npos, nv, nbv = len(toks), len(poss), len(vecs), len(bvecs)
    specs, args = _row_specs(toks, poss, vecs, bvecs, tl, nctx, nb)
    n_in = len(specs)
    cot_slots = []
    for arr, off in cots:
        if arr is None:
            cot_slots.append(None)
            continue
        cot_slots.append((len(specs), off))
        specs.append(pl.BlockSpec((1, tl, arr.shape[2]), lambda b, l, off=off: (b, jnp.maximum(l + off, 0), 0)))
        args.append(arr)
    n_all_in = len(specs)

    out_specs, out_shapes = [], []
    tok_out = []
    for (arr, off, cw, ci), dt in zip(toks, tok_grads):
        if dt is None:
            tok_out.append(None)
            continue
        cw = arr.shape[2] if cw is None else cw
        tok_out.append(len(out_specs))
        out_specs.append(pl.BlockSpec((1, tl, cw), lambda b, l: (b, jnp.maximum(l - drop_blocks, 0), 0)))
        out_shapes.append(jax.ShapeDtypeStruct((nb, nl - drop_blocks * tl, cw), dt))
    vec_out = []
    for arr in vecs:
        vec_out.append(len(out_specs))
        out_specs.append(pl.BlockSpec(arr.shape, lambda b, l: (0, 0)))
        out_shapes.append(jax.ShapeDtypeStruct(arr.shape, F32))
    bv_out = []
    for arr in bvecs:
        c = arr.shape[2]
        lat = len(out_specs)
        out_specs.append(pl.BlockSpec((1, 1, c), lambda b, l: (b, 0, 0)))
        out_shapes.append(jax.ShapeDtypeStruct((nb, 1, c), F32))
        ctx = None
        if nctx:
            ctx = len(out_specs)
            out_specs.append(pl.BlockSpec((1, 1, c), lambda b, l: (0, 0, 0)))
            out_shapes.append(jax.ShapeDtypeStruct((1, 1, c), F32))
        bv_out.append((lat, ctx))
    emit_out = []
    emit_cols = {}
    for idx, c, dt in emit:
        emit_out.append((idx, len(out_specs)))
        out_specs.append(pl.BlockSpec((1, tl, c), lambda b, l: (b, l, 0)))
        out_shapes.append(jax.ShapeDtypeStruct((nb, nl, c), dt))

    def body(*refs):
        ins, os = refs[:n_all_in], refs[n_all_in:]
        b, l = pl.program_id(0), pl.program_id(1)
        tv = [r[0].astype(F32) for r in ins[:nt]]
        pv = [r[0] for r in ins[nt:nt + npos]]
        vv = [r[...] for r in ins[nt + npos:nt + npos + nv]]
        bv = [r[0] for r in ins[nt + npos + nv:n_in]]

        def f(*d):
            return tuple(fn(*d[:nt], *pv, *d[nt:]))

        res, vjp = jax.vjp(f, *tv, *vv, *bv)
        cts = []
        for r, slot in zip(res, cot_slots):
            if slot is None:
                cts.append(jnp.zeros_like(r))
            else:
                i, off = slot
                ct = ins[i][0].astype(F32)
                if off < 0:
                    ct = jnp.where(l + off >= 0, ct, 0.0)
                cts.append(ct)
        grads = vjp(tuple(cts))

        for g, slot in zip(grads[:nt], tok_out):
            if slot is not None:
                os[slot][0] = g.astype(os[slot].dtype)

        @pl.when((b == 0) & (l == 0))
        def _():
            for slot in vec_out:
                os[slot][...] = jnp.zeros_like(os[slot])
            for _, ctx in bv_out:
                if ctx is not None:
                    os[ctx][...] = jnp.zeros_like(os[ctx])

        @pl.when(l == 0)
        def _():
            for lat, _ in bv_out:
                os[lat][...] = jnp.zeros_like(os[lat])

        for g, slot in zip(grads[nt:nt + nv], vec_out):
            os[slot][...] += g
        for g, (lat, ctx) in zip(grads[nt + nv:], bv_out):
            if ctx is None:
                os[lat][0] += g
            else:
                is_ctx = l < nctx
                os[lat][0] += jnp.where(is_ctx, 0.0, g)
                os[ctx][0] += jnp.where(is_ctx, g, 0.0)
        for idx, slot in emit_out:
            os[slot][0] = res[idx].astype(os[slot].dtype)

    out = pl.pallas_call(
        body, name=name, grid=(nb, nl // tl), in_specs=specs, out_specs=out_specs, out_shape=out_shapes,
        compiler_params=_cparams(("arbitrary", "arbitrary")),
    )(*args)
    tg = [None if s is None else out[s] for s in tok_out]
    vg = [out[s] for s in vec_out]
    bg = [(out[lat], None if ctx is None else out[ctx]) for lat, ctx in bv_out]
    em = [out[s] for _, s in emit_out]
    return tg, vg, bg, em


def _seq_specs(toks, vecs, nl, cb):
    specs, args = [], []
    for arr, off, mult in toks:
        specs.append(pl.BlockSpec((1, nl, cb * mult), lambda j, b, off=off: (b, 0, j + off)))
        args.append(arr)
    for arr, off in vecs:
        specs.append(pl.BlockSpec((arr.shape[0], cb), lambda j, b, off=off: (0, j + off)))
        args.append(arr)
    return specs, args


def _seq_fwd(fn, *, toks, vecs, outs, nb, nl, nc, cb, name):
    nt = len(toks)
    specs, args = _seq_specs(toks, vecs, nl, cb)

    def body(*refs):
        ins, os = refs[:len(specs)], refs[len(specs):]
        tv = [r[0].astype(F32) for r in ins[:nt]]
        vv = [r[...] for r in ins[nt:]]
        for o, r in zip(os, fn(*tv, *vv)):
            o[0] = r.astype(o.dtype)

    return pl.pallas_call(
        body, name=name, grid=(nc // cb, nb), in_specs=specs,
        out_specs=[pl.BlockSpec((1, nl, cb), lambda j, b: (b, 0, j)) for _ in outs],
        out_shape=[jax.ShapeDtypeStruct((nb, nl, nc), dt) for dt in outs],
        compiler_params=_cparams(("parallel", "parallel")),
    )(*args)


def _seq_bwd(fn, *, toks, vecs, cots, tok_grads, nb, nl, nc, cb, name):
    nt, nv = len(toks), len(vecs)
    specs, args = _seq_specs(toks, vecs, nl, cb)
    n_in = len(specs)
    cot_counts = [len(group) for group in cots]
    for group in cots:
        for arr in group:
            specs.append(pl.BlockSpec((1, nl, cb), lambda j, b: (b, 0, j)))
            args.append(arr)
    out_specs, out_shapes = [], []
    for (_, _, mult), dt in zip(toks, tok_grads):
        out_specs.append(pl.BlockSpec((1, nl, cb * mult), lambda j, b: (b, 0, j)))
        out_shapes.append(jax.ShapeDtypeStruct((nb, nl, nc * mult), dt))
    for arr, _ in vecs:
        out_specs.append(pl.BlockSpec((arr.shape[0], cb), lambda j, b: (0, j)))
        out_shapes.append(jax.ShapeDtypeStruct((arr.shape[0], nc), F32))

    def body(*refs):
        ins, os = refs[:len(specs)], refs[len(specs):]
        b = pl.program_id(1)
        tv = [r[0].astype(F32) for r in ins[:nt]]
        vv = [r[...] for r in ins[nt:n_in]]
        _, vjp = jax.vjp(lambda *d: tuple(fn(*d)), *tv, *vv)
        cts, o = [], n_in
        for cnt in cot_counts:
            ct = ins[o][0].astype(F32)
            for r in ins[o + 1:o + cnt]:
                ct = ct + r[0].astype(F32)
            cts.append(ct)
            o += cnt
        grads = vjp(tuple(cts))
        for g, o in zip(grads[:nt], os[:nt]):
            o[0] = g.astype(o.dtype)

        @pl.when(b == 0)
        def _():
            for o in os[nt:]:
                o[...] = jnp.zeros_like(o)

        for g, o in zip(grads[nt:], os[nt:]):
            o[...] += g

    out = pl.pallas_call(
        body, name=name, grid=(nc // cb, nb), in_specs=specs, out_specs=out_specs, out_shape=out_shapes,
        compiler_params=_cparams(("parallel", "arbitrary")),
    )(*args)
    return out[:nt], out[nt:]


EXP2_SCALE = ATTN_SCALE * math.log2(math.e)


HEAD_TILE = 128
N_HEAD_PAIRS = N_ATTN_HEADS // 2


def _head_lanes():
    lane = lax.broadcasted_iota(jnp.int32, (1, HEAD_TILE), 1)
    return lane < QK_NOPE_DIM, (lane >= QK_NOPE_DIM) & (lane < QK_DIM)


def _attn_specs(tq, lk):
    q = pl.BlockSpec((1, tq, 2 * HEAD_TILE), lambda b, pr, j: (b, j, pr))
    kv = pl.BlockSpec((1, lk, 2 * HEAD_TILE), lambda b, pr, j: (b, 0, pr))
    kr = pl.BlockSpec((1, lk, HEAD_TILE), lambda b, pr, j: (b, 0, 0))
    o = pl.BlockSpec((1, tq, HEAD_TILE), lambda b, pr, j: (b, j, pr))
    lse = pl.BlockSpec((1, 2, tq, 1), lambda b, pr, j: (b, pr, j, 0))
    return q, kv, kr, o, lse


def _grid_marks(nb, nj):
    b, pr, j = pl.program_id(0), pl.program_id(1), pl.program_id(2)
    first = (b == 0) & (pr == 0) & (j == 0)
    middle = (b == nb // 2) & (pr == 0) & (j == 0)
    last = (b == nb - 1) & (pr == N_HEAD_PAIRS - 1) & (j == nj - 1)
    return first, middle, last


def _attn_fwd(q, kv, kr, late_shard, *, tq, name):
    nb, s, _ = q.shape
    lk = kv.shape[1]
    nj = s // tq
    qs, kvs, krs, os_, lses = _attn_specs(tq, lk)

    def body(q_ref, kv_ref, kr_ref, shard_ref, o_ref, lse_ref, gathered_ref, send_sems, recv_sems):
        start, relay, finish = _gather_stage(shard_ref, gathered_ref, send_sems, recv_sems)
        first, middle, last = _grid_marks(nb, nj)
        pl.when(first)(start)
        pl.when(middle)(relay)
        low, _ = _head_lanes()
        outs = []
        for e in range(2):
            tile = pl.ds(HEAD_TILE * e, HEAD_TILE)
            kv_e = kv_ref[0, :, tile]
            keys = jnp.where(low, kv_e, kr_ref[0])
            sc = _dot(q_ref[0, :, tile], keys, ((1,), (1,)))
            m = jnp.max(sc, axis=-1, keepdims=True)
            p = jnp.exp2((sc - m) * EXP2_SCALE)
            denom = jnp.sum(p, axis=-1, keepdims=True)
            outs.append(_dot(p, kv_e, ((1,), (0,))) / denom)
            lse_ref[0, e] = m * EXP2_SCALE + jnp.log2(denom)
        o_ref[0] = jnp.where(low, pltpu.roll(outs[0], V_HEAD_DIM, 1), outs[1])
        pl.when(last)(finish)

    return pl.pallas_call(
        body, name=name, grid=(nb, N_HEAD_PAIRS, nj), in_specs=[qs, kvs, krs, ANY], out_specs=[os_, lses, ANY],
        out_shape=[jax.ShapeDtypeStruct((nb, s, ATTN_WIDTH), F32), jax.ShapeDtypeStruct((nb, N_ATTN_HEADS, s, 1), F32),
                   jax.ShapeDtypeStruct((N_CHIPS,) + late_shard.shape, late_shard.dtype)],
        scratch_shapes=GATHER_SEMS,
        compiler_params=_cparams(("arbitrary", "arbitrary", "arbitrary")),
    )(q, kv, kr, late_shard)


def _attn_bwd(q, kv, kr, o, lse, do, early_sums, *, tq, name):
    nb, s, _ = q.shape
    lk = kv.shape[1]
    nj = s // tq
    qs, kvs, krs, os_, lses = _attn_specs(tq, lk)

    def body(q_ref, kv_ref, kr_ref, o_ref, lse_ref, do_ref, sums_ref, dq_ref, dkv_ref, dkr_ref, scattered_ref,
             send_sems, recv_sems):
        start, finish = _scatter_stage(sums_ref, scattered_ref, send_sems, recv_sems)
        first, _, last = _grid_marks(nb, nj)
        pl.when(first)(start)
        pr, j = pl.program_id(1), pl.program_id(2)
        low, rope = _head_lanes()
        do_pair = do_ref[0]
        prod = do_pair * o_ref[0]

        @pl.when(j == 0)
        def _():
            dkv_ref[...] = jnp.zeros_like(dkv_ref)

        @pl.when((pr == 0) & (j == 0))
        def _():
            dkr_ref[...] = jnp.zeros_like(dkr_ref)

        dkr = None
        for e in range(2):
            tile = pl.ds(HEAD_TILE * e, HEAD_TILE)
            delta = jnp.sum(jnp.where(low if e == 0 else ~low, prod, 0.0), axis=-1, keepdims=True)
            do_e = jnp.where(low, 0.0, do_pair if e == 1 else pltpu.roll(do_pair, V_HEAD_DIM, 1))
            kv_e, q_e = kv_ref[0, :, tile], q_ref[0, :, tile]
            keys = jnp.where(low, kv_e, kr_ref[0])
            sc = _dot(q_e, keys, ((1,), (1,)))
            p = jnp.exp2(sc * EXP2_SCALE - lse_ref[0, e])
            dp = _dot(do_e, kv_e, ((1,), (1,)))
            ds = (p * (dp - delta)).astype(MXU_DTYPE)
            dq_ref[0, :, tile] = _dot(ds, keys, ((1,), (0,))) * ATTN_SCALE
            dkeys = _dot(ds, q_e, ((0,), (0,)))
            dv = _dot(p, do_e, ((0,), (0,)))
            dkv_ref[0, :, tile] += jnp.where(low, dkeys, dv)
            part = jnp.where(rope, dkeys, 0.0)
            dkr = part if dkr is None else dkr + part
        dkr_ref[0] += dkr

        @pl.when(j == nj - 1)
        def _():
            for e in range(2):
                tile = pl.ds(HEAD_TILE * e, HEAD_TILE)
                dkv_ref[0, :, tile] = dkv_ref[0, :, tile] * jnp.where(low, ATTN_SCALE, 1.0)

        @pl.when((pr == N_HEAD_PAIRS - 1) & (j == nj - 1))
        def _():
            dkr_ref[0] = dkr_ref[0] * ATTN_SCALE

        pl.when(last)(finish)

    return pl.pallas_call(
        body, name=name, grid=(nb, N_HEAD_PAIRS, nj), in_specs=[qs, kvs, krs, os_, lses, os_, ANY],
        out_specs=[qs, kvs, krs, ANY],
        out_shape=[jax.ShapeDtypeStruct(q.shape, F32), jax.ShapeDtypeStruct(kv.shape, F32),
                   jax.ShapeDtypeStruct(kr.shape, F32), jax.ShapeDtypeStruct(early_sums.shape, early_sums.dtype)],
        scratch_shapes=SCATTER_SEMS,
        compiler_params=_cparams(("arbitrary", "arbitrary", "arbitrary")),
    )(q, kv, kr, o, lse, do, early_sums)


N_PAIRS = HEADS_PER_GROUP // 2
PAIR_W = 2 * SSD_HEAD_DIM


def _ssd_chunk(states, xs, dtc, dtr, bm, cm, ac, ar, *, reverse):
    q = dtc.shape[0]
    row = lax.broadcasted_iota(jnp.int32, (q, q), 0)
    col = lax.broadcasted_iota(jnp.int32, (q, q), 1)
    if reverse:
        tri_c, tri_r, mask = col < row, row < col, col >= row
    else:
        tri_c, tri_r, mask = col <= row, row <= col, col <= row
    a_col, a_row = dtc * ac, dtr * ar
    cum_c = _dot_exact(tri_c.astype(F32), a_col)
    cum_r = _dot_exact(a_row, tri_r.astype(F32))
    tot = jnp.sum(a_col, axis=0, keepdims=True)
    cb = _dot(cm, bm, ((1,), (1,)))
    first = lax.broadcasted_iota(jnp.int32, (1, PAIR_W), 1) < SSD_HEAD_DIM
    first_rows = lax.broadcasted_iota(jnp.int32, (PAIR_W, 1), 0) < SSD_HEAD_DIM
    heads = range(HEADS_PER_GROUP)
    cc = [_col_of(cum_c, e) for e in heads]
    cr = [_row_of(cum_r, e) for e in heads]
    dc = [_col_of(dtc, e) for e in heads]
    te = [_col_of(tot, e) for e in heads]
    if reverse:
        within = [jnp.exp(jnp.where(mask, cr[e] - cc[e], -jnp.inf)) for e in heads]
        into, to_end = [jnp.exp(te[e] - cc[e]) for e in heads], [jnp.exp(cc[e]) for e in heads]
    else:
        within = [jnp.exp(jnp.where(mask, cc[e] - cr[e], -jnp.inf)) for e in heads]
        into, to_end = [jnp.exp(cc[e]) for e in heads], [jnp.exp(te[e] - cc[e]) for e in heads]
    decay = [cb * within[e] for e in heads]
    carry = [jnp.exp(te[e]) for e in heads]
    pairs = range(N_PAIRS)

    def both(vals, pr):
        return jnp.where(first, vals[2 * pr], vals[2 * pr + 1])

    xd = [xs[pr] * both(dc, pr) for pr in pairs]
    y_even = [_dot(decay[2 * pr], jnp.where(first, xd[pr], 0.0), ((1,), (0,))) for pr in pairs]
    y_odd = [_dot(decay[2 * pr + 1], jnp.where(first, 0.0, xd[pr]), ((1,), (0,))) for pr in pairs]
    y_off = [_dot(cm, states[pr], ((1,), (1,))) for pr in pairs]
    grow = [_dot(xd[pr] * both(to_end, pr), bm, ((0,), (0,))) for pr in pairs]
    ys = [y_even[pr] + y_odd[pr] + y_off[pr] * both(into, pr) for pr in pairs]
    new_states = [states[pr] * jnp.where(first_rows, carry[2 * pr], carry[2 * pr + 1]) + grow[pr] for pr in pairs]
    return tuple(ys) + tuple(new_states)


def _chunk_of_step(t, ncc, nch, reverse):
    if not reverse:
        return t
    return jnp.where(t < ncc, ncc - 1 - t, nch - 1 - (t - ncc))


X_COLS = D_INNER // SSD_GROUPS
GROUP_COLS = X_COLS + 2 * SSD_STATE


def _scan_in_specs(nch, ncc, reverse, back):
    q, e = SSD_CHUNK, HEADS_PER_GROUP

    def ch(t):
        return _chunk_of_step((nch - 1 - t) if back else t, ncc, nch, reverse)

    return ch, [
        pl.BlockSpec((1, q, GROUP_COLS), lambda b, g, t: (b, ch(t), g)),
        pl.BlockSpec((1, 1, q, e), lambda b, g, t: (b, g, ch(t), 0)),
        pl.BlockSpec((1, 1, e, q), lambda b, g, t: (b, g, 0, ch(t))),
        pl.BlockSpec((1, 1, e), lambda b, g, t: (g, 0, 0)),
        pl.BlockSpec((1, e, 1), lambda b, g, t: (g, 0, 0)),
        pl.BlockSpec((1, 1, X_COLS), lambda b, g, t: (g, 0, 0)),
    ]


def _scan_chunk_fn(reverse, skip):
    def f(states, xs, dtc, dtr, bm, cm, ac, ar, d):
        res = _ssd_chunk(states, xs, dtc, dtr, bm, cm, ac, ar, reverse=reverse)
        if not skip:
            return res
        ys = tuple(res[i] + d[:, PAIR_W * i:PAIR_W * (i + 1)] * xs[i] for i in range(N_PAIRS))
        return ys + tuple(res[N_PAIRS:])

    return f


def _scan_operands(x_ref, dtc_ref, dtr_ref, ac_ref, ar_ref, d_ref):
    xs = [x_ref[0, :, pl.ds(PAIR_W * i, PAIR_W)] for i in range(N_PAIRS)]
    bm = x_ref[0, :, pl.ds(X_COLS, SSD_STATE)]
    cm = x_ref[0, :, pl.ds(X_COLS + SSD_STATE, SSD_STATE)]
    return xs, dtc_ref[0, 0], dtr_ref[0, 0], bm, cm, ac_ref[0], ar_ref[0], d_ref[0]


N_IN = 6


def _scan_fwd(dirs, *, ncc, name):
    nb, lt, _ = dirs[0][0].shape
    q, n = SSD_CHUNK, SSD_STATE
    nch = lt // q
    in_specs, out_specs, out_shapes, fs = [], [], [], []
    for dr in range(2):
        ch, specs = _scan_in_specs(nch, ncc, bool(dr), False)
        in_specs += specs
        fs.append(_scan_chunk_fn(bool(dr), dr == 0))
        out_specs += [pl.BlockSpec((1, q, X_COLS), lambda b, g, t, ch=ch: (b, ch(t), g)),
                      pl.BlockSpec((1, 1, 1, N_PAIRS, PAIR_W, n), lambda b, g, t: (b, g, t, 0, 0, 0))]
        out_shapes += [jax.ShapeDtypeStruct((nb, lt, D_INNER), F32),
                       jax.ShapeDtypeStruct((nb, SSD_GROUPS, nch, N_PAIRS, PAIR_W, n), F32)]

    def body(*refs):
        ins, outs, sts = refs[:2 * N_IN], refs[2 * N_IN:2 * N_IN + 4], refs[2 * N_IN + 4:]
        t = pl.program_id(2)

        @pl.when(t == 0)
        def _():
            for st_ref in sts:
                st_ref[...] = jnp.zeros_like(st_ref)

        entering = [[sts[dr][i] for i in range(N_PAIRS)] for dr in range(2)]
        results = [fs[dr](entering[dr], *_scan_operands(*ins[N_IN * dr:N_IN * (dr + 1)])) for dr in range(2)]
        for dr in range(2):
            (y_ref, ent_ref), st_ref = outs[2 * dr:2 * dr + 2], sts[dr]
            for i in range(N_PAIRS):
                ent_ref[0, 0, 0, i] = entering[dr][i]
                y_ref[0, :, pl.ds(PAIR_W * i, PAIR_W)] = results[dr][i]
                st_ref[i] = results[dr][N_PAIRS + i]

    out = pl.pallas_call(
        body, name=name, grid=(nb, SSD_GROUPS, nch), in_specs=in_specs, out_specs=out_specs, out_shape=out_shapes,
        scratch_shapes=[pltpu.VMEM((N_PAIRS, PAIR_W, n), F32)] * 2,
        compiler_params=_cparams(("parallel", "parallel", "arbitrary")),
    )(*dirs[0], *dirs[1])
    return out[:2], out[2:]


N_SCAN_GRADS = 6


def _scan_bwd(dirs, entering, dy, *, ncc, name):
    nb, lt, _ = dirs[0][0].shape
    q, n, e = SSD_CHUNK, SSD_STATE, HEADS_PER_GROUP
    nch = lt // q
    in_specs, out_specs, out_shapes, fs, args = [], [], [], [], []
    for dr in range(2):
        ch, specs = _scan_in_specs(nch, ncc, bool(dr), True)
        in_specs += specs + [
            pl.BlockSpec((1, 1, 1, N_PAIRS, PAIR_W, n), lambda b, g, t: (b, g, nch - 1 - t, 0, 0, 0)),
            pl.BlockSpec((1, q, X_COLS), lambda b, g, t, ch=ch: (b, ch(t), g))]
        args += list(dirs[dr]) + [entering[dr], dy]
        fs.append(_scan_chunk_fn(bool(dr), dr == 0))
        out_specs += [pl.BlockSpec((1, q, GROUP_COLS), lambda b, g, t, ch=ch: (b, ch(t), g)),
                      pl.BlockSpec((1, 1, q, e), lambda b, g, t, ch=ch: (b, g, ch(t), 0)),
                      pl.BlockSpec((1, 1, e, q), lambda b, g, t, ch=ch: (b, g, 0, ch(t))),
                      pl.BlockSpec((1, 1, 1, e), lambda b, g, t: (b, g, 0, 0)),
                      pl.BlockSpec((1, 1, e, 1), lambda b, g, t: (b, g, 0, 0)),
                      pl.BlockSpec((1, 1, 1, X_COLS), lambda b, g, t: (b, g, 0, 0))]
        out_shapes += [jax.ShapeDtypeStruct((nb, lt, SSD_GROUPS * GROUP_COLS), F32),
                       jax.ShapeDtypeStruct((nb, SSD_GROUPS, lt, e), F32), jax.ShapeDtypeStruct((nb, SSD_GROUPS, e, lt), F32),
                       jax.ShapeDtypeStruct((nb, SSD_GROUPS, 1, e), F32), jax.ShapeDtypeStruct((nb, SSD_GROUPS, e, 1), F32),
                       jax.ShapeDtypeStruct((nb, SSD_GROUPS, 1, X_COLS), F32)]
    n_in = N_IN + 2

    def body(*refs):
        ins = refs[:2 * n_in]
        outs = refs[2 * n_in:2 * n_in + 2 * N_SCAN_GRADS]
        dss = refs[2 * n_in + 2 * N_SCAN_GRADS:]
        t = pl.program_id(2)

        for dr in range(2):
            mine = ins[n_in * dr:n_in * (dr + 1)]
            ent_ref, dy_ref = mine[N_IN], mine[N_IN + 1]
            dx_ref, ddtc_ref, ddtr_ref, dac_ref, dar_ref, dd_ref = outs[N_SCAN_GRADS * dr:N_SCAN_GRADS * (dr + 1)]
            ds_ref = dss[dr]

            @pl.when(t == 0)
            def _():
                for ref in (ds_ref, dac_ref, dar_ref, dd_ref):
                    ref[...] = jnp.zeros_like(ref)

            states = [ent_ref[0, 0, 0, i] for i in range(N_PAIRS)]
            _, vjp = jax.vjp(fs[dr], states, *_scan_operands(*mine[:N_IN]))
            dys = [dy_ref[0, :, pl.ds(PAIR_W * i, PAIR_W)] for i in range(N_PAIRS)]
            gs, gx, gdtc, gdtr, gb, gc, gac, gar, gd = vjp(tuple(dys) + tuple(ds_ref[i] for i in range(N_PAIRS)))
            o = 0
            for part in list(gx) + [gb, gc]:
                dx_ref[0, :, pl.ds(o, part.shape[1])] = part
                o += part.shape[1]
            for i in range(N_PAIRS):
                ds_ref[i] = gs[i]
            ddtc_ref[0, 0] = gdtc
            ddtr_ref[0, 0] = gdtr
            dac_ref[0, 0] += gac
            dar_ref[0, 0] += gar
            dd_ref[0, 0] += gd

    out = pl.pallas_call(
        body, name=name, grid=(nb, SSD_GROUPS, nch), in_specs=in_specs, out_specs=out_specs, out_shape=out_shapes,
        scratch_shapes=[pltpu.VMEM((N_PAIRS, PAIR_W, n), F32)] * 2,
        compiler_params=_cparams(("parallel", "parallel", "arbitrary")),
    )(*args)
    return out[:N_SCAN_GRADS], out[N_SCAN_GRADS:]


def _adamw(w, g, m, v, *, name):
    r, c = w.shape
    tr = _pick(r, (256, 176, 128, 96, 64, 8))
    c1 = 1.0 / (1.0 - ADAM_B1 ** ADAM_STEP)
    c2 = 1.0 / (1.0 - ADAM_B2 ** ADAM_STEP)

    def body(w_ref, g_ref, m_ref, v_ref, d_ref, nm_ref, nv_ref):
        gv = g_ref[...]
        nm = ADAM_B1 * m_ref[...] + (1.0 - ADAM_B1) * gv
        nv = ADAM_B2 * v_ref[...] + (1.0 - ADAM_B2) * (gv * gv)
        d_ref[...] = -ADAM_LR * ((nm * c1) / (jnp.sqrt(nv * c2) + ADAM_EPS) + ADAM_WD * w_ref[...])
        nm_ref[...] = nm
        nv_ref[...] = nv

    spec = pl.BlockSpec((tr, c), lambda i: (i, 0))
    return pl.pallas_call(
        body, name=name, grid=(r // tr,), in_specs=[spec] * 4, out_specs=[spec] * 3,
        out_shape=[jax.ShapeDtypeStruct((r, c), F32)] * 3, compiler_params=_cparams(("parallel",)),
    )(w, g, m, v)


def _sum_rows_tile(r):
    return r if r <= 1024 else _pick(r, (656, 512, 256, 128, 64, 32, 16))


def _sum_slots(x, *, out_dtype, name):
    n, r, c = x.shape
    tr = _sum_rows_tile(r)

    def body(x_ref, o_ref):
        acc = x_ref[0].astype(F32)
        for k in range(1, n):
            acc = acc + x_ref[k].astype(F32)
        o_ref[...] = acc.astype(o_ref.dtype)

    return pl.pallas_call(
        body, name=name, grid=(r // tr,), in_specs=[pl.BlockSpec((n, tr, c), lambda i: (0, i, 0))],
        out_specs=pl.BlockSpec((tr, c), lambda i: (i, 0)), out_shape=jax.ShapeDtypeStruct((r, c), out_dtype),
        compiler_params=_cparams(("parallel",)),
    )(x)


def _sum_list(xs, *, out_dtype, name):
    r, c = xs[0].shape
    tr = _sum_rows_tile(r)

    def body(*refs):
        acc = refs[0][...].astype(F32)
        for ref in refs[1:-1]:
            acc = acc + ref[...].astype(F32)
        refs[-1][...] = acc.astype(refs[-1].dtype)

    spec = pl.BlockSpec((tr, c), lambda i: (i, 0))
    return pl.pallas_call(
        body, name=name, grid=(r // tr,), in_specs=[spec] * len(xs), out_specs=spec,
        out_shape=jax.ShapeDtypeStruct((r, c), out_dtype), compiler_params=_cparams(("parallel",)),
    )(*xs)


ANY = pl.BlockSpec(memory_space=pl.ANY)


def _place():
    return lax.axis_index("x"), lax.axis_index("y"), lax.axis_index("c")


def _allgather_small(v, *, name):
    r, c = v.shape

    def body(v_ref, out_ref, send_sems, recv_sems, local_sem):
        x, y, cc = _place()
        me = 4 * x + 2 * y + cc
        mine = pltpu.make_async_copy(v_ref, out_ref.at[me], local_sem)
        mine.start()
        copies = []
        for k in range(1, N_DEV):
            fx, fy, fc = (k >> 2) & 1, (k >> 1) & 1, k & 1
            peer = (1 - x if fx else x, 1 - y if fy else y, 1 - cc if fc else cc)
            copies.append(pltpu.make_async_remote_copy(
                src_ref=v_ref, dst_ref=out_ref.at[me], send_sem=send_sems.at[k - 1], recv_sem=recv_sems.at[k - 1],
                device_id=peer, device_id_type=MESH))
        for cp in copies:
            cp.start()
        for cp in copies:
            cp.wait()
        mine.wait()

    return pl.pallas_call(
        body, name=name, in_specs=[ANY], out_specs=ANY, out_shape=jax.ShapeDtypeStruct((N_DEV, r, c), v.dtype),
        scratch_shapes=[pltpu.SemaphoreType.DMA((N_DEV - 1,)), pltpu.SemaphoreType.DMA((N_DEV - 1,)),
                        pltpu.SemaphoreType.DMA],
    )(v)


def _other_chips(x, y):
    return [(1 - x, y), (x, 1 - y), (1 - x, 1 - y)]


GATHER_SEMS = [pltpu.SemaphoreType.DMA((6,)), pltpu.SemaphoreType.DMA((6,))]
SCATTER_SEMS = [pltpu.SemaphoreType.DMA((3,)), pltpu.SemaphoreType.DMA((3,))]


def _gather_stage(v_ref, out_ref, send_sems, recv_sems):
    half = v_ref.shape[0] // 2
    x, y, cc = _place()
    sibling = (x, y, 1 - cc)
    chips = _other_chips(x, y)

    def rows(px, py, pc):
        return out_ref.at[2 * px + py, pl.ds(pc * half, half), :]

    def copy(k, block, to, src=None):
        return pltpu.make_async_remote_copy(
            src_ref=rows(*block) if src is None else src, dst_ref=rows(*block),
            send_sem=send_sems.at[k], recv_sem=recv_sems.at[k], device_id=to, device_id_type=MESH)

    my_half = v_ref.at[pl.ds(cc * half, half), :]
    first = [copy(j, (x, y, cc), (*chip, cc), src=my_half) for j, chip in enumerate(chips)]
    passed = [copy(3 + j, (*chip, cc), sibling) for j, chip in enumerate(chips)]

    def start():
        for cp in first:
            cp.start()

    def relay():
        for j, chip in enumerate(chips):
            copy(j, (*chip, cc), (x, y, cc)).wait_recv()
            passed[j].start()

    def finish():
        for j, chip in enumerate(chips):
            copy(3 + j, (*chip, 1 - cc), (x, y, cc)).wait_recv()
        for cp in first + passed:
            cp.wait_send()

    return start, relay, finish


def _gather_shards(mine, *, name):
    r, c = mine.shape

    def body(v_ref, out_ref, send_sems, recv_sems):
        for phase in _gather_stage(v_ref, out_ref, send_sems, recv_sems):
            phase()

    return pl.pallas_call(
        body, name=name, in_specs=[ANY], out_specs=ANY, out_shape=jax.ShapeDtypeStruct((N_CHIPS, r, c), mine.dtype),
        scratch_shapes=GATHER_SEMS,
    )(mine)


def _swap_halves(g, *, name):
    n, _, r, c = g.shape

    def body(g_ref, got_ref, send_sems, recv_sems):
        x, y, cc = _place()
        sibling = (x, y, 1 - cc)
        rems = []
        for j in range(n):
            rems.append(pltpu.make_async_remote_copy(
                src_ref=g_ref.at[j, 1 - cc], dst_ref=got_ref.at[j], send_sem=send_sems.at[j],
                recv_sem=recv_sems.at[j], device_id=sibling, device_id_type=MESH))
        for cp in rems:
            cp.start()
        for cp in rems:
            cp.wait()

    return pl.pallas_call(
        body, name=name, in_specs=[ANY], out_specs=ANY, out_shape=jax.ShapeDtypeStruct((n, r, c), g.dtype),
        scratch_shapes=[pltpu.SemaphoreType.DMA((n,)), pltpu.SemaphoreType.DMA((n,))],
    )(g)


def _scatter_stage(s_ref, out_ref, send_sems, recv_sems):
    x, y, cc = _place()
    me = 2 * x + y
    copies = [pltpu.make_async_remote_copy(
        src_ref=s_ref.at[2 * px + py], dst_ref=out_ref.at[me], send_sem=send_sems.at[j], recv_sem=recv_sems.at[j],
        device_id=(px, py, cc), device_id_type=MESH) for j, (px, py) in enumerate(_other_chips(x, y))]

    def start():
        for cp in copies:
            cp.start()

    def finish():
        for cp in copies:
            cp.wait()

    return start, finish


def _scatter_to_chips(s, *, name):
    def body(s_ref, out_ref, send_sems, recv_sems):
        for phase in _scatter_stage(s_ref, out_ref, send_sems, recv_sems):
            phase()

    return pl.pallas_call(
        body, name=name, in_specs=[ANY], out_specs=ANY, out_shape=jax.ShapeDtypeStruct(s.shape, s.dtype),
        scratch_shapes=SCATTER_SEMS,
    )(s)


def _join_halves(f, *, name):
    r, c = f.shape

    def body(f_ref, out_ref, send_sem, recv_sem):
        x, y, cc = _place()
        cp = pltpu.make_async_remote_copy(src_ref=f_ref, dst_ref=out_ref.at[cc], send_sem=send_sem, recv_sem=recv_sem,
                                          device_id=(x, y, 1 - cc), device_id_type=MESH)
        cp.start()
        cp.wait()

    return pl.pallas_call(
        body, name=name, in_specs=[ANY], out_specs=ANY, out_shape=jax.ShapeDtypeStruct((2, r, c), f.dtype),
        scratch_shapes=[pltpu.SemaphoreType.DMA, pltpu.SemaphoreType.DMA],
    )(f)


def _pack_rows(parts, width=PACK_COLS):
    return jnp.concatenate([p.reshape(-1, width) for p in parts], axis=0)


def _pack_small(parts, rows):
    flat = jnp.concatenate([p.reshape(-1).astype(F32) for p in parts])
    return jnp.pad(flat, (0, rows * LANES - flat.shape[0])).reshape(rows, LANES)


def _unpack_small(packed, shapes):
    flat = packed.reshape(-1)
    out, o = [], 0
    for shp in shapes:
        n = int(np.prod(shp))
        out.append(flat[o:o + n].reshape(shp))
        o += n
    return out


def _perm_in_cols(w):
    a, b = Q_LORA_RANK + KV_LORA_RANK, Q_LORA_RANK + KV_LORA_RANK + QK_ROPE_DIM
    c = IN_WIDTH - 2 * N_SSD_HEADS
    return jnp.concatenate([w[:, :a], w[:, b:c], w[:, a:b], w[:, c:]], axis=1)


def _unperm_in_cols(w):
    a = Q_LORA_RANK + KV_LORA_RANK
    zx = D_INNER + XBC_WIDTH
    return jnp.concatenate([w[:, :a], w[:, a + zx:a + zx + QK_ROPE_DIM], w[:, a:a + zx], w[:, a + zx + QK_ROPE_DIM:]],
                           axis=1)


def _group_xbc(a):
    n = SSD_STATE
    parts = []
    for g in range(SSD_GROUPS):
        parts += [a[..., g * X_COLS:(g + 1) * X_COLS], a[..., D_INNER + g * n:D_INNER + (g + 1) * n],
                  a[..., D_INNER + GN + g * n:D_INNER + GN + (g + 1) * n]]
    return jnp.concatenate(parts, axis=-1)


def _ungroup_xbc(a):
    n = SSD_STATE
    xs = [a[..., g * GROUP_COLS:g * GROUP_COLS + X_COLS] for g in range(SSD_GROUPS)]
    bs = [a[..., g * GROUP_COLS + X_COLS:g * GROUP_COLS + X_COLS + n] for g in range(SSD_GROUPS)]
    cs = [a[..., g * GROUP_COLS + X_COLS + n:(g + 1) * GROUP_COLS] for g in range(SSD_GROUPS)]
    return jnp.concatenate(xs + bs + cs, axis=-1)


UP_BLOCK = 256


def _interleave_up(w):
    parts = []
    for j in range(D_FF // UP_BLOCK):
        parts += [w[:, j * UP_BLOCK:(j + 1) * UP_BLOCK], w[:, D_FF + j * UP_BLOCK:D_FF + (j + 1) * UP_BLOCK]]
    return jnp.concatenate(parts, axis=1)


def _deinterleave_up(w):
    blocks = [w[:, j * UP_BLOCK:(j + 1) * UP_BLOCK] for j in range(2 * D_FF // UP_BLOCK)]
    return jnp.concatenate(blocks[0::2] + blocks[1::2], axis=1)


def _pad_q_heads(w):
    k = w.shape[0]
    return jnp.pad(w.reshape(k, N_ATTN_HEADS, QK_DIM), ((0, 0), (0, 0), (0, HEAD_TILE - QK_DIM))).reshape(k, -1)


def _unpad_q_heads(w):
    k = w.shape[0]
    return w.reshape(k, N_ATTN_HEADS, HEAD_TILE)[..., :QK_DIM].reshape(k, N_ATTN_HEADS * QK_DIM)


def _rope_tables(seq_len):
    n_rows = seq_len // GRID_W
    row = jnp.repeat(jnp.arange(n_rows), GRID_W).astype(F32)
    col = jnp.tile(jnp.arange(GRID_W), n_rows).astype(F32)
    axis_dim = QK_ROPE_DIM // 2
    inv_freq = ROPE_THETA ** (-jnp.arange(0, axis_dim, 2, dtype=F32) / axis_dim)
    ang_r = row[:, None] * inv_freq
    ang_c = col[:, None] * inv_freq
    ang = jnp.concatenate([ang_r, ang_r, ang_c, ang_c], axis=-1)
    return jnp.cos(ang), jnp.sin(ang)


def _rot_matrix(width, start):
    r = np.zeros((width, width), np.float32)
    quarter = QK_ROPE_DIM // 4
    for base in (0, QK_ROPE_DIM // 2):
        for i in range(quarter):
            r[start + base + quarter + i, start + base + i] = -1.0
            r[start + base + i, start + base + quarter + i] = 1.0
    return jnp.asarray(r)


ROPE_STEP = QK_ROPE_DIM // 4


def _rope_flat_fn(x, cos, sin_up, sin_down):
    reps = x.shape[1] // cos.shape[1]

    def heads(t):
        return jnp.concatenate([t] * reps, axis=1)

    return (x * heads(cos) + _roll_lanes(x, -ROPE_STEP) * heads(sin_up) + _roll_lanes(x, ROPE_STEP) * heads(sin_down),)


def _krdt_fn(x, cos, sin, rot, bias):
    lane = lax.broadcasted_iota(jnp.int32, (1, KRDT_WIDTH), 1)
    is_dt = (lane >= QK_ROPE_DIM) & (lane < QK_ROPE_DIM + 2 * N_SSD_HEADS)
    roped = x * cos + _dot_exact(x, rot) * sin
    return (jnp.where(is_dt, _softplus(x + bias), roped),)


def _pre_fn(u, w, shift, scale):
    return (_rms(u, w) * (1.0 + scale) + shift,)


def _norm_fn(x, w):
    return (_rms(x, w),)


def _finish_fn(yf, yb, z, w):
    return (_rms((yf + yb) * _silu(z), w),)


def _mid_fn(x, mix, w_post, w_pre, gate, shift, scale):
    x1 = x + gate * _rms(mix, w_post)
    return (x1, _rms(x1, w_pre) * (1.0 + scale) + shift)


def _loss_fn(x1, ffn, tgt, w_post, gate):
    y = x1 + gate * _rms(ffn, w_post)
    err = y - tgt
    return (0.5 * jnp.mean(err * err, axis=-1, keepdims=True),)


def _bias_fn(x, b):
    return (x + b,)


def _silu_fn(x):
    return (_silu(x),)


def kernel(x, c, ctx, c_ctx, w_mod, b_mod, mix_pre_norm, mix_post_norm, w_in, q_norm, w_q_up, kv_norm, w_kv_up, ssd_conv_w, ssd_conv_b, ssd_a_log, ssd_dt_bias, ssd_d, ssd_norm, w_out, ffn_pre_norm, ffn_post_norm, w_up, ffn_conv_w, ffn_conv_b, w_down, loss_target, m_c_ctx, m_w_mod, m_b_mod, m_mix_pre_norm, m_mix_post_norm, m_w_in, m_q_norm, m_w_q_up, m_kv_norm, m_w_kv_up, m_ssd_conv_w, m_ssd_conv_b, m_ssd_a_log, m_ssd_dt_bias, m_ssd_d, m_ssd_norm, m_w_out, m_ffn_pre_norm, m_ffn_post_norm, m_w_up, m_ffn_conv_w, m_ffn_conv_b, m_w_down, v_c_ctx, v_w_mod, v_b_mod, v_mix_pre_norm, v_mix_post_norm, v_w_in, v_q_norm, v_w_q_up, v_kv_norm, v_w_kv_up, v_ssd_conv_w, v_ssd_conv_b, v_ssd_a_log, v_ssd_dt_bias, v_ssd_d, v_ssd_norm, v_w_out, v_ffn_pre_norm, v_ffn_post_norm, v_w_up, v_ffn_conv_w, v_ffn_conv_b, v_w_down):
    args = dict(locals())
    names = ["c_ctx", "w_mod", "b_mod", "mix_pre_norm", "mix_post_norm", "w_in", "q_norm", "w_q_up", "kv_norm",
             "w_kv_up", "ssd_conv_w", "ssd_conv_b", "ssd_a_log", "ssd_dt_bias", "ssd_d", "ssd_norm", "w_out",
             "ffn_pre_norm", "ffn_post_norm", "w_up", "ffn_conv_w", "ffn_conv_b", "w_down"]
    nb, s, d = x.shape
    nctx_rows = ctx.shape[1]
    lt = nctx_rows + s
    tl = 256 if (nctx_rows % 256 == 0 and s % 256 == 0) else 128
    nctx = nctx_rows // tl
    ncc = nctx_rows // SSD_CHUNK
    h, e, g2 = N_ATTN_HEADS, HEADS_PER_GROUP, SSD_GROUPS
    chip = 2 * lax.axis_index("x") + lax.axis_index("y")

    big_local = {n: args[n][0] for n, _, _, _ in BIG}
    big_info = {n: (rows, cols, axis) for n, rows, cols, axis in BIG}

    def pack_shards(group):
        return _pack_rows([big_local[n].astype(WIRE_DTYPE) for n in group])

    def unpack_gathered(gathered, mine, group):
        gathered = lax.dynamic_update_slice(gathered, mine[None], (chip, 0, 0))
        res, o = {}, 0
        for n in group:
            rows, cols, axis = big_info[n]
            lr, lc = big_local[n].shape
            nr = lr * lc // PACK_COLS
            seg = gathered[:, o:o + nr].reshape(N_CHIPS, lr, lc)
            o += nr
            res[n] = seg.reshape(rows, cols) if axis == 0 else jnp.transpose(seg, (1, 0, 2)).reshape(rows, cols)
        return res

    core = lax.axis_index("c")

    def pair_sums(grads_full, group, tag):
        parts = []
        for n in group:
            _, _, axis = big_info[n]
            lr, lc = big_local[n].shape
            gfull = grads_full[n]
            shards = (gfull.reshape(N_CHIPS, lr, lc) if axis == 0
                      else jnp.transpose(gfull.reshape(lr, N_CHIPS, lc), (1, 0, 2)))
            parts.append(shards.reshape(N_CHIPS, lr * lc // PACK_COLS, PACK_COLS))
        gpack = jnp.concatenate(parts, axis=1).astype(WIRE_DTYPE)
        half = gpack.shape[1] // 2
        gpack = gpack.reshape(N_CHIPS, 2, half, PACK_COLS)
        got = _swap_halves(gpack, name="grad_swap_" + tag)
        own = lax.dynamic_index_in_dim(gpack, core, axis=1, keepdims=False)
        flat = (N_CHIPS * half, PACK_COLS)
        return _sum_list([own.reshape(flat), got.reshape(flat)], out_dtype=WIRE_DTYPE,
                         name="grad_add_pair_" + tag).reshape(N_CHIPS, half, PACK_COLS)

    def chip_total(sums, scattered, tag):
        mine_sum = lax.dynamic_index_in_dim(sums, chip, axis=0, keepdims=True)
        scattered = lax.dynamic_update_slice(scattered, mine_sum, (chip, 0, 0))
        return _sum_slots(scattered, out_dtype=F32, name="grad_add_chips_" + tag)

    packed_now, packed_late = pack_shards(GATHER_NOW), pack_shards(GATHER_LATE)
    full = unpack_gathered(_gather_shards(packed_now, name="gather_weights"), packed_now, GATHER_NOW)
    n_sc, n_fc = ssd_conv_w.shape[2], ffn_conv_w.shape[2]
    n_conv = SSD_CONV * n_sc + FFN_CONV * n_fc
    first_rows = -(-(n_conv + nb * d) // (8 * LANES)) * 8
    first_all = _allgather_small(_pack_small([ssd_conv_w[0], ffn_conv_w[0], c], first_rows), name="gather_conv_c")
    first_all = first_all.reshape(N_DEV, -1)
    conv_all = first_all[::2]
    ssd_conv_full = jnp.concatenate(
        [conv_all[j][:SSD_CONV * n_sc].reshape(SSD_CONV, n_sc) for j in range(N_CHIPS)], axis=1)
    ffn_conv_full = jnp.concatenate(
        [conv_all[j][SSD_CONV * n_sc:n_conv].reshape(FFN_CONV, n_fc) for j in range(N_CHIPS)], axis=1)
    c_every = first_all[:, n_conv:n_conv + nb * d].reshape(N_DEV * nb, d)

    w_in_p = _perm_in_cols(full["w_in"])
    o_cq, o_ckv, o_z = 0, Q_LORA_RANK, Q_LORA_RANK + KV_LORA_RANK
    o_xbc, o_kr = o_z + D_INNER, o_z + D_INNER + XBC_WIDTH
    w_krdt = jnp.pad(w_in_p[:, o_kr:], ((0, 0), (0, KRDT_WIDTH - QK_ROPE_DIM - 2 * N_SSD_HEADS)))
    w_segs = [w_in_p[:, o_cq:o_ckv], w_in_p[:, o_ckv:o_z], w_in_p[:, o_z:o_xbc], _group_xbc(w_in_p[:, o_xbc:o_kr]),
              w_krdt]
    ssd_conv_g, ssd_conv_b_g = _group_xbc(ssd_conv_full), _group_xbc(ssd_conv_b)
    w_q_pad = _pad_q_heads(full["w_q_up"])

    mod_rows = 16
    n_ex = N_DEV * nb
    all_rows = -(-(n_ex + 1) // 16) * 16
    me = 2 * chip + lax.axis_index("c")
    c_all = jnp.concatenate([c_every, c_ctx[None, :], jnp.zeros((all_rows - n_ex - 1, d), F32)], axis=0)[None]
    (s_all,) = _row_fwd(_silu_fn, toks=[(c_all, 0, None, 0)], outs=[(d, F32)], nb=1, nl=all_rows, tl=all_rows,
                        name="mod_silu")
    w_mod_local = w_mod[0]
    mod_cols = w_mod_local.shape[1]
    mod_part = _mm(s_all[0], w_mod_local, name="mod_mm")
    mod_parts = _allgather_small(mod_part, name="gather_mod")[::2]
    mod_every = jnp.concatenate([mod_parts[j] for j in range(N_CHIPS)], axis=1)
    mod_lin = jnp.concatenate([lax.dynamic_slice_in_dim(mod_every, me * nb, nb, axis=0), mod_every[n_ex:n_ex + 1],
                               jnp.zeros((mod_rows - nb - 1, N_MOD * d), F32)], axis=0)
    (mod,) = _row_fwd(_bias_fn, toks=[(mod_lin[None], 0, None, 0)], vecs=[b_mod], outs=[(N_MOD * d, F32)], nb=1,
                      nl=mod_rows, tl=mod_rows, name="mod_bias")
    mods = [mod[0][:, k * d:(k + 1) * d][:, None, :] for k in range(N_MOD)]
    mods_lat = [m[:nb] for m in mods]

    u = jnp.concatenate([ctx, x], axis=1)
    (h1,) = _row_fwd(_pre_fn, toks=[(u, 0, None, 0)], vecs=[mix_pre_norm], bvecs=[mods[0], mods[1]],
                     outs=[(d, MXU_DTYPE)], nb=nb, nl=lt, tl=tl, nctx=nctx, name="pre1")
    h1f = h1.reshape(nb * lt, d)
    p_cq, p_ckv, p_z, p_xbc, p_krdt = [
        _mm(h1f, w, name="in_" + nm).reshape(nb, lt, -1)
        for nm, w in zip(("cq", "ckv", "z", "xbc", "krdt"), w_segs)]

    (cqn,) = _row_fwd(_norm_fn, toks=[(p_cq, nctx, None, 0)], vecs=[q_norm], outs=[(Q_LORA_RANK, MXU_DTYPE)],
                      nb=nb, nl=s, tl=tl, name="q_norm")
    q_flat = _mm(cqn.reshape(nb * s, -1), w_q_pad, name="q_up").reshape(nb, s, h * HEAD_TILE)
    cos, sin = _rope_tables(s)
    ones, zeros = jnp.ones((s, QK_NOPE_DIM), F32), jnp.zeros((s, QK_NOPE_DIM), F32)
    tail = HEAD_TILE - QK_DIM
    up_lanes = ((jnp.arange(QK_ROPE_DIM) // ROPE_STEP) % 2 == 0)[None, :]
    q_tables = [jnp.concatenate([pad, t, pad[:, :tail]], axis=1)[None]
                for pad, t in ((ones, cos), (zeros, jnp.where(up_lanes, -sin, 0.0)), (zeros, jnp.where(up_lanes, 0.0, sin)))]
    tq = 256
    (q_roped,) = _row_fwd(_rope_flat_fn, toks=[(q_flat, 0, None, 0)], poss=q_tables, outs=[(h * HEAD_TILE, MXU_DTYPE)],
                          nb=nb, nl=s, tl=tl, name="rope_q")

    (ckvn,) = _row_fwd(_norm_fn, toks=[(p_ckv, 0, None, 0)], vecs=[kv_norm], outs=[(KV_LORA_RANK, MXU_DTYPE)],
                       nb=nb, nl=lt, tl=tl, name="kv_norm")
    kv_flat = _mm(ckvn.reshape(nb * lt, -1), full["w_kv_up"], out_dtype=MXU_DTYPE, name="kv_up").reshape(nb, lt, -1)

    pad_w = KRDT_WIDTH - QK_ROPE_DIM
    cos_k = jnp.concatenate([jnp.ones((nctx_rows, KRDT_WIDTH), F32),
                             jnp.concatenate([cos, jnp.ones((s, pad_w), F32)], axis=1)], axis=0)[None]
    sin_k = jnp.concatenate([jnp.zeros((nctx_rows, KRDT_WIDTH), F32),
                             jnp.concatenate([sin, jnp.zeros((s, pad_w), F32)], axis=1)], axis=0)[None]
    rot_k = _rot_matrix(KRDT_WIDTH, 0)
    dt_bias_row = jnp.pad(ssd_dt_bias.reshape(1, -1), ((0, 0), (QK_ROPE_DIM, pad_w - 2 * N_SSD_HEADS)))
    (krdt,) = _row_fwd(_krdt_fn, toks=[(p_krdt, 0, None, 0)], poss=[cos_k, sin_k], vecs=[rot_k, dt_bias_row],
                       outs=[(KRDT_WIDTH, F32)], nb=nb, nl=lt, tl=tl, name="krdt")
    kr = jnp.pad(krdt[..., :QK_ROPE_DIM].astype(MXU_DTYPE), ((0, 0), (0, 0), (QK_NOPE_DIM, HEAD_TILE - QK_DIM)))
    attn, lse, gathered_late = _attn_fwd(q_roped, kv_flat, kr, packed_late, tq=tq, name="attn_fwd")
    full.update(unpack_gathered(gathered_late, packed_late, GATHER_LATE))
    w_up_il = _interleave_up(full["w_up"])
    w_out_a, w_out_s = full["w_out"][:ATTN_WIDTH], full["w_out"][ATTN_WIDTH:]

    seg = nctx_rows

    def conv_ssd_fn(xv, w, b):
        return (_silu(_dwconv(xv, w, seg) + b),)

    cb_ssd = 256
    conv_vecs = [(ssd_conv_g, 0), (ssd_conv_b_g, 0)]
    (xbc,) = _seq_fwd(conv_ssd_fn, toks=[(p_xbc, 0, 1)], vecs=conv_vecs, outs=[F32], nb=nb, nl=lt, nc=XBC_WIDTH,
                      cb=cb_ssd, name="conv_ssd")
    dt = krdt[..., QK_ROPE_DIM:QK_ROPE_DIM + 2 * N_SSD_HEADS].reshape(nb, lt, 2, g2, e)
    dtc = jnp.transpose(dt, (2, 0, 3, 1, 4))
    dtr = jnp.transpose(dt, (2, 0, 3, 4, 1))
    a_neg = -jnp.exp(ssd_a_log[0]).reshape(2, g2, e)
    d_chan = jnp.repeat(ssd_d[0], SSD_HEAD_DIM).reshape(g2, 1, X_COLS)
    scan_args = [(xbc, dtc[dr], dtr[dr], a_neg[dr][:, None, :], a_neg[dr][:, :, None], d_chan) for dr in range(2)]
    (y0, ent0), (y1, ent1) = _scan_fwd(scan_args, ncc=ncc, name="scan_fwd")
    ys, ents = [y0, y1], [ent0, ent1]
    (ssd,) = _row_fwd(_finish_fn, toks=[(ys[0], nctx, None, 0), (ys[1], nctx, None, 0), (p_z, nctx, None, 0)],
                      vecs=[ssd_norm], outs=[(D_INNER, MXU_DTYPE)], nb=nb, nl=s, tl=tl, name="ssd_finish")

    attn_f, ssd_f = attn.reshape(nb * s, ATTN_WIDTH), ssd.reshape(nb * s, D_INNER)
    mix = _mm_sum([(attn_f, w_out_a), (ssd_f, w_out_s)], name="out_proj").reshape(nb, s, d)

    mid_bvecs = [mods_lat[2], mods_lat[3], mods_lat[4]]
    x1, h2 = _row_fwd(_mid_fn, toks=[(x, 0, None, 0), (mix, 0, None, 0)], vecs=[mix_post_norm, ffn_pre_norm],
                      bvecs=mid_bvecs, outs=[(d, F32), (d, MXU_DTYPE)], nb=nb, nl=s, tl=tl, name="mid")
    up = _mm(h2.reshape(nb * s, d), w_up_il, name="ffn_up").reshape(nb, s, 2 * D_FF)

    def glu_fn(gv, w, b):
        return (_gelu(_dwconv(gv[:, :UP_BLOCK], w, 0) + b) * gv[:, UP_BLOCK:],)

    cb_ffn = UP_BLOCK
    glu_toks = [(up, 0, 2)]
    glu_vecs = [(ffn_conv_full, 0), (ffn_conv_b, 0)]
    (act,) = _seq_fwd(glu_fn, toks=glu_toks, vecs=glu_vecs, outs=[MXU_DTYPE], nb=nb, nl=s, nc=D_FF, cb=cb_ffn,
                      name="conv_glu")
    ffn = _mm(act.reshape(nb * s, D_FF), full["w_down"], name="ffn_down").reshape(nb, s, d)

    loss_toks = [(x1, 0, None, 0), (ffn, 0, None, 0), (loss_target, 0, None, 0)]
    ones_rows = jnp.ones((nb, s, 1), F32)
    (dx1_a, dffn, _), (g_ffn_post,), ((g_gate5, _),), (loss_rows,) = _row_bwd(
        _loss_fn, toks=loss_toks, vecs=[ffn_post_norm], bvecs=[mods_lat[5]], cots=[(ones_rows, 0)],
        tok_grads=[F32, MXU_DTYPE, None], emit=[(0, 1, F32)], nb=nb, nl=s, tl=tl, name="loss_bwd")
    loss_part = jnp.sum(loss_rows)

    dffn_f = dffn.reshape(nb * s, d)
    g_w_down = _mm(act.reshape(nb * s, D_FF), dffn_f, ta=True, name="wg_down")
    dact = _mm(dffn_f, full["w_down"], tb=True, out_dtype=MXU_DTYPE, name="dg_down").reshape(nb, s, D_FF)
    (dup,), (g_ffn_conv_w, g_ffn_conv_b) = _seq_bwd(
        glu_fn, toks=glu_toks, vecs=glu_vecs, cots=[[dact]], tok_grads=[MXU_DTYPE], nb=nb, nl=s, nc=D_FF,
        cb=cb_ffn, name="conv_glu_bwd")
    dup = dup.reshape(nb * s, 2 * D_FF)
    g_w_up = _deinterleave_up(_mm(h2.reshape(nb * s, d), dup, ta=True, name="wg_up"))
    dh2 = _mm(dup, w_up_il, tb=True, name="dg_up").reshape(nb, s, d)
    early_sums = pair_sums({"w_up": g_w_up, "w_down": g_w_down}, REDUCE_EARLY, "early")

    (dx_res, dmix), (g_mix_post, g_ffn_pre), ((g_gate2, _), (g_shift3, _), (g_scale4, _)), _ = _row_bwd(
        _mid_fn, toks=[(x, 0, None, 0), (mix, 0, None, 0)], vecs=[mix_post_norm, ffn_pre_norm], bvecs=mid_bvecs,
        cots=[(dx1_a, 0), (dh2, 0)], tok_grads=[F32, MXU_DTYPE], nb=nb, nl=s, tl=tl, name="mid_bwd")

    dmix_f = dmix.reshape(nb * s, d)
    g_w_out = jnp.concatenate([_mm(attn_f, dmix_f, ta=True, name="wg_out_attn"),
                               _mm(ssd_f, dmix_f, ta=True, name="wg_out_ssd")], axis=0)
    dattn = _mm(dmix_f, w_out_a, tb=True, name="dg_out_attn").reshape(nb, s, ATTN_WIDTH)
    dssd = _mm(dmix_f, w_out_s, tb=True, name="dg_out_ssd").reshape(nb, s, D_INNER)

    (dy, _, dz), (g_ssd_norm,), _, _ = _row_bwd(
        _finish_fn, toks=[(ys[0], 0, None, 0), (ys[1], 0, None, 0), (p_z, 0, None, 0)], vecs=[ssd_norm],
        cots=[(dssd, -nctx)], tok_grads=[F32, None, MXU_DTYPE], nb=nb, nl=lt, tl=tl, name="ssd_finish_bwd")
    scan_grads = _scan_bwd(scan_args, ents, dy, ncc=ncc, name="scan_bwd")
    g_dt_dirs, g_a = [], []
    for _, gdtc, gdtr, gac, gar, _ in scan_grads:
        g_dt_dirs.append(jnp.transpose(gdtc, (0, 2, 1, 3)) + jnp.transpose(gdtr, (0, 3, 1, 2)))
        g_a.append(jnp.sum(gac[:, :, 0, :] + gar[:, :, :, 0], axis=0))
    g_d_chan = jnp.sum(scan_grads[0][5], axis=0)
    g_a_log = (jnp.stack(g_a) * a_neg).reshape(1, 2, N_SSD_HEADS)
    g_dt = jnp.stack(g_dt_dirs, axis=2).reshape(nb, lt, 2 * N_SSD_HEADS)
    (dp_xbc,), (g_ssd_conv_w, g_ssd_conv_b) = _seq_bwd(
        conv_ssd_fn, toks=[(p_xbc, 0, 1)], vecs=conv_vecs, cots=[[scan_grads[0][0], scan_grads[1][0]]],
        tok_grads=[MXU_DTYPE], nb=nb, nl=lt, nc=XBC_WIDTH, cb=cb_ssd, name="conv_ssd_bwd")
    g_ssd_conv_w, g_ssd_conv_b = _ungroup_xbc(g_ssd_conv_w), _ungroup_xbc(g_ssd_conv_b)

    dq_roped, dkv, dkr, early_scattered = _attn_bwd(q_roped, kv_flat, kr, attn, lse, dattn, early_sums, tq=tq,
                                                    name="attn_bwd")
    (dq_flat,), _, _, _ = _row_bwd(_rope_flat_fn, toks=[(q_flat, 0, None, 0)], poss=q_tables, cots=[(dq_roped, 0)],
                                   tok_grads=[MXU_DTYPE], nb=nb, nl=s, tl=tl, name="rope_q_bwd")
    dq_flat = dq_flat.reshape(nb * s, h * HEAD_TILE)
    g_w_q_up = _unpad_q_heads(_mm(cqn.reshape(nb * s, -1), dq_flat, ta=True, name="wg_q_up"))
    dcqn = _mm(dq_flat, w_q_pad, tb=True, name="dg_q_up").reshape(nb, s, Q_LORA_RANK)
    (dp_cq,), (g_q_norm,), _, _ = _row_bwd(_norm_fn, toks=[(p_cq, 0, None, 0)], vecs=[q_norm], cots=[(dcqn, -nctx)],
                                           tok_grads=[MXU_DTYPE], nb=nb, nl=lt, tl=tl, name="q_norm_bwd")

    dkv_flat = dkv.reshape(nb * lt, -1)
    g_w_kv_up = _mm(ckvn.reshape(nb * lt, -1), dkv_flat, ta=True, name="wg_kv_up")
    dckvn = _mm(dkv_flat, full["w_kv_up"], tb=True, name="dg_kv_up").reshape(nb, lt, KV_LORA_RANK)
    (dp_ckv,), (g_kv_norm,), _, _ = _row_bwd(_norm_fn, toks=[(p_ckv, 0, None, 0)], vecs=[kv_norm], cots=[(dckvn, 0)],
                                             tok_grads=[MXU_DTYPE], nb=nb, nl=lt, tl=tl, name="kv_norm_bwd")

    g_krdt = jnp.concatenate([dkr[..., QK_NOPE_DIM:QK_DIM], g_dt, jnp.zeros((nb, lt, pad_w - 2 * N_SSD_HEADS), F32)],
                             axis=-1)
    (dp_krdt,), (_, g_dt_bias_row), _, _ = _row_bwd(
        _krdt_fn, toks=[(p_krdt, 0, None, 0)], poss=[cos_k, sin_k], vecs=[rot_k, dt_bias_row], cots=[(g_krdt, 0)],
        tok_grads=[MXU_DTYPE], nb=nb, nl=lt, tl=tl, name="krdt_bwd")

    dp_segs = [t.reshape(nb * lt, -1) for t in (dp_cq, dp_ckv, dz, dp_xbc, dp_krdt)]
    g_segs = [_mm(h1f, t, ta=True, name="wg_in_" + nm) for nm, t in zip(("cq", "ckv", "z", "xbc", "krdt"), dp_segs)]
    g_segs[3] = _ungroup_xbc(g_segs[3])
    g_w_in_p = jnp.concatenate(g_segs, axis=1)
    dh1 = _mm_sum(list(zip(dp_segs, w_segs)), tb=True, name="dg_in").reshape(nb, lt, d)

    def pre_res_fn(uv, w, shift, scale):
        return _pre_fn(uv, w, shift, scale) + (uv,)

    (grad_x,), (g_mix_pre,), ((g_shift0, g_shift0c), (g_scale1, g_scale1c)), _ = _row_bwd(
        pre_res_fn, toks=[(u, 0, None, 0)], vecs=[mix_pre_norm], bvecs=[mods[0], mods[1]],
        cots=[(dh1, 0), (dx_res, -nctx)], tok_grads=[F32], nb=nb, nl=lt, tl=tl, nctx=nctx, drop_blocks=nctx,
        name="pre1_bwd")

    zero_row = jnp.zeros((1, 1, d), F32)
    lat = [g_shift0, g_scale1, g_gate2, g_shift3, g_scale4, g_gate5]
    ctxg = [g_shift0c, g_scale1c, zero_row, zero_row, zero_row, zero_row]
    dmod = jnp.concatenate([jnp.concatenate([a, b], axis=0)[:, 0, :] for a, b in zip(lat, ctxg)], axis=-1)
    dmod = jnp.pad(dmod, ((0, mod_rows - nb - 1), (0, 0)))
    _, (g_b_mod,), _, _ = _row_bwd(_bias_fn, toks=[(mod_lin[None], 0, None, 0)], vecs=[b_mod], cots=[(dmod[None], 0)],
                                   tok_grads=[None], nb=1, nl=mod_rows, tl=mod_rows, name="mod_bias_bwd")
    dmod_all = _allgather_small(dmod, name="gather_dmod")
    dmod_ctx = _sum_slots(dmod_all, out_dtype=F32, name="dmod_ctx_add")[nb:nb + 1]
    dmod_every = jnp.concatenate([dmod_all[:, :nb].reshape(n_ex, N_MOD * d), dmod_ctx,
                                  jnp.zeros((all_rows - n_ex - 1, N_MOD * d), F32)], axis=0)
    dmod_mine = lax.dynamic_slice_in_dim(dmod_every, chip * mod_cols, mod_cols, axis=1)
    g_w_mod = _mm(s_all[0], dmod_mine, ta=True, name="wg_mod")[None]
    ds_all = _mm(dmod_mine, w_mod_local, tb=True, name="dg_mod")
    (dc_all,), _, _, _ = _row_bwd(_silu_fn, toks=[(c_all, 0, None, 0)], cots=[(ds_all[None], 0)], tok_grads=[F32],
                                  nb=1, nl=all_rows, tl=all_rows, name="mod_silu_bwd")
    g_c_ctx = 0.5 * dc_all[0, n_ex]

    g_w_in = _unperm_in_cols(g_w_in_p[:, :IN_WIDTH])
    last_sums = pair_sums({"w_in": g_w_in, "w_q_up": g_w_q_up, "w_kv_up": g_w_kv_up, "w_out": g_w_out}, REDUCE_LAST,
                          "last")
    halves = [chip_total(early_sums, early_scattered, "early"),
              chip_total(last_sums, _scatter_to_chips(last_sums, name="grad_scatter"), "last")]
    my_halves = jnp.concatenate(halves, axis=0)
    joined = lax.dynamic_update_slice(_join_halves(my_halves, name="grad_join"), my_halves[None], (core, 0, 0))
    g_shards, o = {}, 0
    for group, hv in zip((REDUCE_EARLY, REDUCE_LAST), halves):
        g_shards[group] = joined[:, o:o + hv.shape[0]].reshape(2 * hv.shape[0], PACK_COLS)
        o += hv.shape[0]

    g_d = jnp.sum(g_d_chan.reshape(N_SSD_HEADS, SSD_HEAD_DIM), axis=1)[None]
    g_dt_bias = g_dt_bias_row[:, QK_ROPE_DIM:QK_ROPE_DIM + 2 * N_SSD_HEADS].reshape(1, 2, N_SSD_HEADS)
    small_names = ["c_ctx", "b_mod", "mix_pre_norm", "mix_post_norm", "q_norm", "kv_norm", "ssd_conv_w", "ssd_conv_b",
                   "ssd_a_log", "ssd_dt_bias", "ssd_d", "ssd_norm", "ffn_pre_norm", "ffn_post_norm", "ffn_conv_w",
                   "ffn_conv_b"]
    small_grads = [g_c_ctx, g_b_mod, g_mix_pre, g_mix_post, g_q_norm, g_kv_norm, g_ssd_conv_w, g_ssd_conv_b,
                   g_a_log, g_dt_bias, g_d, g_ssd_norm, g_ffn_pre, g_ffn_post, g_ffn_conv_w, g_ffn_conv_b]
    small_shapes = [tuple(np.shape(a)) for a in small_grads] + [()]
    n_small = sum(int(np.prod(shp)) for shp in small_shapes)
    small_rows = -(-n_small // (8 * LANES)) * 8
    small_all = _allgather_small(_pack_small(small_grads + [loss_part], small_rows), name="gather_small")
    small_sum = _sum_slots(small_all, out_dtype=F32, name="small_add")
    small_red = _unpack_small(small_sum, small_shapes)
    loss = small_red[-1]
    grads = dict(zip(small_names, small_red[:-1]))
    grads["ssd_conv_w"] = lax.dynamic_slice_in_dim(grads["ssd_conv_w"], chip * n_sc, n_sc, axis=1)[None]
    grads["ffn_conv_w"] = lax.dynamic_slice_in_dim(grads["ffn_conv_w"], chip * n_fc, n_fc, axis=1)[None]
    for n in small_names:
        grads[n] = grads[n].reshape(args[n].shape)

    delta, new_m, new_v = {}, {}, {}
    grads["w_mod"] = g_w_mod
    for group, g_shard in g_shards.items():
        o = 0
        for n in group:
            lr, lc = big_local[n].shape
            nr = lr * lc // PACK_COLS
            grads[n] = g_shard[o:o + nr].reshape(1, lr, lc)
            o += nr
    for n in ["w_mod"] + [n for n, _, _, _ in BIG]:
        dl, nm, nv = _adamw(args[n][0], grads[n][0], args["m_" + n][0], args["v_" + n][0], name="adamw_" + n)
        delta[n], new_m[n], new_v[n] = dl[None], nm[None], nv[None]
    sm_shapes = [args[n].shape for n in small_names]
    n_sm = sum(int(np.prod(shp)) for shp in sm_shapes)
    sm_rows = -(-n_sm // (8 * LANES)) * 8
    packs = [_pack_small([src[n] for n in small_names], sm_rows)
             for src in (args, grads, {n: args["m_" + n] for n in small_names}, {n: args["v_" + n] for n in small_names})]
    for out_dict, packed_out in zip((delta, new_m, new_v), _adamw(*packs, name="adamw_small")):
        out_dict.update(zip(small_names, _unpack_small(packed_out, sm_shapes)))

    return (loss, grad_x, *[grads[n] for n in names], *[delta[n] for n in names], *[new_m[n] for n in names],
            *[new_v[n] for n in names])
```

```python
import functools
import math

import numpy as np
import jax
import jax.numpy as jnp
from jax import lax
from jax.experimental import pallas as pl
from jax.experimental.pallas import tpu as pltpu

F32 = jnp.float32
MXU_DTYPE = jnp.bfloat16
WIRE_DTYPE = jnp.bfloat16
VMEM_LIMIT_BYTES = 56 * 1024 * 1024
HIGHEST = lax.Precision.HIGHEST

D_MODEL = 1024
N_MOD = 6
EPS = 1e-6
GRID_W = 64
N_ATTN_HEADS = 16
QK_NOPE_DIM = 64
QK_ROPE_DIM = 32
QK_DIM = QK_NOPE_DIM + QK_ROPE_DIM
V_HEAD_DIM = 64
Q_LORA_RANK = 384
KV_LORA_RANK = 256
ROPE_THETA = 10000.0
ATTN_SCALE = QK_DIM ** -0.5
ATTN_WIDTH = N_ATTN_HEADS * V_HEAD_DIM
N_SSD_HEADS = 16
SSD_HEAD_DIM = 64
SSD_GROUPS = 2
HEADS_PER_GROUP = N_SSD_HEADS // SSD_GROUPS
SSD_STATE = 128
SSD_CONV = 5
SSD_CHUNK = 128
D_INNER = N_SSD_HEADS * SSD_HEAD_DIM
GN = SSD_GROUPS * SSD_STATE
XBC_WIDTH = D_INNER + 2 * GN
D_FF = 2816
FFN_CONV = 3
KRDT_WIDTH = 128
IN_WIDTH = Q_LORA_RANK + KV_LORA_RANK + QK_ROPE_DIM + D_INNER + XBC_WIDTH + 2 * N_SSD_HEADS

ADAM_LR = 0.001
ADAM_B1 = 0.9
ADAM_B2 = 0.999
ADAM_EPS = 1e-08
ADAM_WD = 0.01
ADAM_STEP = 10

N_CHIPS = 4
N_DEV = 8
MESH = pl.DeviceIdType.MESH
LANES = 128

BIG = (("w_in", D_MODEL, IN_WIDTH, 1),
       ("w_q_up", Q_LORA_RANK, N_ATTN_HEADS * QK_DIM, 1),
       ("w_kv_up", KV_LORA_RANK, N_ATTN_HEADS * (QK_NOPE_DIM + V_HEAD_DIM), 1),
       ("w_out", ATTN_WIDTH + D_INNER, D_MODEL, 0), ("w_up", D_MODEL, 2 * D_FF, 1),
       ("w_down", D_FF, D_MODEL, 0))
PACK_COLS = 1024
GATHER_NOW, GATHER_LATE = ("w_in", "w_q_up", "w_kv_up"), ("w_out", "w_up", "w_down")
REDUCE_EARLY, REDUCE_LAST = ("w_up", "w_down", "w_out"), ("w_in", "w_q_up", "w_kv_up")


def _cparams(sem):
    return pltpu.CompilerParams(dimension_semantics=sem, vmem_limit_bytes=VMEM_LIMIT_BYTES)


def _pick(n, cands):
    for c in cands:
        if n % c == 0:
            return c
    return n


def _sigmoid(x):
    return 0.5 * (jnp.tanh(0.5 * x) + 1.0)


def _silu(x):
    return x * _sigmoid(x)


@jax.custom_vjp
def _softplus(x):
    u = jnp.exp(-jnp.abs(x))
    w = 1.0 + u
    log1p = jnp.where(w == 1.0, u, jnp.log(w) * (u / jnp.where(w == 1.0, 1.0, w - 1.0)))
    return jnp.maximum(x, 0.0) + log1p


def _softplus_fwd(x):
    return _softplus(x), x


def _softplus_bwd(x, g):
    return (g * _sigmoid(x),)


_softplus.defvjp(_softplus_fwd, _softplus_bwd)


@jax.custom_vjp
def _gelu(x):
    return 0.5 * x * (1.0 + lax.erf(x * (2.0 ** -0.5)))


def _gelu_fwd(x):
    return _gelu(x), x


def _gelu_bwd(x, g):
    cdf = 0.5 * (1.0 + lax.erf(x * (2.0 ** -0.5)))
    pdf = jnp.exp(-0.5 * x * x) * (1.0 / math.sqrt(2.0 * math.pi))
    return (g * (cdf + x * pdf),)


_gelu.defvjp(_gelu_fwd, _gelu_bwd)


def _rms(x, w):
    return x * lax.rsqrt(jnp.mean(x * x, axis=-1, keepdims=True) + EPS) * w


def _shift_rows_raw(x, off, seg):
    n = x.shape[0]
    if off == 0:
        return x
    r = pltpu.roll(x, (-off) % n, 0)
    idx = lax.broadcasted_iota(jnp.int32, x.shape, 0)
    src = idx + off
    ok = (src >= 0) & (src < n)
    if seg:
        ok = ok & ((idx < seg) == (src < seg))
    return jnp.where(ok, r, 0.0)


@functools.partial(jax.custom_vjp, nondiff_argnums=(1, 2))
def _shift_rows(x, off, seg):
    return _shift_rows_raw(x, off, seg)


def _shift_rows_fwd(x, off, seg):
    return _shift_rows_raw(x, off, seg), None


def _shift_rows_bwd(off, seg, _, g):
    return (_shift_rows_raw(g, -off, seg),)


_shift_rows.defvjp(_shift_rows_fwd, _shift_rows_bwd)


@functools.partial(jax.custom_vjp, nondiff_argnums=(1,))
def _roll_lanes(x, shift):
    return pltpu.roll(x, shift % x.shape[1], 1)


def _roll_lanes_fwd(x, shift):
    return _roll_lanes(x, shift), None


def _roll_lanes_bwd(shift, _, g):
    return (pltpu.roll(g, (-shift) % g.shape[1], 1),)


_roll_lanes.defvjp(_roll_lanes_fwd, _roll_lanes_bwd)


def _row_of(w, k):
    sel = lax.broadcasted_iota(jnp.int32, (w.shape[0], 1), 0) == k
    return jnp.sum(jnp.where(sel, w, 0.0), axis=0, keepdims=True)


def _col_of(w, k):
    sel = lax.broadcasted_iota(jnp.int32, (1, w.shape[1]), 1) == k
    return jnp.sum(jnp.where(sel, w, 0.0), axis=1, keepdims=True)


def _dwconv(x, w, seg):
    k = w.shape[0]
    acc = None
    for t in range(k):
        term = _shift_rows(x, t - k // 2, seg) * _row_of(w, t)
        acc = term if acc is None else acc + term
    return acc


def _dot(a, b, dims):
    return lax.dot_general(a.astype(MXU_DTYPE), b.astype(MXU_DTYPE), (dims, ((), ())),
                           preferred_element_type=F32)


def _dot_exact(a, b):
    return lax.dot_general(a, b, (((1,), (0,)), ((), ())), precision=HIGHEST,
                           preferred_element_type=F32)


def _mask_dot_raw(mask, x, mask_left):
    hi = x.astype(jnp.bfloat16)
    rest = x - hi.astype(F32)
    mid = rest.astype(jnp.bfloat16)
    low = (rest - mid.astype(F32)).astype(jnp.bfloat16)
    m = mask.astype(jnp.bfloat16)
    acc = None
    for piece in (hi, mid, low):
        term = (lax.dot_general(m, piece, (((1,), (0,)), ((), ())), preferred_element_type=F32) if mask_left
                else lax.dot_general(piece, m, (((1,), (0,)), ((), ())), preferred_element_type=F32))
        acc = term if acc is None else acc + term
    return acc


@functools.partial(jax.custom_vjp, nondiff_argnums=(3,))
def _mask_dot(mask, mask_t, x, mask_left):
    return _mask_dot_raw(mask, x, mask_left)


def _mask_dot_fwd(mask, mask_t, x, mask_left):
    return _mask_dot_raw(mask, x, mask_left), (mask, mask_t)


def _mask_dot_bwd(mask_left, res, g):
    mask, mask_t = res
    return jnp.zeros_like(mask), jnp.zeros_like(mask_t), _mask_dot_raw(mask_t, g, mask_left)


_mask_dot.defvjp(_mask_dot_fwd, _mask_dot_bwd)


def _mm(a, b, *, ta=False, tb=False, out_dtype=F32, name):
    if ta:
        kdim, m = a.shape
    else:
        m, kdim = a.shape
    if tb:
        n, k2 = b.shape
    else:
        k2, n = b.shape
    assert kdim == k2, (a.shape, b.shape, ta, tb)
    tm = _pick(m, (1024, 1408, 512, 384, 256, 128))
    tn = _pick(n, (512, 1408, 384, 256, 128))
    tk = kdim if kdim <= 2048 else _pick(kdim, (2048, 1664, 1536, 1408, 1024, 512, 256, 128))
    nk = kdim // tk
    a_spec = pl.BlockSpec((tk, tm), lambda i, j, k: (k, i)) if ta else pl.BlockSpec((tm, tk), lambda i, j, k: (i, k))
    b_spec = pl.BlockSpec((tn, tk), lambda i, j, k: (j, k)) if tb else pl.BlockSpec((tk, tn), lambda i, j, k: (k, j))
    dims = ((0,) if ta else (1,), (1,) if tb else (0,))

    def body(a_ref, b_ref, o_ref, *scratch):
        if nk == 1:
            o_ref[...] = _dot(a_ref[...], b_ref[...], dims).astype(o_ref.dtype)
            return
        acc_ref, = scratch
        k = pl.program_id(2)

        @pl.when(k == 0)
        def _():
            acc_ref[...] = jnp.zeros_like(acc_ref)

        acc_ref[...] += _dot(a_ref[...], b_ref[...], dims)

        @pl.when(k == nk - 1)
        def _():
            o_ref[...] = acc_ref[...].astype(o_ref.dtype)

    return pl.pallas_call(
        body, name=name, grid=(m // tm, n // tn, nk),
        in_specs=[a_spec, b_spec], out_specs=pl.BlockSpec((tm, tn), lambda i, j, k: (i, j)),
        out_shape=jax.ShapeDtypeStruct((m, n), out_dtype),
        scratch_shapes=[pltpu.VMEM((tm, tn), F32)] if nk > 1 else [],
        compiler_params=_cparams(("parallel", "parallel", "arbitrary")),
    )(a, b)


def _mm_sum(pairs, *, tb=False, out_dtype=F32, name):
    m = pairs[0][0].shape[0]
    n = pairs[0][1].shape[0] if tb else pairs[0][1].shape[1]
    tm = _pick(m, (1024, 1408, 512, 384, 256, 128))
    tn = _pick(n, (512, 1408, 384, 256, 128))
    specs, args = [], []
    for a, b in pairs:
        kdim = a.shape[1]
        specs.append(pl.BlockSpec((tm, kdim), lambda i, j: (i, 0)))
        specs.append(pl.BlockSpec((tn, kdim), lambda i, j: (j, 0)) if tb else pl.BlockSpec((kdim, tn), lambda i, j: (0, j)))
        args += [a, b]
    dims = ((1,), (1,) if tb else (0,))

    def body(*refs):
        acc = None
        for t in range(len(pairs)):
            term = _dot(refs[2 * t][...], refs[2 * t + 1][...], dims)
            acc = term if acc is None else acc + term
        refs[-1][...] = acc.astype(refs[-1].dtype)

    return pl.pallas_call(
        body, name=name, grid=(m // tm, n // tn), in_specs=specs,
        out_specs=pl.BlockSpec((tm, tn), lambda i, j: (i, j)), out_shape=jax.ShapeDtypeStruct((m, n), out_dtype),
        compiler_params=_cparams(("parallel", "parallel")),
    )(*args)


def _row_specs(toks, poss, vecs, bvecs, tl, nctx, nb):
    specs, args = [], []
    for arr, off, cw, ci in toks:
        cw = arr.shape[2] if cw is None else cw
        specs.append(pl.BlockSpec((1, tl, cw), lambda b, l, off=off, ci=ci: (b, l + off, ci)))
        args.append(arr)
    for arr in poss:
        specs.append(pl.BlockSpec((1, tl, arr.shape[2]), lambda b, l: (0, l, 0)))
        args.append(arr)
    for arr in vecs:
        specs.append(pl.BlockSpec(arr.shape, lambda b, l: (0, 0)))
        args.append(arr)
    for arr in bvecs:
        if nctx:
            specs.append(pl.BlockSpec((1, 1, arr.shape[2]), lambda b, l: (jnp.where(l < nctx, nb, b), 0, 0)))
        else:
            specs.append(pl.BlockSpec((1, 1, arr.shape[2]), lambda b, l: (b, 0, 0)))
        args.append(arr)
    return specs, args


def _row_fwd(fn, *, toks, poss=(), vecs=(), bvecs=(), outs, nb, nl, tl, nctx=0, name):
    nt, npos, nv, nbv = len(toks), len(poss), len(vecs), len(bvecs)
    specs, args = _row_specs(toks, poss, vecs, bvecs, tl, nctx, nb)

    def body(*refs):
        ins, os = refs[:len(specs)], refs[len(specs):]
        tv = [r[0].astype(F32) for r in ins[:nt]]
        pv = [r[0] for r in ins[nt:nt + npos]]
        vv = [r[...] for r in ins[nt + npos:nt + npos + nv]]
        bv = [r[0] for r in ins[nt + npos + nv:]]
        res = fn(*tv, *pv, *vv, *bv)
        for o, r in zip(os, res):
            o[0] = r.astype(o.dtype)

    return pl.pallas_call(
        body, name=name, grid=(nb, nl // tl), in_specs=specs,
        out_specs=[pl.BlockSpec((1, tl, c), lambda b, l: (b, l, 0)) for c, _ in outs],
        out_shape=[jax.ShapeDtypeStruct((nb, nl, c), dt) for c, dt in outs],
        compiler_params=_cparams(("parallel", "parallel")),
    )(*args)


def _row_bwd(fn, *, toks, poss=(), vecs=(), bvecs=(), cots, tok_grads, emit=(), nb, nl, tl, nctx=0, name,
             drop_blocks=0):
    nt, npos, nv, nbv = len(toks), len(poss), len(vecs), len(bvecs)
    specs, args = _row_specs(toks, poss, vecs, bvecs, tl, nctx, nb)
    n_in = len(specs)
    cot_slots = []
    for arr, off in cots:
        if arr is None:
            cot_slots.append(None)
            continue
        cot_slots.append((len(specs), off))
        specs.append(pl.BlockSpec((1, tl, arr.shape[2]), lambda b, l, off=off: (b, jnp.maximum(l + off, 0), 0)))
        args.append(arr)
    n_all_in = len(specs)

    out_specs, out_shapes = [], []
    tok_out = []
    for (arr, off, cw, ci), dt in zip(toks, tok_grads):
        if dt is None:
            tok_out.append(None)
            continue
        cw = arr.shape[2] if cw is None else cw
        tok_out.append(len(out_specs))
        out_specs.append(pl.BlockSpec((1, tl, cw), lambda b, l: (b, jnp.maximum(l - drop_blocks, 0), 0)))
        out_shapes.append(jax.ShapeDtypeStruct((nb, nl - drop_blocks * tl, cw), dt))
    vec_out = []
    for arr in vecs:
        vec_out.append(len(out_specs))
        out_specs.append(pl.BlockSpec(arr.shape, lambda b, l: (0, 0)))
        out_shapes.append(jax.ShapeDtypeStruct(arr.shape, F32))
    bv_out = []
    for arr in bvecs:
        c = arr.shape[2]
        lat = len(out_specs)
        out_specs.append(pl.BlockSpec((1, 1, c), lambda b, l: (b, 0, 0)))
        out_shapes.append(jax.ShapeDtypeStruct((nb, 1, c), F32))
        ctx = None
        if nctx:
            ctx = len(out_specs)
            out_specs.append(pl.BlockSpec((1, 1, c), lambda b, l: (0, 0, 0)))
            out_shapes.append(jax.ShapeDtypeStruct((1, 1, c), F32))
        bv_out.append((lat, ctx))
    emit_out = []
    emit_cols = {}
    for idx, c, dt in emit:
        emit_out.append((idx, len(out_specs)))
        out_specs.append(pl.BlockSpec((1, tl, c), lambda b, l: (b, l, 0)))
        out_shapes.append(jax.ShapeDtypeStruct((nb, nl, c), dt))

    def body(*refs):
        ins, os = refs[:n_all_in], refs[n_all_in:]
        b, l = pl.program_id(0), pl.program_id(1)
        tv = [r[0].astype(F32) for r in ins[:nt]]
        pv = [r[0] for r in ins[nt:nt + npos]]
        vv = [r[...] for r in ins[nt + npos:nt + npos + nv]]
        bv = [r[0] for r in ins[nt + npos + nv:n_in]]

        def f(*d):
            return tuple(fn(*d[:nt], *pv, *d[nt:]))

        res, vjp = jax.vjp(f, *tv, *vv, *bv)
        cts = []
        for r, slot in zip(res, cot_slots):
            if slot is None:
                cts.append(jnp.zeros_like(r))
            else:
                i, off = slot
                ct = ins[i][0].astype(F32)
                if off < 0:
                    ct = jnp.where(l + off >= 0, ct, 0.0)
                cts.append(ct)
        grads = vjp(tuple(cts))

        for g, slot in zip(grads[:nt], tok_out):
            if slot is not None:
                os[slot][0] = g.astype(os[slot].dtype)

        @pl.when((b == 0) & (l == 0))
        def _():
            for slot in vec_out:
                os[slot][...] = jnp.zeros_like(os[slot])
            for _, ctx in bv_out:
                if ctx is not None:
                    os[ctx][...] = jnp.zeros_like(os[ctx])

        @pl.when(l == 0)
        def _():
            for lat, _ in bv_out:
                os[lat][...] = jnp.zeros_like(os[lat])

        for g, slot in zip(grads[nt:nt + nv], vec_out):
            os[slot][...] += g
        for g, (lat, ctx) in zip(grads[nt + nv:], bv_out):
            if ctx is None:
                os[lat][0] += g
            else:
                is_ctx = l < nctx
                os[lat][0] += jnp.where(is_ctx, 0.0, g)
                os[ctx][0] += jnp.where(is_ctx, g, 0.0)
        for idx, slot in emit_out:
            os[slot][0] = res[idx].astype(os[slot].dtype)

    out = pl.pallas_call(
        body, name=name, grid=(nb, nl // tl), in_specs=specs, out_specs=out_specs, out_shape=out_shapes,
        compiler_params=_cparams(("arbitrary", "arbitrary")),
    )(*args)
    tg = [None if s is None else out[s] for s in tok_out]
    vg = [out[s] for s in vec_out]
    bg = [(out[lat], None if ctx is None else out[ctx]) for lat, ctx in bv_out]
    em = [out[s] for _, s in emit_out]
    return tg, vg, bg, em


def _seq_specs(toks, vecs, nl, cb):
    specs, args = [], []
    for arr, off, mult in toks:
        specs.append(pl.BlockSpec((1, nl, cb * mult), lambda j, b, off=off: (b, 0, j + off)))
        args.append(arr)
    for arr, off in vecs:
        specs.append(pl.BlockSpec((arr.shape[0], cb), lambda j, b, off=off: (0, j + off)))
        args.append(arr)
    return specs, args


def _seq_fwd(fn, *, toks, vecs, outs, nb, nl, nc, cb, name):
    nt = len(toks)
    specs, args = _seq_specs(toks, vecs, nl, cb)

    def body(*refs):
        ins, os = refs[:len(specs)], refs[len(specs):]
        tv = [r[0].astype(F32) for r in ins[:nt]]
        vv = [r[...] for r in ins[nt:]]
        for o, r in zip(os, fn(*tv, *vv)):
            o[0] = r.astype(o.dtype)

    return pl.pallas_call(
        body, name=name, grid=(nc // cb, nb), in_specs=specs,
        out_specs=[pl.BlockSpec((1, nl, cb), lambda j, b: (b, 0, j)) for _ in outs],
        out_shape=[jax.ShapeDtypeStruct((nb, nl, nc), dt) for dt in outs],
        compiler_params=_cparams(("parallel", "parallel")),
    )(*args)


def _seq_bwd(fn, *, toks, vecs, cots, tok_grads, nb, nl, nc, cb, name):
    nt, nv = len(toks), len(vecs)
    specs, args = _seq_specs(toks, vecs, nl, cb)
    n_in = len(specs)
    cot_counts = [len(group) for group in cots]
    for group in cots:
        for arr in group:
            specs.append(pl.BlockSpec((1, nl, cb), lambda j, b: (b, 0, j)))
            args.append(arr)
    out_specs, out_shapes = [], []
    for (_, _, mult), dt in zip(toks, tok_grads):
        out_specs.append(pl.BlockSpec((1, nl, cb * mult), lambda j, b: (b, 0, j)))
        out_shapes.append(jax.ShapeDtypeStruct((nb, nl, nc * mult), dt))
    for arr, _ in vecs:
        out_specs.append(pl.BlockSpec((arr.shape[0], cb), lambda j, b: (0, j)))
        out_shapes.append(jax.ShapeDtypeStruct((arr.shape[0], nc), F32))

    def body(*refs):
        ins, os = refs[:len(specs)], refs[len(specs):]
        b = pl.program_id(1)
        tv = [r[0].astype(F32) for r in ins[:nt]]
        vv = [r[...] for r in ins[nt:n_in]]
        _, vjp = jax.vjp(lambda *d: tuple(fn(*d)), *tv, *vv)
        cts, o = [], n_in
        for cnt in cot_counts:
            ct = ins[o][0].astype(F32)
            for r in ins[o + 1:o + cnt]:
                ct = ct + r[0].astype(F32)
            cts.append(ct)
            o += cnt
        grads = vjp(tuple(cts))
        for g, o in zip(grads[:nt], os[:nt]):
            o[0] = g.astype(o.dtype)

        @pl.when(b == 0)
        def _():
            for o in os[nt:]:
                o[...] = jnp.zeros_like(o)

        for g, o in zip(grads[nt:], os[nt:]):
            o[...] += g

    out = pl.pallas_call(
        body, name=name, grid=(nc // cb, nb), in_specs=specs, out_specs=out_specs, out_shape=out_shapes,
        compiler_params=_cparams(("parallel", "arbitrary")),
    )(*args)
    return out[:nt], out[nt:]


EXP2_SCALE = ATTN_SCALE * math.log2(math.e)


HEAD_TILE = 128
N_HEAD_PAIRS = N_ATTN_HEADS // 2


def _head_lanes():
    lane = lax.broadcasted_iota(jnp.int32, (1, HEAD_TILE), 1)
    return lane < QK_NOPE_DIM, (lane >= QK_NOPE_DIM) & (lane < QK_DIM)


def _attn_specs(tq, lk):
    q = pl.BlockSpec((1, tq, 2 * HEAD_TILE), lambda b, pr, j: (b, j, pr))
    kv = pl.BlockSpec((1, lk, 2 * HEAD_TILE), lambda b, pr, j: (b, 0, pr))
    kr = pl.BlockSpec((1, lk, HEAD_TILE), lambda b, pr, j: (b, 0, 0))
    o = pl.BlockSpec((1, tq, HEAD_TILE), lambda b, pr, j: (b, j, pr))
    lse = pl.BlockSpec((1, 2, tq, 1), lambda b, pr, j: (b, pr, j, 0))
    return q, kv, kr, o, lse


def _grid_marks(nb, nj):
    b, pr, j = pl.program_id(0), pl.program_id(1), pl.program_id(2)
    first = (b == 0) & (pr == 0) & (j == 0)
    middle = (b == nb // 2) & (pr == 0) & (j == 0)
    last = (b == nb - 1) & (pr == N_HEAD_PAIRS - 1) & (j == nj - 1)
    return first, middle, last


def _attn_fwd(q, kv, kr, late_shard, *, tq, name):
    nb, s, _ = q.shape
    lk = kv.shape[1]
    nj = s // tq
    qs, kvs, krs, os_, lses = _attn_specs(tq, lk)

    def body(q_ref, kv_ref, kr_ref, shard_ref, o_ref, lse_ref, gathered_ref, send_sems, recv_sems):
        start, relay, finish = _gather_stage(shard_ref, gathered_ref, send_sems, recv_sems)
        first, middle, last = _grid_marks(nb, nj)
        pl.when(first)(start)
        pl.when(middle)(relay)
        low, _ = _head_lanes()
        outs = []
        for e in range(2):
            tile = pl.ds(HEAD_TILE * e, HEAD_TILE)
            kv_e = kv_ref[0, :, tile]
            keys = jnp.where(low, kv_e, kr_ref[0])
            sc = _dot(q_ref[0, :, tile], keys, ((1,), (1,)))
            m = jnp.max(sc, axis=-1, keepdims=True)
            p = jnp.exp2((sc - m) * EXP2_SCALE)
            denom = jnp.sum(p, axis=-1, keepdims=True)
            outs.append(_dot(p, kv_e, ((1,), (0,))) / denom)
            lse_ref[0, e] = m * EXP2_SCALE + jnp.log2(denom)
        o_ref[0] = jnp.where(low, pltpu.roll(outs[0], V_HEAD_DIM, 1), outs[1])
        pl.when(last)(finish)

    return pl.pallas_call(
        body, name=name, grid=(nb, N_HEAD_PAIRS, nj), in_specs=[qs, kvs, krs, ANY], out_specs=[os_, lses, ANY],
        out_shape=[jax.ShapeDtypeStruct((nb, s, ATTN_WIDTH), F32), jax.ShapeDtypeStruct((nb, N_ATTN_HEADS, s, 1), F32),
                   jax.ShapeDtypeStruct((N_CHIPS,) + late_shard.shape, late_shard.dtype)],
        scratch_shapes=GATHER_SEMS,
        compiler_params=_cparams(("arbitrary", "arbitrary", "arbitrary")),
    )(q, kv, kr, late_shard)


def _attn_bwd(q, kv, kr, o, lse, do, early_sums, *, tq, name):
    nb, s, _ = q.shape
    lk = kv.shape[1]
    nj = s // tq
    qs, kvs, krs, os_, lses = _attn_specs(tq, lk)

    def body(q_ref, kv_ref, kr_ref, o_ref, lse_ref, do_ref, sums_ref, dq_ref, dkv_ref, dkr_ref, scattered_ref,
             send_sems, recv_sems):
        start, finish = _scatter_stage(sums_ref, scattered_ref, send_sems, recv_sems)
        first, _, last = _grid_marks(nb, nj)
        pl.when(first)(start)
        pr, j = pl.program_id(1), pl.program_id(2)
        low, rope = _head_lanes()
        do_pair = do_ref[0]
        prod = do_pair * o_ref[0]

        @pl.when(j == 0)
        def _():
            dkv_ref[...] = jnp.zeros_like(dkv_ref)

        @pl.when((pr == 0) & (j == 0))
        def _():
            dkr_ref[...] = jnp.zeros_like(dkr_ref)

        dkr = None
        for e in range(2):
            tile = pl.ds(HEAD_TILE * e, HEAD_TILE)
            delta = jnp.sum(jnp.where(low if e == 0 else ~low, prod, 0.0), axis=-1, keepdims=True)
            do_e = jnp.where(low, 0.0, do_pair if e == 1 else pltpu.roll(do_pair, V_HEAD_DIM, 1))
            kv_e, q_e = kv_ref[0, :, tile], q_ref[0, :, tile]
            keys = jnp.where(low, kv_e, kr_ref[0])
            sc = _dot(q_e, keys, ((1,), (1,)))
            p = jnp.exp2(sc * EXP2_SCALE - lse_ref[0, e])
            dp = _dot(do_e, kv_e, ((1,), (1,)))
            ds = (p * (dp - delta)).astype(MXU_DTYPE)
            dq_ref[0, :, tile] = _dot(ds, keys, ((1,), (0,))) * ATTN_SCALE
            dkeys = _dot(ds, q_e, ((0,), (0,)))
            dv = _dot(p, do_e, ((0,), (0,)))
            dkv_ref[0, :, tile] += jnp.where(low, dkeys, dv)
            part = jnp.where(rope, dkeys, 0.0)
            dkr = part if dkr is None else dkr + part
        dkr_ref[0] += dkr

        @pl.when(j == nj - 1)
        def _():
            for e in range(2):
                tile = pl.ds(HEAD_TILE * e, HEAD_TILE)
                dkv_ref[0, :, tile] = dkv_ref[0, :, tile] * jnp.where(low, ATTN_SCALE, 1.0)

        @pl.when((pr == N_HEAD_PAIRS - 1) & (j == nj - 1))
        def _():
            dkr_ref[0] = dkr_ref[0] * ATTN_SCALE

        pl.when(last)(finish)

    return pl.pallas_call(
        body, name=name, grid=(nb, N_HEAD_PAIRS, nj), in_specs=[qs, kvs, krs, os_, lses, os_, ANY],
        out_specs=[qs, kvs, krs, ANY],
        out_shape=[jax.ShapeDtypeStruct(q.shape, F32), jax.ShapeDtypeStruct(kv.shape, F32),
                   jax.ShapeDtypeStruct(kr.shape, F32), jax.ShapeDtypeStruct(early_sums.shape, early_sums.dtype)],
        scratch_shapes=SCATTER_SEMS,
        compiler_params=_cparams(("arbitrary", "arbitrary", "arbitrary")),
    )(q, kv, kr, o, lse, do, early_sums)


N_PAIRS = HEADS_PER_GROUP // 2
PAIR_W = 2 * SSD_HEAD_DIM


def _ssd_chunk(states, xs, dtc, dtr, bm, cm, ac, ar, *, reverse):
    q = dtc.shape[0]
    assert q == PAIR_W == dtr.shape[1]
    row = lax.broadcasted_iota(jnp.int32, (q, q), 0)
    col = lax.broadcasted_iota(jnp.int32, (q, q), 1)
    if reverse:
        tri_c, tri_r, mask = col < row, row < col, col >= row
    else:
        tri_c, tri_r, mask = col <= row, row <= col, col <= row
    a_col, a_row = dtc * ac, dtr * ar
    tri_c, tri_r = tri_c.astype(F32), tri_r.astype(F32)
    cum_c = _mask_dot(tri_c, tri_r, a_col, True)
    cum_r = _mask_dot(tri_r, tri_c, a_row, False)
    tot_c = jnp.sum(a_col, axis=0, keepdims=True)
    tot_r = jnp.sum(a_row, axis=1, keepdims=True)
    cb = _dot(cm, bm, ((1,), (1,)))
    first = lax.broadcasted_iota(jnp.int32, (1, PAIR_W), 1) < SSD_HEAD_DIM
    first_rows = lax.broadcasted_iota(jnp.int32, (PAIR_W, 1), 0) < SSD_HEAD_DIM
    heads, pairs = range(HEADS_PER_GROUP), range(N_PAIRS)
    tile = [slice(PAIR_W * pr, PAIR_W * (pr + 1)) for pr in pairs]
    cum_p = [cum_c[:, tile[pr]] for pr in pairs]
    swapped = [_roll_lanes(cum_p[pr], SSD_HEAD_DIM) for pr in pairs]
    cc = [jnp.where(first, cum_p[e // 2], swapped[e // 2]) if e % 2 == 0
          else jnp.where(first, swapped[e // 2], cum_p[e // 2]) for e in heads]
    cr = [_row_of(cum_r, e) for e in heads]
    if reverse:
        within = [jnp.exp(jnp.where(mask, cr[e] - cc[e], -jnp.inf)) for e in heads]
        into = [jnp.exp(tot_c[:, tile[pr]] - cum_p[pr]) for pr in pairs]
        to_end = [jnp.exp(cum_p[pr]) for pr in pairs]
    else:
        within = [jnp.exp(jnp.where(mask, cc[e] - cr[e], -jnp.inf)) for e in heads]
        into = [jnp.exp(cum_p[pr]) for pr in pairs]
        to_end = [jnp.exp(tot_c[:, tile[pr]] - cum_p[pr]) for pr in pairs]
    decay = [cb * within[e] for e in heads]
    carry = [jnp.exp(_row_of(tot_r, e)) for e in heads]
    xd = [xs[pr] * dtc[:, tile[pr]] for pr in pairs]
    y_even = [_dot(decay[2 * pr], jnp.where(first, xd[pr], 0.0), ((1,), (0,))) for pr in pairs]
    y_odd = [_dot(decay[2 * pr + 1], jnp.where(first, 0.0, xd[pr]), ((1,), (0,))) for pr in pairs]
    y_off = [_dot(cm, states[pr], ((1,), (1,))) for pr in pairs]
    grow = [_dot(xd[pr] * to_end[pr], bm, ((0,), (0,))) for pr in pairs]
    ys = [y_even[pr] + y_odd[pr] + y_off[pr] * into[pr] for pr in pairs]
    new_states = [states[pr] * jnp.where(first_rows, carry[2 * pr], carry[2 * pr + 1]) + grow[pr] for pr in pairs]
    return tuple(ys) + tuple(new_states)


def _chunk_of_step(t, ncc, nch, reverse):
    if not reverse:
        return t
    return jnp.where(t < ncc, ncc - 1 - t, nch - 1 - (t - ncc))


X_COLS = D_INNER // SSD_GROUPS
GROUP_COLS = X_COLS + 2 * SSD_STATE


def _scan_in_specs(nch, ncc, reverse, back):
    q, e = SSD_CHUNK, HEADS_PER_GROUP

    def ch(t):
        return _chunk_of_step((nch - 1 - t) if back else t, ncc, nch, reverse)

    return ch, [
        pl.BlockSpec((1, q, GROUP_COLS), lambda b, g, t: (b, ch(t), g)),
        pl.BlockSpec((1, q, X_COLS), lambda b, g, t: (b, ch(t), g)),
        pl.BlockSpec((1, 1, e, q), lambda b, g, t: (b, g, 0, ch(t))),
        pl.BlockSpec((1, 1, X_COLS), lambda b, g, t: (g, 0, 0)),
        pl.BlockSpec((1, e, 1), lambda b, g, t: (g, 0, 0)),
        pl.BlockSpec((1, 1, X_COLS), lambda b, g, t: (g, 0, 0)),
    ]


def _scan_chunk_fn(reverse, skip):
    def f(states, xs, dtc, dtr, bm, cm, ac, ar, d):
        res = _ssd_chunk(states, xs, dtc, dtr, bm, cm, ac, ar, reverse=reverse)
        if not skip:
            return res
        ys = tuple(res[i] + d[:, PAIR_W * i:PAIR_W * (i + 1)] * xs[i] for i in range(N_PAIRS))
        return ys + tuple(res[N_PAIRS:])

    return f


def _scan_operands(x_ref, dtc_ref, dtr_ref, ac_ref, ar_ref, d_ref):
    xs = [x_ref[0, :, pl.ds(PAIR_W * i, PAIR_W)] for i in range(N_PAIRS)]
    bm = x_ref[0, :, pl.ds(X_COLS, SSD_STATE)]
    cm = x_ref[0, :, pl.ds(X_COLS + SSD_STATE, SSD_STATE)]
    return xs, dtc_ref[0], dtr_ref[0, 0], bm, cm, ac_ref[0], ar_ref[0], d_ref[0]


N_IN = 6


def _scan_fwd(dirs, *, ncc, name):
    nb, lt, _ = dirs[0][0].shape
    q, n = SSD_CHUNK, SSD_STATE
    nch = lt // q
    in_specs, out_specs, out_shapes, fs = [], [], [], []
    for dr in range(2):
        ch, specs = _scan_in_specs(nch, ncc, bool(dr), False)
        in_specs += specs
        fs.append(_scan_chunk_fn(bool(dr), dr == 0))
        out_specs += [pl.BlockSpec((1, q, X_COLS), lambda b, g, t, ch=ch: (b, ch(t), g)),
                      pl.BlockSpec((1, 1, 1, N_PAIRS, PAIR_W, n), lambda b, g, t: (b, g, t, 0, 0, 0))]
        out_shapes += [jax.ShapeDtypeStruct((nb, lt, D_INNER), F32),
                       jax.ShapeDtypeStruct((nb, SSD_GROUPS, nch, N_PAIRS, PAIR_W, n), F32)]

    def body(*refs):
        ins, outs, sts = refs[:2 * N_IN], refs[2 * N_IN:2 * N_IN + 4], refs[2 * N_IN + 4:]
        t = pl.program_id(2)

        @pl.when(t == 0)
        def _():
            for st_ref in sts:
                st_ref[...] = jnp.zeros_like(st_ref)

        entering = [[sts[dr][i] for i in range(N_PAIRS)] for dr in range(2)]
        results = [fs[dr](entering[dr], *_scan_operands(*ins[N_IN * dr:N_IN * (dr + 1)])) for dr in range(2)]
        for dr in range(2):
            (y_ref, ent_ref), st_ref = outs[2 * dr:2 * dr + 2], sts[dr]
            for i in range(N_PAIRS):
                ent_ref[0, 0, 0, i] = entering[dr][i]
                y_ref[0, :, pl.ds(PAIR_W * i, PAIR_W)] = results[dr][i]
                st_ref[i] = results[dr][N_PAIRS + i]

    out = pl.pallas_call(
        body, name=name, grid=(nb, SSD_GROUPS, nch), in_specs=in_specs, out_specs=out_specs, out_shape=out_shapes,
        scratch_shapes=[pltpu.VMEM((N_PAIRS, PAIR_W, n), F32)] * 2,
        compiler_params=_cparams(("parallel", "parallel", "arbitrary")),
    )(*dirs[0], *dirs[1])
    return out[:2], out[2:]


N_SCAN_GRADS = 6


def _scan_bwd(dirs, entering, dy, *, ncc, name):
    nb, lt, _ = dirs[0][0].shape
    q, n, e = SSD_CHUNK, SSD_STATE, HEADS_PER_GROUP
    nch = lt // q
    in_specs, out_specs, out_shapes, fs, args = [], [], [], [], []
    for dr in range(2):
        ch, specs = _scan_in_specs(nch, ncc, bool(dr), True)
        in_specs += specs + [
            pl.BlockSpec((1, 1, 1, N_PAIRS, PAIR_W, n), lambda b, g, t: (b, g, nch - 1 - t, 0, 0, 0)),
            pl.BlockSpec((1, q, X_COLS), lambda b, g, t, ch=ch: (b, ch(t), g))]
        args += list(dirs[dr]) + [entering[dr], dy]
        fs.append(_scan_chunk_fn(bool(dr), dr == 0))
        out_specs += [pl.BlockSpec((1, q, GROUP_COLS), lambda b, g, t, ch=ch: (b, ch(t), g)),
                      pl.BlockSpec((1, q, X_COLS), lambda b, g, t, ch=ch: (b, ch(t), g)),
                      pl.BlockSpec((1, 1, e, q), lambda b, g, t, ch=ch: (b, g, 0, ch(t))),
                      pl.BlockSpec((1, 1, 1, X_COLS), lambda b, g, t: (b, g, 0, 0)),
                      pl.BlockSpec((1, 1, e, 1), lambda b, g, t: (b, g, 0, 0)),
                      pl.BlockSpec((1, 1, 1, X_COLS), lambda b, g, t: (b, g, 0, 0))]
        out_shapes += [jax.ShapeDtypeStruct((nb, lt, SSD_GROUPS * GROUP_COLS), F32),
                       jax.ShapeDtypeStruct((nb, lt, D_INNER), F32), jax.ShapeDtypeStruct((nb, SSD_GROUPS, e, lt), F32),
                       jax.ShapeDtypeStruct((nb, SSD_GROUPS, 1, X_COLS), F32), jax.ShapeDtypeStruct((nb, SSD_GROUPS, e, 1), F32),
                       jax.ShapeDtypeStruct((nb, SSD_GROUPS, 1, X_COLS), F32)]
    n_in = N_IN + 2

    def body(*refs):
        ins = refs[:2 * n_in]
        outs = refs[2 * n_in:2 * n_in + 2 * N_SCAN_GRADS]
        dss = refs[2 * n_in + 2 * N_SCAN_GRADS:]
        t = pl.program_id(2)

        for dr in range(2):
            mine = ins[n_in * dr:n_in * (dr + 1)]
            ent_ref, dy_ref = mine[N_IN], mine[N_IN + 1]
            dx_ref, ddtc_ref, ddtr_ref, dac_ref, dar_ref, dd_ref = outs[N_SCAN_GRADS * dr:N_SCAN_GRADS * (dr + 1)]
            ds_ref = dss[dr]

            @pl.when(t == 0)
            def _():
                for ref in (ds_ref, dac_ref, dar_ref, dd_ref):
                    ref[...] = jnp.zeros_like(ref)

            states = [ent_ref[0, 0, 0, i] for i in range(N_PAIRS)]
            _, vjp = jax.vjp(fs[dr], states, *_scan_operands(*mine[:N_IN]))
            dys = [dy_ref[0, :, pl.ds(PAIR_W * i, PAIR_W)] for i in range(N_PAIRS)]
            gs, gx, gdtc, gdtr, gb, gc, gac, gar, gd = vjp(tuple(dys) + tuple(ds_ref[i] for i in range(N_PAIRS)))
            o = 0
            for part in list(gx) + [gb, gc]:
                dx_ref[0, :, pl.ds(o, part.shape[1])] = part
                o += part.shape[1]
            for i in range(N_PAIRS):
                ds_ref[i] = gs[i]
            ddtc_ref[0] = gdtc
            ddtr_ref[0, 0] = gdtr
            dac_ref[0, 0] += gac
            dar_ref[0, 0] += gar
            dd_ref[0, 0] += gd

    out = pl.pallas_call(
        body, name=name, grid=(nb, SSD_GROUPS, nch), in_specs=in_specs, out_specs=out_specs, out_shape=out_shapes,
        scratch_shapes=[pltpu.VMEM((N_PAIRS, PAIR_W, n), F32)] * 2,
        compiler_params=_cparams(("parallel", "parallel", "arbitrary")),
    )(*args)
    return out[:N_SCAN_GRADS], out[N_SCAN_GRADS:]


def _adamw(w, g, m, v, *, name):
    r, c = w.shape
    tr = _pick(r, (256, 176, 128, 96, 64, 8))
    c1 = 1.0 / (1.0 - ADAM_B1 ** ADAM_STEP)
    c2 = 1.0 / (1.0 - ADAM_B2 ** ADAM_STEP)

    def body(w_ref, g_ref, m_ref, v_ref, d_ref, nm_ref, nv_ref):
        gv = g_ref[...]
        nm = ADAM_B1 * m_ref[...] + (1.0 - ADAM_B1) * gv
        nv = ADAM_B2 * v_ref[...] + (1.0 - ADAM_B2) * (gv * gv)
        d_ref[...] = -ADAM_LR * ((nm * c1) / (jnp.sqrt(nv * c2) + ADAM_EPS) + ADAM_WD * w_ref[...])
        nm_ref[...] = nm
        nv_ref[...] = nv

    spec = pl.BlockSpec((tr, c), lambda i: (i, 0))
    return pl.pallas_call(
        body, name=name, grid=(r // tr,), in_specs=[spec] * 4, out_specs=[spec] * 3,
        out_shape=[jax.ShapeDtypeStruct((r, c), F32)] * 3, compiler_params=_cparams(("parallel",)),
    )(w, g, m, v)


def _sum_rows_tile(r):
    return r if r <= 1024 else _pick(r, (656, 512, 256, 128, 64, 32, 16))


def _sum_slots(x, *, out_dtype, name):
    n, r, c = x.shape
    tr = _sum_rows_tile(r)

    def body(x_ref, o_ref):
        acc = x_ref[0].astype(F32)
        for k in range(1, n):
            acc = acc + x_ref[k].astype(F32)
        o_ref[...] = acc.astype(o_ref.dtype)

    return pl.pallas_call(
        body, name=name, grid=(r // tr,), in_specs=[pl.BlockSpec((n, tr, c), lambda i: (0, i, 0))],
        out_specs=pl.BlockSpec((tr, c), lambda i: (i, 0)), out_shape=jax.ShapeDtypeStruct((r, c), out_dtype),
        compiler_params=_cparams(("parallel",)),
    )(x)


def _sum_list(xs, *, out_dtype, name):
    r, c = xs[0].shape
    tr = _sum_rows_tile(r)

    def body(*refs):
        acc = refs[0][...].astype(F32)
        for ref in refs[1:-1]:
            acc = acc + ref[...].astype(F32)
        refs[-1][...] = acc.astype(refs[-1].dtype)

    spec = pl.BlockSpec((tr, c), lambda i: (i, 0))
    return pl.pallas_call(
        body, name=name, grid=(r // tr,), in_specs=[spec] * len(xs), out_specs=spec,
        out_shape=jax.ShapeDtypeStruct((r, c), out_dtype), compiler_params=_cparams(("parallel",)),
    )(*xs)


ANY = pl.BlockSpec(memory_space=pl.ANY)


def _place():
    return lax.axis_index("x"), lax.axis_index("y"), lax.axis_index("c")


def _allgather_small(v, *, name):
    r, c = v.shape

    def body(v_ref, out_ref, send_sems, recv_sems, local_sem):
        x, y, cc = _place()
        me = 4 * x + 2 * y + cc
        mine = pltpu.make_async_copy(v_ref, out_ref.at[me], local_sem)
        mine.start()
        copies = []
        for k in range(1, N_DEV):
            fx, fy, fc = (k >> 2) & 1, (k >> 1) & 1, k & 1
            peer = (1 - x if fx else x, 1 - y if fy else y, 1 - cc if fc else cc)
            copies.append(pltpu.make_async_remote_copy(
                src_ref=v_ref, dst_ref=out_ref.at[me], send_sem=send_sems.at[k - 1], recv_sem=recv_sems.at[k - 1],
                device_id=peer, device_id_type=MESH))
        for cp in copies:
            cp.start()
        for cp in copies:
            cp.wait()
        mine.wait()

    return pl.pallas_call(
        body, name=name, in_specs=[ANY], out_specs=ANY, out_shape=jax.ShapeDtypeStruct((N_DEV, r, c), v.dtype),
        scratch_shapes=[pltpu.SemaphoreType.DMA((N_DEV - 1,)), pltpu.SemaphoreType.DMA((N_DEV - 1,)),
                        pltpu.SemaphoreType.DMA],
    )(v)


def _other_chips(x, y):
    return [(1 - x, y), (x, 1 - y), (1 - x, 1 - y)]


GATHER_SEMS = [pltpu.SemaphoreType.DMA((6,)), pltpu.SemaphoreType.DMA((6,))]
SCATTER_SEMS = [pltpu.SemaphoreType.DMA((3,)), pltpu.SemaphoreType.DMA((3,))]


def _gather_stage(v_ref, out_ref, send_sems, recv_sems):
    half = v_ref.shape[0] // 2
    x, y, cc = _place()
    sibling = (x, y, 1 - cc)
    chips = _other_chips(x, y)

    def rows(px, py, pc):
        return out_ref.at[2 * px + py, pl.ds(pc * half, half), :]

    def copy(k, block, to, src=None):
        return pltpu.make_async_remote_copy(
            src_ref=rows(*block) if src is None else src, dst_ref=rows(*block),
            send_sem=send_sems.at[k], recv_sem=recv_sems.at[k], device_id=to, device_id_type=MESH)

    my_half = v_ref.at[pl.ds(cc * half, half), :]
    first = [copy(j, (x, y, cc), (*chip, cc), src=my_half) for j, chip in enumerate(chips)]
    passed = [copy(3 + j, (*chip, cc), sibling) for j, chip in enumerate(chips)]

    def start():
        for cp in first:
            cp.start()

    def relay():
        for j, chip in enumerate(chips):
            copy(j, (*chip, cc), (x, y, cc)).wait_recv()
            passed[j].start()

    def finish():
        for j, chip in enumerate(chips):
            copy(3 + j, (*chip, 1 - cc), (x, y, cc)).wait_recv()
        for cp in first + passed:
            cp.wait_send()

    return start, relay, finish


def _gather_shards(mine, *, name):
    r, c = mine.shape

    def body(v_ref, out_ref, send_sems, recv_sems):
        for phase in _gather_stage(v_ref, out_ref, send_sems, recv_sems):
            phase()

    return pl.pallas_call(
        body, name=name, in_specs=[ANY], out_specs=ANY, out_shape=jax.ShapeDtypeStruct((N_CHIPS, r, c), mine.dtype),
        scratch_shapes=GATHER_SEMS,
    )(mine)


def _swap_halves(g, *, name):
    n, _, r, c = g.shape

    def body(g_ref, got_ref, send_sems, recv_sems):
        x, y, cc = _place()
        sibling = (x, y, 1 - cc)
        rems = []
        for j in range(n):
            rems.append(pltpu.make_async_remote_copy(
                src_ref=g_ref.at[j, 1 - cc], dst_ref=got_ref.at[j], send_sem=send_sems.at[j],
                recv_sem=recv_sems.at[j], device_id=sibling, device_id_type=MESH))
        for cp in rems:
            cp.start()
        for cp in rems:
            cp.wait()

    return pl.pallas_call(
        body, name=name, in_specs=[ANY], out_specs=ANY, out_shape=jax.ShapeDtypeStruct((n, r, c), g.dtype),
        scratch_shapes=[pltpu.SemaphoreType.DMA((n,)), pltpu.SemaphoreType.DMA((n,))],
    )(g)


def _scatter_stage(s_ref, out_ref, send_sems, recv_sems):
    x, y, cc = _place()
    me = 2 * x + y
    copies = [pltpu.make_async_remote_copy(
        src_ref=s_ref.at[2 * px + py], dst_ref=out_ref.at[me], send_sem=send_sems.at[j], recv_sem=recv_sems.at[j],
        device_id=(px, py, cc), device_id_type=MESH) for j, (px, py) in enumerate(_other_chips(x, y))]

    def start():
        for cp in copies:
            cp.start()

    def finish():
        for cp in copies:
            cp.wait()

    return start, finish


def _scatter_to_chips(s, *, name):
    def body(s_ref, out_ref, send_sems, recv_sems):
        for phase in _scatter_stage(s_ref, out_ref, send_sems, recv_sems):
            phase()

    return pl.pallas_call(
        body, name=name, in_specs=[ANY], out_specs=ANY, out_shape=jax.ShapeDtypeStruct(s.shape, s.dtype),
        scratch_shapes=SCATTER_SEMS,
    )(s)


def _join_halves(f, *, name):
    r, c = f.shape

    def body(f_ref, out_ref, send_sem, recv_sem):
        x, y, cc = _place()
        cp = pltpu.make_async_remote_copy(src_ref=f_ref, dst_ref=out_ref.at[cc], send_sem=send_sem, recv_sem=recv_sem,
                                          device_id=(x, y, 1 - cc), device_id_type=MESH)
        cp.start()
        cp.wait()

    return pl.pallas_call(
        body, name=name, in_specs=[ANY], out_specs=ANY, out_shape=jax.ShapeDtypeStruct((2, r, c), f.dtype),
        scratch_shapes=[pltpu.SemaphoreType.DMA, pltpu.SemaphoreType.DMA],
    )(f)


def _pack_rows(parts, width=PACK_COLS):
    return jnp.concatenate([p.reshape(-1, width) for p in parts], axis=0)


def _pack_small(parts, rows):
    flat = jnp.concatenate([p.reshape(-1).astype(F32) for p in parts])
    return jnp.pad(flat, (0, rows * LANES - flat.shape[0])).reshape(rows, LANES)


def _unpack_small(packed, shapes):
    flat = packed.reshape(-1)
    out, o = [], 0
    for shp in shapes:
        n = int(np.prod(shp))
        out.append(flat[o:o + n].reshape(shp))
        o += n
    return out


def _perm_in_cols(w):
    a, b = Q_LORA_RANK + KV_LORA_RANK, Q_LORA_RANK + KV_LORA_RANK + QK_ROPE_DIM
    c = IN_WIDTH - 2 * N_SSD_HEADS
    return jnp.concatenate([w[:, :a], w[:, b:c], w[:, a:b], w[:, c:]], axis=1)


def _unperm_in_cols(w):
    a = Q_LORA_RANK + KV_LORA_RANK
    zx = D_INNER + XBC_WIDTH
    return jnp.concatenate([w[:, :a], w[:, a + zx:a + zx + QK_ROPE_DIM], w[:, a:a + zx], w[:, a + zx + QK_ROPE_DIM:]],
                           axis=1)


def _group_xbc(a):
    n = SSD_STATE
    parts = []
    for g in range(SSD_GROUPS):
        parts += [a[..., g * X_COLS:(g + 1) * X_COLS], a[..., D_INNER + g * n:D_INNER + (g + 1) * n],
                  a[..., D_INNER + GN + g * n:D_INNER + GN + (g + 1) * n]]
    return jnp.concatenate(parts, axis=-1)


def _ungroup_xbc(a):
    n = SSD_STATE
    xs = [a[..., g * GROUP_COLS:g * GROUP_COLS + X_COLS] for g in range(SSD_GROUPS)]
    bs = [a[..., g * GROUP_COLS + X_COLS:g * GROUP_COLS + X_COLS + n] for g in range(SSD_GROUPS)]
    cs = [a[..., g * GROUP_COLS + X_COLS + n:(g + 1) * GROUP_COLS] for g in range(SSD_GROUPS)]
    return jnp.concatenate(xs + bs + cs, axis=-1)


UP_BLOCK = 256


def _interleave_up(w):
    parts = []
    for j in range(D_FF // UP_BLOCK):
        parts += [w[:, j * UP_BLOCK:(j + 1) * UP_BLOCK], w[:, D_FF + j * UP_BLOCK:D_FF + (j + 1) * UP_BLOCK]]
    return jnp.concatenate(parts, axis=1)


def _deinterleave_up(w):
    blocks = [w[:, j * UP_BLOCK:(j + 1) * UP_BLOCK] for j in range(2 * D_FF // UP_BLOCK)]
    return jnp.concatenate(blocks[0::2] + blocks[1::2], axis=1)


def _pad_q_heads(w):
    k = w.shape[0]
    return jnp.pad(w.reshape(k, N_ATTN_HEADS, QK_DIM), ((0, 0), (0, 0), (0, HEAD_TILE - QK_DIM))).reshape(k, -1)


def _unpad_q_heads(w):
    k = w.shape[0]
    return w.reshape(k, N_ATTN_HEADS, HEAD_TILE)[..., :QK_DIM].reshape(k, N_ATTN_HEADS * QK_DIM)


def _rope_tables(seq_len):
    n_rows = seq_len // GRID_W
    row = jnp.repeat(jnp.arange(n_rows), GRID_W).astype(F32)
    col = jnp.tile(jnp.arange(GRID_W), n_rows).astype(F32)
    axis_dim = QK_ROPE_DIM // 2
    inv_freq = ROPE_THETA ** (-jnp.arange(0, axis_dim, 2, dtype=F32) / axis_dim)
    ang_r = row[:, None] * inv_freq
    ang_c = col[:, None] * inv_freq
    ang = jnp.concatenate([ang_r, ang_r, ang_c, ang_c], axis=-1)
    return jnp.cos(ang), jnp.sin(ang)


def _rot_matrix(width, start):
    r = np.zeros((width, width), np.float32)
    quarter = QK_ROPE_DIM // 4
    for base in (0, QK_ROPE_DIM // 2):
        for i in range(quarter):
            r[start + base + quarter + i, start + base + i] = -1.0
            r[start + base + i, start + base + quarter + i] = 1.0
    return jnp.asarray(r)


ROPE_STEP = QK_ROPE_DIM // 4


def _rope_flat_fn(x, cos, sin_up, sin_down):
    reps = x.shape[1] // cos.shape[1]

    def heads(t):
        return jnp.concatenate([t] * reps, axis=1)

    return (x * heads(cos) + _roll_lanes(x, -ROPE_STEP) * heads(sin_up) + _roll_lanes(x, ROPE_STEP) * heads(sin_down),)


def _krdt_fn(x, cos, sin, rot, bias):
    lane = lax.broadcasted_iota(jnp.int32, (1, KRDT_WIDTH), 1)
    is_dt = (lane >= QK_ROPE_DIM) & (lane < QK_ROPE_DIM + 2 * N_SSD_HEADS)
    roped = x * cos + _dot_exact(x, rot) * sin
    return (jnp.where(is_dt, _softplus(x + bias), roped),)


def _pre_fn(u, w, shift, scale):
    return (_rms(u, w) * (1.0 + scale) + shift,)


def _norm_fn(x, w):
    return (_rms(x, w),)


def _finish_fn(yf, yb, z, w):
    return (_rms((yf + yb) * _silu(z), w),)


def _mid_fn(x, mix, w_post, w_pre, gate, shift, scale):
    x1 = x + gate * _rms(mix, w_post)
    return (x1, _rms(x1, w_pre) * (1.0 + scale) + shift)


def _loss_fn(x1, ffn, tgt, w_post, gate):
    y = x1 + gate * _rms(ffn, w_post)
    err = y - tgt
    return (0.5 * jnp.mean(err * err, axis=-1, keepdims=True),)


def _bias_fn(x, b):
    return (x + b,)


def _silu_fn(x):
    return (_silu(x),)


def kernel(x, c, ctx, c_ctx, w_mod, b_mod, mix_pre_norm, mix_post_norm, w_in, q_norm, w_q_up, kv_norm, w_kv_up, ssd_conv_w, ssd_conv_b, ssd_a_log, ssd_dt_bias, ssd_d, ssd_norm, w_out, ffn_pre_norm, ffn_post_norm, w_up, ffn_conv_w, ffn_conv_b, w_down, loss_target, m_c_ctx, m_w_mod, m_b_mod, m_mix_pre_norm, m_mix_post_norm, m_w_in, m_q_norm, m_w_q_up, m_kv_norm, m_w_kv_up, m_ssd_conv_w, m_ssd_conv_b, m_ssd_a_log, m_ssd_dt_bias, m_ssd_d, m_ssd_norm, m_w_out, m_ffn_pre_norm, m_ffn_post_norm, m_w_up, m_ffn_conv_w, m_ffn_conv_b, m_w_down, v_c_ctx, v_w_mod, v_b_mod, v_mix_pre_norm, v_mix_post_norm, v_w_in, v_q_norm, v_w_q_up, v_kv_norm, v_w_kv_up, v_ssd_conv_w, v_ssd_conv_b, v_ssd_a_log, v_ssd_dt_bias, v_ssd_d, v_ssd_norm, v_w_out, v_ffn_pre_norm, v_ffn_post_norm, v_w_up, v_ffn_conv_w, v_ffn_conv_b, v_w_down):
    args = dict(locals())
    names = ["c_ctx", "w_mod", "b_mod", "mix_pre_norm", "mix_post_norm", "w_in", "q_norm", "w_q_up", "kv_norm",
             "w_kv_up", "ssd_conv_w", "ssd_conv_b", "ssd_a_log", "ssd_dt_bias", "ssd_d", "ssd_norm", "w_out",
             "ffn_pre_norm", "ffn_post_norm", "w_up", "ffn_conv_w", "ffn_conv_b", "w_down"]
    nb, s, d = x.shape
    nctx_rows = ctx.shape[1]
    lt = nctx_rows + s
    tl = 256 if (nctx_rows % 256 == 0 and s % 256 == 0) else 128
    nctx = nctx_rows // tl
    ncc = nctx_rows // SSD_CHUNK
    h, e, g2 = N_ATTN_HEADS, HEADS_PER_GROUP, SSD_GROUPS
    chip = 2 * lax.axis_index("x") + lax.axis_index("y")

    big_local = {n: args[n][0] for n, _, _, _ in BIG}
    big_info = {n: (rows, cols, axis) for n, rows, cols, axis in BIG}

    def pack_shards(group):
        return _pack_rows([big_local[n].astype(WIRE_DTYPE) for n in group])

    def unpack_gathered(gathered, mine, group):
        gathered = lax.dynamic_update_slice(gathered, mine[None], (chip, 0, 0))
        res, o = {}, 0
        for n in group:
            rows, cols, axis = big_info[n]
            lr, lc = big_local[n].shape
            nr = lr * lc // PACK_COLS
            seg = gathered[:, o:o + nr].reshape(N_CHIPS, lr, lc)
            o += nr
            res[n] = seg.reshape(rows, cols) if axis == 0 else jnp.transpose(seg, (1, 0, 2)).reshape(rows, cols)
        return res

    core = lax.axis_index("c")

    def pair_sums(grads_full, group, tag):
        parts = []
        for n in group:
            _, _, axis = big_info[n]
            lr, lc = big_local[n].shape
            gfull = grads_full[n]
            shards = (gfull.reshape(N_CHIPS, lr, lc) if axis == 0
                      else jnp.transpose(gfull.reshape(lr, N_CHIPS, lc), (1, 0, 2)))
            parts.append(shards.reshape(N_CHIPS, lr * lc // PACK_COLS, PACK_COLS))
        gpack = jnp.concatenate(parts, axis=1).astype(WIRE_DTYPE)
        half = gpack.shape[1] // 2
        gpack = gpack.reshape(N_CHIPS, 2, half, PACK_COLS)
        got = _swap_halves(gpack, name="grad_swap_" + tag)
        own = lax.dynamic_index_in_dim(gpack, core, axis=1, keepdims=False)
        flat = (N_CHIPS * half, PACK_COLS)
        return _sum_list([own.reshape(flat), got.reshape(flat)], out_dtype=WIRE_DTYPE,
                         name="grad_add_pair_" + tag).reshape(N_CHIPS, half, PACK_COLS)

    def chip_total(sums, scattered, tag):
        mine_sum = lax.dynamic_index_in_dim(sums, chip, axis=0, keepdims=True)
        scattered = lax.dynamic_update_slice(scattered, mine_sum, (chip, 0, 0))
        return _sum_slots(scattered, out_dtype=F32, name="grad_add_chips_" + tag)

    packed_now, packed_late = pack_shards(GATHER_NOW), pack_shards(GATHER_LATE)
    full = unpack_gathered(_gather_shards(packed_now, name="gather_weights"), packed_now, GATHER_NOW)
    n_sc, n_fc = ssd_conv_w.shape[2], ffn_conv_w.shape[2]
    n_conv = SSD_CONV * n_sc + FFN_CONV * n_fc
    first_rows = -(-(n_conv + nb * d) // (8 * LANES)) * 8
    first_all = _allgather_small(_pack_small([ssd_conv_w[0], ffn_conv_w[0], c], first_rows), name="gather_conv_c")
    first_all = first_all.reshape(N_DEV, -1)
    conv_all = first_all[::2]
    ssd_conv_full = jnp.concatenate(
        [conv_all[j][:SSD_CONV * n_sc].reshape(SSD_CONV, n_sc) for j in range(N_CHIPS)], axis=1)
    ffn_conv_full = jnp.concatenate(
        [conv_all[j][SSD_CONV * n_sc:n_conv].reshape(FFN_CONV, n_fc) for j in range(N_CHIPS)], axis=1)
    c_every = first_all[:, n_conv:n_conv + nb * d].reshape(N_DEV * nb, d)

    w_in_p = _perm_in_cols(full["w_in"])
    o_cq, o_ckv, o_z = 0, Q_LORA_RANK, Q_LORA_RANK + KV_LORA_RANK
    o_xbc, o_kr = o_z + D_INNER, o_z + D_INNER + XBC_WIDTH
    w_krdt = jnp.pad(w_in_p[:, o_kr:], ((0, 0), (0, KRDT_WIDTH - QK_ROPE_DIM - 2 * N_SSD_HEADS)))
    w_segs = [w_in_p[:, o_cq:o_ckv], w_in_p[:, o_ckv:o_z], w_in_p[:, o_z:o_xbc], _group_xbc(w_in_p[:, o_xbc:o_kr]),
              w_krdt]
    ssd_conv_g, ssd_conv_b_g = _group_xbc(ssd_conv_full), _group_xbc(ssd_conv_b)
    w_q_pad = _pad_q_heads(full["w_q_up"])

    mod_rows = 16
    n_ex = N_DEV * nb
    all_rows = -(-(n_ex + 1) // 16) * 16
    me = 2 * chip + lax.axis_index("c")
    c_all = jnp.concatenate([c_every, c_ctx[None, :], jnp.zeros((all_rows - n_ex - 1, d), F32)], axis=0)[None]
    (s_all,) = _row_fwd(_silu_fn, toks=[(c_all, 0, None, 0)], outs=[(d, F32)], nb=1, nl=all_rows, tl=all_rows,
                        name="mod_silu")
    w_mod_local = w_mod[0]
    mod_cols = w_mod_local.shape[1]
    mod_part = _mm(s_all[0], w_mod_local, name="mod_mm")
    mod_parts = _allgather_small(mod_part, name="gather_mod")[::2]
    mod_every = jnp.concatenate([mod_parts[j] for j in range(N_CHIPS)], axis=1)
    mod_lin = jnp.concatenate([lax.dynamic_slice_in_dim(mod_every, me * nb, nb, axis=0), mod_every[n_ex:n_ex + 1],
                               jnp.zeros((mod_rows - nb - 1, N_MOD * d), F32)], axis=0)
    (mod,) = _row_fwd(_bias_fn, toks=[(mod_lin[None], 0, None, 0)], vecs=[b_mod], outs=[(N_MOD * d, F32)], nb=1,
                      nl=mod_rows, tl=mod_rows, name="mod_bias")
    mods = [mod[0][:, k * d:(k + 1) * d][:, None, :] for k in range(N_MOD)]
    mods_lat = [m[:nb] for m in mods]

    u = jnp.concatenate([ctx, x], axis=1)
    (h1,) = _row_fwd(_pre_fn, toks=[(u, 0, None, 0)], vecs=[mix_pre_norm], bvecs=[mods[0], mods[1]],
                     outs=[(d, MXU_DTYPE)], nb=nb, nl=lt, tl=tl, nctx=nctx, name="pre1")
    h1f = h1.reshape(nb * lt, d)
    p_cq, p_ckv, p_z, p_xbc, p_krdt = [
        _mm(h1f, w, name="in_" + nm).reshape(nb, lt, -1)
        for nm, w in zip(("cq", "ckv", "z", "xbc", "krdt"), w_segs)]

    (cqn,) = _row_fwd(_norm_fn, toks=[(p_cq, nctx, None, 0)], vecs=[q_norm], outs=[(Q_LORA_RANK, MXU_DTYPE)],
                      nb=nb, nl=s, tl=tl, name="q_norm")
    q_flat = _mm(cqn.reshape(nb * s, -1), w_q_pad, name="q_up").reshape(nb, s, h * HEAD_TILE)
    cos, sin = _rope_tables(s)
    ones, zeros = jnp.ones((s, QK_NOPE_DIM), F32), jnp.zeros((s, QK_NOPE_DIM), F32)
    tail = HEAD_TILE - QK_DIM
    up_lanes = ((jnp.arange(QK_ROPE_DIM) // ROPE_STEP) % 2 == 0)[None, :]
    q_tables = [jnp.concatenate([pad, t, pad[:, :tail]], axis=1)[None]
                for pad, t in ((ones, cos), (zeros, jnp.where(up_lanes, -sin, 0.0)), (zeros, jnp.where(up_lanes, 0.0, sin)))]
    tq = 256
    (q_roped,) = _row_fwd(_rope_flat_fn, toks=[(q_flat, 0, None, 0)], poss=q_tables, outs=[(h * HEAD_TILE, MXU_DTYPE)],
                          nb=nb, nl=s, tl=tl, name="rope_q")

    (ckvn,) = _row_fwd(_norm_fn, toks=[(p_ckv, 0, None, 0)], vecs=[kv_norm], outs=[(KV_LORA_RANK, MXU_DTYPE)],
                       nb=nb, nl=lt, tl=tl, name="kv_norm")
    kv_flat = _mm(ckvn.reshape(nb * lt, -1), full["w_kv_up"], out_dtype=MXU_DTYPE, name="kv_up").reshape(nb, lt, -1)

    pad_w = KRDT_WIDTH - QK_ROPE_DIM
    cos_k = jnp.concatenate([jnp.ones((nctx_rows, KRDT_WIDTH), F32),
                             jnp.concatenate([cos, jnp.ones((s, pad_w), F32)], axis=1)], axis=0)[None]
    sin_k = jnp.concatenate([jnp.zeros((nctx_rows, KRDT_WIDTH), F32),
                             jnp.concatenate([sin, jnp.zeros((s, pad_w), F32)], axis=1)], axis=0)[None]
    rot_k = _rot_matrix(KRDT_WIDTH, 0)
    dt_bias_row = jnp.pad(ssd_dt_bias.reshape(1, -1), ((0, 0), (QK_ROPE_DIM, pad_w - 2 * N_SSD_HEADS)))
    (krdt,) = _row_fwd(_krdt_fn, toks=[(p_krdt, 0, None, 0)], poss=[cos_k, sin_k], vecs=[rot_k, dt_bias_row],
                       outs=[(KRDT_WIDTH, F32)], nb=nb, nl=lt, tl=tl, name="krdt")
    kr = jnp.pad(krdt[..., :QK_ROPE_DIM].astype(MXU_DTYPE), ((0, 0), (0, 0), (QK_NOPE_DIM, HEAD_TILE - QK_DIM)))
    attn, lse, gathered_late = _attn_fwd(q_roped, kv_flat, kr, packed_late, tq=tq, name="attn_fwd")
    full.update(unpack_gathered(gathered_late, packed_late, GATHER_LATE))
    w_up_il = _interleave_up(full["w_up"])
    w_out_a, w_out_s = full["w_out"][:ATTN_WIDTH], full["w_out"][ATTN_WIDTH:]

    seg = nctx_rows

    def conv_ssd_fn(xv, w, b):
        return (_silu(_dwconv(xv, w, seg) + b),)

    cb_ssd = 256
    conv_vecs = [(ssd_conv_g, 0), (ssd_conv_b_g, 0)]
    (xbc,) = _seq_fwd(conv_ssd_fn, toks=[(p_xbc, 0, 1)], vecs=conv_vecs, outs=[F32], nb=nb, nl=lt, nc=XBC_WIDTH,
                      cb=cb_ssd, name="conv_ssd")
    dt = krdt[..., QK_ROPE_DIM:QK_ROPE_DIM + 2 * N_SSD_HEADS].reshape(nb, lt, 2, g2, e)
    dtc = [jnp.repeat(dt[:, :, dr].reshape(nb, lt, N_SSD_HEADS), SSD_HEAD_DIM, axis=-1) for dr in range(2)]
    dtr = jnp.transpose(dt, (2, 0, 3, 4, 1))
    a_neg = -jnp.exp(ssd_a_log[0]).reshape(2, g2, e)
    d_chan = jnp.repeat(ssd_d[0], SSD_HEAD_DIM).reshape(g2, 1, X_COLS)
    a_chan = [jnp.repeat(a_neg[dr].reshape(-1), SSD_HEAD_DIM).reshape(g2, 1, X_COLS) for dr in range(2)]
    scan_args = [(xbc, dtc[dr], dtr[dr], a_chan[dr], a_neg[dr][:, :, None], d_chan) for dr in range(2)]
    (y0, ent0), (y1, ent1) = _scan_fwd(scan_args, ncc=ncc, name="scan_fwd")
    ys, ents = [y0, y1], [ent0, ent1]
    (ssd,) = _row_fwd(_finish_fn, toks=[(ys[0], nctx, None, 0), (ys[1], nctx, None, 0), (p_z, nctx, None, 0)],
                      vecs=[ssd_norm], outs=[(D_INNER, MXU_DTYPE)], nb=nb, nl=s, tl=tl, name="ssd_finish")

    attn_f, ssd_f = attn.reshape(nb * s, ATTN_WIDTH), ssd.reshape(nb * s, D_INNER)
    mix = _mm_sum([(attn_f, w_out_a), (ssd_f, w_out_s)], name="out_proj").reshape(nb, s, d)

    mid_bvecs = [mods_lat[2], mods_lat[3], mods_lat[4]]
    x1, h2 = _row_fwd(_mid_fn, toks=[(x, 0, None, 0), (mix, 0, None, 0)], vecs=[mix_post_norm, ffn_pre_norm],
                      bvecs=mid_bvecs, outs=[(d, F32), (d, MXU_DTYPE)], nb=nb, nl=s, tl=tl, name="mid")
    up = _mm(h2.reshape(nb * s, d), w_up_il, name="ffn_up").reshape(nb, s, 2 * D_FF)

    def glu_fn(gv, w, b):
        return (_gelu(_dwconv(gv[:, :UP_BLOCK], w, 0) + b) * gv[:, UP_BLOCK:],)

    cb_ffn = UP_BLOCK
    glu_toks = [(up, 0, 2)]
    glu_vecs = [(ffn_conv_full, 0), (ffn_conv_b, 0)]
    (act,) = _seq_fwd(glu_fn, toks=glu_toks, vecs=glu_vecs, outs=[MXU_DTYPE], nb=nb, nl=s, nc=D_FF, cb=cb_ffn,
                      name="conv_glu")
    ffn = _mm(act.reshape(nb * s, D_FF), full["w_down"], name="ffn_down").reshape(nb, s, d)

    loss_toks = [(x1, 0, None, 0), (ffn, 0, None, 0), (loss_target, 0, None, 0)]
    ones_rows = jnp.ones((nb, s, 1), F32)
    (dx1_a, dffn, _), (g_ffn_post,), ((g_gate5, _),), (loss_rows,) = _row_bwd(
        _loss_fn, toks=loss_toks, vecs=[ffn_post_norm], bvecs=[mods_lat[5]], cots=[(ones_rows, 0)],
        tok_grads=[F32, MXU_DTYPE, None], emit=[(0, 1, F32)], nb=nb, nl=s, tl=tl, name="loss_bwd")
    loss_part = jnp.sum(loss_rows)

    dffn_f = dffn.reshape(nb * s, d)
    g_w_down = _mm(act.reshape(nb * s, D_FF), dffn_f, ta=True, name="wg_down")
    dact = _mm(dffn_f, full["w_down"], tb=True, out_dtype=MXU_DTYPE, name="dg_down").reshape(nb, s, D_FF)
    (dup,), (g_ffn_conv_w, g_ffn_conv_b) = _seq_bwd(
        glu_fn, toks=glu_toks, vecs=glu_vecs, cots=[[dact]], tok_grads=[MXU_DTYPE], nb=nb, nl=s, nc=D_FF,
        cb=cb_ffn, name="conv_glu_bwd")
    dup = dup.reshape(nb * s, 2 * D_FF)
    g_w_up = _deinterleave_up(_mm(h2.reshape(nb * s, d), dup, ta=True, name="wg_up"))
    dh2 = _mm(dup, w_up_il, tb=True, name="dg_up").reshape(nb, s, d)

    (dx_res, dmix), (g_mix_post, g_ffn_pre), ((g_gate2, _), (g_shift3, _), (g_scale4, _)), _ = _row_bwd(
        _mid_fn, toks=[(x, 0, None, 0), (mix, 0, None, 0)], vecs=[mix_post_norm, ffn_pre_norm], bvecs=mid_bvecs,
        cots=[(dx1_a, 0), (dh2, 0)], tok_grads=[F32, MXU_DTYPE], nb=nb, nl=s, tl=tl, name="mid_bwd")

    dmix_f = dmix.reshape(nb * s, d)
    g_w_out = jnp.concatenate([_mm(attn_f, dmix_f, ta=True, name="wg_out_attn"),
                               _mm(ssd_f, dmix_f, ta=True, name="wg_out_ssd")], axis=0)
    early_sums = pair_sums({"w_up": g_w_up, "w_down": g_w_down, "w_out": g_w_out}, REDUCE_EARLY, "early")
    dattn = _mm(dmix_f, w_out_a, tb=True, name="dg_out_attn").reshape(nb, s, ATTN_WIDTH)
    dssd = _mm(dmix_f, w_out_s, tb=True, name="dg_out_ssd").reshape(nb, s, D_INNER)

    (dy, _, dz), (g_ssd_norm,), _, _ = _row_bwd(
        _finish_fn, toks=[(ys[0], 0, None, 0), (ys[1], 0, None, 0), (p_z, 0, None, 0)], vecs=[ssd_norm],
        cots=[(dssd, -nctx)], tok_grads=[F32, None, MXU_DTYPE], nb=nb, nl=lt, tl=tl, name="ssd_finish_bwd")
    scan_grads = _scan_bwd(scan_args, ents, dy, ncc=ncc, name="scan_bwd")
    g_dt_dirs, g_a = [], []
    for _, gdtc, gdtr, gac, gar, _ in scan_grads:
        per_head = jnp.sum(gdtc.reshape(nb, lt, g2, e, SSD_HEAD_DIM), axis=-1)
        g_dt_dirs.append(per_head + jnp.transpose(gdtr, (0, 3, 1, 2)))
        g_a.append(jnp.sum(jnp.sum(gac.reshape(nb, g2, e, SSD_HEAD_DIM), axis=-1) + gar[:, :, :, 0], axis=0))
    g_d_chan = jnp.sum(scan_grads[0][5], axis=0)
    g_a_log = (jnp.stack(g_a) * a_neg).reshape(1, 2, N_SSD_HEADS)
    g_dt = jnp.stack(g_dt_dirs, axis=2).reshape(nb, lt, 2 * N_SSD_HEADS)
    (dp_xbc,), (g_ssd_conv_w, g_ssd_conv_b) = _seq_bwd(
        conv_ssd_fn, toks=[(p_xbc, 0, 1)], vecs=conv_vecs, cots=[[scan_grads[0][0], scan_grads[1][0]]],
        tok_grads=[MXU_DTYPE], nb=nb, nl=lt, nc=XBC_WIDTH, cb=cb_ssd, name="conv_ssd_bwd")
    g_ssd_conv_w, g_ssd_conv_b = _ungroup_xbc(g_ssd_conv_w), _ungroup_xbc(g_ssd_conv_b)

    dq_roped, dkv, dkr, early_scattered = _attn_bwd(q_roped, kv_flat, kr, attn, lse, dattn, early_sums, tq=tq,
                                                    name="attn_bwd")
    (dq_flat,), _, _, _ = _row_bwd(_rope_flat_fn, toks=[(q_flat, 0, None, 0)], poss=q_tables, cots=[(dq_roped, 0)],
                                   tok_grads=[MXU_DTYPE], nb=nb, nl=s, tl=tl, name="rope_q_bwd")
    dq_flat = dq_flat.reshape(nb * s, h * HEAD_TILE)
    g_w_q_up = _unpad_q_heads(_mm(cqn.reshape(nb * s, -1), dq_flat, ta=True, name="wg_q_up"))
    dcqn = _mm(dq_flat, w_q_pad, tb=True, name="dg_q_up").reshape(nb, s, Q_LORA_RANK)
    (dp_cq,), (g_q_norm,), _, _ = _row_bwd(_norm_fn, toks=[(p_cq, 0, None, 0)], vecs=[q_norm], cots=[(dcqn, -nctx)],
                                           tok_grads=[MXU_DTYPE], nb=nb, nl=lt, tl=tl, name="q_norm_bwd")

    dkv_flat = dkv.reshape(nb * lt, -1)
    g_w_kv_up = _mm(ckvn.reshape(nb * lt, -1), dkv_flat, ta=True, name="wg_kv_up")
    dckvn = _mm(dkv_flat, full["w_kv_up"], tb=True, name="dg_kv_up").reshape(nb, lt, KV_LORA_RANK)
    (dp_ckv,), (g_kv_norm,), _, _ = _row_bwd(_norm_fn, toks=[(p_ckv, 0, None, 0)], vecs=[kv_norm], cots=[(dckvn, 0)],
                                             tok_grads=[MXU_DTYPE], nb=nb, nl=lt, tl=tl, name="kv_norm_bwd")

    g_krdt = jnp.concatenate([dkr[..., QK_NOPE_DIM:QK_DIM], g_dt, jnp.zeros((nb, lt, pad_w - 2 * N_SSD_HEADS), F32)],
                             axis=-1)
    (dp_krdt,), (_, g_dt_bias_row), _, _ = _row_bwd(
        _krdt_fn, toks=[(p_krdt, 0, None, 0)], poss=[cos_k, sin_k], vecs=[rot_k, dt_bias_row], cots=[(g_krdt, 0)],
        tok_grads=[MXU_DTYPE], nb=nb, nl=lt, tl=tl, name="krdt_bwd")

    dp_segs = [t.reshape(nb * lt, -1) for t in (dp_cq, dp_ckv, dz, dp_xbc, dp_krdt)]
    g_segs = [_mm(h1f, t, ta=True, name="wg_in_" + nm) for nm, t in zip(("cq", "ckv", "z", "xbc", "krdt"), dp_segs)]
    g_segs[3] = _ungroup_xbc(g_segs[3])
    g_w_in_p = jnp.concatenate(g_segs, axis=1)
    dh1 = _mm_sum(list(zip(dp_segs, w_segs)), tb=True, name="dg_in").reshape(nb, lt, d)

    def pre_res_fn(uv, w, shift, scale):
        return _pre_fn(uv, w, shift, scale) + (uv,)

    (grad_x,), (g_mix_pre,), ((g_shift0, g_shift0c), (g_scale1, g_scale1c)), _ = _row_bwd(
        pre_res_fn, toks=[(u, 0, None, 0)], vecs=[mix_pre_norm], bvecs=[mods[0], mods[1]],
        cots=[(dh1, 0), (dx_res, -nctx)], tok_grads=[F32], nb=nb, nl=lt, tl=tl, nctx=nctx, drop_blocks=nctx,
        name="pre1_bwd")

    zero_row = jnp.zeros((1, 1, d), F32)
    lat = [g_shift0, g_scale1, g_gate2, g_shift3, g_scale4, g_gate5]
    ctxg = [g_shift0c, g_scale1c, zero_row, zero_row, zero_row, zero_row]
    dmod = jnp.concatenate([jnp.concatenate([a, b], axis=0)[:, 0, :] for a, b in zip(lat, ctxg)], axis=-1)
    dmod = jnp.pad(dmod, ((0, mod_rows - nb - 1), (0, 0)))
    _, (g_b_mod,), _, _ = _row_bwd(_bias_fn, toks=[(mod_lin[None], 0, None, 0)], vecs=[b_mod], cots=[(dmod[None], 0)],
                                   tok_grads=[None], nb=1, nl=mod_rows, tl=mod_rows, name="mod_bias_bwd")
    dmod_all = _allgather_small(dmod[:8], name="gather_dmod")
    dmod_ctx = _sum_slots(dmod_all, out_dtype=F32, name="dmod_ctx_add")[nb:nb + 1]
    dmod_every = jnp.concatenate([dmod_all[:, :nb].reshape(n_ex, N_MOD * d), dmod_ctx,
                                  jnp.zeros((all_rows - n_ex - 1, N_MOD * d), F32)], axis=0)
    dmod_mine = lax.dynamic_slice_in_dim(dmod_every, chip * mod_cols, mod_cols, axis=1)
    g_w_mod = _mm(s_all[0], dmod_mine, ta=True, name="wg_mod")[None]
    ds_all = _mm(dmod_mine, w_mod_local, tb=True, name="dg_mod")
    (dc_all,), _, _, _ = _row_bwd(_silu_fn, toks=[(c_all, 0, None, 0)], cots=[(ds_all[None], 0)], tok_grads=[F32],
                                  nb=1, nl=all_rows, tl=all_rows, name="mod_silu_bwd")
    g_c_ctx = 0.5 * dc_all[0, n_ex]

    g_w_in = _unperm_in_cols(g_w_in_p[:, :IN_WIDTH])
    last_sums = pair_sums({"w_in": g_w_in, "w_q_up": g_w_q_up, "w_kv_up": g_w_kv_up}, REDUCE_LAST, "last")
    halves = [chip_total(early_sums, early_scattered, "early"),
              chip_total(last_sums, _scatter_to_chips(last_sums, name="grad_scatter"), "last")]
    my_halves = jnp.concatenate(halves, axis=0)
    joined = lax.dynamic_update_slice(_join_halves(my_halves, name="grad_join"), my_halves[None], (core, 0, 0))
    g_shards, o = {}, 0
    for group, hv in zip((REDUCE_EARLY, REDUCE_LAST), halves):
        g_shards[group] = joined[:, o:o + hv.shape[0]].reshape(2 * hv.shape[0], PACK_COLS)
        o += hv.shape[0]

    g_d = jnp.sum(g_d_chan.reshape(N_SSD_HEADS, SSD_HEAD_DIM), axis=1)[None]
    g_dt_bias = g_dt_bias_row[:, QK_ROPE_DIM:QK_ROPE_DIM + 2 * N_SSD_HEADS].reshape(1, 2, N_SSD_HEADS)
    small_names = ["c_ctx", "b_mod", "mix_pre_norm", "mix_post_norm", "q_norm", "kv_norm", "ssd_conv_w", "ssd_conv_b",
                   "ssd_a_log", "ssd_dt_bias", "ssd_d", "ssd_norm", "ffn_pre_norm", "ffn_post_norm", "ffn_conv_w",
                   "ffn_conv_b"]
    small_grads = [g_c_ctx, g_b_mod, g_mix_pre, g_mix_post, g_q_norm, g_kv_norm, g_ssd_conv_w, g_ssd_conv_b,
                   g_a_log, g_dt_bias, g_d, g_ssd_norm, g_ffn_pre, g_ffn_post, g_ffn_conv_w, g_ffn_conv_b]
    small_shapes = [tuple(np.shape(a)) for a in small_grads] + [()]
    n_small = sum(int(np.prod(shp)) for shp in small_shapes)
    small_rows = -(-n_small // (8 * LANES)) * 8
    small_all = _allgather_small(_pack_small(small_grads + [loss_part], small_rows), name="gather_small")
    small_sum = _sum_slots(small_all, out_dtype=F32, name="small_add")
    small_red = _unpack_small(small_sum, small_shapes)
    loss = small_red[-1]
    grads = dict(zip(small_names, small_red[:-1]))
    grads["ssd_conv_w"] = lax.dynamic_slice_in_dim(grads["ssd_conv_w"], chip * n_sc, n_sc, axis=1)[None]
    grads["ffn_conv_w"] = lax.dynamic_slice_in_dim(grads["ffn_conv_w"], chip * n_fc, n_fc, axis=1)[None]
    for n in small_names:
        grads[n] = grads[n].reshape(args[n].shape)

    delta, new_m, new_v = {}, {}, {}
    grads["w_mod"] = g_w_mod
    for group, g_shard in g_shards.items():
        o = 0
        for n in group:
            lr, lc = big_local[n].shape
            nr = lr * lc // PACK_COLS
            grads[n] = g_shard[o:o + nr].reshape(1, lr, lc)
            o += nr
    for n in ["w_mod"] + [n for n, _, _, _ in BIG]:
        dl, nm, nv = _adamw(args[n][0], grads[n][0], args["m_" + n][0], args["v_" + n][0], name="adamw_" + n)
        delta[n], new_m[n], new_v[n] = dl[None], nm[None], nv[None]
    sm_shapes = [args[n].shape for n in small_names]
    n_sm = sum(int(np.prod(shp)) for shp in sm_shapes)
    sm_rows = -(-n_sm // (8 * LANES)) * 8
    packs = [_pack_small([src[n] for n in small_names], sm_rows)
             for src in (args, grads, {n: args["m_" + n] for n in small_names}, {n: args["v_" + n] for n in small_names})]
    for out_dict, packed_out in zip((delta, new_m, new_v), _adamw(*packs, name="adamw_small")):
        out_dict.update(zip(small_names, _unpack_small(packed_out, sm_shapes)))

    return (loss, grad_x, *[grads[n] for n in names], *[delta[n] for n in names], *[new_m[n] for n in names],
            *[new_v[n] for n in names])
```

```python
import functools
import math

import numpy as np
import jax
import jax.numpy as jnp
from jax import lax
from jax.experimental import pallas as pl
from jax.experimental.pallas import tpu as pltpu

F32 = jnp.float32
MXU_DTYPE = jnp.bfloat16
WIRE_DTYPE = jnp.bfloat16
VMEM_LIMIT_BYTES = 56 * 1024 * 1024
HIGHEST = lax.Precision.HIGHEST

D_MODEL = 1024
N_MOD = 6
EPS = 1e-6
GRID_W = 64
N_ATTN_HEADS = 16
QK_NOPE_DIM = 64
QK_ROPE_DIM = 32
QK_DIM = QK_NOPE_DIM + QK_ROPE_DIM
V_HEAD_DIM = 64
Q_LORA_RANK = 384
KV_LORA_RANK = 256
ROPE_THETA = 10000.0
ATTN_SCALE = QK_DIM ** -0.5
ATTN_WIDTH = N_ATTN_HEADS * V_HEAD_DIM
N_SSD_HEADS = 16
SSD_HEAD_DIM = 64
SSD_GROUPS = 2
HEADS_PER_GROUP = N_SSD_HEADS // SSD_GROUPS
SSD_STATE = 128
SSD_CONV = 5
SSD_CHUNK = 128
D_INNER = N_SSD_HEADS * SSD_HEAD_DIM
GN = SSD_GROUPS * SSD_STATE
XBC_WIDTH = D_INNER + 2 * GN
D_FF = 2816
FFN_CONV = 3
KRDT_WIDTH = 128
IN_WIDTH = Q_LORA_RANK + KV_LORA_RANK + QK_ROPE_DIM + D_INNER + XBC_WIDTH + 2 * N_SSD_HEADS

ADAM_LR = 0.001
ADAM_B1 = 0.9
ADAM_B2 = 0.999
ADAM_EPS = 1e-08
ADAM_WD = 0.01
ADAM_STEP = 10

N_CHIPS = 4
N_DEV = 8
MESH = pl.DeviceIdType.MESH
LANES = 128

BIG = (("w_in", D_MODEL, IN_WIDTH, 1),
       ("w_q_up", Q_LORA_RANK, N_ATTN_HEADS * QK_DIM, 1),
       ("w_kv_up", KV_LORA_RANK, N_ATTN_HEADS * (QK_NOPE_DIM + V_HEAD_DIM), 1),
       ("w_out", ATTN_WIDTH + D_INNER, D_MODEL, 0), ("w_up", D_MODEL, 2 * D_FF, 1),
       ("w_down", D_FF, D_MODEL, 0))
PACK_COLS = 1024
GATHER_NOW, GATHER_LATE = ("w_in", "w_q_up", "w_kv_up"), ("w_out", "w_up", "w_down")
REDUCE_EARLY, REDUCE_LAST = ("w_up", "w_down", "w_out"), ("w_in", "w_q_up", "w_kv_up")


def _cparams(sem):
    return pltpu.CompilerParams(dimension_semantics=sem, vmem_limit_bytes=VMEM_LIMIT_BYTES)


def _pick(n, cands):
    for c in cands:
        if n % c == 0:
            return c
    return n


def _sigmoid(x):
    return 0.5 * (jnp.tanh(0.5 * x) + 1.0)


def _silu(x):
    return x * _sigmoid(x)


@jax.custom_vjp
def _softplus(x):
    u = jnp.exp(-jnp.abs(x))
    w = 1.0 + u
    log1p = jnp.where(w == 1.0, u, jnp.log(w) * (u / jnp.where(w == 1.0, 1.0, w - 1.0)))
    return jnp.maximum(x, 0.0) + log1p


def _softplus_fwd(x):
    return _softplus(x), x


def _softplus_bwd(x, g):
    return (g * _sigmoid(x),)


_softplus.defvjp(_softplus_fwd, _softplus_bwd)


@jax.custom_vjp
def _gelu(x):
    return 0.5 * x * (1.0 + lax.erf(x * (2.0 ** -0.5)))


def _gelu_fwd(x):
    return _gelu(x), x


def _gelu_bwd(x, g):
    cdf = 0.5 * (1.0 + lax.erf(x * (2.0 ** -0.5)))
    pdf = jnp.exp(-0.5 * x * x) * (1.0 / math.sqrt(2.0 * math.pi))
    return (g * (cdf + x * pdf),)


_gelu.defvjp(_gelu_fwd, _gelu_bwd)


def _rms(x, w):
    return x * lax.rsqrt(jnp.mean(x * x, axis=-1, keepdims=True) + EPS) * w


def _shift_rows_raw(x, off, seg):
    n = x.shape[0]
    if off == 0:
        return x
    r = pltpu.roll(x, (-off) % n, 0)
    idx = lax.broadcasted_iota(jnp.int32, x.shape, 0)
    src = idx + off
    ok = (src >= 0) & (src < n)
    if seg:
        ok = ok & ((idx < seg) == (src < seg))
    return jnp.where(ok, r, 0.0)


@functools.partial(jax.custom_vjp, nondiff_argnums=(1, 2))
def _shift_rows(x, off, seg):
    return _shift_rows_raw(x, off, seg)


def _shift_rows_fwd(x, off, seg):
    return _shift_rows_raw(x, off, seg), None


def _shift_rows_bwd(off, seg, _, g):
    return (_shift_rows_raw(g, -off, seg),)


_shift_rows.defvjp(_shift_rows_fwd, _shift_rows_bwd)


@functools.partial(jax.custom_vjp, nondiff_argnums=(1,))
def _roll_lanes(x, shift):
    return pltpu.roll(x, shift % x.shape[1], 1)


def _roll_lanes_fwd(x, shift):
    return _roll_lanes(x, shift), None


def _roll_lanes_bwd(shift, _, g):
    return (pltpu.roll(g, (-shift) % g.shape[1], 1),)


_roll_lanes.defvjp(_roll_lanes_fwd, _roll_lanes_bwd)


def _row_of(w, k):
    sel = lax.broadcasted_iota(jnp.int32, (w.shape[0], 1), 0) == k
    return jnp.sum(jnp.where(sel, w, 0.0), axis=0, keepdims=True)


def _col_of(w, k):
    sel = lax.broadcasted_iota(jnp.int32, (1, w.shape[1]), 1) == k
    return jnp.sum(jnp.where(sel, w, 0.0), axis=1, keepdims=True)


def _dwconv(x, w, seg):
    k = w.shape[0]
    acc = None
    for t in range(k):
        term = _shift_rows(x, t - k // 2, seg) * _row_of(w, t)
        acc = term if acc is None else acc + term
    return acc


def _dot(a, b, dims):
    return lax.dot_general(a.astype(MXU_DTYPE), b.astype(MXU_DTYPE), (dims, ((), ())),
                           preferred_element_type=F32)


def _dot_exact(a, b):
    return lax.dot_general(a, b, (((1,), (0,)), ((), ())), precision=HIGHEST,
                           preferred_element_type=F32)


def _mask_dot_raw(mask, x, mask_left):
    hi = x.astype(jnp.bfloat16)
    rest = x - hi.astype(F32)
    mid = rest.astype(jnp.bfloat16)
    low = (rest - mid.astype(F32)).astype(jnp.bfloat16)
    m = mask.astype(jnp.bfloat16)
    acc = None
    for piece in (hi, mid, low):
        term = (lax.dot_general(m, piece, (((1,), (0,)), ((), ())), preferred_element_type=F32) if mask_left
                else lax.dot_general(piece, m, (((1,), (0,)), ((), ())), preferred_element_type=F32))
        acc = term if acc is None else acc + term
    return acc


@functools.partial(jax.custom_vjp, nondiff_argnums=(3,))
def _mask_dot(mask, mask_t, x, mask_left):
    return _mask_dot_raw(mask, x, mask_left)


def _mask_dot_fwd(mask, mask_t, x, mask_left):
    return _mask_dot_raw(mask, x, mask_left), (mask, mask_t)


def _mask_dot_bwd(mask_left, res, g):
    mask, mask_t = res
    return jnp.zeros_like(mask), jnp.zeros_like(mask_t), _mask_dot_raw(mask_t, g, mask_left)


_mask_dot.defvjp(_mask_dot_fwd, _mask_dot_bwd)


def _mm(a, b, *, ta=False, tb=False, out_dtype=F32, name):
    if ta:
        kdim, m = a.shape
    else:
        m, kdim = a.shape
    if tb:
        n, k2 = b.shape
    else:
        k2, n = b.shape
    assert kdim == k2, (a.shape, b.shape, ta, tb)
    tm = _pick(m, (1024, 1408, 512, 384, 256, 128))
    tn = _pick(n, (512, 1408, 384, 256, 128))
    tk = kdim if kdim <= 2048 else _pick(kdim, (2048, 1664, 1536, 1408, 1024, 512, 256, 128))
    nk = kdim // tk
    a_spec = pl.BlockSpec((tk, tm), lambda i, j, k: (k, i)) if ta else pl.BlockSpec((tm, tk), lambda i, j, k: (i, k))
    b_spec = pl.BlockSpec((tn, tk), lambda i, j, k: (j, k)) if tb else pl.BlockSpec((tk, tn), lambda i, j, k: (k, j))
    dims = ((0,) if ta else (1,), (1,) if tb else (0,))

    def body(a_ref, b_ref, o_ref, *scratch):
        if nk == 1:
            o_ref[...] = _dot(a_ref[...], b_ref[...], dims).astype(o_ref.dtype)
            return
        acc_ref, = scratch
        k = pl.program_id(2)

        @pl.when(k == 0)
        def _():
            acc_ref[...] = jnp.zeros_like(acc_ref)

        acc_ref[...] += _dot(a_ref[...], b_ref[...], dims)

        @pl.when(k == nk - 1)
        def _():
            o_ref[...] = acc_ref[...].astype(o_ref.dtype)

    return pl.pallas_call(
        body, name=name, grid=(m // tm, n // tn, nk),
        in_specs=[a_spec, b_spec], out_specs=pl.BlockSpec((tm, tn), lambda i, j, k: (i, j)),
        out_shape=jax.ShapeDtypeStruct((m, n), out_dtype),
        scratch_shapes=[pltpu.VMEM((tm, tn), F32)] if nk > 1 else [],
        compiler_params=_cparams(("parallel", "parallel", "arbitrary")),
    )(a, b)


def _mm_sum(pairs, *, tb=False, out_dtype=F32, name):
    m = pairs[0][0].shape[0]
    n = pairs[0][1].shape[0] if tb else pairs[0][1].shape[1]
    tm = _pick(m, (1024, 1408, 512, 384, 256, 128))
    tn = _pick(n, (512, 1408, 384, 256, 128))
    specs, args = [], []
    for a, b in pairs:
        kdim = a.shape[1]
        specs.append(pl.BlockSpec((tm, kdim), lambda i, j: (i, 0)))
        specs.append(pl.BlockSpec((tn, kdim), lambda i, j: (j, 0)) if tb else pl.BlockSpec((kdim, tn), lambda i, j: (0, j)))
        args += [a, b]
    dims = ((1,), (1,) if tb else (0,))

    def body(*refs):
        acc = None
        for t in range(len(pairs)):
            term = _dot(refs[2 * t][...], refs[2 * t + 1][...], dims)
            acc = term if acc is None else acc + term
        refs[-1][...] = acc.astype(refs[-1].dtype)

    return pl.pallas_call(
        body, name=name, grid=(m // tm, n // tn), in_specs=specs,
        out_specs=pl.BlockSpec((tm, tn), lambda i, j: (i, j)), out_shape=jax.ShapeDtypeStruct((m, n), out_dtype),
        compiler_params=_cparams(("parallel", "parallel")),
    )(*args)


def _row_specs(toks, poss, vecs, bvecs, tl, nctx, nb):
    specs, args = [], []
    for arr, off, cw, ci in toks:
        cw = arr.shape[2] if cw is None else cw
        specs.append(pl.BlockSpec((1, tl, cw), lambda b, l, off=off, ci=ci: (b, l + off, ci)))
        args.append(arr)
    for arr in poss:
        specs.append(pl.BlockSpec((1, tl, arr.shape[2]), lambda b, l: (0, l, 0)))
        args.append(arr)
    for arr in vecs:
        specs.append(pl.BlockSpec(arr.shape, lambda b, l: (0, 0)))
        args.append(arr)
    for arr in bvecs:
        if nctx:
            specs.append(pl.BlockSpec((1, 1, arr.shape[2]), lambda b, l: (jnp.where(l < nctx, nb, b), 0, 0)))
        else:
            specs.append(pl.BlockSpec((1, 1, arr.shape[2]), lambda b, l: (b, 0, 0)))
        args.append(arr)
    return specs, args


def _row_fwd(fn, *, toks, poss=(), vecs=(), bvecs=(), outs, nb, nl, tl, nctx=0, name):
    nt, npos, nv, nbv = len(toks), len(poss), len(vecs), len(bvecs)
    specs, args = _row_specs(toks, poss, vecs, bvecs, tl, nctx, nb)

    def body(*refs):
        ins, os = refs[:len(specs)], refs[len(specs):]
        tv = [r[0].astype(F32) for r in ins[:nt]]
        pv = [r[0] for r in ins[nt:nt + npos]]
        vv = [r[...] for r in ins[nt + npos:nt + npos + nv]]
        bv = [r[0] for r in ins[nt + npos + nv:]]
        res = fn(*tv, *pv, *vv, *bv)
        for o, r in zip(os, res):
            o[0] = r.astype(o.dtype)

    return pl.pallas_call(
        body, name=name, grid=(nb, nl // tl), in_specs=specs,
        out_specs=[pl.BlockSpec((1, tl, c), lambda b, l: (b, l, 0)) for c, _ in outs],
        out_shape=[jax.ShapeDtypeStruct((nb, nl, c), dt) for c, dt in outs],
        compiler_params=_cparams(("parallel", "parallel")),
    )(*args)


def _row_bwd(fn, *, toks, poss=(), vecs=(), bvecs=(), cots, tok_grads, emit=(), nb, nl, tl, nctx=0, name,
             drop_blocks=0):
    nt, npos, nv, nbv = len(toks), len(poss), len(vecs), len(bvecs)
    specs, args = _row_specs(toks, poss, vecs, bvecs, tl, nctx, nb)
    n_in = len(specs)
    cot_slots = []
    for arr, off in cots:
        if arr is None:
            cot_slots.append(None)
            continue
        cot_slots.append((len(specs), off))
        specs.append(pl.BlockSpec((1, tl, arr.shape[2]), lambda b, l, off=off: (b, jnp.maximum(l + off, 0), 0)))
        args.append(arr)
    n_all_in = len(specs)

    out_specs, out_shapes = [], []
    tok_out = []
    for (arr, off, cw, ci), dt in zip(toks, tok_grads):
        if dt is None:
            tok_out.append(None)
            continue
        cw = arr.shape[2] if cw is None else cw
        tok_out.append(len(out_specs))
        out_specs.append(pl.BlockSpec((1, tl, cw), lambda b, l: (b, jnp.maximum(l - drop_blocks, 0), 0)))
        out_shapes.append(jax.ShapeDtypeStruct((nb, nl - drop_blocks * tl, cw), dt))
    vec_out = []
    for arr in vecs:
        vec_out.append(len(out_specs))
        out_specs.append(pl.BlockSpec(arr.shape, lambda b, l: (0, 0)))
        out_shapes.append(jax.ShapeDtypeStruct(arr.shape, F32))
    bv_out = []
    for arr in bvecs:
        c = arr.shape[2]
        lat = len(out_specs)
        out_specs.append(pl.BlockSpec((1, 1, c), lambda b, l: (b, 0, 0)))
        out_shapes.append(jax.ShapeDtypeStruct((nb, 1, c), F32))
        ctx = None
        if nctx:
            ctx = len(out_specs)
            out_specs.append(pl.BlockSpec((1, 1, c), lambda b, l: (0, 0, 0)))
            out_shapes.append(jax.ShapeDtypeStruct((1, 1, c), F32))
        bv_out.append((lat, ctx))
    emit_out = []
    emit_cols = {}
    for idx, c, dt in emit:
        emit_out.append((idx, len(out_specs)))
        out_specs.append(pl.BlockSpec((1, tl, c), lambda b, l: (b, l, 0)))
        out_shapes.append(jax.ShapeDtypeStruct((nb, nl, c), dt))

    def body(*refs):
        ins, os = refs[:n_all_in], refs[n_all_in:]
        b, l = pl.program_id(0), pl.program_id(1)
        tv = [r[0].astype(F32) for r in ins[:nt]]
        pv = [r[0] for r in ins[nt:nt + npos]]
        vv = [r[...] for r in ins[nt + npos:nt + npos + nv]]
        bv = [r[0] for r in ins[nt + npos + nv:n_in]]

        def f(*d):
            return tuple(fn(*d[:nt], *pv, *d[nt:]))

        res, vjp = jax.vjp(f, *tv, *vv, *bv)
        cts = []
        for r, slot in zip(res, cot_slots):
            if slot is None:
                cts.append(jnp.zeros_like(r))
            else:
                i, off = slot
                ct = ins[i][0].astype(F32)
                if off < 0:
                    ct = jnp.where(l + off >= 0, ct, 0.0)
                cts.append(ct)
        grads = vjp(tuple(cts))

        for g, slot in zip(grads[:nt], tok_out):
            if slot is not None:
                os[slot][0] = g.astype(os[slot].dtype)

        @pl.when((b == 0) & (l == 0))
        def _():
            for slot in vec_out:
                os[slot][...] = jnp.zeros_like(os[slot])
            for _, ctx in bv_out:
                if ctx is not None:
                    os[ctx][...] = jnp.zeros_like(os[ctx])

        @pl.when(l == 0)
        def _():
            for lat, _ in bv_out:
                os[lat][...] = jnp.zeros_like(os[lat])

        for g, slot in zip(grads[nt:nt + nv], vec_out):
            os[slot][...] += g
        for g, (lat, ctx) in zip(grads[nt + nv:], bv_out):
            if ctx is None:
                os[lat][0] += g
            else:
                is_ctx = l < nctx
                os[lat][0] += jnp.where(is_ctx, 0.0, g)
                os[ctx][0] += jnp.where(is_ctx, g, 0.0)
        for idx, slot in emit_out:
            os[slot][0] = res[idx].astype(os[slot].dtype)

    out = pl.pallas_call(
        body, name=name, grid=(nb, nl // tl), in_specs=specs, out_specs=out_specs, out_shape=out_shapes,
        compiler_params=_cparams(("arbitrary", "arbitrary")),
    )(*args)
    tg = [None if s is None else out[s] for s in tok_out]
    vg = [out[s] for s in vec_out]
    bg = [(out[lat], None if ctx is None else out[ctx]) for lat, ctx in bv_out]
    em = [out[s] for _, s in emit_out]
    return tg, vg, bg, em


def _seq_specs(toks, vecs, nl, cb):
    specs, args = [], []
    for arr, off, mult in toks:
        specs.append(pl.BlockSpec((1, nl, cb * mult), lambda j, b, off=off: (b, 0, j + off)))
        args.append(arr)
    for arr, off in vecs:
        specs.append(pl.BlockSpec((arr.shape[0], cb), lambda j, b, off=off: (0, j + off)))
        args.append(arr)
    return specs, args


def _seq_fwd(fn, *, toks, vecs, outs, nb, nl, nc, cb, name):
    nt = len(toks)
    specs, args = _seq_specs(toks, vecs, nl, cb)

    def body(*refs):
        ins, os = refs[:len(specs)], refs[len(specs):]
        tv = [r[0].astype(F32) for r in ins[:nt]]
        vv = [r[...] for r in ins[nt:]]
        for o, r in zip(os, fn(*tv, *vv)):
            o[0] = r.astype(o.dtype)

    return pl.pallas_call(
        body, name=name, grid=(nc // cb, nb), in_specs=specs,
        out_specs=[pl.BlockSpec((1, nl, cb), lambda j, b: (b, 0, j)) for _ in outs],
        out_shape=[jax.ShapeDtypeStruct((nb, nl, nc), dt) for dt in outs],
        compiler_params=_cparams(("parallel", "parallel")),
    )(*args)


def _seq_bwd(fn, *, toks, vecs, cots, tok_grads, nb, nl, nc, cb, name):
    nt, nv = len(toks), len(vecs)
    specs, args = _seq_specs(toks, vecs, nl, cb)
    n_in = len(specs)
    cot_counts = [len(group) for group in cots]
    for group in cots:
        for arr in group:
            specs.append(pl.BlockSpec((1, nl, cb), lambda j, b: (b, 0, j)))
            args.append(arr)
    out_specs, out_shapes = [], []
    for (_, _, mult), dt in zip(toks, tok_grads):
        out_specs.append(pl.BlockSpec((1, nl, cb * mult), lambda j, b: (b, 0, j)))
        out_shapes.append(jax.ShapeDtypeStruct((nb, nl, nc * mult), dt))
    for arr, _ in vecs:
        out_specs.append(pl.BlockSpec((arr.shape[0], cb), lambda j, b: (0, j)))
        out_shapes.append(jax.ShapeDtypeStruct((arr.shape[0], nc), F32))

    def body(*refs):
        ins, os = refs[:len(specs)], refs[len(specs):]
        b = pl.program_id(1)
        tv = [r[0].astype(F32) for r in ins[:nt]]
        vv = [r[...] for r in ins[nt:n_in]]
        _, vjp = jax.vjp(lambda *d: tuple(fn(*d)), *tv, *vv)
        cts, o = [], n_in
        for cnt in cot_counts:
            ct = ins[o][0].astype(F32)
            for r in ins[o + 1:o + cnt]:
                ct = ct + r[0].astype(F32)
            cts.append(ct)
            o += cnt
        grads = vjp(tuple(cts))
        for g, o in zip(grads[:nt], os[:nt]):
            o[0] = g.astype(o.dtype)

        @pl.when(b == 0)
        def _():
            for o in os[nt:]:
                o[...] = jnp.zeros_like(o)

        for g, o in zip(grads[nt:], os[nt:]):
            o[...] += g

    out = pl.pallas_call(
        body, name=name, grid=(nc // cb, nb), in_specs=specs, out_specs=out_specs, out_shape=out_shapes,
        compiler_params=_cparams(("parallel", "arbitrary")),
    )(*args)
    return out[:nt], out[nt:]


EXP2_SCALE = ATTN_SCALE * math.log2(math.e)


HEAD_TILE = 128
N_HEAD_PAIRS = N_ATTN_HEADS // 2


def _head_lanes():
    lane = lax.broadcasted_iota(jnp.int32, (1, HEAD_TILE), 1)
    return lane < QK_NOPE_DIM, (lane >= QK_NOPE_DIM) & (lane < QK_DIM)


def _attn_specs(tq, lk):
    q = pl.BlockSpec((1, tq, 2 * HEAD_TILE), lambda b, pr, j: (b, j, pr))
    kv = pl.BlockSpec((1, lk, 2 * HEAD_TILE), lambda b, pr, j: (b, 0, pr))
    kr = pl.BlockSpec((1, lk, HEAD_TILE), lambda b, pr, j: (b, 0, 0))
    o = pl.BlockSpec((1, tq, HEAD_TILE), lambda b, pr, j: (b, j, pr))
    lse = pl.BlockSpec((1, 2, tq, 1), lambda b, pr, j: (b, pr, j, 0))
    return q, kv, kr, o, lse


def _grid_marks(nb, nj):
    b, pr, j = pl.program_id(0), pl.program_id(1), pl.program_id(2)
    first = (b == 0) & (pr == 0) & (j == 0)
    middle = (b == nb // 2) & (pr == 0) & (j == 0)
    last = (b == nb - 1) & (pr == N_HEAD_PAIRS - 1) & (j == nj - 1)
    return first, middle, last


def _attn_fwd(q, kv, kr, late_shard, *, tq, name):
    nb, s, _ = q.shape
    lk = kv.shape[1]
    nj = s // tq
    qs, kvs, krs, os_, lses = _attn_specs(tq, lk)

    def body(q_ref, kv_ref, kr_ref, shard_ref, o_ref, lse_ref, gathered_ref, send_sems, recv_sems):
        start, relay, finish = _gather_stage(shard_ref, gathered_ref, send_sems, recv_sems)
        first, middle, last = _grid_marks(nb, nj)
        pl.when(first)(start)
        pl.when(middle)(relay)
        low, _ = _head_lanes()
        outs = []
        for e in range(2):
            tile = pl.ds(HEAD_TILE * e, HEAD_TILE)
            kv_e = kv_ref[0, :, tile]
            keys = jnp.where(low, kv_e, kr_ref[0])
            sc = _dot(q_ref[0, :, tile], keys, ((1,), (1,)))
            m = jnp.max(sc, axis=-1, keepdims=True)
            p = jnp.exp2((sc - m) * EXP2_SCALE)
            denom = jnp.sum(p, axis=-1, keepdims=True)
            outs.append(_dot(p, kv_e, ((1,), (0,))) / denom)
            lse_ref[0, e] = m * EXP2_SCALE + jnp.log2(denom)
        o_ref[0] = jnp.where(low, pltpu.roll(outs[0], V_HEAD_DIM, 1), outs[1])
        pl.when(last)(finish)

    return pl.pallas_call(
        body, name=name, grid=(nb, N_HEAD_PAIRS, nj), in_specs=[qs, kvs, krs, ANY], out_specs=[os_, lses, ANY],
        out_shape=[jax.ShapeDtypeStruct((nb, s, ATTN_WIDTH), F32), jax.ShapeDtypeStruct((nb, N_ATTN_HEADS, s, 1), F32),
                   jax.ShapeDtypeStruct((N_CHIPS,) + late_shard.shape, late_shard.dtype)],
        scratch_shapes=GATHER_SEMS,
        compiler_params=_cparams(("arbitrary", "arbitrary", "arbitrary")),
    )(q, kv, kr, late_shard)


def _attn_bwd(q, kv, kr, o, lse, do, early_sums, *, tq, name):
    nb, s, _ = q.shape
    lk = kv.shape[1]
    nj = s // tq
    qs, kvs, krs, os_, lses = _attn_specs(tq, lk)

    def body(q_ref, kv_ref, kr_ref, o_ref, lse_ref, do_ref, sums_ref, dq_ref, dkv_ref, dkr_ref, scattered_ref,
             send_sems, recv_sems):
        start, finish = _scatter_stage(sums_ref, scattered_ref, send_sems, recv_sems)
        first, _, last = _grid_marks(nb, nj)
        pl.when(first)(start)
        pr, j = pl.program_id(1), pl.program_id(2)
        low, rope = _head_lanes()
        do_pair = do_ref[0]
        prod = do_pair * o_ref[0]

        @pl.when(j == 0)
        def _():
            dkv_ref[...] = jnp.zeros_like(dkv_ref)

        @pl.when((pr == 0) & (j == 0))
        def _():
            dkr_ref[...] = jnp.zeros_like(dkr_ref)

        dkr = None
        for e in range(2):
            tile = pl.ds(HEAD_TILE * e, HEAD_TILE)
            delta = jnp.sum(jnp.where(low if e == 0 else ~low, prod, 0.0), axis=-1, keepdims=True)
            do_e = jnp.where(low, 0.0, do_pair if e == 1 else pltpu.roll(do_pair, V_HEAD_DIM, 1))
            kv_e, q_e = kv_ref[0, :, tile], q_ref[0, :, tile]
            keys = jnp.where(low, kv_e, kr_ref[0])
            sc = _dot(q_e, keys, ((1,), (1,)))
            p = jnp.exp2(sc * EXP2_SCALE - lse_ref[0, e])
            dp = _dot(do_e, kv_e, ((1,), (1,)))
            ds = (p * (dp - delta)).astype(MXU_DTYPE)
            dq_ref[0, :, tile] = _dot(ds, keys, ((1,), (0,))) * ATTN_SCALE
            dkeys = _dot(ds, q_e, ((0,), (0,)))
            dv = _dot(p, do_e, ((0,), (0,)))
            dkv_ref[0, :, tile] += jnp.where(low, dkeys, dv)
            part = jnp.where(rope, dkeys, 0.0)
            dkr = part if dkr is None else dkr + part
        dkr_ref[0] += dkr

        @pl.when(j == nj - 1)
        def _():
            for e in range(2):
                tile = pl.ds(HEAD_TILE * e, HEAD_TILE)
                dkv_ref[0, :, tile] = dkv_ref[0, :, tile] * jnp.where(low, ATTN_SCALE, 1.0)

        @pl.when((pr == N_HEAD_PAIRS - 1) & (j == nj - 1))
        def _():
            dkr_ref[0] = dkr_ref[0] * ATTN_SCALE

        pl.when(last)(finish)

    return pl.pallas_call(
        body, name=name, grid=(nb, N_HEAD_PAIRS, nj), in_specs=[qs, kvs, krs, os_, lses, os_, ANY],
        out_specs=[qs, kvs, krs, ANY],
        out_shape=[jax.ShapeDtypeStruct(q.shape, F32), jax.ShapeDtypeStruct(kv.shape, F32),
                   jax.ShapeDtypeStruct(kr.shape, F32), jax.ShapeDtypeStruct(early_sums.shape, early_sums.dtype)],
        scratch_shapes=SCATTER_SEMS,
        compiler_params=_cparams(("arbitrary", "arbitrary", "arbitrary")),
    )(q, kv, kr, o, lse, do, early_sums)


N_PAIRS = HEADS_PER_GROUP // 2
PAIR_W = 2 * SSD_HEAD_DIM


def _ssd_chunk(states, xs, dtc, dtr, bm, cm, ac, ar, *, reverse):
    q = dtc.shape[0]
    assert q == PAIR_W == dtr.shape[1]
    row = lax.broadcasted_iota(jnp.int32, (q, q), 0)
    col = lax.broadcasted_iota(jnp.int32, (q, q), 1)
    if reverse:
        tri_c, tri_r, mask = col < row, row < col, col >= row
    else:
        tri_c, tri_r, mask = col <= row, row <= col, col <= row
    a_col, a_row = dtc * ac, dtr * ar
    tri_c, tri_r = tri_c.astype(F32), tri_r.astype(F32)
    cum_c = _mask_dot(tri_c, tri_r, a_col, True)
    cum_r = _mask_dot(tri_r, tri_c, a_row, False)
    tot_c = jnp.sum(a_col, axis=0, keepdims=True)
    tot_r = jnp.sum(a_row, axis=1, keepdims=True)
    cb = _dot(cm, bm, ((1,), (1,)))
    first = lax.broadcasted_iota(jnp.int32, (1, PAIR_W), 1) < SSD_HEAD_DIM
    first_rows = lax.broadcasted_iota(jnp.int32, (PAIR_W, 1), 0) < SSD_HEAD_DIM
    heads, pairs = range(HEADS_PER_GROUP), range(N_PAIRS)
    tile = [slice(PAIR_W * pr, PAIR_W * (pr + 1)) for pr in pairs]
    cum_p = [cum_c[:, tile[pr]] for pr in pairs]
    swapped = [_roll_lanes(cum_p[pr], SSD_HEAD_DIM) for pr in pairs]
    cc = [jnp.where(first, cum_p[e // 2], swapped[e // 2]) if e % 2 == 0
          else jnp.where(first, swapped[e // 2], cum_p[e // 2]) for e in heads]
    cr = [_row_of(cum_r, e) for e in heads]
    if reverse:
        within = [jnp.exp(jnp.where(mask, cr[e] - cc[e], -jnp.inf)) for e in heads]
        into = [jnp.exp(tot_c[:, tile[pr]] - cum_p[pr]) for pr in pairs]
        to_end = [jnp.exp(cum_p[pr]) for pr in pairs]
    else:
        within = [jnp.exp(jnp.where(mask, cc[e] - cr[e], -jnp.inf)) for e in heads]
        into = [jnp.exp(cum_p[pr]) for pr in pairs]
        to_end = [jnp.exp(tot_c[:, tile[pr]] - cum_p[pr]) for pr in pairs]
    decay = [cb * within[e] for e in heads]
    carry = [jnp.exp(_row_of(tot_r, e)) for e in heads]
    xd = [xs[pr] * dtc[:, tile[pr]] for pr in pairs]
    y_even = [_dot(decay[2 * pr], jnp.where(first, xd[pr], 0.0), ((1,), (0,))) for pr in pairs]
    y_odd = [_dot(decay[2 * pr + 1], jnp.where(first, 0.0, xd[pr]), ((1,), (0,))) for pr in pairs]
    y_off = [_dot(cm, states[pr], ((1,), (1,))) for pr in pairs]
    grow = [_dot(xd[pr] * to_end[pr], bm, ((0,), (0,))) for pr in pairs]
    ys = [y_even[pr] + y_odd[pr] + y_off[pr] * into[pr] for pr in pairs]
    new_states = [states[pr] * jnp.where(first_rows, carry[2 * pr], carry[2 * pr + 1]) + grow[pr] for pr in pairs]
    return tuple(ys) + tuple(new_states)


def _chunk_of_step(t, ncc, nch, reverse):
    if not reverse:
        return t
    return jnp.where(t < ncc, ncc - 1 - t, nch - 1 - (t - ncc))


X_COLS = D_INNER // SSD_GROUPS
GROUP_COLS = X_COLS + 2 * SSD_STATE


def _scan_in_specs(nch, ncc, reverse, back):
    q, e = SSD_CHUNK, HEADS_PER_GROUP

    def ch(t):
        return _chunk_of_step((nch - 1 - t) if back else t, ncc, nch, reverse)

    return ch, [
        pl.BlockSpec((1, q, GROUP_COLS), lambda b, g, t: (b, ch(t), g)),
        pl.BlockSpec((1, q, X_COLS), lambda b, g, t: (b, ch(t), g)),
        pl.BlockSpec((1, 1, e, q), lambda b, g, t: (b, g, 0, ch(t))),
        pl.BlockSpec((1, 1, X_COLS), lambda b, g, t: (g, 0, 0)),
        pl.BlockSpec((1, e, 1), lambda b, g, t: (g, 0, 0)),
        pl.BlockSpec((1, 1, X_COLS), lambda b, g, t: (g, 0, 0)),
    ]


def _scan_chunk_fn(reverse, skip):
    def f(states, xs, dtc, dtr, bm, cm, ac, ar, d):
        res = _ssd_chunk(states, xs, dtc, dtr, bm, cm, ac, ar, reverse=reverse)
        if not skip:
            return res
        ys = tuple(res[i] + d[:, PAIR_W * i:PAIR_W * (i + 1)] * xs[i] for i in range(N_PAIRS))
        return ys + tuple(res[N_PAIRS:])

    return f


def _scan_operands(x_ref, dtc_ref, dtr_ref, ac_ref, ar_ref, d_ref):
    xs = [x_ref[0, :, pl.ds(PAIR_W * i, PAIR_W)] for i in range(N_PAIRS)]
    bm = x_ref[0, :, pl.ds(X_COLS, SSD_STATE)]
    cm = x_ref[0, :, pl.ds(X_COLS + SSD_STATE, SSD_STATE)]
    return xs, dtc_ref[0], dtr_ref[0, 0], bm, cm, ac_ref[0], ar_ref[0], d_ref[0]


N_IN = 6


def _scan_fwd(dirs, *, ncc, name):
    nb, lt, _ = dirs[0][0].shape
    q, n = SSD_CHUNK, SSD_STATE
    nch = lt // q
    in_specs, out_specs, out_shapes, fs = [], [], [], []
    for dr in range(2):
        ch, specs = _scan_in_specs(nch, ncc, bool(dr), False)
        in_specs += specs
        fs.append(_scan_chunk_fn(bool(dr), dr == 0))
        out_specs += [pl.BlockSpec((1, q, X_COLS), lambda b, g, t, ch=ch: (b, ch(t), g)),
                      pl.BlockSpec((1, 1, 1, N_PAIRS, PAIR_W, n), lambda b, g, t: (b, g, t, 0, 0, 0))]
        out_shapes += [jax.ShapeDtypeStruct((nb, lt, D_INNER), F32),
                       jax.ShapeDtypeStruct((nb, SSD_GROUPS, nch, N_PAIRS, PAIR_W, n), F32)]

    def body(*refs):
        ins, outs, sts = refs[:2 * N_IN], refs[2 * N_IN:2 * N_IN + 4], refs[2 * N_IN + 4:]
        t = pl.program_id(2)

        @pl.when(t == 0)
        def _():
            for st_ref in sts:
                st_ref[...] = jnp.zeros_like(st_ref)

        entering = [[sts[dr][i] for i in range(N_PAIRS)] for dr in range(2)]
        results = [fs[dr](entering[dr], *_scan_operands(*ins[N_IN * dr:N_IN * (dr + 1)])) for dr in range(2)]
        for dr in range(2):
            (y_ref, ent_ref), st_ref = outs[2 * dr:2 * dr + 2], sts[dr]
            for i in range(N_PAIRS):
                ent_ref[0, 0, 0, i] = entering[dr][i]
                y_ref[0, :, pl.ds(PAIR_W * i, PAIR_W)] = results[dr][i]
                st_ref[i] = results[dr][N_PAIRS + i]

    out = pl.pallas_call(
        body, name=name, grid=(nb, SSD_GROUPS, nch), in_specs=in_specs, out_specs=out_specs, out_shape=out_shapes,
        scratch_shapes=[pltpu.VMEM((N_PAIRS, PAIR_W, n), F32)] * 2,
        compiler_params=_cparams(("parallel", "parallel", "arbitrary")),
    )(*dirs[0], *dirs[1])
    return out[:2], out[2:]


N_SCAN_GRADS = 6


def _scan_bwd(dirs, entering, dy, *, ncc, name):
    nb, lt, _ = dirs[0][0].shape
    q, n, e = SSD_CHUNK, SSD_STATE, HEADS_PER_GROUP
    nch = lt // q
    in_specs, out_specs, out_shapes, fs, args = [], [], [], [], []
    for dr in range(2):
        ch, specs = _scan_in_specs(nch, ncc, bool(dr), True)
        in_specs += specs + [
            pl.BlockSpec((1, 1, 1, N_PAIRS, PAIR_W, n), lambda b, g, t: (b, g, nch - 1 - t, 0, 0, 0)),
            pl.BlockSpec((1, q, X_COLS), lambda b, g, t, ch=ch: (b, ch(t), g))]
        args += list(dirs[dr]) + [entering[dr], dy]
        fs.append(_scan_chunk_fn(bool(dr), dr == 0))
        out_specs += [pl.BlockSpec((1, q, GROUP_COLS), lambda b, g, t, ch=ch: (b, ch(t), g)),
                      pl.BlockSpec((1, q, X_COLS), lambda b, g, t, ch=ch: (b, ch(t), g)),
                      pl.BlockSpec((1, 1, e, q), lambda b, g, t, ch=ch: (b, g, 0, ch(t))),
                      pl.BlockSpec((1, 1, 1, X_COLS), lambda b, g, t: (b, g, 0, 0)),
                      pl.BlockSpec((1, 1, e, 1), lambda b, g, t: (b, g, 0, 0)),
                      pl.BlockSpec((1, 1, 1, X_COLS), lambda b, g, t: (b, g, 0, 0))]
        out_shapes += [jax.ShapeDtypeStruct((nb, lt, SSD_GROUPS * GROUP_COLS), F32),
                       jax.ShapeDtypeStruct((nb, lt, D_INNER), F32), jax.ShapeDtypeStruct((nb, SSD_GROUPS, e, lt), F32),
                       jax.ShapeDtypeStruct((nb, SSD_GROUPS, 1, X_COLS), F32), jax.ShapeDtypeStruct((nb, SSD_GROUPS, e, 1), F32),
                       jax.ShapeDtypeStruct((nb, SSD_GROUPS, 1, X_COLS), F32)]
    n_in = N_IN + 2

    def body(*refs):
        ins = refs[:2 * n_in]
        outs = refs[2 * n_in:2 * n_in + 2 * N_SCAN_GRADS]
        dss = refs[2 * n_in + 2 * N_SCAN_GRADS:]
        t = pl.program_id(2)

        for dr in range(2):
            mine = ins[n_in * dr:n_in * (dr + 1)]
            ent_ref, dy_ref = mine[N_IN], mine[N_IN + 1]
            dx_ref, ddtc_ref, ddtr_ref, dac_ref, dar_ref, dd_ref = outs[N_SCAN_GRADS * dr:N_SCAN_GRADS * (dr + 1)]
            ds_ref = dss[dr]

            @pl.when(t == 0)
            def _():
                for ref in (ds_ref, dac_ref, dar_ref, dd_ref):
                    ref[...] = jnp.zeros_like(ref)

            states = [ent_ref[0, 0, 0, i] for i in range(N_PAIRS)]
            _, vjp = jax.vjp(fs[dr], states, *_scan_operands(*mine[:N_IN]))
            dys = [dy_ref[0, :, pl.ds(PAIR_W * i, PAIR_W)] for i in range(N_PAIRS)]
            gs, gx, gdtc, gdtr, gb, gc, gac, gar, gd = vjp(tuple(dys) + tuple(ds_ref[i] for i in range(N_PAIRS)))
            o = 0
            for part in list(gx) + [gb, gc]:
                dx_ref[0, :, pl.ds(o, part.shape[1])] = part
                o += part.shape[1]
            for i in range(N_PAIRS):
                ds_ref[i] = gs[i]
            ddtc_ref[0] = gdtc
            ddtr_ref[0, 0] = gdtr
            dac_ref[0, 0] += gac
            dar_ref[0, 0] += gar
            dd_ref[0, 0] += gd

    out = pl.pallas_call(
        body, name=name, grid=(nb, SSD_GROUPS, nch), in_specs=in_specs, out_specs=out_specs, out_shape=out_shapes,
        scratch_shapes=[pltpu.VMEM((N_PAIRS, PAIR_W, n), F32)] * 2,
        compiler_params=_cparams(("parallel", "parallel", "arbitrary")),
    )(*args)
    return out[:N_SCAN_GRADS], out[N_SCAN_GRADS:]


def _adamw(w, g, m, v, *, name):
    r, c = w.shape
    tr = _pick(r, (256, 176, 128, 96, 64, 8))
    c1 = 1.0 / (1.0 - ADAM_B1 ** ADAM_STEP)
    c2 = 1.0 / (1.0 - ADAM_B2 ** ADAM_STEP)

    def body(w_ref, g_ref, m_ref, v_ref, d_ref, nm_ref, nv_ref):
        gv = g_ref[...]
        nm = ADAM_B1 * m_ref[...] + (1.0 - ADAM_B1) * gv
        nv = ADAM_B2 * v_ref[...] + (1.0 - ADAM_B2) * (gv * gv)
        d_ref[...] = -ADAM_LR * ((nm * c1) / (jnp.sqrt(nv * c2) + ADAM_EPS) + ADAM_WD * w_ref[...])
        nm_ref[...] = nm
        nv_ref[...] = nv

    spec = pl.BlockSpec((tr, c), lambda i: (i, 0))
    return pl.pallas_call(
        body, name=name, grid=(r // tr,), in_specs=[spec] * 4, out_specs=[spec] * 3,
        out_shape=[jax.ShapeDtypeStruct((r, c), F32)] * 3, compiler_params=_cparams(("parallel",)),
    )(w, g, m, v)


def _sum_rows_tile(r):
    return r if r <= 1024 else _pick(r, (656, 512, 256, 128, 64, 32, 16))


def _sum_slots(x, *, out_dtype, name):
    n, r, c = x.shape
    tr = _sum_rows_tile(r)

    def body(x_ref, o_ref):
        acc = x_ref[0].astype(F32)
        for k in range(1, n):
            acc = acc + x_ref[k].astype(F32)
        o_ref[...] = acc.astype(o_ref.dtype)

    return pl.pallas_call(
        body, name=name, grid=(r // tr,), in_specs=[pl.BlockSpec((n, tr, c), lambda i: (0, i, 0))],
        out_specs=pl.BlockSpec((tr, c), lambda i: (i, 0)), out_shape=jax.ShapeDtypeStruct((r, c), out_dtype),
        compiler_params=_cparams(("parallel",)),
    )(x)


def _sum_list(xs, *, out_dtype, name):
    r, c = xs[0].shape
    tr = _sum_rows_tile(r)

    def body(*refs):
        acc = refs[0][...].astype(F32)
        for ref in refs[1:-1]:
            acc = acc + ref[...].astype(F32)
        refs[-1][...] = acc.astype(refs[-1].dtype)

    spec = pl.BlockSpec((tr, c), lambda i: (i, 0))
    return pl.pallas_call(
        body, name=name, grid=(r // tr,), in_specs=[spec] * len(xs), out_specs=spec,
        out_shape=jax.ShapeDtypeStruct((r, c), out_dtype), compiler_params=_cparams(("parallel",)),
    )(*xs)


ANY = pl.BlockSpec(memory_space=pl.ANY)


def _place():
    return lax.axis_index("x"), lax.axis_index("y"), lax.axis_index("c")


def _allgather_small(v, *, name):
    r, c = v.shape

    def body(v_ref, out_ref, send_sems, recv_sems, local_sem):
        x, y, cc = _place()
        me = 4 * x + 2 * y + cc
        mine = pltpu.make_async_copy(v_ref, out_ref.at[me], local_sem)
        mine.start()
        copies = []
        for k in range(1, N_DEV):
            fx, fy, fc = (k >> 2) & 1, (k >> 1) & 1, k & 1
            peer = (1 - x if fx else x, 1 - y if fy else y, 1 - cc if fc else cc)
            copies.append(pltpu.make_async_remote_copy(
                src_ref=v_ref, dst_ref=out_ref.at[me], send_sem=send_sems.at[k - 1], recv_sem=recv_sems.at[k - 1],
                device_id=peer, device_id_type=MESH))
        for cp in copies:
            cp.start()
        for cp in copies:
            cp.wait()
        mine.wait()

    return pl.pallas_call(
        body, name=name, in_specs=[ANY], out_specs=ANY, out_shape=jax.ShapeDtypeStruct((N_DEV, r, c), v.dtype),
        scratch_shapes=[pltpu.SemaphoreType.DMA((N_DEV - 1,)), pltpu.SemaphoreType.DMA((N_DEV - 1,)),
                        pltpu.SemaphoreType.DMA],
    )(v)


def _other_chips(x, y):
    return [(1 - x, y), (x, 1 - y), (1 - x, 1 - y)]


GATHER_SEMS = [pltpu.SemaphoreType.DMA((6,)), pltpu.SemaphoreType.DMA((6,))]
SCATTER_SEMS = [pltpu.SemaphoreType.DMA((3,)), pltpu.SemaphoreType.DMA((3,))]


def _gather_stage(v_ref, out_ref, send_sems, recv_sems):
    half = v_ref.shape[0] // 2
    x, y, cc = _place()
    sibling = (x, y, 1 - cc)
    chips = _other_chips(x, y)

    def rows(px, py, pc):
        return out_ref.at[2 * px + py, pl.ds(pc * half, half), :]

    def copy(k, block, to, src=None):
        return pltpu.make_async_remote_copy(
            src_ref=rows(*block) if src is None else src, dst_ref=rows(*block),
            send_sem=send_sems.at[k], recv_sem=recv_sems.at[k], device_id=to, device_id_type=MESH)

    my_half = v_ref.at[pl.ds(cc * half, half), :]
    first = [copy(j, (x, y, cc), (*chip, cc), src=my_half) for j, chip in enumerate(chips)]
    passed = [copy(3 + j, (*chip, cc), sibling) for j, chip in enumerate(chips)]

    def start():
        for cp in first:
            cp.start()

    def relay():
        for j, chip in enumerate(chips):
            copy(j, (*chip, cc), (x, y, cc)).wait_recv()
            passed[j].start()

    def finish():
        for j, chip in enumerate(chips):
            copy(3 + j, (*chip, 1 - cc), (x, y, cc)).wait_recv()
        for cp in first + passed:
            cp.wait_send()

    return start, relay, finish


def _gather_shards(mine, *, name):
    r, c = mine.shape

    def body(v_ref, out_ref, send_sems, recv_sems):
        for phase in _gather_stage(v_ref, out_ref, send_sems, recv_sems):
            phase()

    return pl.pallas_call(
        body, name=name, in_specs=[ANY], out_specs=ANY, out_shape=jax.ShapeDtypeStruct((N_CHIPS, r, c), mine.dtype),
        scratch_shapes=GATHER_SEMS,
    )(mine)


def _swap_halves(g, *, name):
    n, _, r, c = g.shape

    def body(g_ref, got_ref, send_sems, recv_sems):
        x, y, cc = _place()
        sibling = (x, y, 1 - cc)
        rems = []
        for j in range(n):
            rems.append(pltpu.make_async_remote_copy(
                src_ref=g_ref.at[j, 1 - cc], dst_ref=got_ref.at[j], send_sem=send_sems.at[j],
                recv_sem=recv_sems.at[j], device_id=sibling, device_id_type=MESH))
        for cp in rems:
            cp.start()
        for cp in rems:
            cp.wait()

    return pl.pallas_call(
        body, name=name, in_specs=[ANY], out_specs=ANY, out_shape=jax.ShapeDtypeStruct((n, r, c), g.dtype),
        scratch_shapes=[pltpu.SemaphoreType.DMA((n,)), pltpu.SemaphoreType.DMA((n,))],
    )(g)


def _scatter_stage(s_ref, out_ref, send_sems, recv_sems):
    x, y, cc = _place()
    me = 2 * x + y
    copies = [pltpu.make_async_remote_copy(
        src_ref=s_ref.at[2 * px + py], dst_ref=out_ref.at[me], send_sem=send_sems.at[j], recv_sem=recv_sems.at[j],
        device_id=(px, py, cc), device_id_type=MESH) for j, (px, py) in enumerate(_other_chips(x, y))]

    def start():
        for cp in copies:
            cp.start()

    def finish():
        for cp in copies:
            cp.wait()

    return start, finish


def _scatter_to_chips(s, *, name):
    def body(s_ref, out_ref, send_sems, recv_sems):
        for phase in _scatter_stage(s_ref, out_ref, send_sems, recv_sems):
            phase()

    return pl.pallas_call(
        body, name=name, in_specs=[ANY], out_specs=ANY, out_shape=jax.ShapeDtypeStruct(s.shape, s.dtype),
        scratch_shapes=SCATTER_SEMS,
    )(s)


def _join_halves(f, *, name):
    r, c = f.shape

    def body(f_ref, out_ref, send_sem, recv_sem):
        x, y, cc = _place()
        cp = pltpu.make_async_remote_copy(src_ref=f_ref, dst_ref=out_ref.at[cc], send_sem=send_sem, recv_sem=recv_sem,
                                          device_id=(x, y, 1 - cc), device_id_type=MESH)
        cp.start()
        cp.wait()

    return pl.pallas_call(
        body, name=name, in_specs=[ANY], out_specs=ANY, out_shape=jax.ShapeDtypeStruct((2, r, c), f.dtype),
        scratch_shapes=[pltpu.SemaphoreType.DMA, pltpu.SemaphoreType.DMA],
    )(f)


def _pack_rows(parts, width=PACK_COLS):
    return jnp.concatenate([p.reshape(-1, width) for p in parts], axis=0)


def _pack_small(parts, rows):
    flat = jnp.concatenate([p.reshape(-1).astype(F32) for p in parts])
    return jnp.pad(flat, (0, rows * LANES - flat.shape[0])).reshape(rows, LANES)


def _unpack_small(packed, shapes):
    flat = packed.reshape(-1)
    out, o = [], 0
    for shp in shapes:
        n = int(np.prod(shp))
        out.append(flat[o:o + n].reshape(shp))
        o += n
    return out


def _perm_in_cols(w):
    a, b = Q_LORA_RANK + KV_LORA_RANK, Q_LORA_RANK + KV_LORA_RANK + QK_ROPE_DIM
    c = IN_WIDTH - 2 * N_SSD_HEADS
    return jnp.concatenate([w[:, :a], w[:, b:c], w[:, a:b], w[:, c:]], axis=1)


def _unperm_in_cols(w):
    a = Q_LORA_RANK + KV_LORA_RANK
    zx = D_INNER + XBC_WIDTH
    return jnp.concatenate([w[:, :a], w[:, a + zx:a + zx + QK_ROPE_DIM], w[:, a:a + zx], w[:, a + zx + QK_ROPE_DIM:]],
                           axis=1)


def _group_xbc(a):
    n = SSD_STATE
    parts = []
    for g in range(SSD_GROUPS):
        parts += [a[..., g * X_COLS:(g + 1) * X_COLS], a[..., D_INNER + g * n:D_INNER + (g + 1) * n],
                  a[..., D_INNER + GN + g * n:D_INNER + GN + (g + 1) * n]]
    return jnp.concatenate(parts, axis=-1)


def _ungroup_xbc(a):
    n = SSD_STATE
    xs = [a[..., g * GROUP_COLS:g * GROUP_COLS + X_COLS] for g in range(SSD_GROUPS)]
    bs = [a[..., g * GROUP_COLS + X_COLS:g * GROUP_COLS + X_COLS + n] for g in range(SSD_GROUPS)]
    cs = [a[..., g * GROUP_COLS + X_COLS + n:(g + 1) * GROUP_COLS] for g in range(SSD_GROUPS)]
    return jnp.concatenate(xs + bs + cs, axis=-1)


UP_BLOCK = 256


def _interleave_up(w):
    parts = []
    for j in range(D_FF // UP_BLOCK):
        parts += [w[:, j * UP_BLOCK:(j + 1) * UP_BLOCK], w[:, D_FF + j * UP_BLOCK:D_FF + (j + 1) * UP_BLOCK]]
    return jnp.concatenate(parts, axis=1)


def _deinterleave_up(w):
    blocks = [w[:, j * UP_BLOCK:(j + 1) * UP_BLOCK] for j in range(2 * D_FF // UP_BLOCK)]
    return jnp.concatenate(blocks[0::2] + blocks[1::2], axis=1)


def _pad_q_heads(w):
    k = w.shape[0]
    return jnp.pad(w.reshape(k, N_ATTN_HEADS, QK_DIM), ((0, 0), (0, 0), (0, HEAD_TILE - QK_DIM))).reshape(k, -1)


def _unpad_q_heads(w):
    k = w.shape[0]
    return w.reshape(k, N_ATTN_HEADS, HEAD_TILE)[..., :QK_DIM].reshape(k, N_ATTN_HEADS * QK_DIM)


def _rope_tables(seq_len):
    n_rows = seq_len // GRID_W
    row = jnp.repeat(jnp.arange(n_rows), GRID_W).astype(F32)
    col = jnp.tile(jnp.arange(GRID_W), n_rows).astype(F32)
    axis_dim = QK_ROPE_DIM // 2
    inv_freq = ROPE_THETA ** (-jnp.arange(0, axis_dim, 2, dtype=F32) / axis_dim)
    ang_r = row[:, None] * inv_freq
    ang_c = col[:, None] * inv_freq
    ang = jnp.concatenate([ang_r, ang_r, ang_c, ang_c], axis=-1)
    return jnp.cos(ang), jnp.sin(ang)


def _rot_matrix(width, start):
    r = np.zeros((width, width), np.float32)
    quarter = QK_ROPE_DIM // 4
    for base in (0, QK_ROPE_DIM // 2):
        for i in range(quarter):
            r[start + base + quarter + i, start + base + i] = -1.0
            r[start + base + i, start + base + quarter + i] = 1.0
    return jnp.asarray(r)


ROPE_STEP = QK_ROPE_DIM // 4


def _rope_flat_fn(x, cos, sin_up, sin_down):
    reps = x.shape[1] // cos.shape[1]

    def heads(t):
        return jnp.concatenate([t] * reps, axis=1)

    return (x * heads(cos) + _roll_lanes(x, -ROPE_STEP) * heads(sin_up) + _roll_lanes(x, ROPE_STEP) * heads(sin_down),)


def _krdt_fn(x, cos, sin, rot, bias):
    lane = lax.broadcasted_iota(jnp.int32, (1, KRDT_WIDTH), 1)
    is_dt = (lane >= QK_ROPE_DIM) & (lane < QK_ROPE_DIM + 2 * N_SSD_HEADS)
    roped = x * cos + _dot_exact(x, rot) * sin
    return (jnp.where(is_dt, _softplus(x + bias), roped),)


def _pre_fn(u, w, shift, scale):
    return (_rms(u, w) * (1.0 + scale) + shift,)


def _norm_fn(x, w):
    return (_rms(x, w),)


def _finish_fn(yf, yb, z, w):
    return (_rms((yf + yb) * _silu(z), w),)


def _mid_fn(x, mix, w_post, w_pre, gate, shift, scale):
    x1 = x + gate * _rms(mix, w_post)
    return (x1, _rms(x1, w_pre) * (1.0 + scale) + shift)


def _loss_fn(x1, ffn, tgt, w_post, gate):
    y = x1 + gate * _rms(ffn, w_post)
    err = y - tgt
    return (0.5 * jnp.mean(err * err, axis=-1, keepdims=True),)


def _bias_fn(x, b):
    return (x + b,)


def _silu_fn(x):
    return (_silu(x),)


def kernel(x, c, ctx, c_ctx, w_mod, b_mod, mix_pre_norm, mix_post_norm, w_in, q_norm, w_q_up, kv_norm, w_kv_up, ssd_conv_w, ssd_conv_b, ssd_a_log, ssd_dt_bias, ssd_d, ssd_norm, w_out, ffn_pre_norm, ffn_post_norm, w_up, ffn_conv_w, ffn_conv_b, w_down, loss_target, m_c_ctx, m_w_mod, m_b_mod, m_mix_pre_norm, m_mix_post_norm, m_w_in, m_q_norm, m_w_q_up, m_kv_norm, m_w_kv_up, m_ssd_conv_w, m_ssd_conv_b, m_ssd_a_log, m_ssd_dt_bias, m_ssd_d, m_ssd_norm, m_w_out, m_ffn_pre_norm, m_ffn_post_norm, m_w_up, m_ffn_conv_w, m_ffn_conv_b, m_w_down, v_c_ctx, v_w_mod, v_b_mod, v_mix_pre_norm, v_mix_post_norm, v_w_in, v_q_norm, v_w_q_up, v_kv_norm, v_w_kv_up, v_ssd_conv_w, v_ssd_conv_b, v_ssd_a_log, v_ssd_dt_bias, v_ssd_d, v_ssd_norm, v_w_out, v_ffn_pre_norm, v_ffn_post_norm, v_w_up, v_ffn_conv_w, v_ffn_conv_b, v_w_down):
    args = dict(locals())
    names = ["c_ctx", "w_mod", "b_mod", "mix_pre_norm", "mix_post_norm", "w_in", "q_norm", "w_q_up", "kv_norm",
             "w_kv_up", "ssd_conv_w", "ssd_conv_b", "ssd_a_log", "ssd_dt_bias", "ssd_d", "ssd_norm", "w_out",
             "ffn_pre_norm", "ffn_post_norm", "w_up", "ffn_conv_w", "ffn_conv_b", "w_down"]
    nb, s, d = x.shape
    nctx_rows = ctx.shape[1]
    lt = nctx_rows + s
    tl = 256 if (nctx_rows % 256 == 0 and s % 256 == 0) else 128
    nctx = nctx_rows // tl
    ncc = nctx_rows // SSD_CHUNK
    h, e, g2 = N_ATTN_HEADS, HEADS_PER_GROUP, SSD_GROUPS
    chip = 2 * lax.axis_index("x") + lax.axis_index("y")

    big_local = {n: args[n][0] for n, _, _, _ in BIG}
    big_info = {n: (rows, cols, axis) for n, rows, cols, axis in BIG}

    def pack_shards(group):
        return _pack_rows([big_local[n].astype(WIRE_DTYPE) for n in group])

    def unpack_gathered(gathered, mine, group):
        gathered = lax.dynamic_update_slice(gathered, mine[None], (chip, 0, 0))
        res, o = {}, 0
        for n in group:
            rows, cols, axis = big_info[n]
            lr, lc = big_local[n].shape
            nr = lr * lc // PACK_COLS
            seg = gathered[:, o:o + nr].reshape(N_CHIPS, lr, lc)
            o += nr
            res[n] = seg.reshape(rows, cols) if axis == 0 else jnp.transpose(seg, (1, 0, 2)).reshape(rows, cols)
        return res

    core = lax.axis_index("c")

    def pair_sums(grads_full, group, tag):
        parts = []
        for n in group:
            _, _, axis = big_info[n]
            lr, lc = big_local[n].shape
            gfull = grads_full[n]
            shards = (gfull.reshape(N_CHIPS, lr, lc) if axis == 0
                      else jnp.transpose(gfull.reshape(lr, N_CHIPS, lc), (1, 0, 2)))
            parts.append(shards.reshape(N_CHIPS, lr * lc // PACK_COLS, PACK_COLS))
        gpack = jnp.concatenate(parts, axis=1).astype(WIRE_DTYPE)
        half = gpack.shape[1] // 2
        gpack = gpack.reshape(N_CHIPS, 2, half, PACK_COLS)
        got = _swap_halves(gpack, name="grad_swap_" + tag)
        own = lax.dynamic_index_in_dim(gpack, core, axis=1, keepdims=False)
        flat = (N_CHIPS * half, PACK_COLS)
        return _sum_list([own.reshape(flat), got.reshape(flat)], out_dtype=WIRE_DTYPE,
                         name="grad_add_pair_" + tag).reshape(N_CHIPS, half, PACK_COLS)

    def chip_total(sums, scattered, tag):
        mine_sum = lax.dynamic_index_in_dim(sums, chip, axis=0, keepdims=True)
        scattered = lax.dynamic_update_slice(scattered, mine_sum, (chip, 0, 0))
        return _sum_slots(scattered, out_dtype=F32, name="grad_add_chips_" + tag)

    packed_now, packed_late = pack_shards(GATHER_NOW), pack_shards(GATHER_LATE)
    full = unpack_gathered(_gather_shards(packed_now, name="gather_weights"), packed_now, GATHER_NOW)
    n_sc, n_fc = ssd_conv_w.shape[2], ffn_conv_w.shape[2]
    n_conv = SSD_CONV * n_sc + FFN_CONV * n_fc
    first_rows = -(-(n_conv + nb * d) // (8 * LANES)) * 8
    first_all = _allgather_small(_pack_small([ssd_conv_w[0], ffn_conv_w[0], c], first_rows), name="gather_conv_c")
    first_all = first_all.reshape(N_DEV, -1)
    conv_all = first_all[::2]
    ssd_conv_full = jnp.concatenate(
        [conv_all[j][:SSD_CONV * n_sc].reshape(SSD_CONV, n_sc) for j in range(N_CHIPS)], axis=1)
    ffn_conv_full = jnp.concatenate(
        [conv_all[j][SSD_CONV * n_sc:n_conv].reshape(FFN_CONV, n_fc) for j in range(N_CHIPS)], axis=1)
    c_every = first_all[:, n_conv:n_conv + nb * d].reshape(N_DEV * nb, d)

    w_in_p = _perm_in_cols(full["w_in"])
    o_cq, o_ckv, o_z = 0, Q_LORA_RANK, Q_LORA_RANK + KV_LORA_RANK
    o_xbc, o_kr = o_z + D_INNER, o_z + D_INNER + XBC_WIDTH
    w_krdt = jnp.pad(w_in_p[:, o_kr:], ((0, 0), (0, KRDT_WIDTH - QK_ROPE_DIM - 2 * N_SSD_HEADS)))
    w_segs = [w_in_p[:, o_cq:o_ckv], w_in_p[:, o_ckv:o_z], w_in_p[:, o_z:o_xbc], _group_xbc(w_in_p[:, o_xbc:o_kr]),
              w_krdt]
    ssd_conv_g, ssd_conv_b_g = _group_xbc(ssd_conv_full), _group_xbc(ssd_conv_b)
    w_q_pad = _pad_q_heads(full["w_q_up"])

    mod_rows = 16
    n_ex = N_DEV * nb
    all_rows = -(-(n_ex + 1) // 16) * 16
    me = 2 * chip + lax.axis_index("c")
    c_all = jnp.concatenate([c_every, c_ctx[None, :], jnp.zeros((all_rows - n_ex - 1, d), F32)], axis=0)[None]
    (s_all,) = _row_fwd(_silu_fn, toks=[(c_all, 0, None, 0)], outs=[(d, F32)], nb=1, nl=all_rows, tl=all_rows,
                        name="mod_silu")
    w_mod_local = w_mod[0]
    mod_cols = w_mod_local.shape[1]
    mod_part = _mm(s_all[0], w_mod_local, name="mod_mm")
    mod_parts = _allgather_small(mod_part, name="gather_mod")[::2]
    mod_every = jnp.concatenate([mod_parts[j] for j in range(N_CHIPS)], axis=1)
    mod_lin = jnp.concatenate([lax.dynamic_slice_in_dim(mod_every, me * nb, nb, axis=0), mod_every[n_ex:n_ex + 1],
                               jnp.zeros((mod_rows - nb - 1, N_MOD * d), F32)], axis=0)
    (mod,) = _row_fwd(_bias_fn, toks=[(mod_lin[None], 0, None, 0)], vecs=[b_mod], outs=[(N_MOD * d, F32)], nb=1,
                      nl=mod_rows, tl=mod_rows, name="mod_bias")
    mods = [mod[0][:, k * d:(k + 1) * d][:, None, :] for k in range(N_MOD)]
    mods_lat = [m[:nb] for m in mods]

    u = jnp.concatenate([ctx, x], axis=1)
    (h1,) = _row_fwd(_pre_fn, toks=[(u, 0, None, 0)], vecs=[mix_pre_norm], bvecs=[mods[0], mods[1]],
                     outs=[(d, MXU_DTYPE)], nb=nb, nl=lt, tl=tl, nctx=nctx, name="pre1")
    h1f = h1.reshape(nb * lt, d)
    p_cq, p_ckv, p_z, p_xbc, p_krdt = [
        _mm(h1f, w, name="in_" + nm).reshape(nb, lt, -1)
        for nm, w in zip(("cq", "ckv", "z", "xbc", "krdt"), w_segs)]

    (cqn,) = _row_fwd(_norm_fn, toks=[(p_cq, nctx, None, 0)], vecs=[q_norm], outs=[(Q_LORA_RANK, MXU_DTYPE)],
                      nb=nb, nl=s, tl=tl, name="q_norm")
    q_flat = _mm(cqn.reshape(nb * s, -1), w_q_pad, name="q_up").reshape(nb, s, h * HEAD_TILE)
    cos, sin = _rope_tables(s)
    ones, zeros = jnp.ones((s, QK_NOPE_DIM), F32), jnp.zeros((s, QK_NOPE_DIM), F32)
    tail = HEAD_TILE - QK_DIM
    up_lanes = ((jnp.arange(QK_ROPE_DIM) // ROPE_STEP) % 2 == 0)[None, :]
    q_tables = [jnp.concatenate([pad, t, pad[:, :tail]], axis=1)[None]
                for pad, t in ((ones, cos), (zeros, jnp.where(up_lanes, -sin, 0.0)), (zeros, jnp.where(up_lanes, 0.0, sin)))]
    tq = 256
    (q_roped,) = _row_fwd(_rope_flat_fn, toks=[(q_flat, 0, None, 0)], poss=q_tables, outs=[(h * HEAD_TILE, MXU_DTYPE)],
                          nb=nb, nl=s, tl=tl, name="rope_q")

    (ckvn,) = _row_fwd(_norm_fn, toks=[(p_ckv, 0, None, 0)], vecs=[kv_norm], outs=[(KV_LORA_RANK, MXU_DTYPE)],
                       nb=nb, nl=lt, tl=tl, name="kv_norm")
    kv_flat = _mm(ckvn.reshape(nb * lt, -1), full["w_kv_up"], out_dtype=MXU_DTYPE, name="kv_up").reshape(nb, lt, -1)

    pad_w = KRDT_WIDTH - QK_ROPE_DIM
    cos_k = jnp.concatenate([jnp.ones((nctx_rows, KRDT_WIDTH), F32),
                             jnp.concatenate([cos, jnp.ones((s, pad_w), F32)], axis=1)], axis=0)[None]
    sin_k = jnp.concatenate([jnp.zeros((nctx_rows, KRDT_WIDTH), F32),
                             jnp.concatenate([sin, jnp.zeros((s, pad_w), F32)], axis=1)], axis=0)[None]
    rot_k = _rot_matrix(KRDT_WIDTH, 0)
    dt_bias_row = jnp.pad(ssd_dt_bias.reshape(1, -1), ((0, 0), (QK_ROPE_DIM, pad_w - 2 * N_SSD_HEADS)))
    (krdt,) = _row_fwd(_krdt_fn, toks=[(p_krdt, 0, None, 0)], poss=[cos_k, sin_k], vecs=[rot_k, dt_bias_row],
                       outs=[(KRDT_WIDTH, F32)], nb=nb, nl=lt, tl=tl, name="krdt")
    kr = jnp.pad(krdt[..., :QK_ROPE_DIM].astype(MXU_DTYPE), ((0, 0), (0, 0), (QK_NOPE_DIM, HEAD_TILE - QK_DIM)))
    attn, lse, gathered_late = _attn_fwd(q_roped, kv_flat, kr, packed_late, tq=tq, name="attn_fwd")
    full.update(unpack_gathered(gathered_late, packed_late, GATHER_LATE))
    w_up_il = _interleave_up(full["w_up"])
    w_out_a, w_out_s = full["w_out"][:ATTN_WIDTH], full["w_out"][ATTN_WIDTH:]

    seg = nctx_rows

    def conv_ssd_fn(xv, w, b):
        return (_silu(_dwconv(xv, w, seg) + b),)

    cb_ssd = 256
    conv_vecs = [(ssd_conv_g, 0), (ssd_conv_b_g, 0)]
    (xbc,) = _seq_fwd(conv_ssd_fn, toks=[(p_xbc, 0, 1)], vecs=conv_vecs, outs=[F32], nb=nb, nl=lt, nc=XBC_WIDTH,
                      cb=cb_ssd, name="conv_ssd")
    dt = krdt[..., QK_ROPE_DIM:QK_ROPE_DIM + 2 * N_SSD_HEADS].reshape(nb, lt, 2, g2, e)
    dt_lane = QK_ROPE_DIM + N_SSD_HEADS * jnp.arange(2)[:, None, None] + jnp.arange(D_INNER)[None, None, :] // SSD_HEAD_DIM
    spread = (jnp.arange(KRDT_WIDTH)[None, :, None] == dt_lane).astype(F32)

    def spread_fn(v, s0, s1):
        return (_mask_dot_raw(s0, v, False), _mask_dot_raw(s1, v, False))

    dtc = _row_fwd(spread_fn, toks=[(krdt, 0, None, 0)], vecs=[spread[0], spread[1]],
                   outs=[(D_INNER, F32), (D_INNER, F32)], nb=nb, nl=lt, tl=tl, name="dt_spread")
    dtr = jnp.transpose(dt, (2, 0, 3, 4, 1))
    a_neg = -jnp.exp(ssd_a_log[0]).reshape(2, g2, e)
    d_chan = jnp.repeat(ssd_d[0], SSD_HEAD_DIM).reshape(g2, 1, X_COLS)
    a_chan = [jnp.repeat(a_neg[dr].reshape(-1), SSD_HEAD_DIM).reshape(g2, 1, X_COLS) for dr in range(2)]
    scan_args = [(xbc, dtc[dr], dtr[dr], a_chan[dr], a_neg[dr][:, :, None], d_chan) for dr in range(2)]
    (y0, ent0), (y1, ent1) = _scan_fwd(scan_args, ncc=ncc, name="scan_fwd")
    ys, ents = [y0, y1], [ent0, ent1]
    (ssd,) = _row_fwd(_finish_fn, toks=[(ys[0], nctx, None, 0), (ys[1], nctx, None, 0), (p_z, nctx, None, 0)],
                      vecs=[ssd_norm], outs=[(D_INNER, MXU_DTYPE)], nb=nb, nl=s, tl=tl, name="ssd_finish")

    attn_f, ssd_f = attn.reshape(nb * s, ATTN_WIDTH), ssd.reshape(nb * s, D_INNER)
    mix = _mm_sum([(attn_f, w_out_a), (ssd_f, w_out_s)], name="out_proj").reshape(nb, s, d)

    mid_bvecs = [mods_lat[2], mods_lat[3], mods_lat[4]]
    x1, h2 = _row_fwd(_mid_fn, toks=[(x, 0, None, 0), (mix, 0, None, 0)], vecs=[mix_post_norm, ffn_pre_norm],
                      bvecs=mid_bvecs, outs=[(d, F32), (d, MXU_DTYPE)], nb=nb, nl=s, tl=tl, name="mid")
    up = _mm(h2.reshape(nb * s, d), w_up_il, name="ffn_up").reshape(nb, s, 2 * D_FF)

    def glu_fn(gv, w, b):
        return (_gelu(_dwconv(gv[:, :UP_BLOCK], w, 0) + b) * gv[:, UP_BLOCK:],)

    cb_ffn = UP_BLOCK
    glu_toks = [(up, 0, 2)]
    glu_vecs = [(ffn_conv_full, 0), (ffn_conv_b, 0)]
    (act,) = _seq_fwd(glu_fn, toks=glu_toks, vecs=glu_vecs, outs=[MXU_DTYPE], nb=nb, nl=s, nc=D_FF, cb=cb_ffn,
                      name="conv_glu")
    ffn = _mm(act.reshape(nb * s, D_FF), full["w_down"], name="ffn_down").reshape(nb, s, d)

    loss_toks = [(x1, 0, None, 0), (ffn, 0, None, 0), (loss_target, 0, None, 0)]
    ones_rows = jnp.ones((nb, s, 1), F32)
    (dx1_a, dffn, _), (g_ffn_post,), ((g_gate5, _),), (loss_rows,) = _row_bwd(
        _loss_fn, toks=loss_toks, vecs=[ffn_post_norm], bvecs=[mods_lat[5]], cots=[(ones_rows, 0)],
        tok_grads=[F32, MXU_DTYPE, None], emit=[(0, 1, F32)], nb=nb, nl=s, tl=tl, name="loss_bwd")
    loss_part = jnp.sum(loss_rows)

    dffn_f = dffn.reshape(nb * s, d)
    g_w_down = _mm(act.reshape(nb * s, D_FF), dffn_f, ta=True, name="wg_down")
    dact = _mm(dffn_f, full["w_down"], tb=True, out_dtype=MXU_DTYPE, name="dg_down").reshape(nb, s, D_FF)
    (dup,), (g_ffn_conv_w, g_ffn_conv_b) = _seq_bwd(
        glu_fn, toks=glu_toks, vecs=glu_vecs, cots=[[dact]], tok_grads=[MXU_DTYPE], nb=nb, nl=s, nc=D_FF,
        cb=cb_ffn, name="conv_glu_bwd")
    dup = dup.reshape(nb * s, 2 * D_FF)
    g_w_up = _deinterleave_up(_mm(h2.reshape(nb * s, d), dup, ta=True, name="wg_up"))
    dh2 = _mm(dup, w_up_il, tb=True, name="dg_up").reshape(nb, s, d)

    (dx_res, dmix), (g_mix_post, g_ffn_pre), ((g_gate2, _), (g_shift3, _), (g_scale4, _)), _ = _row_bwd(
        _mid_fn, toks=[(x, 0, None, 0), (mix, 0, None, 0)], vecs=[mix_post_norm, ffn_pre_norm], bvecs=mid_bvecs,
        cots=[(dx1_a, 0), (dh2, 0)], tok_grads=[F32, MXU_DTYPE], nb=nb, nl=s, tl=tl, name="mid_bwd")

    dmix_f = dmix.reshape(nb * s, d)
    g_w_out = jnp.concatenate([_mm(attn_f, dmix_f, ta=True, name="wg_out_attn"),
                               _mm(ssd_f, dmix_f, ta=True, name="wg_out_ssd")], axis=0)
    early_sums = pair_sums({"w_up": g_w_up, "w_down": g_w_down, "w_out": g_w_out}, REDUCE_EARLY, "early")
    dattn = _mm(dmix_f, w_out_a, tb=True, name="dg_out_attn").reshape(nb, s, ATTN_WIDTH)
    dssd = _mm(dmix_f, w_out_s, tb=True, name="dg_out_ssd").reshape(nb, s, D_INNER)

    (dy, _, dz), (g_ssd_norm,), _, _ = _row_bwd(
        _finish_fn, toks=[(ys[0], 0, None, 0), (ys[1], 0, None, 0), (p_z, 0, None, 0)], vecs=[ssd_norm],
        cots=[(dssd, -nctx)], tok_grads=[F32, None, MXU_DTYPE], nb=nb, nl=lt, tl=tl, name="ssd_finish_bwd")
    scan_grads = _scan_bwd(scan_args, ents, dy, ncc=ncc, name="scan_bwd")

    def collect_fn(g0, g1, c0, c1):
        return (_mask_dot_raw(c0, g0, False) + _mask_dot_raw(c1, g1, False),)

    (g_dt_lanes,) = _row_fwd(collect_fn, toks=[(scan_grads[0][1], 0, None, 0), (scan_grads[1][1], 0, None, 0)],
                             vecs=[spread[0].T, spread[1].T], outs=[(KRDT_WIDTH, F32)], nb=nb, nl=lt, tl=tl,
                             name="dt_collect")
    g_dt_dirs, g_a = [], []
    for _, _, gdtr, gac, gar, _ in scan_grads:
        g_dt_dirs.append(jnp.transpose(gdtr, (0, 3, 1, 2)))
        g_a.append(jnp.sum(jnp.sum(gac.reshape(nb, g2, e, SSD_HEAD_DIM), axis=-1) + gar[:, :, :, 0], axis=0))
    g_d_chan = jnp.sum(scan_grads[0][5], axis=0)
    g_a_log = (jnp.stack(g_a) * a_neg).reshape(1, 2, N_SSD_HEADS)
    g_dt = (jnp.stack(g_dt_dirs, axis=2).reshape(nb, lt, 2 * N_SSD_HEADS)
            + g_dt_lanes[..., QK_ROPE_DIM:QK_ROPE_DIM + 2 * N_SSD_HEADS])
    (dp_xbc,), (g_ssd_conv_w, g_ssd_conv_b) = _seq_bwd(
        conv_ssd_fn, toks=[(p_xbc, 0, 1)], vecs=conv_vecs, cots=[[scan_grads[0][0], scan_grads[1][0]]],
        tok_grads=[MXU_DTYPE], nb=nb, nl=lt, nc=XBC_WIDTH, cb=cb_ssd, name="conv_ssd_bwd")
    g_ssd_conv_w, g_ssd_conv_b = _ungroup_xbc(g_ssd_conv_w), _ungroup_xbc(g_ssd_conv_b)

    dq_roped, dkv, dkr, early_scattered = _attn_bwd(q_roped, kv_flat, kr, attn, lse, dattn, early_sums, tq=tq,
                                                    name="attn_bwd")
    (dq_flat,), _, _, _ = _row_bwd(_rope_flat_fn, toks=[(q_flat, 0, None, 0)], poss=q_tables, cots=[(dq_roped, 0)],
                                   tok_grads=[MXU_DTYPE], nb=nb, nl=s, tl=tl, name="rope_q_bwd")
    dq_flat = dq_flat.reshape(nb * s, h * HEAD_TILE)
    g_w_q_up = _unpad_q_heads(_mm(cqn.reshape(nb * s, -1), dq_flat, ta=True, name="wg_q_up"))
    dcqn = _mm(dq_flat, w_q_pad, tb=True, name="dg_q_up").reshape(nb, s, Q_LORA_RANK)
    (dp_cq,), (g_q_norm,), _, _ = _row_bwd(_norm_fn, toks=[(p_cq, 0, None, 0)], vecs=[q_norm], cots=[(dcqn, -nctx)],
                                           tok_grads=[MXU_DTYPE], nb=nb, nl=lt, tl=tl, name="q_norm_bwd")

    dkv_flat = dkv.reshape(nb * lt, -1)
    g_w_kv_up = _mm(ckvn.reshape(nb * lt, -1), dkv_flat, ta=True, name="wg_kv_up")
    dckvn = _mm(dkv_flat, full["w_kv_up"], tb=True, name="dg_kv_up").reshape(nb, lt, KV_LORA_RANK)
    (dp_ckv,), (g_kv_norm,), _, _ = _row_bwd(_norm_fn, toks=[(p_ckv, 0, None, 0)], vecs=[kv_norm], cots=[(dckvn, 0)],
                                             tok_grads=[MXU_DTYPE], nb=nb, nl=lt, tl=tl, name="kv_norm_bwd")

    g_krdt = jnp.concatenate([dkr[..., QK_NOPE_DIM:QK_DIM], g_dt, jnp.zeros((nb, lt, pad_w - 2 * N_SSD_HEADS), F32)],
                             axis=-1)
    (dp_krdt,), (_, g_dt_bias_row), _, _ = _row_bwd(
        _krdt_fn, toks=[(p_krdt, 0, None, 0)], poss=[cos_k, sin_k], vecs=[rot_k, dt_bias_row], cots=[(g_krdt, 0)],
        tok_grads=[MXU_DTYPE], nb=nb, nl=lt, tl=tl, name="krdt_bwd")

    dp_segs = [t.reshape(nb * lt, -1) for t in (dp_cq, dp_ckv, dz, dp_xbc, dp_krdt)]
    g_segs = [_mm(h1f, t, ta=True, name="wg_in_" + nm) for nm, t in zip(("cq", "ckv", "z", "xbc", "krdt"), dp_segs)]
    g_segs[3] = _ungroup_xbc(g_segs[3])
    g_w_in_p = jnp.concatenate(g_segs, axis=1)
    dh1 = _mm_sum(list(zip(dp_segs, w_segs)), tb=True, name="dg_in").reshape(nb, lt, d)

    def pre_res_fn(uv, w, shift, scale):
        return _pre_fn(uv, w, shift, scale) + (uv,)

    (grad_x,), (g_mix_pre,), ((g_shift0, g_shift0c), (g_scale1, g_scale1c)), _ = _row_bwd(
        pre_res_fn, toks=[(u, 0, None, 0)], vecs=[mix_pre_norm], bvecs=[mods[0], mods[1]],
        cots=[(dh1, 0), (dx_res, -nctx)], tok_grads=[F32], nb=nb, nl=lt, tl=tl, nctx=nctx, drop_blocks=nctx,
        name="pre1_bwd")

    zero_row = jnp.zeros((1, 1, d), F32)
    lat = [g_shift0, g_scale1, g_gate2, g_shift3, g_scale4, g_gate5]
    ctxg = [g_shift0c, g_scale1c, zero_row, zero_row, zero_row, zero_row]
    dmod = jnp.concatenate([jnp.concatenate([a, b], axis=0)[:, 0, :] for a, b in zip(lat, ctxg)], axis=-1)
    dmod = jnp.pad(dmod, ((0, mod_rows - nb - 1), (0, 0)))
    _, (g_b_mod,), _, _ = _row_bwd(_bias_fn, toks=[(mod_lin[None], 0, None, 0)], vecs=[b_mod], cots=[(dmod[None], 0)],
                                   tok_grads=[None], nb=1, nl=mod_rows, tl=mod_rows, name="mod_bias_bwd")
    dmod_all = _allgather_small(dmod[:8], name="gather_dmod")
    dmod_ctx = _sum_slots(dmod_all, out_dtype=F32, name="dmod_ctx_add")[nb:nb + 1]
    dmod_every = jnp.concatenate([dmod_all[:, :nb].reshape(n_ex, N_MOD * d), dmod_ctx,
                                  jnp.zeros((all_rows - n_ex - 1, N_MOD * d), F32)], axis=0)
    dmod_mine = lax.dynamic_slice_in_dim(dmod_every, chip * mod_cols, mod_cols, axis=1)
    g_w_mod = _mm(s_all[0], dmod_mine, ta=True, name="wg_mod")[None]
    ds_all = _mm(dmod_mine, w_mod_local, tb=True, name="dg_mod")
    (dc_all,), _, _, _ = _row_bwd(_silu_fn, toks=[(c_all, 0, None, 0)], cots=[(ds_all[None], 0)], tok_grads=[F32],
                                  nb=1, nl=all_rows, tl=all_rows, name="mod_silu_bwd")
    g_c_ctx = 0.5 * dc_all[0, n_ex]

    g_w_in = _unperm_in_cols(g_w_in_p[:, :IN_WIDTH])
    last_sums = pair_sums({"w_in": g_w_in, "w_q_up": g_w_q_up, "w_kv_up": g_w_kv_up}, REDUCE_LAST, "last")
    halves = [chip_total(early_sums, early_scattered, "early"),
              chip_total(last_sums, _scatter_to_chips(last_sums, name="grad_scatter"), "last")]
    my_halves = jnp.concatenate(halves, axis=0)
    joined = lax.dynamic_update_slice(_join_halves(my_halves, name="grad_join"), my_halves[None], (core, 0, 0))
    g_shards, o = {}, 0
    for group, hv in zip((REDUCE_EARLY, REDUCE_LAST), halves):
        g_shards[group] = joined[:, o:o + hv.shape[0]].reshape(2 * hv.shape[0], PACK_COLS)
        o += hv.shape[0]

    g_d = jnp.sum(g_d_chan.reshape(N_SSD_HEADS, SSD_HEAD_DIM), axis=1)[None]
    g_dt_bias = g_dt_bias_row[:, QK_ROPE_DIM:QK_ROPE_DIM + 2 * N_SSD_HEADS].reshape(1, 2, N_SSD_HEADS)
    small_names = ["c_ctx", "b_mod", "mix_pre_norm", "mix_post_norm", "q_norm", "kv_norm", "ssd_conv_w", "ssd_conv_b",
                   "ssd_a_log", "ssd_dt_bias", "ssd_d", "ssd_norm", "ffn_pre_norm", "ffn_post_norm", "ffn_conv_w",
                   "ffn_conv_b"]
    small_grads = [g_c_ctx, g_b_mod, g_mix_pre, g_mix_post, g_q_norm, g_kv_norm, g_ssd_conv_w, g_ssd_conv_b,
                   g_a_log, g_dt_bias, g_d, g_ssd_norm, g_ffn_pre, g_ffn_post, g_ffn_conv_w, g_ffn_conv_b]
    small_shapes = [tuple(np.shape(a)) for a in small_grads] + [()]
    n_small = sum(int(np.prod(shp)) for shp in small_shapes)
    small_rows = -(-n_small // (8 * LANES)) * 8
    small_all = _allgather_small(_pack_small(small_grads + [loss_part], small_rows), name="gather_small")
    small_sum = _sum_slots(small_all, out_dtype=F32, name="small_add")
    small_red = _unpack_small(small_sum, small_shapes)
    loss = small_red[-1]
    grads = dict(zip(small_names, small_red[:-1]))
    grads["ssd_conv_w"] = lax.dynamic_slice_in_dim(grads["ssd_conv_w"], chip * n_sc, n_sc, axis=1)[None]
    grads["ffn_conv_w"] = lax.dynamic_slice_in_dim(grads["ffn_conv_w"], chip * n_fc, n_fc, axis=1)[None]
    for n in small_names:
        grads[n] = grads[n].reshape(args[n].shape)

    delta, new_m, new_v = {}, {}, {}
    grads["w_mod"] = g_w_mod
    for group, g_shard in g_shards.items():
        o = 0
        for n in group:
            lr, lc = big_local[n].shape
            nr = lr * lc // PACK_COLS
            grads[n] = g_shard[o:o + nr].reshape(1, lr, lc)
            o += nr
    for n in ["w_mod"] + [n for n, _, _, _ in BIG]:
        dl, nm, nv = _adamw(args[n][0], grads[n][0], args["m_" + n][0], args["v_" + n][0], name="adamw_" + n)
        delta[n], new_m[n], new_v[n] = dl[None], nm[None], nv[None]
    sm_shapes = [args[n].shape for n in small_names]
    n_sm = sum(int(np.prod(shp)) for shp in sm_shapes)
    sm_rows = -(-n_sm // (8 * LANES)) * 8
    packs = [_pack_small([src[n] for n in small_names], sm_rows)
             for src in (args, grads, {n: args["m_" + n] for n in small_names}, {n: args["v_" + n] for n in small_names})]
    for out_dict, packed_out in zip((delta, new_m, new_v), _adamw(*packs, name="adamw_small")):
        out_dict.update(zip(small_names, _unpack_small(packed_out, sm_shapes)))

    return (loss, grad_x, *[grads[n] for n in names], *[delta[n] for n in names], *[new_m[n] for n in names],
            *[new_v[n] for n in names])
```

```python
import functools
import math

import numpy as np
import jax
import jax.numpy as jnp
from jax import lax
from jax.experimental import pallas as pl
from jax.experimental.pallas import tpu as pltpu

F32 = jnp.float32
MXU_DTYPE = jnp.bfloat16
WIRE_DTYPE = jnp.bfloat16
VMEM_LIMIT_BYTES = 56 * 1024 * 1024
HIGHEST = lax.Precision.HIGHEST

D_MODEL = 1024
N_MOD = 6
EPS = 1e-6
GRID_W = 64
N_ATTN_HEADS = 16
QK_NOPE_DIM = 64
QK_ROPE_DIM = 32
QK_DIM = QK_NOPE_DIM + QK_ROPE_DIM
V_HEAD_DIM = 64
Q_LORA_RANK = 384
KV_LORA_RANK = 256
ROPE_THETA = 10000.0
ATTN_SCALE = QK_DIM ** -0.5
ATTN_WIDTH = N_ATTN_HEADS * V_HEAD_DIM
N_SSD_HEADS = 16
SSD_HEAD_DIM = 64
SSD_GROUPS = 2
HEADS_PER_GROUP = N_SSD_HEADS // SSD_GROUPS
SSD_STATE = 128
SSD_CONV = 5
SSD_CHUNK = 128
D_INNER = N_SSD_HEADS * SSD_HEAD_DIM
GN = SSD_GROUPS * SSD_STATE
XBC_WIDTH = D_INNER + 2 * GN
D_FF = 2816
FFN_CONV = 3
KRDT_WIDTH = 128
IN_WIDTH = Q_LORA_RANK + KV_LORA_RANK + QK_ROPE_DIM + D_INNER + XBC_WIDTH + 2 * N_SSD_HEADS

ADAM_LR = 0.001
ADAM_B1 = 0.9
ADAM_B2 = 0.999
ADAM_EPS = 1e-08
ADAM_WD = 0.01
ADAM_STEP = 10

N_CHIPS = 4
N_DEV = 8
MESH = pl.DeviceIdType.MESH
LANES = 128

BIG = (("w_in", D_MODEL, IN_WIDTH, 1),
       ("w_q_up", Q_LORA_RANK, N_ATTN_HEADS * QK_DIM, 1),
       ("w_kv_up", KV_LORA_RANK, N_ATTN_HEADS * (QK_NOPE_DIM + V_HEAD_DIM), 1),
       ("w_out", ATTN_WIDTH + D_INNER, D_MODEL, 0), ("w_up", D_MODEL, 2 * D_FF, 1),
       ("w_down", D_FF, D_MODEL, 0))
PACK_COLS = 1024
GATHER_NOW, GATHER_LATE = ("w_in", "w_q_up", "w_kv_up"), ("w_out", "w_up", "w_down")
REDUCE_EARLY, REDUCE_LAST = ("w_up", "w_down", "w_out"), ("w_in", "w_q_up", "w_kv_up")


def _cparams(sem):
    return pltpu.CompilerParams(dimension_semantics=sem, vmem_limit_bytes=VMEM_LIMIT_BYTES)


def _pick(n, cands):
    for c in cands:
        if n % c == 0:
            return c
    return n


def _sigmoid(x):
    return 0.5 * (jnp.tanh(0.5 * x) + 1.0)


def _silu(x):
    return x * _sigmoid(x)


@jax.custom_vjp
def _softplus(x):
    u = jnp.exp(-jnp.abs(x))
    w = 1.0 + u
    log1p = jnp.where(w == 1.0, u, jnp.log(w) * (u / jnp.where(w == 1.0, 1.0, w - 1.0)))
    return jnp.maximum(x, 0.0) + log1p


def _softplus_fwd(x):
    return _softplus(x), x


def _softplus_bwd(x, g):
    return (g * _sigmoid(x),)


_softplus.defvjp(_softplus_fwd, _softplus_bwd)


@jax.custom_vjp
def _gelu(x):
    return 0.5 * x * (1.0 + lax.erf(x * (2.0 ** -0.5)))


def _gelu_fwd(x):
    return _gelu(x), x


def _gelu_bwd(x, g):
    cdf = 0.5 * (1.0 + lax.erf(x * (2.0 ** -0.5)))
    pdf = jnp.exp(-0.5 * x * x) * (1.0 / math.sqrt(2.0 * math.pi))
    return (g * (cdf + x * pdf),)


_gelu.defvjp(_gelu_fwd, _gelu_bwd)


def _rms(x, w):
    return x * lax.rsqrt(jnp.mean(x * x, axis=-1, keepdims=True) + EPS) * w


def _shift_rows_raw(x, off, seg):
    n = x.shape[0]
    if off == 0:
        return x
    r = pltpu.roll(x, (-off) % n, 0)
    idx = lax.broadcasted_iota(jnp.int32, x.shape, 0)
    src = idx + off
    ok = (src >= 0) & (src < n)
    if seg:
        ok = ok & ((idx < seg) == (src < seg))
    return jnp.where(ok, r, 0.0)


@functools.partial(jax.custom_vjp, nondiff_argnums=(1, 2))
def _shift_rows(x, off, seg):
    return _shift_rows_raw(x, off, seg)


def _shift_rows_fwd(x, off, seg):
    return _shift_rows_raw(x, off, seg), None


def _shift_rows_bwd(off, seg, _, g):
    return (_shift_rows_raw(g, -off, seg),)


_shift_rows.defvjp(_shift_rows_fwd, _shift_rows_bwd)


@functools.partial(jax.custom_vjp, nondiff_argnums=(1,))
def _roll_lanes(x, shift):
    return pltpu.roll(x, shift % x.shape[1], 1)


def _roll_lanes_fwd(x, shift):
    return _roll_lanes(x, shift), None


def _roll_lanes_bwd(shift, _, g):
    return (pltpu.roll(g, (-shift) % g.shape[1], 1),)


_roll_lanes.defvjp(_roll_lanes_fwd, _roll_lanes_bwd)


def _row_of(w, k):
    sel = lax.broadcasted_iota(jnp.int32, (w.shape[0], 1), 0) == k
    return jnp.sum(jnp.where(sel, w, 0.0), axis=0, keepdims=True)


def _col_of(w, k):
    sel = lax.broadcasted_iota(jnp.int32, (1, w.shape[1]), 1) == k
    return jnp.sum(jnp.where(sel, w, 0.0), axis=1, keepdims=True)


def _dwconv(x, w, seg):
    k = w.shape[0]
    acc = None
    for t in range(k):
        term = _shift_rows(x, t - k // 2, seg) * _row_of(w, t)
        acc = term if acc is None else acc + term
    return acc


def _dwconv_back(x, dy, w, seg):
    k = w.shape[0]
    tap = lax.broadcasted_iota(jnp.int32, (k, 1), 0)
    dx, dw = None, jnp.zeros_like(w)
    for t in range(k):
        back = _shift_rows_raw(dy, k // 2 - t, seg)
        term = back * _row_of(w, t)
        dx = term if dx is None else dx + term
        dw = dw + jnp.where(tap == t, jnp.sum(x * back, axis=0, keepdims=True), 0.0)
    return dx, dw


def _dot(a, b, dims):
    return lax.dot_general(a.astype(MXU_DTYPE), b.astype(MXU_DTYPE), (dims, ((), ())),
                           preferred_element_type=F32)


def _dot_exact(a, b):
    return lax.dot_general(a, b, (((1,), (0,)), ((), ())), precision=HIGHEST,
                           preferred_element_type=F32)


def _mask_dot_raw(mask, x, mask_left):
    hi = x.astype(jnp.bfloat16)
    rest = x - hi.astype(F32)
    mid = rest.astype(jnp.bfloat16)
    low = (rest - mid.astype(F32)).astype(jnp.bfloat16)
    m = mask.astype(jnp.bfloat16)
    acc = None
    for piece in (hi, mid, low):
        term = (lax.dot_general(m, piece, (((1,), (0,)), ((), ())), preferred_element_type=F32) if mask_left
                else lax.dot_general(piece, m, (((1,), (0,)), ((), ())), preferred_element_type=F32))
        acc = term if acc is None else acc + term
    return acc


@functools.partial(jax.custom_vjp, nondiff_argnums=(3,))
def _mask_dot(mask, mask_t, x, mask_left):
    return _mask_dot_raw(mask, x, mask_left)


def _mask_dot_fwd(mask, mask_t, x, mask_left):
    return _mask_dot_raw(mask, x, mask_left), (mask, mask_t)


def _mask_dot_bwd(mask_left, res, g):
    mask, mask_t = res
    return jnp.zeros_like(mask), jnp.zeros_like(mask_t), _mask_dot_raw(mask_t, g, mask_left)


_mask_dot.defvjp(_mask_dot_fwd, _mask_dot_bwd)


def _mm(a, b, *, ta=False, tb=False, out_dtype=F32, name):
    if ta:
        kdim, m = a.shape
    else:
        m, kdim = a.shape
    if tb:
        n, k2 = b.shape
    else:
        k2, n = b.shape
    assert kdim == k2, (a.shape, b.shape, ta, tb)
    tm = _pick(m, (1024, 1408, 512, 384, 256, 128))
    tn = _pick(n, (512, 1408, 384, 256, 128))
    tk = kdim if kdim <= 2048 else _pick(kdim, (2048, 1664, 1536, 1408, 1024, 512, 256, 128))
    nk = kdim // tk
    a_spec = pl.BlockSpec((tk, tm), lambda i, j, k: (k, i)) if ta else pl.BlockSpec((tm, tk), lambda i, j, k: (i, k))
    b_spec = pl.BlockSpec((tn, tk), lambda i, j, k: (j, k)) if tb else pl.BlockSpec((tk, tn), lambda i, j, k: (k, j))
    dims = ((0,) if ta else (1,), (1,) if tb else (0,))

    def body(a_ref, b_ref, o_ref, *scratch):
        if nk == 1:
            o_ref[...] = _dot(a_ref[...], b_ref[...], dims).astype(o_ref.dtype)
            return
        acc_ref, = scratch
        k = pl.program_id(2)

        @pl.when(k == 0)
        def _():
            acc_ref[...] = jnp.zeros_like(acc_ref)

        acc_ref[...] += _dot(a_ref[...], b_ref[...], dims)

        @pl.when(k == nk - 1)
        def _():
            o_ref[...] = acc_ref[...].astype(o_ref.dtype)

    return pl.pallas_call(
        body, name=name, grid=(m // tm, n // tn, nk),
        in_specs=[a_spec, b_spec], out_specs=pl.BlockSpec((tm, tn), lambda i, j, k: (i, j)),
        out_shape=jax.ShapeDtypeStruct((m, n), out_dtype),
        scratch_shapes=[pltpu.VMEM((tm, tn), F32)] if nk > 1 else [],
        compiler_params=_cparams(("parallel", "parallel", "arbitrary")),
    )(a, b)


def _mm_sum(pairs, *, tb=False, out_dtype=F32, name):
    m = pairs[0][0].shape[0]
    n = pairs[0][1].shape[0] if tb else pairs[0][1].shape[1]
    tm = _pick(m, (1024, 1408, 512, 384, 256, 128))
    tn = _pick(n, (512, 1408, 384, 256, 128))
    specs, args = [], []
    for a, b in pairs:
        kdim = a.shape[1]
        specs.append(pl.BlockSpec((tm, kdim), lambda i, j: (i, 0)))
        specs.append(pl.BlockSpec((tn, kdim), lambda i, j: (j, 0)) if tb else pl.BlockSpec((kdim, tn), lambda i, j: (0, j)))
        args += [a, b]
    dims = ((1,), (1,) if tb else (0,))

    def body(*refs):
        acc = None
        for t in range(len(pairs)):
            term = _dot(refs[2 * t][...], refs[2 * t + 1][...], dims)
            acc = term if acc is None else acc + term
        refs[-1][...] = acc.astype(refs[-1].dtype)

    return pl.pallas_call(
        body, name=name, grid=(m // tm, n // tn), in_specs=specs,
        out_specs=pl.BlockSpec((tm, tn), lambda i, j: (i, j)), out_shape=jax.ShapeDtypeStruct((m, n), out_dtype),
        compiler_params=_cparams(("parallel", "parallel")),
    )(*args)


def _row_specs(toks, poss, vecs, bvecs, tl, nctx, nb):
    specs, args = [], []
    for arr, off, cw, ci in toks:
        cw = arr.shape[2] if cw is None else cw
        specs.append(pl.BlockSpec((1, tl, cw), lambda b, l, off=off, ci=ci: (b, l + off, ci)))
        args.append(arr)
    for arr in poss:
        specs.append(pl.BlockSpec((1, tl, arr.shape[2]), lambda b, l: (0, l, 0)))
        args.append(arr)
    for arr in vecs:
        specs.append(pl.BlockSpec(arr.shape, lambda b, l: (0, 0)))
        args.append(arr)
    for arr in bvecs:
        if nctx:
            specs.append(pl.BlockSpec((1, 1, arr.shape[2]), lambda b, l: (jnp.where(l < nctx, nb, b), 0, 0)))
        else:
            specs.append(pl.BlockSpec((1, 1, arr.shape[2]), lambda b, l: (b, 0, 0)))
        args.append(arr)
    return specs, args


def _row_fwd(fn, *, toks, poss=(), vecs=(), bvecs=(), outs, nb, nl, tl, nctx=0, name):
    nt, npos, nv, nbv = len(toks), len(poss), len(vecs), len(bvecs)
    specs, args = _row_specs(toks, poss, vecs, bvecs, tl, nctx, nb)

    def body(*refs):
        ins, os = refs[:len(specs)], refs[len(specs):]
        tv = [r[0].astype(F32) for r in ins[:nt]]
        pv = [r[0] for r in ins[nt:nt + npos]]
        vv = [r[...] for r in ins[nt + npos:nt + npos + nv]]
        bv = [r[0] for r in ins[nt + npos + nv:]]
        res = fn(*tv, *pv, *vv, *bv)
        for o, r in zip(os, res):
            o[0] = r.astype(o.dtype)

    return pl.pallas_call(
        body, name=name, grid=(nb, nl // tl), in_specs=specs,
        out_specs=[pl.BlockSpec((1, tl, c), lambda b, l: (b, l, 0)) for c, _ in outs],
        out_shape=[jax.ShapeDtypeStruct((nb, nl, c), dt) for c, dt in outs],
        compiler_params=_cparams(("parallel", "parallel")),
    )(*args)


def _row_bwd(fn, *, toks, poss=(), vecs=(), bvecs=(), cots, tok_grads, emit=(), nb, nl, tl, nctx=0, name,
             drop_blocks=0):
    nt, npos, nv, nbv = len(toks), len(poss), len(vecs), len(bvecs)
    specs, args = _row_specs(toks, poss, vecs, bvecs, tl, nctx, nb)
    n_in = len(specs)
    cot_slots = []
    for arr, off in cots:
        if arr is None:
            cot_slots.append(None)
            continue
        cot_slots.append((len(specs), off))
        specs.append(pl.BlockSpec((1, tl, arr.shape[2]), lambda b, l, off=off: (b, jnp.maximum(l + off, 0), 0)))
        args.append(arr)
    n_all_in = len(specs)

    out_specs, out_shapes = [], []
    tok_out = []
    for (arr, off, cw, ci), dt in zip(toks, tok_grads):
        if dt is None:
            tok_out.append(None)
            continue
        cw = arr.shape[2] if cw is None else cw
        tok_out.append(len(out_specs))
        out_specs.append(pl.BlockSpec((1, tl, cw), lambda b, l: (b, jnp.maximum(l - drop_blocks, 0), 0)))
        out_shapes.append(jax.ShapeDtypeStruct((nb, nl - drop_blocks * tl, cw), dt))
    vec_out = []
    for arr in vecs:
        vec_out.append(len(out_specs))
        out_specs.append(pl.BlockSpec(arr.shape, lambda b, l: (0, 0)))
        out_shapes.append(jax.ShapeDtypeStruct(arr.shape, F32))
    bv_out = []
    for arr in bvecs:
        c = arr.shape[2]
        lat = len(out_specs)
        out_specs.append(pl.BlockSpec((1, 1, c), lambda b, l: (b, 0, 0)))
        out_shapes.append(jax.ShapeDtypeStruct((nb, 1, c), F32))
        ctx = None
        if nctx:
            ctx = len(out_specs)
            out_specs.append(pl.BlockSpec((1, 1, c), lambda b, l: (0, 0, 0)))
            out_shapes.append(jax.ShapeDtypeStruct((1, 1, c), F32))
        bv_out.append((lat, ctx))
    emit_out = []
    emit_cols = {}
    for idx, c, dt in emit:
        emit_out.append((idx, len(out_specs)))
        out_specs.append(pl.BlockSpec((1, tl, c), lambda b, l: (b, l, 0)))
        out_shapes.append(jax.ShapeDtypeStruct((nb, nl, c), dt))

    def body(*refs):
        ins, os = refs[:n_all_in], refs[n_all_in:]
        b, l = pl.program_id(0), pl.program_id(1)
        tv = [r[0].astype(F32) for r in ins[:nt]]
        pv = [r[0] for r in ins[nt:nt + npos]]
        vv = [r[...] for r in ins[nt + npos:nt + npos + nv]]
        bv = [r[0] for r in ins[nt + npos + nv:n_in]]

        def f(*d):
            return tuple(fn(*d[:nt], *pv, *d[nt:]))

        res, vjp = jax.vjp(f, *tv, *vv, *bv)
        cts = []
        for r, slot in zip(res, cot_slots):
            if slot is None:
                cts.append(jnp.zeros_like(r))
            else:
                i, off = slot
                ct = ins[i][0].astype(F32)
                if off < 0:
                    ct = jnp.where(l + off >= 0, ct, 0.0)
                cts.append(ct)
        grads = vjp(tuple(cts))

        for g, slot in zip(grads[:nt], tok_out):
            if slot is not None:
                os[slot][0] = g.astype(os[slot].dtype)

        @pl.when((b == 0) & (l == 0))
        def _():
            for slot in vec_out:
                os[slot][...] = jnp.zeros_like(os[slot])
            for _, ctx in bv_out:
                if ctx is not None:
                    os[ctx][...] = jnp.zeros_like(os[ctx])

        @pl.when(l == 0)
        def _():
            for lat, _ in bv_out:
                os[lat][...] = jnp.zeros_like(os[lat])

        for g, slot in zip(grads[nt:nt + nv], vec_out):
            os[slot][...] += g
        for g, (lat, ctx) in zip(grads[nt + nv:], bv_out):
            if ctx is None:
                os[lat][0] += g
            else:
                is_ctx = l < nctx
                os[lat][0] += jnp.where(is_ctx, 0.0, g)
                os[ctx][0] += jnp.where(is_ctx, g, 0.0)
        for idx, slot in emit_out:
            os[slot][0] = res[idx].astype(os[slot].dtype)

    out = pl.pallas_call(
        body, name=name, grid=(nb, nl // tl), in_specs=specs, out_specs=out_specs, out_shape=out_shapes,
        compiler_params=_cparams(("arbitrary", "arbitrary")),
    )(*args)
    tg = [None if s is None else out[s] for s in tok_out]
    vg = [out[s] for s in vec_out]
    bg = [(out[lat], None if ctx is None else out[ctx]) for lat, ctx in bv_out]
    em = [out[s] for _, s in emit_out]
    return tg, vg, bg, em


def _seq_specs(toks, vecs, nl, cb):
    specs, args = [], []
    for arr, off, mult in toks:
        specs.append(pl.BlockSpec((1, nl, cb * mult), lambda j, b, off=off: (b, 0, j + off)))
        args.append(arr)
    for arr, off in vecs:
        specs.append(pl.BlockSpec((arr.shape[0], cb), lambda j, b, off=off: (0, j + off)))
        args.append(arr)
    return specs, args


def _seq_fwd(fn, *, toks, vecs, outs, nb, nl, nc, cb, name):
    nt = len(toks)
    specs, args = _seq_specs(toks, vecs, nl, cb)

    def body(*refs):
        ins, os = refs[:len(specs)], refs[len(specs):]
        tv = [r[0].astype(F32) for r in ins[:nt]]
        vv = [r[...] for r in ins[nt:]]
        for o, r in zip(os, fn(*tv, *vv)):
            o[0] = r.astype(o.dtype)

    return pl.pallas_call(
        body, name=name, grid=(nc // cb, nb), in_specs=specs,
        out_specs=[pl.BlockSpec((1, nl, cb), lambda j, b: (b, 0, j)) for _ in outs],
        out_shape=[jax.ShapeDtypeStruct((nb, nl, nc), dt) for dt in outs],
        compiler_params=_cparams(("parallel", "parallel")),
    )(*args)


def _seq_bwd(fn, *, toks, vecs, cots, tok_grads, nb, nl, nc, cb, name, bwd_fn=None):
    nt, nv = len(toks), len(vecs)
    specs, args = _seq_specs(toks, vecs, nl, cb)
    n_in = len(specs)
    cot_counts = [len(group) for group in cots]
    for group in cots:
        for arr in group:
            specs.append(pl.BlockSpec((1, nl, cb), lambda j, b: (b, 0, j)))
            args.append(arr)
    out_specs, out_shapes = [], []
    for (_, _, mult), dt in zip(toks, tok_grads):
        out_specs.append(pl.BlockSpec((1, nl, cb * mult), lambda j, b: (b, 0, j)))
        out_shapes.append(jax.ShapeDtypeStruct((nb, nl, nc * mult), dt))
    for arr, _ in vecs:
        out_specs.append(pl.BlockSpec((arr.shape[0], cb), lambda j, b: (0, j)))
        out_shapes.append(jax.ShapeDtypeStruct((arr.shape[0], nc), F32))

    def body(*refs):
        ins, os = refs[:len(specs)], refs[len(specs):]
        b = pl.program_id(1)
        tv = [r[0].astype(F32) for r in ins[:nt]]
        vv = [r[...] for r in ins[nt:n_in]]
        cts, o = [], n_in
        for cnt in cot_counts:
            ct = ins[o][0].astype(F32)
            for r in ins[o + 1:o + cnt]:
                ct = ct + r[0].astype(F32)
            cts.append(ct)
            o += cnt
        if bwd_fn is None:
            _, vjp = jax.vjp(lambda *d: tuple(fn(*d)), *tv, *vv)
            grads = vjp(tuple(cts))
        else:
            grads = bwd_fn(*tv, *vv, *cts)
        for g, o in zip(grads[:nt], os[:nt]):
            o[0] = g.astype(o.dtype)

        @pl.when(b == 0)
        def _():
            for o in os[nt:]:
                o[...] = jnp.zeros_like(o)

        for g, o in zip(grads[nt:], os[nt:]):
            o[...] += g

    out = pl.pallas_call(
        body, name=name, grid=(nc // cb, nb), in_specs=specs, out_specs=out_specs, out_shape=out_shapes,
        compiler_params=_cparams(("parallel", "arbitrary")),
    )(*args)
    return out[:nt], out[nt:]


EXP2_SCALE = ATTN_SCALE * math.log2(math.e)


HEAD_TILE = 128
N_HEAD_PAIRS = N_ATTN_HEADS // 2


def _head_lanes():
    lane = lax.broadcasted_iota(jnp.int32, (1, HEAD_TILE), 1)
    return lane < QK_NOPE_DIM, (lane >= QK_NOPE_DIM) & (lane < QK_DIM)


def _attn_specs(tq, lk):
    q = pl.BlockSpec((1, tq, 2 * HEAD_TILE), lambda b, pr, j: (b, j, pr))
    kv = pl.BlockSpec((1, lk, 2 * HEAD_TILE), lambda b, pr, j: (b, 0, pr))
    kr = pl.BlockSpec((1, lk, HEAD_TILE), lambda b, pr, j: (b, 0, 0))
    o = pl.BlockSpec((1, tq, HEAD_TILE), lambda b, pr, j: (b, j, pr))
    lse = pl.BlockSpec((1, 2, tq, 1), lambda b, pr, j: (b, pr, j, 0))
    return q, kv, kr, o, lse


def _grid_marks(nb, nj):
    b, pr, j = pl.program_id(0), pl.program_id(1), pl.program_id(2)
    first = (b == 0) & (pr == 0) & (j == 0)
    middle = (b == nb // 2) & (pr == 0) & (j == 0)
    last = (b == nb - 1) & (pr == N_HEAD_PAIRS - 1) & (j == nj - 1)
    return first, middle, last


def _attn_fwd(q, kv, kr, late_shard, *, tq, name):
    nb, s, _ = q.shape
    lk = kv.shape[1]
    nj = s // tq
    qs, kvs, krs, os_, lses = _attn_specs(tq, lk)

    def body(q_ref, kv_ref, kr_ref, shard_ref, o_ref, lse_ref, gathered_ref, send_sems, recv_sems):
        start, relay, finish = _gather_stage(shard_ref, gathered_ref, send_sems, recv_sems)
        first, middle, last = _grid_marks(nb, nj)
        pl.when(first)(start)
        pl.when(middle)(relay)
        low, _ = _head_lanes()
        outs = []
        for e in range(2):
            tile = pl.ds(HEAD_TILE * e, HEAD_TILE)
            kv_e = kv_ref[0, :, tile]
            keys = jnp.where(low, kv_e, kr_ref[0])
            sc = _dot(q_ref[0, :, tile], keys, ((1,), (1,)))
            m = jnp.max(sc, axis=-1, keepdims=True)
            p = jnp.exp2((sc - m) * EXP2_SCALE)
            denom = jnp.sum(p, axis=-1, keepdims=True)
            outs.append(_dot(p, kv_e, ((1,), (0,))) / denom)
            lse_ref[0, e] = m * EXP2_SCALE + jnp.log2(denom)
        o_ref[0] = jnp.where(low, pltpu.roll(outs[0], V_HEAD_DIM, 1), outs[1])
        pl.when(last)(finish)

    return pl.pallas_call(
        body, name=name, grid=(nb, N_HEAD_PAIRS, nj), in_specs=[qs, kvs, krs, ANY], out_specs=[os_, lses, ANY],
        out_shape=[jax.ShapeDtypeStruct((nb, s, ATTN_WIDTH), F32), jax.ShapeDtypeStruct((nb, N_ATTN_HEADS, s, 1), F32),
                   jax.ShapeDtypeStruct((N_CHIPS,) + late_shard.shape, late_shard.dtype)],
        scratch_shapes=GATHER_SEMS,
        compiler_params=_cparams(("arbitrary", "arbitrary", "arbitrary")),
    )(q, kv, kr, late_shard)


def _attn_bwd(q, kv, kr, o, lse, do, early_sums, *, tq, name):
    nb, s, _ = q.shape
    lk = kv.shape[1]
    nj = s // tq
    qs, kvs, krs, os_, lses = _attn_specs(tq, lk)

    def body(q_ref, kv_ref, kr_ref, o_ref, lse_ref, do_ref, sums_ref, dq_ref, dkv_ref, dkr_ref, scattered_ref,
             send_sems, recv_sems):
        start, finish = _scatter_stage(sums_ref, scattered_ref, send_sems, recv_sems)
        first, _, last = _grid_marks(nb, nj)
        pl.when(first)(start)
        pr, j = pl.program_id(1), pl.program_id(2)
        low, rope = _head_lanes()
        do_pair = do_ref[0]
        prod = do_pair * o_ref[0]

        @pl.when(j == 0)
        def _():
            dkv_ref[...] = jnp.zeros_like(dkv_ref)

        @pl.when((pr == 0) & (j == 0))
        def _():
            dkr_ref[...] = jnp.zeros_like(dkr_ref)

        dkr = None
        for e in range(2):
            tile = pl.ds(HEAD_TILE * e, HEAD_TILE)
            delta = jnp.sum(jnp.where(low if e == 0 else ~low, prod, 0.0), axis=-1, keepdims=True)
            do_e = jnp.where(low, 0.0, do_pair if e == 1 else pltpu.roll(do_pair, V_HEAD_DIM, 1))
            kv_e, q_e = kv_ref[0, :, tile], q_ref[0, :, tile]
            keys = jnp.where(low, kv_e, kr_ref[0])
            sc = _dot(q_e, keys, ((1,), (1,)))
            p = jnp.exp2(sc * EXP2_SCALE - lse_ref[0, e])
            dp = _dot(do_e, kv_e, ((1,), (1,)))
            ds = (p * (dp - delta)).astype(MXU_DTYPE)
            dq_ref[0, :, tile] = _dot(ds, keys, ((1,), (0,))) * ATTN_SCALE
            dkeys = _dot(ds, q_e, ((0,), (0,)))
            dv = _dot(p, do_e, ((0,), (0,)))
            dkv_ref[0, :, tile] += jnp.where(low, dkeys, dv)
            part = jnp.where(rope, dkeys, 0.0)
            dkr = part if dkr is None else dkr + part
        dkr_ref[0] += dkr

        @pl.when(j == nj - 1)
        def _():
            for e in range(2):
                tile = pl.ds(HEAD_TILE * e, HEAD_TILE)
                dkv_ref[0, :, tile] = dkv_ref[0, :, tile] * jnp.where(low, ATTN_SCALE, 1.0)

        @pl.when((pr == N_HEAD_PAIRS - 1) & (j == nj - 1))
        def _():
            dkr_ref[0] = dkr_ref[0] * ATTN_SCALE

        pl.when(last)(finish)

    return pl.pallas_call(
        body, name=name, grid=(nb, N_HEAD_PAIRS, nj), in_specs=[qs, kvs, krs, os_, lses, os_, ANY],
        out_specs=[qs, kvs, krs, ANY],
        out_shape=[jax.ShapeDtypeStruct(q.shape, F32), jax.ShapeDtypeStruct(kv.shape, F32),
                   jax.ShapeDtypeStruct(kr.shape, F32), jax.ShapeDtypeStruct(early_sums.shape, early_sums.dtype)],
        scratch_shapes=SCATTER_SEMS,
        compiler_params=_cparams(("arbitrary", "arbitrary", "arbitrary")),
    )(q, kv, kr, o, lse, do, early_sums)


N_PAIRS = HEADS_PER_GROUP // 2
PAIR_W = 2 * SSD_HEAD_DIM


def _ssd_chunk(states, xs, dtc, dtr, bm, cm, ac, ar, *, reverse):
    q = dtc.shape[0]
    assert q == PAIR_W == dtr.shape[1]
    row = lax.broadcasted_iota(jnp.int32, (q, q), 0)
    col = lax.broadcasted_iota(jnp.int32, (q, q), 1)
    if reverse:
        tri_c, tri_r, mask = col < row, row < col, col >= row
    else:
        tri_c, tri_r, mask = col <= row, row <= col, col <= row
    a_col, a_row = dtc * ac, dtr * ar
    tri_c, tri_r = tri_c.astype(F32), tri_r.astype(F32)
    cum_c = _mask_dot(tri_c, tri_r, a_col, True)
    cum_r = _mask_dot(tri_r, tri_c, a_row, False)
    tot_c = jnp.sum(a_col, axis=0, keepdims=True)
    tot_r = jnp.sum(a_row, axis=1, keepdims=True)
    cb = _dot(cm, bm, ((1,), (1,)))
    first = lax.broadcasted_iota(jnp.int32, (1, PAIR_W), 1) < SSD_HEAD_DIM
    first_rows = lax.broadcasted_iota(jnp.int32, (PAIR_W, 1), 0) < SSD_HEAD_DIM
    heads, pairs = range(HEADS_PER_GROUP), range(N_PAIRS)
    tile = [slice(PAIR_W * pr, PAIR_W * (pr + 1)) for pr in pairs]
    cum_p = [cum_c[:, tile[pr]] for pr in pairs]
    swapped = [_roll_lanes(cum_p[pr], SSD_HEAD_DIM) for pr in pairs]
    cc = [jnp.where(first, cum_p[e // 2], swapped[e // 2]) if e % 2 == 0
          else jnp.where(first, swapped[e // 2], cum_p[e // 2]) for e in heads]
    cr = [_row_of(cum_r, e) for e in heads]
    if reverse:
        within = [jnp.exp(jnp.where(mask, cr[e] - cc[e], -jnp.inf)) for e in heads]
        into = [jnp.exp(tot_c[:, tile[pr]] - cum_p[pr]) for pr in pairs]
        to_end = [jnp.exp(cum_p[pr]) for pr in pairs]
    else:
        within = [jnp.exp(jnp.where(mask, cc[e] - cr[e], -jnp.inf)) for e in heads]
        into = [jnp.exp(cum_p[pr]) for pr in pairs]
        to_end = [jnp.exp(tot_c[:, tile[pr]] - cum_p[pr]) for pr in pairs]
    decay = [cb * within[e] for e in heads]
    carry = [jnp.exp(_row_of(tot_r, e)) for e in heads]
    xd = [xs[pr] * dtc[:, tile[pr]] for pr in pairs]
    y_even = [_dot(decay[2 * pr], jnp.where(first, xd[pr], 0.0), ((1,), (0,))) for pr in pairs]
    y_odd = [_dot(decay[2 * pr + 1], jnp.where(first, 0.0, xd[pr]), ((1,), (0,))) for pr in pairs]
    y_off = [_dot(cm, states[pr], ((1,), (1,))) for pr in pairs]
    grow = [_dot(xd[pr] * to_end[pr], bm, ((0,), (0,))) for pr in pairs]
    ys = [y_even[pr] + y_odd[pr] + y_off[pr] * into[pr] for pr in pairs]
    new_states = [states[pr] * jnp.where(first_rows, carry[2 * pr], carry[2 * pr + 1]) + grow[pr] for pr in pairs]
    return tuple(ys) + tuple(new_states)


def _chunk_of_step(t, ncc, nch, reverse):
    if not reverse:
        return t
    return jnp.where(t < ncc, ncc - 1 - t, nch - 1 - (t - ncc))


X_COLS = D_INNER // SSD_GROUPS
GROUP_COLS = X_COLS + 2 * SSD_STATE


def _scan_in_specs(nch, ncc, reverse, back):
    q, e = SSD_CHUNK, HEADS_PER_GROUP

    def ch(t):
        return _chunk_of_step((nch - 1 - t) if back else t, ncc, nch, reverse)

    return ch, [
        pl.BlockSpec((1, q, GROUP_COLS), lambda b, g, t: (b, ch(t), g)),
        pl.BlockSpec((1, q, X_COLS), lambda b, g, t: (b, ch(t), g)),
        pl.BlockSpec((1, 1, e, q), lambda b, g, t: (b, g, 0, ch(t))),
        pl.BlockSpec((1, 1, X_COLS), lambda b, g, t: (g, 0, 0)),
        pl.BlockSpec((1, e, 1), lambda b, g, t: (g, 0, 0)),
        pl.BlockSpec((1, 1, X_COLS), lambda b, g, t: (g, 0, 0)),
    ]


def _scan_chunk_fn(reverse, skip):
    def f(states, xs, dtc, dtr, bm, cm, ac, ar, d):
        res = _ssd_chunk(states, xs, dtc, dtr, bm, cm, ac, ar, reverse=reverse)
        if not skip:
            return res
        ys = tuple(res[i] + d[:, PAIR_W * i:PAIR_W * (i + 1)] * xs[i] for i in range(N_PAIRS))
        return ys + tuple(res[N_PAIRS:])

    return f


def _scan_operands(x_ref, dtc_ref, dtr_ref, ac_ref, ar_ref, d_ref):
    xs = [x_ref[0, :, pl.ds(PAIR_W * i, PAIR_W)] for i in range(N_PAIRS)]
    bm = x_ref[0, :, pl.ds(X_COLS, SSD_STATE)]
    cm = x_ref[0, :, pl.ds(X_COLS + SSD_STATE, SSD_STATE)]
    return xs, dtc_ref[0], dtr_ref[0, 0], bm, cm, ac_ref[0], ar_ref[0], d_ref[0]


N_IN = 6


def _scan_fwd(dirs, *, ncc, name):
    nb, lt, _ = dirs[0][0].shape
    q, n = SSD_CHUNK, SSD_STATE
    nch = lt // q
    in_specs, out_specs, out_shapes, fs = [], [], [], []
    for dr in range(2):
        ch, specs = _scan_in_specs(nch, ncc, bool(dr), False)
        in_specs += specs
        fs.append(_scan_chunk_fn(bool(dr), dr == 0))
        out_specs += [pl.BlockSpec((1, q, X_COLS), lambda b, g, t, ch=ch: (b, ch(t), g)),
                      pl.BlockSpec((1, 1, 1, N_PAIRS, PAIR_W, n), lambda b, g, t: (b, g, t, 0, 0, 0))]
        out_shapes += [jax.ShapeDtypeStruct((nb, lt, D_INNER), F32),
                       jax.ShapeDtypeStruct((nb, SSD_GROUPS, nch, N_PAIRS, PAIR_W, n), F32)]

    def body(*refs):
        ins, outs, sts = refs[:2 * N_IN], refs[2 * N_IN:2 * N_IN + 4], refs[2 * N_IN + 4:]
        t = pl.program_id(2)

        @pl.when(t == 0)
        def _():
            for st_ref in sts:
                st_ref[...] = jnp.zeros_like(st_ref)

        entering = [[sts[dr][i] for i in range(N_PAIRS)] for dr in range(2)]
        results = [fs[dr](entering[dr], *_scan_operands(*ins[N_IN * dr:N_IN * (dr + 1)])) for dr in range(2)]
        for dr in range(2):
            (y_ref, ent_ref), st_ref = outs[2 * dr:2 * dr + 2], sts[dr]
            for i in range(N_PAIRS):
                ent_ref[0, 0, 0, i] = entering[dr][i]
                y_ref[0, :, pl.ds(PAIR_W * i, PAIR_W)] = results[dr][i]
                st_ref[i] = results[dr][N_PAIRS + i]

    out = pl.pallas_call(
        body, name=name, grid=(nb, SSD_GROUPS, nch), in_specs=in_specs, out_specs=out_specs, out_shape=out_shapes,
        scratch_shapes=[pltpu.VMEM((N_PAIRS, PAIR_W, n), F32)] * 2,
        compiler_params=_cparams(("parallel", "parallel", "arbitrary")),
    )(*dirs[0], *dirs[1])
    return out[:2], out[2:]


N_SCAN_GRADS = 6


def _scan_bwd(dirs, entering, dy, *, ncc, name):
    nb, lt, _ = dirs[0][0].shape
    q, n, e = SSD_CHUNK, SSD_STATE, HEADS_PER_GROUP
    nch = lt // q
    in_specs, out_specs, out_shapes, fs, args = [], [], [], [], []
    for dr in range(2):
        ch, specs = _scan_in_specs(nch, ncc, bool(dr), True)
        in_specs += specs + [
            pl.BlockSpec((1, 1, 1, N_PAIRS, PAIR_W, n), lambda b, g, t: (b, g, nch - 1 - t, 0, 0, 0)),
            pl.BlockSpec((1, q, X_COLS), lambda b, g, t, ch=ch: (b, ch(t), g))]
        args += list(dirs[dr]) + [entering[dr], dy]
        fs.append(_scan_chunk_fn(bool(dr), dr == 0))
        out_specs += [pl.BlockSpec((1, q, GROUP_COLS), lambda b, g, t, ch=ch: (b, ch(t), g)),
                      pl.BlockSpec((1, q, X_COLS), lambda b, g, t, ch=ch: (b, ch(t), g)),
                      pl.BlockSpec((1, 1, e, q), lambda b, g, t, ch=ch: (b, g, 0, ch(t))),
                      pl.BlockSpec((1, 1, 1, X_COLS), lambda b, g, t: (b, g, 0, 0)),
                      pl.BlockSpec((1, 1, e, 1), lambda b, g, t: (b, g, 0, 0)),
                      pl.BlockSpec((1, 1, 1, X_COLS), lambda b, g, t: (b, g, 0, 0))]
        out_shapes += [jax.ShapeDtypeStruct((nb, lt, SSD_GROUPS * GROUP_COLS), F32),
                       jax.ShapeDtypeStruct((nb, lt, D_INNER), F32), jax.ShapeDtypeStruct((nb, SSD_GROUPS, e, lt), F32),
                       jax.ShapeDtypeStruct((nb, SSD_GROUPS, 1, X_COLS), F32), jax.ShapeDtypeStruct((nb, SSD_GROUPS, e, 1), F32),
                       jax.ShapeDtypeStruct((nb, SSD_GROUPS, 1, X_COLS), F32)]
    n_in = N_IN + 2

    def body(*refs):
        ins = refs[:2 * n_in]
        outs = refs[2 * n_in:2 * n_in + 2 * N_SCAN_GRADS]
        dss = refs[2 * n_in + 2 * N_SCAN_GRADS:]
        t = pl.program_id(2)

        for dr in range(2):
            mine = ins[n_in * dr:n_in * (dr + 1)]
            ent_ref, dy_ref = mine[N_IN], mine[N_IN + 1]
            dx_ref, ddtc_ref, ddtr_ref, dac_ref, dar_ref, dd_ref = outs[N_SCAN_GRADS * dr:N_SCAN_GRADS * (dr + 1)]
            ds_ref = dss[dr]

            @pl.when(t == 0)
            def _():
                for ref in (ds_ref, dac_ref, dar_ref, dd_ref):
                    ref[...] = jnp.zeros_like(ref)

            states = [ent_ref[0, 0, 0, i] for i in range(N_PAIRS)]
            _, vjp = jax.vjp(fs[dr], states, *_scan_operands(*mine[:N_IN]))
            dys = [dy_ref[0, :, pl.ds(PAIR_W * i, PAIR_W)] for i in range(N_PAIRS)]
            gs, gx, gdtc, gdtr, gb, gc, gac, gar, gd = vjp(tuple(dys) + tuple(ds_ref[i] for i in range(N_PAIRS)))
            o = 0
            for part in list(gx) + [gb, gc]:
                dx_ref[0, :, pl.ds(o, part.shape[1])] = part
                o += part.shape[1]
            for i in range(N_PAIRS):
                ds_ref[i] = gs[i]
            ddtc_ref[0] = gdtc
            ddtr_ref[0, 0] = gdtr
            dac_ref[0, 0] += gac
            dar_ref[0, 0] += gar
            dd_ref[0, 0] += gd

    out = pl.pallas_call(
        body, name=name, grid=(nb, SSD_GROUPS, nch), in_specs=in_specs, out_specs=out_specs, out_shape=out_shapes,
        scratch_shapes=[pltpu.VMEM((N_PAIRS, PAIR_W, n), F32)] * 2,
        compiler_params=_cparams(("parallel", "parallel", "arbitrary")),
    )(*args)
    return out[:N_SCAN_GRADS], out[N_SCAN_GRADS:]


def _adamw(w, g, m, v, *, name):
    r, c = w.shape
    tr = _pick(r, (256, 176, 128, 96, 64, 8))
    c1 = 1.0 / (1.0 - ADAM_B1 ** ADAM_STEP)
    c2 = 1.0 / (1.0 - ADAM_B2 ** ADAM_STEP)

    def body(w_ref, g_ref, m_ref, v_ref, d_ref, nm_ref, nv_ref):
        gv = g_ref[...]
        nm = ADAM_B1 * m_ref[...] + (1.0 - ADAM_B1) * gv
        nv = ADAM_B2 * v_ref[...] + (1.0 - ADAM_B2) * (gv * gv)
        d_ref[...] = -ADAM_LR * ((nm * c1) / (jnp.sqrt(nv * c2) + ADAM_EPS) + ADAM_WD * w_ref[...])
        nm_ref[...] = nm
        nv_ref[...] = nv

    spec = pl.BlockSpec((tr, c), lambda i: (i, 0))
    return pl.pallas_call(
        body, name=name, grid=(r // tr,), in_specs=[spec] * 4, out_specs=[spec] * 3,
        out_shape=[jax.ShapeDtypeStruct((r, c), F32)] * 3, compiler_params=_cparams(("parallel",)),
    )(w, g, m, v)


def _sum_rows_tile(r):
    return r if r <= 1024 else _pick(r, (656, 512, 256, 128, 64, 32, 16))


def _sum_slots(x, *, out_dtype, name):
    n, r, c = x.shape
    tr = _sum_rows_tile(r)

    def body(x_ref, o_ref):
        acc = x_ref[0].astype(F32)
        for k in range(1, n):
            acc = acc + x_ref[k].astype(F32)
        o_ref[...] = acc.astype(o_ref.dtype)

    return pl.pallas_call(
        body, name=name, grid=(r // tr,), in_specs=[pl.BlockSpec((n, tr, c), lambda i: (0, i, 0))],
        out_specs=pl.BlockSpec((tr, c), lambda i: (i, 0)), out_shape=jax.ShapeDtypeStruct((r, c), out_dtype),
        compiler_params=_cparams(("parallel",)),
    )(x)


def _sum_list(xs, *, out_dtype, name):
    r, c = xs[0].shape
    tr = _sum_rows_tile(r)

    def body(*refs):
        acc = refs[0][...].astype(F32)
        for ref in refs[1:-1]:
            acc = acc + ref[...].astype(F32)
        refs[-1][...] = acc.astype(refs[-1].dtype)

    spec = pl.BlockSpec((tr, c), lambda i: (i, 0))
    return pl.pallas_call(
        body, name=name, grid=(r // tr,), in_specs=[spec] * len(xs), out_specs=spec,
        out_shape=jax.ShapeDtypeStruct((r, c), out_dtype), compiler_params=_cparams(("parallel",)),
    )(*xs)


ANY = pl.BlockSpec(memory_space=pl.ANY)


def _place():
    return lax.axis_index("x"), lax.axis_index("y"), lax.axis_index("c")


def _allgather_small(v, *, name):
    r, c = v.shape

    def body(v_ref, out_ref, send_sems, recv_sems, local_sem):
        x, y, cc = _place()
        me = 4 * x + 2 * y + cc
        mine = pltpu.make_async_copy(v_ref, out_ref.at[me], local_sem)
        mine.start()
        copies = []
        for k in range(1, N_DEV):
            fx, fy, fc = (k >> 2) & 1, (k >> 1) & 1, k & 1
            peer = (1 - x if fx else x, 1 - y if fy else y, 1 - cc if fc else cc)
            copies.append(pltpu.make_async_remote_copy(
                src_ref=v_ref, dst_ref=out_ref.at[me], send_sem=send_sems.at[k - 1], recv_sem=recv_sems.at[k - 1],
                device_id=peer, device_id_type=MESH))
        for cp in copies:
            cp.start()
        for cp in copies:
            cp.wait()
        mine.wait()

    return pl.pallas_call(
        body, name=name, in_specs=[ANY], out_specs=ANY, out_shape=jax.ShapeDtypeStruct((N_DEV, r, c), v.dtype),
        scratch_shapes=[pltpu.SemaphoreType.DMA((N_DEV - 1,)), pltpu.SemaphoreType.DMA((N_DEV - 1,)),
                        pltpu.SemaphoreType.DMA],
    )(v)


def _other_chips(x, y):
    return [(1 - x, y), (x, 1 - y), (1 - x, 1 - y)]


GATHER_SEMS = [pltpu.SemaphoreType.DMA((6,)), pltpu.SemaphoreType.DMA((6,))]
SCATTER_SEMS = [pltpu.SemaphoreType.DMA((3,)), pltpu.SemaphoreType.DMA((3,))]


def _gather_stage(v_ref, out_ref, send_sems, recv_sems):
    half = v_ref.shape[0] // 2
    x, y, cc = _place()
    sibling = (x, y, 1 - cc)
    chips = _other_chips(x, y)

    def rows(px, py, pc):
        return out_ref.at[2 * px + py, pl.ds(pc * half, half), :]

    def copy(k, block, to, src=None):
        return pltpu.make_async_remote_copy(
            src_ref=rows(*block) if src is None else src, dst_ref=rows(*block),
            send_sem=send_sems.at[k], recv_sem=recv_sems.at[k], device_id=to, device_id_type=MESH)

    my_half = v_ref.at[pl.ds(cc * half, half), :]
    first = [copy(j, (x, y, cc), (*chip, cc), src=my_half) for j, chip in enumerate(chips)]
    passed = [copy(3 + j, (*chip, cc), sibling) for j, chip in enumerate(chips)]

    def start():
        for cp in first:
            cp.start()

    def relay():
        for j, chip in enumerate(chips):
            copy(j, (*chip, cc), (x, y, cc)).wait_recv()
            passed[j].start()

    def finish():
        for j, chip in enumerate(chips):
            copy(3 + j, (*chip, 1 - cc), (x, y, cc)).wait_recv()
        for cp in first + passed:
            cp.wait_send()

    return start, relay, finish


def _gather_shards(mine, *, name):
    r, c = mine.shape

    def body(v_ref, out_ref, send_sems, recv_sems):
        for phase in _gather_stage(v_ref, out_ref, send_sems, recv_sems):
            phase()

    return pl.pallas_call(
        body, name=name, in_specs=[ANY], out_specs=ANY, out_shape=jax.ShapeDtypeStruct((N_CHIPS, r, c), mine.dtype),
        scratch_shapes=GATHER_SEMS,
    )(mine)


def _swap_halves(g, *, name):
    n, _, r, c = g.shape

    def body(g_ref, got_ref, send_sems, recv_sems):
        x, y, cc = _place()
        sibling = (x, y, 1 - cc)
        rems = []
        for j in range(n):
            rems.append(pltpu.make_async_remote_copy(
                src_ref=g_ref.at[j, 1 - cc], dst_ref=got_ref.at[j], send_sem=send_sems.at[j],
                recv_sem=recv_sems.at[j], device_id=sibling, device_id_type=MESH))
        for cp in rems:
            cp.start()
        for cp in rems:
            cp.wait()

    return pl.pallas_call(
        body, name=name, in_specs=[ANY], out_specs=ANY, out_shape=jax.ShapeDtypeStruct((n, r, c), g.dtype),
        scratch_shapes=[pltpu.SemaphoreType.DMA((n,)), pltpu.SemaphoreType.DMA((n,))],
    )(g)


def _scatter_stage(s_ref, out_ref, send_sems, recv_sems):
    x, y, cc = _place()
    me = 2 * x + y
    copies = [pltpu.make_async_remote_copy(
        src_ref=s_ref.at[2 * px + py], dst_ref=out_ref.at[me], send_sem=send_sems.at[j], recv_sem=recv_sems.at[j],
        device_id=(px, py, cc), device_id_type=MESH) for j, (px, py) in enumerate(_other_chips(x, y))]

    def start():
        for cp in copies:
            cp.start()

    def finish():
        for cp in copies:
            cp.wait()

    return start, finish


def _scatter_to_chips(s, *, name):
    def body(s_ref, out_ref, send_sems, recv_sems):
        for phase in _scatter_stage(s_ref, out_ref, send_sems, recv_sems):
            phase()

    return pl.pallas_call(
        body, name=name, in_specs=[ANY], out_specs=ANY, out_shape=jax.ShapeDtypeStruct(s.shape, s.dtype),
        scratch_shapes=SCATTER_SEMS,
    )(s)


def _join_halves(f, *, name):
    r, c = f.shape

    def body(f_ref, out_ref, send_sem, recv_sem):
        x, y, cc = _place()
        cp = pltpu.make_async_remote_copy(src_ref=f_ref, dst_ref=out_ref.at[cc], send_sem=send_sem, recv_sem=recv_sem,
                                          device_id=(x, y, 1 - cc), device_id_type=MESH)
        cp.start()
        cp.wait()

    return pl.pallas_call(
        body, name=name, in_specs=[ANY], out_specs=ANY, out_shape=jax.ShapeDtypeStruct((2, r, c), f.dtype),
        scratch_shapes=[pltpu.SemaphoreType.DMA, pltpu.SemaphoreType.DMA],
    )(f)


def _pack_rows(parts, width=PACK_COLS):
    return jnp.concatenate([p.reshape(-1, width) for p in parts], axis=0)


def _pack_small(parts, rows):
    flat = jnp.concatenate([p.reshape(-1).astype(F32) for p in parts])
    return jnp.pad(flat, (0, rows * LANES - flat.shape[0])).reshape(rows, LANES)


def _unpack_small(packed, shapes):
    flat = packed.reshape(-1)
    out, o = [], 0
    for shp in shapes:
        n = int(np.prod(shp))
        out.append(flat[o:o + n].reshape(shp))
        o += n
    return out


def _perm_in_cols(w):
    a, b = Q_LORA_RANK + KV_LORA_RANK, Q_LORA_RANK + KV_LORA_RANK + QK_ROPE_DIM
    c = IN_WIDTH - 2 * N_SSD_HEADS
    return jnp.concatenate([w[:, :a], w[:, b:c], w[:, a:b], w[:, c:]], axis=1)


def _unperm_in_cols(w):
    a = Q_LORA_RANK + KV_LORA_RANK
    zx = D_INNER + XBC_WIDTH
    return jnp.concatenate([w[:, :a], w[:, a + zx:a + zx + QK_ROPE_DIM], w[:, a:a + zx], w[:, a + zx + QK_ROPE_DIM:]],
                           axis=1)


def _group_xbc(a):
    n = SSD_STATE
    parts = []
    for g in range(SSD_GROUPS):
        parts += [a[..., g * X_COLS:(g + 1) * X_COLS], a[..., D_INNER + g * n:D_INNER + (g + 1) * n],
                  a[..., D_INNER + GN + g * n:D_INNER + GN + (g + 1) * n]]
    return jnp.concatenate(parts, axis=-1)


def _ungroup_xbc(a):
    n = SSD_STATE
    xs = [a[..., g * GROUP_COLS:g * GROUP_COLS + X_COLS] for g in range(SSD_GROUPS)]
    bs = [a[..., g * GROUP_COLS + X_COLS:g * GROUP_COLS + X_COLS + n] for g in range(SSD_GROUPS)]
    cs = [a[..., g * GROUP_COLS + X_COLS + n:(g + 1) * GROUP_COLS] for g in range(SSD_GROUPS)]
    return jnp.concatenate(xs + bs + cs, axis=-1)


UP_BLOCK = 256


def _interleave_up(w):
    parts = []
    for j in range(D_FF // UP_BLOCK):
        parts += [w[:, j * UP_BLOCK:(j + 1) * UP_BLOCK], w[:, D_FF + j * UP_BLOCK:D_FF + (j + 1) * UP_BLOCK]]
    return jnp.concatenate(parts, axis=1)


def _deinterleave_up(w):
    blocks = [w[:, j * UP_BLOCK:(j + 1) * UP_BLOCK] for j in range(2 * D_FF // UP_BLOCK)]
    return jnp.concatenate(blocks[0::2] + blocks[1::2], axis=1)


def _pad_q_heads(w):
    k = w.shape[0]
    return jnp.pad(w.reshape(k, N_ATTN_HEADS, QK_DIM), ((0, 0), (0, 0), (0, HEAD_TILE - QK_DIM))).reshape(k, -1)


def _unpad_q_heads(w):
    k = w.shape[0]
    return w.reshape(k, N_ATTN_HEADS, HEAD_TILE)[..., :QK_DIM].reshape(k, N_ATTN_HEADS * QK_DIM)


def _rope_tables(seq_len):
    n_rows = seq_len // GRID_W
    row = jnp.repeat(jnp.arange(n_rows), GRID_W).astype(F32)
    col = jnp.tile(jnp.arange(GRID_W), n_rows).astype(F32)
    axis_dim = QK_ROPE_DIM // 2
    inv_freq = ROPE_THETA ** (-jnp.arange(0, axis_dim, 2, dtype=F32) / axis_dim)
    ang_r = row[:, None] * inv_freq
    ang_c = col[:, None] * inv_freq
    ang = jnp.concatenate([ang_r, ang_r, ang_c, ang_c], axis=-1)
    return jnp.cos(ang), jnp.sin(ang)


def _rot_matrix(width, start):
    r = np.zeros((width, width), np.float32)
    quarter = QK_ROPE_DIM // 4
    for base in (0, QK_ROPE_DIM // 2):
        for i in range(quarter):
            r[start + base + quarter + i, start + base + i] = -1.0
            r[start + base + i, start + base + quarter + i] = 1.0
    return jnp.asarray(r)


ROPE_STEP = QK_ROPE_DIM // 4


def _rope_flat_fn(x, cos, sin_up, sin_down):
    reps = x.shape[1] // cos.shape[1]

    def heads(t):
        return jnp.concatenate([t] * reps, axis=1)

    return (x * heads(cos) + _roll_lanes(x, -ROPE_STEP) * heads(sin_up) + _roll_lanes(x, ROPE_STEP) * heads(sin_down),)


def _rope_flat_transpose_fn(g, cos, sin_up, sin_down):
    reps = g.shape[1] // cos.shape[1]

    def heads(t):
        return jnp.concatenate([t] * reps, axis=1)

    return (g * heads(cos) + _roll_lanes(g * heads(sin_up), ROPE_STEP) + _roll_lanes(g * heads(sin_down), -ROPE_STEP),)


def _krdt_fn(x, cos, sin, rot, bias):
    lane = lax.broadcasted_iota(jnp.int32, (1, KRDT_WIDTH), 1)
    is_dt = (lane >= QK_ROPE_DIM) & (lane < QK_ROPE_DIM + 2 * N_SSD_HEADS)
    roped = x * cos + _dot_exact(x, rot) * sin
    return (jnp.where(is_dt, _softplus(x + bias), roped),)


def _pre_fn(u, w, shift, scale):
    return (_rms(u, w) * (1.0 + scale) + shift,)


def _norm_fn(x, w):
    return (_rms(x, w),)


def _finish_fn(yf, yb, z, w):
    return (_rms((yf + yb) * _silu(z), w),)


def _mid_fn(x, mix, w_post, w_pre, gate, shift, scale):
    x1 = x + gate * _rms(mix, w_post)
    return (x1, _rms(x1, w_pre) * (1.0 + scale) + shift)


def _loss_fn(x1, ffn, tgt, w_post, gate):
    y = x1 + gate * _rms(ffn, w_post)
    err = y - tgt
    return (0.5 * jnp.mean(err * err, axis=-1, keepdims=True),)


def _bias_fn(x, b):
    return (x + b,)


def _silu_fn(x):
    return (_silu(x),)


def kernel(x, c, ctx, c_ctx, w_mod, b_mod, mix_pre_norm, mix_post_norm, w_in, q_norm, w_q_up, kv_norm, w_kv_up, ssd_conv_w, ssd_conv_b, ssd_a_log, ssd_dt_bias, ssd_d, ssd_norm, w_out, ffn_pre_norm, ffn_post_norm, w_up, ffn_conv_w, ffn_conv_b, w_down, loss_target, m_c_ctx, m_w_mod, m_b_mod, m_mix_pre_norm, m_mix_post_norm, m_w_in, m_q_norm, m_w_q_up, m_kv_norm, m_w_kv_up, m_ssd_conv_w, m_ssd_conv_b, m_ssd_a_log, m_ssd_dt_bias, m_ssd_d, m_ssd_norm, m_w_out, m_ffn_pre_norm, m_ffn_post_norm, m_w_up, m_ffn_conv_w, m_ffn_conv_b, m_w_down, v_c_ctx, v_w_mod, v_b_mod, v_mix_pre_norm, v_mix_post_norm, v_w_in, v_q_norm, v_w_q_up, v_kv_norm, v_w_kv_up, v_ssd_conv_w, v_ssd_conv_b, v_ssd_a_log, v_ssd_dt_bias, v_ssd_d, v_ssd_norm, v_w_out, v_ffn_pre_norm, v_ffn_post_norm, v_w_up, v_ffn_conv_w, v_ffn_conv_b, v_w_down):
    args = dict(locals())
    names = ["c_ctx", "w_mod", "b_mod", "mix_pre_norm", "mix_post_norm", "w_in", "q_norm", "w_q_up", "kv_norm",
             "w_kv_up", "ssd_conv_w", "ssd_conv_b", "ssd_a_log", "ssd_dt_bias", "ssd_d", "ssd_norm", "w_out",
             "ffn_pre_norm", "ffn_post_norm", "w_up", "ffn_conv_w", "ffn_conv_b", "w_down"]
    nb, s, d = x.shape
    nctx_rows = ctx.shape[1]
    lt = nctx_rows + s
    tl = 256 if (nctx_rows % 256 == 0 and s % 256 == 0) else 128
    nctx = nctx_rows // tl
    ncc = nctx_rows // SSD_CHUNK
    h, e, g2 = N_ATTN_HEADS, HEADS_PER_GROUP, SSD_GROUPS
    chip = 2 * lax.axis_index("x") + lax.axis_index("y")

    big_local = {n: args[n][0] for n, _, _, _ in BIG}
    big_info = {n: (rows, cols, axis) for n, rows, cols, axis in BIG}

    def pack_shards(group):
        return _pack_rows([big_local[n].astype(WIRE_DTYPE) for n in group])

    def unpack_gathered(gathered, mine, group):
        gathered = lax.dynamic_update_slice(gathered, mine[None], (chip, 0, 0))
        res, o = {}, 0
        for n in group:
            rows, cols, axis = big_info[n]
            lr, lc = big_local[n].shape
            nr = lr * lc // PACK_COLS
            seg = gathered[:, o:o + nr].reshape(N_CHIPS, lr, lc)
            o += nr
            res[n] = seg.reshape(rows, cols) if axis == 0 else jnp.transpose(seg, (1, 0, 2)).reshape(rows, cols)
        return res

    core = lax.axis_index("c")

    def pair_sums(grads_full, group, tag):
        parts = []
        for n in group:
            _, _, axis = big_info[n]
            lr, lc = big_local[n].shape
            gfull = grads_full[n]
            shards = (gfull.reshape(N_CHIPS, lr, lc) if axis == 0
                      else jnp.transpose(gfull.reshape(lr, N_CHIPS, lc), (1, 0, 2)))
            parts.append(shards.reshape(N_CHIPS, lr * lc // PACK_COLS, PACK_COLS))
        gpack = jnp.concatenate(parts, axis=1).astype(WIRE_DTYPE)
        half = gpack.shape[1] // 2
        gpack = gpack.reshape(N_CHIPS, 2, half, PACK_COLS)
        got = _swap_halves(gpack, name="grad_swap_" + tag)
        own = lax.dynamic_index_in_dim(gpack, core, axis=1, keepdims=False)
        flat = (N_CHIPS * half, PACK_COLS)
        return _sum_list([own.reshape(flat), got.reshape(flat)], out_dtype=WIRE_DTYPE,
                         name="grad_add_pair_" + tag).reshape(N_CHIPS, half, PACK_COLS)

    def chip_total(sums, scattered, tag):
        mine_sum = lax.dynamic_index_in_dim(sums, chip, axis=0, keepdims=True)
        scattered = lax.dynamic_update_slice(scattered, mine_sum, (chip, 0, 0))
        return _sum_slots(scattered, out_dtype=F32, name="grad_add_chips_" + tag)

    packed_now, packed_late = pack_shards(GATHER_NOW), pack_shards(GATHER_LATE)
    full = unpack_gathered(_gather_shards(packed_now, name="gather_weights"), packed_now, GATHER_NOW)
    n_sc, n_fc = ssd_conv_w.shape[2], ffn_conv_w.shape[2]
    n_conv = SSD_CONV * n_sc + FFN_CONV * n_fc
    first_rows = -(-(n_conv + nb * d) // (8 * LANES)) * 8
    first_all = _allgather_small(_pack_small([ssd_conv_w[0], ffn_conv_w[0], c], first_rows), name="gather_conv_c")
    first_all = first_all.reshape(N_DEV, -1)
    conv_all = first_all[::2]
    ssd_conv_full = jnp.concatenate(
        [conv_all[j][:SSD_CONV * n_sc].reshape(SSD_CONV, n_sc) for j in range(N_CHIPS)], axis=1)
    ffn_conv_full = jnp.concatenate(
        [conv_all[j][SSD_CONV * n_sc:n_conv].reshape(FFN_CONV, n_fc) for j in range(N_CHIPS)], axis=1)
    c_every = first_all[:, n_conv:n_conv + nb * d].reshape(N_DEV * nb, d)

    w_in_p = _perm_in_cols(full["w_in"])
    o_cq, o_ckv, o_z = 0, Q_LORA_RANK, Q_LORA_RANK + KV_LORA_RANK
    o_xbc, o_kr = o_z + D_INNER, o_z + D_INNER + XBC_WIDTH
    w_krdt = jnp.pad(w_in_p[:, o_kr:], ((0, 0), (0, KRDT_WIDTH - QK_ROPE_DIM - 2 * N_SSD_HEADS)))
    w_segs = [w_in_p[:, o_cq:o_ckv], w_in_p[:, o_ckv:o_z], w_in_p[:, o_z:o_xbc], _group_xbc(w_in_p[:, o_xbc:o_kr]),
              w_krdt]
    ssd_conv_g, ssd_conv_b_g = _group_xbc(ssd_conv_full), _group_xbc(ssd_conv_b)
    w_q_pad = _pad_q_heads(full["w_q_up"])

    mod_rows = 16
    n_ex = N_DEV * nb
    all_rows = -(-(n_ex + 1) // 16) * 16
    me = 2 * chip + lax.axis_index("c")
    c_all = jnp.concatenate([c_every, c_ctx[None, :], jnp.zeros((all_rows - n_ex - 1, d), F32)], axis=0)[None]
    (s_all,) = _row_fwd(_silu_fn, toks=[(c_all, 0, None, 0)], outs=[(d, F32)], nb=1, nl=all_rows, tl=all_rows,
                        name="mod_silu")
    w_mod_local = w_mod[0]
    mod_cols = w_mod_local.shape[1]
    mod_part = _mm(s_all[0], w_mod_local, name="mod_mm")
    mod_parts = _allgather_small(mod_part, name="gather_mod")[::2]
    mod_every = jnp.concatenate([mod_parts[j] for j in range(N_CHIPS)], axis=1)
    mod_lin = jnp.concatenate([lax.dynamic_slice_in_dim(mod_every, me * nb, nb, axis=0), mod_every[n_ex:n_ex + 1],
                               jnp.zeros((mod_rows - nb - 1, N_MOD * d), F32)], axis=0)
    (mod,) = _row_fwd(_bias_fn, toks=[(mod_lin[None], 0, None, 0)], vecs=[b_mod], outs=[(N_MOD * d, F32)], nb=1,
                      nl=mod_rows, tl=mod_rows, name="mod_bias")
    mods = [mod[0][:, k * d:(k + 1) * d][:, None, :] for k in range(N_MOD)]
    mods_lat = [m[:nb] for m in mods]

    u = jnp.concatenate([ctx, x], axis=1)
    (h1,) = _row_fwd(_pre_fn, toks=[(u, 0, None, 0)], vecs=[mix_pre_norm], bvecs=[mods[0], mods[1]],
                     outs=[(d, MXU_DTYPE)], nb=nb, nl=lt, tl=tl, nctx=nctx, name="pre1")
    h1f = h1.reshape(nb * lt, d)
    p_cq, p_ckv, p_z, p_xbc, p_krdt = [
        _mm(h1f, w, name="in_" + nm).reshape(nb, lt, -1)
        for nm, w in zip(("cq", "ckv", "z", "xbc", "krdt"), w_segs)]

    (cqn,) = _row_fwd(_norm_fn, toks=[(p_cq, nctx, None, 0)], vecs=[q_norm], outs=[(Q_LORA_RANK, MXU_DTYPE)],
                      nb=nb, nl=s, tl=tl, name="q_norm")
    q_flat = _mm(cqn.reshape(nb * s, -1), w_q_pad, name="q_up").reshape(nb, s, h * HEAD_TILE)
    cos, sin = _rope_tables(s)
    ones, zeros = jnp.ones((s, QK_NOPE_DIM), F32), jnp.zeros((s, QK_NOPE_DIM), F32)
    tail = HEAD_TILE - QK_DIM
    up_lanes = ((jnp.arange(QK_ROPE_DIM) // ROPE_STEP) % 2 == 0)[None, :]
    q_tables = [jnp.concatenate([pad, t, pad[:, :tail]], axis=1)[None]
                for pad, t in ((ones, cos), (zeros, jnp.where(up_lanes, -sin, 0.0)), (zeros, jnp.where(up_lanes, 0.0, sin)))]
    tq = 256
    (q_roped,) = _row_fwd(_rope_flat_fn, toks=[(q_flat, 0, None, 0)], poss=q_tables, outs=[(h * HEAD_TILE, MXU_DTYPE)],
                          nb=nb, nl=s, tl=tl, name="rope_q")

    (ckvn,) = _row_fwd(_norm_fn, toks=[(p_ckv, 0, None, 0)], vecs=[kv_norm], outs=[(KV_LORA_RANK, MXU_DTYPE)],
                       nb=nb, nl=lt, tl=tl, name="kv_norm")
    kv_flat = _mm(ckvn.reshape(nb * lt, -1), full["w_kv_up"], out_dtype=MXU_DTYPE, name="kv_up").reshape(nb, lt, -1)

    pad_w = KRDT_WIDTH - QK_ROPE_DIM
    cos_k = jnp.concatenate([jnp.ones((nctx_rows, KRDT_WIDTH), F32),
                             jnp.concatenate([cos, jnp.ones((s, pad_w), F32)], axis=1)], axis=0)[None]
    sin_k = jnp.concatenate([jnp.zeros((nctx_rows, KRDT_WIDTH), F32),
                             jnp.concatenate([sin, jnp.zeros((s, pad_w), F32)], axis=1)], axis=0)[None]
    rot_k = _rot_matrix(KRDT_WIDTH, 0)
    dt_bias_row = jnp.pad(ssd_dt_bias.reshape(1, -1), ((0, 0), (QK_ROPE_DIM, pad_w - 2 * N_SSD_HEADS)))
    (krdt,) = _row_fwd(_krdt_fn, toks=[(p_krdt, 0, None, 0)], poss=[cos_k, sin_k], vecs=[rot_k, dt_bias_row],
                       outs=[(KRDT_WIDTH, F32)], nb=nb, nl=lt, tl=tl, name="krdt")
    kr = jnp.pad(krdt[..., :QK_ROPE_DIM].astype(MXU_DTYPE), ((0, 0), (0, 0), (QK_NOPE_DIM, HEAD_TILE - QK_DIM)))
    attn, lse, gathered_late = _attn_fwd(q_roped, kv_flat, kr, packed_late, tq=tq, name="attn_fwd")
    full.update(unpack_gathered(gathered_late, packed_late, GATHER_LATE))
    w_up_il = _interleave_up(full["w_up"])
    w_out_a, w_out_s = full["w_out"][:ATTN_WIDTH], full["w_out"][ATTN_WIDTH:]

    seg = nctx_rows

    def conv_ssd_fn(xv, w, b):
        return (_silu(_dwconv(xv, w, seg) + b),)

    def conv_ssd_bwd(xv, w, b, dy):
        cv = _dwconv(xv, w, seg) + b
        sg = _sigmoid(cv)
        dc = dy * (sg * (1.0 + cv * (1.0 - sg)))
        dx, dw = _dwconv_back(xv, dc, w, seg)
        return dx, dw, jnp.sum(dc, axis=0, keepdims=True)

    cb_ssd = 256
    conv_vecs = [(ssd_conv_g, 0), (ssd_conv_b_g, 0)]
    (xbc,) = _seq_fwd(conv_ssd_fn, toks=[(p_xbc, 0, 1)], vecs=conv_vecs, outs=[F32], nb=nb, nl=lt, nc=XBC_WIDTH,
                      cb=cb_ssd, name="conv_ssd")
    dt = krdt[..., QK_ROPE_DIM:QK_ROPE_DIM + 2 * N_SSD_HEADS].reshape(nb, lt, 2, g2, e)
    dt_lane = QK_ROPE_DIM + N_SSD_HEADS * jnp.arange(2)[:, None, None] + jnp.arange(D_INNER)[None, None, :] // SSD_HEAD_DIM
    spread = (jnp.arange(KRDT_WIDTH)[None, :, None] == dt_lane).astype(F32)

    def spread_fn(v, s0, s1):
        return (_mask_dot_raw(s0, v, False), _mask_dot_raw(s1, v, False))

    dtc = _row_fwd(spread_fn, toks=[(krdt, 0, None, 0)], vecs=[spread[0], spread[1]],
                   outs=[(D_INNER, F32), (D_INNER, F32)], nb=nb, nl=lt, tl=tl, name="dt_spread")
    dtr = jnp.transpose(dt, (2, 0, 3, 4, 1))
    a_neg = -jnp.exp(ssd_a_log[0]).reshape(2, g2, e)
    d_chan = jnp.repeat(ssd_d[0], SSD_HEAD_DIM).reshape(g2, 1, X_COLS)
    a_chan = [jnp.repeat(a_neg[dr].reshape(-1), SSD_HEAD_DIM).reshape(g2, 1, X_COLS) for dr in range(2)]
    scan_args = [(xbc, dtc[dr], dtr[dr], a_chan[dr], a_neg[dr][:, :, None], d_chan) for dr in range(2)]
    (y0, ent0), (y1, ent1) = _scan_fwd(scan_args, ncc=ncc, name="scan_fwd")
    ys, ents = [y0, y1], [ent0, ent1]
    (ssd,) = _row_fwd(_finish_fn, toks=[(ys[0], nctx, None, 0), (ys[1], nctx, None, 0), (p_z, nctx, None, 0)],
                      vecs=[ssd_norm], outs=[(D_INNER, MXU_DTYPE)], nb=nb, nl=s, tl=tl, name="ssd_finish")

    attn_f, ssd_f = attn.reshape(nb * s, ATTN_WIDTH), ssd.reshape(nb * s, D_INNER)
    mix = _mm_sum([(attn_f, w_out_a), (ssd_f, w_out_s)], name="out_proj").reshape(nb, s, d)

    mid_bvecs = [mods_lat[2], mods_lat[3], mods_lat[4]]
    x1, h2 = _row_fwd(_mid_fn, toks=[(x, 0, None, 0), (mix, 0, None, 0)], vecs=[mix_post_norm, ffn_pre_norm],
                      bvecs=mid_bvecs, outs=[(d, F32), (d, MXU_DTYPE)], nb=nb, nl=s, tl=tl, name="mid")
    up = _mm(h2.reshape(nb * s, d), w_up_il, name="ffn_up").reshape(nb, s, 2 * D_FF)

    def glu_fn(gv, w, b):
        return (_gelu(_dwconv(gv[:, :UP_BLOCK], w, 0) + b) * gv[:, UP_BLOCK:],)

    def glu_bwd(gv, w, b, da):
        gate, val = gv[:, :UP_BLOCK], gv[:, UP_BLOCK:]
        cv = _dwconv(gate, w, 0) + b
        cdf = 0.5 * (1.0 + lax.erf(cv * (2.0 ** -0.5)))
        pdf = jnp.exp(-0.5 * cv * cv) * (1.0 / math.sqrt(2.0 * math.pi))
        dc = (da * val) * (cdf + cv * pdf)
        dgate, dw = _dwconv_back(gate, dc, w, 0)
        return jnp.concatenate([dgate, da * (cv * cdf)], axis=1), dw, jnp.sum(dc, axis=0, keepdims=True)

    cb_ffn = UP_BLOCK
    glu_toks = [(up, 0, 2)]
    glu_vecs = [(ffn_conv_full, 0), (ffn_conv_b, 0)]
    (act,) = _seq_fwd(glu_fn, toks=glu_toks, vecs=glu_vecs, outs=[MXU_DTYPE], nb=nb, nl=s, nc=D_FF, cb=cb_ffn,
                      name="conv_glu")
    ffn = _mm(act.reshape(nb * s, D_FF), full["w_down"], name="ffn_down").reshape(nb, s, d)

    loss_toks = [(x1, 0, None, 0), (ffn, 0, None, 0), (loss_target, 0, None, 0)]
    ones_rows = jnp.ones((nb, s, 1), F32)
    (dx1_a, dffn, _), (g_ffn_post,), ((g_gate5, _),), (loss_rows,) = _row_bwd(
        _loss_fn, toks=loss_toks, vecs=[ffn_post_norm], bvecs=[mods_lat[5]], cots=[(ones_rows, 0)],
        tok_grads=[F32, MXU_DTYPE, None], emit=[(0, 1, F32)], nb=nb, nl=s, tl=tl, name="loss_bwd")
    loss_part = jnp.sum(loss_rows)

    dffn_f = dffn.reshape(nb * s, d)
    g_w_down = _mm(act.reshape(nb * s, D_FF), dffn_f, ta=True, name="wg_down")
    dact = _mm(dffn_f, full["w_down"], tb=True, out_dtype=MXU_DTYPE, name="dg_down").reshape(nb, s, D_FF)
    (dup,), (g_ffn_conv_w, g_ffn_conv_b) = _seq_bwd(
        glu_fn, toks=glu_toks, vecs=glu_vecs, cots=[[dact]], tok_grads=[MXU_DTYPE], nb=nb, nl=s, nc=D_FF,
        cb=cb_ffn, name="conv_glu_bwd", bwd_fn=glu_bwd)
    dup = dup.reshape(nb * s, 2 * D_FF)
    g_w_up = _deinterleave_up(_mm(h2.reshape(nb * s, d), dup, ta=True, name="wg_up"))
    dh2 = _mm(dup, w_up_il, tb=True, name="dg_up").reshape(nb, s, d)

    (dx_res, dmix), (g_mix_post, g_ffn_pre), ((g_gate2, _), (g_shift3, _), (g_scale4, _)), _ = _row_bwd(
        _mid_fn, toks=[(x, 0, None, 0), (mix, 0, None, 0)], vecs=[mix_post_norm, ffn_pre_norm], bvecs=mid_bvecs,
        cots=[(dx1_a, 0), (dh2, 0)], tok_grads=[F32, MXU_DTYPE], nb=nb, nl=s, tl=tl, name="mid_bwd")

    dmix_f = dmix.reshape(nb * s, d)
    g_w_out = jnp.concatenate([_mm(attn_f, dmix_f, ta=True, name="wg_out_attn"),
                               _mm(ssd_f, dmix_f, ta=True, name="wg_out_ssd")], axis=0)
    early_sums = pair_sums({"w_up": g_w_up, "w_down": g_w_down, "w_out": g_w_out}, REDUCE_EARLY, "early")
    dattn = _mm(dmix_f, w_out_a, tb=True, name="dg_out_attn").reshape(nb, s, ATTN_WIDTH)
    dssd = _mm(dmix_f, w_out_s, tb=True, name="dg_out_ssd").reshape(nb, s, D_INNER)

    (dy, _, dz), (g_ssd_norm,), _, _ = _row_bwd(
        _finish_fn, toks=[(ys[0], 0, None, 0), (ys[1], 0, None, 0), (p_z, 0, None, 0)], vecs=[ssd_norm],
        cots=[(dssd, -nctx)], tok_grads=[F32, None, MXU_DTYPE], nb=nb, nl=lt, tl=tl, name="ssd_finish_bwd")
    scan_grads = _scan_bwd(scan_args, ents, dy, ncc=ncc, name="scan_bwd")

    def collect_fn(g0, g1, c0, c1):
        return (_mask_dot_raw(c0, g0, False) + _mask_dot_raw(c1, g1, False),)

    (g_dt_lanes,) = _row_fwd(collect_fn, toks=[(scan_grads[0][1], 0, None, 0), (scan_grads[1][1], 0, None, 0)],
                             vecs=[spread[0].T, spread[1].T], outs=[(KRDT_WIDTH, F32)], nb=nb, nl=lt, tl=tl,
                             name="dt_collect")
    g_dt_dirs, g_a = [], []
    for _, _, gdtr, gac, gar, _ in scan_grads:
        g_dt_dirs.append(jnp.transpose(gdtr, (0, 3, 1, 2)))
        g_a.append(jnp.sum(jnp.sum(gac.reshape(nb, g2, e, SSD_HEAD_DIM), axis=-1) + gar[:, :, :, 0], axis=0))
    g_d_chan = jnp.sum(scan_grads[0][5], axis=0)
    g_a_log = (jnp.stack(g_a) * a_neg).reshape(1, 2, N_SSD_HEADS)
    g_dt = (jnp.stack(g_dt_dirs, axis=2).reshape(nb, lt, 2 * N_SSD_HEADS)
            + g_dt_lanes[..., QK_ROPE_DIM:QK_ROPE_DIM + 2 * N_SSD_HEADS])
    (dp_xbc,), (g_ssd_conv_w, g_ssd_conv_b) = _seq_bwd(
        conv_ssd_fn, toks=[(p_xbc, 0, 1)], vecs=conv_vecs, cots=[[scan_grads[0][0], scan_grads[1][0]]],
        tok_grads=[MXU_DTYPE], nb=nb, nl=lt, nc=XBC_WIDTH, cb=cb_ssd, name="conv_ssd_bwd", bwd_fn=conv_ssd_bwd)
    g_ssd_conv_w, g_ssd_conv_b = _ungroup_xbc(g_ssd_conv_w), _ungroup_xbc(g_ssd_conv_b)

    dq_roped, dkv, dkr, early_scattered = _attn_bwd(q_roped, kv_flat, kr, attn, lse, dattn, early_sums, tq=tq,
                                                    name="attn_bwd")
    (dq_flat,) = _row_fwd(_rope_flat_transpose_fn, toks=[(dq_roped, 0, None, 0)], poss=q_tables,
                          outs=[(h * HEAD_TILE, MXU_DTYPE)], nb=nb, nl=s, tl=tl, name="rope_q_bwd")
    dq_flat = dq_flat.reshape(nb * s, h * HEAD_TILE)
    g_w_q_up = _unpad_q_heads(_mm(cqn.reshape(nb * s, -1), dq_flat, ta=True, name="wg_q_up"))
    dcqn = _mm(dq_flat, w_q_pad, tb=True, name="dg_q_up").reshape(nb, s, Q_LORA_RANK)
    (dp_cq,), (g_q_norm,), _, _ = _row_bwd(_norm_fn, toks=[(p_cq, 0, None, 0)], vecs=[q_norm], cots=[(dcqn, -nctx)],
                                           tok_grads=[MXU_DTYPE], nb=nb, nl=lt, tl=tl, name="q_norm_bwd")

    dkv_flat = dkv.reshape(nb * lt, -1)
    g_w_kv_up = _mm(ckvn.reshape(nb * lt, -1), dkv_flat, ta=True, name="wg_kv_up")
    dckvn = _mm(dkv_flat, full["w_kv_up"], tb=True, name="dg_kv_up").reshape(nb, lt, KV_LORA_RANK)
    (dp_ckv,), (g_kv_norm,), _, _ = _row_bwd(_norm_fn, toks=[(p_ckv, 0, None, 0)], vecs=[kv_norm], cots=[(dckvn, 0)],
                                             tok_grads=[MXU_DTYPE], nb=nb, nl=lt, tl=tl, name="kv_norm_bwd")

    g_krdt = jnp.concatenate([dkr[..., QK_NOPE_DIM:QK_DIM], g_dt, jnp.zeros((nb, lt, pad_w - 2 * N_SSD_HEADS), F32)],
                             axis=-1)
    (dp_krdt,), (_, g_dt_bias_row), _, _ = _row_bwd(
        _krdt_fn, toks=[(p_krdt, 0, None, 0)], poss=[cos_k, sin_k], vecs=[rot_k, dt_bias_row], cots=[(g_krdt, 0)],
        tok_grads=[MXU_DTYPE], nb=nb, nl=lt, tl=tl, name="krdt_bwd")

    dp_segs = [t.reshape(nb * lt, -1) for t in (dp_cq, dp_ckv, dz, dp_xbc, dp_krdt)]
    g_segs = [_mm(h1f, t, ta=True, name="wg_in_" + nm) for nm, t in zip(("cq", "ckv", "z", "xbc", "krdt"), dp_segs)]
    g_segs[3] = _ungroup_xbc(g_segs[3])
    g_w_in_p = jnp.concatenate(g_segs, axis=1)
    dh1 = _mm_sum(list(zip(dp_segs, w_segs)), tb=True, name="dg_in").reshape(nb, lt, d)

    def pre_res_fn(uv, w, shift, scale):
        return _pre_fn(uv, w, shift, scale) + (uv,)

    (grad_x,), (g_mix_pre,), ((g_shift0, g_shift0c), (g_scale1, g_scale1c)), _ = _row_bwd(
        pre_res_fn, toks=[(u, 0, None, 0)], vecs=[mix_pre_norm], bvecs=[mods[0], mods[1]],
        cots=[(dh1, 0), (dx_res, -nctx)], tok_grads=[F32], nb=nb, nl=lt, tl=tl, nctx=nctx, drop_blocks=nctx,
        name="pre1_bwd")

    zero_row = jnp.zeros((1, 1, d), F32)
    lat = [g_shift0, g_scale1, g_gate2, g_shift3, g_scale4, g_gate5]
    ctxg = [g_shift0c, g_scale1c, zero_row, zero_row, zero_row, zero_row]
    dmod = jnp.concatenate([jnp.concatenate([a, b], axis=0)[:, 0, :] for a, b in zip(lat, ctxg)], axis=-1)
    dmod = jnp.pad(dmod, ((0, mod_rows - nb - 1), (0, 0)))
    _, (g_b_mod,), _, _ = _row_bwd(_bias_fn, toks=[(mod_lin[None], 0, None, 0)], vecs=[b_mod], cots=[(dmod[None], 0)],
                                   tok_grads=[None], nb=1, nl=mod_rows, tl=mod_rows, name="mod_bias_bwd")
    dmod_all = _allgather_small(dmod[:8], name="gather_dmod")
    dmod_ctx = _sum_slots(dmod_all, out_dtype=F32, name="dmod_ctx_add")[nb:nb + 1]
    dmod_every = jnp.concatenate([dmod_all[:, :nb].reshape(n_ex, N_MOD * d), dmod_ctx,
                                  jnp.zeros((all_rows - n_ex - 1, N_MOD * d), F32)], axis=0)
    dmod_mine = lax.dynamic_slice_in_dim(dmod_every, chip * mod_cols, mod_cols, axis=1)
    g_w_mod = _mm(s_all[0], dmod_mine, ta=True, name="wg_mod")[None]
    ds_all = _mm(dmod_mine, w_mod_local, tb=True, name="dg_mod")
    (dc_all,), _, _, _ = _row_bwd(_silu_fn, toks=[(c_all, 0, None, 0)], cots=[(ds_all[None], 0)], tok_grads=[F32],
                                  nb=1, nl=all_rows, tl=all_rows, name="mod_silu_bwd")
    g_c_ctx = 0.5 * dc_all[0, n_ex]

    g_w_in = _unperm_in_cols(g_w_in_p[:, :IN_WIDTH])
    last_sums = pair_sums({"w_in": g_w_in, "w_q_up": g_w_q_up, "w_kv_up": g_w_kv_up}, REDUCE_LAST, "last")
    halves = [chip_total(early_sums, early_scattered, "early"),
              chip_total(last_sums, _scatter_to_chips(last_sums, name="grad_scatter"), "last")]
    my_halves = jnp.concatenate(halves, axis=0)
    joined = lax.dynamic_update_slice(_join_halves(my_halves, name="grad_join"), my_halves[None], (core, 0, 0))
    g_shards, o = {}, 0
    for group, hv in zip((REDUCE_EARLY, REDUCE_LAST), halves):
        g_shards[group] = joined[:, o:o + hv.shape[0]].reshape(2 * hv.shape[0], PACK_COLS)
        o += hv.shape[0]

    g_d = jnp.sum(g_d_chan.reshape(N_SSD_HEADS, SSD_HEAD_DIM), axis=1)[None]
    g_dt_bias = g_dt_bias_row[:, QK_ROPE_DIM:QK_ROPE_DIM + 2 * N_SSD_HEADS].reshape(1, 2, N_SSD_HEADS)
    small_names = ["c_ctx", "b_mod", "mix_pre_norm", "mix_post_norm", "q_norm", "kv_norm", "ssd_conv_w", "ssd_conv_b",
                   "ssd_a_log", "ssd_dt_bias", "ssd_d", "ssd_norm", "ffn_pre_norm", "ffn_post_norm", "ffn_conv_w",
                   "ffn_conv_b"]
    small_grads = [g_c_ctx, g_b_mod, g_mix_pre, g_mix_post, g_q_norm, g_kv_norm, g_ssd_conv_w, g_ssd_conv_b,
                   g_a_log, g_dt_bias, g_d, g_ssd_norm, g_ffn_pre, g_ffn_post, g_ffn_conv_w, g_ffn_conv_b]
    small_shapes = [tuple(np.shape(a)) for a in small_grads] + [()]
    n_small = sum(int(np.prod(shp)) for shp in small_shapes)
    small_rows = -(-n_small // (8 * LANES)) * 8
    small_all = _allgather_small(_pack_small(small_grads + [loss_part], small_rows), name="gather_small")
    small_sum = _sum_slots(small_all, out_dtype=F32, name="small_add")
    small_red = _unpack_small(small_sum, small_shapes)
    loss = small_red[-1]
    grads = dict(zip(small_names, small_red[:-1]))
    grads["ssd_conv_w"] = lax.dynamic_slice_in_dim(grads["ssd_conv_w"], chip * n_sc, n_sc, axis=1)[None]
    grads["ffn_conv_w"] = lax.dynamic_slice_in_dim(grads["ffn_conv_w"], chip * n_fc, n_fc, axis=1)[None]
    for n in small_names:
        grads[n] = grads[n].reshape(args[n].shape)

    delta, new_m, new_v = {}, {}, {}
    grads["w_mod"] = g_w_mod
    for group, g_shard in g_shards.items():
        o = 0
        for n in group:
            lr, lc = big_local[n].shape
            nr = lr * lc // PACK_COLS
            grads[n] = g_shard[o:o + nr].reshape(1, lr, lc)
            o += nr
    for n in ["w_mod"] + [n for n, _, _, _ in BIG]:
        dl, nm, nv = _adamw(args[n][0], grads[n][0], args["m_" + n][0], args["v_" + n][0], name="adamw_" + n)
        delta[n], new_m[n], new_v[n] = dl[None], nm[None], nv[None]
    sm_shapes = [args[n].shape for n in small_names]
    n_sm = sum(int(np.prod(shp)) for shp in sm_shapes)
    sm_rows = -(-n_sm // (8 * LANES)) * 8
    packs = [_pack_small([src[n] for n in small_names], sm_rows)
             for src in (args, grads, {n: args["m_" + n] for n in small_names}, {n: args["v_" + n] for n in small_names})]
    for out_dict, packed_out in zip((delta, new_m, new_v), _adamw(*packs, name="adamw_small")):
        out_dict.update(zip(small_names, _unpack_small(packed_out, sm_shapes)))

    return (loss, grad_x, *[grads[n] for n in names], *[delta[n] for n in names], *[new_m[n] for n in names],
            *[new_v[n] for n in names])
```

```python
import functools
import math

import numpy as np
import jax
import jax.numpy as jnp
from jax import lax
from jax.experimental import pallas as pl
from jax.experimental.pallas import tpu as pltpu

F32 = jnp.float32
MXU_DTYPE = jnp.bfloat16
WIRE_DTYPE = jnp.bfloat16
VMEM_LIMIT_BYTES = 56 * 1024 * 1024
HIGHEST = lax.Precision.HIGHEST

D_MODEL = 1024
N_MOD = 6
EPS = 1e-6
GRID_W = 64
N_ATTN_HEADS = 16
QK_NOPE_DIM = 64
QK_ROPE_DIM = 32
QK_DIM = QK_NOPE_DIM + QK_ROPE_DIM
V_HEAD_DIM = 64
Q_LORA_RANK = 384
KV_LORA_RANK = 256
ROPE_THETA = 10000.0
ATTN_SCALE = QK_DIM ** -0.5
ATTN_WIDTH = N_ATTN_HEADS * V_HEAD_DIM
N_SSD_HEADS = 16
SSD_HEAD_DIM = 64
SSD_GROUPS = 2
HEADS_PER_GROUP = N_SSD_HEADS // SSD_GROUPS
SSD_STATE = 128
SSD_CONV = 5
SSD_CHUNK = 128
D_INNER = N_SSD_HEADS * SSD_HEAD_DIM
GN = SSD_GROUPS * SSD_STATE
XBC_WIDTH = D_INNER + 2 * GN
D_FF = 2816
FFN_CONV = 3
KRDT_WIDTH = 128
IN_WIDTH = Q_LORA_RANK + KV_LORA_RANK + QK_ROPE_DIM + D_INNER + XBC_WIDTH + 2 * N_SSD_HEADS

ADAM_LR = 0.001
ADAM_B1 = 0.9
ADAM_B2 = 0.999
ADAM_EPS = 1e-08
ADAM_WD = 0.01
ADAM_STEP = 10

N_CHIPS = 4
N_DEV = 8
MESH = pl.DeviceIdType.MESH
LANES = 128

BIG = (("w_in", D_MODEL, IN_WIDTH, 1),
       ("w_q_up", Q_LORA_RANK, N_ATTN_HEADS * QK_DIM, 1),
       ("w_kv_up", KV_LORA_RANK, N_ATTN_HEADS * (QK_NOPE_DIM + V_HEAD_DIM), 1),
       ("w_out", ATTN_WIDTH + D_INNER, D_MODEL, 0), ("w_up", D_MODEL, 2 * D_FF, 1),
       ("w_down", D_FF, D_MODEL, 0))
PACK_COLS = 1024
GATHER_NOW, GATHER_LATE = ("w_in", "w_q_up", "w_kv_up"), ("w_out", "w_up", "w_down")
REDUCE_EARLY, REDUCE_LAST = ("w_up", "w_down", "w_out"), ("w_in", "w_q_up", "w_kv_up")


def _cparams(sem):
    return pltpu.CompilerParams(dimension_semantics=sem, vmem_limit_bytes=VMEM_LIMIT_BYTES)


def _pick(n, cands):
    for c in cands:
        if n % c == 0:
            return c
    return n


def _sigmoid(x):
    return 0.5 * (jnp.tanh(0.5 * x) + 1.0)


def _silu(x):
    return x * _sigmoid(x)


@jax.custom_vjp
def _softplus(x):
    u = jnp.exp(-jnp.abs(x))
    w = 1.0 + u
    log1p = jnp.where(w == 1.0, u, jnp.log(w) * (u / jnp.where(w == 1.0, 1.0, w - 1.0)))
    return jnp.maximum(x, 0.0) + log1p


def _softplus_fwd(x):
    return _softplus(x), x


def _softplus_bwd(x, g):
    return (g * _sigmoid(x),)


_softplus.defvjp(_softplus_fwd, _softplus_bwd)


@jax.custom_vjp
def _gelu(x):
    return 0.5 * x * (1.0 + lax.erf(x * (2.0 ** -0.5)))


def _gelu_fwd(x):
    return _gelu(x), x


def _gelu_bwd(x, g):
    cdf = 0.5 * (1.0 + lax.erf(x * (2.0 ** -0.5)))
    pdf = jnp.exp(-0.5 * x * x) * (1.0 / math.sqrt(2.0 * math.pi))
    return (g * (cdf + x * pdf),)


_gelu.defvjp(_gelu_fwd, _gelu_bwd)


def _rms(x, w):
    return x * lax.rsqrt(jnp.mean(x * x, axis=-1, keepdims=True) + EPS) * w


def _shift_rows_raw(x, off, seg):
    n = x.shape[0]
    if off == 0:
        return x
    r = pltpu.roll(x, (-off) % n, 0)
    idx = lax.broadcasted_iota(jnp.int32, x.shape, 0)
    src = idx + off
    ok = (src >= 0) & (src < n)
    if seg:
        ok = ok & ((idx < seg) == (src < seg))
    return jnp.where(ok, r, 0.0)


@functools.partial(jax.custom_vjp, nondiff_argnums=(1, 2))
def _shift_rows(x, off, seg):
    return _shift_rows_raw(x, off, seg)


def _shift_rows_fwd(x, off, seg):
    return _shift_rows_raw(x, off, seg), None


def _shift_rows_bwd(off, seg, _, g):
    return (_shift_rows_raw(g, -off, seg),)


_shift_rows.defvjp(_shift_rows_fwd, _shift_rows_bwd)


@functools.partial(jax.custom_vjp, nondiff_argnums=(1,))
def _roll_lanes(x, shift):
    return pltpu.roll(x, shift % x.shape[1], 1)


def _roll_lanes_fwd(x, shift):
    return _roll_lanes(x, shift), None


def _roll_lanes_bwd(shift, _, g):
    return (pltpu.roll(g, (-shift) % g.shape[1], 1),)


_roll_lanes.defvjp(_roll_lanes_fwd, _roll_lanes_bwd)


def _row_of(w, k):
    sel = lax.broadcasted_iota(jnp.int32, (w.shape[0], 1), 0) == k
    return jnp.sum(jnp.where(sel, w, 0.0), axis=0, keepdims=True)


def _col_of(w, k):
    sel = lax.broadcasted_iota(jnp.int32, (1, w.shape[1]), 1) == k
    return jnp.sum(jnp.where(sel, w, 0.0), axis=1, keepdims=True)


def _dwconv(x, w, seg):
    k = w.shape[0]
    acc = None
    for t in range(k):
        term = _shift_rows(x, t - k // 2, seg) * _row_of(w, t)
        acc = term if acc is None else acc + term
    return acc


def _dwconv_back(x, dy, w, seg):
    k = w.shape[0]
    tap = lax.broadcasted_iota(jnp.int32, (k, 1), 0)
    dx, dw = None, jnp.zeros_like(w)
    for t in range(k):
        back = _shift_rows_raw(dy, k // 2 - t, seg)
        term = back * _row_of(w, t)
        dx = term if dx is None else dx + term
        dw = dw + jnp.where(tap == t, jnp.sum(x * back, axis=0, keepdims=True), 0.0)
    return dx, dw


def _dot(a, b, dims):
    return lax.dot_general(a.astype(MXU_DTYPE), b.astype(MXU_DTYPE), (dims, ((), ())),
                           preferred_element_type=F32)


def _dot_exact(a, b):
    return lax.dot_general(a, b, (((1,), (0,)), ((), ())), precision=HIGHEST,
                           preferred_element_type=F32)


def _mask_dot_raw(mask, x, mask_left):
    hi = x.astype(jnp.bfloat16)
    rest = x - hi.astype(F32)
    mid = rest.astype(jnp.bfloat16)
    low = (rest - mid.astype(F32)).astype(jnp.bfloat16)
    m = mask.astype(jnp.bfloat16)
    acc = None
    for piece in (hi, mid, low):
        term = (lax.dot_general(m, piece, (((1,), (0,)), ((), ())), preferred_element_type=F32) if mask_left
                else lax.dot_general(piece, m, (((1,), (0,)), ((), ())), preferred_element_type=F32))
        acc = term if acc is None else acc + term
    return acc


@functools.partial(jax.custom_vjp, nondiff_argnums=(3,))
def _mask_dot(mask, mask_t, x, mask_left):
    return _mask_dot_raw(mask, x, mask_left)


def _mask_dot_fwd(mask, mask_t, x, mask_left):
    return _mask_dot_raw(mask, x, mask_left), (mask, mask_t)


def _mask_dot_bwd(mask_left, res, g):
    mask, mask_t = res
    return jnp.zeros_like(mask), jnp.zeros_like(mask_t), _mask_dot_raw(mask_t, g, mask_left)


_mask_dot.defvjp(_mask_dot_fwd, _mask_dot_bwd)


def _mm(a, b, *, ta=False, tb=False, out_dtype=F32, name):
    if ta:
        kdim, m = a.shape
    else:
        m, kdim = a.shape
    if tb:
        n, k2 = b.shape
    else:
        k2, n = b.shape
    assert kdim == k2, (a.shape, b.shape, ta, tb)
    tm = _pick(m, (1024, 1408, 512, 384, 256, 128))
    tn = n if n == 1024 else _pick(n, (512, 1408, 384, 256, 128))
    tk = kdim if kdim <= 2048 else _pick(kdim, (2048, 1664, 1536, 1408, 1024, 512, 256, 128))
    nk = kdim // tk
    a_spec = pl.BlockSpec((tk, tm), lambda i, j, k: (k, i)) if ta else pl.BlockSpec((tm, tk), lambda i, j, k: (i, k))
    b_spec = pl.BlockSpec((tn, tk), lambda i, j, k: (j, k)) if tb else pl.BlockSpec((tk, tn), lambda i, j, k: (k, j))
    dims = ((0,) if ta else (1,), (1,) if tb else (0,))

    def body(a_ref, b_ref, o_ref, *scratch):
        if nk == 1:
            o_ref[...] = _dot(a_ref[...], b_ref[...], dims).astype(o_ref.dtype)
            return
        acc_ref, = scratch
        k = pl.program_id(2)

        @pl.when(k == 0)
        def _():
            acc_ref[...] = jnp.zeros_like(acc_ref)

        acc_ref[...] += _dot(a_ref[...], b_ref[...], dims)

        @pl.when(k == nk - 1)
        def _():
            o_ref[...] = acc_ref[...].astype(o_ref.dtype)

    return pl.pallas_call(
        body, name=name, grid=(m // tm, n // tn, nk),
        in_specs=[a_spec, b_spec], out_specs=pl.BlockSpec((tm, tn), lambda i, j, k: (i, j)),
        out_shape=jax.ShapeDtypeStruct((m, n), out_dtype),
        scratch_shapes=[pltpu.VMEM((tm, tn), F32)] if nk > 1 else [],
        compiler_params=_cparams(("parallel", "parallel", "arbitrary")),
    )(a, b)


def _mm_sum(pairs, *, tb=False, out_dtype=F32, name):
    m = pairs[0][0].shape[0]
    n = pairs[0][1].shape[0] if tb else pairs[0][1].shape[1]
    tm = _pick(m, (1024, 1408, 512, 384, 256, 128))
    tn = n if n == 1024 else _pick(n, (512, 1408, 384, 256, 128))
    specs, args = [], []
    for a, b in pairs:
        kdim = a.shape[1]
        specs.append(pl.BlockSpec((tm, kdim), lambda i, j: (i, 0)))
        specs.append(pl.BlockSpec((tn, kdim), lambda i, j: (j, 0)) if tb else pl.BlockSpec((kdim, tn), lambda i, j: (0, j)))
        args += [a, b]
    dims = ((1,), (1,) if tb else (0,))

    def body(*refs):
        acc = None
        for t in range(len(pairs)):
            term = _dot(refs[2 * t][...], refs[2 * t + 1][...], dims)
            acc = term if acc is None else acc + term
        refs[-1][...] = acc.astype(refs[-1].dtype)

    return pl.pallas_call(
        body, name=name, grid=(m // tm, n // tn), in_specs=specs,
        out_specs=pl.BlockSpec((tm, tn), lambda i, j: (i, j)), out_shape=jax.ShapeDtypeStruct((m, n), out_dtype),
        compiler_params=_cparams(("parallel", "parallel")),
    )(*args)


def _row_specs(toks, poss, vecs, bvecs, tl, nctx, nb):
    specs, args = [], []
    for arr, off, cw, ci in toks:
        cw = arr.shape[2] if cw is None else cw
        specs.append(pl.BlockSpec((1, tl, cw), lambda b, l, off=off, ci=ci: (b, l + off, ci)))
        args.append(arr)
    for arr in poss:
        specs.append(pl.BlockSpec((1, tl, arr.shape[2]), lambda b, l: (0, l, 0)))
        args.append(arr)
    for arr in vecs:
        specs.append(pl.BlockSpec(arr.shape, lambda b, l: (0, 0)))
        args.append(arr)
    for arr in bvecs:
        if nctx:
            specs.append(pl.BlockSpec((1, 1, arr.shape[2]), lambda b, l: (jnp.where(l < nctx, nb, b), 0, 0)))
        else:
            specs.append(pl.BlockSpec((1, 1, arr.shape[2]), lambda b, l: (b, 0, 0)))
        args.append(arr)
    return specs, args


def _row_fwd(fn, *, toks, poss=(), vecs=(), bvecs=(), outs, nb, nl, tl, nctx=0, name):
    nt, npos, nv, nbv = len(toks), len(poss), len(vecs), len(bvecs)
    specs, args = _row_specs(toks, poss, vecs, bvecs, tl, nctx, nb)

    def body(*refs):
        ins, os = refs[:len(specs)], refs[len(specs):]
        tv = [r[0].astype(F32) for r in ins[:nt]]
        pv = [r[0] for r in ins[nt:nt + npos]]
        vv = [r[...] for r in ins[nt + npos:nt + npos + nv]]
        bv = [r[0] for r in ins[nt + npos + nv:]]
        res = fn(*tv, *pv, *vv, *bv)
        for o, r in zip(os, res):
            o[0] = r.astype(o.dtype)

    return pl.pallas_call(
        body, name=name, grid=(nb, nl // tl), in_specs=specs,
        out_specs=[pl.BlockSpec((1, tl, c), lambda b, l: (b, l, 0)) for c, _ in outs],
        out_shape=[jax.ShapeDtypeStruct((nb, nl, c), dt) for c, dt in outs],
        compiler_params=_cparams(("parallel", "parallel")),
    )(*args)


def _row_bwd(fn, *, toks, poss=(), vecs=(), bvecs=(), cots, tok_grads, emit=(), nb, nl, tl, nctx=0, name,
             drop_blocks=0):
    nt, npos, nv, nbv = len(toks), len(poss), len(vecs), len(bvecs)
    specs, args = _row_specs(toks, poss, vecs, bvecs, tl, nctx, nb)
    n_in = len(specs)
    cot_slots = []
    for arr, off in cots:
        if arr is None:
            cot_slots.append(None)
            continue
        cot_slots.append((len(specs), off))
        specs.append(pl.BlockSpec((1, tl, arr.shape[2]), lambda b, l, off=off: (b, jnp.maximum(l + off, 0), 0)))
        args.append(arr)
    n_all_in = len(specs)

    out_specs, out_shapes = [], []
    tok_out = []
    for (arr, off, cw, ci), dt in zip(toks, tok_grads):
        if dt is None:
            tok_out.append(None)
            continue
        cw = arr.shape[2] if cw is None else cw
        tok_out.append(len(out_specs))
        out_specs.append(pl.BlockSpec((1, tl, cw), lambda b, l: (b, jnp.maximum(l - drop_blocks, 0), 0)))
        out_shapes.append(jax.ShapeDtypeStruct((nb, nl - drop_blocks * tl, cw), dt))
    vec_out = []
    for arr in vecs:
        vec_out.append(len(out_specs))
        out_specs.append(pl.BlockSpec(arr.shape, lambda b, l: (0, 0)))
        out_shapes.append(jax.ShapeDtypeStruct(arr.shape, F32))
    bv_out = []
    for arr in bvecs:
        c = arr.shape[2]
        lat = len(out_specs)
        out_specs.append(pl.BlockSpec((1, 1, c), lambda b, l: (b, 0, 0)))
        out_shapes.append(jax.ShapeDtypeStruct((nb, 1, c), F32))
        ctx = None
        if nctx:
            ctx = len(out_specs)
            out_specs.append(pl.BlockSpec((1, 1, c), lambda b, l: (0, 0, 0)))
            out_shapes.append(jax.ShapeDtypeStruct((1, 1, c), F32))
        bv_out.append((lat, ctx))
    emit_out = []
    emit_cols = {}
    for idx, c, dt in emit:
        emit_out.append((idx, len(out_specs)))
        out_specs.append(pl.BlockSpec((1, tl, c), lambda b, l: (b, l, 0)))
        out_shapes.append(jax.ShapeDtypeStruct((nb, nl, c), dt))

    def body(*refs):
        ins, os = refs[:n_all_in], refs[n_all_in:]
        b, l = pl.program_id(0), pl.program_id(1)
        tv = [r[0].astype(F32) for r in ins[:nt]]
        pv = [r[0] for r in ins[nt:nt + npos]]
        vv = [r[...] for r in ins[nt + npos:nt + npos + nv]]
        bv = [r[0] for r in ins[nt + npos + nv:n_in]]

        def f(*d):
            return tuple(fn(*d[:nt], *pv, *d[nt:]))

        res, vjp = jax.vjp(f, *tv, *vv, *bv)
        cts = []
        for r, slot in zip(res, cot_slots):
            if slot is None:
                cts.append(jnp.zeros_like(r))
            else:
                i, off = slot
                ct = ins[i][0].astype(F32)
                if off < 0:
                    ct = jnp.where(l + off >= 0, ct, 0.0)
                cts.append(ct)
        grads = vjp(tuple(cts))

        for g, slot in zip(grads[:nt], tok_out):
            if slot is not None:
                os[slot][0] = g.astype(os[slot].dtype)

        @pl.when((b == 0) & (l == 0))
        def _():
            for slot in vec_out:
                os[slot][...] = jnp.zeros_like(os[slot])
            for _, ctx in bv_out:
                if ctx is not None:
                    os[ctx][...] = jnp.zeros_like(os[ctx])

        @pl.when(l == 0)
        def _():
            for lat, _ in bv_out:
                os[lat][...] = jnp.zeros_like(os[lat])

        for g, slot in zip(grads[nt:nt + nv], vec_out):
            os[slot][...] += g
        for g, (lat, ctx) in zip(grads[nt + nv:], bv_out):
            if ctx is None:
                os[lat][0] += g
            else:
                is_ctx = l < nctx
                os[lat][0] += jnp.where(is_ctx, 0.0, g)
                os[ctx][0] += jnp.where(is_ctx, g, 0.0)
        for idx, slot in emit_out:
            os[slot][0] = res[idx].astype(os[slot].dtype)

    out = pl.pallas_call(
        body, name=name, grid=(nb, nl // tl), in_specs=specs, out_specs=out_specs, out_shape=out_shapes,
        compiler_params=_cparams(("arbitrary", "arbitrary")),
    )(*args)
    tg = [None if s is None else out[s] for s in tok_out]
    vg = [out[s] for s in vec_out]
    bg = [(out[lat], None if ctx is None else out[ctx]) for lat, ctx in bv_out]
    em = [out[s] for _, s in emit_out]
    return tg, vg, bg, em


def _seq_specs(toks, vecs, nl, cb):
    specs, args = [], []
    for arr, off, mult in toks:
        specs.append(pl.BlockSpec((1, nl, cb * mult), lambda j, b, off=off: (b, 0, j + off)))
        args.append(arr)
    for arr, off in vecs:
        specs.append(pl.BlockSpec((arr.shape[0], cb), lambda j, b, off=off: (0, j + off)))
        args.append(arr)
    return specs, args


def _seq_fwd(fn, *, toks, vecs, outs, nb, nl, nc, cb, name):
    nt = len(toks)
    specs, args = _seq_specs(toks, vecs, nl, cb)

    def body(*refs):
        ins, os = refs[:len(specs)], refs[len(specs):]
        tv = [r[0].astype(F32) for r in ins[:nt]]
        vv = [r[...] for r in ins[nt:]]
        for o, r in zip(os, fn(*tv, *vv)):
            o[0] = r.astype(o.dtype)

    return pl.pallas_call(
        body, name=name, grid=(nc // cb, nb), in_specs=specs,
        out_specs=[pl.BlockSpec((1, nl, cb), lambda j, b: (b, 0, j)) for _ in outs],
        out_shape=[jax.ShapeDtypeStruct((nb, nl, nc), dt) for dt in outs],
        compiler_params=_cparams(("parallel", "parallel")),
    )(*args)


def _seq_bwd(fn, *, toks, vecs, cots, tok_grads, nb, nl, nc, cb, name, bwd_fn=None):
    nt, nv = len(toks), len(vecs)
    specs, args = _seq_specs(toks, vecs, nl, cb)
    n_in = len(specs)
    cot_counts = [len(group) for group in cots]
    for group in cots:
        for arr in group:
            specs.append(pl.BlockSpec((1, nl, cb), lambda j, b: (b, 0, j)))
            args.append(arr)
    out_specs, out_shapes = [], []
    for (_, _, mult), dt in zip(toks, tok_grads):
        out_specs.append(pl.BlockSpec((1, nl, cb * mult), lambda j, b: (b, 0, j)))
        out_shapes.append(jax.ShapeDtypeStruct((nb, nl, nc * mult), dt))
    for arr, _ in vecs:
        out_specs.append(pl.BlockSpec((arr.shape[0], cb), lambda j, b: (0, j)))
        out_shapes.append(jax.ShapeDtypeStruct((arr.shape[0], nc), F32))

    def body(*refs):
        ins, os = refs[:len(specs)], refs[len(specs):]
        b = pl.program_id(1)
        tv = [r[0].astype(F32) for r in ins[:nt]]
        vv = [r[...] for r in ins[nt:n_in]]
        cts, o = [], n_in
        for cnt in cot_counts:
            ct = ins[o][0].astype(F32)
            for r in ins[o + 1:o + cnt]:
                ct = ct + r[0].astype(F32)
            cts.append(ct)
            o += cnt
        if bwd_fn is None:
            _, vjp = jax.vjp(lambda *d: tuple(fn(*d)), *tv, *vv)
            grads = vjp(tuple(cts))
        else:
            grads = bwd_fn(*tv, *vv, *cts)
        for g, o in zip(grads[:nt], os[:nt]):
            o[0] = g.astype(o.dtype)

        @pl.when(b == 0)
        def _():
            for o in os[nt:]:
                o[...] = jnp.zeros_like(o)

        for g, o in zip(grads[nt:], os[nt:]):
            o[...] += g

    out = pl.pallas_call(
        body, name=name, grid=(nc // cb, nb), in_specs=specs, out_specs=out_specs, out_shape=out_shapes,
        compiler_params=_cparams(("parallel", "arbitrary")),
    )(*args)
    return out[:nt], out[nt:]


EXP2_SCALE = ATTN_SCALE * math.log2(math.e)


HEAD_TILE = 128
N_HEAD_PAIRS = N_ATTN_HEADS // 2


def _head_lanes():
    lane = lax.broadcasted_iota(jnp.int32, (1, HEAD_TILE), 1)
    return lane < QK_NOPE_DIM, (lane >= QK_NOPE_DIM) & (lane < QK_DIM)


def _attn_specs(tq, lk):
    q = pl.BlockSpec((1, tq, 2 * HEAD_TILE), lambda b, pr, j: (b, j, pr))
    kv = pl.BlockSpec((1, lk, 2 * HEAD_TILE), lambda b, pr, j: (b, 0, pr))
    kr = pl.BlockSpec((1, lk, HEAD_TILE), lambda b, pr, j: (b, 0, 0))
    o = pl.BlockSpec((1, tq, HEAD_TILE), lambda b, pr, j: (b, j, pr))
    lse = pl.BlockSpec((1, 2, tq, 1), lambda b, pr, j: (b, pr, j, 0))
    return q, kv, kr, o, lse


def _grid_marks(nb, nj):
    b, pr, j = pl.program_id(0), pl.program_id(1), pl.program_id(2)
    first = (b == 0) & (pr == 0) & (j == 0)
    middle = (b == nb // 2) & (pr == 0) & (j == 0)
    last = (b == nb - 1) & (pr == N_HEAD_PAIRS - 1) & (j == nj - 1)
    return first, middle, last


def _attn_fwd(q, kv, kr, late_shard, *, tq, name):
    nb, s, _ = q.shape
    lk = kv.shape[1]
    nj = s // tq
    qs, kvs, krs, os_, lses = _attn_specs(tq, lk)

    def body(q_ref, kv_ref, kr_ref, shard_ref, o_ref, lse_ref, gathered_ref, send_sems, recv_sems):
        start, relay, finish = _gather_stage(shard_ref, gathered_ref, send_sems, recv_sems)
        first, middle, last = _grid_marks(nb, nj)
        pl.when(first)(start)
        pl.when(middle)(relay)
        low, _ = _head_lanes()
        outs = []
        for e in range(2):
            tile = pl.ds(HEAD_TILE * e, HEAD_TILE)
            kv_e = kv_ref[0, :, tile]
            keys = jnp.where(low, kv_e, kr_ref[0])
            sc = _dot(q_ref[0, :, tile], keys, ((1,), (1,)))
            m = jnp.max(sc, axis=-1, keepdims=True)
            p = jnp.exp2((sc - m) * EXP2_SCALE)
            denom = jnp.sum(p, axis=-1, keepdims=True)
            outs.append(_dot(p, kv_e, ((1,), (0,))) / denom)
            lse_ref[0, e] = m * EXP2_SCALE + jnp.log2(denom)
        o_ref[0] = jnp.where(low, pltpu.roll(outs[0], V_HEAD_DIM, 1), outs[1])
        pl.when(last)(finish)

    return pl.pallas_call(
        body, name=name, grid=(nb, N_HEAD_PAIRS, nj), in_specs=[qs, kvs, krs, ANY], out_specs=[os_, lses, ANY],
        out_shape=[jax.ShapeDtypeStruct((nb, s, ATTN_WIDTH), F32), jax.ShapeDtypeStruct((nb, N_ATTN_HEADS, s, 1), F32),
                   jax.ShapeDtypeStruct((N_CHIPS,) + late_shard.shape, late_shard.dtype)],
        scratch_shapes=GATHER_SEMS,
        compiler_params=_cparams(("arbitrary", "arbitrary", "arbitrary")),
    )(q, kv, kr, late_shard)


def _attn_bwd(q, kv, kr, o, lse, do, early_sums, *, tq, name):
    nb, s, _ = q.shape
    lk = kv.shape[1]
    nj = s // tq
    qs, kvs, krs, os_, lses = _attn_specs(tq, lk)

    def body(q_ref, kv_ref, kr_ref, o_ref, lse_ref, do_ref, sums_ref, dq_ref, dkv_ref, dkr_ref, scattered_ref,
             send_sems, recv_sems):
        start, finish = _scatter_stage(sums_ref, scattered_ref, send_sems, recv_sems)
        first, _, last = _grid_marks(nb, nj)
        pl.when(first)(start)
        pr, j = pl.program_id(1), pl.program_id(2)
        low, rope = _head_lanes()
        do_pair = do_ref[0]
        prod = do_pair * o_ref[0]

        @pl.when(j == 0)
        def _():
            dkv_ref[...] = jnp.zeros_like(dkv_ref)

        @pl.when((pr == 0) & (j == 0))
        def _():
            dkr_ref[...] = jnp.zeros_like(dkr_ref)

        dkr = None
        for e in range(2):
            tile = pl.ds(HEAD_TILE * e, HEAD_TILE)
            delta = jnp.sum(jnp.where(low if e == 0 else ~low, prod, 0.0), axis=-1, keepdims=True)
            do_e = jnp.where(low, 0.0, do_pair if e == 1 else pltpu.roll(do_pair, V_HEAD_DIM, 1))
            kv_e, q_e = kv_ref[0, :, tile], q_ref[0, :, tile]
            keys = jnp.where(low, kv_e, kr_ref[0])
            sc = _dot(q_e, keys, ((1,), (1,)))
            p = jnp.exp2(sc * EXP2_SCALE - lse_ref[0, e])
            dp = _dot(do_e, kv_e, ((1,), (1,)))
            ds = (p * (dp - delta)).astype(MXU_DTYPE)
            dq_ref[0, :, tile] = _dot(ds, keys, ((1,), (0,))) * ATTN_SCALE
            dkeys = _dot(ds, q_e, ((0,), (0,)))
            dv = _dot(p, do_e, ((0,), (0,)))
            dkv_ref[0, :, tile] += jnp.where(low, dkeys, dv)
            part = jnp.where(rope, dkeys, 0.0)
            dkr = part if dkr is None else dkr + part
        dkr_ref[0] += dkr

        @pl.when(j == nj - 1)
        def _():
            for e in range(2):
                tile = pl.ds(HEAD_TILE * e, HEAD_TILE)
                dkv_ref[0, :, tile] = dkv_ref[0, :, tile] * jnp.where(low, ATTN_SCALE, 1.0)

        @pl.when((pr == N_HEAD_PAIRS - 1) & (j == nj - 1))
        def _():
            dkr_ref[0] = dkr_ref[0] * ATTN_SCALE

        pl.when(last)(finish)

    return pl.pallas_call(
        body, name=name, grid=(nb, N_HEAD_PAIRS, nj), in_specs=[qs, kvs, krs, os_, lses, os_, ANY],
        out_specs=[qs, kvs, krs, ANY],
        out_shape=[jax.ShapeDtypeStruct(q.shape, F32), jax.ShapeDtypeStruct(kv.shape, F32),
                   jax.ShapeDtypeStruct(kr.shape, F32), jax.ShapeDtypeStruct(early_sums.shape, early_sums.dtype)],
        scratch_shapes=SCATTER_SEMS,
        compiler_params=_cparams(("arbitrary", "arbitrary", "arbitrary")),
    )(q, kv, kr, o, lse, do, early_sums)


N_PAIRS = HEADS_PER_GROUP // 2
PAIR_W = 2 * SSD_HEAD_DIM


def _ssd_chunk(states, xs, dtc, dtr, bm, cm, ac, ar, *, reverse):
    q = dtc.shape[0]
    assert q == PAIR_W == dtr.shape[1]
    row = lax.broadcasted_iota(jnp.int32, (q, q), 0)
    col = lax.broadcasted_iota(jnp.int32, (q, q), 1)
    if reverse:
        tri_c, tri_r, mask = col < row, row < col, col >= row
    else:
        tri_c, tri_r, mask = col <= row, row <= col, col <= row
    a_col, a_row = dtc * ac, dtr * ar
    tri_c, tri_r = tri_c.astype(F32), tri_r.astype(F32)
    cum_c = _mask_dot(tri_c, tri_r, a_col, True)
    cum_r = _mask_dot(tri_r, tri_c, a_row, False)
    tot_c = jnp.sum(a_col, axis=0, keepdims=True)
    tot_r = jnp.sum(a_row, axis=1, keepdims=True)
    cb = _dot(cm, bm, ((1,), (1,)))
    first = lax.broadcasted_iota(jnp.int32, (1, PAIR_W), 1) < SSD_HEAD_DIM
    first_rows = lax.broadcasted_iota(jnp.int32, (PAIR_W, 1), 0) < SSD_HEAD_DIM
    heads, pairs = range(HEADS_PER_GROUP), range(N_PAIRS)
    tile = [slice(PAIR_W * pr, PAIR_W * (pr + 1)) for pr in pairs]
    cum_p = [cum_c[:, tile[pr]] for pr in pairs]
    swapped = [_roll_lanes(cum_p[pr], SSD_HEAD_DIM) for pr in pairs]
    cc = [jnp.where(first, cum_p[e // 2], swapped[e // 2]) if e % 2 == 0
          else jnp.where(first, swapped[e // 2], cum_p[e // 2]) for e in heads]
    cr = [_row_of(cum_r, e) for e in heads]
    if reverse:
        within = [jnp.exp(jnp.where(mask, cr[e] - cc[e], -jnp.inf)) for e in heads]
        into = [jnp.exp(tot_c[:, tile[pr]] - cum_p[pr]) for pr in pairs]
        to_end = [jnp.exp(cum_p[pr]) for pr in pairs]
    else:
        within = [jnp.exp(jnp.where(mask, cc[e] - cr[e], -jnp.inf)) for e in heads]
        into = [jnp.exp(cum_p[pr]) for pr in pairs]
        to_end = [jnp.exp(tot_c[:, tile[pr]] - cum_p[pr]) for pr in pairs]
    decay = [cb * within[e] for e in heads]
    carry = [jnp.exp(_row_of(tot_r, e)) for e in heads]
    xd = [xs[pr] * dtc[:, tile[pr]] for pr in pairs]
    y_even = [_dot(decay[2 * pr], jnp.where(first, xd[pr], 0.0), ((1,), (0,))) for pr in pairs]
    y_odd = [_dot(decay[2 * pr + 1], jnp.where(first, 0.0, xd[pr]), ((1,), (0,))) for pr in pairs]
    y_off = [_dot(cm, states[pr], ((1,), (1,))) for pr in pairs]
    grow = [_dot(xd[pr] * to_end[pr], bm, ((0,), (0,))) for pr in pairs]
    ys = [y_even[pr] + y_odd[pr] + y_off[pr] * into[pr] for pr in pairs]
    new_states = [states[pr] * jnp.where(first_rows, carry[2 * pr], carry[2 * pr + 1]) + grow[pr] for pr in pairs]
    return tuple(ys) + tuple(new_states)


def _chunk_of_step(t, ncc, nch, reverse):
    if not reverse:
        return t
    return jnp.where(t < ncc, ncc - 1 - t, nch - 1 - (t - ncc))


X_COLS = D_INNER // SSD_GROUPS
GROUP_COLS = X_COLS + 2 * SSD_STATE


def _scan_in_specs(nch, ncc, reverse, back):
    q, e = SSD_CHUNK, HEADS_PER_GROUP

    def ch(t):
        return _chunk_of_step((nch - 1 - t) if back else t, ncc, nch, reverse)

    return ch, [
        pl.BlockSpec((1, q, GROUP_COLS), lambda b, g, t: (b, ch(t), g)),
        pl.BlockSpec((1, q, X_COLS), lambda b, g, t: (b, ch(t), g)),
        pl.BlockSpec((1, 1, e, q), lambda b, g, t: (b, g, 0, ch(t))),
        pl.BlockSpec((1, 1, X_COLS), lambda b, g, t: (g, 0, 0)),
        pl.BlockSpec((1, e, 1), lambda b, g, t: (g, 0, 0)),
        pl.BlockSpec((1, 1, X_COLS), lambda b, g, t: (g, 0, 0)),
    ]


def _scan_chunk_fn(reverse, skip):
    def f(states, xs, dtc, dtr, bm, cm, ac, ar, d):
        res = _ssd_chunk(states, xs, dtc, dtr, bm, cm, ac, ar, reverse=reverse)
        if not skip:
            return res
        ys = tuple(res[i] + d[:, PAIR_W * i:PAIR_W * (i + 1)] * xs[i] for i in range(N_PAIRS))
        return ys + tuple(res[N_PAIRS:])

    return f


def _scan_operands(x_ref, dtc_ref, dtr_ref, ac_ref, ar_ref, d_ref):
    xs = [x_ref[0, :, pl.ds(PAIR_W * i, PAIR_W)] for i in range(N_PAIRS)]
    bm = x_ref[0, :, pl.ds(X_COLS, SSD_STATE)]
    cm = x_ref[0, :, pl.ds(X_COLS + SSD_STATE, SSD_STATE)]
    return xs, dtc_ref[0], dtr_ref[0, 0], bm, cm, ac_ref[0], ar_ref[0], d_ref[0]


N_IN = 6


def _scan_fwd(dirs, *, ncc, name):
    nb, lt, _ = dirs[0][0].shape
    q, n = SSD_CHUNK, SSD_STATE
    nch = lt // q
    in_specs, out_specs, out_shapes, fs = [], [], [], []
    for dr in range(2):
        ch, specs = _scan_in_specs(nch, ncc, bool(dr), False)
        in_specs += specs
        fs.append(_scan_chunk_fn(bool(dr), dr == 0))
        out_specs += [pl.BlockSpec((1, q, X_COLS), lambda b, g, t, ch=ch: (b, ch(t), g)),
                      pl.BlockSpec((1, 1, 1, N_PAIRS, PAIR_W, n), lambda b, g, t: (b, g, t, 0, 0, 0))]
        out_shapes += [jax.ShapeDtypeStruct((nb, lt, D_INNER), F32),
                       jax.ShapeDtypeStruct((nb, SSD_GROUPS, nch, N_PAIRS, PAIR_W, n), F32)]

    def body(*refs):
        ins, outs, sts = refs[:2 * N_IN], refs[2 * N_IN:2 * N_IN + 4], refs[2 * N_IN + 4:]
        t = pl.program_id(2)

        @pl.when(t == 0)
        def _():
            for st_ref in sts:
                st_ref[...] = jnp.zeros_like(st_ref)

        entering = [[sts[dr][i] for i in range(N_PAIRS)] for dr in range(2)]
        results = [fs[dr](entering[dr], *_scan_operands(*ins[N_IN * dr:N_IN * (dr + 1)])) for dr in range(2)]
        for dr in range(2):
            (y_ref, ent_ref), st_ref = outs[2 * dr:2 * dr + 2], sts[dr]
            for i in range(N_PAIRS):
                ent_ref[0, 0, 0, i] = entering[dr][i]
                y_ref[0, :, pl.ds(PAIR_W * i, PAIR_W)] = results[dr][i]
                st_ref[i] = results[dr][N_PAIRS + i]

    out = pl.pallas_call(
        body, name=name, grid=(nb, SSD_GROUPS, nch), in_specs=in_specs, out_specs=out_specs, out_shape=out_shapes,
        scratch_shapes=[pltpu.VMEM((N_PAIRS, PAIR_W, n), F32)] * 2,
        compiler_params=_cparams(("parallel", "parallel", "arbitrary")),
    )(*dirs[0], *dirs[1])
    return out[:2], out[2:]


N_SCAN_GRADS = 6


def _scan_bwd(dirs, entering, dy, *, ncc, name):
    nb, lt, _ = dirs[0][0].shape
    q, n, e = SSD_CHUNK, SSD_STATE, HEADS_PER_GROUP
    nch = lt // q
    in_specs, out_specs, out_shapes, fs, args = [], [], [], [], []
    for dr in range(2):
        ch, specs = _scan_in_specs(nch, ncc, bool(dr), True)
        in_specs += specs + [
            pl.BlockSpec((1, 1, 1, N_PAIRS, PAIR_W, n), lambda b, g, t: (b, g, nch - 1 - t, 0, 0, 0)),
            pl.BlockSpec((1, q, X_COLS), lambda b, g, t, ch=ch: (b, ch(t), g))]
        args += list(dirs[dr]) + [entering[dr], dy]
        fs.append(_scan_chunk_fn(bool(dr), dr == 0))
        out_specs += [pl.BlockSpec((1, q, GROUP_COLS), lambda b, g, t, ch=ch: (b, ch(t), g)),
                      pl.BlockSpec((1, q, X_COLS), lambda b, g, t, ch=ch: (b, ch(t), g)),
                      pl.BlockSpec((1, 1, e, q), lambda b, g, t, ch=ch: (b, g, 0, ch(t))),
                      pl.BlockSpec((1, 1, 1, X_COLS), lambda b, g, t: (b, g, 0, 0)),
                      pl.BlockSpec((1, 1, e, 1), lambda b, g, t: (b, g, 0, 0)),
                      pl.BlockSpec((1, 1, 1, X_COLS), lambda b, g, t: (b, g, 0, 0))]
        out_shapes += [jax.ShapeDtypeStruct((nb, lt, SSD_GROUPS * GROUP_COLS), F32),
                       jax.ShapeDtypeStruct((nb, lt, D_INNER), F32), jax.ShapeDtypeStruct((nb, SSD_GROUPS, e, lt), F32),
                       jax.ShapeDtypeStruct((nb, SSD_GROUPS, 1, X_COLS), F32), jax.ShapeDtypeStruct((nb, SSD_GROUPS, e, 1), F32),
                       jax.ShapeDtypeStruct((nb, SSD_GROUPS, 1, X_COLS), F32)]
    n_in = N_IN + 2

    def body(*refs):
        ins = refs[:2 * n_in]
        outs = refs[2 * n_in:2 * n_in + 2 * N_SCAN_GRADS]
        dss = refs[2 * n_in + 2 * N_SCAN_GRADS:]
        t = pl.program_id(2)

        for dr in range(2):
            mine = ins[n_in * dr:n_in * (dr + 1)]
            ent_ref, dy_ref = mine[N_IN], mine[N_IN + 1]
            dx_ref, ddtc_ref, ddtr_ref, dac_ref, dar_ref, dd_ref = outs[N_SCAN_GRADS * dr:N_SCAN_GRADS * (dr + 1)]
            ds_ref = dss[dr]

            @pl.when(t == 0)
            def _():
                for ref in (ds_ref, dac_ref, dar_ref, dd_ref):
                    ref[...] = jnp.zeros_like(ref)

            states = [ent_ref[0, 0, 0, i] for i in range(N_PAIRS)]
            _, vjp = jax.vjp(fs[dr], states, *_scan_operands(*mine[:N_IN]))
            dys = [dy_ref[0, :, pl.ds(PAIR_W * i, PAIR_W)] for i in range(N_PAIRS)]
            gs, gx, gdtc, gdtr, gb, gc, gac, gar, gd = vjp(tuple(dys) + tuple(ds_ref[i] for i in range(N_PAIRS)))
            o = 0
            for part in list(gx) + [gb, gc]:
                dx_ref[0, :, pl.ds(o, part.shape[1])] = part
                o += part.shape[1]
            for i in range(N_PAIRS):
                ds_ref[i] = gs[i]
            ddtc_ref[0] = gdtc
            ddtr_ref[0, 0] = gdtr
            dac_ref[0, 0] += gac
            dar_ref[0, 0] += gar
            dd_ref[0, 0] += gd

    out = pl.pallas_call(
        body, name=name, grid=(nb, SSD_GROUPS, nch), in_specs=in_specs, out_specs=out_specs, out_shape=out_shapes,
        scratch_shapes=[pltpu.VMEM((N_PAIRS, PAIR_W, n), F32)] * 2,
        compiler_params=_cparams(("parallel", "parallel", "arbitrary")),
    )(*args)
    return out[:N_SCAN_GRADS], out[N_SCAN_GRADS:]


def _adamw(w, g, m, v, *, name):
    r, c = w.shape
    tr = _pick(r, (256, 176, 128, 96, 64, 8))
    c1 = 1.0 / (1.0 - ADAM_B1 ** ADAM_STEP)
    c2 = 1.0 / (1.0 - ADAM_B2 ** ADAM_STEP)

    def body(w_ref, g_ref, m_ref, v_ref, d_ref, nm_ref, nv_ref):
        gv = g_ref[...]
        nm = ADAM_B1 * m_ref[...] + (1.0 - ADAM_B1) * gv
        nv = ADAM_B2 * v_ref[...] + (1.0 - ADAM_B2) * (gv * gv)
        d_ref[...] = -ADAM_LR * ((nm * c1) / (jnp.sqrt(nv * c2) + ADAM_EPS) + ADAM_WD * w_ref[...])
        nm_ref[...] = nm
        nv_ref[...] = nv

    spec = pl.BlockSpec((tr, c), lambda i: (i, 0))
    return pl.pallas_call(
        body, name=name, grid=(r // tr,), in_specs=[spec] * 4, out_specs=[spec] * 3,
        out_shape=[jax.ShapeDtypeStruct((r, c), F32)] * 3, compiler_params=_cparams(("parallel",)),
    )(w, g, m, v)


def _sum_rows_tile(r):
    return r if r <= 1024 else _pick(r, (656, 512, 256, 128, 64, 32, 16))


def _sum_slots(x, *, out_dtype, name):
    n, r, c = x.shape
    tr = _sum_rows_tile(r)

    def body(x_ref, o_ref):
        acc = x_ref[0].astype(F32)
        for k in range(1, n):
            acc = acc + x_ref[k].astype(F32)
        o_ref[...] = acc.astype(o_ref.dtype)

    return pl.pallas_call(
        body, name=name, grid=(r // tr,), in_specs=[pl.BlockSpec((n, tr, c), lambda i: (0, i, 0))],
        out_specs=pl.BlockSpec((tr, c), lambda i: (i, 0)), out_shape=jax.ShapeDtypeStruct((r, c), out_dtype),
        compiler_params=_cparams(("parallel",)),
    )(x)


def _sum_list(xs, *, out_dtype, name):
    r, c = xs[0].shape
    tr = _sum_rows_tile(r)

    def body(*refs):
        acc = refs[0][...].astype(F32)
        for ref in refs[1:-1]:
            acc = acc + ref[...].astype(F32)
        refs[-1][...] = acc.astype(refs[-1].dtype)

    spec = pl.BlockSpec((tr, c), lambda i: (i, 0))
    return pl.pallas_call(
        body, name=name, grid=(r // tr,), in_specs=[spec] * len(xs), out_specs=spec,
        out_shape=jax.ShapeDtypeStruct((r, c), out_dtype), compiler_params=_cparams(("parallel",)),
    )(*xs)


ANY = pl.BlockSpec(memory_space=pl.ANY)


def _place():
    return lax.axis_index("x"), lax.axis_index("y"), lax.axis_index("c")


def _allgather_small(v, *, name):
    r, c = v.shape

    def body(v_ref, out_ref, send_sems, recv_sems, local_sem):
        x, y, cc = _place()
        me = 4 * x + 2 * y + cc
        mine = pltpu.make_async_copy(v_ref, out_ref.at[me], local_sem)
        mine.start()
        copies = []
        for k in range(1, N_DEV):
            fx, fy, fc = (k >> 2) & 1, (k >> 1) & 1, k & 1
            peer = (1 - x if fx else x, 1 - y if fy else y, 1 - cc if fc else cc)
            copies.append(pltpu.make_async_remote_copy(
                src_ref=v_ref, dst_ref=out_ref.at[me], send_sem=send_sems.at[k - 1], recv_sem=recv_sems.at[k - 1],
                device_id=peer, device_id_type=MESH))
        for cp in copies:
            cp.start()
        for cp in copies:
            cp.wait()
        mine.wait()

    return pl.pallas_call(
        body, name=name, in_specs=[ANY], out_specs=ANY, out_shape=jax.ShapeDtypeStruct((N_DEV, r, c), v.dtype),
        scratch_shapes=[pltpu.SemaphoreType.DMA((N_DEV - 1,)), pltpu.SemaphoreType.DMA((N_DEV - 1,)),
                        pltpu.SemaphoreType.DMA],
    )(v)


def _other_chips(x, y):
    return [(1 - x, y), (x, 1 - y), (1 - x, 1 - y)]


GATHER_SEMS = [pltpu.SemaphoreType.DMA((6,)), pltpu.SemaphoreType.DMA((6,))]
SCATTER_SEMS = [pltpu.SemaphoreType.DMA((3,)), pltpu.SemaphoreType.DMA((3,))]


def _gather_stage(v_ref, out_ref, send_sems, recv_sems):
    half = v_ref.shape[0] // 2
    x, y, cc = _place()
    sibling = (x, y, 1 - cc)
    chips = _other_chips(x, y)

    def rows(px, py, pc):
        return out_ref.at[2 * px + py, pl.ds(pc * half, half), :]

    def copy(k, block, to, src=None):
        return pltpu.make_async_remote_copy(
            src_ref=rows(*block) if src is None else src, dst_ref=rows(*block),
            send_sem=send_sems.at[k], recv_sem=recv_sems.at[k], device_id=to, device_id_type=MESH)

    my_half = v_ref.at[pl.ds(cc * half, half), :]
    first = [copy(j, (x, y, cc), (*chip, cc), src=my_half) for j, chip in enumerate(chips)]
    passed = [copy(3 + j, (*chip, cc), sibling) for j, chip in enumerate(chips)]

    def start():
        for cp in first:
            cp.start()

    def relay():
        for j, chip in enumerate(chips):
            copy(j, (*chip, cc), (x, y, cc)).wait_recv()
            passed[j].start()

    def finish():
        for j, chip in enumerate(chips):
            copy(3 + j, (*chip, 1 - cc), (x, y, cc)).wait_recv()
        for cp in first + passed:
            cp.wait_send()

    return start, relay, finish


def _gather_shards(mine, *, name):
    r, c = mine.shape

    def body(v_ref, out_ref, send_sems, recv_sems):
        for phase in _gather_stage(v_ref, out_ref, send_sems, recv_sems):
            phase()

    return pl.pallas_call(
        body, name=name, in_specs=[ANY], out_specs=ANY, out_shape=jax.ShapeDtypeStruct((N_CHIPS, r, c), mine.dtype),
        scratch_shapes=GATHER_SEMS,
    )(mine)


def _swap_halves(g, *, name):
    n, _, r, c = g.shape

    def body(g_ref, got_ref, send_sems, recv_sems):
        x, y, cc = _place()
        sibling = (x, y, 1 - cc)
        rems = []
        for j in range(n):
            rems.append(pltpu.make_async_remote_copy(
                src_ref=g_ref.at[j, 1 - cc], dst_ref=got_ref.at[j], send_sem=send_sems.at[j],
                recv_sem=recv_sems.at[j], device_id=sibling, device_id_type=MESH))
        for cp in rems:
            cp.start()
        for cp in rems:
            cp.wait()

    return pl.pallas_call(
        body, name=name, in_specs=[ANY], out_specs=ANY, out_shape=jax.ShapeDtypeStruct((n, r, c), g.dtype),
        scratch_shapes=[pltpu.SemaphoreType.DMA((n,)), pltpu.SemaphoreType.DMA((n,))],
    )(g)


def _scatter_stage(s_ref, out_ref, send_sems, recv_sems):
    x, y, cc = _place()
    me = 2 * x + y
    copies = [pltpu.make_async_remote_copy(
        src_ref=s_ref.at[2 * px + py], dst_ref=out_ref.at[me], send_sem=send_sems.at[j], recv_sem=recv_sems.at[j],
        device_id=(px, py, cc), device_id_type=MESH) for j, (px, py) in enumerate(_other_chips(x, y))]

    def start():
        for cp in copies:
            cp.start()

    def finish():
        for cp in copies:
            cp.wait()

    return start, finish


def _scatter_to_chips(s, *, name):
    def body(s_ref, out_ref, send_sems, recv_sems):
        for phase in _scatter_stage(s_ref, out_ref, send_sems, recv_sems):
            phase()

    return pl.pallas_call(
        body, name=name, in_specs=[ANY], out_specs=ANY, out_shape=jax.ShapeDtypeStruct(s.shape, s.dtype),
        scratch_shapes=SCATTER_SEMS,
    )(s)


def _join_halves(f, *, name):
    r, c = f.shape

    def body(f_ref, out_ref, send_sem, recv_sem):
        x, y, cc = _place()
        cp = pltpu.make_async_remote_copy(src_ref=f_ref, dst_ref=out_ref.at[cc], send_sem=send_sem, recv_sem=recv_sem,
                                          device_id=(x, y, 1 - cc), device_id_type=MESH)
        cp.start()
        cp.wait()

    return pl.pallas_call(
        body, name=name, in_specs=[ANY], out_specs=ANY, out_shape=jax.ShapeDtypeStruct((2, r, c), f.dtype),
        scratch_shapes=[pltpu.SemaphoreType.DMA, pltpu.SemaphoreType.DMA],
    )(f)


def _pack_rows(parts, width=PACK_COLS):
    return jnp.concatenate([p.reshape(-1, width) for p in parts], axis=0)


def _pack_small(parts, rows):
    flat = jnp.concatenate([p.reshape(-1).astype(F32) for p in parts])
    return jnp.pad(flat, (0, rows * LANES - flat.shape[0])).reshape(rows, LANES)


def _unpack_small(packed, shapes):
    flat = packed.reshape(-1)
    out, o = [], 0
    for shp in shapes:
        n = int(np.prod(shp))
        out.append(flat[o:o + n].reshape(shp))
        o += n
    return out


def _perm_in_cols(w):
    a, b = Q_LORA_RANK + KV_LORA_RANK, Q_LORA_RANK + KV_LORA_RANK + QK_ROPE_DIM
    c = IN_WIDTH - 2 * N_SSD_HEADS
    return jnp.concatenate([w[:, :a], w[:, b:c], w[:, a:b], w[:, c:]], axis=1)


def _unperm_in_cols(w):
    a = Q_LORA_RANK + KV_LORA_RANK
    zx = D_INNER + XBC_WIDTH
    return jnp.concatenate([w[:, :a], w[:, a + zx:a + zx + QK_ROPE_DIM], w[:, a:a + zx], w[:, a + zx + QK_ROPE_DIM:]],
                           axis=1)


def _group_xbc(a):
    n = SSD_STATE
    parts = []
    for g in range(SSD_GROUPS):
        parts += [a[..., g * X_COLS:(g + 1) * X_COLS], a[..., D_INNER + g * n:D_INNER + (g + 1) * n],
                  a[..., D_INNER + GN + g * n:D_INNER + GN + (g + 1) * n]]
    return jnp.concatenate(parts, axis=-1)


def _ungroup_xbc(a):
    n = SSD_STATE
    xs = [a[..., g * GROUP_COLS:g * GROUP_COLS + X_COLS] for g in range(SSD_GROUPS)]
    bs = [a[..., g * GROUP_COLS + X_COLS:g * GROUP_COLS + X_COLS + n] for g in range(SSD_GROUPS)]
    cs = [a[..., g * GROUP_COLS + X_COLS + n:(g + 1) * GROUP_COLS] for g in range(SSD_GROUPS)]
    return jnp.concatenate(xs + bs + cs, axis=-1)


UP_BLOCK = 256


def _interleave_up(w):
    parts = []
    for j in range(D_FF // UP_BLOCK):
        parts += [w[:, j * UP_BLOCK:(j + 1) * UP_BLOCK], w[:, D_FF + j * UP_BLOCK:D_FF + (j + 1) * UP_BLOCK]]
    return jnp.concatenate(parts, axis=1)


def _deinterleave_up(w):
    blocks = [w[:, j * UP_BLOCK:(j + 1) * UP_BLOCK] for j in range(2 * D_FF // UP_BLOCK)]
    return jnp.concatenate(blocks[0::2] + blocks[1::2], axis=1)


def _pad_q_heads(w):
    k = w.shape[0]
    return jnp.pad(w.reshape(k, N_ATTN_HEADS, QK_DIM), ((0, 0), (0, 0), (0, HEAD_TILE - QK_DIM))).reshape(k, -1)


def _unpad_q_heads(w):
    k = w.shape[0]
    return w.reshape(k, N_ATTN_HEADS, HEAD_TILE)[..., :QK_DIM].reshape(k, N_ATTN_HEADS * QK_DIM)


def _rope_tables(seq_len):
    n_rows = seq_len // GRID_W
    row = jnp.repeat(jnp.arange(n_rows), GRID_W).astype(F32)
    col = jnp.tile(jnp.arange(GRID_W), n_rows).astype(F32)
    axis_dim = QK_ROPE_DIM // 2
    inv_freq = ROPE_THETA ** (-jnp.arange(0, axis_dim, 2, dtype=F32) / axis_dim)
    ang_r = row[:, None] * inv_freq
    ang_c = col[:, None] * inv_freq
    ang = jnp.concatenate([ang_r, ang_r, ang_c, ang_c], axis=-1)
    return jnp.cos(ang), jnp.sin(ang)


def _rot_matrix(width, start):
    r = np.zeros((width, width), np.float32)
    quarter = QK_ROPE_DIM // 4
    for base in (0, QK_ROPE_DIM // 2):
        for i in range(quarter):
            r[start + base + quarter + i, start + base + i] = -1.0
            r[start + base + i, start + base + quarter + i] = 1.0
    return jnp.asarray(r)


ROPE_STEP = QK_ROPE_DIM // 4


def _rope_flat_fn(x, cos, sin_up, sin_down):
    reps = x.shape[1] // cos.shape[1]

    def heads(t):
        return jnp.concatenate([t] * reps, axis=1)

    return (x * heads(cos) + _roll_lanes(x, -ROPE_STEP) * heads(sin_up) + _roll_lanes(x, ROPE_STEP) * heads(sin_down),)


def _rope_flat_transpose_fn(g, cos, sin_up, sin_down):
    reps = g.shape[1] // cos.shape[1]

    def heads(t):
        return jnp.concatenate([t] * reps, axis=1)

    return (g * heads(cos) + _roll_lanes(g * heads(sin_up), ROPE_STEP) + _roll_lanes(g * heads(sin_down), -ROPE_STEP),)


def _krdt_fn(x, cos, sin, rot, bias):
    lane = lax.broadcasted_iota(jnp.int32, (1, KRDT_WIDTH), 1)
    is_dt = (lane >= QK_ROPE_DIM) & (lane < QK_ROPE_DIM + 2 * N_SSD_HEADS)
    roped = x * cos + _dot_exact(x, rot) * sin
    return (jnp.where(is_dt, _softplus(x + bias), roped),)


def _pre_fn(u, w, shift, scale):
    return (_rms(u, w) * (1.0 + scale) + shift,)


def _norm_fn(x, w):
    return (_rms(x, w),)


def _finish_fn(yf, yb, z, w):
    return (_rms((yf + yb) * _silu(z), w),)


def _mid_fn(x, mix, w_post, w_pre, gate, shift, scale):
    x1 = x + gate * _rms(mix, w_post)
    return (x1, _rms(x1, w_pre) * (1.0 + scale) + shift)


def _loss_fn(x1, ffn, tgt, w_post, gate):
    y = x1 + gate * _rms(ffn, w_post)
    err = y - tgt
    return (0.5 * jnp.mean(err * err, axis=-1, keepdims=True),)


def _bias_fn(x, b):
    return (x + b,)


def _silu_fn(x):
    return (_silu(x),)


def kernel(x, c, ctx, c_ctx, w_mod, b_mod, mix_pre_norm, mix_post_norm, w_in, q_norm, w_q_up, kv_norm, w_kv_up, ssd_conv_w, ssd_conv_b, ssd_a_log, ssd_dt_bias, ssd_d, ssd_norm, w_out, ffn_pre_norm, ffn_post_norm, w_up, ffn_conv_w, ffn_conv_b, w_down, loss_target, m_c_ctx, m_w_mod, m_b_mod, m_mix_pre_norm, m_mix_post_norm, m_w_in, m_q_norm, m_w_q_up, m_kv_norm, m_w_kv_up, m_ssd_conv_w, m_ssd_conv_b, m_ssd_a_log, m_ssd_dt_bias, m_ssd_d, m_ssd_norm, m_w_out, m_ffn_pre_norm, m_ffn_post_norm, m_w_up, m_ffn_conv_w, m_ffn_conv_b, m_w_down, v_c_ctx, v_w_mod, v_b_mod, v_mix_pre_norm, v_mix_post_norm, v_w_in, v_q_norm, v_w_q_up, v_kv_norm, v_w_kv_up, v_ssd_conv_w, v_ssd_conv_b, v_ssd_a_log, v_ssd_dt_bias, v_ssd_d, v_ssd_norm, v_w_out, v_ffn_pre_norm, v_ffn_post_norm, v_w_up, v_ffn_conv_w, v_ffn_conv_b, v_w_down):
    args = dict(locals())
    names = ["c_ctx", "w_mod", "b_mod", "mix_pre_norm", "mix_post_norm", "w_in", "q_norm", "w_q_up", "kv_norm",
             "w_kv_up", "ssd_conv_w", "ssd_conv_b", "ssd_a_log", "ssd_dt_bias", "ssd_d", "ssd_norm", "w_out",
             "ffn_pre_norm", "ffn_post_norm", "w_up", "ffn_conv_w", "ffn_conv_b", "w_down"]
    nb, s, d = x.shape
    nctx_rows = ctx.shape[1]
    lt = nctx_rows + s
    tl = 256 if (nctx_rows % 256 == 0 and s % 256 == 0) else 128
    nctx = nctx_rows // tl
    ncc = nctx_rows // SSD_CHUNK
    h, e, g2 = N_ATTN_HEADS, HEADS_PER_GROUP, SSD_GROUPS
    chip = 2 * lax.axis_index("x") + lax.axis_index("y")

    big_local = {n: args[n][0] for n, _, _, _ in BIG}
    big_info = {n: (rows, cols, axis) for n, rows, cols, axis in BIG}

    def pack_shards(group):
        return _pack_rows([big_local[n].astype(WIRE_DTYPE) for n in group])

    def unpack_gathered(gathered, mine, group):
        gathered = lax.dynamic_update_slice(gathered, mine[None], (chip, 0, 0))
        res, o = {}, 0
        for n in group:
            rows, cols, axis = big_info[n]
            lr, lc = big_local[n].shape
            nr = lr * lc // PACK_COLS
            seg = gathered[:, o:o + nr].reshape(N_CHIPS, lr, lc)
            o += nr
            res[n] = seg.reshape(rows, cols) if axis == 0 else jnp.transpose(seg, (1, 0, 2)).reshape(rows, cols)
        return res

    core = lax.axis_index("c")

    def pair_sums(grads_full, group, tag):
        parts = []
        for n in group:
            _, _, axis = big_info[n]
            lr, lc = big_local[n].shape
            gfull = grads_full[n]
            shards = (gfull.reshape(N_CHIPS, lr, lc) if axis == 0
                      else jnp.transpose(gfull.reshape(lr, N_CHIPS, lc), (1, 0, 2)))
            parts.append(shards.reshape(N_CHIPS, lr * lc // PACK_COLS, PACK_COLS))
        gpack = jnp.concatenate(parts, axis=1).astype(WIRE_DTYPE)
        half = gpack.shape[1] // 2
        gpack = gpack.reshape(N_CHIPS, 2, half, PACK_COLS)
        got = _swap_halves(gpack, name="grad_swap_" + tag)
        own = lax.dynamic_index_in_dim(gpack, core, axis=1, keepdims=False)
        flat = (N_CHIPS * half, PACK_COLS)
        return _sum_list([own.reshape(flat), got.reshape(flat)], out_dtype=WIRE_DTYPE,
                         name="grad_add_pair_" + tag).reshape(N_CHIPS, half, PACK_COLS)

    def chip_total(sums, scattered, tag):
        mine_sum = lax.dynamic_index_in_dim(sums, chip, axis=0, keepdims=True)
        scattered = lax.dynamic_update_slice(scattered, mine_sum, (chip, 0, 0))
        return _sum_slots(scattered, out_dtype=F32, name="grad_add_chips_" + tag)

    packed_now, packed_late = pack_shards(GATHER_NOW), pack_shards(GATHER_LATE)
    full = unpack_gathered(_gather_shards(packed_now, name="gather_weights"), packed_now, GATHER_NOW)
    n_sc, n_fc = ssd_conv_w.shape[2], ffn_conv_w.shape[2]
    n_conv = SSD_CONV * n_sc + FFN_CONV * n_fc
    first_rows = -(-(n_conv + nb * d) // (8 * LANES)) * 8
    first_all = _allgather_small(_pack_small([ssd_conv_w[0], ffn_conv_w[0], c], first_rows), name="gather_conv_c")
    first_all = first_all.reshape(N_DEV, -1)
    conv_all = first_all[::2]
    ssd_conv_full = jnp.concatenate(
        [conv_all[j][:SSD_CONV * n_sc].reshape(SSD_CONV, n_sc) for j in range(N_CHIPS)], axis=1)
    ffn_conv_full = jnp.concatenate(
        [conv_all[j][SSD_CONV * n_sc:n_conv].reshape(FFN_CONV, n_fc) for j in range(N_CHIPS)], axis=1)
    c_every = first_all[:, n_conv:n_conv + nb * d].reshape(N_DEV * nb, d)

    w_in_p = _perm_in_cols(full["w_in"])
    o_cq, o_ckv, o_z = 0, Q_LORA_RANK, Q_LORA_RANK + KV_LORA_RANK
    o_xbc, o_kr = o_z + D_INNER, o_z + D_INNER + XBC_WIDTH
    w_krdt = jnp.pad(w_in_p[:, o_kr:], ((0, 0), (0, KRDT_WIDTH - QK_ROPE_DIM - 2 * N_SSD_HEADS)))
    w_segs = [w_in_p[:, o_cq:o_ckv], w_in_p[:, o_ckv:o_z], w_in_p[:, o_z:o_xbc], _group_xbc(w_in_p[:, o_xbc:o_kr]),
              w_krdt]
    ssd_conv_g, ssd_conv_b_g = _group_xbc(ssd_conv_full), _group_xbc(ssd_conv_b)
    w_q_pad = _pad_q_heads(full["w_q_up"])

    mod_rows = 16
    n_ex = N_DEV * nb
    all_rows = -(-(n_ex + 1) // 16) * 16
    me = 2 * chip + lax.axis_index("c")
    c_all = jnp.concatenate([c_every, c_ctx[None, :], jnp.zeros((all_rows - n_ex - 1, d), F32)], axis=0)[None]
    (s_all,) = _row_fwd(_silu_fn, toks=[(c_all, 0, None, 0)], outs=[(d, F32)], nb=1, nl=all_rows, tl=all_rows,
                        name="mod_silu")
    w_mod_local = w_mod[0]
    mod_cols = w_mod_local.shape[1]
    mod_part = _mm(s_all[0], w_mod_local, name="mod_mm")
    mod_parts = _allgather_small(mod_part, name="gather_mod")[::2]
    mod_every = jnp.concatenate([mod_parts[j] for j in range(N_CHIPS)], axis=1)
    mod_lin = jnp.concatenate([lax.dynamic_slice_in_dim(mod_every, me * nb, nb, axis=0), mod_every[n_ex:n_ex + 1],
                               jnp.zeros((mod_rows - nb - 1, N_MOD * d), F32)], axis=0)
    (mod,) = _row_fwd(_bias_fn, toks=[(mod_lin[None], 0, None, 0)], vecs=[b_mod], outs=[(N_MOD * d, F32)], nb=1,
                      nl=mod_rows, tl=mod_rows, name="mod_bias")
    mods = [mod[0][:, k * d:(k + 1) * d][:, None, :] for k in range(N_MOD)]
    mods_lat = [m[:nb] for m in mods]

    u = jnp.concatenate([ctx, x], axis=1)
    (h1,) = _row_fwd(_pre_fn, toks=[(u, 0, None, 0)], vecs=[mix_pre_norm], bvecs=[mods[0], mods[1]],
                     outs=[(d, MXU_DTYPE)], nb=nb, nl=lt, tl=tl, nctx=nctx, name="pre1")
    h1f = h1.reshape(nb * lt, d)
    p_cq, p_ckv, p_z, p_xbc, p_krdt = [
        _mm(h1f, w, name="in_" + nm).reshape(nb, lt, -1)
        for nm, w in zip(("cq", "ckv", "z", "xbc", "krdt"), w_segs)]

    (cqn,) = _row_fwd(_norm_fn, toks=[(p_cq, nctx, None, 0)], vecs=[q_norm], outs=[(Q_LORA_RANK, MXU_DTYPE)],
                      nb=nb, nl=s, tl=tl, name="q_norm")
    q_flat = _mm(cqn.reshape(nb * s, -1), w_q_pad, name="q_up").reshape(nb, s, h * HEAD_TILE)
    cos, sin = _rope_tables(s)
    ones, zeros = jnp.ones((s, QK_NOPE_DIM), F32), jnp.zeros((s, QK_NOPE_DIM), F32)
    tail = HEAD_TILE - QK_DIM
    up_lanes = ((jnp.arange(QK_ROPE_DIM) // ROPE_STEP) % 2 == 0)[None, :]
    q_tables = [jnp.concatenate([pad, t, pad[:, :tail]], axis=1)[None]
                for pad, t in ((ones, cos), (zeros, jnp.where(up_lanes, -sin, 0.0)), (zeros, jnp.where(up_lanes, 0.0, sin)))]
    tq = 256
    (q_roped,) = _row_fwd(_rope_flat_fn, toks=[(q_flat, 0, None, 0)], poss=q_tables, outs=[(h * HEAD_TILE, MXU_DTYPE)],
                          nb=nb, nl=s, tl=tl, name="rope_q")

    (ckvn,) = _row_fwd(_norm_fn, toks=[(p_ckv, 0, None, 0)], vecs=[kv_norm], outs=[(KV_LORA_RANK, MXU_DTYPE)],
                       nb=nb, nl=lt, tl=tl, name="kv_norm")
    kv_flat = _mm(ckvn.reshape(nb * lt, -1), full["w_kv_up"], out_dtype=MXU_DTYPE, name="kv_up").reshape(nb, lt, -1)

    pad_w = KRDT_WIDTH - QK_ROPE_DIM
    cos_k = jnp.concatenate([jnp.ones((nctx_rows, KRDT_WIDTH), F32),
                             jnp.concatenate([cos, jnp.ones((s, pad_w), F32)], axis=1)], axis=0)[None]
    sin_k = jnp.concatenate([jnp.zeros((nctx_rows, KRDT_WIDTH), F32),
                             jnp.concatenate([sin, jnp.zeros((s, pad_w), F32)], axis=1)], axis=0)[None]
    rot_k = _rot_matrix(KRDT_WIDTH, 0)
    dt_bias_row = jnp.pad(ssd_dt_bias.reshape(1, -1), ((0, 0), (QK_ROPE_DIM, pad_w - 2 * N_SSD_HEADS)))
    (krdt,) = _row_fwd(_krdt_fn, toks=[(p_krdt, 0, None, 0)], poss=[cos_k, sin_k], vecs=[rot_k, dt_bias_row],
                       outs=[(KRDT_WIDTH, F32)], nb=nb, nl=lt, tl=tl, name="krdt")
    kr = jnp.pad(krdt[..., :QK_ROPE_DIM].astype(MXU_DTYPE), ((0, 0), (0, 0), (QK_NOPE_DIM, HEAD_TILE - QK_DIM)))
    attn, lse, gathered_late = _attn_fwd(q_roped, kv_flat, kr, packed_late, tq=tq, name="attn_fwd")
    full.update(unpack_gathered(gathered_late, packed_late, GATHER_LATE))
    w_up_il = _interleave_up(full["w_up"])
    w_out_a, w_out_s = full["w_out"][:ATTN_WIDTH], full["w_out"][ATTN_WIDTH:]

    seg = nctx_rows

    def conv_ssd_fn(xv, w, b):
        return (_silu(_dwconv(xv, w, seg) + b),)

    def conv_ssd_bwd(xv, w, b, dy):
        cv = _dwconv(xv, w, seg) + b
        sg = _sigmoid(cv)
        dc = dy * (sg * (1.0 + cv * (1.0 - sg)))
        dx, dw = _dwconv_back(xv, dc, w, seg)
        return dx, dw, jnp.sum(dc, axis=0, keepdims=True)

    cb_ssd = 256
    conv_vecs = [(ssd_conv_g, 0), (ssd_conv_b_g, 0)]
    (xbc,) = _seq_fwd(conv_ssd_fn, toks=[(p_xbc, 0, 1)], vecs=conv_vecs, outs=[F32], nb=nb, nl=lt, nc=XBC_WIDTH,
                      cb=cb_ssd, name="conv_ssd")
    dt = krdt[..., QK_ROPE_DIM:QK_ROPE_DIM + 2 * N_SSD_HEADS].reshape(nb, lt, 2, g2, e)
    dt_lane = QK_ROPE_DIM + N_SSD_HEADS * jnp.arange(2)[:, None, None] + jnp.arange(D_INNER)[None, None, :] // SSD_HEAD_DIM
    spread = (jnp.arange(KRDT_WIDTH)[None, :, None] == dt_lane).astype(F32)

    def spread_fn(v, s0, s1):
        return (_mask_dot_raw(s0, v, False), _mask_dot_raw(s1, v, False))

    dtc = _row_fwd(spread_fn, toks=[(krdt, 0, None, 0)], vecs=[spread[0], spread[1]],
                   outs=[(D_INNER, F32), (D_INNER, F32)], nb=nb, nl=lt, tl=tl, name="dt_spread")
    dtr = jnp.transpose(dt, (2, 0, 3, 4, 1))
    a_neg = -jnp.exp(ssd_a_log[0]).reshape(2, g2, e)
    d_chan = jnp.repeat(ssd_d[0], SSD_HEAD_DIM).reshape(g2, 1, X_COLS)
    a_chan = [jnp.repeat(a_neg[dr].reshape(-1), SSD_HEAD_DIM).reshape(g2, 1, X_COLS) for dr in range(2)]
    scan_args = [(xbc, dtc[dr], dtr[dr], a_chan[dr], a_neg[dr][:, :, None], d_chan) for dr in range(2)]
    (y0, ent0), (y1, ent1) = _scan_fwd(scan_args, ncc=ncc, name="scan_fwd")
    ys, ents = [y0, y1], [ent0, ent1]
    (ssd,) = _row_fwd(_finish_fn, toks=[(ys[0], nctx, None, 0), (ys[1], nctx, None, 0), (p_z, nctx, None, 0)],
                      vecs=[ssd_norm], outs=[(D_INNER, MXU_DTYPE)], nb=nb, nl=s, tl=tl, name="ssd_finish")

    attn_f, ssd_f = attn.reshape(nb * s, ATTN_WIDTH), ssd.reshape(nb * s, D_INNER)
    mix = _mm_sum([(attn_f, w_out_a), (ssd_f, w_out_s)], name="out_proj").reshape(nb, s, d)

    mid_bvecs = [mods_lat[2], mods_lat[3], mods_lat[4]]
    x1, h2 = _row_fwd(_mid_fn, toks=[(x, 0, None, 0), (mix, 0, None, 0)], vecs=[mix_post_norm, ffn_pre_norm],
                      bvecs=mid_bvecs, outs=[(d, F32), (d, MXU_DTYPE)], nb=nb, nl=s, tl=tl, name="mid")
    up = _mm(h2.reshape(nb * s, d), w_up_il, name="ffn_up").reshape(nb, s, 2 * D_FF)

    def glu_fn(gv, w, b):
        return (_gelu(_dwconv(gv[:, :UP_BLOCK], w, 0) + b) * gv[:, UP_BLOCK:],)

    def glu_bwd(gv, w, b, da):
        gate, val = gv[:, :UP_BLOCK], gv[:, UP_BLOCK:]
        cv = _dwconv(gate, w, 0) + b
        cdf = 0.5 * (1.0 + lax.erf(cv * (2.0 ** -0.5)))
        pdf = jnp.exp(-0.5 * cv * cv) * (1.0 / math.sqrt(2.0 * math.pi))
        dc = (da * val) * (cdf + cv * pdf)
        dgate, dw = _dwconv_back(gate, dc, w, 0)
        return jnp.concatenate([dgate, da * (cv * cdf)], axis=1), dw, jnp.sum(dc, axis=0, keepdims=True)

    cb_ffn = UP_BLOCK
    glu_toks = [(up, 0, 2)]
    glu_vecs = [(ffn_conv_full, 0), (ffn_conv_b, 0)]
    (act,) = _seq_fwd(glu_fn, toks=glu_toks, vecs=glu_vecs, outs=[MXU_DTYPE], nb=nb, nl=s, nc=D_FF, cb=cb_ffn,
                      name="conv_glu")
    ffn = _mm(act.reshape(nb * s, D_FF), full["w_down"], name="ffn_down").reshape(nb, s, d)

    loss_toks = [(x1, 0, None, 0), (ffn, 0, None, 0), (loss_target, 0, None, 0)]
    ones_rows = jnp.ones((nb, s, 1), F32)
    (dx1_a, dffn, _), (g_ffn_post,), ((g_gate5, _),), (loss_rows,) = _row_bwd(
        _loss_fn, toks=loss_toks, vecs=[ffn_post_norm], bvecs=[mods_lat[5]], cots=[(ones_rows, 0)],
        tok_grads=[F32, MXU_DTYPE, None], emit=[(0, 1, F32)], nb=nb, nl=s, tl=tl, name="loss_bwd")
    loss_part = jnp.sum(loss_rows)

    dffn_f = dffn.reshape(nb * s, d)
    g_w_down = _mm(act.reshape(nb * s, D_FF), dffn_f, ta=True, name="wg_down")
    dact = _mm(dffn_f, full["w_down"], tb=True, out_dtype=MXU_DTYPE, name="dg_down").reshape(nb, s, D_FF)
    (dup,), (g_ffn_conv_w, g_ffn_conv_b) = _seq_bwd(
        glu_fn, toks=glu_toks, vecs=glu_vecs, cots=[[dact]], tok_grads=[MXU_DTYPE], nb=nb, nl=s, nc=D_FF,
        cb=cb_ffn, name="conv_glu_bwd", bwd_fn=glu_bwd)
    dup = dup.reshape(nb * s, 2 * D_FF)
    g_w_up = _deinterleave_up(_mm(h2.reshape(nb * s, d), dup, ta=True, name="wg_up"))
    dh2 = _mm(dup, w_up_il, tb=True, name="dg_up").reshape(nb, s, d)

    (dx_res, dmix), (g_mix_post, g_ffn_pre), ((g_gate2, _), (g_shift3, _), (g_scale4, _)), _ = _row_bwd(
        _mid_fn, toks=[(x, 0, None, 0), (mix, 0, None, 0)], vecs=[mix_post_norm, ffn_pre_norm], bvecs=mid_bvecs,
        cots=[(dx1_a, 0), (dh2, 0)], tok_grads=[F32, MXU_DTYPE], nb=nb, nl=s, tl=tl, name="mid_bwd")

    dmix_f = dmix.reshape(nb * s, d)
    g_w_out = jnp.concatenate([_mm(attn_f, dmix_f, ta=True, name="wg_out_attn"),
                               _mm(ssd_f, dmix_f, ta=True, name="wg_out_ssd")], axis=0)
    early_sums = pair_sums({"w_up": g_w_up, "w_down": g_w_down, "w_out": g_w_out}, REDUCE_EARLY, "early")
    dattn = _mm(dmix_f, w_out_a, tb=True, name="dg_out_attn").reshape(nb, s, ATTN_WIDTH)
    dssd = _mm(dmix_f, w_out_s, tb=True, name="dg_out_ssd").reshape(nb, s, D_INNER)

    (dy, _, dz), (g_ssd_norm,), _, _ = _row_bwd(
        _finish_fn, toks=[(ys[0], 0, None, 0), (ys[1], 0, None, 0), (p_z, 0, None, 0)], vecs=[ssd_norm],
        cots=[(dssd, -nctx)], tok_grads=[F32, None, MXU_DTYPE], nb=nb, nl=lt, tl=tl, name="ssd_finish_bwd")
    scan_grads = _scan_bwd(scan_args, ents, dy, ncc=ncc, name="scan_bwd")

    def collect_fn(g0, g1, c0, c1):
        return (_mask_dot_raw(c0, g0, False) + _mask_dot_raw(c1, g1, False),)

    (g_dt_lanes,) = _row_fwd(collect_fn, toks=[(scan_grads[0][1], 0, None, 0), (scan_grads[1][1], 0, None, 0)],
                             vecs=[spread[0].T, spread[1].T], outs=[(KRDT_WIDTH, F32)], nb=nb, nl=lt, tl=tl,
                             name="dt_collect")
    g_dt_dirs, g_a = [], []
    for _, _, gdtr, gac, gar, _ in scan_grads:
        g_dt_dirs.append(jnp.transpose(gdtr, (0, 3, 1, 2)))
        g_a.append(jnp.sum(jnp.sum(gac.reshape(nb, g2, e, SSD_HEAD_DIM), axis=-1) + gar[:, :, :, 0], axis=0))
    g_d_chan = jnp.sum(scan_grads[0][5], axis=0)
    g_a_log = (jnp.stack(g_a) * a_neg).reshape(1, 2, N_SSD_HEADS)
    g_dt = (jnp.stack(g_dt_dirs, axis=2).reshape(nb, lt, 2 * N_SSD_HEADS)
            + g_dt_lanes[..., QK_ROPE_DIM:QK_ROPE_DIM + 2 * N_SSD_HEADS])
    (dp_xbc,), (g_ssd_conv_w, g_ssd_conv_b) = _seq_bwd(
        conv_ssd_fn, toks=[(p_xbc, 0, 1)], vecs=conv_vecs, cots=[[scan_grads[0][0], scan_grads[1][0]]],
        tok_grads=[MXU_DTYPE], nb=nb, nl=lt, nc=XBC_WIDTH, cb=cb_ssd, name="conv_ssd_bwd", bwd_fn=conv_ssd_bwd)
    g_ssd_conv_w, g_ssd_conv_b = _ungroup_xbc(g_ssd_conv_w), _ungroup_xbc(g_ssd_conv_b)

    dq_roped, dkv, dkr, early_scattered = _attn_bwd(q_roped, kv_flat, kr, attn, lse, dattn, early_sums, tq=tq,
                                                    name="attn_bwd")
    (dq_flat,) = _row_fwd(_rope_flat_transpose_fn, toks=[(dq_roped, 0, None, 0)], poss=q_tables,
                          outs=[(h * HEAD_TILE, MXU_DTYPE)], nb=nb, nl=s, tl=tl, name="rope_q_bwd")
    dq_flat = dq_flat.reshape(nb * s, h * HEAD_TILE)
    g_w_q_up = _unpad_q_heads(_mm(cqn.reshape(nb * s, -1), dq_flat, ta=True, name="wg_q_up"))
    dcqn = _mm(dq_flat, w_q_pad, tb=True, name="dg_q_up").reshape(nb, s, Q_LORA_RANK)
    (dp_cq,), (g_q_norm,), _, _ = _row_bwd(_norm_fn, toks=[(p_cq, 0, None, 0)], vecs=[q_norm], cots=[(dcqn, -nctx)],
                                           tok_grads=[MXU_DTYPE], nb=nb, nl=lt, tl=tl, name="q_norm_bwd")

    dkv_flat = dkv.reshape(nb * lt, -1)
    g_w_kv_up = _mm(ckvn.reshape(nb * lt, -1), dkv_flat, ta=True, name="wg_kv_up")
    dckvn = _mm(dkv_flat, full["w_kv_up"], tb=True, name="dg_kv_up").reshape(nb, lt, KV_LORA_RANK)
    (dp_ckv,), (g_kv_norm,), _, _ = _row_bwd(_norm_fn, toks=[(p_ckv, 0, None, 0)], vecs=[kv_norm], cots=[(dckvn, 0)],
                                             tok_grads=[MXU_DTYPE], nb=nb, nl=lt, tl=tl, name="kv_norm_bwd")

    g_krdt = jnp.concatenate([dkr[..., QK_NOPE_DIM:QK_DIM], g_dt, jnp.zeros((nb, lt, pad_w - 2 * N_SSD_HEADS), F32)],
                             axis=-1)
    (dp_krdt,), (_, g_dt_bias_row), _, _ = _row_bwd(
        _krdt_fn, toks=[(p_krdt, 0, None, 0)], poss=[cos_k, sin_k], vecs=[rot_k, dt_bias_row], cots=[(g_krdt, 0)],
        tok_grads=[MXU_DTYPE], nb=nb, nl=lt, tl=tl, name="krdt_bwd")

    dp_segs = [t.reshape(nb * lt, -1) for t in (dp_cq, dp_ckv, dz, dp_xbc, dp_krdt)]
    g_segs = [_mm(h1f, t, ta=True, name="wg_in_" + nm) for nm, t in zip(("cq", "ckv", "z", "xbc", "krdt"), dp_segs)]
    g_segs[3] = _ungroup_xbc(g_segs[3])
    g_w_in_p = jnp.concatenate(g_segs, axis=1)
    dh1 = _mm_sum(list(zip(dp_segs, w_segs)), tb=True, name="dg_in").reshape(nb, lt, d)

    def pre_res_fn(uv, w, shift, scale):
        return _pre_fn(uv, w, shift, scale) + (uv,)

    (grad_x,), (g_mix_pre,), ((g_shift0, g_shift0c), (g_scale1, g_scale1c)), _ = _row_bwd(
        pre_res_fn, toks=[(u, 0, None, 0)], vecs=[mix_pre_norm], bvecs=[mods[0], mods[1]],
        cots=[(dh1, 0), (dx_res, -nctx)], tok_grads=[F32], nb=nb, nl=lt, tl=tl, nctx=nctx, drop_blocks=nctx,
        name="pre1_bwd")

    zero_row = jnp.zeros((1, 1, d), F32)
    lat = [g_shift0, g_scale1, g_gate2, g_shift3, g_scale4, g_gate5]
    ctxg = [g_shift0c, g_scale1c, zero_row, zero_row, zero_row, zero_row]
    dmod = jnp.concatenate([jnp.concatenate([a, b], axis=0)[:, 0, :] for a, b in zip(lat, ctxg)], axis=-1)
    dmod = jnp.pad(dmod, ((0, mod_rows - nb - 1), (0, 0)))
    _, (g_b_mod,), _, _ = _row_bwd(_bias_fn, toks=[(mod_lin[None], 0, None, 0)], vecs=[b_mod], cots=[(dmod[None], 0)],
                                   tok_grads=[None], nb=1, nl=mod_rows, tl=mod_rows, name="mod_bias_bwd")
    dmod_all = _allgather_small(dmod[:8], name="gather_dmod")
    dmod_ctx = _sum_slots(dmod_all, out_dtype=F32, name="dmod_ctx_add")[nb:nb + 1]
    dmod_every = jnp.concatenate([dmod_all[:, :nb].reshape(n_ex, N_MOD * d), dmod_ctx,
                                  jnp.zeros((all_rows - n_ex - 1, N_MOD * d), F32)], axis=0)
    dmod_mine = lax.dynamic_slice_in_dim(dmod_every, chip * mod_cols, mod_cols, axis=1)
    g_w_mod = _mm(s_all[0], dmod_mine, ta=True, name="wg_mod")[None]
    ds_all = _mm(dmod_mine, w_mod_local, tb=True, name="dg_mod")
    (dc_all,), _, _, _ = _row_bwd(_silu_fn, toks=[(c_all, 0, None, 0)], cots=[(ds_all[None], 0)], tok_grads=[F32],
                                  nb=1, nl=all_rows, tl=all_rows, name="mod_silu_bwd")
    g_c_ctx = 0.5 * dc_all[0, n_ex]

    g_w_in = _unperm_in_cols(g_w_in_p[:, :IN_WIDTH])
    last_sums = pair_sums({"w_in": g_w_in, "w_q_up": g_w_q_up, "w_kv_up": g_w_kv_up}, REDUCE_LAST, "last")
    halves = [chip_total(early_sums, early_scattered, "early"),
              chip_total(last_sums, _scatter_to_chips(last_sums, name="grad_scatter"), "last")]
    my_halves = jnp.concatenate(halves, axis=0)
    joined = lax.dynamic_update_slice(_join_halves(my_halves, name="grad_join"), my_halves[None], (core, 0, 0))
    g_shards, o = {}, 0
    for group, hv in zip((REDUCE_EARLY, REDUCE_LAST), halves):
        g_shards[group] = joined[:, o:o + hv.shape[0]].reshape(2 * hv.shape[0], PACK_COLS)
        o += hv.shape[0]

    g_d = jnp.sum(g_d_chan.reshape(N_SSD_HEADS, SSD_HEAD_DIM), axis=1)[None]
    g_dt_bias = g_dt_bias_row[:, QK_ROPE_DIM:QK_ROPE_DIM + 2 * N_SSD_HEADS].reshape(1, 2, N_SSD_HEADS)
    small_names = ["c_ctx", "b_mod", "mix_pre_norm", "mix_post_norm", "q_norm", "kv_norm", "ssd_conv_w", "ssd_conv_b",
                   "ssd_a_log", "ssd_dt_bias", "ssd_d", "ssd_norm", "ffn_pre_norm", "ffn_post_norm", "ffn_conv_w",
                   "ffn_conv_b"]
    small_grads = [g_c_ctx, g_b_mod, g_mix_pre, g_mix_post, g_q_norm, g_kv_norm, g_ssd_conv_w, g_ssd_conv_b,
                   g_a_log, g_dt_bias, g_d, g_ssd_norm, g_ffn_pre, g_ffn_post, g_ffn_conv_w, g_ffn_conv_b]
    small_shapes = [tuple(np.shape(a)) for a in small_grads] + [()]
    n_small = sum(int(np.prod(shp)) for shp in small_shapes)
    small_rows = -(-n_small // (8 * LANES)) * 8
    small_all = _allgather_small(_pack_small(small_grads + [loss_part], small_rows), name="gather_small")
    small_sum = _sum_slots(small_all, out_dtype=F32, name="small_add")
    small_red = _unpack_small(small_sum, small_shapes)
    loss = small_red[-1]
    grads = dict(zip(small_names, small_red[:-1]))
    grads["ssd_conv_w"] = lax.dynamic_slice_in_dim(grads["ssd_conv_w"], chip * n_sc, n_sc, axis=1)[None]
    grads["ffn_conv_w"] = lax.dynamic_slice_in_dim(grads["ffn_conv_w"], chip * n_fc, n_fc, axis=1)[None]
    for n in small_names:
        grads[n] = grads[n].reshape(args[n].shape)

    delta, new_m, new_v = {}, {}, {}
    grads["w_mod"] = g_w_mod
    for group, g_shard in g_shards.items():
        o = 0
        for n in group:
            lr, lc = big_local[n].shape
            nr = lr * lc // PACK_COLS
            grads[n] = g_shard[o:o + nr].reshape(1, lr, lc)
            o += nr
    for n in ["w_mod"] + [n for n, _, _, _ in BIG]:
        dl, nm, nv = _adamw(args[n][0], grads[n][0], args["m_" + n][0], args["v_" + n][0], name="adamw_" + n)
        delta[n], new_m[n], new_v[n] = dl[None], nm[None], nv[None]
    sm_shapes = [args[n].shape for n in small_names]
    n_sm = sum(int(np.prod(shp)) for shp in sm_shapes)
    sm_rows = -(-n_sm // (8 * LANES)) * 8
    packs = [_pack_small([src[n] for n in small_names], sm_rows)
             for src in (args, grads, {n: args["m_" + n] for n in small_names}, {n: args["v_" + n] for n in small_names})]
    for out_dict, packed_out in zip((delta, new_m, new_v), _adamw(*packs, name="adamw_small")):
        out_dict.update(zip(small_names, _unpack_small(packed_out, sm_shapes)))

    return (loss, grad_x, *[grads[n] for n in names], *[delta[n] for n in names], *[new_m[n] for n in names],
            *[new_v[n] for n in names])
```

```python
import functools
import math

import numpy as np
import jax
import jax.numpy as jnp
from jax import lax
from jax.experimental import pallas as pl
from jax.experimental.pallas import tpu as pltpu

F32 = jnp.float32
MXU_DTYPE = jnp.bfloat16
WIRE_DTYPE = jnp.bfloat16
VMEM_LIMIT_BYTES = 56 * 1024 * 1024
HIGHEST = lax.Precision.HIGHEST

D_MODEL = 1024
N_MOD = 6
EPS = 1e-6
GRID_W = 64
N_ATTN_HEADS = 16
QK_NOPE_DIM = 64
QK_ROPE_DIM = 32
QK_DIM = QK_NOPE_DIM + QK_ROPE_DIM
V_HEAD_DIM = 64
Q_LORA_RANK = 384
KV_LORA_RANK = 256
ROPE_THETA = 10000.0
ATTN_SCALE = QK_DIM ** -0.5
ATTN_WIDTH = N_ATTN_HEADS * V_HEAD_DIM
N_SSD_HEADS = 16
SSD_HEAD_DIM = 64
SSD_GROUPS = 2
HEADS_PER_GROUP = N_SSD_HEADS // SSD_GROUPS
SSD_STATE = 128
SSD_CONV = 5
SSD_CHUNK = 128
D_INNER = N_SSD_HEADS * SSD_HEAD_DIM
GN = SSD_GROUPS * SSD_STATE
XBC_WIDTH = D_INNER + 2 * GN
D_FF = 2816
FFN_CONV = 3
KRDT_WIDTH = 128
IN_WIDTH = Q_LORA_RANK + KV_LORA_RANK + QK_ROPE_DIM + D_INNER + XBC_WIDTH + 2 * N_SSD_HEADS

ADAM_LR = 0.001
ADAM_B1 = 0.9
ADAM_B2 = 0.999
ADAM_EPS = 1e-08
ADAM_WD = 0.01
ADAM_STEP = 10

N_CHIPS = 4
N_DEV = 8
MESH = pl.DeviceIdType.MESH
LANES = 128

BIG = (("w_in", D_MODEL, IN_WIDTH, 1),
       ("w_q_up", Q_LORA_RANK, N_ATTN_HEADS * QK_DIM, 1),
       ("w_kv_up", KV_LORA_RANK, N_ATTN_HEADS * (QK_NOPE_DIM + V_HEAD_DIM), 1),
       ("w_out", ATTN_WIDTH + D_INNER, D_MODEL, 0), ("w_up", D_MODEL, 2 * D_FF, 1),
       ("w_down", D_FF, D_MODEL, 0))
PACK_COLS = 1024
GATHER_NOW, GATHER_LATE = ("w_in", "w_q_up", "w_kv_up"), ("w_out", "w_up", "w_down")
REDUCE_EARLY, REDUCE_LAST = ("w_up", "w_down", "w_out"), ("w_in", "w_q_up", "w_kv_up")


def _cparams(sem):
    return pltpu.CompilerParams(dimension_semantics=sem, vmem_limit_bytes=VMEM_LIMIT_BYTES)


def _pick(n, cands):
    for c in cands:
        if n % c == 0:
            return c
    return n


def _sigmoid(x):
    return 0.5 * (jnp.tanh(0.5 * x) + 1.0)


def _silu(x):
    return x * _sigmoid(x)


@jax.custom_vjp
def _softplus(x):
    u = jnp.exp(-jnp.abs(x))
    w = 1.0 + u
    log1p = jnp.where(w == 1.0, u, jnp.log(w) * (u / jnp.where(w == 1.0, 1.0, w - 1.0)))
    return jnp.maximum(x, 0.0) + log1p


def _softplus_fwd(x):
    return _softplus(x), x


def _softplus_bwd(x, g):
    return (g * _sigmoid(x),)


_softplus.defvjp(_softplus_fwd, _softplus_bwd)


@jax.custom_vjp
def _gelu(x):
    return 0.5 * x * (1.0 + lax.erf(x * (2.0 ** -0.5)))


def _gelu_fwd(x):
    return _gelu(x), x


def _gelu_bwd(x, g):
    cdf = 0.5 * (1.0 + lax.erf(x * (2.0 ** -0.5)))
    pdf = jnp.exp(-0.5 * x * x) * (1.0 / math.sqrt(2.0 * math.pi))
    return (g * (cdf + x * pdf),)


_gelu.defvjp(_gelu_fwd, _gelu_bwd)


def _rms(x, w):
    return x * lax.rsqrt(jnp.mean(x * x, axis=-1, keepdims=True) + EPS) * w


def _shift_rows_raw(x, off, seg):
    n = x.shape[0]
    if off == 0:
        return x
    r = pltpu.roll(x, (-off) % n, 0)
    idx = lax.broadcasted_iota(jnp.int32, x.shape, 0)
    src = idx + off
    ok = (src >= 0) & (src < n)
    if seg:
        ok = ok & ((idx < seg) == (src < seg))
    return jnp.where(ok, r, 0.0)


@functools.partial(jax.custom_vjp, nondiff_argnums=(1, 2))
def _shift_rows(x, off, seg):
    return _shift_rows_raw(x, off, seg)


def _shift_rows_fwd(x, off, seg):
    return _shift_rows_raw(x, off, seg), None


def _shift_rows_bwd(off, seg, _, g):
    return (_shift_rows_raw(g, -off, seg),)


_shift_rows.defvjp(_shift_rows_fwd, _shift_rows_bwd)


@functools.partial(jax.custom_vjp, nondiff_argnums=(1,))
def _roll_lanes(x, shift):
    return pltpu.roll(x, shift % x.shape[1], 1)


def _roll_lanes_fwd(x, shift):
    return _roll_lanes(x, shift), None


def _roll_lanes_bwd(shift, _, g):
    return (pltpu.roll(g, (-shift) % g.shape[1], 1),)


_roll_lanes.defvjp(_roll_lanes_fwd, _roll_lanes_bwd)


def _row_of(w, k):
    sel = lax.broadcasted_iota(jnp.int32, (w.shape[0], 1), 0) == k
    return jnp.sum(jnp.where(sel, w, 0.0), axis=0, keepdims=True)


def _col_of(w, k):
    sel = lax.broadcasted_iota(jnp.int32, (1, w.shape[1]), 1) == k
    return jnp.sum(jnp.where(sel, w, 0.0), axis=1, keepdims=True)


def _dwconv(x, w, seg):
    k = w.shape[0]
    acc = None
    for t in range(k):
        term = _shift_rows(x, t - k // 2, seg) * _row_of(w, t)
        acc = term if acc is None else acc + term
    return acc


def _dwconv_back(x, dy, w, seg):
    k = w.shape[0]
    tap = lax.broadcasted_iota(jnp.int32, (k, 1), 0)
    dx, dw = None, jnp.zeros_like(w)
    for t in range(k):
        back = _shift_rows_raw(dy, k // 2 - t, seg)
        term = back * _row_of(w, t)
        dx = term if dx is None else dx + term
        dw = dw + jnp.where(tap == t, jnp.sum(x * back, axis=0, keepdims=True), 0.0)
    return dx, dw


def _dot(a, b, dims):
    return lax.dot_general(a.astype(MXU_DTYPE), b.astype(MXU_DTYPE), (dims, ((), ())),
                           preferred_element_type=F32)


def _dot_exact(a, b):
    return lax.dot_general(a, b, (((1,), (0,)), ((), ())), precision=HIGHEST,
                           preferred_element_type=F32)


def _mask_dot_raw(mask, x, mask_left):
    hi = x.astype(jnp.bfloat16)
    rest = x - hi.astype(F32)
    mid = rest.astype(jnp.bfloat16)
    low = (rest - mid.astype(F32)).astype(jnp.bfloat16)
    m = mask.astype(jnp.bfloat16)
    acc = None
    for piece in (hi, mid, low):
        term = (lax.dot_general(m, piece, (((1,), (0,)), ((), ())), preferred_element_type=F32) if mask_left
                else lax.dot_general(piece, m, (((1,), (0,)), ((), ())), preferred_element_type=F32))
        acc = term if acc is None else acc + term
    return acc


@functools.partial(jax.custom_vjp, nondiff_argnums=(3,))
def _mask_dot(mask, mask_t, x, mask_left):
    return _mask_dot_raw(mask, x, mask_left)


def _mask_dot_fwd(mask, mask_t, x, mask_left):
    return _mask_dot_raw(mask, x, mask_left), (mask, mask_t)


def _mask_dot_bwd(mask_left, res, g):
    mask, mask_t = res
    return jnp.zeros_like(mask), jnp.zeros_like(mask_t), _mask_dot_raw(mask_t, g, mask_left)


_mask_dot.defvjp(_mask_dot_fwd, _mask_dot_bwd)


MM_VMEM_BUDGET = 40 * 1024 * 1024
MM_STEP_BYTES = 1 << 20
MM_TILES = (2816, 2048, 1408, 1024, 512, 384, 256, 128)


def _mm_tiles(m, n, kdim, a_bytes, b_bytes, out_bytes):
    tk = kdim if kdim <= 2048 else _pick(kdim, (2048, 1664, 1536, 1408, 1024, 512, 256, 128))
    nk = kdim // tk
    best = None
    for tm in [c for c in MM_TILES if c <= m and m % c == 0] or [m]:
        for tn in [c for c in MM_TILES if c <= n and n % c == 0] or [n]:
            vmem = (2 * (tm * tk * a_bytes + tk * tn * b_bytes) + 2 * tm * tn * out_bytes
                    + tm * tn * 4 * (2 if nk > 1 else 1) + 2 * (tm * tk + tk * tn))
            if vmem > MM_VMEM_BUDGET:
                continue
            a_reads = 1 if nk == 1 else n // tn
            b_reads = 1 if (nk == 1 and n == tn) else m // tm
            cost = (m * kdim * a_bytes * a_reads + kdim * n * b_bytes * b_reads
                    + (m // tm) * (n // tn) * nk * MM_STEP_BYTES)
            if best is None or cost < best[0]:
                best = (cost, tm, tn)
    assert best is not None, (m, n, kdim)
    return best[1], best[2], tk


def _mm(a, b, *, ta=False, tb=False, out_dtype=F32, name):
    if ta:
        kdim, m = a.shape
    else:
        m, kdim = a.shape
    if tb:
        n, k2 = b.shape
    else:
        k2, n = b.shape
    assert kdim == k2, (a.shape, b.shape, ta, tb)
    tm, tn, tk = _mm_tiles(m, n, kdim, a.dtype.itemsize, b.dtype.itemsize, jnp.dtype(out_dtype).itemsize)
    nk = kdim // tk
    a_spec = pl.BlockSpec((tk, tm), lambda i, j, k: (k, i)) if ta else pl.BlockSpec((tm, tk), lambda i, j, k: (i, k))
    b_spec = pl.BlockSpec((tn, tk), lambda i, j, k: (j, k)) if tb else pl.BlockSpec((tk, tn), lambda i, j, k: (k, j))
    dims = ((0,) if ta else (1,), (1,) if tb else (0,))

    def body(a_ref, b_ref, o_ref, *scratch):
        if nk == 1:
            o_ref[...] = _dot(a_ref[...], b_ref[...], dims).astype(o_ref.dtype)
            return
        acc_ref, = scratch
        k = pl.program_id(2)

        @pl.when(k == 0)
        def _():
            acc_ref[...] = jnp.zeros_like(acc_ref)

        acc_ref[...] += _dot(a_ref[...], b_ref[...], dims)

        @pl.when(k == nk - 1)
        def _():
            o_ref[...] = acc_ref[...].astype(o_ref.dtype)

    return pl.pallas_call(
        body, name=name, grid=(m // tm, n // tn, nk),
        in_specs=[a_spec, b_spec], out_specs=pl.BlockSpec((tm, tn), lambda i, j, k: (i, j)),
        out_shape=jax.ShapeDtypeStruct((m, n), out_dtype),
        scratch_shapes=[pltpu.VMEM((tm, tn), F32)] if nk > 1 else [],
        compiler_params=_cparams(("parallel", "parallel", "arbitrary")),
    )(a, b)


def _mm_sum(pairs, *, tb=False, out_dtype=F32, name):
    m = pairs[0][0].shape[0]
    n = pairs[0][1].shape[0] if tb else pairs[0][1].shape[1]
    tm = _pick(m, (1024, 1408, 512, 384, 256, 128))
    tn = n if n == 1024 else _pick(n, (512, 1408, 384, 256, 128))
    specs, args = [], []
    for a, b in pairs:
        kdim = a.shape[1]
        specs.append(pl.BlockSpec((tm, kdim), lambda i, j: (i, 0)))
        specs.append(pl.BlockSpec((tn, kdim), lambda i, j: (j, 0)) if tb else pl.BlockSpec((kdim, tn), lambda i, j: (0, j)))
        args += [a, b]
    dims = ((1,), (1,) if tb else (0,))

    def body(*refs):
        acc = None
        for t in range(len(pairs)):
            term = _dot(refs[2 * t][...], refs[2 * t + 1][...], dims)
            acc = term if acc is None else acc + term
        refs[-1][...] = acc.astype(refs[-1].dtype)

    return pl.pallas_call(
        body, name=name, grid=(m // tm, n // tn), in_specs=specs,
        out_specs=pl.BlockSpec((tm, tn), lambda i, j: (i, j)), out_shape=jax.ShapeDtypeStruct((m, n), out_dtype),
        compiler_params=_cparams(("parallel", "parallel")),
    )(*args)


def _row_specs(toks, poss, vecs, bvecs, tl, nctx, nb):
    specs, args = [], []
    for arr, off, cw, ci in toks:
        cw = arr.shape[2] if cw is None else cw
        specs.append(pl.BlockSpec((1, tl, cw), lambda b, l, off=off, ci=ci: (b, l + off, ci)))
        args.append(arr)
    for arr in poss:
        specs.append(pl.BlockSpec((1, tl, arr.shape[2]), lambda b, l: (0, l, 0)))
        args.append(arr)
    for arr in vecs:
        specs.append(pl.BlockSpec(arr.shape, lambda b, l: (0, 0)))
        args.append(arr)
    for arr in bvecs:
        if nctx:
            specs.append(pl.BlockSpec((1, 1, arr.shape[2]), lambda b, l: (jnp.where(l < nctx, nb, b), 0, 0)))
        else:
            specs.append(pl.BlockSpec((1, 1, arr.shape[2]), lambda b, l: (b, 0, 0)))
        args.append(arr)
    return specs, args


def _row_fwd(fn, *, toks, poss=(), vecs=(), bvecs=(), outs, nb, nl, tl, nctx=0, name):
    nt, npos, nv, nbv = len(toks), len(poss), len(vecs), len(bvecs)
    specs, args = _row_specs(toks, poss, vecs, bvecs, tl, nctx, nb)

    def body(*refs):
        ins, os = refs[:len(specs)], refs[len(specs):]
        tv = [r[0].astype(F32) for r in ins[:nt]]
        pv = [r[0] for r in ins[nt:nt + npos]]
        vv = [r[...] for r in ins[nt + npos:nt + npos + nv]]
        bv = [r[0] for r in ins[nt + npos + nv:]]
        res = fn(*tv, *pv, *vv, *bv)
        for o, r in zip(os, res):
            o[0] = r.astype(o.dtype)

    return pl.pallas_call(
        body, name=name, grid=(nb, nl // tl), in_specs=specs,
        out_specs=[pl.BlockSpec((1, tl, c), lambda b, l: (b, l, 0)) for c, _ in outs],
        out_shape=[jax.ShapeDtypeStruct((nb, nl, c), dt) for c, dt in outs],
        compiler_params=_cparams(("parallel", "parallel")),
    )(*args)


def _row_bwd(fn, *, toks, poss=(), vecs=(), bvecs=(), cots, tok_grads, emit=(), nb, nl, tl, nctx=0, name,
             drop_blocks=0):
    nt, npos, nv, nbv = len(toks), len(poss), len(vecs), len(bvecs)
    specs, args = _row_specs(toks, poss, vecs, bvecs, tl, nctx, nb)
    n_in = len(specs)
    cot_slots = []
    for arr, off in cots:
        if arr is None:
            cot_slots.append(None)
            continue
        cot_slots.append((len(specs), off))
        specs.append(pl.BlockSpec((1, tl, arr.shape[2]), lambda b, l, off=off: (b, jnp.maximum(l + off, 0), 0)))
        args.append(arr)
    n_all_in = len(specs)

    out_specs, out_shapes = [], []
    tok_out = []
    for (arr, off, cw, ci), dt in zip(toks, tok_grads):
        if dt is None:
            tok_out.append(None)
            continue
        cw = arr.shape[2] if cw is None else cw
        tok_out.append(len(out_specs))
        out_specs.append(pl.BlockSpec((1, tl, cw), lambda b, l: (b, jnp.maximum(l - drop_blocks, 0), 0)))
        out_shapes.append(jax.ShapeDtypeStruct((nb, nl - drop_blocks * tl, cw), dt))
    vec_out = []
    for arr in vecs:
        vec_out.append(len(out_specs))
        out_specs.append(pl.BlockSpec(arr.shape, lambda b, l: (0, 0)))
        out_shapes.append(jax.ShapeDtypeStruct(arr.shape, F32))
    bv_out = []
    for arr in bvecs:
        c = arr.shape[2]
        lat = len(out_specs)
        out_specs.append(pl.BlockSpec((1, 1, c), lambda b, l: (b, 0, 0)))
        out_shapes.append(jax.ShapeDtypeStruct((nb, 1, c), F32))
        ctx = None
        if nctx:
            ctx = len(out_specs)
            out_specs.append(pl.BlockSpec((1, 1, c), lambda b, l: (0, 0, 0)))
            out_shapes.append(jax.ShapeDtypeStruct((1, 1, c), F32))
        bv_out.append((lat, ctx))
    emit_out = []
    emit_cols = {}
    for idx, c, dt in emit:
        emit_out.append((idx, len(out_specs)))
        out_specs.append(pl.BlockSpec((1, tl, c), lambda b, l: (b, l, 0)))
        out_shapes.append(jax.ShapeDtypeStruct((nb, nl, c), dt))

    def body(*refs):
        ins, os = refs[:n_all_in], refs[n_all_in:]
        b, l = pl.program_id(0), pl.program_id(1)
        tv = [r[0].astype(F32) for r in ins[:nt]]
        pv = [r[0] for r in ins[nt:nt + npos]]
        vv = [r[...] for r in ins[nt + npos:nt + npos + nv]]
        bv = [r[0] for r in ins[nt + npos + nv:n_in]]

        def f(*d):
            return tuple(fn(*d[:nt], *pv, *d[nt:]))

        res, vjp = jax.vjp(f, *tv, *vv, *bv)
        cts = []
        for r, slot in zip(res, cot_slots):
            if slot is None:
                cts.append(jnp.zeros_like(r))
            else:
                i, off = slot
                ct = ins[i][0].astype(F32)
                if off < 0:
                    ct = jnp.where(l + off >= 0, ct, 0.0)
                cts.append(ct)
        grads = vjp(tuple(cts))

        for g, slot in zip(grads[:nt], tok_out):
            if slot is not None:
                os[slot][0] = g.astype(os[slot].dtype)

        @pl.when((b == 0) & (l == 0))
        def _():
            for slot in vec_out:
                os[slot][...] = jnp.zeros_like(os[slot])
            for _, ctx in bv_out:
                if ctx is not None:
                    os[ctx][...] = jnp.zeros_like(os[ctx])

        @pl.when(l == 0)
        def _():
            for lat, _ in bv_out:
                os[lat][...] = jnp.zeros_like(os[lat])

        for g, slot in zip(grads[nt:nt + nv], vec_out):
            os[slot][...] += g
        for g, (lat, ctx) in zip(grads[nt + nv:], bv_out):
            if ctx is None:
                os[lat][0] += g
            else:
                is_ctx = l < nctx
                os[lat][0] += jnp.where(is_ctx, 0.0, g)
                os[ctx][0] += jnp.where(is_ctx, g, 0.0)
        for idx, slot in emit_out:
            os[slot][0] = res[idx].astype(os[slot].dtype)

    out = pl.pallas_call(
        body, name=name, grid=(nb, nl // tl), in_specs=specs, out_specs=out_specs, out_shape=out_shapes,
        compiler_params=_cparams(("arbitrary", "arbitrary")),
    )(*args)
    tg = [None if s is None else out[s] for s in tok_out]
    vg = [out[s] for s in vec_out]
    bg = [(out[lat], None if ctx is None else out[ctx]) for lat, ctx in bv_out]
    em = [out[s] for _, s in emit_out]
    return tg, vg, bg, em


def _seq_specs(toks, vecs, nl, cb):
    specs, args = [], []
    for arr, off, mult in toks:
        specs.append(pl.BlockSpec((1, nl, cb * mult), lambda j, b, off=off: (b, 0, j + off)))
        args.append(arr)
    for arr, off in vecs:
        specs.append(pl.BlockSpec((arr.shape[0], cb), lambda j, b, off=off: (0, j + off)))
        args.append(arr)
    return specs, args


def _seq_fwd(fn, *, toks, vecs, outs, nb, nl, nc, cb, name):
    nt = len(toks)
    specs, args = _seq_specs(toks, vecs, nl, cb)

    def body(*refs):
        ins, os = refs[:len(specs)], refs[len(specs):]
        tv = [r[0].astype(F32) for r in ins[:nt]]
        vv = [r[...] for r in ins[nt:]]
        for o, r in zip(os, fn(*tv, *vv)):
            o[0] = r.astype(o.dtype)

    return pl.pallas_call(
        body, name=name, grid=(nc // cb, nb), in_specs=specs,
        out_specs=[pl.BlockSpec((1, nl, cb), lambda j, b: (b, 0, j)) for _ in outs],
        out_shape=[jax.ShapeDtypeStruct((nb, nl, nc), dt) for dt in outs],
        compiler_params=_cparams(("parallel", "parallel")),
    )(*args)


def _seq_bwd(fn, *, toks, vecs, cots, tok_grads, nb, nl, nc, cb, name, bwd_fn=None):
    nt, nv = len(toks), len(vecs)
    specs, args = _seq_specs(toks, vecs, nl, cb)
    n_in = len(specs)
    cot_counts = [len(group) for group in cots]
    for group in cots:
        for arr in group:
            specs.append(pl.BlockSpec((1, nl, cb), lambda j, b: (b, 0, j)))
            args.append(arr)
    out_specs, out_shapes = [], []
    for (_, _, mult), dt in zip(toks, tok_grads):
        out_specs.append(pl.BlockSpec((1, nl, cb * mult), lambda j, b: (b, 0, j)))
        out_shapes.append(jax.ShapeDtypeStruct((nb, nl, nc * mult), dt))
    for arr, _ in vecs:
        out_specs.append(pl.BlockSpec((arr.shape[0], cb), lambda j, b: (0, j)))
        out_shapes.append(jax.ShapeDtypeStruct((arr.shape[0], nc), F32))

    def body(*refs):
        ins, os = refs[:len(specs)], refs[len(specs):]
        b = pl.program_id(1)
        tv = [r[0].astype(F32) for r in ins[:nt]]
        vv = [r[...] for r in ins[nt:n_in]]
        cts, o = [], n_in
        for cnt in cot_counts:
            ct = ins[o][0].astype(F32)
            for r in ins[o + 1:o + cnt]:
                ct = ct + r[0].astype(F32)
            cts.append(ct)
            o += cnt
        if bwd_fn is None:
            _, vjp = jax.vjp(lambda *d: tuple(fn(*d)), *tv, *vv)
            grads = vjp(tuple(cts))
        else:
            grads = bwd_fn(*tv, *vv, *cts)
        for g, o in zip(grads[:nt], os[:nt]):
            o[0] = g.astype(o.dtype)

        @pl.when(b == 0)
        def _():
            for o in os[nt:]:
                o[...] = jnp.zeros_like(o)

        for g, o in zip(grads[nt:], os[nt:]):
            o[...] += g

    out = pl.pallas_call(
        body, name=name, grid=(nc // cb, nb), in_specs=specs, out_specs=out_specs, out_shape=out_shapes,
        compiler_params=_cparams(("parallel", "arbitrary")),
    )(*args)
    return out[:nt], out[nt:]


EXP2_SCALE = ATTN_SCALE * math.log2(math.e)


HEAD_TILE = 128
N_HEAD_PAIRS = N_ATTN_HEADS // 2


def _head_lanes():
    lane = lax.broadcasted_iota(jnp.int32, (1, HEAD_TILE), 1)
    return lane < QK_NOPE_DIM, (lane >= QK_NOPE_DIM) & (lane < QK_DIM)


def _attn_specs(tq, lk):
    q = pl.BlockSpec((1, tq, 2 * HEAD_TILE), lambda b, pr, j: (b, j, pr))
    kv = pl.BlockSpec((1, lk, 2 * HEAD_TILE), lambda b, pr, j: (b, 0, pr))
    kr = pl.BlockSpec((1, lk, HEAD_TILE), lambda b, pr, j: (b, 0, 0))
    o = pl.BlockSpec((1, tq, HEAD_TILE), lambda b, pr, j: (b, j, pr))
    lse = pl.BlockSpec((1, 2, tq, 1), lambda b, pr, j: (b, pr, j, 0))
    return q, kv, kr, o, lse


def _grid_marks(nb, nj):
    b, pr, j = pl.program_id(0), pl.program_id(1), pl.program_id(2)
    first = (b == 0) & (pr == 0) & (j == 0)
    middle = (b == nb // 2) & (pr == 0) & (j == 0)
    last = (b == nb - 1) & (pr == N_HEAD_PAIRS - 1) & (j == nj - 1)
    return first, middle, last


def _attn_fwd(q, kv, kr, late_shard, *, tq, name):
    nb, s, _ = q.shape
    lk = kv.shape[1]
    nj = s // tq
    qs, kvs, krs, os_, lses = _attn_specs(tq, lk)

    def body(q_ref, kv_ref, kr_ref, shard_ref, o_ref, lse_ref, gathered_ref, send_sems, recv_sems):
        start, relay, finish = _gather_stage(shard_ref, gathered_ref, send_sems, recv_sems)
        first, middle, last = _grid_marks(nb, nj)
        pl.when(first)(start)
        pl.when(middle)(relay)
        low, _ = _head_lanes()
        outs = []
        for e in range(2):
            tile = pl.ds(HEAD_TILE * e, HEAD_TILE)
            kv_e = kv_ref[0, :, tile]
            keys = jnp.where(low, kv_e, kr_ref[0])
            sc = _dot(q_ref[0, :, tile], keys, ((1,), (1,)))
            m = jnp.max(sc, axis=-1, keepdims=True)
            p = jnp.exp2((sc - m) * EXP2_SCALE)
            denom = jnp.sum(p, axis=-1, keepdims=True)
            outs.append(_dot(p, kv_e, ((1,), (0,))) / denom)
            lse_ref[0, e] = m * EXP2_SCALE + jnp.log2(denom)
        o_ref[0] = jnp.where(low, pltpu.roll(outs[0], V_HEAD_DIM, 1), outs[1])
        pl.when(last)(finish)

    return pl.pallas_call(
        body, name=name, grid=(nb, N_HEAD_PAIRS, nj), in_specs=[qs, kvs, krs, ANY], out_specs=[os_, lses, ANY],
        out_shape=[jax.ShapeDtypeStruct((nb, s, ATTN_WIDTH), F32), jax.ShapeDtypeStruct((nb, N_ATTN_HEADS, s, 1), F32),
                   jax.ShapeDtypeStruct((N_CHIPS,) + late_shard.shape, late_shard.dtype)],
        scratch_shapes=GATHER_SEMS,
        compiler_params=_cparams(("arbitrary", "arbitrary", "arbitrary")),
    )(q, kv, kr, late_shard)


def _attn_bwd(q, kv, kr, o, lse, do, early_sums, *, tq, name):
    nb, s, _ = q.shape
    lk = kv.shape[1]
    nj = s // tq
    qs, kvs, krs, os_, lses = _attn_specs(tq, lk)

    def body(q_ref, kv_ref, kr_ref, o_ref, lse_ref, do_ref, sums_ref, dq_ref, dkv_ref, dkr_ref, scattered_ref,
             send_sems, recv_sems):
        start, finish = _scatter_stage(sums_ref, scattered_ref, send_sems, recv_sems)
        first, _, last = _grid_marks(nb, nj)
        pl.when(first)(start)
        pr, j = pl.program_id(1), pl.program_id(2)
        low, rope = _head_lanes()
        do_pair = do_ref[0]
        prod = do_pair * o_ref[0]

        @pl.when(j == 0)
        def _():
            dkv_ref[...] = jnp.zeros_like(dkv_ref)

        @pl.when((pr == 0) & (j == 0))
        def _():
            dkr_ref[...] = jnp.zeros_like(dkr_ref)

        dkr = None
        for e in range(2):
            tile = pl.ds(HEAD_TILE * e, HEAD_TILE)
            delta = jnp.sum(jnp.where(low if e == 0 else ~low, prod, 0.0), axis=-1, keepdims=True)
            do_e = jnp.where(low, 0.0, do_pair if e == 1 else pltpu.roll(do_pair, V_HEAD_DIM, 1))
            kv_e, q_e = kv_ref[0, :, tile], q_ref[0, :, tile]
            keys = jnp.where(low, kv_e, kr_ref[0])
            sc = _dot(q_e, keys, ((1,), (1,)))
            p = jnp.exp2(sc * EXP2_SCALE - lse_ref[0, e])
            dp = _dot(do_e, kv_e, ((1,), (1,)))
            ds = (p * (dp - delta)).astype(MXU_DTYPE)
            dq_ref[0, :, tile] = _dot(ds, keys, ((1,), (0,))) * ATTN_SCALE
            dkeys = _dot(ds, q_e, ((0,), (0,)))
            dv = _dot(p, do_e, ((0,), (0,)))
            dkv_ref[0, :, tile] += jnp.where(low, dkeys, dv)
            part = jnp.where(rope, dkeys, 0.0)
            dkr = part if dkr is None else dkr + part
        dkr_ref[0] += dkr

        @pl.when(j == nj - 1)
        def _():
            for e in range(2):
                tile = pl.ds(HEAD_TILE * e, HEAD_TILE)
                dkv_ref[0, :, tile] = dkv_ref[0, :, tile] * jnp.where(low, ATTN_SCALE, 1.0)

        @pl.when((pr == N_HEAD_PAIRS - 1) & (j == nj - 1))
        def _():
            dkr_ref[0] = dkr_ref[0] * ATTN_SCALE

        pl.when(last)(finish)

    return pl.pallas_call(
        body, name=name, grid=(nb, N_HEAD_PAIRS, nj), in_specs=[qs, kvs, krs, os_, lses, os_, ANY],
        out_specs=[qs, kvs, krs, ANY],
        out_shape=[jax.ShapeDtypeStruct(q.shape, F32), jax.ShapeDtypeStruct(kv.shape, F32),
                   jax.ShapeDtypeStruct(kr.shape, F32), jax.ShapeDtypeStruct(early_sums.shape, early_sums.dtype)],
        scratch_shapes=SCATTER_SEMS,
        compiler_params=_cparams(("arbitrary", "arbitrary", "arbitrary")),
    )(q, kv, kr, o, lse, do, early_sums)


N_PAIRS = HEADS_PER_GROUP // 2
PAIR_W = 2 * SSD_HEAD_DIM


def _ssd_chunk(states, xs, dtc, dtr, bm, cm, ac, ar, *, reverse):
    q = dtc.shape[0]
    assert q == PAIR_W == dtr.shape[1]
    row = lax.broadcasted_iota(jnp.int32, (q, q), 0)
    col = lax.broadcasted_iota(jnp.int32, (q, q), 1)
    if reverse:
        tri_c, tri_r, mask = col < row, row < col, col >= row
    else:
        tri_c, tri_r, mask = col <= row, row <= col, col <= row
    a_col, a_row = dtc * ac, dtr * ar
    tri_c, tri_r = tri_c.astype(F32), tri_r.astype(F32)
    cum_c = _mask_dot(tri_c, tri_r, a_col, True)
    cum_r = _mask_dot(tri_r, tri_c, a_row, False)
    tot_c = jnp.sum(a_col, axis=0, keepdims=True)
    tot_r = jnp.sum(a_row, axis=1, keepdims=True)
    cb = _dot(cm, bm, ((1,), (1,)))
    first = lax.broadcasted_iota(jnp.int32, (1, PAIR_W), 1) < SSD_HEAD_DIM
    first_rows = lax.broadcasted_iota(jnp.int32, (PAIR_W, 1), 0) < SSD_HEAD_DIM
    heads, pairs = range(HEADS_PER_GROUP), range(N_PAIRS)
    tile = [slice(PAIR_W * pr, PAIR_W * (pr + 1)) for pr in pairs]
    cum_p = [cum_c[:, tile[pr]] for pr in pairs]
    swapped = [_roll_lanes(cum_p[pr], SSD_HEAD_DIM) for pr in pairs]
    cc = [jnp.where(first, cum_p[e // 2], swapped[e // 2]) if e % 2 == 0
          else jnp.where(first, swapped[e // 2], cum_p[e // 2]) for e in heads]
    cr = [_row_of(cum_r, e) for e in heads]
    if reverse:
        within = [jnp.exp(jnp.where(mask, cr[e] - cc[e], -jnp.inf)) for e in heads]
        into = [jnp.exp(tot_c[:, tile[pr]] - cum_p[pr]) for pr in pairs]
        to_end = [jnp.exp(cum_p[pr]) for pr in pairs]
    else:
        within = [jnp.exp(jnp.where(mask, cc[e] - cr[e], -jnp.inf)) for e in heads]
        into = [jnp.exp(cum_p[pr]) for pr in pairs]
        to_end = [jnp.exp(tot_c[:, tile[pr]] - cum_p[pr]) for pr in pairs]
    decay = [cb * within[e] for e in heads]
    carry = [jnp.exp(_row_of(tot_r, e)) for e in heads]
    xd = [xs[pr] * dtc[:, tile[pr]] for pr in pairs]
    y_even = [_dot(decay[2 * pr], jnp.where(first, xd[pr], 0.0), ((1,), (0,))) for pr in pairs]
    y_odd = [_dot(decay[2 * pr + 1], jnp.where(first, 0.0, xd[pr]), ((1,), (0,))) for pr in pairs]
    y_off = [_dot(cm, states[pr], ((1,), (1,))) for pr in pairs]
    grow = [_dot(xd[pr] * to_end[pr], bm, ((0,), (0,))) for pr in pairs]
    ys = [y_even[pr] + y_odd[pr] + y_off[pr] * into[pr] for pr in pairs]
    new_states = [states[pr] * jnp.where(first_rows, carry[2 * pr], carry[2 * pr + 1]) + grow[pr] for pr in pairs]
    return tuple(ys) + tuple(new_states)


def _chunk_of_step(t, ncc, nch, reverse):
    if not reverse:
        return t
    return jnp.where(t < ncc, ncc - 1 - t, nch - 1 - (t - ncc))


X_COLS = D_INNER // SSD_GROUPS
GROUP_COLS = X_COLS + 2 * SSD_STATE


def _scan_in_specs(nch, ncc, reverse, back):
    q, e = SSD_CHUNK, HEADS_PER_GROUP

    def ch(t):
        return _chunk_of_step((nch - 1 - t) if back else t, ncc, nch, reverse)

    return ch, [
        pl.BlockSpec((1, q, GROUP_COLS), lambda b, g, t: (b, ch(t), g)),
        pl.BlockSpec((1, q, X_COLS), lambda b, g, t: (b, ch(t), g)),
        pl.BlockSpec((1, 1, e, q), lambda b, g, t: (b, g, 0, ch(t))),
        pl.BlockSpec((1, 1, X_COLS), lambda b, g, t: (g, 0, 0)),
        pl.BlockSpec((1, e, 1), lambda b, g, t: (g, 0, 0)),
        pl.BlockSpec((1, 1, X_COLS), lambda b, g, t: (g, 0, 0)),
    ]


def _scan_chunk_fn(reverse, skip):
    def f(states, xs, dtc, dtr, bm, cm, ac, ar, d):
        res = _ssd_chunk(states, xs, dtc, dtr, bm, cm, ac, ar, reverse=reverse)
        if not skip:
            return res
        ys = tuple(res[i] + d[:, PAIR_W * i:PAIR_W * (i + 1)] * xs[i] for i in range(N_PAIRS))
        return ys + tuple(res[N_PAIRS:])

    return f


def _scan_operands(x_ref, dtc_ref, dtr_ref, ac_ref, ar_ref, d_ref):
    xs = [x_ref[0, :, pl.ds(PAIR_W * i, PAIR_W)] for i in range(N_PAIRS)]
    bm = x_ref[0, :, pl.ds(X_COLS, SSD_STATE)]
    cm = x_ref[0, :, pl.ds(X_COLS + SSD_STATE, SSD_STATE)]
    return xs, dtc_ref[0], dtr_ref[0, 0], bm, cm, ac_ref[0], ar_ref[0], d_ref[0]


N_IN = 6


def _scan_fwd(dirs, *, ncc, name):
    nb, lt, _ = dirs[0][0].shape
    q, n = SSD_CHUNK, SSD_STATE
    nch = lt // q
    in_specs, out_specs, out_shapes, fs = [], [], [], []
    for dr in range(2):
        ch, specs = _scan_in_specs(nch, ncc, bool(dr), False)
        in_specs += specs
        fs.append(_scan_chunk_fn(bool(dr), dr == 0))
        out_specs += [pl.BlockSpec((1, q, X_COLS), lambda b, g, t, ch=ch: (b, ch(t), g)),
                      pl.BlockSpec((1, 1, 1, N_PAIRS, PAIR_W, n), lambda b, g, t: (b, g, t, 0, 0, 0))]
        out_shapes += [jax.ShapeDtypeStruct((nb, lt, D_INNER), F32),
                       jax.ShapeDtypeStruct((nb, SSD_GROUPS, nch, N_PAIRS, PAIR_W, n), F32)]

    def body(*refs):
        ins, outs, sts = refs[:2 * N_IN], refs[2 * N_IN:2 * N_IN + 4], refs[2 * N_IN + 4:]
        t = pl.program_id(2)

        @pl.when(t == 0)
        def _():
            for st_ref in sts:
                st_ref[...] = jnp.zeros_like(st_ref)

        entering = [[sts[dr][i] for i in range(N_PAIRS)] for dr in range(2)]
        results = [fs[dr](entering[dr], *_scan_operands(*ins[N_IN * dr:N_IN * (dr + 1)])) for dr in range(2)]
        for dr in range(2):
            (y_ref, ent_ref), st_ref = outs[2 * dr:2 * dr + 2], sts[dr]
            for i in range(N_PAIRS):
                ent_ref[0, 0, 0, i] = entering[dr][i]
                y_ref[0, :, pl.ds(PAIR_W * i, PAIR_W)] = results[dr][i]
                st_ref[i] = results[dr][N_PAIRS + i]

    out = pl.pallas_call(
        body, name=name, grid=(nb, SSD_GROUPS, nch), in_specs=in_specs, out_specs=out_specs, out_shape=out_shapes,
        scratch_shapes=[pltpu.VMEM((N_PAIRS, PAIR_W, n), F32)] * 2,
        compiler_params=_cparams(("parallel", "parallel", "arbitrary")),
    )(*dirs[0], *dirs[1])
    return out[:2], out[2:]


N_SCAN_GRADS = 6


def _scan_bwd(dirs, entering, dy, *, ncc, name):
    nb, lt, _ = dirs[0][0].shape
    q, n, e = SSD_CHUNK, SSD_STATE, HEADS_PER_GROUP
    nch = lt // q
    in_specs, out_specs, out_shapes, fs, args = [], [], [], [], []
    for dr in range(2):
        ch, specs = _scan_in_specs(nch, ncc, bool(dr), True)
        in_specs += specs + [
            pl.BlockSpec((1, 1, 1, N_PAIRS, PAIR_W, n), lambda b, g, t: (b, g, nch - 1 - t, 0, 0, 0)),
            pl.BlockSpec((1, q, X_COLS), lambda b, g, t, ch=ch: (b, ch(t), g))]
        args += list(dirs[dr]) + [entering[dr], dy]
        fs.append(_scan_chunk_fn(bool(dr), dr == 0))
        out_specs += [pl.BlockSpec((1, q, GROUP_COLS), lambda b, g, t, ch=ch: (b, ch(t), g)),
                      pl.BlockSpec((1, q, X_COLS), lambda b, g, t, ch=ch: (b, ch(t), g)),
                      pl.BlockSpec((1, 1, e, q), lambda b, g, t, ch=ch: (b, g, 0, ch(t))),
                      pl.BlockSpec((1, 1, 1, X_COLS), lambda b, g, t: (b, g, 0, 0)),
                      pl.BlockSpec((1, 1, e, 1), lambda b, g, t: (b, g, 0, 0)),
                      pl.BlockSpec((1, 1, 1, X_COLS), lambda b, g, t: (b, g, 0, 0))]
        out_shapes += [jax.ShapeDtypeStruct((nb, lt, SSD_GROUPS * GROUP_COLS), F32),
                       jax.ShapeDtypeStruct((nb, lt, D_INNER), F32), jax.ShapeDtypeStruct((nb, SSD_GROUPS, e, lt), F32),
                       jax.ShapeDtypeStruct((nb, SSD_GROUPS, 1, X_COLS), F32), jax.ShapeDtypeStruct((nb, SSD_GROUPS, e, 1), F32),
                       jax.ShapeDtypeStruct((nb, SSD_GROUPS, 1, X_COLS), F32)]
    n_in = N_IN + 2

    def body(*refs):
        ins = refs[:2 * n_in]
        outs = refs[2 * n_in:2 * n_in + 2 * N_SCAN_GRADS]
        dss = refs[2 * n_in + 2 * N_SCAN_GRADS:]
        t = pl.program_id(2)

        for dr in range(2):
            mine = ins[n_in * dr:n_in * (dr + 1)]
            ent_ref, dy_ref = mine[N_IN], mine[N_IN + 1]
            dx_ref, ddtc_ref, ddtr_ref, dac_ref, dar_ref, dd_ref = outs[N_SCAN_GRADS * dr:N_SCAN_GRADS * (dr + 1)]
            ds_ref = dss[dr]

            @pl.when(t == 0)
            def _():
                for ref in (ds_ref, dac_ref, dar_ref, dd_ref):
                    ref[...] = jnp.zeros_like(ref)

            states = [ent_ref[0, 0, 0, i] for i in range(N_PAIRS)]
            _, vjp = jax.vjp(fs[dr], states, *_scan_operands(*mine[:N_IN]))
            dys = [dy_ref[0, :, pl.ds(PAIR_W * i, PAIR_W)] for i in range(N_PAIRS)]
            gs, gx, gdtc, gdtr, gb, gc, gac, gar, gd = vjp(tuple(dys) + tuple(ds_ref[i] for i in range(N_PAIRS)))
            o = 0
            for part in list(gx) + [gb, gc]:
                dx_ref[0, :, pl.ds(o, part.shape[1])] = part
                o += part.shape[1]
            for i in range(N_PAIRS):
                ds_ref[i] = gs[i]
            ddtc_ref[0] = gdtc
            ddtr_ref[0, 0] = gdtr
            dac_ref[0, 0] += gac
            dar_ref[0, 0] += gar
            dd_ref[0, 0] += gd

    out = pl.pallas_call(
        body, name=name, grid=(nb, SSD_GROUPS, nch), in_specs=in_specs, out_specs=out_specs, out_shape=out_shapes,
        scratch_shapes=[pltpu.VMEM((N_PAIRS, PAIR_W, n), F32)] * 2,
        compiler_params=_cparams(("parallel", "parallel", "arbitrary")),
    )(*args)
    return out[:N_SCAN_GRADS], out[N_SCAN_GRADS:]


def _adamw(w, g, m, v, *, name):
    r, c = w.shape
    tr = _pick(r, (256, 176, 128, 96, 64, 8))
    c1 = 1.0 / (1.0 - ADAM_B1 ** ADAM_STEP)
    c2 = 1.0 / (1.0 - ADAM_B2 ** ADAM_STEP)

    def body(w_ref, g_ref, m_ref, v_ref, d_ref, nm_ref, nv_ref):
        gv = g_ref[...]
        nm = ADAM_B1 * m_ref[...] + (1.0 - ADAM_B1) * gv
        nv = ADAM_B2 * v_ref[...] + (1.0 - ADAM_B2) * (gv * gv)
        d_ref[...] = -ADAM_LR * ((nm * c1) / (jnp.sqrt(nv * c2) + ADAM_EPS) + ADAM_WD * w_ref[...])
        nm_ref[...] = nm
        nv_ref[...] = nv

    spec = pl.BlockSpec((tr, c), lambda i: (i, 0))
    return pl.pallas_call(
        body, name=name, grid=(r // tr,), in_specs=[spec] * 4, out_specs=[spec] * 3,
        out_shape=[jax.ShapeDtypeStruct((r, c), F32)] * 3, compiler_params=_cparams(("parallel",)),
    )(w, g, m, v)


def _sum_rows_tile(r):
    return r if r <= 1024 else _pick(r, (656, 512, 256, 128, 64, 32, 16))


def _sum_slots(x, *, out_dtype, name):
    n, r, c = x.shape
    tr = _sum_rows_tile(r)

    def body(x_ref, o_ref):
        acc = x_ref[0].astype(F32)
        for k in range(1, n):
            acc = acc + x_ref[k].astype(F32)
        o_ref[...] = acc.astype(o_ref.dtype)

    return pl.pallas_call(
        body, name=name, grid=(r // tr,), in_specs=[pl.BlockSpec((n, tr, c), lambda i: (0, i, 0))],
        out_specs=pl.BlockSpec((tr, c), lambda i: (i, 0)), out_shape=jax.ShapeDtypeStruct((r, c), out_dtype),
        compiler_params=_cparams(("parallel",)),
    )(x)


def _sum_list(xs, *, out_dtype, name):
    r, c = xs[0].shape
    tr = _sum_rows_tile(r)

    def body(*refs):
        acc = refs[0][...].astype(F32)
        for ref in refs[1:-1]:
            acc = acc + ref[...].astype(F32)
        refs[-1][...] = acc.astype(refs[-1].dtype)

    spec = pl.BlockSpec((tr, c), lambda i: (i, 0))
    return pl.pallas_call(
        body, name=name, grid=(r // tr,), in_specs=[spec] * len(xs), out_specs=spec,
        out_shape=jax.ShapeDtypeStruct((r, c), out_dtype), compiler_params=_cparams(("parallel",)),
    )(*xs)


ANY = pl.BlockSpec(memory_space=pl.ANY)


def _place():
    return lax.axis_index("x"), lax.axis_index("y"), lax.axis_index("c")


def _allgather_small(v, *, name):
    r, c = v.shape

    def body(v_ref, out_ref, send_sems, recv_sems, local_sem):
        x, y, cc = _place()
        me = 4 * x + 2 * y + cc
        mine = pltpu.make_async_copy(v_ref, out_ref.at[me], local_sem)
        mine.start()
        copies = []
        for k in range(1, N_DEV):
            fx, fy, fc = (k >> 2) & 1, (k >> 1) & 1, k & 1
            peer = (1 - x if fx else x, 1 - y if fy else y, 1 - cc if fc else cc)
            copies.append(pltpu.make_async_remote_copy(
                src_ref=v_ref, dst_ref=out_ref.at[me], send_sem=send_sems.at[k - 1], recv_sem=recv_sems.at[k - 1],
                device_id=peer, device_id_type=MESH))
        for cp in copies:
            cp.start()
        for cp in copies:
            cp.wait()
        mine.wait()

    return pl.pallas_call(
        body, name=name, in_specs=[ANY], out_specs=ANY, out_shape=jax.ShapeDtypeStruct((N_DEV, r, c), v.dtype),
        scratch_shapes=[pltpu.SemaphoreType.DMA((N_DEV - 1,)), pltpu.SemaphoreType.DMA((N_DEV - 1,)),
                        pltpu.SemaphoreType.DMA],
    )(v)


def _other_chips(x, y):
    return [(1 - x, y), (x, 1 - y), (1 - x, 1 - y)]


GATHER_SEMS = [pltpu.SemaphoreType.DMA((6,)), pltpu.SemaphoreType.DMA((6,))]
SCATTER_SEMS = [pltpu.SemaphoreType.DMA((3,)), pltpu.SemaphoreType.DMA((3,))]


def _gather_stage(v_ref, out_ref, send_sems, recv_sems):
    half = v_ref.shape[0] // 2
    x, y, cc = _place()
    sibling = (x, y, 1 - cc)
    chips = _other_chips(x, y)

    def rows(px, py, pc):
        return out_ref.at[2 * px + py, pl.ds(pc * half, half), :]

    def copy(k, block, to, src=None):
        return pltpu.make_async_remote_copy(
            src_ref=rows(*block) if src is None else src, dst_ref=rows(*block),
            send_sem=send_sems.at[k], recv_sem=recv_sems.at[k], device_id=to, device_id_type=MESH)

    my_half = v_ref.at[pl.ds(cc * half, half), :]
    first = [copy(j, (x, y, cc), (*chip, cc), src=my_half) for j, chip in enumerate(chips)]
    passed = [copy(3 + j, (*chip, cc), sibling) for j, chip in enumerate(chips)]

    def start():
        for cp in first:
            cp.start()

    def relay():
        for j, chip in enumerate(chips):
            copy(j, (*chip, cc), (x, y, cc)).wait_recv()
            passed[j].start()

    def finish():
        for j, chip in enumerate(chips):
            copy(3 + j, (*chip, 1 - cc), (x, y, cc)).wait_recv()
        for cp in first + passed:
            cp.wait_send()

    return start, relay, finish


def _gather_shards(mine, *, name):
    r, c = mine.shape

    def body(v_ref, out_ref, send_sems, recv_sems):
        for phase in _gather_stage(v_ref, out_ref, send_sems, recv_sems):
            phase()

    return pl.pallas_call(
        body, name=name, in_specs=[ANY], out_specs=ANY, out_shape=jax.ShapeDtypeStruct((N_CHIPS, r, c), mine.dtype),
        scratch_shapes=GATHER_SEMS,
    )(mine)


def _swap_halves(g, *, name):
    n, _, r, c = g.shape

    def body(g_ref, got_ref, send_sems, recv_sems):
        x, y, cc = _place()
        sibling = (x, y, 1 - cc)
        rems = []
        for j in range(n):
            rems.append(pltpu.make_async_remote_copy(
                src_ref=g_ref.at[j, 1 - cc], dst_ref=got_ref.at[j], send_sem=send_sems.at[j],
                recv_sem=recv_sems.at[j], device_id=sibling, device_id_type=MESH))
        for cp in rems:
            cp.start()
        for cp in rems:
            cp.wait()

    return pl.pallas_call(
        body, name=name, in_specs=[ANY], out_specs=ANY, out_shape=jax.ShapeDtypeStruct((n, r, c), g.dtype),
        scratch_shapes=[pltpu.SemaphoreType.DMA((n,)), pltpu.SemaphoreType.DMA((n,))],
    )(g)


def _scatter_stage(s_ref, out_ref, send_sems, recv_sems):
    x, y, cc = _place()
    me = 2 * x + y
    copies = [pltpu.make_async_remote_copy(
        src_ref=s_ref.at[2 * px + py], dst_ref=out_ref.at[me], send_sem=send_sems.at[j], recv_sem=recv_sems.at[j],
        device_id=(px, py, cc), device_id_type=MESH) for j, (px, py) in enumerate(_other_chips(x, y))]

    def start():
        for cp in copies:
            cp.start()

    def finish():
        for cp in copies:
            cp.wait()

    return start, finish


def _scatter_to_chips(s, *, name):
    def body(s_ref, out_ref, send_sems, recv_sems):
        for phase in _scatter_stage(s_ref, out_ref, send_sems, recv_sems):
            phase()

    return pl.pallas_call(
        body, name=name, in_specs=[ANY], out_specs=ANY, out_shape=jax.ShapeDtypeStruct(s.shape, s.dtype),
        scratch_shapes=SCATTER_SEMS,
    )(s)


def _join_halves(f, *, name):
    r, c = f.shape

    def body(f_ref, out_ref, send_sem, recv_sem):
        x, y, cc = _place()
        cp = pltpu.make_async_remote_copy(src_ref=f_ref, dst_ref=out_ref.at[cc], send_sem=send_sem, recv_sem=recv_sem,
                                          device_id=(x, y, 1 - cc), device_id_type=MESH)
        cp.start()
        cp.wait()

    return pl.pallas_call(
        body, name=name, in_specs=[ANY], out_specs=ANY, out_shape=jax.ShapeDtypeStruct((2, r, c), f.dtype),
        scratch_shapes=[pltpu.SemaphoreType.DMA, pltpu.SemaphoreType.DMA],
    )(f)


def _pack_rows(parts, width=PACK_COLS):
    return jnp.concatenate([p.reshape(-1, width) for p in parts], axis=0)


def _pack_small(parts, rows):
    flat = jnp.concatenate([p.reshape(-1).astype(F32) for p in parts])
    return jnp.pad(flat, (0, rows * LANES - flat.shape[0])).reshape(rows, LANES)


def _unpack_small(packed, shapes):
    flat = packed.reshape(-1)
    out, o = [], 0
    for shp in shapes:
        n = int(np.prod(shp))
        out.append(flat[o:o + n].reshape(shp))
        o += n
    return out


def _perm_in_cols(w):
    a, b = Q_LORA_RANK + KV_LORA_RANK, Q_LORA_RANK + KV_LORA_RANK + QK_ROPE_DIM
    c = IN_WIDTH - 2 * N_SSD_HEADS
    return jnp.concatenate([w[:, :a], w[:, b:c], w[:, a:b], w[:, c:]], axis=1)


def _unperm_in_cols(w):
    a = Q_LORA_RANK + KV_LORA_RANK
    zx = D_INNER + XBC_WIDTH
    return jnp.concatenate([w[:, :a], w[:, a + zx:a + zx + QK_ROPE_DIM], w[:, a:a + zx], w[:, a + zx + QK_ROPE_DIM:]],
                           axis=1)


def _group_xbc(a):
    n = SSD_STATE
    parts = []
    for g in range(SSD_GROUPS):
        parts += [a[..., g * X_COLS:(g + 1) * X_COLS], a[..., D_INNER + g * n:D_INNER + (g + 1) * n],
                  a[..., D_INNER + GN + g * n:D_INNER + GN + (g + 1) * n]]
    return jnp.concatenate(parts, axis=-1)


def _ungroup_xbc(a):
    n = SSD_STATE
    xs = [a[..., g * GROUP_COLS:g * GROUP_COLS + X_COLS] for g in range(SSD_GROUPS)]
    bs = [a[..., g * GROUP_COLS + X_COLS:g * GROUP_COLS + X_COLS + n] for g in range(SSD_GROUPS)]
    cs = [a[..., g * GROUP_COLS + X_COLS + n:(g + 1) * GROUP_COLS] for g in range(SSD_GROUPS)]
    return jnp.concatenate(xs + bs + cs, axis=-1)


UP_BLOCK = 256


def _interleave_up(w):
    parts = []
    for j in range(D_FF // UP_BLOCK):
        parts += [w[:, j * UP_BLOCK:(j + 1) * UP_BLOCK], w[:, D_FF + j * UP_BLOCK:D_FF + (j + 1) * UP_BLOCK]]
    return jnp.concatenate(parts, axis=1)


def _deinterleave_up(w):
    blocks = [w[:, j * UP_BLOCK:(j + 1) * UP_BLOCK] for j in range(2 * D_FF // UP_BLOCK)]
    return jnp.concatenate(blocks[0::2] + blocks[1::2], axis=1)


def _pad_q_heads(w):
    k = w.shape[0]
    return jnp.pad(w.reshape(k, N_ATTN_HEADS, QK_DIM), ((0, 0), (0, 0), (0, HEAD_TILE - QK_DIM))).reshape(k, -1)


def _unpad_q_heads(w):
    k = w.shape[0]
    return w.reshape(k, N_ATTN_HEADS, HEAD_TILE)[..., :QK_DIM].reshape(k, N_ATTN_HEADS * QK_DIM)


def _rope_tables(seq_len):
    n_rows = seq_len // GRID_W
    row = jnp.repeat(jnp.arange(n_rows), GRID_W).astype(F32)
    col = jnp.tile(jnp.arange(GRID_W), n_rows).astype(F32)
    axis_dim = QK_ROPE_DIM // 2
    inv_freq = ROPE_THETA ** (-jnp.arange(0, axis_dim, 2, dtype=F32) / axis_dim)
    ang_r = row[:, None] * inv_freq
    ang_c = col[:, None] * inv_freq
    ang = jnp.concatenate([ang_r, ang_r, ang_c, ang_c], axis=-1)
    return jnp.cos(ang), jnp.sin(ang)


def _rot_matrix(width, start):
    r = np.zeros((width, width), np.float32)
    quarter = QK_ROPE_DIM // 4
    for base in (0, QK_ROPE_DIM // 2):
        for i in range(quarter):
            r[start + base + quarter + i, start + base + i] = -1.0
            r[start + base + i, start + base + quarter + i] = 1.0
    return jnp.asarray(r)


ROPE_STEP = QK_ROPE_DIM // 4


def _rope_flat_fn(x, cos, sin_up, sin_down):
    reps = x.shape[1] // cos.shape[1]

    def heads(t):
        return jnp.concatenate([t] * reps, axis=1)

    return (x * heads(cos) + _roll_lanes(x, -ROPE_STEP) * heads(sin_up) + _roll_lanes(x, ROPE_STEP) * heads(sin_down),)


def _rope_flat_transpose_fn(g, cos, sin_up, sin_down):
    reps = g.shape[1] // cos.shape[1]

    def heads(t):
        return jnp.concatenate([t] * reps, axis=1)

    return (g * heads(cos) + _roll_lanes(g * heads(sin_up), ROPE_STEP) + _roll_lanes(g * heads(sin_down), -ROPE_STEP),)


def _krdt_fn(x, cos, sin, rot, bias):
    lane = lax.broadcasted_iota(jnp.int32, (1, KRDT_WIDTH), 1)
    is_dt = (lane >= QK_ROPE_DIM) & (lane < QK_ROPE_DIM + 2 * N_SSD_HEADS)
    roped = x * cos + _dot_exact(x, rot) * sin
    return (jnp.where(is_dt, _softplus(x + bias), roped),)


def _pre_fn(u, w, shift, scale):
    return (_rms(u, w) * (1.0 + scale) + shift,)


def _norm_fn(x, w):
    return (_rms(x, w),)


def _finish_fn(yf, yb, z, w):
    return (_rms((yf + yb) * _silu(z), w),)


def _mid_fn(x, mix, w_post, w_pre, gate, shift, scale):
    x1 = x + gate * _rms(mix, w_post)
    return (x1, _rms(x1, w_pre) * (1.0 + scale) + shift)


def _loss_fn(x1, ffn, tgt, w_post, gate):
    y = x1 + gate * _rms(ffn, w_post)
    err = y - tgt
    return (0.5 * jnp.mean(err * err, axis=-1, keepdims=True),)


def _bias_fn(x, b):
    return (x + b,)


def _silu_fn(x):
    return (_silu(x),)


def kernel(x, c, ctx, c_ctx, w_mod, b_mod, mix_pre_norm, mix_post_norm, w_in, q_norm, w_q_up, kv_norm, w_kv_up, ssd_conv_w, ssd_conv_b, ssd_a_log, ssd_dt_bias, ssd_d, ssd_norm, w_out, ffn_pre_norm, ffn_post_norm, w_up, ffn_conv_w, ffn_conv_b, w_down, loss_target, m_c_ctx, m_w_mod, m_b_mod, m_mix_pre_norm, m_mix_post_norm, m_w_in, m_q_norm, m_w_q_up, m_kv_norm, m_w_kv_up, m_ssd_conv_w, m_ssd_conv_b, m_ssd_a_log, m_ssd_dt_bias, m_ssd_d, m_ssd_norm, m_w_out, m_ffn_pre_norm, m_ffn_post_norm, m_w_up, m_ffn_conv_w, m_ffn_conv_b, m_w_down, v_c_ctx, v_w_mod, v_b_mod, v_mix_pre_norm, v_mix_post_norm, v_w_in, v_q_norm, v_w_q_up, v_kv_norm, v_w_kv_up, v_ssd_conv_w, v_ssd_conv_b, v_ssd_a_log, v_ssd_dt_bias, v_ssd_d, v_ssd_norm, v_w_out, v_ffn_pre_norm, v_ffn_post_norm, v_w_up, v_ffn_conv_w, v_ffn_conv_b, v_w_down):
    args = dict(locals())
    names = ["c_ctx", "w_mod", "b_mod", "mix_pre_norm", "mix_post_norm", "w_in", "q_norm", "w_q_up", "kv_norm",
             "w_kv_up", "ssd_conv_w", "ssd_conv_b", "ssd_a_log", "ssd_dt_bias", "ssd_d", "ssd_norm", "w_out",
             "ffn_pre_norm", "ffn_post_norm", "w_up", "ffn_conv_w", "ffn_conv_b", "w_down"]
    nb, s, d = x.shape
    nctx_rows = ctx.shape[1]
    lt = nctx_rows + s
    tl = 256 if (nctx_rows % 256 == 0 and s % 256 == 0) else 128
    nctx = nctx_rows // tl
    ncc = nctx_rows // SSD_CHUNK
    h, e, g2 = N_ATTN_HEADS, HEADS_PER_GROUP, SSD_GROUPS
    chip = 2 * lax.axis_index("x") + lax.axis_index("y")

    big_local = {n: args[n][0] for n, _, _, _ in BIG}
    big_info = {n: (rows, cols, axis) for n, rows, cols, axis in BIG}

    def pack_shards(group):
        return _pack_rows([big_local[n].astype(WIRE_DTYPE) for n in group])

    def unpack_gathered(gathered, mine, group):
        gathered = lax.dynamic_update_slice(gathered, mine[None], (chip, 0, 0))
        res, o = {}, 0
        for n in group:
            rows, cols, axis = big_info[n]
            lr, lc = big_local[n].shape
            nr = lr * lc // PACK_COLS
            seg = gathered[:, o:o + nr].reshape(N_CHIPS, lr, lc)
            o += nr
            res[n] = seg.reshape(rows, cols) if axis == 0 else jnp.transpose(seg, (1, 0, 2)).reshape(rows, cols)
        return res

    core = lax.axis_index("c")

    def pair_sums(grads_full, group, tag):
        parts = []
        for n in group:
            _, _, axis = big_info[n]
            lr, lc = big_local[n].shape
            gfull = grads_full[n]
            shards = (gfull.reshape(N_CHIPS, lr, lc) if axis == 0
                      else jnp.transpose(gfull.reshape(lr, N_CHIPS, lc), (1, 0, 2)))
            parts.append(shards.reshape(N_CHIPS, lr * lc // PACK_COLS, PACK_COLS))
        gpack = jnp.concatenate(parts, axis=1).astype(WIRE_DTYPE)
        half = gpack.shape[1] // 2
        gpack = gpack.reshape(N_CHIPS, 2, half, PACK_COLS)
        got = _swap_halves(gpack, name="grad_swap_" + tag)
        own = lax.dynamic_index_in_dim(gpack, core, axis=1, keepdims=False)
        flat = (N_CHIPS * half, PACK_COLS)
        return _sum_list([own.reshape(flat), got.reshape(flat)], out_dtype=WIRE_DTYPE,
                         name="grad_add_pair_" + tag).reshape(N_CHIPS, half, PACK_COLS)

    def chip_total(sums, scattered, tag):
        mine_sum = lax.dynamic_index_in_dim(sums, chip, axis=0, keepdims=True)
        scattered = lax.dynamic_update_slice(scattered, mine_sum, (chip, 0, 0))
        return _sum_slots(scattered, out_dtype=F32, name="grad_add_chips_" + tag)

    packed_now, packed_late = pack_shards(GATHER_NOW), pack_shards(GATHER_LATE)
    full = unpack_gathered(_gather_shards(packed_now, name="gather_weights"), packed_now, GATHER_NOW)
    n_sc, n_fc = ssd_conv_w.shape[2], ffn_conv_w.shape[2]
    n_conv = SSD_CONV * n_sc + FFN_CONV * n_fc
    first_rows = -(-(n_conv + nb * d) // (8 * LANES)) * 8
    first_all = _allgather_small(_pack_small([ssd_conv_w[0], ffn_conv_w[0], c], first_rows), name="gather_conv_c")
    first_all = first_all.reshape(N_DEV, -1)
    conv_all = first_all[::2]
    ssd_conv_full = jnp.concatenate(
        [conv_all[j][:SSD_CONV * n_sc].reshape(SSD_CONV, n_sc) for j in range(N_CHIPS)], axis=1)
    ffn_conv_full = jnp.concatenate(
        [conv_all[j][SSD_CONV * n_sc:n_conv].reshape(FFN_CONV, n_fc) for j in range(N_CHIPS)], axis=1)
    c_every = first_all[:, n_conv:n_conv + nb * d].reshape(N_DEV * nb, d)

    w_in_p = _perm_in_cols(full["w_in"])
    o_cq, o_ckv, o_z = 0, Q_LORA_RANK, Q_LORA_RANK + KV_LORA_RANK
    o_xbc, o_kr = o_z + D_INNER, o_z + D_INNER + XBC_WIDTH
    w_krdt = jnp.pad(w_in_p[:, o_kr:], ((0, 0), (0, KRDT_WIDTH - QK_ROPE_DIM - 2 * N_SSD_HEADS)))
    w_segs = [w_in_p[:, o_cq:o_ckv], w_in_p[:, o_ckv:o_z], w_in_p[:, o_z:o_xbc], _group_xbc(w_in_p[:, o_xbc:o_kr]),
              w_krdt]
    ssd_conv_g, ssd_conv_b_g = _group_xbc(ssd_conv_full), _group_xbc(ssd_conv_b)
    w_q_pad = _pad_q_heads(full["w_q_up"])

    mod_rows = 16
    n_ex = N_DEV * nb
    all_rows = -(-(n_ex + 1) // 16) * 16
    me = 2 * chip + lax.axis_index("c")
    c_all = jnp.concatenate([c_every, c_ctx[None, :], jnp.zeros((all_rows - n_ex - 1, d), F32)], axis=0)[None]
    (s_all,) = _row_fwd(_silu_fn, toks=[(c_all, 0, None, 0)], outs=[(d, F32)], nb=1, nl=all_rows, tl=all_rows,
                        name="mod_silu")
    w_mod_local = w_mod[0]
    mod_cols = w_mod_local.shape[1]
    mod_part = _mm(s_all[0], w_mod_local, name="mod_mm")
    mod_parts = _allgather_small(mod_part, name="gather_mod")[::2]
    mod_every = jnp.concatenate([mod_parts[j] for j in range(N_CHIPS)], axis=1)
    mod_lin = jnp.concatenate([lax.dynamic_slice_in_dim(mod_every, me * nb, nb, axis=0), mod_every[n_ex:n_ex + 1],
                               jnp.zeros((mod_rows - nb - 1, N_MOD * d), F32)], axis=0)
    (mod,) = _row_fwd(_bias_fn, toks=[(mod_lin[None], 0, None, 0)], vecs=[b_mod], outs=[(N_MOD * d, F32)], nb=1,
                      nl=mod_rows, tl=mod_rows, name="mod_bias")
    mods = [mod[0][:, k * d:(k + 1) * d][:, None, :] for k in range(N_MOD)]
    mods_lat = [m[:nb] for m in mods]

    u = jnp.concatenate([ctx, x], axis=1)
    (h1,) = _row_fwd(_pre_fn, toks=[(u, 0, None, 0)], vecs=[mix_pre_norm], bvecs=[mods[0], mods[1]],
                     outs=[(d, MXU_DTYPE)], nb=nb, nl=lt, tl=tl, nctx=nctx, name="pre1")
    h1f = h1.reshape(nb * lt, d)
    p_cq, p_ckv, p_z, p_xbc, p_krdt = [
        _mm(h1f, w, name="in_" + nm).reshape(nb, lt, -1)
        for nm, w in zip(("cq", "ckv", "z", "xbc", "krdt"), w_segs)]

    (cqn,) = _row_fwd(_norm_fn, toks=[(p_cq, nctx, None, 0)], vecs=[q_norm], outs=[(Q_LORA_RANK, MXU_DTYPE)],
                      nb=nb, nl=s, tl=tl, name="q_norm")
    q_flat = _mm(cqn.reshape(nb * s, -1), w_q_pad, name="q_up").reshape(nb, s, h * HEAD_TILE)
    cos, sin = _rope_tables(s)
    ones, zeros = jnp.ones((s, QK_NOPE_DIM), F32), jnp.zeros((s, QK_NOPE_DIM), F32)
    tail = HEAD_TILE - QK_DIM
    up_lanes = ((jnp.arange(QK_ROPE_DIM) // ROPE_STEP) % 2 == 0)[None, :]
    q_tables = [jnp.concatenate([pad, t, pad[:, :tail]], axis=1)[None]
                for pad, t in ((ones, cos), (zeros, jnp.where(up_lanes, -sin, 0.0)), (zeros, jnp.where(up_lanes, 0.0, sin)))]
    tq = 256
    (q_roped,) = _row_fwd(_rope_flat_fn, toks=[(q_flat, 0, None, 0)], poss=q_tables, outs=[(h * HEAD_TILE, MXU_DTYPE)],
                          nb=nb, nl=s, tl=tl, name="rope_q")

    (ckvn,) = _row_fwd(_norm_fn, toks=[(p_ckv, 0, None, 0)], vecs=[kv_norm], outs=[(KV_LORA_RANK, MXU_DTYPE)],
                       nb=nb, nl=lt, tl=tl, name="kv_norm")
    kv_flat = _mm(ckvn.reshape(nb * lt, -1), full["w_kv_up"], out_dtype=MXU_DTYPE, name="kv_up").reshape(nb, lt, -1)

    pad_w = KRDT_WIDTH - QK_ROPE_DIM
    cos_k = jnp.concatenate([jnp.ones((nctx_rows, KRDT_WIDTH), F32),
                             jnp.concatenate([cos, jnp.ones((s, pad_w), F32)], axis=1)], axis=0)[None]
    sin_k = jnp.concatenate([jnp.zeros((nctx_rows, KRDT_WIDTH), F32),
                             jnp.concatenate([sin, jnp.zeros((s, pad_w), F32)], axis=1)], axis=0)[None]
    rot_k = _rot_matrix(KRDT_WIDTH, 0)
    dt_bias_row = jnp.pad(ssd_dt_bias.reshape(1, -1), ((0, 0), (QK_ROPE_DIM, pad_w - 2 * N_SSD_HEADS)))
    (krdt,) = _row_fwd(_krdt_fn, toks=[(p_krdt, 0, None, 0)], poss=[cos_k, sin_k], vecs=[rot_k, dt_bias_row],
                       outs=[(KRDT_WIDTH, F32)], nb=nb, nl=lt, tl=tl, name="krdt")
    kr = jnp.pad(krdt[..., :QK_ROPE_DIM].astype(MXU_DTYPE), ((0, 0), (0, 0), (QK_NOPE_DIM, HEAD_TILE - QK_DIM)))
    attn, lse, gathered_late = _attn_fwd(q_roped, kv_flat, kr, packed_late, tq=tq, name="attn_fwd")
    full.update(unpack_gathered(gathered_late, packed_late, GATHER_LATE))
    w_up_il = _interleave_up(full["w_up"])
    w_out_a, w_out_s = full["w_out"][:ATTN_WIDTH], full["w_out"][ATTN_WIDTH:]

    seg = nctx_rows

    def conv_ssd_fn(xv, w, b):
        return (_silu(_dwconv(xv, w, seg) + b),)

    def conv_ssd_bwd(xv, w, b, dy):
        cv = _dwconv(xv, w, seg) + b
        sg = _sigmoid(cv)
        dc = dy * (sg * (1.0 + cv * (1.0 - sg)))
        dx, dw = _dwconv_back(xv, dc, w, seg)
        return dx, dw, jnp.sum(dc, axis=0, keepdims=True)

    cb_ssd = 256
    conv_vecs = [(ssd_conv_g, 0), (ssd_conv_b_g, 0)]
    (xbc,) = _seq_fwd(conv_ssd_fn, toks=[(p_xbc, 0, 1)], vecs=conv_vecs, outs=[F32], nb=nb, nl=lt, nc=XBC_WIDTH,
                      cb=cb_ssd, name="conv_ssd")
    dt = krdt[..., QK_ROPE_DIM:QK_ROPE_DIM + 2 * N_SSD_HEADS].reshape(nb, lt, 2, g2, e)
    dt_lane = QK_ROPE_DIM + N_SSD_HEADS * jnp.arange(2)[:, None, None] + jnp.arange(D_INNER)[None, None, :] // SSD_HEAD_DIM
    spread = (jnp.arange(KRDT_WIDTH)[None, :, None] == dt_lane).astype(F32)

    def spread_fn(v, s0, s1):
        return (_mask_dot_raw(s0, v, False), _mask_dot_raw(s1, v, False))

    dtc = _row_fwd(spread_fn, toks=[(krdt, 0, None, 0)], vecs=[spread[0], spread[1]],
                   outs=[(D_INNER, F32), (D_INNER, F32)], nb=nb, nl=lt, tl=tl, name="dt_spread")
    dtr = jnp.transpose(dt, (2, 0, 3, 4, 1))
    a_neg = -jnp.exp(ssd_a_log[0]).reshape(2, g2, e)
    d_chan = jnp.repeat(ssd_d[0], SSD_HEAD_DIM).reshape(g2, 1, X_COLS)
    a_chan = [jnp.repeat(a_neg[dr].reshape(-1), SSD_HEAD_DIM).reshape(g2, 1, X_COLS) for dr in range(2)]
    scan_args = [(xbc, dtc[dr], dtr[dr], a_chan[dr], a_neg[dr][:, :, None], d_chan) for dr in range(2)]
    (y0, ent0), (y1, ent1) = _scan_fwd(scan_args, ncc=ncc, name="scan_fwd")
    ys, ents = [y0, y1], [ent0, ent1]
    (ssd,) = _row_fwd(_finish_fn, toks=[(ys[0], nctx, None, 0), (ys[1], nctx, None, 0), (p_z, nctx, None, 0)],
                      vecs=[ssd_norm], outs=[(D_INNER, MXU_DTYPE)], nb=nb, nl=s, tl=tl, name="ssd_finish")

    attn_f, ssd_f = attn.reshape(nb * s, ATTN_WIDTH), ssd.reshape(nb * s, D_INNER)
    mix = _mm_sum([(attn_f, w_out_a), (ssd_f, w_out_s)], name="out_proj").reshape(nb, s, d)

    mid_bvecs = [mods_lat[2], mods_lat[3], mods_lat[4]]
    x1, h2 = _row_fwd(_mid_fn, toks=[(x, 0, None, 0), (mix, 0, None, 0)], vecs=[mix_post_norm, ffn_pre_norm],
                      bvecs=mid_bvecs, outs=[(d, F32), (d, MXU_DTYPE)], nb=nb, nl=s, tl=tl, name="mid")
    up = _mm(h2.reshape(nb * s, d), w_up_il, name="ffn_up").reshape(nb, s, 2 * D_FF)

    def glu_fn(gv, w, b):
        return (_gelu(_dwconv(gv[:, :UP_BLOCK], w, 0) + b) * gv[:, UP_BLOCK:],)

    def glu_bwd(gv, w, b, da):
        gate, val = gv[:, :UP_BLOCK], gv[:, UP_BLOCK:]
        cv = _dwconv(gate, w, 0) + b
        cdf = 0.5 * (1.0 + lax.erf(cv * (2.0 ** -0.5)))
        pdf = jnp.exp(-0.5 * cv * cv) * (1.0 / math.sqrt(2.0 * math.pi))
        dc = (da * val) * (cdf + cv * pdf)
        dgate, dw = _dwconv_back(gate, dc, w, 0)
        return jnp.concatenate([dgate, da * (cv * cdf)], axis=1), dw, jnp.sum(dc, axis=0, keepdims=True)

    cb_ffn = UP_BLOCK
    glu_toks = [(up, 0, 2)]
    glu_vecs = [(ffn_conv_full, 0), (ffn_conv_b, 0)]
    (act,) = _seq_fwd(glu_fn, toks=glu_toks, vecs=glu_vecs, outs=[MXU_DTYPE], nb=nb, nl=s, nc=D_FF, cb=cb_ffn,
                      name="conv_glu")
    ffn = _mm(act.reshape(nb * s, D_FF), full["w_down"], name="ffn_down").reshape(nb, s, d)

    loss_toks = [(x1, 0, None, 0), (ffn, 0, None, 0), (loss_target, 0, None, 0)]
    ones_rows = jnp.ones((nb, s, 1), F32)
    (dx1_a, dffn, _), (g_ffn_post,), ((g_gate5, _),), (loss_rows,) = _row_bwd(
        _loss_fn, toks=loss_toks, vecs=[ffn_post_norm], bvecs=[mods_lat[5]], cots=[(ones_rows, 0)],
        tok_grads=[F32, MXU_DTYPE, None], emit=[(0, 1, F32)], nb=nb, nl=s, tl=tl, name="loss_bwd")
    loss_part = jnp.sum(loss_rows)

    dffn_f = dffn.reshape(nb * s, d)
    g_w_down = _mm(act.reshape(nb * s, D_FF), dffn_f, ta=True, name="wg_down")
    dact = _mm(dffn_f, full["w_down"], tb=True, out_dtype=MXU_DTYPE, name="dg_down").reshape(nb, s, D_FF)
    (dup,), (g_ffn_conv_w, g_ffn_conv_b) = _seq_bwd(
        glu_fn, toks=glu_toks, vecs=glu_vecs, cots=[[dact]], tok_grads=[MXU_DTYPE], nb=nb, nl=s, nc=D_FF,
        cb=cb_ffn, name="conv_glu_bwd", bwd_fn=glu_bwd)
    dup = dup.reshape(nb * s, 2 * D_FF)
    g_w_up = _deinterleave_up(_mm(h2.reshape(nb * s, d), dup, ta=True, name="wg_up"))
    dh2 = _mm(dup, w_up_il, tb=True, name="dg_up").reshape(nb, s, d)

    (dx_res, dmix), (g_mix_post, g_ffn_pre), ((g_gate2, _), (g_shift3, _), (g_scale4, _)), _ = _row_bwd(
        _mid_fn, toks=[(x, 0, None, 0), (mix, 0, None, 0)], vecs=[mix_post_norm, ffn_pre_norm], bvecs=mid_bvecs,
        cots=[(dx1_a, 0), (dh2, 0)], tok_grads=[F32, MXU_DTYPE], nb=nb, nl=s, tl=tl, name="mid_bwd")

    dmix_f = dmix.reshape(nb * s, d)
    g_w_out = jnp.concatenate([_mm(attn_f, dmix_f, ta=True, name="wg_out_attn"),
                               _mm(ssd_f, dmix_f, ta=True, name="wg_out_ssd")], axis=0)
    early_sums = pair_sums({"w_up": g_w_up, "w_down": g_w_down, "w_out": g_w_out}, REDUCE_EARLY, "early")
    dattn = _mm(dmix_f, w_out_a, tb=True, name="dg_out_attn").reshape(nb, s, ATTN_WIDTH)
    dssd = _mm(dmix_f, w_out_s, tb=True, name="dg_out_ssd").reshape(nb, s, D_INNER)

    (dy, _, dz), (g_ssd_norm,), _, _ = _row_bwd(
        _finish_fn, toks=[(ys[0], 0, None, 0), (ys[1], 0, None, 0), (p_z, 0, None, 0)], vecs=[ssd_norm],
        cots=[(dssd, -nctx)], tok_grads=[F32, None, MXU_DTYPE], nb=nb, nl=lt, tl=tl, name="ssd_finish_bwd")
    scan_grads = _scan_bwd(scan_args, ents, dy, ncc=ncc, name="scan_bwd")

    def collect_fn(g0, g1, c0, c1):
        return (_mask_dot_raw(c0, g0, False) + _mask_dot_raw(c1, g1, False),)

    (g_dt_lanes,) = _row_fwd(collect_fn, toks=[(scan_grads[0][1], 0, None, 0), (scan_grads[1][1], 0, None, 0)],
                             vecs=[spread[0].T, spread[1].T], outs=[(KRDT_WIDTH, F32)], nb=nb, nl=lt, tl=tl,
                             name="dt_collect")
    g_dt_dirs, g_a = [], []
    for _, _, gdtr, gac, gar, _ in scan_grads:
        g_dt_dirs.append(jnp.transpose(gdtr, (0, 3, 1, 2)))
        g_a.append(jnp.sum(jnp.sum(gac.reshape(nb, g2, e, SSD_HEAD_DIM), axis=-1) + gar[:, :, :, 0], axis=0))
    g_d_chan = jnp.sum(scan_grads[0][5], axis=0)
    g_a_log = (jnp.stack(g_a) * a_neg).reshape(1, 2, N_SSD_HEADS)
    g_dt = (jnp.stack(g_dt_dirs, axis=2).reshape(nb, lt, 2 * N_SSD_HEADS)
            + g_dt_lanes[..., QK_ROPE_DIM:QK_ROPE_DIM + 2 * N_SSD_HEADS])
    (dp_xbc,), (g_ssd_conv_w, g_ssd_conv_b) = _seq_bwd(
        conv_ssd_fn, toks=[(p_xbc, 0, 1)], vecs=conv_vecs, cots=[[scan_grads[0][0], scan_grads[1][0]]],
        tok_grads=[MXU_DTYPE], nb=nb, nl=lt, nc=XBC_WIDTH, cb=cb_ssd, name="conv_ssd_bwd", bwd_fn=conv_ssd_bwd)
    g_ssd_conv_w, g_ssd_conv_b = _ungroup_xbc(g_ssd_conv_w), _ungroup_xbc(g_ssd_conv_b)

    dq_roped, dkv, dkr, early_scattered = _attn_bwd(q_roped, kv_flat, kr, attn, lse, dattn, early_sums, tq=tq,
                                                    name="attn_bwd")
    (dq_flat,) = _row_fwd(_rope_flat_transpose_fn, toks=[(dq_roped, 0, None, 0)], poss=q_tables,
                          outs=[(h * HEAD_TILE, MXU_DTYPE)], nb=nb, nl=s, tl=tl, name="rope_q_bwd")
    dq_flat = dq_flat.reshape(nb * s, h * HEAD_TILE)
    g_w_q_up = _unpad_q_heads(_mm(cqn.reshape(nb * s, -1), dq_flat, ta=True, name="wg_q_up"))
    dcqn = _mm(dq_flat, w_q_pad, tb=True, name="dg_q_up").reshape(nb, s, Q_LORA_RANK)
    (dp_cq,), (g_q_norm,), _, _ = _row_bwd(_norm_fn, toks=[(p_cq, 0, None, 0)], vecs=[q_norm], cots=[(dcqn, -nctx)],
                                           tok_grads=[MXU_DTYPE], nb=nb, nl=lt, tl=tl, name="q_norm_bwd")

    dkv_flat = dkv.reshape(nb * lt, -1)
    g_w_kv_up = _mm(ckvn.reshape(nb * lt, -1), dkv_flat, ta=True, name="wg_kv_up")
    dckvn = _mm(dkv_flat, full["w_kv_up"], tb=True, name="dg_kv_up").reshape(nb, lt, KV_LORA_RANK)
    (dp_ckv,), (g_kv_norm,), _, _ = _row_bwd(_norm_fn, toks=[(p_ckv, 0, None, 0)], vecs=[kv_norm], cots=[(dckvn, 0)],
                                             tok_grads=[MXU_DTYPE], nb=nb, nl=lt, tl=tl, name="kv_norm_bwd")

    g_krdt = jnp.concatenate([dkr[..., QK_NOPE_DIM:QK_DIM], g_dt, jnp.zeros((nb, lt, pad_w - 2 * N_SSD_HEADS), F32)],
                             axis=-1)
    (dp_krdt,), (_, g_dt_bias_row), _, _ = _row_bwd(
        _krdt_fn, toks=[(p_krdt, 0, None, 0)], poss=[cos_k, sin_k], vecs=[rot_k, dt_bias_row], cots=[(g_krdt, 0)],
        tok_grads=[MXU_DTYPE], nb=nb, nl=lt, tl=tl, name="krdt_bwd")

    dp_segs = [t.reshape(nb * lt, -1) for t in (dp_cq, dp_ckv, dz, dp_xbc, dp_krdt)]
    g_segs = [_mm(h1f, t, ta=True, name="wg_in_" + nm) for nm, t in zip(("cq", "ckv", "z", "xbc", "krdt"), dp_segs)]
    g_segs[3] = _ungroup_xbc(g_segs[3])
    g_w_in_p = jnp.concatenate(g_segs, axis=1)
    dh1 = _mm_sum(list(zip(dp_segs, w_segs)), tb=True, name="dg_in").reshape(nb, lt, d)

    def pre_res_fn(uv, w, shift, scale):
        return _pre_fn(uv, w, shift, scale) + (uv,)

    (grad_x,), (g_mix_pre,), ((g_shift0, g_shift0c), (g_scale1, g_scale1c)), _ = _row_bwd(
        pre_res_fn, toks=[(u, 0, None, 0)], vecs=[mix_pre_norm], bvecs=[mods[0], mods[1]],
        cots=[(dh1, 0), (dx_res, -nctx)], tok_grads=[F32], nb=nb, nl=lt, tl=tl, nctx=nctx, drop_blocks=nctx,
        name="pre1_bwd")

    zero_row = jnp.zeros((1, 1, d), F32)
    lat = [g_shift0, g_scale1, g_gate2, g_shift3, g_scale4, g_gate5]
    ctxg = [g_shift0c, g_scale1c, zero_row, zero_row, zero_row, zero_row]
    dmod = jnp.concatenate([jnp.concatenate([a, b], axis=0)[:, 0, :] for a, b in zip(lat, ctxg)], axis=-1)
    dmod = jnp.pad(dmod, ((0, mod_rows - nb - 1), (0, 0)))
    _, (g_b_mod,), _, _ = _row_bwd(_bias_fn, toks=[(mod_lin[None], 0, None, 0)], vecs=[b_mod], cots=[(dmod[None], 0)],
                                   tok_grads=[None], nb=1, nl=mod_rows, tl=mod_rows, name="mod_bias_bwd")
    dmod_all = _allgather_small(dmod[:8], name="gather_dmod")
    dmod_ctx = _sum_slots(dmod_all, out_dtype=F32, name="dmod_ctx_add")[nb:nb + 1]
    dmod_every = jnp.concatenate([dmod_all[:, :nb].reshape(n_ex, N_MOD * d), dmod_ctx,
                                  jnp.zeros((all_rows - n_ex - 1, N_MOD * d), F32)], axis=0)
    dmod_mine = lax.dynamic_slice_in_dim(dmod_every, chip * mod_cols, mod_cols, axis=1)
    g_w_mod = _mm(s_all[0], dmod_mine, ta=True, name="wg_mod")[None]
    ds_all = _mm(dmod_mine, w_mod_local, tb=True, name="dg_mod")
    (dc_all,), _, _, _ = _row_bwd(_silu_fn, toks=[(c_all, 0, None, 0)], cots=[(ds_all[None], 0)], tok_grads=[F32],
                                  nb=1, nl=all_rows, tl=all_rows, name="mod_silu_bwd")
    g_c_ctx = 0.5 * dc_all[0, n_ex]

    g_w_in = _unperm_in_cols(g_w_in_p[:, :IN_WIDTH])
    last_sums = pair_sums({"w_in": g_w_in, "w_q_up": g_w_q_up, "w_kv_up": g_w_kv_up}, REDUCE_LAST, "last")
    halves = [chip_total(early_sums, early_scattered, "early"),
              chip_total(last_sums, _scatter_to_chips(last_sums, name="grad_scatter"), "last")]
    my_halves = jnp.concatenate(halves, axis=0)
    joined = lax.dynamic_update_slice(_join_halves(my_halves, name="grad_join"), my_halves[None], (core, 0, 0))
    g_shards, o = {}, 0
    for group, hv in zip((REDUCE_EARLY, REDUCE_LAST), halves):
        g_shards[group] = joined[:, o:o + hv.shape[0]].reshape(2 * hv.shape[0], PACK_COLS)
        o += hv.shape[0]

    g_d = jnp.sum(g_d_chan.reshape(N_SSD_HEADS, SSD_HEAD_DIM), axis=1)[None]
    g_dt_bias = g_dt_bias_row[:, QK_ROPE_DIM:QK_ROPE_DIM + 2 * N_SSD_HEADS].reshape(1, 2, N_SSD_HEADS)
    small_names = ["c_ctx", "b_mod", "mix_pre_norm", "mix_post_norm", "q_norm", "kv_norm", "ssd_conv_w", "ssd_conv_b",
                   "ssd_a_log", "ssd_dt_bias", "ssd_d", "ssd_norm", "ffn_pre_norm", "ffn_post_norm", "ffn_conv_w",
                   "ffn_conv_b"]
    small_grads = [g_c_ctx, g_b_mod, g_mix_pre, g_mix_post, g_q_norm, g_kv_norm, g_ssd_conv_w, g_ssd_conv_b,
                   g_a_log, g_dt_bias, g_d, g_ssd_norm, g_ffn_pre, g_ffn_post, g_ffn_conv_w, g_ffn_conv_b]
    small_shapes = [tuple(np.shape(a)) for a in small_grads] + [()]
    n_small = sum(int(np.prod(shp)) for shp in small_shapes)
    small_rows = -(-n_small // (8 * LANES)) * 8
    small_all = _allgather_small(_pack_small(small_grads + [loss_part], small_rows), name="gather_small")
    small_sum = _sum_slots(small_all, out_dtype=F32, name="small_add")
    small_red = _unpack_small(small_sum, small_shapes)
    loss = small_red[-1]
    grads = dict(zip(small_names, small_red[:-1]))
    grads["ssd_conv_w"] = lax.dynamic_slice_in_dim(grads["ssd_conv_w"], chip * n_sc, n_sc, axis=1)[None]
    grads["ffn_conv_w"] = lax.dynamic_slice_in_dim(grads["ffn_conv_w"], chip * n_fc, n_fc, axis=1)[None]
    for n in small_names:
        grads[n] = grads[n].reshape(args[n].shape)

    delta, new_m, new_v = {}, {}, {}
    grads["w_mod"] = g_w_mod
    for group, g_shard in g_shards.items():
        o = 0
        for n in group:
            lr, lc = big_local[n].shape
            nr = lr * lc // PACK_COLS
            grads[n] = g_shard[o:o + nr].reshape(1, lr, lc)
            o += nr
    for n in ["w_mod"] + [n for n, _, _, _ in BIG]:
        dl, nm, nv = _adamw(args[n][0], grads[n][0], args["m_" + n][0], args["v_" + n][0], name="adamw_" + n)
        delta[n], new_m[n], new_v[n] = dl[None], nm[None], nv[None]
    sm_shapes = [args[n].shape for n in small_names]
    n_sm = sum(int(np.prod(shp)) for shp in sm_shapes)
    sm_rows = -(-n_sm // (8 * LANES)) * 8
    packs = [_pack_small([src[n] for n in small_names], sm_rows)
             for src in (args, grads, {n: args["m_" + n] for n in small_names}, {n: args["v_" + n] for n in small_names})]
    for out_dict, packed_out in zip((delta, new_m, new_v), _adamw(*packs, name="adamw_small")):
        out_dict.update(zip(small_names, _unpack_small(packed_out, sm_shapes)))

    return (loss, grad_x, *[grads[n] for n in names], *[delta[n] for n in names], *[new_m[n] for n in names],
            *[new_v[n] for n in names])
```

```python
import functools
import math

import numpy as np
import jax
import jax.numpy as jnp
from jax import lax
from jax.experimental import pallas as pl
from jax.experimental.pallas import tpu as pltpu

F32 = jnp.float32
MXU_DTYPE = jnp.bfloat16
WIRE_DTYPE = jnp.bfloat16
VMEM_LIMIT_BYTES = 56 * 1024 * 1024
HIGHEST = lax.Precision.HIGHEST

D_MODEL = 1024
N_MOD = 6
EPS = 1e-6
GRID_W = 64
N_ATTN_HEADS = 16
QK_NOPE_DIM = 64
QK_ROPE_DIM = 32
QK_DIM = QK_NOPE_DIM + QK_ROPE_DIM
V_HEAD_DIM = 64
Q_LORA_RANK = 384
KV_LORA_RANK = 256
ROPE_THETA = 10000.0
ATTN_SCALE = QK_DIM ** -0.5
ATTN_WIDTH = N_ATTN_HEADS * V_HEAD_DIM
N_SSD_HEADS = 16
SSD_HEAD_DIM = 64
SSD_GROUPS = 2
HEADS_PER_GROUP = N_SSD_HEADS // SSD_GROUPS
SSD_STATE = 128
SSD_CONV = 5
SSD_CHUNK = 128
D_INNER = N_SSD_HEADS * SSD_HEAD_DIM
GN = SSD_GROUPS * SSD_STATE
XBC_WIDTH = D_INNER + 2 * GN
D_FF = 2816
FFN_CONV = 3
KRDT_WIDTH = 128
IN_WIDTH = Q_LORA_RANK + KV_LORA_RANK + QK_ROPE_DIM + D_INNER + XBC_WIDTH + 2 * N_SSD_HEADS

ADAM_LR = 0.001
ADAM_B1 = 0.9
ADAM_B2 = 0.999
ADAM_EPS = 1e-08
ADAM_WD = 0.01
ADAM_STEP = 10

N_CHIPS = 4
N_DEV = 8
MESH = pl.DeviceIdType.MESH
LANES = 128

BIG = (("w_in", D_MODEL, IN_WIDTH, 1),
       ("w_q_up", Q_LORA_RANK, N_ATTN_HEADS * QK_DIM, 1),
       ("w_kv_up", KV_LORA_RANK, N_ATTN_HEADS * (QK_NOPE_DIM + V_HEAD_DIM), 1),
       ("w_out", ATTN_WIDTH + D_INNER, D_MODEL, 0), ("w_up", D_MODEL, 2 * D_FF, 1),
       ("w_down", D_FF, D_MODEL, 0))
PACK_COLS = 1024
GATHER_NOW, GATHER_LATE = ("w_in", "w_q_up", "w_kv_up"), ("w_out", "w_up", "w_down")
REDUCE_EARLY, REDUCE_LAST = ("w_up", "w_down", "w_out"), ("w_in", "w_q_up", "w_kv_up")


def _cparams(sem):
    return pltpu.CompilerParams(dimension_semantics=sem, vmem_limit_bytes=VMEM_LIMIT_BYTES)


def _pick(n, cands):
    for c in cands:
        if n % c == 0:
            return c
    return n


def _sigmoid(x):
    return 0.5 * (jnp.tanh(0.5 * x) + 1.0)


def _silu(x):
    return x * _sigmoid(x)


@jax.custom_vjp
def _softplus(x):
    u = jnp.exp(-jnp.abs(x))
    w = 1.0 + u
    log1p = jnp.where(w == 1.0, u, jnp.log(w) * (u / jnp.where(w == 1.0, 1.0, w - 1.0)))
    return jnp.maximum(x, 0.0) + log1p


def _softplus_fwd(x):
    return _softplus(x), x


def _softplus_bwd(x, g):
    return (g * _sigmoid(x),)


_softplus.defvjp(_softplus_fwd, _softplus_bwd)


@jax.custom_vjp
def _gelu(x):
    return 0.5 * x * (1.0 + lax.erf(x * (2.0 ** -0.5)))


def _gelu_fwd(x):
    return _gelu(x), x


def _gelu_bwd(x, g):
    cdf = 0.5 * (1.0 + lax.erf(x * (2.0 ** -0.5)))
    pdf = jnp.exp(-0.5 * x * x) * (1.0 / math.sqrt(2.0 * math.pi))
    return (g * (cdf + x * pdf),)


_gelu.defvjp(_gelu_fwd, _gelu_bwd)


def _rms(x, w):
    return x * lax.rsqrt(jnp.mean(x * x, axis=-1, keepdims=True) + EPS) * w


def _shift_rows_raw(x, off, seg):
    n = x.shape[0]
    if off == 0:
        return x
    r = pltpu.roll(x, (-off) % n, 0)
    idx = lax.broadcasted_iota(jnp.int32, x.shape, 0)
    src = idx + off
    ok = (src >= 0) & (src < n)
    if seg:
        ok = ok & ((idx < seg) == (src < seg))
    return jnp.where(ok, r, 0.0)


@functools.partial(jax.custom_vjp, nondiff_argnums=(1, 2))
def _shift_rows(x, off, seg):
    return _shift_rows_raw(x, off, seg)


def _shift_rows_fwd(x, off, seg):
    return _shift_rows_raw(x, off, seg), None


def _shift_rows_bwd(off, seg, _, g):
    return (_shift_rows_raw(g, -off, seg),)


_shift_rows.defvjp(_shift_rows_fwd, _shift_rows_bwd)


@functools.partial(jax.custom_vjp, nondiff_argnums=(1,))
def _roll_lanes(x, shift):
    return pltpu.roll(x, shift % x.shape[1], 1)


def _roll_lanes_fwd(x, shift):
    return _roll_lanes(x, shift), None


def _roll_lanes_bwd(shift, _, g):
    return (pltpu.roll(g, (-shift) % g.shape[1], 1),)


_roll_lanes.defvjp(_roll_lanes_fwd, _roll_lanes_bwd)


def _row_of(w, k):
    sel = lax.broadcasted_iota(jnp.int32, (w.shape[0], 1), 0) == k
    return jnp.sum(jnp.where(sel, w, 0.0), axis=0, keepdims=True)


def _col_of(w, k):
    sel = lax.broadcasted_iota(jnp.int32, (1, w.shape[1]), 1) == k
    return jnp.sum(jnp.where(sel, w, 0.0), axis=1, keepdims=True)


def _dwconv(x, w, seg):
    k = w.shape[0]
    acc = None
    for t in range(k):
        term = _shift_rows(x, t - k // 2, seg) * _row_of(w, t)
        acc = term if acc is None else acc + term
    return acc


def _dwconv_back(x, dy, w, seg):
    k = w.shape[0]
    tap = lax.broadcasted_iota(jnp.int32, (k, 1), 0)
    dx, dw = None, jnp.zeros_like(w)
    for t in range(k):
        back = _shift_rows_raw(dy, k // 2 - t, seg)
        term = back * _row_of(w, t)
        dx = term if dx is None else dx + term
        dw = dw + jnp.where(tap == t, jnp.sum(x * back, axis=0, keepdims=True), 0.0)
    return dx, dw


def _dot(a, b, dims):
    return lax.dot_general(a.astype(MXU_DTYPE), b.astype(MXU_DTYPE), (dims, ((), ())),
                           preferred_element_type=F32)


def _dot_exact(a, b):
    return lax.dot_general(a, b, (((1,), (0,)), ((), ())), precision=HIGHEST,
                           preferred_element_type=F32)


def _mask_dot_raw(mask, x, mask_left):
    hi = x.astype(jnp.bfloat16)
    rest = x - hi.astype(F32)
    mid = rest.astype(jnp.bfloat16)
    low = (rest - mid.astype(F32)).astype(jnp.bfloat16)
    m = mask.astype(jnp.bfloat16)
    acc = None
    for piece in (hi, mid, low):
        term = (lax.dot_general(m, piece, (((1,), (0,)), ((), ())), preferred_element_type=F32) if mask_left
                else lax.dot_general(piece, m, (((1,), (0,)), ((), ())), preferred_element_type=F32))
        acc = term if acc is None else acc + term
    return acc


@functools.partial(jax.custom_vjp, nondiff_argnums=(3,))
def _mask_dot(mask, mask_t, x, mask_left):
    return _mask_dot_raw(mask, x, mask_left)


def _mask_dot_fwd(mask, mask_t, x, mask_left):
    return _mask_dot_raw(mask, x, mask_left), (mask, mask_t)


def _mask_dot_bwd(mask_left, res, g):
    mask, mask_t = res
    return jnp.zeros_like(mask), jnp.zeros_like(mask_t), _mask_dot_raw(mask_t, g, mask_left)


_mask_dot.defvjp(_mask_dot_fwd, _mask_dot_bwd)


MM_VMEM_BUDGET = 40 * 1024 * 1024
MM_STEP_BYTES = 1 << 20
MM_TILES = (2816, 2048, 1536, 1408, 1024, 512, 384, 256, 128)


def _mm_tiles(m, n, kdim, a_bytes, b_bytes, out_bytes):
    tk = kdim if kdim <= 2048 else _pick(kdim, (2048, 1664, 1536, 1408, 1024, 512, 256, 128))
    nk = kdim // tk
    best = None
    for tm in [c for c in MM_TILES if c <= m and m % c == 0] or [m]:
        for tn in [c for c in MM_TILES if c <= n and n % c == 0] or [n]:
            casts = 2 * ((tm * tk if a_bytes != 2 else 0) + (tk * tn if b_bytes != 2 else 0))
            vmem = (2 * (tm * tk * a_bytes + tk * tn * b_bytes) + 2 * tm * tn * out_bytes
                    + tm * tn * 4 * (2 if nk > 1 else 1) + casts)
            if vmem > MM_VMEM_BUDGET:
                continue
            a_reads = 1 if nk == 1 else n // tn
            b_reads = 1 if (nk == 1 and n == tn) else m // tm
            cost = (m * kdim * a_bytes * a_reads + kdim * n * b_bytes * b_reads
                    + (m // tm) * (n // tn) * nk * MM_STEP_BYTES)
            if best is None or cost < best[0]:
                best = (cost, tm, tn)
    assert best is not None, (m, n, kdim)
    return best[1], best[2], tk


def _mm(a, b, *, ta=False, tb=False, out_dtype=F32, name):
    if ta:
        kdim, m = a.shape
    else:
        m, kdim = a.shape
    if tb:
        n, k2 = b.shape
    else:
        k2, n = b.shape
    assert kdim == k2, (a.shape, b.shape, ta, tb)
    tm, tn, tk = _mm_tiles(m, n, kdim, a.dtype.itemsize, b.dtype.itemsize, jnp.dtype(out_dtype).itemsize)
    nk = kdim // tk
    a_spec = pl.BlockSpec((tk, tm), lambda i, j, k: (k, i)) if ta else pl.BlockSpec((tm, tk), lambda i, j, k: (i, k))
    b_spec = pl.BlockSpec((tn, tk), lambda i, j, k: (j, k)) if tb else pl.BlockSpec((tk, tn), lambda i, j, k: (k, j))
    dims = ((0,) if ta else (1,), (1,) if tb else (0,))

    def body(a_ref, b_ref, o_ref, *scratch):
        if nk == 1:
            o_ref[...] = _dot(a_ref[...], b_ref[...], dims).astype(o_ref.dtype)
            return
        acc_ref, = scratch
        k = pl.program_id(2)

        @pl.when(k == 0)
        def _():
            acc_ref[...] = jnp.zeros_like(acc_ref)

        acc_ref[...] += _dot(a_ref[...], b_ref[...], dims)

        @pl.when(k == nk - 1)
        def _():
            o_ref[...] = acc_ref[...].astype(o_ref.dtype)

    return pl.pallas_call(
        body, name=name, grid=(m // tm, n // tn, nk),
        in_specs=[a_spec, b_spec], out_specs=pl.BlockSpec((tm, tn), lambda i, j, k: (i, j)),
        out_shape=jax.ShapeDtypeStruct((m, n), out_dtype),
        scratch_shapes=[pltpu.VMEM((tm, tn), F32)] if nk > 1 else [],
        compiler_params=_cparams(("parallel", "parallel", "arbitrary")),
    )(a, b)


def _mm_sum(pairs, *, tb=False, out_dtype=F32, name):
    m = pairs[0][0].shape[0]
    n = pairs[0][1].shape[0] if tb else pairs[0][1].shape[1]
    tm = _pick(m, (1024, 1408, 512, 384, 256, 128))
    tn = n if n == 1024 else _pick(n, (512, 1408, 384, 256, 128))
    specs, args = [], []
    for a, b in pairs:
        kdim = a.shape[1]
        specs.append(pl.BlockSpec((tm, kdim), lambda i, j: (i, 0)))
        specs.append(pl.BlockSpec((tn, kdim), lambda i, j: (j, 0)) if tb else pl.BlockSpec((kdim, tn), lambda i, j: (0, j)))
        args += [a, b]
    dims = ((1,), (1,) if tb else (0,))

    def body(*refs):
        acc = None
        for t in range(len(pairs)):
            term = _dot(refs[2 * t][...], refs[2 * t + 1][...], dims)
            acc = term if acc is None else acc + term
        refs[-1][...] = acc.astype(refs[-1].dtype)

    return pl.pallas_call(
        body, name=name, grid=(m // tm, n // tn), in_specs=specs,
        out_specs=pl.BlockSpec((tm, tn), lambda i, j: (i, j)), out_shape=jax.ShapeDtypeStruct((m, n), out_dtype),
        compiler_params=_cparams(("parallel", "parallel")),
    )(*args)


def _row_specs(toks, poss, vecs, bvecs, tl, nctx, nb):
    specs, args = [], []
    for arr, off, cw, ci in toks:
        cw = arr.shape[2] if cw is None else cw
        specs.append(pl.BlockSpec((1, tl, cw), lambda b, l, off=off, ci=ci: (b, l + off, ci)))
        args.append(arr)
    for arr in poss:
        specs.append(pl.BlockSpec((1, tl, arr.shape[2]), lambda b, l: (0, l, 0)))
        args.append(arr)
    for arr in vecs:
        specs.append(pl.BlockSpec(arr.shape, lambda b, l: (0, 0)))
        args.append(arr)
    for arr in bvecs:
        if nctx:
            specs.append(pl.BlockSpec((1, 1, arr.shape[2]), lambda b, l: (jnp.where(l < nctx, nb, b), 0, 0)))
        else:
            specs.append(pl.BlockSpec((1, 1, arr.shape[2]), lambda b, l: (b, 0, 0)))
        args.append(arr)
    return specs, args


def _row_fwd(fn, *, toks, poss=(), vecs=(), bvecs=(), outs, nb, nl, tl, nctx=0, name):
    nt, npos, nv, nbv = len(toks), len(poss), len(vecs), len(bvecs)
    specs, args = _row_specs(toks, poss, vecs, bvecs, tl, nctx, nb)

    def body(*refs):
        ins, os = refs[:len(specs)], refs[len(specs):]
        tv = [r[0].astype(F32) for r in ins[:nt]]
        pv = [r[0] for r in ins[nt:nt + npos]]
        vv = [r[...] for r in ins[nt + npos:nt + npos + nv]]
        bv = [r[0] for r in ins[nt + npos + nv:]]
        res = fn(*tv, *pv, *vv, *bv)
        for o, r in zip(os, res):
            o[0] = r.astype(o.dtype)

    return pl.pallas_call(
        body, name=name, grid=(nb, nl // tl), in_specs=specs,
        out_specs=[pl.BlockSpec((1, tl, c), lambda b, l: (b, l, 0)) for c, _ in outs],
        out_shape=[jax.ShapeDtypeStruct((nb, nl, c), dt) for c, dt in outs],
        compiler_params=_cparams(("parallel", "parallel")),
    )(*args)


def _row_bwd(fn, *, toks, poss=(), vecs=(), bvecs=(), cots, tok_grads, emit=(), nb, nl, tl, nctx=0, name,
             drop_blocks=0):
    nt, npos, nv, nbv = len(toks), len(poss), len(vecs), len(bvecs)
    specs, args = _row_specs(toks, poss, vecs, bvecs, tl, nctx, nb)
    n_in = len(specs)
    cot_slots = []
    for arr, off in cots:
        if arr is None:
            cot_slots.append(None)
            continue
        cot_slots.append((len(specs), off))
        specs.append(pl.BlockSpec((1, tl, arr.shape[2]), lambda b, l, off=off: (b, jnp.maximum(l + off, 0), 0)))
        args.append(arr)
    n_all_in = len(specs)

    out_specs, out_shapes = [], []
    tok_out = []
    for (arr, off, cw, ci), dt in zip(toks, tok_grads):
        if dt is None:
            tok_out.append(None)
            continue
        cw = arr.shape[2] if cw is None else cw
        tok_out.append(len(out_specs))
        out_specs.append(pl.BlockSpec((1, tl, cw), lambda b, l: (b, jnp.maximum(l - drop_blocks, 0), 0)))
        out_shapes.append(jax.ShapeDtypeStruct((nb, nl - drop_blocks * tl, cw), dt))
    vec_out = []
    for arr in vecs:
        vec_out.append(len(out_specs))
        out_specs.append(pl.BlockSpec(arr.shape, lambda b, l: (0, 0)))
        out_shapes.append(jax.ShapeDtypeStruct(arr.shape, F32))
    bv_out = []
    for arr in bvecs:
        c = arr.shape[2]
        lat = len(out_specs)
        out_specs.append(pl.BlockSpec((1, 1, c), lambda b, l: (b, 0, 0)))
        out_shapes.append(jax.ShapeDtypeStruct((nb, 1, c), F32))
        ctx = None
        if nctx:
            ctx = len(out_specs)
            out_specs.append(pl.BlockSpec((1, 1, c), lambda b, l: (0, 0, 0)))
            out_shapes.append(jax.ShapeDtypeStruct((1, 1, c), F32))
        bv_out.append((lat, ctx))
    emit_out = []
    emit_cols = {}
    for idx, c, dt in emit:
        emit_out.append((idx, len(out_specs)))
        out_specs.append(pl.BlockSpec((1, tl, c), lambda b, l: (b, l, 0)))
        out_shapes.append(jax.ShapeDtypeStruct((nb, nl, c), dt))

    def body(*refs):
        ins, os = refs[:n_all_in], refs[n_all_in:]
        b, l = pl.program_id(0), pl.program_id(1)
        tv = [r[0].astype(F32) for r in ins[:nt]]
        pv = [r[0] for r in ins[nt:nt + npos]]
        vv = [r[...] for r in ins[nt + npos:nt + npos + nv]]
        bv = [r[0] for r in ins[nt + npos + nv:n_in]]

        def f(*d):
            return tuple(fn(*d[:nt], *pv, *d[nt:]))

        res, vjp = jax.vjp(f, *tv, *vv, *bv)
        cts = []
        for r, slot in zip(res, cot_slots):
            if slot is None:
                cts.append(jnp.zeros_like(r))
            else:
                i, off = slot
                ct = ins[i][0].astype(F32)
                if off < 0:
                    ct = jnp.where(l + off >= 0, ct, 0.0)
                cts.append(ct)
        grads = vjp(tuple(cts))

        for g, slot in zip(grads[:nt], tok_out):
            if slot is not None:
                os[slot][0] = g.astype(os[slot].dtype)

        @pl.when((b == 0) & (l == 0))
        def _():
            for slot in vec_out:
                os[slot][...] = jnp.zeros_like(os[slot])
            for _, ctx in bv_out:
                if ctx is not None:
                    os[ctx][...] = jnp.zeros_like(os[ctx])

        @pl.when(l == 0)
        def _():
            for lat, _ in bv_out:
                os[lat][...] = jnp.zeros_like(os[lat])

        for g, slot in zip(grads[nt:nt + nv], vec_out):
            os[slot][...] += g
        for g, (lat, ctx) in zip(grads[nt + nv:], bv_out):
            if ctx is None:
                os[lat][0] += g
            else:
                is_ctx = l < nctx
                os[lat][0] += jnp.where(is_ctx, 0.0, g)
                os[ctx][0] += jnp.where(is_ctx, g, 0.0)
        for idx, slot in emit_out:
            os[slot][0] = res[idx].astype(os[slot].dtype)

    out = pl.pallas_call(
        body, name=name, grid=(nb, nl // tl), in_specs=specs, out_specs=out_specs, out_shape=out_shapes,
        compiler_params=_cparams(("arbitrary", "arbitrary")),
    )(*args)
    tg = [None if s is None else out[s] for s in tok_out]
    vg = [out[s] for s in vec_out]
    bg = [(out[lat], None if ctx is None else out[ctx]) for lat, ctx in bv_out]
    em = [out[s] for _, s in emit_out]
    return tg, vg, bg, em


def _seq_specs(toks, vecs, nl, cb):
    specs, args = [], []
    for arr, off, mult in toks:
        specs.append(pl.BlockSpec((1, nl, cb * mult), lambda j, b, off=off: (b, 0, j + off)))
        args.append(arr)
    for arr, off in vecs:
        specs.append(pl.BlockSpec((arr.shape[0], cb), lambda j, b, off=off: (0, j + off)))
        args.append(arr)
    return specs, args


def _seq_fwd(fn, *, toks, vecs, outs, nb, nl, nc, cb, name):
    nt = len(toks)
    specs, args = _seq_specs(toks, vecs, nl, cb)

    def body(*refs):
        ins, os = refs[:len(specs)], refs[len(specs):]
        tv = [r[0].astype(F32) for r in ins[:nt]]
        vv = [r[...] for r in ins[nt:]]
        for o, r in zip(os, fn(*tv, *vv)):
            o[0] = r.astype(o.dtype)

    return pl.pallas_call(
        body, name=name, grid=(nc // cb, nb), in_specs=specs,
        out_specs=[pl.BlockSpec((1, nl, cb), lambda j, b: (b, 0, j)) for _ in outs],
        out_shape=[jax.ShapeDtypeStruct((nb, nl, nc), dt) for dt in outs],
        compiler_params=_cparams(("parallel", "parallel")),
    )(*args)


def _seq_bwd(fn, *, toks, vecs, cots, tok_grads, nb, nl, nc, cb, name, bwd_fn=None):
    nt, nv = len(toks), len(vecs)
    specs, args = _seq_specs(toks, vecs, nl, cb)
    n_in = len(specs)
    cot_counts = [len(group) for group in cots]
    for group in cots:
        for arr in group:
            specs.append(pl.BlockSpec((1, nl, cb), lambda j, b: (b, 0, j)))
            args.append(arr)
    out_specs, out_shapes = [], []
    for (_, _, mult), dt in zip(toks, tok_grads):
        out_specs.append(pl.BlockSpec((1, nl, cb * mult), lambda j, b: (b, 0, j)))
        out_shapes.append(jax.ShapeDtypeStruct((nb, nl, nc * mult), dt))
    for arr, _ in vecs:
        out_specs.append(pl.BlockSpec((arr.shape[0], cb), lambda j, b: (0, j)))
        out_shapes.append(jax.ShapeDtypeStruct((arr.shape[0], nc), F32))

    def body(*refs):
        ins, os = refs[:len(specs)], refs[len(specs):]
        b = pl.program_id(1)
        tv = [r[0].astype(F32) for r in ins[:nt]]
        vv = [r[...] for r in ins[nt:n_in]]
        cts, o = [], n_in
        for cnt in cot_counts:
            ct = ins[o][0].astype(F32)
            for r in ins[o + 1:o + cnt]:
                ct = ct + r[0].astype(F32)
            cts.append(ct)
            o += cnt
        if bwd_fn is None:
            _, vjp = jax.vjp(lambda *d: tuple(fn(*d)), *tv, *vv)
            grads = vjp(tuple(cts))
        else:
            grads = bwd_fn(*tv, *vv, *cts)
        for g, o in zip(grads[:nt], os[:nt]):
            o[0] = g.astype(o.dtype)

        @pl.when(b == 0)
        def _():
            for o in os[nt:]:
                o[...] = jnp.zeros_like(o)

        for g, o in zip(grads[nt:], os[nt:]):
            o[...] += g

    out = pl.pallas_call(
        body, name=name, grid=(nc // cb, nb), in_specs=specs, out_specs=out_specs, out_shape=out_shapes,
        compiler_params=_cparams(("parallel", "arbitrary")),
    )(*args)
    return out[:nt], out[nt:]


EXP2_SCALE = ATTN_SCALE * math.log2(math.e)


HEAD_TILE = 128
N_HEAD_PAIRS = N_ATTN_HEADS // 2


def _head_lanes():
    lane = lax.broadcasted_iota(jnp.int32, (1, HEAD_TILE), 1)
    return lane < QK_NOPE_DIM, (lane >= QK_NOPE_DIM) & (lane < QK_DIM)


def _attn_specs(tq, lk):
    q = pl.BlockSpec((1, tq, 2 * HEAD_TILE), lambda b, pr, j: (b, j, pr))
    kv = pl.BlockSpec((1, lk, 2 * HEAD_TILE), lambda b, pr, j: (b, 0, pr))
    kr = pl.BlockSpec((1, lk, HEAD_TILE), lambda b, pr, j: (b, 0, 0))
    o = pl.BlockSpec((1, tq, HEAD_TILE), lambda b, pr, j: (b, j, pr))
    lse = pl.BlockSpec((1, 2, tq, 1), lambda b, pr, j: (b, pr, j, 0))
    return q, kv, kr, o, lse


def _grid_marks(nb, nj):
    b, pr, j = pl.program_id(0), pl.program_id(1), pl.program_id(2)
    first = (b == 0) & (pr == 0) & (j == 0)
    middle = (b == nb // 2) & (pr == 0) & (j == 0)
    last = (b == nb - 1) & (pr == N_HEAD_PAIRS - 1) & (j == nj - 1)
    return first, middle, last


def _attn_fwd(q, kv, kr, late_shard, *, tq, name):
    nb, s, _ = q.shape
    lk = kv.shape[1]
    nj = s // tq
    qs, kvs, krs, os_, lses = _attn_specs(tq, lk)

    def body(q_ref, kv_ref, kr_ref, shard_ref, o_ref, lse_ref, gathered_ref, send_sems, recv_sems):
        start, relay, finish = _gather_stage(shard_ref, gathered_ref, send_sems, recv_sems)
        first, middle, last = _grid_marks(nb, nj)
        pl.when(first)(start)
        pl.when(middle)(relay)
        low, _ = _head_lanes()
        outs = []
        for e in range(2):
            tile = pl.ds(HEAD_TILE * e, HEAD_TILE)
            kv_e = kv_ref[0, :, tile]
            keys = jnp.where(low, kv_e, kr_ref[0])
            sc = _dot(q_ref[0, :, tile], keys, ((1,), (1,)))
            m = jnp.max(sc, axis=-1, keepdims=True)
            p = jnp.exp2((sc - m) * EXP2_SCALE)
            denom = jnp.sum(p, axis=-1, keepdims=True)
            outs.append(_dot(p, kv_e, ((1,), (0,))) / denom)
            lse_ref[0, e] = m * EXP2_SCALE + jnp.log2(denom)
        o_ref[0] = jnp.where(low, pltpu.roll(outs[0], V_HEAD_DIM, 1), outs[1])
        pl.when(last)(finish)

    return pl.pallas_call(
        body, name=name, grid=(nb, N_HEAD_PAIRS, nj), in_specs=[qs, kvs, krs, ANY], out_specs=[os_, lses, ANY],
        out_shape=[jax.ShapeDtypeStruct((nb, s, ATTN_WIDTH), F32), jax.ShapeDtypeStruct((nb, N_ATTN_HEADS, s, 1), F32),
                   jax.ShapeDtypeStruct((N_CHIPS,) + late_shard.shape, late_shard.dtype)],
        scratch_shapes=GATHER_SEMS,
        compiler_params=_cparams(("arbitrary", "arbitrary", "arbitrary")),
    )(q, kv, kr, late_shard)


def _attn_bwd(q, kv, kr, o, lse, do, early_sums, *, tq, name):
    nb, s, _ = q.shape
    lk = kv.shape[1]
    nj = s // tq
    qs, kvs, krs, os_, lses = _attn_specs(tq, lk)

    def body(q_ref, kv_ref, kr_ref, o_ref, lse_ref, do_ref, sums_ref, dq_ref, dkv_ref, dkr_ref, scattered_ref,
             send_sems, recv_sems):
        start, finish = _scatter_stage(sums_ref, scattered_ref, send_sems, recv_sems)
        first, _, last = _grid_marks(nb, nj)
        pl.when(first)(start)
        pr, j = pl.program_id(1), pl.program_id(2)
        low, rope = _head_lanes()
        do_pair = do_ref[0]
        prod = do_pair * o_ref[0]

        @pl.when(j == 0)
        def _():
            dkv_ref[...] = jnp.zeros_like(dkv_ref)

        @pl.when((pr == 0) & (j == 0))
        def _():
            dkr_ref[...] = jnp.zeros_like(dkr_ref)

        dkr = None
        for e in range(2):
            tile = pl.ds(HEAD_TILE * e, HEAD_TILE)
            delta = jnp.sum(jnp.where(low if e == 0 else ~low, prod, 0.0), axis=-1, keepdims=True)
            do_e = jnp.where(low, 0.0, do_pair if e == 1 else pltpu.roll(do_pair, V_HEAD_DIM, 1))
            kv_e, q_e = kv_ref[0, :, tile], q_ref[0, :, tile]
            keys = jnp.where(low, kv_e, kr_ref[0])
            sc = _dot(q_e, keys, ((1,), (1,)))
            p = jnp.exp2(sc * EXP2_SCALE - lse_ref[0, e])
            dp = _dot(do_e, kv_e, ((1,), (1,)))
            ds = (p * (dp - delta)).astype(MXU_DTYPE)
            dq_ref[0, :, tile] = _dot(ds, keys, ((1,), (0,))) * ATTN_SCALE
            dkeys = _dot(ds, q_e, ((0,), (0,)))
            dv = _dot(p, do_e, ((0,), (0,)))
            dkv_ref[0, :, tile] += jnp.where(low, dkeys, dv)
            part = jnp.where(rope, dkeys, 0.0)
            dkr = part if dkr is None else dkr + part
        dkr_ref[0] += dkr

        @pl.when(j == nj - 1)
        def _():
            for e in range(2):
                tile = pl.ds(HEAD_TILE * e, HEAD_TILE)
                dkv_ref[0, :, tile] = dkv_ref[0, :, tile] * jnp.where(low, ATTN_SCALE, 1.0)

        @pl.when((pr == N_HEAD_PAIRS - 1) & (j == nj - 1))
        def _():
            dkr_ref[0] = dkr_ref[0] * ATTN_SCALE

        pl.when(last)(finish)

    return pl.pallas_call(
        body, name=name, grid=(nb, N_HEAD_PAIRS, nj), in_specs=[qs, kvs, krs, os_, lses, os_, ANY],
        out_specs=[qs, kvs, krs, ANY],
        out_shape=[jax.ShapeDtypeStruct(q.shape, F32), jax.ShapeDtypeStruct(kv.shape, F32),
                   jax.ShapeDtypeStruct(kr.shape, F32), jax.ShapeDtypeStruct(early_sums.shape, early_sums.dtype)],
        scratch_shapes=SCATTER_SEMS,
        compiler_params=_cparams(("arbitrary", "arbitrary", "arbitrary")),
    )(q, kv, kr, o, lse, do, early_sums)


N_PAIRS = HEADS_PER_GROUP // 2
PAIR_W = 2 * SSD_HEAD_DIM


def _ssd_chunk(states, xs, dtc, dtr, bm, cm, ac, ar, *, reverse):
    q = dtc.shape[0]
    assert q == PAIR_W == dtr.shape[1]
    row = lax.broadcasted_iota(jnp.int32, (q, q), 0)
    col = lax.broadcasted_iota(jnp.int32, (q, q), 1)
    if reverse:
        tri_c, tri_r, mask = col < row, row < col, col >= row
    else:
        tri_c, tri_r, mask = col <= row, row <= col, col <= row
    a_col, a_row = dtc * ac, dtr * ar
    tri_c, tri_r = tri_c.astype(F32), tri_r.astype(F32)
    cum_c = _mask_dot(tri_c, tri_r, a_col, True)
    cum_r = _mask_dot(tri_r, tri_c, a_row, False)
    tot_c = jnp.sum(a_col, axis=0, keepdims=True)
    tot_r = jnp.sum(a_row, axis=1, keepdims=True)
    cb = _dot(cm, bm, ((1,), (1,)))
    first = lax.broadcasted_iota(jnp.int32, (1, PAIR_W), 1) < SSD_HEAD_DIM
    first_rows = lax.broadcasted_iota(jnp.int32, (PAIR_W, 1), 0) < SSD_HEAD_DIM
    heads, pairs = range(HEADS_PER_GROUP), range(N_PAIRS)
    tile = [slice(PAIR_W * pr, PAIR_W * (pr + 1)) for pr in pairs]
    cum_p = [cum_c[:, tile[pr]] for pr in pairs]
    swapped = [_roll_lanes(cum_p[pr], SSD_HEAD_DIM) for pr in pairs]
    cc = [jnp.where(first, cum_p[e // 2], swapped[e // 2]) if e % 2 == 0
          else jnp.where(first, swapped[e // 2], cum_p[e // 2]) for e in heads]
    cr = [_row_of(cum_r, e) for e in heads]
    if reverse:
        within = [jnp.exp(jnp.where(mask, cr[e] - cc[e], -jnp.inf)) for e in heads]
        into = [jnp.exp(tot_c[:, tile[pr]] - cum_p[pr]) for pr in pairs]
        to_end = [jnp.exp(cum_p[pr]) for pr in pairs]
    else:
        within = [jnp.exp(jnp.where(mask, cc[e] - cr[e], -jnp.inf)) for e in heads]
        into = [jnp.exp(cum_p[pr]) for pr in pairs]
        to_end = [jnp.exp(tot_c[:, tile[pr]] - cum_p[pr]) for pr in pairs]
    decay = [cb * within[e] for e in heads]
    carry = [jnp.exp(_row_of(tot_r, e)) for e in heads]
    xd = [xs[pr] * dtc[:, tile[pr]] for pr in pairs]
    y_even = [_dot(decay[2 * pr], jnp.where(first, xd[pr], 0.0), ((1,), (0,))) for pr in pairs]
    y_odd = [_dot(decay[2 * pr + 1], jnp.where(first, 0.0, xd[pr]), ((1,), (0,))) for pr in pairs]
    y_off = [_dot(cm, states[pr], ((1,), (1,))) for pr in pairs]
    grow = [_dot(xd[pr] * to_end[pr], bm, ((0,), (0,))) for pr in pairs]
    ys = [y_even[pr] + y_odd[pr] + y_off[pr] * into[pr] for pr in pairs]
    new_states = [states[pr] * jnp.where(first_rows, carry[2 * pr], carry[2 * pr + 1]) + grow[pr] for pr in pairs]
    return tuple(ys) + tuple(new_states)


def _chunk_of_step(t, ncc, nch, reverse):
    if not reverse:
        return t
    return jnp.where(t < ncc, ncc - 1 - t, nch - 1 - (t - ncc))


X_COLS = D_INNER // SSD_GROUPS
GROUP_COLS = X_COLS + 2 * SSD_STATE


def _scan_in_specs(nch, ncc, reverse, back):
    q, e = SSD_CHUNK, HEADS_PER_GROUP

    def ch(t):
        return _chunk_of_step((nch - 1 - t) if back else t, ncc, nch, reverse)

    return ch, [
        pl.BlockSpec((1, q, GROUP_COLS), lambda b, g, t: (b, ch(t), g)),
        pl.BlockSpec((1, q, X_COLS), lambda b, g, t: (b, ch(t), g)),
        pl.BlockSpec((1, 1, e, q), lambda b, g, t: (b, g, 0, ch(t))),
        pl.BlockSpec((1, 1, X_COLS), lambda b, g, t: (g, 0, 0)),
        pl.BlockSpec((1, e, 1), lambda b, g, t: (g, 0, 0)),
        pl.BlockSpec((1, 1, X_COLS), lambda b, g, t: (g, 0, 0)),
    ]


def _scan_chunk_fn(reverse, skip):
    def f(states, xs, dtc, dtr, bm, cm, ac, ar, d):
        res = _ssd_chunk(states, xs, dtc, dtr, bm, cm, ac, ar, reverse=reverse)
        if not skip:
            return res
        ys = tuple(res[i] + d[:, PAIR_W * i:PAIR_W * (i + 1)] * xs[i] for i in range(N_PAIRS))
        return ys + tuple(res[N_PAIRS:])

    return f


def _scan_operands(x_ref, dtc_ref, dtr_ref, ac_ref, ar_ref, d_ref):
    xs = [x_ref[0, :, pl.ds(PAIR_W * i, PAIR_W)] for i in range(N_PAIRS)]
    bm = x_ref[0, :, pl.ds(X_COLS, SSD_STATE)]
    cm = x_ref[0, :, pl.ds(X_COLS + SSD_STATE, SSD_STATE)]
    return xs, dtc_ref[0], dtr_ref[0, 0], bm, cm, ac_ref[0], ar_ref[0], d_ref[0]


N_IN = 6


def _scan_fwd(dirs, *, ncc, name):
    nb, lt, _ = dirs[0][0].shape
    q, n = SSD_CHUNK, SSD_STATE
    nch = lt // q
    in_specs, out_specs, out_shapes, fs = [], [], [], []
    for dr in range(2):
        ch, specs = _scan_in_specs(nch, ncc, bool(dr), False)
        in_specs += specs
        fs.append(_scan_chunk_fn(bool(dr), dr == 0))
        out_specs += [pl.BlockSpec((1, q, X_COLS), lambda b, g, t, ch=ch: (b, ch(t), g)),
                      pl.BlockSpec((1, 1, 1, N_PAIRS, PAIR_W, n), lambda b, g, t: (b, g, t, 0, 0, 0))]
        out_shapes += [jax.ShapeDtypeStruct((nb, lt, D_INNER), F32),
                       jax.ShapeDtypeStruct((nb, SSD_GROUPS, nch, N_PAIRS, PAIR_W, n), F32)]

    def body(*refs):
        ins, outs, sts = refs[:2 * N_IN], refs[2 * N_IN:2 * N_IN + 4], refs[2 * N_IN + 4:]
        t = pl.program_id(2)

        @pl.when(t == 0)
        def _():
            for st_ref in sts:
                st_ref[...] = jnp.zeros_like(st_ref)

        entering = [[sts[dr][i] for i in range(N_PAIRS)] for dr in range(2)]
        results = [fs[dr](entering[dr], *_scan_operands(*ins[N_IN * dr:N_IN * (dr + 1)])) for dr in range(2)]
        for dr in range(2):
            (y_ref, ent_ref), st_ref = outs[2 * dr:2 * dr + 2], sts[dr]
            for i in range(N_PAIRS):
                ent_ref[0, 0, 0, i] = entering[dr][i]
                y_ref[0, :, pl.ds(PAIR_W * i, PAIR_W)] = results[dr][i]
                st_ref[i] = results[dr][N_PAIRS + i]

    out = pl.pallas_call(
        body, name=name, grid=(nb, SSD_GROUPS, nch), in_specs=in_specs, out_specs=out_specs, out_shape=out_shapes,
        scratch_shapes=[pltpu.VMEM((N_PAIRS, PAIR_W, n), F32)] * 2,
        compiler_params=_cparams(("parallel", "parallel", "arbitrary")),
    )(*dirs[0], *dirs[1])
    return out[:2], out[2:]


N_SCAN_GRADS = 6


def _scan_bwd(dirs, entering, dy, *, ncc, name):
    nb, lt, _ = dirs[0][0].shape
    q, n, e = SSD_CHUNK, SSD_STATE, HEADS_PER_GROUP
    nch = lt // q
    in_specs, out_specs, out_shapes, fs, args = [], [], [], [], []
    for dr in range(2):
        ch, specs = _scan_in_specs(nch, ncc, bool(dr), True)
        in_specs += specs + [
            pl.BlockSpec((1, 1, 1, N_PAIRS, PAIR_W, n), lambda b, g, t: (b, g, nch - 1 - t, 0, 0, 0)),
            pl.BlockSpec((1, q, X_COLS), lambda b, g, t, ch=ch: (b, ch(t), g))]
        args += list(dirs[dr]) + [entering[dr], dy]
        fs.append(_scan_chunk_fn(bool(dr), dr == 0))
        out_specs += [pl.BlockSpec((1, q, GROUP_COLS), lambda b, g, t, ch=ch: (b, ch(t), g)),
                      pl.BlockSpec((1, q, X_COLS), lambda b, g, t, ch=ch: (b, ch(t), g)),
                      pl.BlockSpec((1, 1, e, q), lambda b, g, t, ch=ch: (b, g, 0, ch(t))),
                      pl.BlockSpec((1, 1, 1, X_COLS), lambda b, g, t: (b, g, 0, 0)),
                      pl.BlockSpec((1, 1, e, 1), lambda b, g, t: (b, g, 0, 0)),
                      pl.BlockSpec((1, 1, 1, X_COLS), lambda b, g, t: (b, g, 0, 0))]
        out_shapes += [jax.ShapeDtypeStruct((nb, lt, SSD_GROUPS * GROUP_COLS), F32),
                       jax.ShapeDtypeStruct((nb, lt, D_INNER), F32), jax.ShapeDtypeStruct((nb, SSD_GROUPS, e, lt), F32),
                       jax.ShapeDtypeStruct((nb, SSD_GROUPS, 1, X_COLS), F32), jax.ShapeDtypeStruct((nb, SSD_GROUPS, e, 1), F32),
                       jax.ShapeDtypeStruct((nb, SSD_GROUPS, 1, X_COLS), F32)]
    n_in = N_IN + 2

    def body(*refs):
        ins = refs[:2 * n_in]
        outs = refs[2 * n_in:2 * n_in + 2 * N_SCAN_GRADS]
        dss = refs[2 * n_in + 2 * N_SCAN_GRADS:]
        t = pl.program_id(2)

        for dr in range(2):
            mine = ins[n_in * dr:n_in * (dr + 1)]
            ent_ref, dy_ref = mine[N_IN], mine[N_IN + 1]
            dx_ref, ddtc_ref, ddtr_ref, dac_ref, dar_ref, dd_ref = outs[N_SCAN_GRADS * dr:N_SCAN_GRADS * (dr + 1)]
            ds_ref = dss[dr]

            @pl.when(t == 0)
            def _():
                for ref in (ds_ref, dac_ref, dar_ref, dd_ref):
                    ref[...] = jnp.zeros_like(ref)

            states = [ent_ref[0, 0, 0, i] for i in range(N_PAIRS)]
            _, vjp = jax.vjp(fs[dr], states, *_scan_operands(*mine[:N_IN]))
            dys = [dy_ref[0, :, pl.ds(PAIR_W * i, PAIR_W)] for i in range(N_PAIRS)]
            gs, gx, gdtc, gdtr, gb, gc, gac, gar, gd = vjp(tuple(dys) + tuple(ds_ref[i] for i in range(N_PAIRS)))
            o = 0
            for part in list(gx) + [gb, gc]:
                dx_ref[0, :, pl.ds(o, part.shape[1])] = part
                o += part.shape[1]
            for i in range(N_PAIRS):
                ds_ref[i] = gs[i]
            ddtc_ref[0] = gdtc
            ddtr_ref[0, 0] = gdtr
            dac_ref[0, 0] += gac
            dar_ref[0, 0] += gar
            dd_ref[0, 0] += gd

    out = pl.pallas_call(
        body, name=name, grid=(nb, SSD_GROUPS, nch), in_specs=in_specs, out_specs=out_specs, out_shape=out_shapes,
        scratch_shapes=[pltpu.VMEM((N_PAIRS, PAIR_W, n), F32)] * 2,
        compiler_params=_cparams(("parallel", "parallel", "arbitrary")),
    )(*args)
    return out[:N_SCAN_GRADS], out[N_SCAN_GRADS:]


def _adamw(w, g, m, v, *, name):
    r, c = w.shape
    tr = _pick(r, (256, 176, 128, 96, 64, 8))
    c1 = 1.0 / (1.0 - ADAM_B1 ** ADAM_STEP)
    c2 = 1.0 / (1.0 - ADAM_B2 ** ADAM_STEP)

    def body(w_ref, g_ref, m_ref, v_ref, d_ref, nm_ref, nv_ref):
        gv = g_ref[...]
        nm = ADAM_B1 * m_ref[...] + (1.0 - ADAM_B1) * gv
        nv = ADAM_B2 * v_ref[...] + (1.0 - ADAM_B2) * (gv * gv)
        d_ref[...] = -ADAM_LR * ((nm * c1) / (jnp.sqrt(nv * c2) + ADAM_EPS) + ADAM_WD * w_ref[...])
        nm_ref[...] = nm
        nv_ref[...] = nv

    spec = pl.BlockSpec((tr, c), lambda i: (i, 0))
    return pl.pallas_call(
        body, name=name, grid=(r // tr,), in_specs=[spec] * 4, out_specs=[spec] * 3,
        out_shape=[jax.ShapeDtypeStruct((r, c), F32)] * 3, compiler_params=_cparams(("parallel",)),
    )(w, g, m, v)


def _sum_rows_tile(r):
    return r if r <= 1024 else _pick(r, (656, 512, 256, 128, 64, 32, 16))


def _sum_slots(x, *, out_dtype, name):
    n, r, c = x.shape
    tr = _sum_rows_tile(r)

    def body(x_ref, o_ref):
        acc = x_ref[0].astype(F32)
        for k in range(1, n):
            acc = acc + x_ref[k].astype(F32)
        o_ref[...] = acc.astype(o_ref.dtype)

    return pl.pallas_call(
        body, name=name, grid=(r // tr,), in_specs=[pl.BlockSpec((n, tr, c), lambda i: (0, i, 0))],
        out_specs=pl.BlockSpec((tr, c), lambda i: (i, 0)), out_shape=jax.ShapeDtypeStruct((r, c), out_dtype),
        compiler_params=_cparams(("parallel",)),
    )(x)


def _sum_list(xs, *, out_dtype, name):
    r, c = xs[0].shape
    tr = _sum_rows_tile(r)

    def body(*refs):
        acc = refs[0][...].astype(F32)
        for ref in refs[1:-1]:
            acc = acc + ref[...].astype(F32)
        refs[-1][...] = acc.astype(refs[-1].dtype)

    spec = pl.BlockSpec((tr, c), lambda i: (i, 0))
    return pl.pallas_call(
        body, name=name, grid=(r // tr,), in_specs=[spec] * len(xs), out_specs=spec,
        out_shape=jax.ShapeDtypeStruct((r, c), out_dtype), compiler_params=_cparams(("parallel",)),
    )(*xs)


ANY = pl.BlockSpec(memory_space=pl.ANY)


def _place():
    return lax.axis_index("x"), lax.axis_index("y"), lax.axis_index("c")


def _allgather_small(v, *, name):
    r, c = v.shape

    def body(v_ref, out_ref, send_sems, recv_sems, local_sem):
        x, y, cc = _place()
        me = 4 * x + 2 * y + cc
        mine = pltpu.make_async_copy(v_ref, out_ref.at[me], local_sem)
        mine.start()
        copies = []
        for k in range(1, N_DEV):
            fx, fy, fc = (k >> 2) & 1, (k >> 1) & 1, k & 1
            peer = (1 - x if fx else x, 1 - y if fy else y, 1 - cc if fc else cc)
            copies.append(pltpu.make_async_remote_copy(
                src_ref=v_ref, dst_ref=out_ref.at[me], send_sem=send_sems.at[k - 1], recv_sem=recv_sems.at[k - 1],
                device_id=peer, device_id_type=MESH))
        for cp in copies:
            cp.start()
        for cp in copies:
            cp.wait()
        mine.wait()

    return pl.pallas_call(
        body, name=name, in_specs=[ANY], out_specs=ANY, out_shape=jax.ShapeDtypeStruct((N_DEV, r, c), v.dtype),
        scratch_shapes=[pltpu.SemaphoreType.DMA((N_DEV - 1,)), pltpu.SemaphoreType.DMA((N_DEV - 1,)),
                        pltpu.SemaphoreType.DMA],
    )(v)


def _other_chips(x, y):
    return [(1 - x, y), (x, 1 - y), (1 - x, 1 - y)]


GATHER_SEMS = [pltpu.SemaphoreType.DMA((6,)), pltpu.SemaphoreType.DMA((6,))]
SCATTER_SEMS = [pltpu.SemaphoreType.DMA((3,)), pltpu.SemaphoreType.DMA((3,))]


def _gather_stage(v_ref, out_ref, send_sems, recv_sems):
    half = v_ref.shape[0] // 2
    x, y, cc = _place()
    sibling = (x, y, 1 - cc)
    chips = _other_chips(x, y)

    def rows(px, py, pc):
        return out_ref.at[2 * px + py, pl.ds(pc * half, half), :]

    def copy(k, block, to, src=None):
        return pltpu.make_async_remote_copy(
            src_ref=rows(*block) if src is None else src, dst_ref=rows(*block),
            send_sem=send_sems.at[k], recv_sem=recv_sems.at[k], device_id=to, device_id_type=MESH)

    my_half = v_ref.at[pl.ds(cc * half, half), :]
    first = [copy(j, (x, y, cc), (*chip, cc), src=my_half) for j, chip in enumerate(chips)]
    passed = [copy(3 + j, (*chip, cc), sibling) for j, chip in enumerate(chips)]

    def start():
        for cp in first:
            cp.start()

    def relay():
        for j, chip in enumerate(chips):
            copy(j, (*chip, cc), (x, y, cc)).wait_recv()
            passed[j].start()

    def finish():
        for j, chip in enumerate(chips):
            copy(3 + j, (*chip, 1 - cc), (x, y, cc)).wait_recv()
        for cp in first + passed:
            cp.wait_send()

    return start, relay, finish


def _gather_shards(mine, *, name):
    r, c = mine.shape

    def body(v_ref, out_ref, send_sems, recv_sems):
        for phase in _gather_stage(v_ref, out_ref, send_sems, recv_sems):
            phase()

    return pl.pallas_call(
        body, name=name, in_specs=[ANY], out_specs=ANY, out_shape=jax.ShapeDtypeStruct((N_CHIPS, r, c), mine.dtype),
        scratch_shapes=GATHER_SEMS,
    )(mine)


def _swap_halves(g, *, name):
    n, _, r, c = g.shape

    def body(g_ref, got_ref, send_sems, recv_sems):
        x, y, cc = _place()
        sibling = (x, y, 1 - cc)
        rems = []
        for j in range(n):
            rems.append(pltpu.make_async_remote_copy(
                src_ref=g_ref.at[j, 1 - cc], dst_ref=got_ref.at[j], send_sem=send_sems.at[j],
                recv_sem=recv_sems.at[j], device_id=sibling, device_id_type=MESH))
        for cp in rems:
            cp.start()
        for cp in rems:
            cp.wait()

    return pl.pallas_call(
        body, name=name, in_specs=[ANY], out_specs=ANY, out_shape=jax.ShapeDtypeStruct((n, r, c), g.dtype),
        scratch_shapes=[pltpu.SemaphoreType.DMA((n,)), pltpu.SemaphoreType.DMA((n,))],
    )(g)


def _scatter_stage(s_ref, out_ref, send_sems, recv_sems):
    x, y, cc = _place()
    me = 2 * x + y
    copies = [pltpu.make_async_remote_copy(
        src_ref=s_ref.at[2 * px + py], dst_ref=out_ref.at[me], send_sem=send_sems.at[j], recv_sem=recv_sems.at[j],
        device_id=(px, py, cc), device_id_type=MESH) for j, (px, py) in enumerate(_other_chips(x, y))]

    def start():
        for cp in copies:
            cp.start()

    def finish():
        for cp in copies:
            cp.wait()

    return start, finish


def _scatter_to_chips(s, *, name):
    def body(s_ref, out_ref, send_sems, recv_sems):
        for phase in _scatter_stage(s_ref, out_ref, send_sems, recv_sems):
            phase()

    return pl.pallas_call(
        body, name=name, in_specs=[ANY], out_specs=ANY, out_shape=jax.ShapeDtypeStruct(s.shape, s.dtype),
        scratch_shapes=SCATTER_SEMS,
    )(s)


def _join_halves(f, *, name):
    r, c = f.shape

    def body(f_ref, out_ref, send_sem, recv_sem):
        x, y, cc = _place()
        cp = pltpu.make_async_remote_copy(src_ref=f_ref, dst_ref=out_ref.at[cc], send_sem=send_sem, recv_sem=recv_sem,
                                          device_id=(x, y, 1 - cc), device_id_type=MESH)
        cp.start()
        cp.wait()

    return pl.pallas_call(
        body, name=name, in_specs=[ANY], out_specs=ANY, out_shape=jax.ShapeDtypeStruct((2, r, c), f.dtype),
        scratch_shapes=[pltpu.SemaphoreType.DMA, pltpu.SemaphoreType.DMA],
    )(f)


def _pack_rows(parts, width=PACK_COLS):
    return jnp.concatenate([p.reshape(-1, width) for p in parts], axis=0)


def _pack_small(parts, rows):
    flat = jnp.concatenate([p.reshape(-1).astype(F32) for p in parts])
    return jnp.pad(flat, (0, rows * LANES - flat.shape[0])).reshape(rows, LANES)


def _unpack_small(packed, shapes):
    flat = packed.reshape(-1)
    out, o = [], 0
    for shp in shapes:
        n = int(np.prod(shp))
        out.append(flat[o:o + n].reshape(shp))
        o += n
    return out


def _perm_in_cols(w):
    a, b = Q_LORA_RANK + KV_LORA_RANK, Q_LORA_RANK + KV_LORA_RANK + QK_ROPE_DIM
    c = IN_WIDTH - 2 * N_SSD_HEADS
    return jnp.concatenate([w[:, :a], w[:, b:c], w[:, a:b], w[:, c:]], axis=1)


def _unperm_in_cols(w):
    a = Q_LORA_RANK + KV_LORA_RANK
    zx = D_INNER + XBC_WIDTH
    return jnp.concatenate([w[:, :a], w[:, a + zx:a + zx + QK_ROPE_DIM], w[:, a:a + zx], w[:, a + zx + QK_ROPE_DIM:]],
                           axis=1)


def _group_xbc(a):
    n = SSD_STATE
    parts = []
    for g in range(SSD_GROUPS):
        parts += [a[..., g * X_COLS:(g + 1) * X_COLS], a[..., D_INNER + g * n:D_INNER + (g + 1) * n],
                  a[..., D_INNER + GN + g * n:D_INNER + GN + (g + 1) * n]]
    return jnp.concatenate(parts, axis=-1)


def _ungroup_xbc(a):
    n = SSD_STATE
    xs = [a[..., g * GROUP_COLS:g * GROUP_COLS + X_COLS] for g in range(SSD_GROUPS)]
    bs = [a[..., g * GROUP_COLS + X_COLS:g * GROUP_COLS + X_COLS + n] for g in range(SSD_GROUPS)]
    cs = [a[..., g * GROUP_COLS + X_COLS + n:(g + 1) * GROUP_COLS] for g in range(SSD_GROUPS)]
    return jnp.concatenate(xs + bs + cs, axis=-1)


UP_BLOCK = 256


def _interleave_up(w):
    parts = []
    for j in range(D_FF // UP_BLOCK):
        parts += [w[:, j * UP_BLOCK:(j + 1) * UP_BLOCK], w[:, D_FF + j * UP_BLOCK:D_FF + (j + 1) * UP_BLOCK]]
    return jnp.concatenate(parts, axis=1)


def _deinterleave_up(w):
    blocks = [w[:, j * UP_BLOCK:(j + 1) * UP_BLOCK] for j in range(2 * D_FF // UP_BLOCK)]
    return jnp.concatenate(blocks[0::2] + blocks[1::2], axis=1)


def _pad_q_heads(w):
    k = w.shape[0]
    return jnp.pad(w.reshape(k, N_ATTN_HEADS, QK_DIM), ((0, 0), (0, 0), (0, HEAD_TILE - QK_DIM))).reshape(k, -1)


def _unpad_q_heads(w):
    k = w.shape[0]
    return w.reshape(k, N_ATTN_HEADS, HEAD_TILE)[..., :QK_DIM].reshape(k, N_ATTN_HEADS * QK_DIM)


def _rope_tables(seq_len):
    n_rows = seq_len // GRID_W
    row = jnp.repeat(jnp.arange(n_rows), GRID_W).astype(F32)
    col = jnp.tile(jnp.arange(GRID_W), n_rows).astype(F32)
    axis_dim = QK_ROPE_DIM // 2
    inv_freq = ROPE_THETA ** (-jnp.arange(0, axis_dim, 2, dtype=F32) / axis_dim)
    ang_r = row[:, None] * inv_freq
    ang_c = col[:, None] * inv_freq
    ang = jnp.concatenate([ang_r, ang_r, ang_c, ang_c], axis=-1)
    return jnp.cos(ang), jnp.sin(ang)


def _rot_matrix(width, start):
    r = np.zeros((width, width), np.float32)
    quarter = QK_ROPE_DIM // 4
    for base in (0, QK_ROPE_DIM // 2):
        for i in range(quarter):
            r[start + base + quarter + i, start + base + i] = -1.0
            r[start + base + i, start + base + quarter + i] = 1.0
    return jnp.asarray(r)


ROPE_STEP = QK_ROPE_DIM // 4


def _rope_flat_fn(x, cos, sin_up, sin_down):
    reps = x.shape[1] // cos.shape[1]

    def heads(t):
        return jnp.concatenate([t] * reps, axis=1)

    return (x * heads(cos) + _roll_lanes(x, -ROPE_STEP) * heads(sin_up) + _roll_lanes(x, ROPE_STEP) * heads(sin_down),)


def _rope_flat_transpose_fn(g, cos, sin_up, sin_down):
    reps = g.shape[1] // cos.shape[1]

    def heads(t):
        return jnp.concatenate([t] * reps, axis=1)

    return (g * heads(cos) + _roll_lanes(g * heads(sin_up), ROPE_STEP) + _roll_lanes(g * heads(sin_down), -ROPE_STEP),)


def _krdt_fn(x, cos, sin, rot, bias):
    lane = lax.broadcasted_iota(jnp.int32, (1, KRDT_WIDTH), 1)
    is_dt = (lane >= QK_ROPE_DIM) & (lane < QK_ROPE_DIM + 2 * N_SSD_HEADS)
    roped = x * cos + _dot_exact(x, rot) * sin
    return (jnp.where(is_dt, _softplus(x + bias), roped),)


def _pre_fn(u, w, shift, scale):
    return (_rms(u, w) * (1.0 + scale) + shift,)


def _norm_fn(x, w):
    return (_rms(x, w),)


def _finish_fn(yf, yb, z, w):
    return (_rms((yf + yb) * _silu(z), w),)


def _mid_fn(x, mix, w_post, w_pre, gate, shift, scale):
    x1 = x + gate * _rms(mix, w_post)
    return (x1, _rms(x1, w_pre) * (1.0 + scale) + shift)


def _loss_fn(x1, ffn, tgt, w_post, gate):
    y = x1 + gate * _rms(ffn, w_post)
    err = y - tgt
    return (0.5 * jnp.mean(err * err, axis=-1, keepdims=True),)


def _bias_fn(x, b):
    return (x + b,)


def _silu_fn(x):
    return (_silu(x),)


def kernel(x, c, ctx, c_ctx, w_mod, b_mod, mix_pre_norm, mix_post_norm, w_in, q_norm, w_q_up, kv_norm, w_kv_up, ssd_conv_w, ssd_conv_b, ssd_a_log, ssd_dt_bias, ssd_d, ssd_norm, w_out, ffn_pre_norm, ffn_post_norm, w_up, ffn_conv_w, ffn_conv_b, w_down, loss_target, m_c_ctx, m_w_mod, m_b_mod, m_mix_pre_norm, m_mix_post_norm, m_w_in, m_q_norm, m_w_q_up, m_kv_norm, m_w_kv_up, m_ssd_conv_w, m_ssd_conv_b, m_ssd_a_log, m_ssd_dt_bias, m_ssd_d, m_ssd_norm, m_w_out, m_ffn_pre_norm, m_ffn_post_norm, m_w_up, m_ffn_conv_w, m_ffn_conv_b, m_w_down, v_c_ctx, v_w_mod, v_b_mod, v_mix_pre_norm, v_mix_post_norm, v_w_in, v_q_norm, v_w_q_up, v_kv_norm, v_w_kv_up, v_ssd_conv_w, v_ssd_conv_b, v_ssd_a_log, v_ssd_dt_bias, v_ssd_d, v_ssd_norm, v_w_out, v_ffn_pre_norm, v_ffn_post_norm, v_w_up, v_ffn_conv_w, v_ffn_conv_b, v_w_down):
    args = dict(locals())
    names = ["c_ctx", "w_mod", "b_mod", "mix_pre_norm", "mix_post_norm", "w_in", "q_norm", "w_q_up", "kv_norm",
             "w_kv_up", "ssd_conv_w", "ssd_conv_b", "ssd_a_log", "ssd_dt_bias", "ssd_d", "ssd_norm", "w_out",
             "ffn_pre_norm", "ffn_post_norm", "w_up", "ffn_conv_w", "ffn_conv_b", "w_down"]
    nb, s, d = x.shape
    nctx_rows = ctx.shape[1]
    lt = nctx_rows + s
    tl = 256 if (nctx_rows % 256 == 0 and s % 256 == 0) else 128
    nctx = nctx_rows // tl
    ncc = nctx_rows // SSD_CHUNK
    h, e, g2 = N_ATTN_HEADS, HEADS_PER_GROUP, SSD_GROUPS
    chip = 2 * lax.axis_index("x") + lax.axis_index("y")

    big_local = {n: args[n][0] for n, _, _, _ in BIG}
    big_info = {n: (rows, cols, axis) for n, rows, cols, axis in BIG}

    def pack_shards(group):
        return _pack_rows([big_local[n].astype(WIRE_DTYPE) for n in group])

    def unpack_gathered(gathered, mine, group):
        gathered = lax.dynamic_update_slice(gathered, mine[None], (chip, 0, 0))
        res, o = {}, 0
        for n in group:
            rows, cols, axis = big_info[n]
            lr, lc = big_local[n].shape
            nr = lr * lc // PACK_COLS
            seg = gathered[:, o:o + nr].reshape(N_CHIPS, lr, lc)
            o += nr
            res[n] = seg.reshape(rows, cols) if axis == 0 else jnp.transpose(seg, (1, 0, 2)).reshape(rows, cols)
        return res

    core = lax.axis_index("c")

    def pair_sums(grads_full, group, tag):
        parts = []
        for n in group:
            _, _, axis = big_info[n]
            lr, lc = big_local[n].shape
            gfull = grads_full[n]
            shards = (gfull.reshape(N_CHIPS, lr, lc) if axis == 0
                      else jnp.transpose(gfull.reshape(lr, N_CHIPS, lc), (1, 0, 2)))
            parts.append(shards.reshape(N_CHIPS, lr * lc // PACK_COLS, PACK_COLS))
        gpack = jnp.concatenate(parts, axis=1).astype(WIRE_DTYPE)
        half = gpack.shape[1] // 2
        gpack = gpack.reshape(N_CHIPS, 2, half, PACK_COLS)
        got = _swap_halves(gpack, name="grad_swap_" + tag)
        own = lax.dynamic_index_in_dim(gpack, core, axis=1, keepdims=False)
        flat = (N_CHIPS * half, PACK_COLS)
        return _sum_list([own.reshape(flat), got.reshape(flat)], out_dtype=WIRE_DTYPE,
                         name="grad_add_pair_" + tag).reshape(N_CHIPS, half, PACK_COLS)

    def chip_total(sums, scattered, tag):
        mine_sum = lax.dynamic_index_in_dim(sums, chip, axis=0, keepdims=True)
        scattered = lax.dynamic_update_slice(scattered, mine_sum, (chip, 0, 0))
        return _sum_slots(scattered, out_dtype=F32, name="grad_add_chips_" + tag)

    packed_now, packed_late = pack_shards(GATHER_NOW), pack_shards(GATHER_LATE)
    full = unpack_gathered(_gather_shards(packed_now, name="gather_weights"), packed_now, GATHER_NOW)
    n_sc, n_fc = ssd_conv_w.shape[2], ffn_conv_w.shape[2]
    n_conv = SSD_CONV * n_sc + FFN_CONV * n_fc
    first_rows = -(-(n_conv + nb * d) // (8 * LANES)) * 8
    first_all = _allgather_small(_pack_small([ssd_conv_w[0], ffn_conv_w[0], c], first_rows), name="gather_conv_c")
    first_all = first_all.reshape(N_DEV, -1)
    conv_all = first_all[::2]
    ssd_conv_full = jnp.concatenate(
        [conv_all[j][:SSD_CONV * n_sc].reshape(SSD_CONV, n_sc) for j in range(N_CHIPS)], axis=1)
    ffn_conv_full = jnp.concatenate(
        [conv_all[j][SSD_CONV * n_sc:n_conv].reshape(FFN_CONV, n_fc) for j in range(N_CHIPS)], axis=1)
    c_every = first_all[:, n_conv:n_conv + nb * d].reshape(N_DEV * nb, d)

    w_in_p = _perm_in_cols(full["w_in"])
    o_cq, o_ckv, o_z = 0, Q_LORA_RANK, Q_LORA_RANK + KV_LORA_RANK
    o_xbc, o_kr = o_z + D_INNER, o_z + D_INNER + XBC_WIDTH
    w_krdt = jnp.pad(w_in_p[:, o_kr:], ((0, 0), (0, KRDT_WIDTH - QK_ROPE_DIM - 2 * N_SSD_HEADS)))
    w_segs = [w_in_p[:, o_cq:o_ckv], w_in_p[:, o_ckv:o_z], w_in_p[:, o_z:o_xbc], _group_xbc(w_in_p[:, o_xbc:o_kr]),
              w_krdt]
    ssd_conv_g, ssd_conv_b_g = _group_xbc(ssd_conv_full), _group_xbc(ssd_conv_b)
    w_q_pad = _pad_q_heads(full["w_q_up"])

    mod_rows = 16
    n_ex = N_DEV * nb
    all_rows = -(-(n_ex + 1) // 16) * 16
    me = 2 * chip + lax.axis_index("c")
    c_all = jnp.concatenate([c_every, c_ctx[None, :], jnp.zeros((all_rows - n_ex - 1, d), F32)], axis=0)[None]
    (s_all,) = _row_fwd(_silu_fn, toks=[(c_all, 0, None, 0)], outs=[(d, F32)], nb=1, nl=all_rows, tl=all_rows,
                        name="mod_silu")
    w_mod_local = w_mod[0]
    mod_cols = w_mod_local.shape[1]
    mod_part = _mm(s_all[0], w_mod_local, name="mod_mm")
    mod_parts = _allgather_small(mod_part, name="gather_mod")[::2]
    mod_every = jnp.concatenate([mod_parts[j] for j in range(N_CHIPS)], axis=1)
    mod_lin = jnp.concatenate([lax.dynamic_slice_in_dim(mod_every, me * nb, nb, axis=0), mod_every[n_ex:n_ex + 1],
                               jnp.zeros((mod_rows - nb - 1, N_MOD * d), F32)], axis=0)
    (mod,) = _row_fwd(_bias_fn, toks=[(mod_lin[None], 0, None, 0)], vecs=[b_mod], outs=[(N_MOD * d, F32)], nb=1,
                      nl=mod_rows, tl=mod_rows, name="mod_bias")
    mods = [mod[0][:, k * d:(k + 1) * d][:, None, :] for k in range(N_MOD)]
    mods_lat = [m[:nb] for m in mods]

    u = jnp.concatenate([ctx, x], axis=1)
    (h1,) = _row_fwd(_pre_fn, toks=[(u, 0, None, 0)], vecs=[mix_pre_norm], bvecs=[mods[0], mods[1]],
                     outs=[(d, MXU_DTYPE)], nb=nb, nl=lt, tl=tl, nctx=nctx, name="pre1")
    h1f = h1.reshape(nb * lt, d)
    p_cq, p_ckv, p_z, p_xbc, p_krdt = [
        _mm(h1f, w, name="in_" + nm).reshape(nb, lt, -1)
        for nm, w in zip(("cq", "ckv", "z", "xbc", "krdt"), w_segs)]

    (cqn,) = _row_fwd(_norm_fn, toks=[(p_cq, nctx, None, 0)], vecs=[q_norm], outs=[(Q_LORA_RANK, MXU_DTYPE)],
                      nb=nb, nl=s, tl=tl, name="q_norm")
    q_flat = _mm(cqn.reshape(nb * s, -1), w_q_pad, name="q_up").reshape(nb, s, h * HEAD_TILE)
    cos, sin = _rope_tables(s)
    ones, zeros = jnp.ones((s, QK_NOPE_DIM), F32), jnp.zeros((s, QK_NOPE_DIM), F32)
    tail = HEAD_TILE - QK_DIM
    up_lanes = ((jnp.arange(QK_ROPE_DIM) // ROPE_STEP) % 2 == 0)[None, :]
    q_tables = [jnp.concatenate([pad, t, pad[:, :tail]], axis=1)[None]
                for pad, t in ((ones, cos), (zeros, jnp.where(up_lanes, -sin, 0.0)), (zeros, jnp.where(up_lanes, 0.0, sin)))]
    tq = 256
    (q_roped,) = _row_fwd(_rope_flat_fn, toks=[(q_flat, 0, None, 0)], poss=q_tables, outs=[(h * HEAD_TILE, MXU_DTYPE)],
                          nb=nb, nl=s, tl=tl, name="rope_q")

    (ckvn,) = _row_fwd(_norm_fn, toks=[(p_ckv, 0, None, 0)], vecs=[kv_norm], outs=[(KV_LORA_RANK, MXU_DTYPE)],
                       nb=nb, nl=lt, tl=tl, name="kv_norm")
    kv_flat = _mm(ckvn.reshape(nb * lt, -1), full["w_kv_up"], out_dtype=MXU_DTYPE, name="kv_up").reshape(nb, lt, -1)

    pad_w = KRDT_WIDTH - QK_ROPE_DIM
    cos_k = jnp.concatenate([jnp.ones((nctx_rows, KRDT_WIDTH), F32),
                             jnp.concatenate([cos, jnp.ones((s, pad_w), F32)], axis=1)], axis=0)[None]
    sin_k = jnp.concatenate([jnp.zeros((nctx_rows, KRDT_WIDTH), F32),
                             jnp.concatenate([sin, jnp.zeros((s, pad_w), F32)], axis=1)], axis=0)[None]
    rot_k = _rot_matrix(KRDT_WIDTH, 0)
    dt_bias_row = jnp.pad(ssd_dt_bias.reshape(1, -1), ((0, 0), (QK_ROPE_DIM, pad_w - 2 * N_SSD_HEADS)))
    (krdt,) = _row_fwd(_krdt_fn, toks=[(p_krdt, 0, None, 0)], poss=[cos_k, sin_k], vecs=[rot_k, dt_bias_row],
                       outs=[(KRDT_WIDTH, F32)], nb=nb, nl=lt, tl=tl, name="krdt")
    kr = jnp.pad(krdt[..., :QK_ROPE_DIM].astype(MXU_DTYPE), ((0, 0), (0, 0), (QK_NOPE_DIM, HEAD_TILE - QK_DIM)))
    attn, lse, gathered_late = _attn_fwd(q_roped, kv_flat, kr, packed_late, tq=tq, name="attn_fwd")
    full.update(unpack_gathered(gathered_late, packed_late, GATHER_LATE))
    w_up_il = _interleave_up(full["w_up"])
    w_out_a, w_out_s = full["w_out"][:ATTN_WIDTH], full["w_out"][ATTN_WIDTH:]

    seg = nctx_rows

    def conv_ssd_fn(xv, w, b):
        return (_silu(_dwconv(xv, w, seg) + b),)

    def conv_ssd_bwd(xv, w, b, dy):
        cv = _dwconv(xv, w, seg) + b
        sg = _sigmoid(cv)
        dc = dy * (sg * (1.0 + cv * (1.0 - sg)))
        dx, dw = _dwconv_back(xv, dc, w, seg)
        return dx, dw, jnp.sum(dc, axis=0, keepdims=True)

    cb_ssd = 256
    conv_vecs = [(ssd_conv_g, 0), (ssd_conv_b_g, 0)]
    (xbc,) = _seq_fwd(conv_ssd_fn, toks=[(p_xbc, 0, 1)], vecs=conv_vecs, outs=[F32], nb=nb, nl=lt, nc=XBC_WIDTH,
                      cb=cb_ssd, name="conv_ssd")
    dt = krdt[..., QK_ROPE_DIM:QK_ROPE_DIM + 2 * N_SSD_HEADS].reshape(nb, lt, 2, g2, e)
    dt_lane = QK_ROPE_DIM + N_SSD_HEADS * jnp.arange(2)[:, None, None] + jnp.arange(D_INNER)[None, None, :] // SSD_HEAD_DIM
    spread = (jnp.arange(KRDT_WIDTH)[None, :, None] == dt_lane).astype(F32)

    def spread_fn(v, s0, s1):
        return (_mask_dot_raw(s0, v, False), _mask_dot_raw(s1, v, False))

    dtc = _row_fwd(spread_fn, toks=[(krdt, 0, None, 0)], vecs=[spread[0], spread[1]],
                   outs=[(D_INNER, F32), (D_INNER, F32)], nb=nb, nl=lt, tl=tl, name="dt_spread")
    dtr = jnp.transpose(dt, (2, 0, 3, 4, 1))
    a_neg = -jnp.exp(ssd_a_log[0]).reshape(2, g2, e)
    d_chan = jnp.repeat(ssd_d[0], SSD_HEAD_DIM).reshape(g2, 1, X_COLS)
    a_chan = [jnp.repeat(a_neg[dr].reshape(-1), SSD_HEAD_DIM).reshape(g2, 1, X_COLS) for dr in range(2)]
    scan_args = [(xbc, dtc[dr], dtr[dr], a_chan[dr], a_neg[dr][:, :, None], d_chan) for dr in range(2)]
    (y0, ent0), (y1, ent1) = _scan_fwd(scan_args, ncc=ncc, name="scan_fwd")
    ys, ents = [y0, y1], [ent0, ent1]
    (ssd,) = _row_fwd(_finish_fn, toks=[(ys[0], nctx, None, 0), (ys[1], nctx, None, 0), (p_z, nctx, None, 0)],
                      vecs=[ssd_norm], outs=[(D_INNER, MXU_DTYPE)], nb=nb, nl=s, tl=tl, name="ssd_finish")

    attn_f, ssd_f = attn.reshape(nb * s, ATTN_WIDTH), ssd.reshape(nb * s, D_INNER)
    mix = _mm_sum([(attn_f, w_out_a), (ssd_f, w_out_s)], name="out_proj").reshape(nb, s, d)

    mid_bvecs = [mods_lat[2], mods_lat[3], mods_lat[4]]
    x1, h2 = _row_fwd(_mid_fn, toks=[(x, 0, None, 0), (mix, 0, None, 0)], vecs=[mix_post_norm, ffn_pre_norm],
                      bvecs=mid_bvecs, outs=[(d, F32), (d, MXU_DTYPE)], nb=nb, nl=s, tl=tl, name="mid")
    up = _mm(h2.reshape(nb * s, d), w_up_il, name="ffn_up").reshape(nb, s, 2 * D_FF)

    def glu_fn(gv, w, b):
        return (_gelu(_dwconv(gv[:, :UP_BLOCK], w, 0) + b) * gv[:, UP_BLOCK:],)

    def glu_bwd(gv, w, b, da):
        gate, val = gv[:, :UP_BLOCK], gv[:, UP_BLOCK:]
        cv = _dwconv(gate, w, 0) + b
        cdf = 0.5 * (1.0 + lax.erf(cv * (2.0 ** -0.5)))
        pdf = jnp.exp(-0.5 * cv * cv) * (1.0 / math.sqrt(2.0 * math.pi))
        dc = (da * val) * (cdf + cv * pdf)
        dgate, dw = _dwconv_back(gate, dc, w, 0)
        return jnp.concatenate([dgate, da * (cv * cdf)], axis=1), dw, jnp.sum(dc, axis=0, keepdims=True)

    cb_ffn = UP_BLOCK
    glu_toks = [(up, 0, 2)]
    glu_vecs = [(ffn_conv_full, 0), (ffn_conv_b, 0)]
    (act,) = _seq_fwd(glu_fn, toks=glu_toks, vecs=glu_vecs, outs=[MXU_DTYPE], nb=nb, nl=s, nc=D_FF, cb=cb_ffn,
                      name="conv_glu")
    ffn = _mm(act.reshape(nb * s, D_FF), full["w_down"], name="ffn_down").reshape(nb, s, d)

    loss_toks = [(x1, 0, None, 0), (ffn, 0, None, 0), (loss_target, 0, None, 0)]
    ones_rows = jnp.ones((nb, s, 1), F32)
    (dx1_a, dffn, _), (g_ffn_post,), ((g_gate5, _),), (loss_rows,) = _row_bwd(
        _loss_fn, toks=loss_toks, vecs=[ffn_post_norm], bvecs=[mods_lat[5]], cots=[(ones_rows, 0)],
        tok_grads=[F32, MXU_DTYPE, None], emit=[(0, 1, F32)], nb=nb, nl=s, tl=tl, name="loss_bwd")
    loss_part = jnp.sum(loss_rows)

    dffn_f = dffn.reshape(nb * s, d)
    g_w_down = _mm(act.reshape(nb * s, D_FF), dffn_f, ta=True, name="wg_down")
    dact = _mm(dffn_f, full["w_down"], tb=True, out_dtype=MXU_DTYPE, name="dg_down").reshape(nb, s, D_FF)
    (dup,), (g_ffn_conv_w, g_ffn_conv_b) = _seq_bwd(
        glu_fn, toks=glu_toks, vecs=glu_vecs, cots=[[dact]], tok_grads=[MXU_DTYPE], nb=nb, nl=s, nc=D_FF,
        cb=cb_ffn, name="conv_glu_bwd", bwd_fn=glu_bwd)
    dup = dup.reshape(nb * s, 2 * D_FF)
    g_w_up = _deinterleave_up(_mm(h2.reshape(nb * s, d), dup, ta=True, name="wg_up"))
    dh2 = _mm(dup, w_up_il, tb=True, name="dg_up").reshape(nb, s, d)

    (dx_res, dmix), (g_mix_post, g_ffn_pre), ((g_gate2, _), (g_shift3, _), (g_scale4, _)), _ = _row_bwd(
        _mid_fn, toks=[(x, 0, None, 0), (mix, 0, None, 0)], vecs=[mix_post_norm, ffn_pre_norm], bvecs=mid_bvecs,
        cots=[(dx1_a, 0), (dh2, 0)], tok_grads=[F32, MXU_DTYPE], nb=nb, nl=s, tl=tl, name="mid_bwd")

    dmix_f = dmix.reshape(nb * s, d)
    g_w_out = jnp.concatenate([_mm(attn_f, dmix_f, ta=True, name="wg_out_attn"),
                               _mm(ssd_f, dmix_f, ta=True, name="wg_out_ssd")], axis=0)
    early_sums = pair_sums({"w_up": g_w_up, "w_down": g_w_down, "w_out": g_w_out}, REDUCE_EARLY, "early")
    dattn = _mm(dmix_f, w_out_a, tb=True, name="dg_out_attn").reshape(nb, s, ATTN_WIDTH)
    dssd = _mm(dmix_f, w_out_s, tb=True, name="dg_out_ssd").reshape(nb, s, D_INNER)

    (dy, _, dz), (g_ssd_norm,), _, _ = _row_bwd(
        _finish_fn, toks=[(ys[0], 0, None, 0), (ys[1], 0, None, 0), (p_z, 0, None, 0)], vecs=[ssd_norm],
        cots=[(dssd, -nctx)], tok_grads=[F32, None, MXU_DTYPE], nb=nb, nl=lt, tl=tl, name="ssd_finish_bwd")
    scan_grads = _scan_bwd(scan_args, ents, dy, ncc=ncc, name="scan_bwd")

    def collect_fn(g0, g1, c0, c1):
        return (_mask_dot_raw(c0, g0, False) + _mask_dot_raw(c1, g1, False),)

    (g_dt_lanes,) = _row_fwd(collect_fn, toks=[(scan_grads[0][1], 0, None, 0), (scan_grads[1][1], 0, None, 0)],
                             vecs=[spread[0].T, spread[1].T], outs=[(KRDT_WIDTH, F32)], nb=nb, nl=lt, tl=tl,
                             name="dt_collect")
    g_dt_dirs, g_a = [], []
    for _, _, gdtr, gac, gar, _ in scan_grads:
        g_dt_dirs.append(jnp.transpose(gdtr, (0, 3, 1, 2)))
        g_a.append(jnp.sum(jnp.sum(gac.reshape(nb, g2, e, SSD_HEAD_DIM), axis=-1) + gar[:, :, :, 0], axis=0))
    g_d_chan = jnp.sum(scan_grads[0][5], axis=0)
    g_a_log = (jnp.stack(g_a) * a_neg).reshape(1, 2, N_SSD_HEADS)
    g_dt = (jnp.stack(g_dt_dirs, axis=2).reshape(nb, lt, 2 * N_SSD_HEADS)
            + g_dt_lanes[..., QK_ROPE_DIM:QK_ROPE_DIM + 2 * N_SSD_HEADS])
    (dp_xbc,), (g_ssd_conv_w, g_ssd_conv_b) = _seq_bwd(
        conv_ssd_fn, toks=[(p_xbc, 0, 1)], vecs=conv_vecs, cots=[[scan_grads[0][0], scan_grads[1][0]]],
        tok_grads=[MXU_DTYPE], nb=nb, nl=lt, nc=XBC_WIDTH, cb=cb_ssd, name="conv_ssd_bwd", bwd_fn=conv_ssd_bwd)
    g_ssd_conv_w, g_ssd_conv_b = _ungroup_xbc(g_ssd_conv_w), _ungroup_xbc(g_ssd_conv_b)

    dq_roped, dkv, dkr, early_scattered = _attn_bwd(q_roped, kv_flat, kr, attn, lse, dattn, early_sums, tq=tq,
                                                    name="attn_bwd")
    (dq_flat,) = _row_fwd(_rope_flat_transpose_fn, toks=[(dq_roped, 0, None, 0)], poss=q_tables,
                          outs=[(h * HEAD_TILE, MXU_DTYPE)], nb=nb, nl=s, tl=tl, name="rope_q_bwd")
    dq_flat = dq_flat.reshape(nb * s, h * HEAD_TILE)
    g_w_q_up = _unpad_q_heads(_mm(cqn.reshape(nb * s, -1), dq_flat, ta=True, name="wg_q_up"))
    dcqn = _mm(dq_flat, w_q_pad, tb=True, name="dg_q_up").reshape(nb, s, Q_LORA_RANK)
    (dp_cq,), (g_q_norm,), _, _ = _row_bwd(_norm_fn, toks=[(p_cq, 0, None, 0)], vecs=[q_norm], cots=[(dcqn, -nctx)],
                                           tok_grads=[MXU_DTYPE], nb=nb, nl=lt, tl=tl, name="q_norm_bwd")

    dkv_flat = dkv.reshape(nb * lt, -1)
    g_w_kv_up = _mm(ckvn.reshape(nb * lt, -1), dkv_flat, ta=True, name="wg_kv_up")
    dckvn = _mm(dkv_flat, full["w_kv_up"], tb=True, name="dg_kv_up").reshape(nb, lt, KV_LORA_RANK)
    (dp_ckv,), (g_kv_norm,), _, _ = _row_bwd(_norm_fn, toks=[(p_ckv, 0, None, 0)], vecs=[kv_norm], cots=[(dckvn, 0)],
                                             tok_grads=[MXU_DTYPE], nb=nb, nl=lt, tl=tl, name="kv_norm_bwd")

    g_krdt = jnp.concatenate([dkr[..., QK_NOPE_DIM:QK_DIM], g_dt, jnp.zeros((nb, lt, pad_w - 2 * N_SSD_HEADS), F32)],
                             axis=-1)
    (dp_krdt,), (_, g_dt_bias_row), _, _ = _row_bwd(
        _krdt_fn, toks=[(p_krdt, 0, None, 0)], poss=[cos_k, sin_k], vecs=[rot_k, dt_bias_row], cots=[(g_krdt, 0)],
        tok_grads=[MXU_DTYPE], nb=nb, nl=lt, tl=tl, name="krdt_bwd")

    dp_segs = [t.reshape(nb * lt, -1) for t in (dp_cq, dp_ckv, dz, dp_xbc, dp_krdt)]
    g_segs = [_mm(h1f, t, ta=True, name="wg_in_" + nm) for nm, t in zip(("cq", "ckv", "z", "xbc", "krdt"), dp_segs)]
    g_segs[3] = _ungroup_xbc(g_segs[3])
    g_w_in_p = jnp.concatenate(g_segs, axis=1)
    dh1 = _mm_sum(list(zip(dp_segs, w_segs)), tb=True, name="dg_in").reshape(nb, lt, d)

    def pre_res_fn(uv, w, shift, scale):
        return _pre_fn(uv, w, shift, scale) + (uv,)

    (grad_x,), (g_mix_pre,), ((g_shift0, g_shift0c), (g_scale1, g_scale1c)), _ = _row_bwd(
        pre_res_fn, toks=[(u, 0, None, 0)], vecs=[mix_pre_norm], bvecs=[mods[0], mods[1]],
        cots=[(dh1, 0), (dx_res, -nctx)], tok_grads=[F32], nb=nb, nl=lt, tl=tl, nctx=nctx, drop_blocks=nctx,
        name="pre1_bwd")

    zero_row = jnp.zeros((1, 1, d), F32)
    lat = [g_shift0, g_scale1, g_gate2, g_shift3, g_scale4, g_gate5]
    ctxg = [g_shift0c, g_scale1c, zero_row, zero_row, zero_row, zero_row]
    dmod = jnp.concatenate([jnp.concatenate([a, b], axis=0)[:, 0, :] for a, b in zip(lat, ctxg)], axis=-1)
    dmod = jnp.pad(dmod, ((0, mod_rows - nb - 1), (0, 0)))
    _, (g_b_mod,), _, _ = _row_bwd(_bias_fn, toks=[(mod_lin[None], 0, None, 0)], vecs=[b_mod], cots=[(dmod[None], 0)],
                                   tok_grads=[None], nb=1, nl=mod_rows, tl=mod_rows, name="mod_bias_bwd")
    dmod_all = _allgather_small(dmod[:8], name="gather_dmod")
    dmod_ctx = _sum_slots(dmod_all, out_dtype=F32, name="dmod_ctx_add")[nb:nb + 1]
    dmod_every = jnp.concatenate([dmod_all[:, :nb].reshape(n_ex, N_MOD * d), dmod_ctx,
                                  jnp.zeros((all_rows - n_ex - 1, N_MOD * d), F32)], axis=0)
    dmod_mine = lax.dynamic_slice_in_dim(dmod_every, chip * mod_cols, mod_cols, axis=1)
    g_w_mod = _mm(s_all[0], dmod_mine, ta=True, name="wg_mod")[None]
    ds_all = _mm(dmod_mine, w_mod_local, tb=True, name="dg_mod")
    (dc_all,), _, _, _ = _row_bwd(_silu_fn, toks=[(c_all, 0, None, 0)], cots=[(ds_all[None], 0)], tok_grads=[F32],
                                  nb=1, nl=all_rows, tl=all_rows, name="mod_silu_bwd")
    g_c_ctx = 0.5 * dc_all[0, n_ex]

    g_w_in = _unperm_in_cols(g_w_in_p[:, :IN_WIDTH])
    last_sums = pair_sums({"w_in": g_w_in, "w_q_up": g_w_q_up, "w_kv_up": g_w_kv_up}, REDUCE_LAST, "last")
    halves = [chip_total(early_sums, early_scattered, "early"),
              chip_total(last_sums, _scatter_to_chips(last_sums, name="grad_scatter"), "last")]
    my_halves = jnp.concatenate(halves, axis=0)
    joined = lax.dynamic_update_slice(_join_halves(my_halves, name="grad_join"), my_halves[None], (core, 0, 0))
    g_shards, o = {}, 0
    for group, hv in zip((REDUCE_EARLY, REDUCE_LAST), halves):
        g_shards[group] = joined[:, o:o + hv.shape[0]].reshape(2 * hv.shape[0], PACK_COLS)
        o += hv.shape[0]

    g_d = jnp.sum(g_d_chan.reshape(N_SSD_HEADS, SSD_HEAD_DIM), axis=1)[None]
    g_dt_bias = g_dt_bias_row[:, QK_ROPE_DIM:QK_ROPE_DIM + 2 * N_SSD_HEADS].reshape(1, 2, N_SSD_HEADS)
    small_names = ["c_ctx", "b_mod", "mix_pre_norm", "mix_post_norm", "q_norm", "kv_norm", "ssd_conv_w", "ssd_conv_b",
                   "ssd_a_log", "ssd_dt_bias", "ssd_d", "ssd_norm", "ffn_pre_norm", "ffn_post_norm", "ffn_conv_w",
                   "ffn_conv_b"]
    small_grads = [g_c_ctx, g_b_mod, g_mix_pre, g_mix_post, g_q_norm, g_kv_norm, g_ssd_conv_w, g_ssd_conv_b,
                   g_a_log, g_dt_bias, g_d, g_ssd_norm, g_ffn_pre, g_ffn_post, g_ffn_conv_w, g_ffn_conv_b]
    small_shapes = [tuple(np.shape(a)) for a in small_grads] + [()]
    n_small = sum(int(np.prod(shp)) for shp in small_shapes)
    small_rows = -(-n_small // (8 * LANES)) * 8
    small_all = _allgather_small(_pack_small(small_grads + [loss_part], small_rows), name="gather_small")
    small_sum = _sum_slots(small_all, out_dtype=F32, name="small_add")
    small_red = _unpack_small(small_sum, small_shapes)
    loss = small_red[-1]
    grads = dict(zip(small_names, small_red[:-1]))
    grads["ssd_conv_w"] = lax.dynamic_slice_in_dim(grads["ssd_conv_w"], chip * n_sc, n_sc, axis=1)[None]
    grads["ffn_conv_w"] = lax.dynamic_slice_in_dim(grads["ffn_conv_w"], chip * n_fc, n_fc, axis=1)[None]
    for n in small_names:
        grads[n] = grads[n].reshape(args[n].shape)

    delta, new_m, new_v = {}, {}, {}
    grads["w_mod"] = g_w_mod
    for group, g_shard in g_shards.items():
        o = 0
        for n in group:
            lr, lc = big_local[n].shape
            nr = lr * lc // PACK_COLS
            grads[n] = g_shard[o:o + nr].reshape(1, lr, lc)
            o += nr
    for n in ["w_mod"] + [n for n, _, _, _ in BIG]:
        dl, nm, nv = _adamw(args[n][0], grads[n][0], args["m_" + n][0], args["v_" + n][0], name="adamw_" + n)
        delta[n], new_m[n], new_v[n] = dl[None], nm[None], nv[None]
    sm_shapes = [args[n].shape for n in small_names]
    n_sm = sum(int(np.prod(shp)) for shp in sm_shapes)
    sm_rows = -(-n_sm // (8 * LANES)) * 8
    packs = [_pack_small([src[n] for n in small_names], sm_rows)
             for src in (args, grads, {n: args["m_" + n] for n in small_names}, {n: args["v_" + n] for n in small_names})]
    for out_dict, packed_out in zip((delta, new_m, new_v), _adamw(*packs, name="adamw_small")):
        out_dict.update(zip(small_names, _unpack_small(packed_out, sm_shapes)))

    return (loss, grad_x, *[grads[n] for n in names], *[delta[n] for n in names], *[new_m[n] for n in names],
            *[new_v[n] for n in names])
```

```python
import functools
import math

import numpy as np
import jax
import jax.numpy as jnp
from jax import lax
from jax.experimental import pallas as pl
from jax.experimental.pallas import tpu as pltpu

F32 = jnp.float32
MXU_DTYPE = jnp.bfloat16
WIRE_DTYPE = jnp.bfloat16
VMEM_LIMIT_BYTES = 56 * 1024 * 1024
HIGHEST = lax.Precision.HIGHEST

D_MODEL = 1024
N_MOD = 6
EPS = 1e-6
GRID_W = 64
N_ATTN_HEADS = 16
QK_NOPE_DIM = 64
QK_ROPE_DIM = 32
QK_DIM = QK_NOPE_DIM + QK_ROPE_DIM
V_HEAD_DIM = 64
Q_LORA_RANK = 384
KV_LORA_RANK = 256
ROPE_THETA = 10000.0
ATTN_SCALE = QK_DIM ** -0.5
ATTN_WIDTH = N_ATTN_HEADS * V_HEAD_DIM
N_SSD_HEADS = 16
SSD_HEAD_DIM = 64
SSD_GROUPS = 2
HEADS_PER_GROUP = N_SSD_HEADS // SSD_GROUPS
SSD_STATE = 128
SSD_CONV = 5
SSD_CHUNK = 128
D_INNER = N_SSD_HEADS * SSD_HEAD_DIM
GN = SSD_GROUPS * SSD_STATE
XBC_WIDTH = D_INNER + 2 * GN
D_FF = 2816
FFN_CONV = 3
KRDT_WIDTH = 128
IN_WIDTH = Q_LORA_RANK + KV_LORA_RANK + QK_ROPE_DIM + D_INNER + XBC_WIDTH + 2 * N_SSD_HEADS

ADAM_LR = 0.001
ADAM_B1 = 0.9
ADAM_B2 = 0.999
ADAM_EPS = 1e-08
ADAM_WD = 0.01
ADAM_STEP = 10

N_CHIPS = 4
N_DEV = 8
MESH = pl.DeviceIdType.MESH
LANES = 128

BIG = (("w_in", D_MODEL, IN_WIDTH, 1),
       ("w_q_up", Q_LORA_RANK, N_ATTN_HEADS * QK_DIM, 1),
       ("w_kv_up", KV_LORA_RANK, N_ATTN_HEADS * (QK_NOPE_DIM + V_HEAD_DIM), 1),
       ("w_out", ATTN_WIDTH + D_INNER, D_MODEL, 0), ("w_up", D_MODEL, 2 * D_FF, 1),
       ("w_down", D_FF, D_MODEL, 0))
PACK_COLS = 1024
GATHER_NOW, GATHER_LATE = ("w_in", "w_q_up", "w_kv_up"), ("w_out", "w_up", "w_down")
REDUCE_EARLY, REDUCE_LAST = ("w_up", "w_down", "w_out"), ("w_in", "w_q_up", "w_kv_up")


def _cparams(sem):
    return pltpu.CompilerParams(dimension_semantics=sem, vmem_limit_bytes=VMEM_LIMIT_BYTES)


def _pick(n, cands):
    for c in cands:
        if n % c == 0:
            return c
    return n


def _sigmoid(x):
    return 0.5 * (jnp.tanh(0.5 * x) + 1.0)


def _silu(x):
    return x * _sigmoid(x)


@jax.custom_vjp
def _softplus(x):
    u = jnp.exp(-jnp.abs(x))
    w = 1.0 + u
    log1p = jnp.where(w == 1.0, u, jnp.log(w) * (u / jnp.where(w == 1.0, 1.0, w - 1.0)))
    return jnp.maximum(x, 0.0) + log1p


def _softplus_fwd(x):
    return _softplus(x), x


def _softplus_bwd(x, g):
    return (g * _sigmoid(x),)


_softplus.defvjp(_softplus_fwd, _softplus_bwd)


@jax.custom_vjp
def _gelu(x):
    return 0.5 * x * (1.0 + lax.erf(x * (2.0 ** -0.5)))


def _gelu_fwd(x):
    return _gelu(x), x


def _gelu_bwd(x, g):
    cdf = 0.5 * (1.0 + lax.erf(x * (2.0 ** -0.5)))
    pdf = jnp.exp(-0.5 * x * x) * (1.0 / math.sqrt(2.0 * math.pi))
    return (g * (cdf + x * pdf),)


_gelu.defvjp(_gelu_fwd, _gelu_bwd)


def _rms(x, w):
    return x * lax.rsqrt(jnp.mean(x * x, axis=-1, keepdims=True) + EPS) * w


def _shift_rows_raw(x, off, seg):
    n = x.shape[0]
    if off == 0:
        return x
    r = pltpu.roll(x, (-off) % n, 0)
    idx = lax.broadcasted_iota(jnp.int32, x.shape, 0)
    src = idx + off
    ok = (src >= 0) & (src < n)
    if seg:
        ok = ok & ((idx < seg) == (src < seg))
    return jnp.where(ok, r, 0.0)


@functools.partial(jax.custom_vjp, nondiff_argnums=(1, 2))
def _shift_rows(x, off, seg):
    return _shift_rows_raw(x, off, seg)


def _shift_rows_fwd(x, off, seg):
    return _shift_rows_raw(x, off, seg), None


def _shift_rows_bwd(off, seg, _, g):
    return (_shift_rows_raw(g, -off, seg),)


_shift_rows.defvjp(_shift_rows_fwd, _shift_rows_bwd)


@functools.partial(jax.custom_vjp, nondiff_argnums=(1,))
def _roll_lanes(x, shift):
    return pltpu.roll(x, shift % x.shape[1], 1)


def _roll_lanes_fwd(x, shift):
    return _roll_lanes(x, shift), None


def _roll_lanes_bwd(shift, _, g):
    return (pltpu.roll(g, (-shift) % g.shape[1], 1),)


_roll_lanes.defvjp(_roll_lanes_fwd, _roll_lanes_bwd)


def _row_of(w, k):
    sel = lax.broadcasted_iota(jnp.int32, (w.shape[0], 1), 0) == k
    return jnp.sum(jnp.where(sel, w, 0.0), axis=0, keepdims=True)


def _col_of(w, k):
    sel = lax.broadcasted_iota(jnp.int32, (1, w.shape[1]), 1) == k
    return jnp.sum(jnp.where(sel, w, 0.0), axis=1, keepdims=True)


def _dwconv(x, w, seg):
    k = w.shape[0]
    acc = None
    for t in range(k):
        term = _shift_rows(x, t - k // 2, seg) * _row_of(w, t)
        acc = term if acc is None else acc + term
    return acc


def _dwconv_back(x, dy, w, seg):
    k = w.shape[0]
    tap = lax.broadcasted_iota(jnp.int32, (k, 1), 0)
    dx, dw = None, jnp.zeros_like(w)
    for t in range(k):
        back = _shift_rows_raw(dy, k // 2 - t, seg)
        term = back * _row_of(w, t)
        dx = term if dx is None else dx + term
        dw = dw + jnp.where(tap == t, jnp.sum(x * back, axis=0, keepdims=True), 0.0)
    return dx, dw


def _dot(a, b, dims):
    return lax.dot_general(a.astype(MXU_DTYPE), b.astype(MXU_DTYPE), (dims, ((), ())),
                           preferred_element_type=F32)


def _dot_exact(a, b):
    return lax.dot_general(a, b, (((1,), (0,)), ((), ())), precision=HIGHEST,
                           preferred_element_type=F32)


def _mask_dot_raw(mask, x, mask_left):
    hi = x.astype(jnp.bfloat16)
    rest = x - hi.astype(F32)
    mid = rest.astype(jnp.bfloat16)
    low = (rest - mid.astype(F32)).astype(jnp.bfloat16)
    m = mask.astype(jnp.bfloat16)
    acc = None
    for piece in (hi, mid, low):
        term = (lax.dot_general(m, piece, (((1,), (0,)), ((), ())), preferred_element_type=F32) if mask_left
                else lax.dot_general(piece, m, (((1,), (0,)), ((), ())), preferred_element_type=F32))
        acc = term if acc is None else acc + term
    return acc


@functools.partial(jax.custom_vjp, nondiff_argnums=(3,))
def _mask_dot(mask, mask_t, x, mask_left):
    return _mask_dot_raw(mask, x, mask_left)


def _mask_dot_fwd(mask, mask_t, x, mask_left):
    return _mask_dot_raw(mask, x, mask_left), (mask, mask_t)


def _mask_dot_bwd(mask_left, res, g):
    mask, mask_t = res
    return jnp.zeros_like(mask), jnp.zeros_like(mask_t), _mask_dot_raw(mask_t, g, mask_left)


_mask_dot.defvjp(_mask_dot_fwd, _mask_dot_bwd)


MM_VMEM_BUDGET = 40 * 1024 * 1024
MM_STEP_BYTES = 1 << 20
MM_TILES = (2816, 2048, 1536, 1408, 1024, 512, 384, 256, 128)


def _mm_tiles(m, n, kdim, a_bytes, b_bytes, out_bytes):
    tk = kdim if kdim <= 2048 else _pick(kdim, (2048, 1664, 1536, 1408, 1024, 512, 256, 128))
    nk = kdim // tk
    best = None
    for tm in [c for c in MM_TILES if c <= m and m % c == 0] or [m]:
        for tn in [c for c in MM_TILES if c <= n and n % c == 0] or [n]:
            casts = 2 * ((tm * tk if a_bytes != 2 else 0) + (tk * tn if b_bytes != 2 else 0))
            vmem = (2 * (tm * tk * a_bytes + tk * tn * b_bytes) + 2 * tm * tn * out_bytes
                    + tm * tn * 4 * (2 if nk > 1 else 1) + casts)
            if vmem > MM_VMEM_BUDGET:
                continue
            a_reads = 1 if nk == 1 else n // tn
            b_reads = 1 if (nk == 1 and n == tn) else m // tm
            cost = (m * kdim * a_bytes * a_reads + kdim * n * b_bytes * b_reads
                    + (m // tm) * (n // tn) * nk * MM_STEP_BYTES)
            if best is None or cost < best[0]:
                best = (cost, tm, tn)
    assert best is not None, (m, n, kdim)
    return best[1], best[2], tk


def _mm(a, b, *, ta=False, tb=False, out_dtype=F32, name):
    if ta:
        kdim, m = a.shape
    else:
        m, kdim = a.shape
    if tb:
        n, k2 = b.shape
    else:
        k2, n = b.shape
    assert kdim == k2, (a.shape, b.shape, ta, tb)
    tm, tn, tk = _mm_tiles(m, n, kdim, a.dtype.itemsize, b.dtype.itemsize, jnp.dtype(out_dtype).itemsize)
    nk = kdim // tk
    a_spec = pl.BlockSpec((tk, tm), lambda i, j, k: (k, i)) if ta else pl.BlockSpec((tm, tk), lambda i, j, k: (i, k))
    b_spec = pl.BlockSpec((tn, tk), lambda i, j, k: (j, k)) if tb else pl.BlockSpec((tk, tn), lambda i, j, k: (k, j))
    dims = ((0,) if ta else (1,), (1,) if tb else (0,))

    def body(a_ref, b_ref, o_ref, *scratch):
        if nk == 1:
            o_ref[...] = _dot(a_ref[...], b_ref[...], dims).astype(o_ref.dtype)
            return
        acc_ref, = scratch
        k = pl.program_id(2)

        @pl.when(k == 0)
        def _():
            acc_ref[...] = jnp.zeros_like(acc_ref)

        acc_ref[...] += _dot(a_ref[...], b_ref[...], dims)

        @pl.when(k == nk - 1)
        def _():
            o_ref[...] = acc_ref[...].astype(o_ref.dtype)

    return pl.pallas_call(
        body, name=name, grid=(m // tm, n // tn, nk),
        in_specs=[a_spec, b_spec], out_specs=pl.BlockSpec((tm, tn), lambda i, j, k: (i, j)),
        out_shape=jax.ShapeDtypeStruct((m, n), out_dtype),
        scratch_shapes=[pltpu.VMEM((tm, tn), F32)] if nk > 1 else [],
        compiler_params=_cparams(("parallel", "parallel", "arbitrary")),
    )(a, b)


def _mm_sum(pairs, *, tb=False, out_dtype=F32, name):
    m = pairs[0][0].shape[0]
    n = pairs[0][1].shape[0] if tb else pairs[0][1].shape[1]
    tm = _pick(m, (1024, 1408, 512, 384, 256, 128))
    tn = n if n == 1024 else _pick(n, (512, 1408, 384, 256, 128))
    specs, args = [], []
    for a, b in pairs:
        kdim = a.shape[1]
        specs.append(pl.BlockSpec((tm, kdim), lambda i, j: (i, 0)))
        specs.append(pl.BlockSpec((tn, kdim), lambda i, j: (j, 0)) if tb else pl.BlockSpec((kdim, tn), lambda i, j: (0, j)))
        args += [a, b]
    dims = ((1,), (1,) if tb else (0,))

    def body(*refs):
        acc = None
        for t in range(len(pairs)):
            term = _dot(refs[2 * t][...], refs[2 * t + 1][...], dims)
            acc = term if acc is None else acc + term
        refs[-1][...] = acc.astype(refs[-1].dtype)

    return pl.pallas_call(
        body, name=name, grid=(m // tm, n // tn), in_specs=specs,
        out_specs=pl.BlockSpec((tm, tn), lambda i, j: (i, j)), out_shape=jax.ShapeDtypeStruct((m, n), out_dtype),
        compiler_params=_cparams(("parallel", "parallel")),
    )(*args)


def _row_specs(toks, poss, vecs, bvecs, tl, nctx, nb):
    specs, args = [], []
    for arr, off, cw, ci in toks:
        cw = arr.shape[2] if cw is None else cw
        specs.append(pl.BlockSpec((1, tl, cw), lambda b, l, off=off, ci=ci: (b, l + off, ci)))
        args.append(arr)
    for arr in poss:
        specs.append(pl.BlockSpec((1, tl, arr.shape[2]), lambda b, l: (0, l, 0)))
        args.append(arr)
    for arr in vecs:
        specs.append(pl.BlockSpec(arr.shape, lambda b, l: (0, 0)))
        args.append(arr)
    for arr in bvecs:
        if nctx:
            specs.append(pl.BlockSpec((1, 1, arr.shape[2]), lambda b, l: (jnp.where(l < nctx, nb, b), 0, 0)))
        else:
            specs.append(pl.BlockSpec((1, 1, arr.shape[2]), lambda b, l: (b, 0, 0)))
        args.append(arr)
    return specs, args


def _row_fwd(fn, *, toks, poss=(), vecs=(), bvecs=(), outs, nb, nl, tl, nctx=0, name):
    nt, npos, nv, nbv = len(toks), len(poss), len(vecs), len(bvecs)
    specs, args = _row_specs(toks, poss, vecs, bvecs, tl, nctx, nb)

    def body(*refs):
        ins, os = refs[:len(specs)], refs[len(specs):]
        tv = [r[0].astype(F32) for r in ins[:nt]]
        pv = [r[0] for r in ins[nt:nt + npos]]
        vv = [r[...] for r in ins[nt + npos:nt + npos + nv]]
        bv = [r[0] for r in ins[nt + npos + nv:]]
        res = fn(*tv, *pv, *vv, *bv)
        for o, r in zip(os, res):
            o[0] = r.astype(o.dtype)

    return pl.pallas_call(
        body, name=name, grid=(nb, nl // tl), in_specs=specs,
        out_specs=[pl.BlockSpec((1, tl, c), lambda b, l: (b, l, 0)) for c, _ in outs],
        out_shape=[jax.ShapeDtypeStruct((nb, nl, c), dt) for c, dt in outs],
        compiler_params=_cparams(("parallel", "parallel")),
    )(*args)


def _row_bwd(fn, *, toks, poss=(), vecs=(), bvecs=(), cots, tok_grads, emit=(), nb, nl, tl, nctx=0, name,
             drop_blocks=0, ride=None):
    nt, npos, nv, nbv = len(toks), len(poss), len(vecs), len(bvecs)
    specs, args = _row_specs(toks, poss, vecs, bvecs, tl, nctx, nb)
    n_in = len(specs)
    cot_slots = []
    for arr, off in cots:
        if arr is None:
            cot_slots.append(None)
            continue
        cot_slots.append((len(specs), off))
        specs.append(pl.BlockSpec((1, tl, arr.shape[2]), lambda b, l, off=off: (b, jnp.maximum(l + off, 0), 0)))
        args.append(arr)
    n_all_in = len(specs)

    out_specs, out_shapes = [], []
    tok_out = []
    for (arr, off, cw, ci), dt in zip(toks, tok_grads):
        if dt is None:
            tok_out.append(None)
            continue
        cw = arr.shape[2] if cw is None else cw
        tok_out.append(len(out_specs))
        out_specs.append(pl.BlockSpec((1, tl, cw), lambda b, l: (b, jnp.maximum(l - drop_blocks, 0), 0)))
        out_shapes.append(jax.ShapeDtypeStruct((nb, nl - drop_blocks * tl, cw), dt))
    vec_out = []
    for arr in vecs:
        vec_out.append(len(out_specs))
        out_specs.append(pl.BlockSpec(arr.shape, lambda b, l: (0, 0)))
        out_shapes.append(jax.ShapeDtypeStruct(arr.shape, F32))
    bv_out = []
    for arr in bvecs:
        c = arr.shape[2]
        lat = len(out_specs)
        out_specs.append(pl.BlockSpec((1, 1, c), lambda b, l: (b, 0, 0)))
        out_shapes.append(jax.ShapeDtypeStruct((nb, 1, c), F32))
        ctx = None
        if nctx:
            ctx = len(out_specs)
            out_specs.append(pl.BlockSpec((1, 1, c), lambda b, l: (0, 0, 0)))
            out_shapes.append(jax.ShapeDtypeStruct((1, 1, c), F32))
        bv_out.append((lat, ctx))
    emit_out = []
    emit_cols = {}
    for idx, c, dt in emit:
        emit_out.append((idx, len(out_specs)))
        out_specs.append(pl.BlockSpec((1, tl, c), lambda b, l: (b, l, 0)))
        out_shapes.append(jax.ShapeDtypeStruct((nb, nl, c), dt))

    n_ride = 0 if ride is None else 1
    ride_out = len(out_specs)
    if ride is not None:
        specs.append(ANY)
        args.append(ride)
        out_specs.append(ANY)
        out_shapes.append(jax.ShapeDtypeStruct(ride.shape, ride.dtype))

    def body(*refs):
        ins, os = refs[:n_all_in], refs[n_all_in + n_ride:]
        b, l = pl.program_id(0), pl.program_id(1)
        if ride is not None:
            start, finish = _scatter_stage(refs[n_all_in], os[ride_out], os[-2], os[-1])
            pl.when((b == 0) & (l == 0))(start)
        tv = [r[0].astype(F32) for r in ins[:nt]]
        pv = [r[0] for r in ins[nt:nt + npos]]
        vv = [r[...] for r in ins[nt + npos:nt + npos + nv]]
        bv = [r[0] for r in ins[nt + npos + nv:n_in]]

        def f(*d):
            return tuple(fn(*d[:nt], *pv, *d[nt:]))

        res, vjp = jax.vjp(f, *tv, *vv, *bv)
        cts = []
        for r, slot in zip(res, cot_slots):
            if slot is None:
                cts.append(jnp.zeros_like(r))
            else:
                i, off = slot
                ct = ins[i][0].astype(F32)
                if off < 0:
                    ct = jnp.where(l + off >= 0, ct, 0.0)
                cts.append(ct)
        grads = vjp(tuple(cts))

        for g, slot in zip(grads[:nt], tok_out):
            if slot is not None:
                os[slot][0] = g.astype(os[slot].dtype)

        @pl.when((b == 0) & (l == 0))
        def _():
            for slot in vec_out:
                os[slot][...] = jnp.zeros_like(os[slot])
            for _, ctx in bv_out:
                if ctx is not None:
                    os[ctx][...] = jnp.zeros_like(os[ctx])

        @pl.when(l == 0)
        def _():
            for lat, _ in bv_out:
                os[lat][...] = jnp.zeros_like(os[lat])

        for g, slot in zip(grads[nt:nt + nv], vec_out):
            os[slot][...] += g
        for g, (lat, ctx) in zip(grads[nt + nv:], bv_out):
            if ctx is None:
                os[lat][0] += g
            else:
                is_ctx = l < nctx
                os[lat][0] += jnp.where(is_ctx, 0.0, g)
                os[ctx][0] += jnp.where(is_ctx, g, 0.0)
        for idx, slot in emit_out:
            os[slot][0] = res[idx].astype(os[slot].dtype)
        if ride is not None:
            pl.when((b == nb - 1) & (l == nl // tl - 1))(finish)

    out = pl.pallas_call(
        body, name=name, grid=(nb, nl // tl), in_specs=specs, out_specs=out_specs, out_shape=out_shapes,
        scratch_shapes=SCATTER_SEMS if ride is not None else [],
        compiler_params=_cparams(("arbitrary", "arbitrary")),
    )(*args)
    tg = [None if s is None else out[s] for s in tok_out]
    vg = [out[s] for s in vec_out]
    bg = [(out[lat], None if ctx is None else out[ctx]) for lat, ctx in bv_out]
    em = [out[s] for _, s in emit_out] + ([out[ride_out]] if ride is not None else [])
    return tg, vg, bg, em


def _seq_specs(toks, vecs, nl, cb):
    specs, args = [], []
    for arr, off, mult in toks:
        specs.append(pl.BlockSpec((1, nl, cb * mult), lambda j, b, off=off: (b, 0, j + off)))
        args.append(arr)
    for arr, off in vecs:
        specs.append(pl.BlockSpec((arr.shape[0], cb), lambda j, b, off=off: (0, j + off)))
        args.append(arr)
    return specs, args


def _seq_fwd(fn, *, toks, vecs, outs, nb, nl, nc, cb, name):
    nt = len(toks)
    specs, args = _seq_specs(toks, vecs, nl, cb)

    def body(*refs):
        ins, os = refs[:len(specs)], refs[len(specs):]
        tv = [r[0].astype(F32) for r in ins[:nt]]
        vv = [r[...] for r in ins[nt:]]
        for o, r in zip(os, fn(*tv, *vv)):
            o[0] = r.astype(o.dtype)

    return pl.pallas_call(
        body, name=name, grid=(nc // cb, nb), in_specs=specs,
        out_specs=[pl.BlockSpec((1, nl, cb), lambda j, b: (b, 0, j)) for _ in outs],
        out_shape=[jax.ShapeDtypeStruct((nb, nl, nc), dt) for dt in outs],
        compiler_params=_cparams(("parallel", "parallel")),
    )(*args)


def _seq_bwd(fn, *, toks, vecs, cots, tok_grads, nb, nl, nc, cb, name, bwd_fn=None):
    nt, nv = len(toks), len(vecs)
    specs, args = _seq_specs(toks, vecs, nl, cb)
    n_in = len(specs)
    cot_counts = [len(group) for group in cots]
    for group in cots:
        for arr in group:
            specs.append(pl.BlockSpec((1, nl, cb), lambda j, b: (b, 0, j)))
            args.append(arr)
    out_specs, out_shapes = [], []
    for (_, _, mult), dt in zip(toks, tok_grads):
        out_specs.append(pl.BlockSpec((1, nl, cb * mult), lambda j, b: (b, 0, j)))
        out_shapes.append(jax.ShapeDtypeStruct((nb, nl, nc * mult), dt))
    for arr, _ in vecs:
        out_specs.append(pl.BlockSpec((arr.shape[0], cb), lambda j, b: (0, j)))
        out_shapes.append(jax.ShapeDtypeStruct((arr.shape[0], nc), F32))

    def body(*refs):
        ins, os = refs[:len(specs)], refs[len(specs):]
        b = pl.program_id(1)
        tv = [r[0].astype(F32) for r in ins[:nt]]
        vv = [r[...] for r in ins[nt:n_in]]
        cts, o = [], n_in
        for cnt in cot_counts:
            ct = ins[o][0].astype(F32)
            for r in ins[o + 1:o + cnt]:
                ct = ct + r[0].astype(F32)
            cts.append(ct)
            o += cnt
        if bwd_fn is None:
            _, vjp = jax.vjp(lambda *d: tuple(fn(*d)), *tv, *vv)
            grads = vjp(tuple(cts))
        else:
            grads = bwd_fn(*tv, *vv, *cts)
        for g, o in zip(grads[:nt], os[:nt]):
            o[0] = g.astype(o.dtype)

        @pl.when(b == 0)
        def _():
            for o in os[nt:]:
                o[...] = jnp.zeros_like(o)

        for g, o in zip(grads[nt:], os[nt:]):
            o[...] += g

    out = pl.pallas_call(
        body, name=name, grid=(nc // cb, nb), in_specs=specs, out_specs=out_specs, out_shape=out_shapes,
        compiler_params=_cparams(("parallel", "arbitrary")),
    )(*args)
    return out[:nt], out[nt:]


EXP2_SCALE = ATTN_SCALE * math.log2(math.e)


HEAD_TILE = 128
N_HEAD_PAIRS = N_ATTN_HEADS // 2


def _head_lanes():
    lane = lax.broadcasted_iota(jnp.int32, (1, HEAD_TILE), 1)
    return lane < QK_NOPE_DIM, (lane >= QK_NOPE_DIM) & (lane < QK_DIM)


def _attn_specs(tq, lk):
    q = pl.BlockSpec((1, tq, 2 * HEAD_TILE), lambda b, pr, j: (b, j, pr))
    kv = pl.BlockSpec((1, lk, 2 * HEAD_TILE), lambda b, pr, j: (b, 0, pr))
    kr = pl.BlockSpec((1, lk, HEAD_TILE), lambda b, pr, j: (b, 0, 0))
    o = pl.BlockSpec((1, tq, HEAD_TILE), lambda b, pr, j: (b, j, pr))
    lse = pl.BlockSpec((1, 2, tq, 1), lambda b, pr, j: (b, pr, j, 0))
    return q, kv, kr, o, lse


def _grid_marks(nb, nj):
    b, pr, j = pl.program_id(0), pl.program_id(1), pl.program_id(2)
    first = (b == 0) & (pr == 0) & (j == 0)
    middle = (b == nb // 2) & (pr == 0) & (j == 0)
    last = (b == nb - 1) & (pr == N_HEAD_PAIRS - 1) & (j == nj - 1)
    return first, middle, last


def _attn_fwd(q, kv, kr, late_shard, *, tq, name):
    nb, s, _ = q.shape
    lk = kv.shape[1]
    nj = s // tq
    qs, kvs, krs, os_, lses = _attn_specs(tq, lk)

    def body(q_ref, kv_ref, kr_ref, shard_ref, o_ref, lse_ref, gathered_ref, send_sems, recv_sems):
        start, relay, finish = _gather_stage(shard_ref, gathered_ref, send_sems, recv_sems)
        first, middle, last = _grid_marks(nb, nj)
        pl.when(first)(start)
        pl.when(middle)(relay)
        low, _ = _head_lanes()
        outs = []
        for e in range(2):
            tile = pl.ds(HEAD_TILE * e, HEAD_TILE)
            kv_e = kv_ref[0, :, tile]
            keys = jnp.where(low, kv_e, kr_ref[0])
            sc = _dot(q_ref[0, :, tile], keys, ((1,), (1,)))
            m = jnp.max(sc, axis=-1, keepdims=True)
            p = jnp.exp2((sc - m) * EXP2_SCALE)
            denom = jnp.sum(p, axis=-1, keepdims=True)
            outs.append(_dot(p, kv_e, ((1,), (0,))) / denom)
            lse_ref[0, e] = m * EXP2_SCALE + jnp.log2(denom)
        o_ref[0] = jnp.where(low, pltpu.roll(outs[0], V_HEAD_DIM, 1), outs[1])
        pl.when(last)(finish)

    return pl.pallas_call(
        body, name=name, grid=(nb, N_HEAD_PAIRS, nj), in_specs=[qs, kvs, krs, ANY], out_specs=[os_, lses, ANY],
        out_shape=[jax.ShapeDtypeStruct((nb, s, ATTN_WIDTH), F32), jax.ShapeDtypeStruct((nb, N_ATTN_HEADS, s, 1), F32),
                   jax.ShapeDtypeStruct((N_CHIPS,) + late_shard.shape, late_shard.dtype)],
        scratch_shapes=GATHER_SEMS,
        compiler_params=_cparams(("arbitrary", "arbitrary", "arbitrary")),
    )(q, kv, kr, late_shard)


def _attn_bwd(q, kv, kr, o, lse, do, early_sums, *, tq, name):
    nb, s, _ = q.shape
    lk = kv.shape[1]
    nj = s // tq
    qs, kvs, krs, os_, lses = _attn_specs(tq, lk)

    def body(q_ref, kv_ref, kr_ref, o_ref, lse_ref, do_ref, sums_ref, dq_ref, dkv_ref, dkr_ref, scattered_ref,
             send_sems, recv_sems):
        start, finish = _scatter_stage(sums_ref, scattered_ref, send_sems, recv_sems)
        first, _, last = _grid_marks(nb, nj)
        pl.when(first)(start)
        pr, j = pl.program_id(1), pl.program_id(2)
        low, rope = _head_lanes()
        do_pair = do_ref[0]
        prod = do_pair * o_ref[0]

        @pl.when(j == 0)
        def _():
            dkv_ref[...] = jnp.zeros_like(dkv_ref)

        @pl.when((pr == 0) & (j == 0))
        def _():
            dkr_ref[...] = jnp.zeros_like(dkr_ref)

        dkr = None
        for e in range(2):
            tile = pl.ds(HEAD_TILE * e, HEAD_TILE)
            delta = jnp.sum(jnp.where(low if e == 0 else ~low, prod, 0.0), axis=-1, keepdims=True)
            do_e = jnp.where(low, 0.0, do_pair if e == 1 else pltpu.roll(do_pair, V_HEAD_DIM, 1))
            kv_e, q_e = kv_ref[0, :, tile], q_ref[0, :, tile]
            keys = jnp.where(low, kv_e, kr_ref[0])
            sc = _dot(q_e, keys, ((1,), (1,)))
            p = jnp.exp2(sc * EXP2_SCALE - lse_ref[0, e])
            dp = _dot(do_e, kv_e, ((1,), (1,)))
            ds = (p * (dp - delta)).astype(MXU_DTYPE)
            dq_ref[0, :, tile] = _dot(ds, keys, ((1,), (0,))) * ATTN_SCALE
            dkeys = _dot(ds, q_e, ((0,), (0,)))
            dv = _dot(p, do_e, ((0,), (0,)))
            dkv_ref[0, :, tile] += jnp.where(low, dkeys, dv)
            part = jnp.where(rope, dkeys, 0.0)
            dkr = part if dkr is None else dkr + part
        dkr_ref[0] += dkr

        @pl.when(j == nj - 1)
        def _():
            for e in range(2):
                tile = pl.ds(HEAD_TILE * e, HEAD_TILE)
                dkv_ref[0, :, tile] = dkv_ref[0, :, tile] * jnp.where(low, ATTN_SCALE, 1.0)

        @pl.when((pr == N_HEAD_PAIRS - 1) & (j == nj - 1))
        def _():
            dkr_ref[0] = dkr_ref[0] * ATTN_SCALE

        pl.when(last)(finish)

    return pl.pallas_call(
        body, name=name, grid=(nb, N_HEAD_PAIRS, nj), in_specs=[qs, kvs, krs, os_, lses, os_, ANY],
        out_specs=[qs, kvs, krs, ANY],
        out_shape=[jax.ShapeDtypeStruct(q.shape, F32), jax.ShapeDtypeStruct(kv.shape, F32),
                   jax.ShapeDtypeStruct(kr.shape, F32), jax.ShapeDtypeStruct(early_sums.shape, early_sums.dtype)],
        scratch_shapes=SCATTER_SEMS,
        compiler_params=_cparams(("arbitrary", "arbitrary", "arbitrary")),
    )(q, kv, kr, o, lse, do, early_sums)


N_PAIRS = HEADS_PER_GROUP // 2
PAIR_W = 2 * SSD_HEAD_DIM


def _ssd_chunk(states, xs, dtc, dtr, bm, cm, ac, ar, *, reverse):
    q = dtc.shape[0]
    assert q == PAIR_W == dtr.shape[1]
    row = lax.broadcasted_iota(jnp.int32, (q, q), 0)
    col = lax.broadcasted_iota(jnp.int32, (q, q), 1)
    if reverse:
        tri_c, tri_r, mask = col < row, row < col, col >= row
    else:
        tri_c, tri_r, mask = col <= row, row <= col, col <= row
    a_col, a_row = dtc * ac, dtr * ar
    tri_c, tri_r = tri_c.astype(F32), tri_r.astype(F32)
    cum_c = _mask_dot(tri_c, tri_r, a_col, True)
    cum_r = _mask_dot(tri_r, tri_c, a_row, False)
    tot_c = jnp.sum(a_col, axis=0, keepdims=True)
    tot_r = jnp.sum(a_row, axis=1, keepdims=True)
    cb = _dot(cm, bm, ((1,), (1,)))
    first = lax.broadcasted_iota(jnp.int32, (1, PAIR_W), 1) < SSD_HEAD_DIM
    first_rows = lax.broadcasted_iota(jnp.int32, (PAIR_W, 1), 0) < SSD_HEAD_DIM
    heads, pairs = range(HEADS_PER_GROUP), range(N_PAIRS)
    tile = [slice(PAIR_W * pr, PAIR_W * (pr + 1)) for pr in pairs]
    cum_p = [cum_c[:, tile[pr]] for pr in pairs]
    swapped = [_roll_lanes(cum_p[pr], SSD_HEAD_DIM) for pr in pairs]
    cc = [jnp.where(first, cum_p[e // 2], swapped[e // 2]) if e % 2 == 0
          else jnp.where(first, swapped[e // 2], cum_p[e // 2]) for e in heads]
    cr = [_row_of(cum_r, e) for e in heads]
    if reverse:
        within = [jnp.exp(jnp.where(mask, cr[e] - cc[e], -jnp.inf)) for e in heads]
        into = [jnp.exp(tot_c[:, tile[pr]] - cum_p[pr]) for pr in pairs]
        to_end = [jnp.exp(cum_p[pr]) for pr in pairs]
    else:
        within = [jnp.exp(jnp.where(mask, cc[e] - cr[e], -jnp.inf)) for e in heads]
        into = [jnp.exp(cum_p[pr]) for pr in pairs]
        to_end = [jnp.exp(tot_c[:, tile[pr]] - cum_p[pr]) for pr in pairs]
    decay = [cb * within[e] for e in heads]
    carry = [jnp.exp(_row_of(tot_r, e)) for e in heads]
    xd = [xs[pr] * dtc[:, tile[pr]] for pr in pairs]
    y_even = [_dot(decay[2 * pr], jnp.where(first, xd[pr], 0.0), ((1,), (0,))) for pr in pairs]
    y_odd = [_dot(decay[2 * pr + 1], jnp.where(first, 0.0, xd[pr]), ((1,), (0,))) for pr in pairs]
    y_off = [_dot(cm, states[pr], ((1,), (1,))) for pr in pairs]
    grow = [_dot(xd[pr] * to_end[pr], bm, ((0,), (0,))) for pr in pairs]
    ys = [y_even[pr] + y_odd[pr] + y_off[pr] * into[pr] for pr in pairs]
    new_states = [states[pr] * jnp.where(first_rows, carry[2 * pr], carry[2 * pr + 1]) + grow[pr] for pr in pairs]
    return tuple(ys) + tuple(new_states)


def _chunk_of_step(t, ncc, nch, reverse):
    if not reverse:
        return t
    return jnp.where(t < ncc, ncc - 1 - t, nch - 1 - (t - ncc))


X_COLS = D_INNER // SSD_GROUPS
GROUP_COLS = X_COLS + 2 * SSD_STATE


def _scan_in_specs(nch, ncc, reverse, back):
    q, e = SSD_CHUNK, HEADS_PER_GROUP

    def ch(t):
        return _chunk_of_step((nch - 1 - t) if back else t, ncc, nch, reverse)

    return ch, [
        pl.BlockSpec((1, q, GROUP_COLS), lambda b, g, t: (b, ch(t), g)),
        pl.BlockSpec((1, q, X_COLS), lambda b, g, t: (b, ch(t), g)),
        pl.BlockSpec((1, 1, e, q), lambda b, g, t: (b, g, 0, ch(t))),
        pl.BlockSpec((1, 1, X_COLS), lambda b, g, t: (g, 0, 0)),
        pl.BlockSpec((1, e, 1), lambda b, g, t: (g, 0, 0)),
        pl.BlockSpec((1, 1, X_COLS), lambda b, g, t: (g, 0, 0)),
    ]


def _scan_chunk_fn(reverse, skip):
    def f(states, xs, dtc, dtr, bm, cm, ac, ar, d):
        res = _ssd_chunk(states, xs, dtc, dtr, bm, cm, ac, ar, reverse=reverse)
        if not skip:
            return res
        ys = tuple(res[i] + d[:, PAIR_W * i:PAIR_W * (i + 1)] * xs[i] for i in range(N_PAIRS))
        return ys + tuple(res[N_PAIRS:])

    return f


def _scan_operands(x_ref, dtc_ref, dtr_ref, ac_ref, ar_ref, d_ref):
    xs = [x_ref[0, :, pl.ds(PAIR_W * i, PAIR_W)] for i in range(N_PAIRS)]
    bm = x_ref[0, :, pl.ds(X_COLS, SSD_STATE)]
    cm = x_ref[0, :, pl.ds(X_COLS + SSD_STATE, SSD_STATE)]
    return xs, dtc_ref[0], dtr_ref[0, 0], bm, cm, ac_ref[0], ar_ref[0], d_ref[0]


N_IN = 6


def _scan_fwd(dirs, *, ncc, name):
    nb, lt, _ = dirs[0][0].shape
    q, n = SSD_CHUNK, SSD_STATE
    nch = lt // q
    in_specs, out_specs, out_shapes, fs = [], [], [], []
    for dr in range(2):
        ch, specs = _scan_in_specs(nch, ncc, bool(dr), False)
        in_specs += specs
        fs.append(_scan_chunk_fn(bool(dr), dr == 0))
        out_specs += [pl.BlockSpec((1, q, X_COLS), lambda b, g, t, ch=ch: (b, ch(t), g)),
                      pl.BlockSpec((1, 1, 1, N_PAIRS, PAIR_W, n), lambda b, g, t: (b, g, t, 0, 0, 0))]
        out_shapes += [jax.ShapeDtypeStruct((nb, lt, D_INNER), F32),
                       jax.ShapeDtypeStruct((nb, SSD_GROUPS, nch, N_PAIRS, PAIR_W, n), F32)]

    def body(*refs):
        ins, outs, sts = refs[:2 * N_IN], refs[2 * N_IN:2 * N_IN + 4], refs[2 * N_IN + 4:]
        t = pl.program_id(2)

        @pl.when(t == 0)
        def _():
            for st_ref in sts:
                st_ref[...] = jnp.zeros_like(st_ref)

        entering = [[sts[dr][i] for i in range(N_PAIRS)] for dr in range(2)]
        results = [fs[dr](entering[dr], *_scan_operands(*ins[N_IN * dr:N_IN * (dr + 1)])) for dr in range(2)]
        for dr in range(2):
            (y_ref, ent_ref), st_ref = outs[2 * dr:2 * dr + 2], sts[dr]
            for i in range(N_PAIRS):
                ent_ref[0, 0, 0, i] = entering[dr][i]
                y_ref[0, :, pl.ds(PAIR_W * i, PAIR_W)] = results[dr][i]
                st_ref[i] = results[dr][N_PAIRS + i]

    out = pl.pallas_call(
        body, name=name, grid=(nb, SSD_GROUPS, nch), in_specs=in_specs, out_specs=out_specs, out_shape=out_shapes,
        scratch_shapes=[pltpu.VMEM((N_PAIRS, PAIR_W, n), F32)] * 2,
        compiler_params=_cparams(("parallel", "parallel", "arbitrary")),
    )(*dirs[0], *dirs[1])
    return out[:2], out[2:]


N_SCAN_GRADS = 6


def _scan_bwd(dirs, entering, dy, *, ncc, name):
    nb, lt, _ = dirs[0][0].shape
    q, n, e = SSD_CHUNK, SSD_STATE, HEADS_PER_GROUP
    nch = lt // q
    in_specs, out_specs, out_shapes, fs, args = [], [], [], [], []
    for dr in range(2):
        ch, specs = _scan_in_specs(nch, ncc, bool(dr), True)
        in_specs += specs + [
            pl.BlockSpec((1, 1, 1, N_PAIRS, PAIR_W, n), lambda b, g, t: (b, g, nch - 1 - t, 0, 0, 0)),
            pl.BlockSpec((1, q, X_COLS), lambda b, g, t, ch=ch: (b, ch(t), g))]
        args += list(dirs[dr]) + [entering[dr], dy]
        fs.append(_scan_chunk_fn(bool(dr), dr == 0))
        out_specs += [pl.BlockSpec((1, q, GROUP_COLS), lambda b, g, t, ch=ch: (b, ch(t), g)),
                      pl.BlockSpec((1, q, X_COLS), lambda b, g, t, ch=ch: (b, ch(t), g)),
                      pl.BlockSpec((1, 1, e, q), lambda b, g, t, ch=ch: (b, g, 0, ch(t))),
                      pl.BlockSpec((1, 1, 1, X_COLS), lambda b, g, t: (b, g, 0, 0)),
                      pl.BlockSpec((1, 1, e, 1), lambda b, g, t: (b, g, 0, 0)),
                      pl.BlockSpec((1, 1, 1, X_COLS), lambda b, g, t: (b, g, 0, 0))]
        out_shapes += [jax.ShapeDtypeStruct((nb, lt, SSD_GROUPS * GROUP_COLS), F32),
                       jax.ShapeDtypeStruct((nb, lt, D_INNER), F32), jax.ShapeDtypeStruct((nb, SSD_GROUPS, e, lt), F32),
                       jax.ShapeDtypeStruct((nb, SSD_GROUPS, 1, X_COLS), F32), jax.ShapeDtypeStruct((nb, SSD_GROUPS, e, 1), F32),
                       jax.ShapeDtypeStruct((nb, SSD_GROUPS, 1, X_COLS), F32)]
    n_in = N_IN + 2

    def body(*refs):
        ins = refs[:2 * n_in]
        outs = refs[2 * n_in:2 * n_in + 2 * N_SCAN_GRADS]
        dss = refs[2 * n_in + 2 * N_SCAN_GRADS:]
        t = pl.program_id(2)

        for dr in range(2):
            mine = ins[n_in * dr:n_in * (dr + 1)]
            ent_ref, dy_ref = mine[N_IN], mine[N_IN + 1]
            dx_ref, ddtc_ref, ddtr_ref, dac_ref, dar_ref, dd_ref = outs[N_SCAN_GRADS * dr:N_SCAN_GRADS * (dr + 1)]
            ds_ref = dss[dr]

            @pl.when(t == 0)
            def _():
                for ref in (ds_ref, dac_ref, dar_ref, dd_ref):
                    ref[...] = jnp.zeros_like(ref)

            states = [ent_ref[0, 0, 0, i] for i in range(N_PAIRS)]
            _, vjp = jax.vjp(fs[dr], states, *_scan_operands(*mine[:N_IN]))
            dys = [dy_ref[0, :, pl.ds(PAIR_W * i, PAIR_W)] for i in range(N_PAIRS)]
            gs, gx, gdtc, gdtr, gb, gc, gac, gar, gd = vjp(tuple(dys) + tuple(ds_ref[i] for i in range(N_PAIRS)))
            o = 0
            for part in list(gx) + [gb, gc]:
                dx_ref[0, :, pl.ds(o, part.shape[1])] = part
                o += part.shape[1]
            for i in range(N_PAIRS):
                ds_ref[i] = gs[i]
            ddtc_ref[0] = gdtc
            ddtr_ref[0, 0] = gdtr
            dac_ref[0, 0] += gac
            dar_ref[0, 0] += gar
            dd_ref[0, 0] += gd

    out = pl.pallas_call(
        body, name=name, grid=(nb, SSD_GROUPS, nch), in_specs=in_specs, out_specs=out_specs, out_shape=out_shapes,
        scratch_shapes=[pltpu.VMEM((N_PAIRS, PAIR_W, n), F32)] * 2,
        compiler_params=_cparams(("parallel", "parallel", "arbitrary")),
    )(*args)
    return out[:N_SCAN_GRADS], out[N_SCAN_GRADS:]


def _adamw(w, g, m, v, *, name):
    r, c = w.shape
    tr = _pick(r, (256, 176, 128, 96, 64, 8))
    c1 = 1.0 / (1.0 - ADAM_B1 ** ADAM_STEP)
    c2 = 1.0 / (1.0 - ADAM_B2 ** ADAM_STEP)

    def body(w_ref, g_ref, m_ref, v_ref, d_ref, nm_ref, nv_ref):
        gv = g_ref[...]
        nm = ADAM_B1 * m_ref[...] + (1.0 - ADAM_B1) * gv
        nv = ADAM_B2 * v_ref[...] + (1.0 - ADAM_B2) * (gv * gv)
        d_ref[...] = -ADAM_LR * ((nm * c1) / (jnp.sqrt(nv * c2) + ADAM_EPS) + ADAM_WD * w_ref[...])
        nm_ref[...] = nm
        nv_ref[...] = nv

    spec = pl.BlockSpec((tr, c), lambda i: (i, 0))
    return pl.pallas_call(
        body, name=name, grid=(r // tr,), in_specs=[spec] * 4, out_specs=[spec] * 3,
        out_shape=[jax.ShapeDtypeStruct((r, c), F32)] * 3, compiler_params=_cparams(("parallel",)),
    )(w, g, m, v)


def _sum_rows_tile(r):
    return r if r <= 1024 else _pick(r, (656, 512, 256, 128, 64, 32, 16))


def _sum_slots(x, *, out_dtype, name):
    n, r, c = x.shape
    tr = _sum_rows_tile(r)

    def body(x_ref, o_ref):
        acc = x_ref[0].astype(F32)
        for k in range(1, n):
            acc = acc + x_ref[k].astype(F32)
        o_ref[...] = acc.astype(o_ref.dtype)

    return pl.pallas_call(
        body, name=name, grid=(r // tr,), in_specs=[pl.BlockSpec((n, tr, c), lambda i: (0, i, 0))],
        out_specs=pl.BlockSpec((tr, c), lambda i: (i, 0)), out_shape=jax.ShapeDtypeStruct((r, c), out_dtype),
        compiler_params=_cparams(("parallel",)),
    )(x)


def _sum_list(xs, *, out_dtype, name):
    r, c = xs[0].shape
    tr = _sum_rows_tile(r)

    def body(*refs):
        acc = refs[0][...].astype(F32)
        for ref in refs[1:-1]:
            acc = acc + ref[...].astype(F32)
        refs[-1][...] = acc.astype(refs[-1].dtype)

    spec = pl.BlockSpec((tr, c), lambda i: (i, 0))
    return pl.pallas_call(
        body, name=name, grid=(r // tr,), in_specs=[spec] * len(xs), out_specs=spec,
        out_shape=jax.ShapeDtypeStruct((r, c), out_dtype), compiler_params=_cparams(("parallel",)),
    )(*xs)


ANY = pl.BlockSpec(memory_space=pl.ANY)


def _place():
    return lax.axis_index("x"), lax.axis_index("y"), lax.axis_index("c")


def _allgather_small(v, *, name):
    r, c = v.shape

    def body(v_ref, out_ref, send_sems, recv_sems, local_sem):
        x, y, cc = _place()
        me = 4 * x + 2 * y + cc
        mine = pltpu.make_async_copy(v_ref, out_ref.at[me], local_sem)
        mine.start()
        copies = []
        for k in range(1, N_DEV):
            fx, fy, fc = (k >> 2) & 1, (k >> 1) & 1, k & 1
            peer = (1 - x if fx else x, 1 - y if fy else y, 1 - cc if fc else cc)
            copies.append(pltpu.make_async_remote_copy(
                src_ref=v_ref, dst_ref=out_ref.at[me], send_sem=send_sems.at[k - 1], recv_sem=recv_sems.at[k - 1],
                device_id=peer, device_id_type=MESH))
        for cp in copies:
            cp.start()
        for cp in copies:
            cp.wait()
        mine.wait()

    return pl.pallas_call(
        body, name=name, in_specs=[ANY], out_specs=ANY, out_shape=jax.ShapeDtypeStruct((N_DEV, r, c), v.dtype),
        scratch_shapes=[pltpu.SemaphoreType.DMA((N_DEV - 1,)), pltpu.SemaphoreType.DMA((N_DEV - 1,)),
                        pltpu.SemaphoreType.DMA],
    )(v)


def _other_chips(x, y):
    return [(1 - x, y), (x, 1 - y), (1 - x, 1 - y)]


GATHER_SEMS = [pltpu.SemaphoreType.DMA((6,)), pltpu.SemaphoreType.DMA((6,))]
SCATTER_SEMS = [pltpu.SemaphoreType.DMA((3,)), pltpu.SemaphoreType.DMA((3,))]


def _gather_stage(v_ref, out_ref, send_sems, recv_sems):
    half = v_ref.shape[0] // 2
    x, y, cc = _place()
    sibling = (x, y, 1 - cc)
    chips = _other_chips(x, y)

    def rows(px, py, pc):
        return out_ref.at[2 * px + py, pl.ds(pc * half, half), :]

    def copy(k, block, to, src=None):
        return pltpu.make_async_remote_copy(
            src_ref=rows(*block) if src is None else src, dst_ref=rows(*block),
            send_sem=send_sems.at[k], recv_sem=recv_sems.at[k], device_id=to, device_id_type=MESH)

    my_half = v_ref.at[pl.ds(cc * half, half), :]
    first = [copy(j, (x, y, cc), (*chip, cc), src=my_half) for j, chip in enumerate(chips)]
    passed = [copy(3 + j, (*chip, cc), sibling) for j, chip in enumerate(chips)]

    def start():
        for cp in first:
            cp.start()

    def relay():
        for j, chip in enumerate(chips):
            copy(j, (*chip, cc), (x, y, cc)).wait_recv()
            passed[j].start()

    def finish():
        for j, chip in enumerate(chips):
            copy(3 + j, (*chip, 1 - cc), (x, y, cc)).wait_recv()
        for cp in first + passed:
            cp.wait_send()

    return start, relay, finish


def _gather_shards(mine, *, name):
    r, c = mine.shape

    def body(v_ref, out_ref, send_sems, recv_sems):
        for phase in _gather_stage(v_ref, out_ref, send_sems, recv_sems):
            phase()

    return pl.pallas_call(
        body, name=name, in_specs=[ANY], out_specs=ANY, out_shape=jax.ShapeDtypeStruct((N_CHIPS, r, c), mine.dtype),
        scratch_shapes=GATHER_SEMS,
    )(mine)


def _swap_halves(g, *, name):
    n, _, r, c = g.shape

    def body(g_ref, got_ref, send_sems, recv_sems):
        x, y, cc = _place()
        sibling = (x, y, 1 - cc)
        rems = []
        for j in range(n):
            rems.append(pltpu.make_async_remote_copy(
                src_ref=g_ref.at[j, 1 - cc], dst_ref=got_ref.at[j], send_sem=send_sems.at[j],
                recv_sem=recv_sems.at[j], device_id=sibling, device_id_type=MESH))
        for cp in rems:
            cp.start()
        for cp in rems:
            cp.wait()

    return pl.pallas_call(
        body, name=name, in_specs=[ANY], out_specs=ANY, out_shape=jax.ShapeDtypeStruct((n, r, c), g.dtype),
        scratch_shapes=[pltpu.SemaphoreType.DMA((n,)), pltpu.SemaphoreType.DMA((n,))],
    )(g)


def _scatter_stage(s_ref, out_ref, send_sems, recv_sems):
    x, y, cc = _place()
    me = 2 * x + y
    copies = [pltpu.make_async_remote_copy(
        src_ref=s_ref.at[2 * px + py], dst_ref=out_ref.at[me], send_sem=send_sems.at[j], recv_sem=recv_sems.at[j],
        device_id=(px, py, cc), device_id_type=MESH) for j, (px, py) in enumerate(_other_chips(x, y))]

    def start():
        for cp in copies:
            cp.start()

    def finish():
        for cp in copies:
            cp.wait()

    return start, finish


def _scatter_to_chips(s, *, name):
    def body(s_ref, out_ref, send_sems, recv_sems):
        for phase in _scatter_stage(s_ref, out_ref, send_sems, recv_sems):
            phase()

    return pl.pallas_call(
        body, name=name, in_specs=[ANY], out_specs=ANY, out_shape=jax.ShapeDtypeStruct(s.shape, s.dtype),
        scratch_shapes=SCATTER_SEMS,
    )(s)


def _join_halves(f, *, name):
    r, c = f.shape

    def body(f_ref, out_ref, send_sem, recv_sem):
        x, y, cc = _place()
        cp = pltpu.make_async_remote_copy(src_ref=f_ref, dst_ref=out_ref.at[cc], send_sem=send_sem, recv_sem=recv_sem,
                                          device_id=(x, y, 1 - cc), device_id_type=MESH)
        cp.start()
        cp.wait()

    return pl.pallas_call(
        body, name=name, in_specs=[ANY], out_specs=ANY, out_shape=jax.ShapeDtypeStruct((2, r, c), f.dtype),
        scratch_shapes=[pltpu.SemaphoreType.DMA, pltpu.SemaphoreType.DMA],
    )(f)


def _pack_rows(parts, width=PACK_COLS):
    return jnp.concatenate([p.reshape(-1, width) for p in parts], axis=0)


def _pack_small(parts, rows):
    flat = jnp.concatenate([p.reshape(-1).astype(F32) for p in parts])
    return jnp.pad(flat, (0, rows * LANES - flat.shape[0])).reshape(rows, LANES)


def _unpack_small(packed, shapes):
    flat = packed.reshape(-1)
    out, o = [], 0
    for shp in shapes:
        n = int(np.prod(shp))
        out.append(flat[o:o + n].reshape(shp))
        o += n
    return out


def _perm_in_cols(w):
    a, b = Q_LORA_RANK + KV_LORA_RANK, Q_LORA_RANK + KV_LORA_RANK + QK_ROPE_DIM
    c = IN_WIDTH - 2 * N_SSD_HEADS
    return jnp.concatenate([w[:, :a], w[:, b:c], w[:, a:b], w[:, c:]], axis=1)


def _unperm_in_cols(w):
    a = Q_LORA_RANK + KV_LORA_RANK
    zx = D_INNER + XBC_WIDTH
    return jnp.concatenate([w[:, :a], w[:, a + zx:a + zx + QK_ROPE_DIM], w[:, a:a + zx], w[:, a + zx + QK_ROPE_DIM:]],
                           axis=1)


def _group_xbc(a):
    n = SSD_STATE
    parts = []
    for g in range(SSD_GROUPS):
        parts += [a[..., g * X_COLS:(g + 1) * X_COLS], a[..., D_INNER + g * n:D_INNER + (g + 1) * n],
                  a[..., D_INNER + GN + g * n:D_INNER + GN + (g + 1) * n]]
    return jnp.concatenate(parts, axis=-1)


def _ungroup_xbc(a):
    n = SSD_STATE
    xs = [a[..., g * GROUP_COLS:g * GROUP_COLS + X_COLS] for g in range(SSD_GROUPS)]
    bs = [a[..., g * GROUP_COLS + X_COLS:g * GROUP_COLS + X_COLS + n] for g in range(SSD_GROUPS)]
    cs = [a[..., g * GROUP_COLS + X_COLS + n:(g + 1) * GROUP_COLS] for g in range(SSD_GROUPS)]
    return jnp.concatenate(xs + bs + cs, axis=-1)


UP_BLOCK = 256


def _interleave_up(w):
    parts = []
    for j in range(D_FF // UP_BLOCK):
        parts += [w[:, j * UP_BLOCK:(j + 1) * UP_BLOCK], w[:, D_FF + j * UP_BLOCK:D_FF + (j + 1) * UP_BLOCK]]
    return jnp.concatenate(parts, axis=1)


def _deinterleave_up(w):
    blocks = [w[:, j * UP_BLOCK:(j + 1) * UP_BLOCK] for j in range(2 * D_FF // UP_BLOCK)]
    return jnp.concatenate(blocks[0::2] + blocks[1::2], axis=1)


def _pad_q_heads(w):
    k = w.shape[0]
    return jnp.pad(w.reshape(k, N_ATTN_HEADS, QK_DIM), ((0, 0), (0, 0), (0, HEAD_TILE - QK_DIM))).reshape(k, -1)


def _unpad_q_heads(w):
    k = w.shape[0]
    return w.reshape(k, N_ATTN_HEADS, HEAD_TILE)[..., :QK_DIM].reshape(k, N_ATTN_HEADS * QK_DIM)


def _rope_tables(seq_len):
    n_rows = seq_len // GRID_W
    row = jnp.repeat(jnp.arange(n_rows), GRID_W).astype(F32)
    col = jnp.tile(jnp.arange(GRID_W), n_rows).astype(F32)
    axis_dim = QK_ROPE_DIM // 2
    inv_freq = ROPE_THETA ** (-jnp.arange(0, axis_dim, 2, dtype=F32) / axis_dim)
    ang_r = row[:, None] * inv_freq
    ang_c = col[:, None] * inv_freq
    ang = jnp.concatenate([ang_r, ang_r, ang_c, ang_c], axis=-1)
    return jnp.cos(ang), jnp.sin(ang)


def _rot_matrix(width, start):
    r = np.zeros((width, width), np.float32)
    quarter = QK_ROPE_DIM // 4
    for base in (0, QK_ROPE_DIM // 2):
        for i in range(quarter):
            r[start + base + quarter + i, start + base + i] = -1.0
            r[start + base + i, start + base + quarter + i] = 1.0
    return jnp.asarray(r)


ROPE_STEP = QK_ROPE_DIM // 4


def _rope_flat_fn(x, cos, sin_up, sin_down):
    reps = x.shape[1] // cos.shape[1]

    def heads(t):
        return jnp.concatenate([t] * reps, axis=1)

    return (x * heads(cos) + _roll_lanes(x, -ROPE_STEP) * heads(sin_up) + _roll_lanes(x, ROPE_STEP) * heads(sin_down),)


def _rope_flat_transpose_fn(g, cos, sin_up, sin_down):
    reps = g.shape[1] // cos.shape[1]

    def heads(t):
        return jnp.concatenate([t] * reps, axis=1)

    return (g * heads(cos) + _roll_lanes(g * heads(sin_up), ROPE_STEP) + _roll_lanes(g * heads(sin_down), -ROPE_STEP),)


def _krdt_fn(x, cos, sin, rot, bias):
    lane = lax.broadcasted_iota(jnp.int32, (1, KRDT_WIDTH), 1)
    is_dt = (lane >= QK_ROPE_DIM) & (lane < QK_ROPE_DIM + 2 * N_SSD_HEADS)
    roped = x * cos + _dot_exact(x, rot) * sin
    return (jnp.where(is_dt, _softplus(x + bias), roped),)


def _pre_fn(u, w, shift, scale):
    return (_rms(u, w) * (1.0 + scale) + shift,)


def _norm_fn(x, w):
    return (_rms(x, w),)


def _finish_fn(yf, yb, z, w):
    return (_rms((yf + yb) * _silu(z), w),)


def _mid_fn(x, mix, w_post, w_pre, gate, shift, scale):
    x1 = x + gate * _rms(mix, w_post)
    return (x1, _rms(x1, w_pre) * (1.0 + scale) + shift)


def _loss_fn(x1, ffn, tgt, w_post, gate):
    y = x1 + gate * _rms(ffn, w_post)
    err = y - tgt
    return (0.5 * jnp.mean(err * err, axis=-1, keepdims=True),)


def _bias_fn(x, b):
    return (x + b,)


def _silu_fn(x):
    return (_silu(x),)


def kernel(x, c, ctx, c_ctx, w_mod, b_mod, mix_pre_norm, mix_post_norm, w_in, q_norm, w_q_up, kv_norm, w_kv_up, ssd_conv_w, ssd_conv_b, ssd_a_log, ssd_dt_bias, ssd_d, ssd_norm, w_out, ffn_pre_norm, ffn_post_norm, w_up, ffn_conv_w, ffn_conv_b, w_down, loss_target, m_c_ctx, m_w_mod, m_b_mod, m_mix_pre_norm, m_mix_post_norm, m_w_in, m_q_norm, m_w_q_up, m_kv_norm, m_w_kv_up, m_ssd_conv_w, m_ssd_conv_b, m_ssd_a_log, m_ssd_dt_bias, m_ssd_d, m_ssd_norm, m_w_out, m_ffn_pre_norm, m_ffn_post_norm, m_w_up, m_ffn_conv_w, m_ffn_conv_b, m_w_down, v_c_ctx, v_w_mod, v_b_mod, v_mix_pre_norm, v_mix_post_norm, v_w_in, v_q_norm, v_w_q_up, v_kv_norm, v_w_kv_up, v_ssd_conv_w, v_ssd_conv_b, v_ssd_a_log, v_ssd_dt_bias, v_ssd_d, v_ssd_norm, v_w_out, v_ffn_pre_norm, v_ffn_post_norm, v_w_up, v_ffn_conv_w, v_ffn_conv_b, v_w_down):
    args = dict(locals())
    names = ["c_ctx", "w_mod", "b_mod", "mix_pre_norm", "mix_post_norm", "w_in", "q_norm", "w_q_up", "kv_norm",
             "w_kv_up", "ssd_conv_w", "ssd_conv_b", "ssd_a_log", "ssd_dt_bias", "ssd_d", "ssd_norm", "w_out",
             "ffn_pre_norm", "ffn_post_norm", "w_up", "ffn_conv_w", "ffn_conv_b", "w_down"]
    nb, s, d = x.shape
    nctx_rows = ctx.shape[1]
    lt = nctx_rows + s
    tl = 256 if (nctx_rows % 256 == 0 and s % 256 == 0) else 128
    nctx = nctx_rows // tl
    ncc = nctx_rows // SSD_CHUNK
    h, e, g2 = N_ATTN_HEADS, HEADS_PER_GROUP, SSD_GROUPS
    chip = 2 * lax.axis_index("x") + lax.axis_index("y")

    big_local = {n: args[n][0] for n, _, _, _ in BIG}
    big_info = {n: (rows, cols, axis) for n, rows, cols, axis in BIG}

    def pack_shards(group):
        return _pack_rows([big_local[n].astype(WIRE_DTYPE) for n in group])

    def unpack_gathered(gathered, mine, group):
        gathered = lax.dynamic_update_slice(gathered, mine[None], (chip, 0, 0))
        res, o = {}, 0
        for n in group:
            rows, cols, axis = big_info[n]
            lr, lc = big_local[n].shape
            nr = lr * lc // PACK_COLS
            seg = gathered[:, o:o + nr].reshape(N_CHIPS, lr, lc)
            o += nr
            res[n] = seg.reshape(rows, cols) if axis == 0 else jnp.transpose(seg, (1, 0, 2)).reshape(rows, cols)
        return res

    core = lax.axis_index("c")

    def pair_sums(grads_full, group, tag):
        parts = []
        for n in group:
            _, _, axis = big_info[n]
            lr, lc = big_local[n].shape
            gfull = grads_full[n]
            shards = (gfull.reshape(N_CHIPS, lr, lc) if axis == 0
                      else jnp.transpose(gfull.reshape(lr, N_CHIPS, lc), (1, 0, 2)))
            parts.append(shards.reshape(N_CHIPS, lr * lc // PACK_COLS, PACK_COLS))
        gpack = jnp.concatenate(parts, axis=1).astype(WIRE_DTYPE)
        half = gpack.shape[1] // 2
        gpack = gpack.reshape(N_CHIPS, 2, half, PACK_COLS)
        got = _swap_halves(gpack, name="grad_swap_" + tag)
        own = lax.dynamic_index_in_dim(gpack, core, axis=1, keepdims=False)
        flat = (N_CHIPS * half, PACK_COLS)
        return _sum_list([own.reshape(flat), got.reshape(flat)], out_dtype=WIRE_DTYPE,
                         name="grad_add_pair_" + tag).reshape(N_CHIPS, half, PACK_COLS)

    def chip_total(sums, scattered, tag):
        mine_sum = lax.dynamic_index_in_dim(sums, chip, axis=0, keepdims=True)
        scattered = lax.dynamic_update_slice(scattered, mine_sum, (chip, 0, 0))
        return _sum_slots(scattered, out_dtype=F32, name="grad_add_chips_" + tag)

    packed_now, packed_late = pack_shards(GATHER_NOW), pack_shards(GATHER_LATE)
    full = unpack_gathered(_gather_shards(packed_now, name="gather_weights"), packed_now, GATHER_NOW)
    n_sc, n_fc = ssd_conv_w.shape[2], ffn_conv_w.shape[2]
    n_conv = SSD_CONV * n_sc + FFN_CONV * n_fc
    first_rows = -(-(n_conv + nb * d) // (8 * LANES)) * 8
    first_all = _allgather_small(_pack_small([ssd_conv_w[0], ffn_conv_w[0], c], first_rows), name="gather_conv_c")
    first_all = first_all.reshape(N_DEV, -1)
    conv_all = first_all[::2]
    ssd_conv_full = jnp.concatenate(
        [conv_all[j][:SSD_CONV * n_sc].reshape(SSD_CONV, n_sc) for j in range(N_CHIPS)], axis=1)
    ffn_conv_full = jnp.concatenate(
        [conv_all[j][SSD_CONV * n_sc:n_conv].reshape(FFN_CONV, n_fc) for j in range(N_CHIPS)], axis=1)
    c_every = first_all[:, n_conv:n_conv + nb * d].reshape(N_DEV * nb, d)

    w_in_p = _perm_in_cols(full["w_in"])
    o_cq, o_ckv, o_z = 0, Q_LORA_RANK, Q_LORA_RANK + KV_LORA_RANK
    o_xbc, o_kr = o_z + D_INNER, o_z + D_INNER + XBC_WIDTH
    w_krdt = jnp.pad(w_in_p[:, o_kr:], ((0, 0), (0, KRDT_WIDTH - QK_ROPE_DIM - 2 * N_SSD_HEADS)))
    w_segs = [w_in_p[:, o_cq:o_ckv], w_in_p[:, o_ckv:o_z], w_in_p[:, o_z:o_xbc], _group_xbc(w_in_p[:, o_xbc:o_kr]),
              w_krdt]
    ssd_conv_g, ssd_conv_b_g = _group_xbc(ssd_conv_full), _group_xbc(ssd_conv_b)
    w_q_pad = _pad_q_heads(full["w_q_up"])

    mod_rows = 16
    n_ex = N_DEV * nb
    all_rows = -(-(n_ex + 1) // 16) * 16
    me = 2 * chip + lax.axis_index("c")
    c_all = jnp.concatenate([c_every, c_ctx[None, :], jnp.zeros((all_rows - n_ex - 1, d), F32)], axis=0)[None]
    (s_all,) = _row_fwd(_silu_fn, toks=[(c_all, 0, None, 0)], outs=[(d, F32)], nb=1, nl=all_rows, tl=all_rows,
                        name="mod_silu")
    w_mod_local = w_mod[0]
    mod_cols = w_mod_local.shape[1]
    mod_part = _mm(s_all[0], w_mod_local, name="mod_mm")
    mod_parts = _allgather_small(mod_part, name="gather_mod")[::2]
    mod_every = jnp.concatenate([mod_parts[j] for j in range(N_CHIPS)], axis=1)
    mod_lin = jnp.concatenate([lax.dynamic_slice_in_dim(mod_every, me * nb, nb, axis=0), mod_every[n_ex:n_ex + 1],
                               jnp.zeros((mod_rows - nb - 1, N_MOD * d), F32)], axis=0)
    (mod,) = _row_fwd(_bias_fn, toks=[(mod_lin[None], 0, None, 0)], vecs=[b_mod], outs=[(N_MOD * d, F32)], nb=1,
                      nl=mod_rows, tl=mod_rows, name="mod_bias")
    mods = [mod[0][:, k * d:(k + 1) * d][:, None, :] for k in range(N_MOD)]
    mods_lat = [m[:nb] for m in mods]

    u = jnp.concatenate([ctx, x], axis=1)
    (h1,) = _row_fwd(_pre_fn, toks=[(u, 0, None, 0)], vecs=[mix_pre_norm], bvecs=[mods[0], mods[1]],
                     outs=[(d, MXU_DTYPE)], nb=nb, nl=lt, tl=tl, nctx=nctx, name="pre1")
    h1f = h1.reshape(nb * lt, d)
    p_cq, p_ckv, p_z, p_xbc, p_krdt = [
        _mm(h1f, w, name="in_" + nm).reshape(nb, lt, -1)
        for nm, w in zip(("cq", "ckv", "z", "xbc", "krdt"), w_segs)]

    (cqn,) = _row_fwd(_norm_fn, toks=[(p_cq, nctx, None, 0)], vecs=[q_norm], outs=[(Q_LORA_RANK, MXU_DTYPE)],
                      nb=nb, nl=s, tl=tl, name="q_norm")
    q_flat = _mm(cqn.reshape(nb * s, -1), w_q_pad, name="q_up").reshape(nb, s, h * HEAD_TILE)
    cos, sin = _rope_tables(s)
    ones, zeros = jnp.ones((s, QK_NOPE_DIM), F32), jnp.zeros((s, QK_NOPE_DIM), F32)
    tail = HEAD_TILE - QK_DIM
    up_lanes = ((jnp.arange(QK_ROPE_DIM) // ROPE_STEP) % 2 == 0)[None, :]
    q_tables = [jnp.concatenate([pad, t, pad[:, :tail]], axis=1)[None]
                for pad, t in ((ones, cos), (zeros, jnp.where(up_lanes, -sin, 0.0)), (zeros, jnp.where(up_lanes, 0.0, sin)))]
    tq = 256
    (q_roped,) = _row_fwd(_rope_flat_fn, toks=[(q_flat, 0, None, 0)], poss=q_tables, outs=[(h * HEAD_TILE, MXU_DTYPE)],
                          nb=nb, nl=s, tl=tl, name="rope_q")

    (ckvn,) = _row_fwd(_norm_fn, toks=[(p_ckv, 0, None, 0)], vecs=[kv_norm], outs=[(KV_LORA_RANK, MXU_DTYPE)],
                       nb=nb, nl=lt, tl=tl, name="kv_norm")
    kv_flat = _mm(ckvn.reshape(nb * lt, -1), full["w_kv_up"], out_dtype=MXU_DTYPE, name="kv_up").reshape(nb, lt, -1)

    pad_w = KRDT_WIDTH - QK_ROPE_DIM
    cos_k = jnp.concatenate([jnp.ones((nctx_rows, KRDT_WIDTH), F32),
                             jnp.concatenate([cos, jnp.ones((s, pad_w), F32)], axis=1)], axis=0)[None]
    sin_k = jnp.concatenate([jnp.zeros((nctx_rows, KRDT_WIDTH), F32),
                             jnp.concatenate([sin, jnp.zeros((s, pad_w), F32)], axis=1)], axis=0)[None]
    rot_k = _rot_matrix(KRDT_WIDTH, 0)
    dt_bias_row = jnp.pad(ssd_dt_bias.reshape(1, -1), ((0, 0), (QK_ROPE_DIM, pad_w - 2 * N_SSD_HEADS)))
    (krdt,) = _row_fwd(_krdt_fn, toks=[(p_krdt, 0, None, 0)], poss=[cos_k, sin_k], vecs=[rot_k, dt_bias_row],
                       outs=[(KRDT_WIDTH, F32)], nb=nb, nl=lt, tl=tl, name="krdt")
    kr = jnp.pad(krdt[..., :QK_ROPE_DIM].astype(MXU_DTYPE), ((0, 0), (0, 0), (QK_NOPE_DIM, HEAD_TILE - QK_DIM)))
    attn, lse, gathered_late = _attn_fwd(q_roped, kv_flat, kr, packed_late, tq=tq, name="attn_fwd")
    full.update(unpack_gathered(gathered_late, packed_late, GATHER_LATE))
    w_up_il = _interleave_up(full["w_up"])
    w_out_a, w_out_s = full["w_out"][:ATTN_WIDTH], full["w_out"][ATTN_WIDTH:]

    seg = nctx_rows

    def conv_ssd_fn(xv, w, b):
        return (_silu(_dwconv(xv, w, seg) + b),)

    def conv_ssd_bwd(xv, w, b, dy):
        cv = _dwconv(xv, w, seg) + b
        sg = _sigmoid(cv)
        dc = dy * (sg * (1.0 + cv * (1.0 - sg)))
        dx, dw = _dwconv_back(xv, dc, w, seg)
        return dx, dw, jnp.sum(dc, axis=0, keepdims=True)

    cb_ssd = 256
    conv_vecs = [(ssd_conv_g, 0), (ssd_conv_b_g, 0)]
    (xbc,) = _seq_fwd(conv_ssd_fn, toks=[(p_xbc, 0, 1)], vecs=conv_vecs, outs=[F32], nb=nb, nl=lt, nc=XBC_WIDTH,
                      cb=cb_ssd, name="conv_ssd")
    dt = krdt[..., QK_ROPE_DIM:QK_ROPE_DIM + 2 * N_SSD_HEADS].reshape(nb, lt, 2, g2, e)
    dt_lane = QK_ROPE_DIM + N_SSD_HEADS * jnp.arange(2)[:, None, None] + jnp.arange(D_INNER)[None, None, :] // SSD_HEAD_DIM
    spread = (jnp.arange(KRDT_WIDTH)[None, :, None] == dt_lane).astype(F32)

    def spread_fn(v, s0, s1):
        return (_mask_dot_raw(s0, v, False), _mask_dot_raw(s1, v, False))

    dtc = _row_fwd(spread_fn, toks=[(krdt, 0, None, 0)], vecs=[spread[0], spread[1]],
                   outs=[(D_INNER, F32), (D_INNER, F32)], nb=nb, nl=lt, tl=tl, name="dt_spread")
    dtr = jnp.transpose(dt, (2, 0, 3, 4, 1))
    a_neg = -jnp.exp(ssd_a_log[0]).reshape(2, g2, e)
    d_chan = jnp.repeat(ssd_d[0], SSD_HEAD_DIM).reshape(g2, 1, X_COLS)
    a_chan = [jnp.repeat(a_neg[dr].reshape(-1), SSD_HEAD_DIM).reshape(g2, 1, X_COLS) for dr in range(2)]
    scan_args = [(xbc, dtc[dr], dtr[dr], a_chan[dr], a_neg[dr][:, :, None], d_chan) for dr in range(2)]
    (y0, ent0), (y1, ent1) = _scan_fwd(scan_args, ncc=ncc, name="scan_fwd")
    ys, ents = [y0, y1], [ent0, ent1]
    (ssd,) = _row_fwd(_finish_fn, toks=[(ys[0], nctx, None, 0), (ys[1], nctx, None, 0), (p_z, nctx, None, 0)],
                      vecs=[ssd_norm], outs=[(D_INNER, MXU_DTYPE)], nb=nb, nl=s, tl=tl, name="ssd_finish")

    attn_f, ssd_f = attn.reshape(nb * s, ATTN_WIDTH), ssd.reshape(nb * s, D_INNER)
    mix = _mm_sum([(attn_f, w_out_a), (ssd_f, w_out_s)], name="out_proj").reshape(nb, s, d)

    mid_bvecs = [mods_lat[2], mods_lat[3], mods_lat[4]]
    x1, h2 = _row_fwd(_mid_fn, toks=[(x, 0, None, 0), (mix, 0, None, 0)], vecs=[mix_post_norm, ffn_pre_norm],
                      bvecs=mid_bvecs, outs=[(d, F32), (d, MXU_DTYPE)], nb=nb, nl=s, tl=tl, name="mid")
    up = _mm(h2.reshape(nb * s, d), w_up_il, name="ffn_up").reshape(nb, s, 2 * D_FF)

    def glu_fn(gv, w, b):
        return (_gelu(_dwconv(gv[:, :UP_BLOCK], w, 0) + b) * gv[:, UP_BLOCK:],)

    def glu_bwd(gv, w, b, da):
        gate, val = gv[:, :UP_BLOCK], gv[:, UP_BLOCK:]
        cv = _dwconv(gate, w, 0) + b
        cdf = 0.5 * (1.0 + lax.erf(cv * (2.0 ** -0.5)))
        pdf = jnp.exp(-0.5 * cv * cv) * (1.0 / math.sqrt(2.0 * math.pi))
        dc = (da * val) * (cdf + cv * pdf)
        dgate, dw = _dwconv_back(gate, dc, w, 0)
        return jnp.concatenate([dgate, da * (cv * cdf)], axis=1), dw, jnp.sum(dc, axis=0, keepdims=True)

    cb_ffn = UP_BLOCK
    glu_toks = [(up, 0, 2)]
    glu_vecs = [(ffn_conv_full, 0), (ffn_conv_b, 0)]
    (act,) = _seq_fwd(glu_fn, toks=glu_toks, vecs=glu_vecs, outs=[MXU_DTYPE], nb=nb, nl=s, nc=D_FF, cb=cb_ffn,
                      name="conv_glu")
    ffn = _mm(act.reshape(nb * s, D_FF), full["w_down"], name="ffn_down").reshape(nb, s, d)

    loss_toks = [(x1, 0, None, 0), (ffn, 0, None, 0), (loss_target, 0, None, 0)]
    ones_rows = jnp.ones((nb, s, 1), F32)
    (dx1_a, dffn, _), (g_ffn_post,), ((g_gate5, _),), (loss_rows,) = _row_bwd(
        _loss_fn, toks=loss_toks, vecs=[ffn_post_norm], bvecs=[mods_lat[5]], cots=[(ones_rows, 0)],
        tok_grads=[F32, MXU_DTYPE, None], emit=[(0, 1, F32)], nb=nb, nl=s, tl=tl, name="loss_bwd")
    loss_part = jnp.sum(loss_rows)

    dffn_f = dffn.reshape(nb * s, d)
    g_w_down = _mm(act.reshape(nb * s, D_FF), dffn_f, ta=True, name="wg_down")
    dact = _mm(dffn_f, full["w_down"], tb=True, out_dtype=MXU_DTYPE, name="dg_down").reshape(nb, s, D_FF)
    (dup,), (g_ffn_conv_w, g_ffn_conv_b) = _seq_bwd(
        glu_fn, toks=glu_toks, vecs=glu_vecs, cots=[[dact]], tok_grads=[MXU_DTYPE], nb=nb, nl=s, nc=D_FF,
        cb=cb_ffn, name="conv_glu_bwd", bwd_fn=glu_bwd)
    dup = dup.reshape(nb * s, 2 * D_FF)
    g_w_up = _deinterleave_up(_mm(h2.reshape(nb * s, d), dup, ta=True, name="wg_up"))
    dh2 = _mm(dup, w_up_il, tb=True, name="dg_up").reshape(nb, s, d)

    (dx_res, dmix), (g_mix_post, g_ffn_pre), ((g_gate2, _), (g_shift3, _), (g_scale4, _)), _ = _row_bwd(
        _mid_fn, toks=[(x, 0, None, 0), (mix, 0, None, 0)], vecs=[mix_post_norm, ffn_pre_norm], bvecs=mid_bvecs,
        cots=[(dx1_a, 0), (dh2, 0)], tok_grads=[F32, MXU_DTYPE], nb=nb, nl=s, tl=tl, name="mid_bwd")

    dmix_f = dmix.reshape(nb * s, d)
    g_w_out = jnp.concatenate([_mm(attn_f, dmix_f, ta=True, name="wg_out_attn"),
                               _mm(ssd_f, dmix_f, ta=True, name="wg_out_ssd")], axis=0)
    early_sums = pair_sums({"w_up": g_w_up, "w_down": g_w_down, "w_out": g_w_out}, REDUCE_EARLY, "early")
    dattn = _mm(dmix_f, w_out_a, tb=True, name="dg_out_attn").reshape(nb, s, ATTN_WIDTH)
    dssd = _mm(dmix_f, w_out_s, tb=True, name="dg_out_ssd").reshape(nb, s, D_INNER)

    (dy, _, dz), (g_ssd_norm,), _, _ = _row_bwd(
        _finish_fn, toks=[(ys[0], 0, None, 0), (ys[1], 0, None, 0), (p_z, 0, None, 0)], vecs=[ssd_norm],
        cots=[(dssd, -nctx)], tok_grads=[F32, None, MXU_DTYPE], nb=nb, nl=lt, tl=tl, name="ssd_finish_bwd")
    scan_grads = _scan_bwd(scan_args, ents, dy, ncc=ncc, name="scan_bwd")

    def collect_fn(g0, g1, c0, c1):
        return (_mask_dot_raw(c0, g0, False) + _mask_dot_raw(c1, g1, False),)

    (g_dt_lanes,) = _row_fwd(collect_fn, toks=[(scan_grads[0][1], 0, None, 0), (scan_grads[1][1], 0, None, 0)],
                             vecs=[spread[0].T, spread[1].T], outs=[(KRDT_WIDTH, F32)], nb=nb, nl=lt, tl=tl,
                             name="dt_collect")
    g_dt_dirs, g_a = [], []
    for _, _, gdtr, gac, gar, _ in scan_grads:
        g_dt_dirs.append(jnp.transpose(gdtr, (0, 3, 1, 2)))
        g_a.append(jnp.sum(jnp.sum(gac.reshape(nb, g2, e, SSD_HEAD_DIM), axis=-1) + gar[:, :, :, 0], axis=0))
    g_d_chan = jnp.sum(scan_grads[0][5], axis=0)
    g_a_log = (jnp.stack(g_a) * a_neg).reshape(1, 2, N_SSD_HEADS)
    g_dt = (jnp.stack(g_dt_dirs, axis=2).reshape(nb, lt, 2 * N_SSD_HEADS)
            + g_dt_lanes[..., QK_ROPE_DIM:QK_ROPE_DIM + 2 * N_SSD_HEADS])
    (dp_xbc,), (g_ssd_conv_w, g_ssd_conv_b) = _seq_bwd(
        conv_ssd_fn, toks=[(p_xbc, 0, 1)], vecs=conv_vecs, cots=[[scan_grads[0][0], scan_grads[1][0]]],
        tok_grads=[MXU_DTYPE], nb=nb, nl=lt, nc=XBC_WIDTH, cb=cb_ssd, name="conv_ssd_bwd", bwd_fn=conv_ssd_bwd)
    g_ssd_conv_w, g_ssd_conv_b = _ungroup_xbc(g_ssd_conv_w), _ungroup_xbc(g_ssd_conv_b)

    dq_roped, dkv, dkr, early_scattered = _attn_bwd(q_roped, kv_flat, kr, attn, lse, dattn, early_sums, tq=tq,
                                                    name="attn_bwd")
    (dq_flat,) = _row_fwd(_rope_flat_transpose_fn, toks=[(dq_roped, 0, None, 0)], poss=q_tables,
                          outs=[(h * HEAD_TILE, MXU_DTYPE)], nb=nb, nl=s, tl=tl, name="rope_q_bwd")
    dq_flat = dq_flat.reshape(nb * s, h * HEAD_TILE)
    g_w_q_up = _unpad_q_heads(_mm(cqn.reshape(nb * s, -1), dq_flat, ta=True, name="wg_q_up"))
    dcqn = _mm(dq_flat, w_q_pad, tb=True, name="dg_q_up").reshape(nb, s, Q_LORA_RANK)
    (dp_cq,), (g_q_norm,), _, _ = _row_bwd(_norm_fn, toks=[(p_cq, 0, None, 0)], vecs=[q_norm], cots=[(dcqn, -nctx)],
                                           tok_grads=[MXU_DTYPE], nb=nb, nl=lt, tl=tl, name="q_norm_bwd")

    dkv_flat = dkv.reshape(nb * lt, -1)
    g_w_kv_up = _mm(ckvn.reshape(nb * lt, -1), dkv_flat, ta=True, name="wg_kv_up")
    dckvn = _mm(dkv_flat, full["w_kv_up"], tb=True, name="dg_kv_up").reshape(nb, lt, KV_LORA_RANK)
    (dp_ckv,), (g_kv_norm,), _, _ = _row_bwd(_norm_fn, toks=[(p_ckv, 0, None, 0)], vecs=[kv_norm], cots=[(dckvn, 0)],
                                             tok_grads=[MXU_DTYPE], nb=nb, nl=lt, tl=tl, name="kv_norm_bwd")

    g_krdt = jnp.concatenate([dkr[..., QK_NOPE_DIM:QK_DIM], g_dt, jnp.zeros((nb, lt, pad_w - 2 * N_SSD_HEADS), F32)],
                             axis=-1)
    (dp_krdt,), (_, g_dt_bias_row), _, _ = _row_bwd(
        _krdt_fn, toks=[(p_krdt, 0, None, 0)], poss=[cos_k, sin_k], vecs=[rot_k, dt_bias_row], cots=[(g_krdt, 0)],
        tok_grads=[MXU_DTYPE], nb=nb, nl=lt, tl=tl, name="krdt_bwd")

    dp_segs = [t.reshape(nb * lt, -1) for t in (dp_cq, dp_ckv, dz, dp_xbc, dp_krdt)]
    g_segs = [_mm(h1f, t, ta=True, name="wg_in_" + nm) for nm, t in zip(("cq", "ckv", "z", "xbc", "krdt"), dp_segs)]
    g_segs[3] = _ungroup_xbc(g_segs[3])
    g_w_in_p = jnp.concatenate(g_segs, axis=1)
    dh1 = _mm_sum(list(zip(dp_segs, w_segs)), tb=True, name="dg_in").reshape(nb, lt, d)
    g_w_in = _unperm_in_cols(g_w_in_p[:, :IN_WIDTH])
    last_sums = pair_sums({"w_in": g_w_in, "w_q_up": g_w_q_up, "w_kv_up": g_w_kv_up}, REDUCE_LAST, "last")

    def pre_res_fn(uv, w, shift, scale):
        return _pre_fn(uv, w, shift, scale) + (uv,)

    (grad_x,), (g_mix_pre,), ((g_shift0, g_shift0c), (g_scale1, g_scale1c)), (last_scattered,) = _row_bwd(
        pre_res_fn, toks=[(u, 0, None, 0)], vecs=[mix_pre_norm], bvecs=[mods[0], mods[1]],
        cots=[(dh1, 0), (dx_res, -nctx)], tok_grads=[F32], nb=nb, nl=lt, tl=tl, nctx=nctx, drop_blocks=nctx,
        name="pre1_bwd", ride=last_sums)

    zero_row = jnp.zeros((1, 1, d), F32)
    lat = [g_shift0, g_scale1, g_gate2, g_shift3, g_scale4, g_gate5]
    ctxg = [g_shift0c, g_scale1c, zero_row, zero_row, zero_row, zero_row]
    dmod = jnp.concatenate([jnp.concatenate([a, b], axis=0)[:, 0, :] for a, b in zip(lat, ctxg)], axis=-1)
    dmod = jnp.pad(dmod, ((0, mod_rows - nb - 1), (0, 0)))
    _, (g_b_mod,), _, _ = _row_bwd(_bias_fn, toks=[(mod_lin[None], 0, None, 0)], vecs=[b_mod], cots=[(dmod[None], 0)],
                                   tok_grads=[None], nb=1, nl=mod_rows, tl=mod_rows, name="mod_bias_bwd")
    dmod_all = _allgather_small(dmod[:8], name="gather_dmod")
    dmod_ctx = _sum_slots(dmod_all, out_dtype=F32, name="dmod_ctx_add")[nb:nb + 1]
    dmod_every = jnp.concatenate([dmod_all[:, :nb].reshape(n_ex, N_MOD * d), dmod_ctx,
                                  jnp.zeros((all_rows - n_ex - 1, N_MOD * d), F32)], axis=0)
    dmod_mine = lax.dynamic_slice_in_dim(dmod_every, chip * mod_cols, mod_cols, axis=1)
    g_w_mod = _mm(s_all[0], dmod_mine, ta=True, name="wg_mod")[None]
    ds_all = _mm(dmod_mine, w_mod_local, tb=True, name="dg_mod")
    (dc_all,), _, _, _ = _row_bwd(_silu_fn, toks=[(c_all, 0, None, 0)], cots=[(ds_all[None], 0)], tok_grads=[F32],
                                  nb=1, nl=all_rows, tl=all_rows, name="mod_silu_bwd")
    g_c_ctx = 0.5 * dc_all[0, n_ex]

    halves = [chip_total(early_sums, early_scattered, "early"), chip_total(last_sums, last_scattered, "last")]
    my_halves = jnp.concatenate(halves, axis=0)
    joined = lax.dynamic_update_slice(_join_halves(my_halves, name="grad_join"), my_halves[None], (core, 0, 0))
    g_shards, o = {}, 0
    for group, hv in zip((REDUCE_EARLY, REDUCE_LAST), halves):
        g_shards[group] = joined[:, o:o + hv.shape[0]].reshape(2 * hv.shape[0], PACK_COLS)
        o += hv.shape[0]

    g_d = jnp.sum(g_d_chan.reshape(N_SSD_HEADS, SSD_HEAD_DIM), axis=1)[None]
    g_dt_bias = g_dt_bias_row[:, QK_ROPE_DIM:QK_ROPE_DIM + 2 * N_SSD_HEADS].reshape(1, 2, N_SSD_HEADS)
    small_names = ["c_ctx", "b_mod", "mix_pre_norm", "mix_post_norm", "q_norm", "kv_norm", "ssd_conv_w", "ssd_conv_b",
                   "ssd_a_log", "ssd_dt_bias", "ssd_d", "ssd_norm", "ffn_pre_norm", "ffn_post_norm", "ffn_conv_w",
                   "ffn_conv_b"]
    small_grads = [g_c_ctx, g_b_mod, g_mix_pre, g_mix_post, g_q_norm, g_kv_norm, g_ssd_conv_w, g_ssd_conv_b,
                   g_a_log, g_dt_bias, g_d, g_ssd_norm, g_ffn_pre, g_ffn_post, g_ffn_conv_w, g_ffn_conv_b]
    small_shapes = [tuple(np.shape(a)) for a in small_grads] + [()]
    n_small = sum(int(np.prod(shp)) for shp in small_shapes)
    small_rows = -(-n_small // (8 * LANES)) * 8
    small_all = _allgather_small(_pack_small(small_grads + [loss_part], small_rows), name="gather_small")
    small_sum = _sum_slots(small_all, out_dtype=F32, name="small_add")
    small_red = _unpack_small(small_sum, small_shapes)
    loss = small_red[-1]
    grads = dict(zip(small_names, small_red[:-1]))
    grads["ssd_conv_w"] = lax.dynamic_slice_in_dim(grads["ssd_conv_w"], chip * n_sc, n_sc, axis=1)[None]
    grads["ffn_conv_w"] = lax.dynamic_slice_in_dim(grads["ffn_conv_w"], chip * n_fc, n_fc, axis=1)[None]
    for n in small_names:
        grads[n] = grads[n].reshape(args[n].shape)

    delta, new_m, new_v = {}, {}, {}
    grads["w_mod"] = g_w_mod
    for group, g_shard in g_shards.items():
        o = 0
        for n in group:
            lr, lc = big_local[n].shape
            nr = lr * lc // PACK_COLS
            grads[n] = g_shard[o:o + nr].reshape(1, lr, lc)
            o += nr
    for n in ["w_mod"] + [n for n, _, _, _ in BIG]:
        dl, nm, nv = _adamw(args[n][0], grads[n][0], args["m_" + n][0], args["v_" + n][0], name="adamw_" + n)
        delta[n], new_m[n], new_v[n] = dl[None], nm[None], nv[None]
    sm_shapes = [args[n].shape for n in small_names]
    n_sm = sum(int(np.prod(shp)) for shp in sm_shapes)
    sm_rows = -(-n_sm // (8 * LANES)) * 8
    packs = [_pack_small([src[n] for n in small_names], sm_rows)
             for src in (args, grads, {n: args["m_" + n] for n in small_names}, {n: args["v_" + n] for n in small_names})]
    for out_dict, packed_out in zip((delta, new_m, new_v), _adamw(*packs, name="adamw_small")):
        out_dict.update(zip(small_names, _unpack_small(packed_out, sm_shapes)))

    return (loss, grad_x, *[grads[n] for n in names], *[delta[n] for n in names], *[new_m[n] for n in names],
            *[new_v[n] for n in names])
```
